```python
import math
import jax, jax.numpy as jnp
from jax import lax
import numpy as np

D_MODEL = 1024
BATCH = 4
SEQ = 4096
DEPTH = 1
DEC_BATCH = 32
DEC_SEQ = 32
PAST_LEN = 2048

CHUNK = 64
HEAD_DIM = 64
N_HEADS_A = 16
N_KV_A = 2
GQA_REP = N_HEADS_A // N_KV_A
WINDOW = 128
WIN_CHUNKS = WINDOW // CHUNK
D_ATT = N_HEADS_A * HEAD_DIM
D_KV = N_KV_A * HEAD_DIM
NUM_BUCKETS = 32
MAX_DISTANCE = 128
GMLP_CHUNK = 128
D_GMLP = 1024
N_GROUPS_B = 4
GROUP_W = D_GMLP // N_GROUPS_B
N_EXPERTS = 32
TOP_K = 4
D_FF = 1024
SWIGLU_LIMIT = 7.0
SWIGLU_ALPHA = 1.702
MOE_BLOCK = 128
NORM_EPS = 1e-5
IN_WIDTHS = [D_ATT, D_KV, D_KV, D_GMLP, D_GMLP, D_MODEL, D_MODEL]
IN_SPLITS = [int(c) for c in np.cumsum(IN_WIDTHS)[:-1]]
D_IN = int(sum(IN_WIDTHS))

kernel_name = "hybrid_swa_sink_gmlp_moe_stream_step"

F32 = jnp.float32


def rms_norm(x, g):
    xf = x.astype(F32)
    y = xf * lax.rsqrt(jnp.mean(xf * xf, axis=-1, keepdims=True) + NORM_EPS)
    return (y * g.astype(F32)).astype(x.dtype)


def layer_norm(x, g, b):
    xf = x.astype(F32)
    mu = jnp.mean(xf, axis=-1, keepdims=True)
    var = jnp.mean(jnp.square(xf - mu), axis=-1, keepdims=True)
    return ((xf - mu) * lax.rsqrt(var + NORM_EPS) * g.astype(F32) + b.astype(F32)).astype(x.dtype)


def t5_bucket(rel):
    half = NUM_BUCKETS // 2
    max_exact = half // 2
    ret = jnp.where(rel > 0, half, 0)
    n = jnp.abs(rel)
    nf = jnp.maximum(n, 1).astype(F32)
    large = max_exact + (jnp.log(nf / max_exact) / math.log(MAX_DISTANCE / max_exact)
                         * (half - max_exact)).astype(jnp.int32)
    large = jnp.minimum(large, half - 1)
    return ret + jnp.where(n < max_exact, n, large)


def rel_bias(table, q_pos, k_pos):
    bucket = t5_bucket(k_pos[None, :] - q_pos[:, None])
    bias = table.astype(F32)[bucket]
    return jnp.transpose(bias, (2, 0, 1)).reshape(N_KV_A, GQA_REP, q_pos.shape[0], k_pos.shape[0])


def sink_softmax(logits, sinks, valid):
    logits = jnp.where(valid, logits, -jnp.inf)
    s = sinks.astype(F32)[:, :, None, None]
    m = jnp.maximum(jnp.max(logits, axis=-1, keepdims=True), s)
    p = jnp.exp(logits - m)
    return p / (jnp.sum(p, axis=-1, keepdims=True) + jnp.exp(s - m))


def window_attn_prompt(q, k, v, sinks, table):
    b, s = q.shape[:2]
    nc = s // CHUNK
    band = (WIN_CHUNKS + 1) * CHUNK
    qc = q.reshape(b, nc, CHUNK, N_KV_A, GQA_REP, HEAD_DIM)
    pad = ((0, 0), (WINDOW, 0), (0, 0), (0, 0))
    kp = jnp.pad(k, pad).reshape(b, nc + WIN_CHUNKS, CHUNK, N_KV_A, HEAD_DIM)
    vp = jnp.pad(v, pad).reshape(b, nc + WIN_CHUNKS, CHUNK, N_KV_A, HEAD_DIM)
    kb = jnp.concatenate([kp[:, i:i + nc] for i in range(WIN_CHUNKS + 1)], axis=2)
    vb = jnp.concatenate([vp[:, i:i + nc] for i in range(WIN_CHUNKS + 1)], axis=2)
    logits = jnp.einsum('bcqgrd,bckgd->bcgrqk', qc, kb, preferred_element_type=F32) * (HEAD_DIM ** -0.5)
    logits = logits + rel_bias(table, jnp.arange(CHUNK) + WINDOW, jnp.arange(band))
    k_pos = jnp.arange(nc)[:, None] * CHUNK - WINDOW + jnp.arange(band)[None, :]
    valid = (k_pos >= 0)[None, :, None, None, None, :]
    w = sink_softmax(logits, sinks, valid)
    o = jnp.einsum('bcgrqk,bckgd->bcqgrd', w.astype(v.dtype), vb)
    return o.reshape(b, s, D_ATT)


def window_attn_sample(q, k_new, v_new, cache_k, cache_v, sinks, table):
    b, t = q.shape[:2]
    L = cache_k.shape[1]
    kk = jnp.concatenate([cache_k, k_new], axis=1)
    vv = jnp.concatenate([cache_v, v_new], axis=1)
    qg = q.reshape(b, t, N_KV_A, GQA_REP, HEAD_DIM)
    logits = jnp.einsum('bqgrd,bkgd->bgrqk', qg, kk, preferred_element_type=F32) * (HEAD_DIM ** -0.5)
    logits = logits + rel_bias(table, L + jnp.arange(t), jnp.arange(L + t))
    w = sink_softmax(logits, sinks, True)
    o = jnp.einsum('bgrqk,bkgd->bqgrd', w.astype(vv.dtype), vv)
    return o.reshape(b, t, D_ATT)


def spatial_gating(u, vn, w_s, b_s):
    b, s, _ = vn.shape
    c = min(s, GMLP_CHUNK)
    n = s // c
    vc = vn.reshape(b, n, c, N_GROUPS_B, GROUP_W)
    mask = jnp.tril(jnp.ones((c, c), dtype=bool))
    w = jnp.where(mask[None], w_s[:, :c, :c], 0).astype(vc.dtype)
    mixed = jnp.einsum('gij,bcjgd->bcigd', w, vc) + b_s[:, :c].T[None, None, :, :, None]
    return u * mixed.reshape(b, s, D_GMLP)


def swiglu_expert(xb, w1, b1, w2, b2):
    h = xb @ w1 + b1
    gate = jnp.minimum(h[:, :D_FF], SWIGLU_LIMIT)
    up = jnp.clip(h[:, D_FF:], -SWIGLU_LIMIT, SWIGLU_LIMIT)
    act = gate * jax.nn.sigmoid(SWIGLU_ALPHA * gate) * (up + 1.0)
    return act @ w2 + b2


def moe(x, w_router, b_router, w1, b1, w2, b2):
    shape = x.shape
    xf = x.reshape(-1, D_MODEL)
    n = xf.shape[0]
    nk = n * TOP_K
    logits = (xf @ w_router + b_router).astype(F32)
    top_val, top_idx = lax.top_k(logits, TOP_K)
    gate = jax.nn.softmax(top_val, axis=-1)
    flat_e = top_idx.reshape(-1).astype(jnp.int32)
    flat_tok = jnp.arange(nk, dtype=jnp.int32) // TOP_K
    flat_w = gate.reshape(-1)
    order = jnp.argsort(flat_e)
    sorted_e = flat_e[order]
    counts = jnp.bincount(flat_e, length=N_EXPERTS)
    padded = (counts + MOE_BLOCK - 1) // MOE_BLOCK * MOE_BLOCK
    start = jnp.cumsum(counts) - counts
    pend = jnp.cumsum(padded)
    pstart = pend - padded
    dest = pstart[sorted_e] + (jnp.arange(nk, dtype=jnp.int32) - start[sorted_e])
    n_blocks = -(-(nk + N_EXPERTS * (MOE_BLOCK - 1)) // MOE_BLOCK)
    n_slots = n_blocks * MOE_BLOCK
    slot_tok = jnp.full((n_slots,), n, jnp.int32).at[dest].set(flat_tok[order])
    slot_w = jnp.zeros((n_slots,), F32).at[dest].set(flat_w[order])
    block_e = jnp.minimum(jnp.searchsorted(pend, jnp.arange(n_blocks) * MOE_BLOCK, side='right'),
                          N_EXPERTS - 1)
    x_pad = jnp.concatenate([xf, jnp.zeros((1, D_MODEL), xf.dtype)], axis=0)
    xs = x_pad[slot_tok].reshape(n_blocks, MOE_BLOCK, D_MODEL)

    def run_block(args):
        xb, e = args
        return swiglu_expert(xb, w1[e], b1[e], w2[e], b2[e])

    ys = lax.map(run_block, (xs, block_e)).reshape(n_slots, D_MODEL)
    ys = ys * slot_w[:, None].astype(ys.dtype)
    out = jax.ops.segment_sum(ys, slot_tok, num_segments=n + 1)[:n]
    return out.reshape(shape)


def project_in(x, norm_g, w_in):
    z = rms_norm(x, norm_g) @ w_in
    q, k, v, u, vg, ga, gb = jnp.split(z, IN_SPLITS, axis=-1)
    b, s = x.shape[:2]
    q = q.reshape(b, s, N_HEADS_A, HEAD_DIM)
    k = k.reshape(b, s, N_KV_A, HEAD_DIM)
    v = v.reshape(b, s, N_KV_A, HEAD_DIM)
    return q, k, v, u, vg, ga, gb


def finish_layer(x, o_att, u, vg, ga, gb, lp):
    u = jax.nn.gelu(u, approximate=False)
    vn = layer_norm(jax.nn.gelu(vg, approximate=False), lp['sgu_ln_g'], lp['sgu_ln_b'])
    o_sgu = spatial_gating(u, vn, lp['sgu_w'], lp['sgu_b'])
    merged = (jax.nn.sigmoid(ga) * (o_att @ lp['w_o_att'])
              + jax.nn.sigmoid(gb) * (o_sgu @ lp['w_o_sgu']))
    x = x + merged @ lp['w_out']
    x = x + moe(rms_norm(x, lp['norm2_g']), lp['w_router'], lp['b_router'],
                lp['w_exp_in'], lp['b_exp_in'], lp['w_exp_out'], lp['b_exp_out'])
    return x, vn


def setup_inputs(seed: int = 0) -> dict:
    key = jax.random.key(seed)
    ks = jax.random.split(key, 24)
    win_rows = min(WINDOW, PAST_LEN)
    nrm = jax.random.normal
    return {
        'x_prompt': nrm(ks[0], (BATCH, SEQ, D_MODEL), F32),
        'x_sample': nrm(ks[1], (DEC_BATCH, DEC_SEQ, D_MODEL), F32),
        'cache_win_k': nrm(ks[2], (DEPTH, DEC_BATCH, win_rows, N_KV_A, HEAD_DIM), F32),
        'cache_win_v': nrm(ks[3], (DEPTH, DEC_BATCH, win_rows, N_KV_A, HEAD_DIM), F32),
        'norm1_g': 1.0 + 0.02 * nrm(ks[4], (DEPTH, D_MODEL), F32),
        'w_in': nrm(ks[5], (DEPTH, D_MODEL, D_IN), F32) * D_MODEL ** -0.5,
        'attn_sinks': 0.5 * nrm(ks[6], (DEPTH, N_KV_A, GQA_REP), F32),
        'rel_bias_table': 0.5 * nrm(ks[7], (NUM_BUCKETS, N_HEADS_A), F32),
        'sgu_ln_g': 1.0 + 0.02 * nrm(ks[8], (DEPTH, D_GMLP), F32),
        'sgu_ln_b': 0.02 * nrm(ks[9], (DEPTH, D_GMLP), F32),
        'sgu_w': nrm(ks[10], (DEPTH, N_GROUPS_B, GMLP_CHUNK, GMLP_CHUNK), F32) * GMLP_CHUNK ** -0.5,
        'sgu_b': 1.0 + 0.02 * nrm(ks[11], (DEPTH, N_GROUPS_B, GMLP_CHUNK), F32),
        'w_o_att': nrm(ks[12], (DEPTH, D_ATT, D_MODEL), F32) * D_ATT ** -0.5,
        'w_o_sgu': nrm(ks[13], (DEPTH, D_GMLP, D_MODEL), F32) * D_GMLP ** -0.5,
        'w_out': nrm(ks[14], (DEPTH, D_MODEL, D_MODEL), F32) * D_MODEL ** -0.5,
        'norm2_g': 1.0 + 0.02 * nrm(ks[15], (DEPTH, D_MODEL), F32),
        'w_router': nrm(ks[16], (DEPTH, D_MODEL, N_EXPERTS), F32) * D_MODEL ** -0.5,
        'b_router': 0.01 * nrm(ks[17], (DEPTH, N_EXPERTS), F32),
        'w_exp_in': nrm(ks[18], (DEPTH, N_EXPERTS, D_MODEL, 2 * D_FF), F32) * D_MODEL ** -0.5,
        'b_exp_in': 0.01 * nrm(ks[19], (DEPTH, N_EXPERTS, 2 * D_FF), F32),
        'w_exp_out': nrm(ks[20], (DEPTH, N_EXPERTS, D_FF, D_MODEL), F32) * D_FF ** -0.5,
        'b_exp_out': 0.01 * nrm(ks[21], (DEPTH, N_EXPERTS, D_MODEL), F32),
        'final_norm_g': 1.0 + 0.02 * nrm(ks[22], (D_MODEL,), F32),
    }


def reference(x_prompt, x_sample, cache_win_k, cache_win_v, norm1_g, w_in, attn_sinks, rel_bias_table,
              sgu_ln_g, sgu_ln_b, sgu_w, sgu_b, w_o_att, w_o_sgu, w_out, norm2_g, w_router, b_router,
              w_exp_in, b_exp_in, w_exp_out, b_exp_out, final_norm_g):
    x_p, x_s = x_prompt, x_sample
    kp_rows, vp_rows, ks_rows, vs_rows, sgu_rows = [], [], [], [], []
    for l in range(DEPTH):
        lp = dict(sgu_ln_g=sgu_ln_g[l], sgu_ln_b=sgu_ln_b[l], sgu_w=sgu_w[l], sgu_b=sgu_b[l],
                  w_o_att=w_o_att[l], w_o_sgu=w_o_sgu[l], w_out=w_out[l], norm2_g=norm2_g[l],
                  w_router=w_router[l], b_router=b_router[l], w_exp_in=w_exp_in[l],
                  b_exp_in=b_exp_in[l], w_exp_out=w_exp_out[l], b_exp_out=b_exp_out[l])
        q, k, v, u, vg, ga, gb = project_in(x_p, norm1_g[l], w_in[l])
        o_att = window_attn_prompt(q, k, v, attn_sinks[l], rel_bias_table)
        kp_rows.append(k[:, -WINDOW:])
        vp_rows.append(v[:, -WINDOW:])
        x_p, _ = finish_layer(x_p, o_att, u, vg, ga, gb, lp)
        q, k, v, u, vg, ga, gb = project_in(x_s, norm1_g[l], w_in[l])
        o_att = window_attn_sample(q, k, v, cache_win_k[l], cache_win_v[l], attn_sinks[l], rel_bias_table)
        ks_rows.append(k)
        vs_rows.append(v)
        x_s, vn = finish_layer(x_s, o_att, u, vg, ga, gb, lp)
        sgu_rows.append(vn)
    y_prompt = rms_norm(x_p, final_norm_g)
    y_sample = rms_norm(x_s, final_norm_g)
    new_win_k_prompt = jnp.stack(kp_rows)
    new_win_v_prompt = jnp.stack(vp_rows)
    new_win_k_sample = jnp.stack(ks_rows)
    new_win_v_sample = jnp.stack(vs_rows)
    new_sgu_v_sample = jnp.stack(sgu_rows)
    return (y_prompt, y_sample, new_win_k_prompt, new_win_v_prompt, new_win_k_sample, new_win_v_sample, new_sgu_v_sample)
```

```python
import functools
import math

import numpy as np
import jax
import jax.numpy as jnp
from jax import lax
from jax.experimental import pallas as pl
from jax.experimental.pallas import tpu as pltpu

F32 = jnp.float32
BF16 = jnp.bfloat16
I32 = jnp.int32

D_MODEL = 1024
HEAD_DIM = 64
N_HEADS = 16
N_KV = 2
REP = N_HEADS // N_KV
CHUNK = 64
WINDOW = 128
BAND = WINDOW + CHUNK
D_ATT = N_HEADS * HEAD_DIM
D_KV = N_KV * HEAD_DIM
NUM_BUCKETS = 32
MAX_DISTANCE = 128
GMLP_CHUNK = 128
D_GMLP = 1024
N_GROUPS = 4
GROUP_W = D_GMLP // N_GROUPS
N_EXPERTS = 32
TOP_K = 4
D_FF = 1024
SWIGLU_LIMIT = 7.0
SWIGLU_ALPHA = 1.702
NORM_EPS = 1e-5
D_IN = D_ATT + 2 * D_KV + 2 * D_GMLP + 2 * D_MODEL
COL_Q = 0
COL_KV = D_ATT
COL_U = COL_KV + 2 * D_KV
COL_VG = COL_U + D_GMLP
COL_GA = COL_VG + D_GMLP
COL_GB = COL_GA + D_MODEL
SQRT_HALF = float(np.sqrt(0.5))

LANES = 128
BF16_SUBLANES = 16
VMEM_LIMIT = 56 * 1024 * 1024

TOK_TILE = 256
GRAN = BF16_SUBLANES
SEG_BITS = (TOK_TILE // GRAN).bit_length()
LOCAL_ROWS = TOP_K * TOK_TILE + N_EXPERTS * GRAN
ROW_BLOCK = 256
PAD_BITS = (ROW_BLOCK // GRAN - 1).bit_length()
ROUTE_ROWS = 16


def _const_spec(shape):
    nd = len(shape)
    return pl.BlockSpec(shape, lambda *_: (0,) * nd, pipeline_mode=pl.Buffered(1))


def _rms_norm(x, g):
    ms = jnp.mean(x * x, axis=-1, keepdims=True)
    return x * lax.rsqrt(ms + NORM_EPS) * g


def _gelu(x):
    return 0.5 * x * (1.0 + lax.erf(x * SQRT_HALF))


def _sink_softmax(logits, sink):
    m = jnp.maximum(jnp.max(logits, axis=-1, keepdims=True), sink)
    p = jnp.exp(logits - m)
    den = jnp.sum(p, axis=-1, keepdims=True) + jnp.exp(sink - m)
    return p / den


def _attend(q_heads, k, v, bias, sink, valid):
    q = jnp.concatenate(q_heads, axis=0)
    logits = lax.dot_general(q, k, (((1,), (1,)), ((), ())), preferred_element_type=F32)
    logits = logits * (HEAD_DIM ** -0.5) + bias
    if valid is not None:
        logits = jnp.where(valid, logits, -jnp.inf)
    w = _sink_softmax(logits, sink).astype(BF16)
    return jnp.dot(w, v, preferred_element_type=F32)


def _layer_tail(x, xn_b, oatt_ref, osgu_ref, w_in_ref, sguw_ref, sgub_ref, lng_ref, lnb_ref,
                woa_ref, wos_ref, wout_ref, g2_ref, wr_ref, br_ref, tri_ref, low_ref,
                x1_ref, h_ref, route_ref, routet_ref, cnt_ref, vn_ref):
    T = x.shape[0]
    u = _gelu(jnp.dot(xn_b, w_in_ref[:, COL_U:COL_VG], preferred_element_type=F32))
    vg = _gelu(jnp.dot(xn_b, w_in_ref[:, COL_VG:COL_GA], preferred_element_type=F32))
    mu = jnp.mean(vg, axis=-1, keepdims=True)
    var = jnp.mean(jnp.square(vg - mu), axis=-1, keepdims=True)
    vn = (vg - mu) * lax.rsqrt(var + NORM_EPS) * lng_ref[...] + lnb_ref[...]
    if vn_ref is not None:
        vn_ref[...] = vn
    vn_b = vn.astype(BF16)
    for j in range(T // GMLP_CHUNK):
        rows = slice(j * GMLP_CHUNK, (j + 1) * GMLP_CHUNK)
        for g in range(N_GROUPS):
            cols = slice(g * GROUP_W, (g + 1) * GROUP_W)
            mixed = jnp.dot(sguw_ref[g], vn_b[rows, cols], preferred_element_type=F32) + sgub_ref[g]
            osgu_ref[rows, cols] = (u[rows, cols] * mixed).astype(BF16)

    ga = jax.nn.sigmoid(jnp.dot(xn_b, w_in_ref[:, COL_GA:COL_GB], preferred_element_type=F32))
    merged = ga * jnp.dot(oatt_ref[...], woa_ref[...], preferred_element_type=F32)
    gb = jax.nn.sigmoid(jnp.dot(xn_b, w_in_ref[:, COL_GB:D_IN], preferred_element_type=F32))
    merged = merged + gb * jnp.dot(osgu_ref[...], wos_ref[...], preferred_element_type=F32)
    x1 = x + jnp.dot(merged.astype(BF16), wout_ref[...], preferred_element_type=F32)
    x1_ref[...] = x1

    h_b = _rms_norm(x1, g2_ref[...]).astype(BF16)
    h_ref[...] = h_b
    logits = jnp.dot(h_b, wr_ref[...], preferred_element_type=F32)
    lt = jnp.transpose(logits)[:N_EXPERTS, :] + br_ref[...]
    e_iota = lax.broadcasted_iota(I32, (N_EXPERTS, T), 0).astype(F32)
    cur = lt
    vals, idxs = [], []
    for _ in range(TOP_K):
        m = jnp.max(cur, axis=0, keepdims=True)
        ik = jnp.min(jnp.where(cur == m, e_iota, float(N_EXPERTS)), axis=0, keepdims=True)
        vals.append(m)
        idxs.append(ik)
        cur = jnp.where(e_iota == ik, -jnp.inf, cur)
    exps = [jnp.exp(v - vals[0]) for v in vals]
    den = exps[0] + exps[1] + exps[2] + exps[3]
    gates = [e / den for e in exps]

    onehot = jnp.zeros((N_EXPERTS, T), F32)
    for ik in idxs:
        onehot = onehot + jnp.where(e_iota == ik, 1.0, 0.0)
    rank = jnp.dot(onehot.astype(BF16), tri_ref[...], preferred_element_type=F32)
    cnt = jnp.sum(onehot, axis=1, keepdims=True)
    cnt_b = jnp.broadcast_to(cnt, (N_EXPERTS, LANES))
    gran = jnp.floor((cnt_b + (GRAN - 1)) * (1.0 / GRAN))
    off = jnp.dot(low_ref[...], gran.astype(BF16), preferred_element_type=F32) * GRAN
    base = off[:, 0:1] + rank
    poss = [jnp.sum(jnp.where(e_iota == ik, base, 0.0), axis=0, keepdims=True) for ik in idxs]

    rec = jnp.concatenate(idxs + poss + gates
                          + [jnp.zeros((ROUTE_ROWS - 3 * TOP_K, T), F32)], axis=0)
    route_ref[0] = rec
    rec_pad = jnp.concatenate([rec, jnp.zeros((LANES - ROUTE_ROWS, T), F32)], axis=0)
    routet_ref[...] = jnp.transpose(rec_pad)
    cnt_ref[0] = cnt_b


def _prompt_layer_kernel(x_ref, g1_ref, w_in_ref, bias_ref, sink_ref, sguw_ref, sgub_ref, lng_ref,
                         lnb_ref, woa_ref, wos_ref, wout_ref, g2_ref, wr_ref, br_ref, tri_ref, low_ref,
                         x1_ref, h_ref, route_ref, routet_ref, cnt_ref, kvwin_ref,
                         kvx_ref, oatt_ref, osgu_ref):
    T = TOK_TILE
    j = pl.program_id(1)
    x = x_ref[0]
    xn_b = _rms_norm(x, g1_ref[...]).astype(BF16)
    zq = jnp.dot(xn_b, w_in_ref[:, COL_Q:COL_KV], preferred_element_type=F32).astype(BF16)
    zkv = jnp.dot(xn_b, w_in_ref[:, COL_KV:COL_U], preferred_element_type=F32)

    @pl.when(j == pl.num_programs(1) - 1)
    def _():
        kvwin_ref[0] = zkv[T - WINDOW:, :]

    @pl.when(j == 0)
    def _():
        kvx_ref[0:WINDOW, :] = jnp.zeros((WINDOW, 2 * D_KV), BF16)

    @pl.when(j > 0)
    def _():
        kvx_ref[0:WINDOW, :] = kvx_ref[T:T + WINDOW, :]

    kvx_ref[WINDOW:, :] = zkv.astype(BF16)

    col = lax.broadcasted_iota(I32, (1, BAND), 1)
    for c in range(T // CHUNK):
        r0 = c * CHUNK
        kvb = kvx_ref[r0:r0 + BAND, :]
        valid = (col + (j * T + r0 - WINDOW)) >= 0
        for g in range(N_KV):
            qh = [zq[r0:r0 + CHUNK, (g * REP + r) * HEAD_DIM:(g * REP + r + 1) * HEAD_DIM]
                  for r in range(REP)]
            o = _attend(qh, kvb[:, g * HEAD_DIM:(g + 1) * HEAD_DIM],
                        kvb[:, D_KV + g * HEAD_DIM:D_KV + (g + 1) * HEAD_DIM],
                        bias_ref[g], sink_ref[g], valid)
            for r in range(REP):
                hcol = (g * REP + r) * HEAD_DIM
                oatt_ref[r0:r0 + CHUNK, hcol:hcol + HEAD_DIM] = o[r * CHUNK:(r + 1) * CHUNK].astype(BF16)

    _layer_tail(x, xn_b, oatt_ref, osgu_ref, w_in_ref, sguw_ref, sgub_ref, lng_ref, lnb_ref,
                woa_ref, wos_ref, wout_ref, g2_ref, wr_ref, br_ref, tri_ref, low_ref,
                x1_ref, h_ref, route_ref, routet_ref, cnt_ref, None)


def _sample_layer_kernel(x1_in, h_in, route_in, routet_in, cnt_in,
                         x_ref, ck_ref, cv_ref, g1_ref, w_in_ref, bias_ref, sink_ref, sguw_ref, sgub_ref,
                         lng_ref, lnb_ref, woa_ref, wos_ref, wout_ref, g2_ref, wr_ref, br_ref, tri_ref,
                         low_ref,
                         x1_ref, h_ref, route_ref, routet_ref, cnt_ref, kvnew_ref, vn_ref,
                         oatt_ref, osgu_ref):
    del x1_in, h_in, route_in, routet_in, cnt_in
    T = TOK_TILE
    nq = x_ref.shape[0] // ck_ref.shape[0]
    x = x_ref[...]
    xn_b = _rms_norm(x, g1_ref[...]).astype(BF16)
    zq = jnp.dot(xn_b, w_in_ref[:, COL_Q:COL_KV], preferred_element_type=F32).astype(BF16)
    zkv = jnp.dot(xn_b, w_in_ref[:, COL_KV:COL_U], preferred_element_type=F32)
    kvnew_ref[...] = zkv
    zkv_b = zkv.astype(BF16)
    for b in range(T // nq):
        r0 = b * nq
        kk = jnp.concatenate([ck_ref[b].astype(BF16), zkv_b[r0:r0 + nq, :D_KV]], axis=0)
        vv = jnp.concatenate([cv_ref[b].astype(BF16), zkv_b[r0:r0 + nq, D_KV:]], axis=0)
        for g in range(N_KV):
            qh = [zq[r0:r0 + nq, (g * REP + r) * HEAD_DIM:(g * REP + r + 1) * HEAD_DIM]
                  for r in range(REP)]
            o = _attend(qh, kk[:, g * HEAD_DIM:(g + 1) * HEAD_DIM], vv[:, g * HEAD_DIM:(g + 1) * HEAD_DIM],
                        bias_ref[g], sink_ref[g], None)
            for r in range(REP):
                hcol = (g * REP + r) * HEAD_DIM
                oatt_ref[r0:r0 + nq, hcol:hcol + HEAD_DIM] = o[r * nq:(r + 1) * nq].astype(BF16)

    _layer_tail(x, xn_b, oatt_ref, osgu_ref, w_in_ref, sguw_ref, sgub_ref, lng_ref, lnb_ref,
                woa_ref, wos_ref, wout_ref, g2_ref, wr_ref, br_ref, tri_ref, low_ref,
                x1_ref, h_ref, route_ref, routet_ref, cnt_ref, vn_ref)


def _segment_copies(n_gran, src_row, dst_row, bits, make_copy, act):
    for b in range(bits):
        rows = GRAN << b

        @pl.when(((n_gran >> b) & 1) == 1)
        def _(b=b, rows=rows):
            done = (n_gran & ((1 << b) - 1)) * GRAN
            s = pl.multiple_of(src_row + done, GRAN)
            d = pl.multiple_of(dst_row + done, GRAN)
            act(make_copy(s, d, rows))


def _dispatch_kernel(cnt_ref, loff_ref, goff_ref, padn_ref, padoff_ref,
                     route_ref, h_ref, xs_hbm, stage_ref, zero_ref, sem):
    i = pl.program_id(0)
    T = TOK_TILE
    pos = route_ref[0, TOP_K:2 * TOP_K, :].astype(I32)
    r_iota = lax.broadcasted_iota(I32, (LOCAL_ROWS, T), 0)
    p = jnp.zeros((LOCAL_ROWS, T), F32)
    for k in range(TOP_K):
        p = jnp.where(r_iota == pos[k:k + 1, :], 1.0, p)
    stage_ref[...] = jnp.dot(p.astype(BF16), h_ref[...], preferred_element_type=F32).astype(BF16)

    def seg_copy(s, d, rows):
        return pltpu.make_async_copy(stage_ref.at[pl.ds(s, rows)], xs_hbm.at[pl.ds(d, rows)], sem.at[0])

    def for_segments(act):
        def body(e, carry):
            t = i * N_EXPERTS + e
            _segment_copies(cnt_ref[t], loff_ref[t] * GRAN, goff_ref[t] * GRAN, SEG_BITS, seg_copy, act)
            return carry
        lax.fori_loop(0, N_EXPERTS, body, 0)

    for_segments(lambda cp: cp.start())
    for_segments(lambda cp: cp.wait())

    @pl.when(i == pl.num_programs(0) - 1)
    def _():
        zero_ref[...] = jnp.zeros(zero_ref.shape, BF16)

        def pad_copy(s, d, rows):
            return pltpu.make_async_copy(zero_ref.at[pl.ds(s, rows)], xs_hbm.at[pl.ds(d, rows)], sem.at[1])

        def for_pads(act):
            def body(e, carry):
                _segment_copies(padn_ref[e], 0, padoff_ref[e] * GRAN, PAD_BITS, pad_copy, act)
                return carry
            lax.fori_loop(0, N_EXPERTS, body, 0)

        for_pads(lambda cp: cp.start())
        for_pads(lambda cp: cp.wait())


def _expert_kernel(be_ref, nu_ref, xs_ref, w1_ref, b1_ref, w2_ref, b2_ref, ys_ref, w1b_ref, w2b_ref):
    i = pl.program_id(0)

    @pl.when(i < nu_ref[0])
    def _():
        new_expert = jnp.logical_or(i == 0, be_ref[i] != be_ref[jnp.maximum(i - 1, 0)])

        @pl.when(new_expert)
        def _():
            w1b_ref[...] = w1_ref[0].astype(BF16)
            w2b_ref[...] = w2_ref[0].astype(BF16)

        h1 = jnp.dot(xs_ref[...], w1b_ref[...], preferred_element_type=F32) + b1_ref[0]
        gate = jnp.minimum(h1[:, :D_FF], SWIGLU_LIMIT)
        up = jnp.clip(h1[:, D_FF:], -SWIGLU_LIMIT, SWIGLU_LIMIT)
        act = gate * jax.nn.sigmoid(SWIGLU_ALPHA * gate) * (up + 1.0)
        y = jnp.dot(act.astype(BF16), w2b_ref[...], preferred_element_type=F32) + b2_ref[0]
        ys_ref[...] = y.astype(BF16)


def _combine_kernel(n_prompt_tiles, cnt_ref, loff_ref, goff_ref,
                    routet_ref, x1_ref, gf_ref, ys_hbm, yp_ref, ysm_ref, stage_ref, sem):
    i = pl.program_id(0)
    T = TOK_TILE

    @pl.when(i == 0)
    def _():
        stage_ref[...] = jnp.zeros(stage_ref.shape, BF16)

    def seg_copy(s, d, rows):
        return pltpu.make_async_copy(ys_hbm.at[pl.ds(d, rows)], stage_ref.at[pl.ds(s, rows)], sem.at[0])

    def for_segments(act):
        def body(e, carry):
            t = i * N_EXPERTS + e
            _segment_copies(cnt_ref[t], loff_ref[t] * GRAN, goff_ref[t] * GRAN, SEG_BITS, seg_copy, act)
            return carry
        lax.fori_loop(0, N_EXPERTS, body, 0)

    for_segments(lambda cp: cp.start())

    rt = routet_ref[...]
    l_iota = lax.broadcasted_iota(I32, (T, LOCAL_ROWS), 1)
    pw = jnp.zeros((T, LOCAL_ROWS), F32)
    for k in range(TOP_K):
        pos_k = rt[:, TOP_K + k:TOP_K + k + 1].astype(I32)
        pw = jnp.where(l_iota == pos_k, rt[:, 2 * TOP_K + k:2 * TOP_K + k + 1], pw)

    for_segments(lambda cp: cp.wait())
    moe = jnp.dot(pw.astype(BF16), stage_ref[...], preferred_element_type=F32)
    y = _rms_norm(x1_ref[...] + moe, gf_ref[...])

    @pl.when(i < n_prompt_tiles)
    def _():
        yp_ref[...] = y

    @pl.when(i >= n_prompt_tiles)
    def _():
        ysm_ref[...] = y


def _t5_bucket(rel):
    half = NUM_BUCKETS // 2
    max_exact = half // 2
    ret = jnp.where(rel > 0, half, 0)
    n = jnp.abs(rel)
    nf = jnp.maximum(n, 1).astype(F32)
    large = max_exact + (jnp.log(nf / max_exact) / math.log(MAX_DISTANCE / max_exact)
                         * (half - max_exact)).astype(jnp.int32)
    large = jnp.minimum(large, half - 1)
    return ret + jnp.where(n < max_exact, n, large)


def _stacked_rel_bias(table, q_pos, k_pos):
    bucket = _t5_bucket(k_pos[None, :] - q_pos[:, None])
    bias = jnp.transpose(table.astype(F32)[bucket], (2, 0, 1))
    return bias.reshape(N_KV, REP * q_pos.shape[0], k_pos.shape[0])


def _stacked_sinks(sinks, nq):
    return jnp.repeat(sinks.astype(F32).reshape(N_KV, REP, 1), nq, axis=2).reshape(N_KV, REP * nq, 1)


def kernel(x_prompt, x_sample, cache_win_k, cache_win_v, norm1_g, w_in, attn_sinks, rel_bias_table, sgu_ln_g, sgu_ln_b, sgu_w, sgu_b, w_o_att, w_o_sgu, w_out, norm2_g, w_router, b_router, w_exp_in, b_exp_in, w_exp_out, b_exp_out, final_norm_g):
    batch, seq, _ = x_prompt.shape
    dec_batch, dec_seq, _ = x_sample.shape
    cache_rows = cache_win_k.shape[2]
    assert x_prompt.shape[2] == D_MODEL and w_in.shape == (1, D_MODEL, D_IN)
    assert seq % TOK_TILE == 0 and TOK_TILE % GMLP_CHUNK == 0 and TOK_TILE >= WINDOW
    assert TOK_TILE % dec_seq == 0 and (dec_batch * dec_seq) % TOK_TILE == 0
    assert dec_seq <= GMLP_CHUNK and GMLP_CHUNK % dec_seq == 0 and cache_rows == WINDOW
    T = TOK_TILE
    n_prompt = batch * seq
    n_sample = dec_batch * dec_seq
    n_tok = n_prompt + n_sample
    tiles_per_seq = seq // T
    n_ptiles = n_prompt // T
    n_stiles = n_sample // T
    n_tiles = n_ptiles + n_stiles
    seqs_per_tile = T // dec_seq

    w_in_b = w_in[0].astype(BF16)
    woa_b = w_o_att[0].astype(BF16)
    wos_b = w_o_sgu[0].astype(BF16)
    wout_b = w_out[0].astype(BF16)
    wr_b = jnp.pad(w_router[0], ((0, 0), (0, LANES - N_EXPERTS))).astype(BF16)
    br_col = b_router[0].astype(F32).reshape(N_EXPERTS, 1)
    g1 = norm1_g[0].reshape(1, D_MODEL)
    g2 = norm2_g[0].reshape(1, D_MODEL)
    gf = final_norm_g.reshape(1, D_MODEL)
    lng = sgu_ln_g[0].reshape(1, D_GMLP)
    lnb = sgu_ln_b[0].reshape(1, D_GMLP)
    tril = jnp.tril(jnp.ones((GMLP_CHUNK, GMLP_CHUNK), dtype=bool))
    sguw_p = jnp.where(tril[None], sgu_w[0], 0).astype(BF16)
    sgub_p = jnp.broadcast_to(sgu_b[0][:, :, None], (N_GROUPS, GMLP_CHUNK, GROUP_W)).astype(F32)
    reps = GMLP_CHUNK // dec_seq
    corner = jnp.where(tril[None, :dec_seq, :dec_seq], sgu_w[0][:, :dec_seq, :dec_seq], 0)
    sguw_s = jnp.einsum('ab,gij->gaibj', jnp.eye(reps, dtype=F32), corner).reshape(
        N_GROUPS, GMLP_CHUNK, GMLP_CHUNK).astype(BF16)
    sgub_s = jnp.broadcast_to(jnp.tile(sgu_b[0][:, :dec_seq], (1, reps))[:, :, None],
                              (N_GROUPS, GMLP_CHUNK, GROUP_W)).astype(F32)
    bias_p = _stacked_rel_bias(rel_bias_table, jnp.arange(CHUNK) + WINDOW, jnp.arange(BAND))
    bias_s = _stacked_rel_bias(rel_bias_table, cache_rows + jnp.arange(dec_seq),
                               jnp.arange(cache_rows + dec_seq))
    sink_p = _stacked_sinks(attn_sinks[0], CHUNK)
    sink_s = _stacked_sinks(attn_sinks[0], dec_seq)
    tri = jnp.triu(jnp.ones((T, T), F32), k=1).astype(BF16)
    low = jnp.tril(jnp.ones((N_EXPERTS, N_EXPERTS), F32), k=-1).astype(BF16)

    layer_out_shapes = (
        jax.ShapeDtypeStruct((n_tok, D_MODEL), F32),
        jax.ShapeDtypeStruct((n_tok, D_MODEL), BF16),
        jax.ShapeDtypeStruct((n_tiles, ROUTE_ROWS, T), F32),
        jax.ShapeDtypeStruct((n_tok, LANES), F32),
        jax.ShapeDtypeStruct((n_tiles, N_EXPERTS, LANES), F32),
    )
    shared_consts = (w_in_b,)
    tail_consts_p = (sguw_p, sgub_p, lng, lnb, woa_b, wos_b, wout_b, g2, wr_b, br_col, tri, low)
    tail_consts_s = (sguw_s, sgub_s, lng, lnb, woa_b, wos_b, wout_b, g2, wr_b, br_col, tri, low)

    def tile_specs(tile_of):
        return [
            pl.BlockSpec((T, D_MODEL), lambda *g: (tile_of(*g), 0)),
            pl.BlockSpec((T, D_MODEL), lambda *g: (tile_of(*g), 0)),
            pl.BlockSpec((1, ROUTE_ROWS, T), lambda *g: (tile_of(*g), 0, 0)),
            pl.BlockSpec((T, LANES), lambda *g: (tile_of(*g), 0)),
            pl.BlockSpec((1, N_EXPERTS, LANES), lambda *g: (tile_of(*g), 0, 0)),
        ]

    prompt_consts = (g1,) + shared_consts + (bias_p, sink_p) + tail_consts_p
    x1, h, route, routet, cnt, kvwin = pl.pallas_call(
        _prompt_layer_kernel,
        grid=(batch, tiles_per_seq),
        in_specs=[pl.BlockSpec((1, T, D_MODEL), lambda b, j: (b, j, 0))]
                 + [_const_spec(c.shape) for c in prompt_consts],
        out_specs=tile_specs(lambda b, j: b * tiles_per_seq + j)
                  + [pl.BlockSpec((1, WINDOW, 2 * D_KV), lambda b, j: (b, 0, 0))],
        out_shape=layer_out_shapes + (jax.ShapeDtypeStruct((batch, WINDOW, 2 * D_KV), F32),),
        scratch_shapes=[pltpu.VMEM((T + WINDOW, 2 * D_KV), BF16),
                        pltpu.VMEM((T, D_ATT), BF16),
                        pltpu.VMEM((T, D_GMLP), BF16)],
        compiler_params=pltpu.CompilerParams(dimension_semantics=("arbitrary", "arbitrary"),
                                             vmem_limit_bytes=VMEM_LIMIT),
        name="layer_prompt",
    )(x_prompt, *prompt_consts)

    xs_flat = x_sample.reshape(n_sample, D_MODEL)
    ck = cache_win_k[0].reshape(dec_batch, cache_rows, D_KV)
    cv = cache_win_v[0].reshape(dec_batch, cache_rows, D_KV)
    sample_consts = (g1,) + shared_consts + (bias_s, sink_s) + tail_consts_s
    any_spec = pl.BlockSpec(memory_space=pl.ANY)
    x1, h, route, routet, cnt, kvnew, vn_s = pl.pallas_call(
        _sample_layer_kernel,
        grid=(n_stiles,),
        in_specs=[any_spec] * 5
                 + [pl.BlockSpec((T, D_MODEL), lambda i: (i, 0)),
                    pl.BlockSpec((seqs_per_tile, cache_rows, D_KV), lambda i: (i, 0, 0)),
                    pl.BlockSpec((seqs_per_tile, cache_rows, D_KV), lambda i: (i, 0, 0))]
                 + [_const_spec(c.shape) for c in sample_consts],
        out_specs=tile_specs(lambda i: n_ptiles + i)
                  + [pl.BlockSpec((T, 2 * D_KV), lambda i: (i, 0)),
                     pl.BlockSpec((T, D_GMLP), lambda i: (i, 0))],
        out_shape=layer_out_shapes + (jax.ShapeDtypeStruct((n_sample, 2 * D_KV), F32),
                                      jax.ShapeDtypeStruct((n_sample, D_GMLP), F32)),
        scratch_shapes=[pltpu.VMEM((T, D_ATT), BF16), pltpu.VMEM((T, D_GMLP), BF16)],
        input_output_aliases={0: 0, 1: 1, 2: 2, 3: 3, 4: 4},
        compiler_params=pltpu.CompilerParams(dimension_semantics=("arbitrary",),
                                             vmem_limit_bytes=VMEM_LIMIT),
        name="layer_sample",
    )(x1, h, route, routet, cnt, xs_flat, ck, cv, *sample_consts)

    blk_gran = ROW_BLOCK // GRAN
    counts = cnt[:, :, 0].astype(I32)
    seg_gran = (counts + (GRAN - 1)) // GRAN
    local_off = jnp.cumsum(seg_gran, axis=1) - seg_gran
    tot_gran = jnp.sum(seg_gran, axis=0)
    ptot_gran = (tot_gran + (blk_gran - 1)) // blk_gran * blk_gran
    pend_gran = jnp.cumsum(ptot_gran)
    gstart = pend_gran - ptot_gran
    global_off = gstart[None, :] + jnp.cumsum(seg_gran, axis=0) - seg_gran
    pad_n = ptot_gran - tot_gran
    pad_off = gstart + tot_gran
    n_blocks = -(-(TOP_K * n_tok + n_tiles * N_EXPERTS * (GRAN - 1) + N_EXPERTS * (ROW_BLOCK - GRAN))
                 // ROW_BLOCK)
    n_rows = n_blocks * ROW_BLOCK
    pend_blk = pend_gran // blk_gran
    n_used = pend_blk[-1:].astype(I32)
    block_e = jnp.minimum(jnp.sum(pend_blk[None, :] <= jnp.arange(n_blocks, dtype=I32)[:, None], axis=1),
                          N_EXPERTS - 1).astype(I32)
    seg_gran_f = seg_gran.reshape(-1).astype(I32)
    local_off_f = local_off.reshape(-1).astype(I32)
    global_off_f = global_off.reshape(-1).astype(I32)

    xs = pl.pallas_call(
        _dispatch_kernel,
        grid_spec=pltpu.PrefetchScalarGridSpec(
            num_scalar_prefetch=5,
            grid=(n_tiles,),
            in_specs=[pl.BlockSpec((1, ROUTE_ROWS, T), lambda i, *_: (i, 0, 0)),
                      pl.BlockSpec((T, D_MODEL), lambda i, *_: (i, 0))],
            out_specs=pl.BlockSpec(memory_space=pl.ANY),
            scratch_shapes=[pltpu.VMEM((LOCAL_ROWS, D_MODEL), BF16),
                            pltpu.VMEM((ROW_BLOCK, D_MODEL), BF16),
                            pltpu.SemaphoreType.DMA((2,))]),
        out_shape=jax.ShapeDtypeStruct((n_rows, D_MODEL), BF16),
        compiler_params=pltpu.CompilerParams(dimension_semantics=("arbitrary",),
                                             vmem_limit_bytes=VMEM_LIMIT),
        name="moe_dispatch",
    )(seg_gran_f, local_off_f, global_off_f, pad_n.astype(I32), pad_off.astype(I32), route, h)

    def used_block(i, be, nu):
        return jnp.minimum(i, nu[0] - 1)

    ys = pl.pallas_call(
        _expert_kernel,
        grid_spec=pltpu.PrefetchScalarGridSpec(
            num_scalar_prefetch=2,
            grid=(n_blocks,),
            in_specs=[pl.BlockSpec((ROW_BLOCK, D_MODEL), lambda i, be, nu: (used_block(i, be, nu), 0)),
                      pl.BlockSpec((1, D_MODEL, 2 * D_FF), lambda i, be, nu: (be[i], 0, 0)),
                      pl.BlockSpec((1, 1, 2 * D_FF), lambda i, be, nu: (be[i], 0, 0)),
                      pl.BlockSpec((1, D_FF, D_MODEL), lambda i, be, nu: (be[i], 0, 0)),
                      pl.BlockSpec((1, 1, D_MODEL), lambda i, be, nu: (be[i], 0, 0))],
            out_specs=pl.BlockSpec((ROW_BLOCK, D_MODEL), lambda i, be, nu: (used_block(i, be, nu), 0)),
            scratch_shapes=[pltpu.VMEM((D_MODEL, 2 * D_FF), BF16), pltpu.VMEM((D_FF, D_MODEL), BF16)]),
        out_shape=jax.ShapeDtypeStruct((n_rows, D_MODEL), BF16),
        compiler_params=pltpu.CompilerParams(dimension_semantics=("arbitrary",),
                                             vmem_limit_bytes=VMEM_LIMIT),
        name="moe_experts",
    )(block_e, n_used, xs, w_exp_in[0], b_exp_in[0].reshape(N_EXPERTS, 1, 2 * D_FF),
      w_exp_out[0], b_exp_out[0].reshape(N_EXPERTS, 1, D_MODEL))

    y_p, y_s = pl.pallas_call(
        functools.partial(_combine_kernel, n_ptiles),
        grid_spec=pltpu.PrefetchScalarGridSpec(
            num_scalar_prefetch=3,
            grid=(n_tiles,),
            in_specs=[pl.BlockSpec((T, LANES), lambda i, *_: (i, 0)),
                      pl.BlockSpec((T, D_MODEL), lambda i, *_: (i, 0)),
                      pl.BlockSpec((1, D_MODEL), lambda i, *_: (0, 0)),
                      pl.BlockSpec(memory_space=pl.ANY)],
            out_specs=[pl.BlockSpec((T, D_MODEL), lambda i, *_: (jnp.minimum(i, n_ptiles - 1), 0)),
                       pl.BlockSpec((T, D_MODEL), lambda i, *_: (jnp.maximum(i - n_ptiles, 0), 0))],
            scratch_shapes=[pltpu.VMEM((LOCAL_ROWS, D_MODEL), BF16), pltpu.SemaphoreType.DMA((1,))]),
        out_shape=(jax.ShapeDtypeStruct((n_prompt, D_MODEL), F32),
                   jax.ShapeDtypeStruct((n_sample, D_MODEL), F32)),
        compiler_params=pltpu.CompilerParams(dimension_semantics=("arbitrary",),
                                             vmem_limit_bytes=VMEM_LIMIT),
        name="moe_combine",
    )(seg_gran_f, local_off_f, global_off_f, routet, x1, gf, ys)

    y_prompt = y_p.reshape(batch, seq, D_MODEL)
    y_sample = y_s.reshape(dec_batch, dec_seq, D_MODEL)
    new_win_k_prompt = kvwin[:, :, :D_KV].reshape(1, batch, WINDOW, N_KV, HEAD_DIM)
    new_win_v_prompt = kvwin[:, :, D_KV:].reshape(1, batch, WINDOW, N_KV, HEAD_DIM)
    new_win_k_sample = kvnew[:, :D_KV].reshape(1, dec_batch, dec_seq, N_KV, HEAD_DIM)
    new_win_v_sample = kvnew[:, D_KV:].reshape(1, dec_batch, dec_seq, N_KV, HEAD_DIM)
    new_sgu_v_sample = vn_s.reshape(1, dec_batch, dec_seq, D_GMLP)
    return (y_prompt, y_sample, new_win_k_prompt, new_win_v_prompt, new_win_k_sample,
            new_win_v_sample, new_sgu_v_sample)
```

```python
import functools
import math

import numpy as np
import jax
import jax.numpy as jnp
from jax import lax
from jax.experimental import pallas as pl
from jax.experimental.pallas import tpu as pltpu

F32 = jnp.float32
BF16 = jnp.bfloat16
I32 = jnp.int32

D_MODEL = 1024
HEAD_DIM = 64
N_HEADS = 16
N_KV = 2
REP = N_HEADS // N_KV
CHUNK = 64
WINDOW = 128
BAND = WINDOW + CHUNK
KEY_PAD = 256
D_ATT = N_HEADS * HEAD_DIM
D_KV = N_KV * HEAD_DIM
NUM_BUCKETS = 32
MAX_DISTANCE = 128
GMLP_CHUNK = 128
D_GMLP = 1024
N_GROUPS = 4
GROUP_W = D_GMLP // N_GROUPS
N_EXPERTS = 32
TOP_K = 4
D_FF = 1024
SWIGLU_LIMIT = 7.0
SWIGLU_ALPHA = 1.702
NORM_EPS = 1e-5
D_IN = D_ATT + 2 * D_KV + 2 * D_GMLP + 2 * D_MODEL
COL_Q = 0
COL_KV = D_ATT
COL_U = COL_KV + 2 * D_KV
COL_VG = COL_U + D_GMLP
COL_GA = COL_VG + D_GMLP
COL_GB = COL_GA + D_MODEL
SQRT_HALF = float(np.sqrt(0.5))

LANES = 128
BF16_SUBLANES = 16
VMEM_LIMIT = 56 * 1024 * 1024

TOK_TILE = 256
GRAN = BF16_SUBLANES
SEG_BITS = (TOK_TILE // GRAN).bit_length()
LOCAL_ROWS = TOP_K * TOK_TILE + N_EXPERTS * GRAN
ROW_BLOCK = 256
PAD_BITS = (ROW_BLOCK // GRAN - 1).bit_length()
ROUTE_ROWS = 16


def _const_spec(shape):
    nd = len(shape)
    return pl.BlockSpec(shape, lambda *_: (0,) * nd, pipeline_mode=pl.Buffered(1))


def _rms_norm(x, g):
    ms = jnp.mean(x * x, axis=-1, keepdims=True)
    return x * lax.rsqrt(ms + NORM_EPS) * g


def _gelu(x):
    return 0.5 * x * (1.0 + lax.erf(x * SQRT_HALF))


def _attend(q, k, v, bias, valid):
    logits = lax.dot_general(q, k, (((1,), (1,)), ((), ())), preferred_element_type=F32) + bias
    if valid is not None:
        logits = jnp.where(valid, logits, -jnp.inf)
    m = jnp.max(logits, axis=-1, keepdims=True)
    p = jnp.exp(logits - m)
    den = jnp.sum(p, axis=-1, keepdims=True)
    w = (p * (1.0 / den)).astype(BF16)
    return jnp.dot(w, v, preferred_element_type=F32)


def _attention_rows(zq, row0, n_rows, k_ext, v_ext, bias_ref, valid, oatt_ref, heads_per_unit):
    lane = lax.broadcasted_iota(I32, (1, LANES), 1)
    for g in range(N_KV):
        in_group = (lane >= g * HEAD_DIM) & (lane < (g + 1) * HEAD_DIM)
        qmask = jnp.where(in_group, HEAD_DIM ** -0.5, 0.0).astype(BF16)
        for part in range(REP // heads_per_unit):
            tiles = range(part * heads_per_unit, (part + 1) * heads_per_unit)
            q = jnp.concatenate([zq[row0:row0 + n_rows, t * LANES:(t + 1) * LANES] * qmask for t in tiles],
                                axis=0)
            b0 = part * heads_per_unit * n_rows
            o = _attend(q, k_ext, v_ext, bias_ref[g, b0:b0 + heads_per_unit * n_rows, :], valid)
            for u, t in enumerate(tiles):
                c0 = t * LANES + g * HEAD_DIM
                oatt_ref[row0:row0 + n_rows, c0:c0 + HEAD_DIM] = (
                    o[u * n_rows:(u + 1) * n_rows, g * HEAD_DIM:(g + 1) * HEAD_DIM].astype(BF16))


def _layer_tail(x, xn_b, oatt_ref, osgu_ref, w_in_ref, sguw_ref, sgub_ref, lng_ref, lnb_ref,
                woa_ref, wos_ref, wout_ref, g2_ref, wr_ref, br_ref, tri_ref, low_ref,
                x1_ref, h_ref, route_ref, routet_ref, cnt_ref, vn_ref):
    T = x.shape[0]
    u = _gelu(jnp.dot(xn_b, w_in_ref[:, COL_U:COL_VG], preferred_element_type=F32))
    vg = _gelu(jnp.dot(xn_b, w_in_ref[:, COL_VG:COL_GA], preferred_element_type=F32))
    mu = jnp.mean(vg, axis=-1, keepdims=True)
    var = jnp.mean(jnp.square(vg - mu), axis=-1, keepdims=True)
    vn = (vg - mu) * lax.rsqrt(var + NORM_EPS) * lng_ref[...] + lnb_ref[...]
    if vn_ref is not None:
        vn_ref[...] = vn
    vn_b = vn.astype(BF16)
    for j in range(T // GMLP_CHUNK):
        rows = slice(j * GMLP_CHUNK, (j + 1) * GMLP_CHUNK)
        for g in range(N_GROUPS):
            cols = slice(g * GROUP_W, (g + 1) * GROUP_W)
            mixed = jnp.dot(sguw_ref[g], vn_b[rows, cols], preferred_element_type=F32) + sgub_ref[g]
            osgu_ref[rows, cols] = (u[rows, cols] * mixed).astype(BF16)

    ga = jax.nn.sigmoid(jnp.dot(xn_b, w_in_ref[:, COL_GA:COL_GB], preferred_element_type=F32))
    merged = ga * jnp.dot(oatt_ref[...], woa_ref[...], preferred_element_type=F32)
    gb = jax.nn.sigmoid(jnp.dot(xn_b, w_in_ref[:, COL_GB:D_IN], preferred_element_type=F32))
    merged = merged + gb * jnp.dot(osgu_ref[...], wos_ref[...], preferred_element_type=F32)
    x1 = x + jnp.dot(merged.astype(BF16), wout_ref[...], preferred_element_type=F32)
    x1_ref[...] = x1

    h_b = _rms_norm(x1, g2_ref[...]).astype(BF16)
    h_ref[...] = h_b
    logits = jnp.dot(h_b, wr_ref[...], preferred_element_type=F32)
    lt = jnp.transpose(logits)[:N_EXPERTS, :] + br_ref[...]
    e_iota = lax.broadcasted_iota(I32, (N_EXPERTS, T), 0).astype(F32)
    cur = lt
    vals, idxs = [], []
    for _ in range(TOP_K):
        m = jnp.max(cur, axis=0, keepdims=True)
        ik = jnp.min(jnp.where(cur == m, e_iota, float(N_EXPERTS)), axis=0, keepdims=True)
        vals.append(m)
        idxs.append(ik)
        cur = jnp.where(e_iota == ik, -jnp.inf, cur)
    exps = [jnp.exp(v - vals[0]) for v in vals]
    den = exps[0] + exps[1] + exps[2] + exps[3]
    gates = [e / den for e in exps]

    onehot = jnp.zeros((N_EXPERTS, T), F32)
    for ik in idxs:
        onehot = onehot + jnp.where(e_iota == ik, 1.0, 0.0)
    rank = jnp.dot(onehot.astype(BF16), tri_ref[...], preferred_element_type=F32)
    cnt = jnp.sum(onehot, axis=1, keepdims=True)
    cnt_b = jnp.broadcast_to(cnt, (N_EXPERTS, LANES))
    gran = jnp.floor((cnt_b + (GRAN - 1)) * (1.0 / GRAN))
    off = jnp.dot(low_ref[...], gran.astype(BF16), preferred_element_type=F32) * GRAN
    base = off[:, 0:1] + rank
    poss = [jnp.sum(jnp.where(e_iota == ik, base, 0.0), axis=0, keepdims=True) for ik in idxs]

    rec = jnp.concatenate(idxs + poss + gates
                          + [jnp.zeros((ROUTE_ROWS - 3 * TOP_K, T), F32)], axis=0)
    route_ref[0] = rec
    rec_pad = jnp.concatenate([rec, jnp.zeros((LANES - ROUTE_ROWS, T), F32)], axis=0)
    routet_ref[...] = jnp.transpose(rec_pad)
    cnt_ref[0] = cnt_b


def _prompt_layer_kernel(x_ref, g1_ref, w_in_ref, bias_ref, sguw_ref, sgub_ref, lng_ref,
                         lnb_ref, woa_ref, wos_ref, wout_ref, g2_ref, wr_ref, br_ref, tri_ref, low_ref,
                         x1_ref, h_ref, route_ref, routet_ref, cnt_ref, kvwin_ref,
                         kvx_ref, oatt_ref, osgu_ref):
    T = TOK_TILE
    j = pl.program_id(1)
    x = x_ref[0]
    xn_b = _rms_norm(x, g1_ref[...]).astype(BF16)
    zq = jnp.dot(xn_b, w_in_ref[:, COL_Q:COL_KV], preferred_element_type=F32).astype(BF16)
    zkv = jnp.dot(xn_b, w_in_ref[:, COL_KV:COL_U], preferred_element_type=F32)

    @pl.when(j == pl.num_programs(1) - 1)
    def _():
        kvwin_ref[0] = zkv[T - WINDOW:, :]

    @pl.when(j == 0)
    def _():
        kvx_ref[0:WINDOW, :] = jnp.zeros((WINDOW, 2 * D_KV), BF16)

    @pl.when(j > 0)
    def _():
        kvx_ref[0:WINDOW, :] = kvx_ref[T:T + WINDOW, :]

    kvx_ref[WINDOW:, :] = zkv.astype(BF16)

    col = lax.broadcasted_iota(I32, (1, KEY_PAD), 1)
    key_pad = jnp.zeros((KEY_PAD - BAND, 2 * D_KV), BF16)
    for c in range(T // CHUNK):
        r0 = c * CHUNK
        kvb = jnp.concatenate([kvx_ref[r0:r0 + BAND, :], key_pad], axis=0)
        valid = ((col + (j * T + r0 - WINDOW)) >= 0) | (col >= BAND) if r0 < WINDOW else None
        _attention_rows(zq, r0, CHUNK, kvb[:, :D_KV], kvb[:, D_KV:], bias_ref, valid, oatt_ref,
                        heads_per_unit=4)

    _layer_tail(x, xn_b, oatt_ref, osgu_ref, w_in_ref, sguw_ref, sgub_ref, lng_ref, lnb_ref,
                woa_ref, wos_ref, wout_ref, g2_ref, wr_ref, br_ref, tri_ref, low_ref,
                x1_ref, h_ref, route_ref, routet_ref, cnt_ref, None)


def _sample_layer_kernel(x1_in, h_in, route_in, routet_in, cnt_in,
                         x_ref, ck_ref, cv_ref, g1_ref, w_in_ref, bias_ref, sguw_ref, sgub_ref,
                         lng_ref, lnb_ref, woa_ref, wos_ref, wout_ref, g2_ref, wr_ref, br_ref, tri_ref,
                         low_ref,
                         x1_ref, h_ref, route_ref, routet_ref, cnt_ref, kvnew_ref, vn_ref,
                         oatt_ref, osgu_ref):
    del x1_in, h_in, route_in, routet_in, cnt_in
    T = TOK_TILE
    nq = x_ref.shape[0] // ck_ref.shape[0]
    x = x_ref[...]
    xn_b = _rms_norm(x, g1_ref[...]).astype(BF16)
    zq = jnp.dot(xn_b, w_in_ref[:, COL_Q:COL_KV], preferred_element_type=F32).astype(BF16)
    zkv = jnp.dot(xn_b, w_in_ref[:, COL_KV:COL_U], preferred_element_type=F32)
    kvnew_ref[...] = zkv
    zkv_b = zkv.astype(BF16)
    n_cache = ck_ref.shape[1]
    key_pad = jnp.zeros((KEY_PAD - n_cache - nq, D_KV), BF16)
    for b in range(T // nq):
        r0 = b * nq
        kk = jnp.concatenate([ck_ref[b].astype(BF16), zkv_b[r0:r0 + nq, :D_KV], key_pad], axis=0)
        vv = jnp.concatenate([cv_ref[b].astype(BF16), zkv_b[r0:r0 + nq, D_KV:], key_pad], axis=0)
        _attention_rows(zq, r0, nq, kk, vv, bias_ref, None, oatt_ref, heads_per_unit=REP)

    _layer_tail(x, xn_b, oatt_ref, osgu_ref, w_in_ref, sguw_ref, sgub_ref, lng_ref, lnb_ref,
                woa_ref, wos_ref, wout_ref, g2_ref, wr_ref, br_ref, tri_ref, low_ref,
                x1_ref, h_ref, route_ref, routet_ref, cnt_ref, vn_ref)


def _segment_copies(n_gran, src_row, dst_row, bits, make_copy, act):
    for b in range(bits):
        rows = GRAN << b

        @pl.when(((n_gran >> b) & 1) == 1)
        def _(b=b, rows=rows):
            done = (n_gran & ((1 << b) - 1)) * GRAN
            s = pl.multiple_of(src_row + done, GRAN)
            d = pl.multiple_of(dst_row + done, GRAN)
            act(make_copy(s, d, rows))


def _dispatch_kernel(cnt_ref, loff_ref, goff_ref, padn_ref, padoff_ref,
                     route_ref, h_ref, xs_hbm, stage_ref, zero_ref, sem):
    i = pl.program_id(0)
    T = TOK_TILE
    pos = route_ref[0, TOP_K:2 * TOP_K, :].astype(I32)
    r_iota = lax.broadcasted_iota(I32, (LOCAL_ROWS, T), 0)
    p = jnp.zeros((LOCAL_ROWS, T), F32)
    for k in range(TOP_K):
        p = jnp.where(r_iota == pos[k:k + 1, :], 1.0, p)
    stage_ref[...] = jnp.dot(p.astype(BF16), h_ref[...], preferred_element_type=F32).astype(BF16)

    def seg_copy(s, d, rows):
        return pltpu.make_async_copy(stage_ref.at[pl.ds(s, rows)], xs_hbm.at[pl.ds(d, rows)], sem.at[0])

    def for_segments(act):
        def body(e, carry):
            t = i * N_EXPERTS + e
            _segment_copies(cnt_ref[t], loff_ref[t] * GRAN, goff_ref[t] * GRAN, SEG_BITS, seg_copy, act)
            return carry
        lax.fori_loop(0, N_EXPERTS, body, 0)

    for_segments(lambda cp: cp.start())
    for_segments(lambda cp: cp.wait())

    @pl.when(i == pl.num_programs(0) - 1)
    def _():
        zero_ref[...] = jnp.zeros(zero_ref.shape, BF16)

        def pad_copy(s, d, rows):
            return pltpu.make_async_copy(zero_ref.at[pl.ds(s, rows)], xs_hbm.at[pl.ds(d, rows)], sem.at[1])

        def for_pads(act):
            def body(e, carry):
                _segment_copies(padn_ref[e], 0, padoff_ref[e] * GRAN, PAD_BITS, pad_copy, act)
                return carry
            lax.fori_loop(0, N_EXPERTS, body, 0)

        for_pads(lambda cp: cp.start())
        for_pads(lambda cp: cp.wait())


def _expert_kernel(be_ref, nu_ref, xs_ref, w1_ref, b1_ref, w2_ref, b2_ref, ys_ref, w1b_ref, w2b_ref):
    i = pl.program_id(0)

    @pl.when(i < nu_ref[0])
    def _():
        new_expert = jnp.logical_or(i == 0, be_ref[i] != be_ref[jnp.maximum(i - 1, 0)])

        @pl.when(new_expert)
        def _():
            w1b_ref[...] = w1_ref[0].astype(BF16)
            w2b_ref[...] = w2_ref[0].astype(BF16)

        h1 = jnp.dot(xs_ref[...], w1b_ref[...], preferred_element_type=F32) + b1_ref[0]
        gate = jnp.minimum(h1[:, :D_FF], SWIGLU_LIMIT)
        up = jnp.clip(h1[:, D_FF:], -SWIGLU_LIMIT, SWIGLU_LIMIT)
        act = gate * jax.nn.sigmoid(SWIGLU_ALPHA * gate) * (up + 1.0)
        y = jnp.dot(act.astype(BF16), w2b_ref[...], preferred_element_type=F32) + b2_ref[0]
        ys_ref[...] = y.astype(BF16)


def _combine_kernel(n_prompt_tiles, cnt_ref, loff_ref, goff_ref,
                    routet_ref, x1_ref, gf_ref, ys_hbm, yp_ref, ysm_ref, stage_ref, sem):
    i = pl.program_id(0)
    T = TOK_TILE

    @pl.when(i == 0)
    def _():
        stage_ref[...] = jnp.zeros(stage_ref.shape, BF16)

    def seg_copy(s, d, rows):
        return pltpu.make_async_copy(ys_hbm.at[pl.ds(d, rows)], stage_ref.at[pl.ds(s, rows)], sem.at[0])

    def for_segments(act):
        def body(e, carry):
            t = i * N_EXPERTS + e
            _segment_copies(cnt_ref[t], loff_ref[t] * GRAN, goff_ref[t] * GRAN, SEG_BITS, seg_copy, act)
            return carry
        lax.fori_loop(0, N_EXPERTS, body, 0)

    for_segments(lambda cp: cp.start())

    rt = routet_ref[...]
    l_iota = lax.broadcasted_iota(I32, (T, LOCAL_ROWS), 1)
    pw = jnp.zeros((T, LOCAL_ROWS), F32)
    for k in range(TOP_K):
        pos_k = rt[:, TOP_K + k:TOP_K + k + 1].astype(I32)
        pw = jnp.where(l_iota == pos_k, rt[:, 2 * TOP_K + k:2 * TOP_K + k + 1], pw)

    for_segments(lambda cp: cp.wait())
    moe = jnp.dot(pw.astype(BF16), stage_ref[...], preferred_element_type=F32)
    y = _rms_norm(x1_ref[...] + moe, gf_ref[...])

    @pl.when(i < n_prompt_tiles)
    def _():
        yp_ref[...] = y

    @pl.when(i >= n_prompt_tiles)
    def _():
        ysm_ref[...] = y


def _t5_bucket(rel):
    half = NUM_BUCKETS // 2
    max_exact = half // 2
    ret = jnp.where(rel > 0, half, 0)
    n = jnp.abs(rel)
    nf = jnp.maximum(n, 1).astype(F32)
    large = max_exact + (jnp.log(nf / max_exact) / math.log(MAX_DISTANCE / max_exact)
                         * (half - max_exact)).astype(jnp.int32)
    large = jnp.minimum(large, half - 1)
    return ret + jnp.where(n < max_exact, n, large)


def _stacked_bias(table, sinks, q_pos, k_pos):
    nq, nk = q_pos.shape[0], k_pos.shape[0]
    bucket = _t5_bucket(k_pos[None, :] - q_pos[:, None])[None]
    tab = table.astype(F32)
    bias = jnp.zeros((N_HEADS, nq, nk), F32)
    for b in range(NUM_BUCKETS):
        bias = jnp.where(bucket == b, tab[b][:, None, None], bias)
    bias = bias.reshape(N_KV, REP * nq, nk)
    sink = jnp.repeat(sinks.astype(F32).reshape(N_KV, REP, 1), nq, axis=2).reshape(N_KV, REP * nq, 1)
    tail = jnp.full((N_KV, REP * nq, KEY_PAD - nk - 1), -jnp.inf, F32)
    return jnp.concatenate([bias, sink, tail], axis=-1)


def kernel(x_prompt, x_sample, cache_win_k, cache_win_v, norm1_g, w_in, attn_sinks, rel_bias_table, sgu_ln_g, sgu_ln_b, sgu_w, sgu_b, w_o_att, w_o_sgu, w_out, norm2_g, w_router, b_router, w_exp_in, b_exp_in, w_exp_out, b_exp_out, final_norm_g):
    batch, seq, _ = x_prompt.shape
    dec_batch, dec_seq, _ = x_sample.shape
    cache_rows = cache_win_k.shape[2]
    assert x_prompt.shape[2] == D_MODEL and w_in.shape == (1, D_MODEL, D_IN)
    assert seq % TOK_TILE == 0 and TOK_TILE % GMLP_CHUNK == 0 and TOK_TILE >= WINDOW
    assert TOK_TILE % dec_seq == 0 and (dec_batch * dec_seq) % TOK_TILE == 0
    assert dec_seq <= GMLP_CHUNK and GMLP_CHUNK % dec_seq == 0 and cache_rows == WINDOW
    T = TOK_TILE
    n_prompt = batch * seq
    n_sample = dec_batch * dec_seq
    n_tok = n_prompt + n_sample
    tiles_per_seq = seq // T
    n_ptiles = n_prompt // T
    n_stiles = n_sample // T
    n_tiles = n_ptiles + n_stiles
    seqs_per_tile = T // dec_seq

    w_q = w_in[0][:, :D_ATT].reshape(D_MODEL, N_KV, REP, HEAD_DIM).transpose(0, 2, 1, 3).reshape(D_MODEL, D_ATT)
    w_in_b = jnp.concatenate([w_q, w_in[0][:, D_ATT:]], axis=1).astype(BF16)
    woa_b = w_o_att[0].reshape(N_KV, REP, HEAD_DIM, D_MODEL).transpose(1, 0, 2, 3).reshape(
        D_ATT, D_MODEL).astype(BF16)
    wos_b = w_o_sgu[0].astype(BF16)
    wout_b = w_out[0].astype(BF16)
    wr_b = jnp.pad(w_router[0], ((0, 0), (0, LANES - N_EXPERTS))).astype(BF16)
    br_col = b_router[0].astype(F32).reshape(N_EXPERTS, 1)
    g1 = norm1_g[0].reshape(1, D_MODEL)
    g2 = norm2_g[0].reshape(1, D_MODEL)
    gf = final_norm_g.reshape(1, D_MODEL)
    lng = sgu_ln_g[0].reshape(1, D_GMLP)
    lnb = sgu_ln_b[0].reshape(1, D_GMLP)
    tril = jnp.tril(jnp.ones((GMLP_CHUNK, GMLP_CHUNK), dtype=bool))
    sguw_p = jnp.where(tril[None], sgu_w[0], 0).astype(BF16)
    sgub_p = jnp.broadcast_to(sgu_b[0][:, :, None], (N_GROUPS, GMLP_CHUNK, GROUP_W)).astype(F32)
    reps = GMLP_CHUNK // dec_seq
    corner = jnp.where(tril[None, :dec_seq, :dec_seq], sgu_w[0][:, :dec_seq, :dec_seq], 0)
    sguw_s = jnp.einsum('ab,gij->gaibj', jnp.eye(reps, dtype=F32), corner).reshape(
        N_GROUPS, GMLP_CHUNK, GMLP_CHUNK).astype(BF16)
    sgub_s = jnp.broadcast_to(jnp.tile(sgu_b[0][:, :dec_seq], (1, reps))[:, :, None],
                              (N_GROUPS, GMLP_CHUNK, GROUP_W)).astype(F32)
    bias_p = _stacked_bias(rel_bias_table, attn_sinks[0], jnp.arange(CHUNK) + WINDOW, jnp.arange(BAND))
    bias_s = _stacked_bias(rel_bias_table, attn_sinks[0], cache_rows + jnp.arange(dec_seq),
                           jnp.arange(cache_rows + dec_seq))
    tri = jnp.triu(jnp.ones((T, T), F32), k=1).astype(BF16)
    low = jnp.tril(jnp.ones((N_EXPERTS, N_EXPERTS), F32), k=-1).astype(BF16)

    layer_out_shapes = (
        jax.ShapeDtypeStruct((n_tok, D_MODEL), F32),
        jax.ShapeDtypeStruct((n_tok, D_MODEL), BF16),
        jax.ShapeDtypeStruct((n_tiles, ROUTE_ROWS, T), F32),
        jax.ShapeDtypeStruct((n_tok, LANES), F32),
        jax.ShapeDtypeStruct((n_tiles, N_EXPERTS, LANES), F32),
    )
    shared_consts = (w_in_b,)
    tail_consts_p = (sguw_p, sgub_p, lng, lnb, woa_b, wos_b, wout_b, g2, wr_b, br_col, tri, low)
    tail_consts_s = (sguw_s, sgub_s, lng, lnb, woa_b, wos_b, wout_b, g2, wr_b, br_col, tri, low)

    def tile_specs(tile_of):
        return [
            pl.BlockSpec((T, D_MODEL), lambda *g: (tile_of(*g), 0)),
            pl.BlockSpec((T, D_MODEL), lambda *g: (tile_of(*g), 0)),
            pl.BlockSpec((1, ROUTE_ROWS, T), lambda *g: (tile_of(*g), 0, 0)),
            pl.BlockSpec((T, LANES), lambda *g: (tile_of(*g), 0)),
            pl.BlockSpec((1, N_EXPERTS, LANES), lambda *g: (tile_of(*g), 0, 0)),
        ]

    prompt_consts = (g1,) + shared_consts + (bias_p,) + tail_consts_p
    x1, h, route, routet, cnt, kvwin = pl.pallas_call(
        _prompt_layer_kernel,
        grid=(batch, tiles_per_seq),
        in_specs=[pl.BlockSpec((1, T, D_MODEL), lambda b, j: (b, j, 0))]
                 + [_const_spec(c.shape) for c in prompt_consts],
        out_specs=tile_specs(lambda b, j: b * tiles_per_seq + j)
                  + [pl.BlockSpec((1, WINDOW, 2 * D_KV), lambda b, j: (b, 0, 0))],
        out_shape=layer_out_shapes + (jax.ShapeDtypeStruct((batch, WINDOW, 2 * D_KV), F32),),
        scratch_shapes=[pltpu.VMEM((T + WINDOW, 2 * D_KV), BF16),
                        pltpu.VMEM((T, D_ATT), BF16),
                        pltpu.VMEM((T, D_GMLP), BF16)],
        compiler_params=pltpu.CompilerParams(dimension_semantics=("arbitrary", "arbitrary"),
                                             vmem_limit_bytes=VMEM_LIMIT),
        name="layer_prompt",
    )(x_prompt, *prompt_consts)

    xs_flat = x_sample.reshape(n_sample, D_MODEL)
    ck = cache_win_k[0].reshape(dec_batch, cache_rows, D_KV)
    cv = cache_win_v[0].reshape(dec_batch, cache_rows, D_KV)
    sample_consts = (g1,) + shared_consts + (bias_s,) + tail_consts_s
    any_spec = pl.BlockSpec(memory_space=pl.ANY)
    x1, h, route, routet, cnt, kvnew, vn_s = pl.pallas_call(
        _sample_layer_kernel,
        grid=(n_stiles,),
        in_specs=[any_spec] * 5
                 + [pl.BlockSpec((T, D_MODEL), lambda i: (i, 0)),
                    pl.BlockSpec((seqs_per_tile, cache_rows, D_KV), lambda i: (i, 0, 0)),
                    pl.BlockSpec((seqs_per_tile, cache_rows, D_KV), lambda i: (i, 0, 0))]
                 + [_const_spec(c.shape) for c in sample_consts],
        out_specs=tile_specs(lambda i: n_ptiles + i)
                  + [pl.BlockSpec((T, 2 * D_KV), lambda i: (i, 0)),
                     pl.BlockSpec((T, D_GMLP), lambda i: (i, 0))],
        out_shape=layer_out_shapes + (jax.ShapeDtypeStruct((n_sample, 2 * D_KV), F32),
                                      jax.ShapeDtypeStruct((n_sample, D_GMLP), F32)),
        scratch_shapes=[pltpu.VMEM((T, D_ATT), BF16), pltpu.VMEM((T, D_GMLP), BF16)],
        input_output_aliases={0: 0, 1: 1, 2: 2, 3: 3, 4: 4},
        compiler_params=pltpu.CompilerParams(dimension_semantics=("arbitrary",),
                                             vmem_limit_bytes=VMEM_LIMIT),
        name="layer_sample",
    )(x1, h, route, routet, cnt, xs_flat, ck, cv, *sample_consts)

    blk_gran = ROW_BLOCK // GRAN
    counts = cnt[:, :, 0].astype(I32)
    seg_gran = (counts + (GRAN - 1)) // GRAN
    local_off = jnp.cumsum(seg_gran, axis=1) - seg_gran
    tot_gran = jnp.sum(seg_gran, axis=0)
    ptot_gran = (tot_gran + (blk_gran - 1)) // blk_gran * blk_gran
    pend_gran = jnp.cumsum(ptot_gran)
    gstart = pend_gran - ptot_gran
    global_off = gstart[None, :] + jnp.cumsum(seg_gran, axis=0) - seg_gran
    pad_n = ptot_gran - tot_gran
    pad_off = gstart + tot_gran
    n_blocks = -(-(TOP_K * n_tok + n_tiles * N_EXPERTS * (GRAN - 1) + N_EXPERTS * (ROW_BLOCK - GRAN))
                 // ROW_BLOCK)
    n_rows = n_blocks * ROW_BLOCK
    pend_blk = pend_gran // blk_gran
    n_used = pend_blk[-1:].astype(I32)
    block_e = jnp.minimum(jnp.sum(pend_blk[None, :] <= jnp.arange(n_blocks, dtype=I32)[:, None], axis=1),
                          N_EXPERTS - 1).astype(I32)
    seg_gran_f = seg_gran.reshape(-1).astype(I32)
    local_off_f = local_off.reshape(-1).astype(I32)
    global_off_f = global_off.reshape(-1).astype(I32)

    xs = pl.pallas_call(
        _dispatch_kernel,
        grid_spec=pltpu.PrefetchScalarGridSpec(
            num_scalar_prefetch=5,
            grid=(n_tiles,),
            in_specs=[pl.BlockSpec((1, ROUTE_ROWS, T), lambda i, *_: (i, 0, 0)),
                      pl.BlockSpec((T, D_MODEL), lambda i, *_: (i, 0))],
            out_specs=pl.BlockSpec(memory_space=pl.ANY),
            scratch_shapes=[pltpu.VMEM((LOCAL_ROWS, D_MODEL), BF16),
                            pltpu.VMEM((ROW_BLOCK, D_MODEL), BF16),
                            pltpu.SemaphoreType.DMA((2,))]),
        out_shape=jax.ShapeDtypeStruct((n_rows, D_MODEL), BF16),
        compiler_params=pltpu.CompilerParams(dimension_semantics=("arbitrary",),
                                             vmem_limit_bytes=VMEM_LIMIT),
        name="moe_dispatch",
    )(seg_gran_f, local_off_f, global_off_f, pad_n.astype(I32), pad_off.astype(I32), route, h)

    def used_block(i, be, nu):
        return jnp.minimum(i, nu[0] - 1)

    ys = pl.pallas_call(
        _expert_kernel,
        grid_spec=pltpu.PrefetchScalarGridSpec(
            num_scalar_prefetch=2,
            grid=(n_blocks,),
            in_specs=[pl.BlockSpec((ROW_BLOCK, D_MODEL), lambda i, be, nu: (used_block(i, be, nu), 0)),
                      pl.BlockSpec((1, D_MODEL, 2 * D_FF), lambda i, be, nu: (be[i], 0, 0)),
                      pl.BlockSpec((1, 1, 2 * D_FF), lambda i, be, nu: (be[i], 0, 0)),
                      pl.BlockSpec((1, D_FF, D_MODEL), lambda i, be, nu: (be[i], 0, 0)),
                      pl.BlockSpec((1, 1, D_MODEL), lambda i, be, nu: (be[i], 0, 0))],
            out_specs=pl.BlockSpec((ROW_BLOCK, D_MODEL), lambda i, be, nu: (used_block(i, be, nu), 0)),
            scratch_shapes=[pltpu.VMEM((D_MODEL, 2 * D_FF), BF16), pltpu.VMEM((D_FF, D_MODEL), BF16)]),
        out_shape=jax.ShapeDtypeStruct((n_rows, D_MODEL), BF16),
        compiler_params=pltpu.CompilerParams(dimension_semantics=("arbitrary",),
                                             vmem_limit_bytes=VMEM_LIMIT),
        name="moe_experts",
    )(block_e, n_used, xs, w_exp_in[0], b_exp_in[0].reshape(N_EXPERTS, 1, 2 * D_FF),
      w_exp_out[0], b_exp_out[0].reshape(N_EXPERTS, 1, D_MODEL))

    y_p, y_s = pl.pallas_call(
        functools.partial(_combine_kernel, n_ptiles),
        grid_spec=pltpu.PrefetchScalarGridSpec(
            num_scalar_prefetch=3,
            grid=(n_tiles,),
            in_specs=[pl.BlockSpec((T, LANES), lambda i, *_: (i, 0)),
                      pl.BlockSpec((T, D_MODEL), lambda i, *_: (i, 0)),
                      pl.BlockSpec((1, D_MODEL), lambda i, *_: (0, 0)),
                      pl.BlockSpec(memory_space=pl.ANY)],
            out_specs=[pl.BlockSpec((T, D_MODEL), lambda i, *_: (jnp.minimum(i, n_ptiles - 1), 0)),
                       pl.BlockSpec((T, D_MODEL), lambda i, *_: (jnp.maximum(i - n_ptiles, 0), 0))],
            scratch_shapes=[pltpu.VMEM((LOCAL_ROWS, D_MODEL), BF16), pltpu.SemaphoreType.DMA((1,))]),
        out_shape=(jax.ShapeDtypeStruct((n_prompt, D_MODEL), F32),
                   jax.ShapeDtypeStruct((n_sample, D_MODEL), F32)),
        compiler_params=pltpu.CompilerParams(dimension_semantics=("arbitrary",),
                                             vmem_limit_bytes=VMEM_LIMIT),
        name="moe_combine",
    )(seg_gran_f, local_off_f, global_off_f, routet, x1, gf, ys)

    y_prompt = y_p.reshape(batch, seq, D_MODEL)
    y_sample = y_s.reshape(dec_batch, dec_seq, D_MODEL)
    new_win_k_prompt = kvwin[:, :, :D_KV].reshape(1, batch, WINDOW, N_KV, HEAD_DIM)
    new_win_v_prompt = kvwin[:, :, D_KV:].reshape(1, batch, WINDOW, N_KV, HEAD_DIM)
    new_win_k_sample = kvnew[:, :D_KV].reshape(1, dec_batch, dec_seq, N_KV, HEAD_DIM)
    new_win_v_sample = kvnew[:, D_KV:].reshape(1, dec_batch, dec_seq, N_KV, HEAD_DIM)
    new_sgu_v_sample = vn_s.reshape(1, dec_batch, dec_seq, D_GMLP)
    return (y_prompt, y_sample, new_win_k_prompt, new_win_v_prompt, new_win_k_sample,
            new_win_v_sample, new_sgu_v_sample)
```

```python
import functools
import math

import numpy as np
import jax
import jax.numpy as jnp
from jax import lax
from jax.experimental import pallas as pl
from jax.experimental.pallas import tpu as pltpu

F32 = jnp.float32
BF16 = jnp.bfloat16
I32 = jnp.int32

D_MODEL = 1024
HEAD_DIM = 64
N_HEADS = 16
N_KV = 2
REP = N_HEADS // N_KV
CHUNK = 64
WINDOW = 128
BAND = WINDOW + CHUNK
KEY_PAD = 256
D_ATT = N_HEADS * HEAD_DIM
D_KV = N_KV * HEAD_DIM
NUM_BUCKETS = 32
MAX_DISTANCE = 128
GMLP_CHUNK = 128
D_GMLP = 1024
N_GROUPS = 4
GROUP_W = D_GMLP // N_GROUPS
N_EXPERTS = 32
TOP_K = 4
D_FF = 1024
SWIGLU_LIMIT = 7.0
SWIGLU_ALPHA = 1.702
NORM_EPS = 1e-5
D_IN = D_ATT + 2 * D_KV + 2 * D_GMLP + 2 * D_MODEL
COL_Q = 0
COL_KV = D_ATT
COL_U = COL_KV + 2 * D_KV
COL_VG = COL_U + D_GMLP
COL_GA = COL_VG + D_GMLP
COL_GB = COL_GA + D_MODEL
SQRT_HALF = float(np.sqrt(0.5))

LANES = 128
BF16_SUBLANES = 16
VMEM_LIMIT = 56 * 1024 * 1024

TOK_TILE = 256
GRAN = BF16_SUBLANES
SEG_BITS = (TOK_TILE // GRAN).bit_length()
LOCAL_ROWS = TOP_K * TOK_TILE + N_EXPERTS * GRAN
ROW_BLOCK = 256
PAD_BITS = (ROW_BLOCK // GRAN - 1).bit_length()
TILE_BITS = (LOCAL_ROWS // GRAN).bit_length()
ROUTE_ROWS = 16


def _const_spec(shape):
    nd = len(shape)
    return pl.BlockSpec(shape, lambda *_: (0,) * nd, pipeline_mode=pl.Buffered(1))


def _rms_norm(x, g):
    ms = jnp.mean(x * x, axis=-1, keepdims=True)
    return x * lax.rsqrt(ms + NORM_EPS) * g


def _gelu(x):
    return 0.5 * x * (1.0 + lax.erf(x * SQRT_HALF))


def _attend(q, k, v, bias, valid):
    logits = lax.dot_general(q, k, (((1,), (1,)), ((), ())), preferred_element_type=F32) + bias
    if valid is not None:
        logits = jnp.where(valid, logits, -jnp.inf)
    m = jnp.max(logits, axis=-1, keepdims=True)
    p = jnp.exp(logits - m)
    den = jnp.sum(p, axis=-1, keepdims=True)
    w = (p * (1.0 / den)).astype(BF16)
    return jnp.dot(w, v, preferred_element_type=F32)


def _attention_rows(zq, row0, n_rows, k_ext, v_ext, bias_ref, valid, oatt_ref, heads_per_unit):
    lane = lax.broadcasted_iota(I32, (1, LANES), 1)
    for g in range(N_KV):
        in_group = (lane >= g * HEAD_DIM) & (lane < (g + 1) * HEAD_DIM)
        qmask = jnp.where(in_group, HEAD_DIM ** -0.5, 0.0).astype(BF16)
        for part in range(REP // heads_per_unit):
            tiles = range(part * heads_per_unit, (part + 1) * heads_per_unit)
            q = jnp.concatenate([zq[row0:row0 + n_rows, t * LANES:(t + 1) * LANES] * qmask for t in tiles],
                                axis=0)
            b0 = part * heads_per_unit * n_rows
            o = _attend(q, k_ext, v_ext, bias_ref[g, b0:b0 + heads_per_unit * n_rows, :], valid)
            for u, t in enumerate(tiles):
                c0 = t * LANES + g * HEAD_DIM
                oatt_ref[row0:row0 + n_rows, c0:c0 + HEAD_DIM] = (
                    o[u * n_rows:(u + 1) * n_rows, g * HEAD_DIM:(g + 1) * HEAD_DIM].astype(BF16))


def _layer_tail(x, xn_b, oatt_ref, osgu_ref, w_in_ref, sguw_ref, sgub_ref, lng_ref, lnb_ref,
                woa_ref, wos_ref, wout_ref, g2_ref, wr_ref, br_ref, tri_ref, low_ref,
                x1_ref, h_ref, route_ref, routet_ref, cnt_ref, vn_ref):
    T = x.shape[0]
    u = _gelu(jnp.dot(xn_b, w_in_ref[:, COL_U:COL_VG], preferred_element_type=F32))
    vg = _gelu(jnp.dot(xn_b, w_in_ref[:, COL_VG:COL_GA], preferred_element_type=F32))
    mu = jnp.mean(vg, axis=-1, keepdims=True)
    var = jnp.mean(jnp.square(vg - mu), axis=-1, keepdims=True)
    vn = (vg - mu) * lax.rsqrt(var + NORM_EPS) * lng_ref[...] + lnb_ref[...]
    if vn_ref is not None:
        vn_ref[...] = vn
    vn_b = vn.astype(BF16)
    for j in range(T // GMLP_CHUNK):
        rows = slice(j * GMLP_CHUNK, (j + 1) * GMLP_CHUNK)
        for g in range(N_GROUPS):
            cols = slice(g * GROUP_W, (g + 1) * GROUP_W)
            mixed = jnp.dot(sguw_ref[g], vn_b[rows, cols], preferred_element_type=F32) + sgub_ref[g]
            osgu_ref[rows, cols] = (u[rows, cols] * mixed).astype(BF16)

    ga = jax.nn.sigmoid(jnp.dot(xn_b, w_in_ref[:, COL_GA:COL_GB], preferred_element_type=F32))
    merged = ga * jnp.dot(oatt_ref[...], woa_ref[...], preferred_element_type=F32)
    gb = jax.nn.sigmoid(jnp.dot(xn_b, w_in_ref[:, COL_GB:D_IN], preferred_element_type=F32))
    merged = merged + gb * jnp.dot(osgu_ref[...], wos_ref[...], preferred_element_type=F32)
    x1 = x + jnp.dot(merged.astype(BF16), wout_ref[...], preferred_element_type=F32)
    x1_ref[...] = x1

    h_b = _rms_norm(x1, g2_ref[...]).astype(BF16)
    h_ref[...] = h_b
    logits = jnp.dot(h_b, wr_ref[...], preferred_element_type=F32)
    lt = jnp.transpose(logits)[:N_EXPERTS, :] + br_ref[...]
    e_iota = lax.broadcasted_iota(I32, (N_EXPERTS, T), 0).astype(F32)
    cur = lt
    vals, idxs = [], []
    for _ in range(TOP_K):
        m = jnp.max(cur, axis=0, keepdims=True)
        ik = jnp.min(jnp.where(cur == m, e_iota, float(N_EXPERTS)), axis=0, keepdims=True)
        vals.append(m)
        idxs.append(ik)
        cur = jnp.where(e_iota == ik, -jnp.inf, cur)
    exps = [jnp.exp(v - vals[0]) for v in vals]
    den = exps[0] + exps[1] + exps[2] + exps[3]
    gates = [e / den for e in exps]

    onehot = jnp.zeros((N_EXPERTS, T), F32)
    for ik in idxs:
        onehot = onehot + jnp.where(e_iota == ik, 1.0, 0.0)
    rank = jnp.dot(onehot.astype(BF16), tri_ref[...], preferred_element_type=F32)
    cnt = jnp.sum(onehot, axis=1, keepdims=True)
    cnt_b = jnp.broadcast_to(cnt, (N_EXPERTS, LANES))
    gran = jnp.floor((cnt_b + (GRAN - 1)) * (1.0 / GRAN))
    off = jnp.dot(low_ref[...], gran.astype(BF16), preferred_element_type=F32) * GRAN
    base = off[:, 0:1] + rank
    poss = [jnp.sum(jnp.where(e_iota == ik, base, 0.0), axis=0, keepdims=True) for ik in idxs]

    rec = jnp.concatenate(idxs + poss + gates
                          + [jnp.zeros((ROUTE_ROWS - 3 * TOP_K, T), F32)], axis=0)
    route_ref[0] = rec
    rec_pad = jnp.concatenate([rec, jnp.zeros((LANES - ROUTE_ROWS, T), F32)], axis=0)
    routet_ref[...] = jnp.transpose(rec_pad)
    cnt_ref[0] = cnt_b


def _prompt_layer_kernel(x_ref, g1_ref, w_in_ref, bias_ref, sguw_ref, sgub_ref, lng_ref,
                         lnb_ref, woa_ref, wos_ref, wout_ref, g2_ref, wr_ref, br_ref, tri_ref, low_ref,
                         x1_ref, h_ref, route_ref, routet_ref, cnt_ref, kvwin_ref,
                         kvx_ref, oatt_ref, osgu_ref):
    T = TOK_TILE
    j = pl.program_id(1)
    x = x_ref[0]
    xn_b = _rms_norm(x, g1_ref[...]).astype(BF16)
    zq = jnp.dot(xn_b, w_in_ref[:, COL_Q:COL_KV], preferred_element_type=F32).astype(BF16)
    zkv = jnp.dot(xn_b, w_in_ref[:, COL_KV:COL_U], preferred_element_type=F32)

    @pl.when(j == pl.num_programs(1) - 1)
    def _():
        kvwin_ref[0] = zkv[T - WINDOW:, :]

    @pl.when(j == 0)
    def _():
        kvx_ref[0:WINDOW, :] = jnp.zeros((WINDOW, 2 * D_KV), BF16)

    @pl.when(j > 0)
    def _():
        kvx_ref[0:WINDOW, :] = kvx_ref[T:T + WINDOW, :]

    kvx_ref[WINDOW:, :] = zkv.astype(BF16)

    col = lax.broadcasted_iota(I32, (1, KEY_PAD), 1)
    key_pad = jnp.zeros((KEY_PAD - BAND, 2 * D_KV), BF16)
    for c in range(T // CHUNK):
        r0 = c * CHUNK
        kvb = jnp.concatenate([kvx_ref[r0:r0 + BAND, :], key_pad], axis=0)
        valid = ((col + (j * T + r0 - WINDOW)) >= 0) | (col >= BAND) if r0 < WINDOW else None
        _attention_rows(zq, r0, CHUNK, kvb[:, :D_KV], kvb[:, D_KV:], bias_ref, valid, oatt_ref,
                        heads_per_unit=4)

    _layer_tail(x, xn_b, oatt_ref, osgu_ref, w_in_ref, sguw_ref, sgub_ref, lng_ref, lnb_ref,
                woa_ref, wos_ref, wout_ref, g2_ref, wr_ref, br_ref, tri_ref, low_ref,
                x1_ref, h_ref, route_ref, routet_ref, cnt_ref, None)


def _sample_layer_kernel(x1_in, h_in, route_in, routet_in, cnt_in,
                         x_ref, ck_ref, cv_ref, g1_ref, w_in_ref, bias_ref, sguw_ref, sgub_ref,
                         lng_ref, lnb_ref, woa_ref, wos_ref, wout_ref, g2_ref, wr_ref, br_ref, tri_ref,
                         low_ref,
                         x1_ref, h_ref, route_ref, routet_ref, cnt_ref, kvnew_ref, vn_ref,
                         oatt_ref, osgu_ref):
    del x1_in, h_in, route_in, routet_in, cnt_in
    T = TOK_TILE
    nq = x_ref.shape[0] // ck_ref.shape[0]
    x = x_ref[...]
    xn_b = _rms_norm(x, g1_ref[...]).astype(BF16)
    zq = jnp.dot(xn_b, w_in_ref[:, COL_Q:COL_KV], preferred_element_type=F32).astype(BF16)
    zkv = jnp.dot(xn_b, w_in_ref[:, COL_KV:COL_U], preferred_element_type=F32)
    kvnew_ref[...] = zkv
    zkv_b = zkv.astype(BF16)
    n_cache = ck_ref.shape[1]
    key_pad = jnp.zeros((KEY_PAD - n_cache - nq, D_KV), BF16)
    for b in range(T // nq):
        r0 = b * nq
        kk = jnp.concatenate([ck_ref[b].astype(BF16), zkv_b[r0:r0 + nq, :D_KV], key_pad], axis=0)
        vv = jnp.concatenate([cv_ref[b].astype(BF16), zkv_b[r0:r0 + nq, D_KV:], key_pad], axis=0)
        _attention_rows(zq, r0, nq, kk, vv, bias_ref, None, oatt_ref, heads_per_unit=REP)

    _layer_tail(x, xn_b, oatt_ref, osgu_ref, w_in_ref, sguw_ref, sgub_ref, lng_ref, lnb_ref,
                woa_ref, wos_ref, wout_ref, g2_ref, wr_ref, br_ref, tri_ref, low_ref,
                x1_ref, h_ref, route_ref, routet_ref, cnt_ref, vn_ref)


def _segment_copies(n_gran, src_row, dst_row, bits, make_copy, act):
    for b in range(bits):
        rows = GRAN << b

        @pl.when(((n_gran >> b) & 1) == 1)
        def _(b=b, rows=rows):
            done = (n_gran & ((1 << b) - 1)) * GRAN
            s = pl.multiple_of(src_row + done, GRAN)
            d = pl.multiple_of(dst_row + done, GRAN)
            act(make_copy(s, d, rows))


def _wait_granules(n_gran, bits, make_copy):
    for b in range(bits):
        @pl.when(((n_gran >> b) & 1) == 1)
        def _(b=b):
            make_copy(GRAN << b).wait()


def _dispatch_kernel(cnt_ref, loff_ref, goff_ref, tot_ref, padn_ref, padoff_ref,
                     route_ref, h_ref, xs_hbm, stage_ref, zero_ref, sem):
    i = pl.program_id(0)
    last = pl.num_programs(0) - 1
    slot = lax.rem(i, 2)
    T = TOK_TILE
    pos = route_ref[0, TOP_K:2 * TOP_K, :].astype(I32)
    r_iota = lax.broadcasted_iota(I32, (LOCAL_ROWS, T), 0)
    p = jnp.zeros((LOCAL_ROWS, T), F32)
    for k in range(TOP_K):
        p = jnp.where(r_iota == pos[k:k + 1, :], 1.0, p)
    stage_ref[slot] = jnp.dot(p.astype(BF16), h_ref[...], preferred_element_type=F32).astype(BF16)

    def seg_copy(s, d, rows):
        return pltpu.make_async_copy(stage_ref.at[slot, pl.ds(s, rows)], xs_hbm.at[pl.ds(d, rows)],
                                     sem.at[slot])

    def start_segment(e, carry):
        t = i * N_EXPERTS + e
        _segment_copies(cnt_ref[t], loff_ref[t] * GRAN, goff_ref[t] * GRAN, SEG_BITS, seg_copy,
                        lambda cp: cp.start())
        return carry

    lax.fori_loop(0, N_EXPERTS, start_segment, 0)

    def wait_tile(tile, tile_slot):
        _wait_granules(tot_ref[tile], TILE_BITS, lambda rows: pltpu.make_async_copy(
            stage_ref.at[tile_slot, pl.ds(0, rows)], xs_hbm.at[pl.ds(0, rows)], sem.at[tile_slot]))

    @pl.when(i > 0)
    def _():
        wait_tile(i - 1, 1 - slot)

    @pl.when(i == last)
    def _():
        wait_tile(i, slot)
        zero_ref[...] = jnp.zeros(zero_ref.shape, BF16)

        def pad_copy(s, d, rows):
            return pltpu.make_async_copy(zero_ref.at[pl.ds(s, rows)], xs_hbm.at[pl.ds(d, rows)], sem.at[2])

        def for_pads(act):
            def body(e, carry):
                _segment_copies(padn_ref[e], 0, padoff_ref[e] * GRAN, PAD_BITS, pad_copy, act)
                return carry
            lax.fori_loop(0, N_EXPERTS, body, 0)

        for_pads(lambda cp: cp.start())
        for_pads(lambda cp: cp.wait())


def _expert_kernel(blk0_ref, nblk_ref, xs_hbm, w1_ref, b1_ref, w2_ref, b2_ref, ys_hbm,
                   w1b_ref, w2b_ref, xbuf_ref, ybuf_ref, sem_in, sem_out):
    e = pl.program_id(0)
    nb = nblk_ref[e]
    row0 = blk0_ref[e] * ROW_BLOCK

    def in_copy(k, slot):
        r = pl.multiple_of(row0 + k * ROW_BLOCK, ROW_BLOCK)
        return pltpu.make_async_copy(xs_hbm.at[pl.ds(r, ROW_BLOCK)], xbuf_ref.at[slot], sem_in.at[slot])

    def out_copy(k, slot):
        r = pl.multiple_of(row0 + k * ROW_BLOCK, ROW_BLOCK)
        return pltpu.make_async_copy(ybuf_ref.at[slot], ys_hbm.at[pl.ds(r, ROW_BLOCK)], sem_out.at[slot])

    @pl.when(nb > 0)
    def _():
        in_copy(0, 0).start()

    w1b_ref[...] = w1_ref[0].astype(BF16)
    w2b_ref[...] = w2_ref[0].astype(BF16)

    def block(k, carry):
        slot = lax.rem(k, 2)

        @pl.when(k + 1 < nb)
        def _():
            in_copy(k + 1, 1 - slot).start()

        in_copy(k, slot).wait()

        @pl.when(k >= 2)
        def _():
            out_copy(k - 2, slot).wait()

        h1 = jnp.dot(xbuf_ref[slot], w1b_ref[...], preferred_element_type=F32) + b1_ref[0]
        gate = jnp.minimum(h1[:, :D_FF], SWIGLU_LIMIT)
        up = jnp.clip(h1[:, D_FF:], -SWIGLU_LIMIT, SWIGLU_LIMIT)
        act = gate * jax.nn.sigmoid(SWIGLU_ALPHA * gate) * (up + 1.0)
        y = jnp.dot(act.astype(BF16), w2b_ref[...], preferred_element_type=F32) + b2_ref[0]
        ybuf_ref[slot] = y.astype(BF16)
        out_copy(k, slot).start()
        return carry

    lax.fori_loop(0, nb, block, 0)

    @pl.when(nb >= 2)
    def _():
        out_copy(nb - 2, lax.rem(nb, 2)).wait()

    @pl.when(nb >= 1)
    def _():
        out_copy(nb - 1, lax.rem(nb + 1, 2)).wait()


def _combine_kernel(n_prompt_tiles, cnt_ref, loff_ref, goff_ref, tot_ref,
                    routet_ref, x1_ref, gf_ref, ys_hbm, yp_ref, ysm_ref, stage_ref, sem):
    i = pl.program_id(0)
    slot = lax.rem(i, 2)
    T = TOK_TILE

    def start_tile(tile, tile_slot):
        def seg_copy(s, d, rows):
            return pltpu.make_async_copy(ys_hbm.at[pl.ds(d, rows)], stage_ref.at[tile_slot, pl.ds(s, rows)],
                                         sem.at[tile_slot])

        def body(e, carry):
            t = tile * N_EXPERTS + e
            _segment_copies(cnt_ref[t], loff_ref[t] * GRAN, goff_ref[t] * GRAN, SEG_BITS, seg_copy,
                            lambda cp: cp.start())
            return carry
        lax.fori_loop(0, N_EXPERTS, body, 0)

    @pl.when(i == 0)
    def _():
        stage_ref[...] = jnp.zeros(stage_ref.shape, BF16)
        start_tile(i, slot)

    @pl.when(i + 1 < pl.num_programs(0))
    def _():
        start_tile(i + 1, 1 - slot)

    rt = routet_ref[...]
    l_iota = lax.broadcasted_iota(I32, (T, LOCAL_ROWS), 1)
    pw = jnp.zeros((T, LOCAL_ROWS), F32)
    for k in range(TOP_K):
        pos_k = rt[:, TOP_K + k:TOP_K + k + 1].astype(I32)
        pw = jnp.where(l_iota == pos_k, rt[:, 2 * TOP_K + k:2 * TOP_K + k + 1], pw)

    _wait_granules(tot_ref[i], TILE_BITS, lambda rows: pltpu.make_async_copy(
        ys_hbm.at[pl.ds(0, rows)], stage_ref.at[slot, pl.ds(0, rows)], sem.at[slot]))
    moe = jnp.dot(pw.astype(BF16), stage_ref[slot], preferred_element_type=F32)
    y = _rms_norm(x1_ref[...] + moe, gf_ref[...])

    @pl.when(i < n_prompt_tiles)
    def _():
        yp_ref[...] = y

    @pl.when(i >= n_prompt_tiles)
    def _():
        ysm_ref[...] = y


def _t5_bucket(rel):
    half = NUM_BUCKETS // 2
    max_exact = half // 2
    ret = jnp.where(rel > 0, half, 0)
    n = jnp.abs(rel)
    nf = jnp.maximum(n, 1).astype(F32)
    large = max_exact + (jnp.log(nf / max_exact) / math.log(MAX_DISTANCE / max_exact)
                         * (half - max_exact)).astype(jnp.int32)
    large = jnp.minimum(large, half - 1)
    return ret + jnp.where(n < max_exact, n, large)


def _stacked_bias(table, sinks, q_pos, k_pos):
    nq, nk = q_pos.shape[0], k_pos.shape[0]
    bucket = _t5_bucket(k_pos[None, :] - q_pos[:, None])[None]
    tab = table.astype(F32)
    bias = jnp.zeros((N_HEADS, nq, nk), F32)
    for b in range(NUM_BUCKETS):
        bias = jnp.where(bucket == b, tab[b][:, None, None], bias)
    bias = bias.reshape(N_KV, REP * nq, nk)
    sink = jnp.repeat(sinks.astype(F32).reshape(N_KV, REP, 1), nq, axis=2).reshape(N_KV, REP * nq, 1)
    tail = jnp.full((N_KV, REP * nq, KEY_PAD - nk - 1), -jnp.inf, F32)
    return jnp.concatenate([bias, sink, tail], axis=-1)


def kernel(x_prompt, x_sample, cache_win_k, cache_win_v, norm1_g, w_in, attn_sinks, rel_bias_table, sgu_ln_g, sgu_ln_b, sgu_w, sgu_b, w_o_att, w_o_sgu, w_out, norm2_g, w_router, b_router, w_exp_in, b_exp_in, w_exp_out, b_exp_out, final_norm_g):
    batch, seq, _ = x_prompt.shape
    dec_batch, dec_seq, _ = x_sample.shape
    cache_rows = cache_win_k.shape[2]
    assert x_prompt.shape[2] == D_MODEL and w_in.shape == (1, D_MODEL, D_IN)
    assert seq % TOK_TILE == 0 and TOK_TILE % GMLP_CHUNK == 0 and TOK_TILE >= WINDOW
    assert TOK_TILE % dec_seq == 0 and (dec_batch * dec_seq) % TOK_TILE == 0
    assert dec_seq <= GMLP_CHUNK and GMLP_CHUNK % dec_seq == 0 and cache_rows == WINDOW
    T = TOK_TILE
    n_prompt = batch * seq
    n_sample = dec_batch * dec_seq
    n_tok = n_prompt + n_sample
    tiles_per_seq = seq // T
    n_ptiles = n_prompt // T
    n_stiles = n_sample // T
    n_tiles = n_ptiles + n_stiles
    seqs_per_tile = T // dec_seq

    w_q = w_in[0][:, :D_ATT].reshape(D_MODEL, N_KV, REP, HEAD_DIM).transpose(0, 2, 1, 3).reshape(D_MODEL, D_ATT)
    w_in_b = jnp.concatenate([w_q, w_in[0][:, D_ATT:]], axis=1).astype(BF16)
    woa_b = w_o_att[0].reshape(N_KV, REP, HEAD_DIM, D_MODEL).transpose(1, 0, 2, 3).reshape(
        D_ATT, D_MODEL).astype(BF16)
    wos_b = w_o_sgu[0].astype(BF16)
    wout_b = w_out[0].astype(BF16)
    wr_b = jnp.pad(w_router[0], ((0, 0), (0, LANES - N_EXPERTS))).astype(BF16)
    br_col = b_router[0].astype(F32).reshape(N_EXPERTS, 1)
    g1 = norm1_g[0].reshape(1, D_MODEL)
    g2 = norm2_g[0].reshape(1, D_MODEL)
    gf = final_norm_g.reshape(1, D_MODEL)
    lng = sgu_ln_g[0].reshape(1, D_GMLP)
    lnb = sgu_ln_b[0].reshape(1, D_GMLP)
    tril = jnp.tril(jnp.ones((GMLP_CHUNK, GMLP_CHUNK), dtype=bool))
    sguw_p = jnp.where(tril[None], sgu_w[0], 0).astype(BF16)
    sgub_p = jnp.broadcast_to(sgu_b[0][:, :, None], (N_GROUPS, GMLP_CHUNK, GROUP_W)).astype(F32)
    reps = GMLP_CHUNK // dec_seq
    corner = jnp.where(tril[None, :dec_seq, :dec_seq], sgu_w[0][:, :dec_seq, :dec_seq], 0)
    sguw_s = jnp.einsum('ab,gij->gaibj', jnp.eye(reps, dtype=F32), corner).reshape(
        N_GROUPS, GMLP_CHUNK, GMLP_CHUNK).astype(BF16)
    sgub_s = jnp.broadcast_to(jnp.tile(sgu_b[0][:, :dec_seq], (1, reps))[:, :, None],
                              (N_GROUPS, GMLP_CHUNK, GROUP_W)).astype(F32)
    bias_p = _stacked_bias(rel_bias_table, attn_sinks[0], jnp.arange(CHUNK) + WINDOW, jnp.arange(BAND))
    bias_s = _stacked_bias(rel_bias_table, attn_sinks[0], cache_rows + jnp.arange(dec_seq),
                           jnp.arange(cache_rows + dec_seq))
    tri = jnp.triu(jnp.ones((T, T), F32), k=1).astype(BF16)
    low = jnp.tril(jnp.ones((N_EXPERTS, N_EXPERTS), F32), k=-1).astype(BF16)

    layer_out_shapes = (
        jax.ShapeDtypeStruct((n_tok, D_MODEL), F32),
        jax.ShapeDtypeStruct((n_tok, D_MODEL), BF16),
        jax.ShapeDtypeStruct((n_tiles, ROUTE_ROWS, T), F32),
        jax.ShapeDtypeStruct((n_tok, LANES), F32),
        jax.ShapeDtypeStruct((n_tiles, N_EXPERTS, LANES), F32),
    )
    shared_consts = (w_in_b,)
    tail_consts_p = (sguw_p, sgub_p, lng, lnb, woa_b, wos_b, wout_b, g2, wr_b, br_col, tri, low)
    tail_consts_s = (sguw_s, sgub_s, lng, lnb, woa_b, wos_b, wout_b, g2, wr_b, br_col, tri, low)

    def tile_specs(tile_of):
        return [
            pl.BlockSpec((T, D_MODEL), lambda *g: (tile_of(*g), 0)),
            pl.BlockSpec((T, D_MODEL), lambda *g: (tile_of(*g), 0)),
            pl.BlockSpec((1, ROUTE_ROWS, T), lambda *g: (tile_of(*g), 0, 0)),
            pl.BlockSpec((T, LANES), lambda *g: (tile_of(*g), 0)),
            pl.BlockSpec((1, N_EXPERTS, LANES), lambda *g: (tile_of(*g), 0, 0)),
        ]

    prompt_consts = (g1,) + shared_consts + (bias_p,) + tail_consts_p
    x1, h, route, routet, cnt, kvwin = pl.pallas_call(
        _prompt_layer_kernel,
        grid=(batch, tiles_per_seq),
        in_specs=[pl.BlockSpec((1, T, D_MODEL), lambda b, j: (b, j, 0))]
                 + [_const_spec(c.shape) for c in prompt_consts],
        out_specs=tile_specs(lambda b, j: b * tiles_per_seq + j)
                  + [pl.BlockSpec((1, WINDOW, 2 * D_KV), lambda b, j: (b, 0, 0))],
        out_shape=layer_out_shapes + (jax.ShapeDtypeStruct((batch, WINDOW, 2 * D_KV), F32),),
        scratch_shapes=[pltpu.VMEM((T + WINDOW, 2 * D_KV), BF16),
                        pltpu.VMEM((T, D_ATT), BF16),
                        pltpu.VMEM((T, D_GMLP), BF16)],
        compiler_params=pltpu.CompilerParams(dimension_semantics=("arbitrary", "arbitrary"),
                                             vmem_limit_bytes=VMEM_LIMIT),
        name="layer_prompt",
    )(x_prompt, *prompt_consts)

    xs_flat = x_sample.reshape(n_sample, D_MODEL)
    ck = cache_win_k[0].reshape(dec_batch, cache_rows, D_KV)
    cv = cache_win_v[0].reshape(dec_batch, cache_rows, D_KV)
    sample_consts = (g1,) + shared_consts + (bias_s,) + tail_consts_s
    any_spec = pl.BlockSpec(memory_space=pl.ANY)
    x1, h, route, routet, cnt, kvnew, vn_s = pl.pallas_call(
        _sample_layer_kernel,
        grid=(n_stiles,),
        in_specs=[any_spec] * 5
                 + [pl.BlockSpec((T, D_MODEL), lambda i: (i, 0)),
                    pl.BlockSpec((seqs_per_tile, cache_rows, D_KV), lambda i: (i, 0, 0)),
                    pl.BlockSpec((seqs_per_tile, cache_rows, D_KV), lambda i: (i, 0, 0))]
                 + [_const_spec(c.shape) for c in sample_consts],
        out_specs=tile_specs(lambda i: n_ptiles + i)
                  + [pl.BlockSpec((T, 2 * D_KV), lambda i: (i, 0)),
                     pl.BlockSpec((T, D_GMLP), lambda i: (i, 0))],
        out_shape=layer_out_shapes + (jax.ShapeDtypeStruct((n_sample, 2 * D_KV), F32),
                                      jax.ShapeDtypeStruct((n_sample, D_GMLP), F32)),
        scratch_shapes=[pltpu.VMEM((T, D_ATT), BF16), pltpu.VMEM((T, D_GMLP), BF16)],
        input_output_aliases={0: 0, 1: 1, 2: 2, 3: 3, 4: 4},
        compiler_params=pltpu.CompilerParams(dimension_semantics=("arbitrary",),
                                             vmem_limit_bytes=VMEM_LIMIT),
        name="layer_sample",
    )(x1, h, route, routet, cnt, xs_flat, ck, cv, *sample_consts)

    blk_gran = ROW_BLOCK // GRAN
    counts = cnt[:, :, 0].astype(I32)
    seg_gran = (counts + (GRAN - 1)) // GRAN
    local_off = jnp.cumsum(seg_gran, axis=1) - seg_gran
    tot_gran = jnp.sum(seg_gran, axis=0)
    ptot_gran = (tot_gran + (blk_gran - 1)) // blk_gran * blk_gran
    pend_gran = jnp.cumsum(ptot_gran)
    gstart = pend_gran - ptot_gran
    global_off = gstart[None, :] + jnp.cumsum(seg_gran, axis=0) - seg_gran
    pad_n = ptot_gran - tot_gran
    pad_off = gstart + tot_gran
    n_rows = -(-(TOP_K * n_tok + n_tiles * N_EXPERTS * (GRAN - 1) + N_EXPERTS * (ROW_BLOCK - GRAN))
               // ROW_BLOCK) * ROW_BLOCK
    blk0 = (gstart // blk_gran).astype(I32)
    nblk = (ptot_gran // blk_gran).astype(I32)
    tile_gran = jnp.sum(seg_gran, axis=1).astype(I32)
    seg_gran_f = seg_gran.reshape(-1).astype(I32)
    local_off_f = local_off.reshape(-1).astype(I32)
    global_off_f = global_off.reshape(-1).astype(I32)

    xs = pl.pallas_call(
        _dispatch_kernel,
        grid_spec=pltpu.PrefetchScalarGridSpec(
            num_scalar_prefetch=6,
            grid=(n_tiles,),
            in_specs=[pl.BlockSpec((1, ROUTE_ROWS, T), lambda i, *_: (i, 0, 0)),
                      pl.BlockSpec((T, D_MODEL), lambda i, *_: (i, 0))],
            out_specs=pl.BlockSpec(memory_space=pl.ANY),
            scratch_shapes=[pltpu.VMEM((2, LOCAL_ROWS, D_MODEL), BF16),
                            pltpu.VMEM((ROW_BLOCK, D_MODEL), BF16),
                            pltpu.SemaphoreType.DMA((3,))]),
        out_shape=jax.ShapeDtypeStruct((n_rows, D_MODEL), BF16),
        compiler_params=pltpu.CompilerParams(dimension_semantics=("arbitrary",),
                                             vmem_limit_bytes=VMEM_LIMIT),
        name="moe_dispatch",
    )(seg_gran_f, local_off_f, global_off_f, tile_gran, pad_n.astype(I32), pad_off.astype(I32), route, h)

    ys = pl.pallas_call(
        _expert_kernel,
        grid_spec=pltpu.PrefetchScalarGridSpec(
            num_scalar_prefetch=2,
            grid=(N_EXPERTS,),
            in_specs=[pl.BlockSpec(memory_space=pl.ANY),
                      pl.BlockSpec((1, D_MODEL, 2 * D_FF), lambda e, *_: (e, 0, 0)),
                      pl.BlockSpec((1, 1, 2 * D_FF), lambda e, *_: (e, 0, 0)),
                      pl.BlockSpec((1, D_FF, D_MODEL), lambda e, *_: (e, 0, 0)),
                      pl.BlockSpec((1, 1, D_MODEL), lambda e, *_: (e, 0, 0))],
            out_specs=pl.BlockSpec(memory_space=pl.ANY),
            scratch_shapes=[pltpu.VMEM((D_MODEL, 2 * D_FF), BF16), pltpu.VMEM((D_FF, D_MODEL), BF16),
                            pltpu.VMEM((2, ROW_BLOCK, D_MODEL), BF16), pltpu.VMEM((2, ROW_BLOCK, D_MODEL), BF16),
                            pltpu.SemaphoreType.DMA((2,)), pltpu.SemaphoreType.DMA((2,))]),
        out_shape=jax.ShapeDtypeStruct((n_rows, D_MODEL), BF16),
        compiler_params=pltpu.CompilerParams(dimension_semantics=("arbitrary",),
                                             vmem_limit_bytes=VMEM_LIMIT),
        name="moe_experts",
    )(blk0, nblk, xs, w_exp_in[0], b_exp_in[0].reshape(N_EXPERTS, 1, 2 * D_FF),
      w_exp_out[0], b_exp_out[0].reshape(N_EXPERTS, 1, D_MODEL))

    y_p, y_s = pl.pallas_call(
        functools.partial(_combine_kernel, n_ptiles),
        grid_spec=pltpu.PrefetchScalarGridSpec(
            num_scalar_prefetch=4,
            grid=(n_tiles,),
            in_specs=[pl.BlockSpec((T, LANES), lambda i, *_: (i, 0)),
                      pl.BlockSpec((T, D_MODEL), lambda i, *_: (i, 0)),
                      pl.BlockSpec((1, D_MODEL), lambda i, *_: (0, 0)),
                      pl.BlockSpec(memory_space=pl.ANY)],
            out_specs=[pl.BlockSpec((T, D_MODEL), lambda i, *_: (jnp.minimum(i, n_ptiles - 1), 0)),
                       pl.BlockSpec((T, D_MODEL), lambda i, *_: (jnp.maximum(i - n_ptiles, 0), 0))],
            scratch_shapes=[pltpu.VMEM((2, LOCAL_ROWS, D_MODEL), BF16), pltpu.SemaphoreType.DMA((2,))]),
        out_shape=(jax.ShapeDtypeStruct((n_prompt, D_MODEL), F32),
                   jax.ShapeDtypeStruct((n_sample, D_MODEL), F32)),
        compiler_params=pltpu.CompilerParams(dimension_semantics=("arbitrary",),
                                             vmem_limit_bytes=VMEM_LIMIT),
        name="moe_combine",
    )(seg_gran_f, local_off_f, global_off_f, tile_gran, routet, x1, gf, ys)

    y_prompt = y_p.reshape(batch, seq, D_MODEL)
    y_sample = y_s.reshape(dec_batch, dec_seq, D_MODEL)
    new_win_k_prompt = kvwin[:, :, :D_KV].reshape(1, batch, WINDOW, N_KV, HEAD_DIM)
    new_win_v_prompt = kvwin[:, :, D_KV:].reshape(1, batch, WINDOW, N_KV, HEAD_DIM)
    new_win_k_sample = kvnew[:, :D_KV].reshape(1, dec_batch, dec_seq, N_KV, HEAD_DIM)
    new_win_v_sample = kvnew[:, D_KV:].reshape(1, dec_batch, dec_seq, N_KV, HEAD_DIM)
    new_sgu_v_sample = vn_s.reshape(1, dec_batch, dec_seq, D_GMLP)
    return (y_prompt, y_sample, new_win_k_prompt, new_win_v_prompt, new_win_k_sample,
            new_win_v_sample, new_sgu_v_sample)
```

```python
import functools
import math

import numpy as np
import jax
import jax.numpy as jnp
from jax import lax
from jax.experimental import pallas as pl
from jax.experimental.pallas import tpu as pltpu

F32 = jnp.float32
BF16 = jnp.bfloat16
I32 = jnp.int32

D_MODEL = 1024
HEAD_DIM = 64
N_HEADS = 16
N_KV = 2
REP = N_HEADS // N_KV
CHUNK = 64
WINDOW = 128
BAND = WINDOW + CHUNK
KEY_PAD = 256
D_ATT = N_HEADS * HEAD_DIM
D_KV = N_KV * HEAD_DIM
NUM_BUCKETS = 32
MAX_DISTANCE = 128
GMLP_CHUNK = 128
D_GMLP = 1024
N_GROUPS = 4
GROUP_W = D_GMLP // N_GROUPS
N_EXPERTS = 32
TOP_K = 4
D_FF = 1024
SWIGLU_LIMIT = 7.0
SWIGLU_ALPHA = 1.702
NORM_EPS = 1e-5
D_IN = D_ATT + 2 * D_KV + 2 * D_GMLP + 2 * D_MODEL
D_REST = D_IN - D_ATT
COL_KV = 0
COL_U = COL_KV + 2 * D_KV
COL_VG = COL_U + D_GMLP
COL_GA = COL_VG + D_GMLP
COL_GB = COL_GA + D_MODEL
SQRT_HALF = float(np.sqrt(0.5))

LANES = 128
BF16_SUBLANES = 16
VMEM_LIMIT = 56 * 1024 * 1024

TOK_TILE = 256
GRAN = BF16_SUBLANES
SEG_BITS = (TOK_TILE // GRAN).bit_length()
LOCAL_ROWS = TOP_K * TOK_TILE + N_EXPERTS * GRAN
ROW_BLOCK = 256
PAD_BITS = (ROW_BLOCK // GRAN - 1).bit_length()
TILE_BITS = (LOCAL_ROWS // GRAN).bit_length()
BLOCK_DMA_PRIORITY = 1
ROUTE_ROWS = 16


def _const_spec(shape):
    nd = len(shape)
    return pl.BlockSpec(shape, lambda *_: (0,) * nd, pipeline_mode=pl.Buffered(1))


def _rms_norm(x, g):
    ms = jnp.mean(x * x, axis=-1, keepdims=True)
    return x * lax.rsqrt(ms + NORM_EPS) * g


def _gelu(x):
    return 0.5 * x * (1.0 + lax.erf(x * SQRT_HALF))


def _attend(q, k, v, bias, valid):
    logits = lax.dot_general(q, k, (((1,), (1,)), ((), ())), preferred_element_type=F32) + bias
    if valid is not None:
        logits = jnp.where(valid, logits, -jnp.inf)
    m = jnp.max(logits, axis=-1, keepdims=True)
    p = jnp.exp(logits - m)
    den = jnp.sum(p, axis=-1, keepdims=True)
    w = (p * (1.0 / den)).astype(BF16)
    return jnp.dot(w, v, preferred_element_type=F32)


def _attention_rows(zq, row0, n_rows, k_ext, v_ext, bias_ref, valid, oatt_ref, heads_per_unit):
    lane = lax.broadcasted_iota(I32, (1, LANES), 1)
    for g in range(N_KV):
        in_group = (lane >= g * HEAD_DIM) & (lane < (g + 1) * HEAD_DIM)
        qmask = jnp.where(in_group, HEAD_DIM ** -0.5, 0.0).astype(BF16)
        for part in range(REP // heads_per_unit):
            tiles = range(part * heads_per_unit, (part + 1) * heads_per_unit)
            q = jnp.concatenate([zq[row0:row0 + n_rows, t * LANES:(t + 1) * LANES] * qmask for t in tiles],
                                axis=0)
            b0 = part * heads_per_unit * n_rows
            o = _attend(q, k_ext, v_ext, bias_ref[g, b0:b0 + heads_per_unit * n_rows, :], valid)
            for u, t in enumerate(tiles):
                c0 = t * LANES + g * HEAD_DIM
                oatt_ref[row0:row0 + n_rows, c0:c0 + HEAD_DIM] = (
                    o[u * n_rows:(u + 1) * n_rows, g * HEAD_DIM:(g + 1) * HEAD_DIM].astype(BF16))


def _layer_tail(x, xn_b, oatt_ref, osgu_ref, w_in_ref, sguw_ref, sgub_ref, lng_ref, lnb_ref,
                woa_ref, wos_ref, wout_ref, g2_ref, wr_ref, br_ref, tri_ref, low_ref,
                x1_ref, h_ref, route_ref, routet_ref, cnt_ref, vn_ref):
    T = x.shape[0]
    u = _gelu(jnp.dot(xn_b, w_in_ref[:, COL_U:COL_VG], preferred_element_type=F32))
    vg = _gelu(jnp.dot(xn_b, w_in_ref[:, COL_VG:COL_GA], preferred_element_type=F32))
    mu = jnp.mean(vg, axis=-1, keepdims=True)
    var = jnp.mean(jnp.square(vg - mu), axis=-1, keepdims=True)
    vn = (vg - mu) * lax.rsqrt(var + NORM_EPS) * lng_ref[...] + lnb_ref[...]
    if vn_ref is not None:
        vn_ref[...] = vn
    vn_b = vn.astype(BF16)
    for j in range(T // GMLP_CHUNK):
        rows = slice(j * GMLP_CHUNK, (j + 1) * GMLP_CHUNK)
        for g in range(N_GROUPS):
            cols = slice(g * GROUP_W, (g + 1) * GROUP_W)
            mixed = jnp.dot(sguw_ref[g], vn_b[rows, cols], preferred_element_type=F32) + sgub_ref[g]
            osgu_ref[rows, cols] = (u[rows, cols] * mixed).astype(BF16)

    ga = jax.nn.sigmoid(jnp.dot(xn_b, w_in_ref[:, COL_GA:COL_GB], preferred_element_type=F32))
    merged = ga * jnp.dot(oatt_ref[...], woa_ref[...], preferred_element_type=F32)
    gb = jax.nn.sigmoid(jnp.dot(xn_b, w_in_ref[:, COL_GB:D_REST], preferred_element_type=F32))
    merged = merged + gb * jnp.dot(osgu_ref[...], wos_ref[...], preferred_element_type=F32)
    x1 = x + jnp.dot(merged.astype(BF16), wout_ref[...], preferred_element_type=F32)
    x1_ref[...] = x1

    h_b = _rms_norm(x1, g2_ref[...]).astype(BF16)
    h_ref[...] = h_b
    logits = jnp.dot(h_b, wr_ref[...], preferred_element_type=F32)
    lt = jnp.transpose(logits)[:N_EXPERTS, :] + br_ref[...]
    e_iota = lax.broadcasted_iota(I32, (N_EXPERTS, T), 0).astype(F32)
    cur = lt
    vals, idxs = [], []
    for _ in range(TOP_K):
        m = jnp.max(cur, axis=0, keepdims=True)
        ik = jnp.min(jnp.where(cur == m, e_iota, float(N_EXPERTS)), axis=0, keepdims=True)
        vals.append(m)
        idxs.append(ik)
        cur = jnp.where(e_iota == ik, -jnp.inf, cur)
    exps = [jnp.exp(v - vals[0]) for v in vals]
    den = exps[0] + exps[1] + exps[2] + exps[3]
    gates = [e / den for e in exps]

    onehot = jnp.zeros((N_EXPERTS, T), F32)
    for ik in idxs:
        onehot = onehot + jnp.where(e_iota == ik, 1.0, 0.0)
    rank = jnp.dot(onehot.astype(BF16), tri_ref[...], preferred_element_type=F32)
    cnt = jnp.sum(onehot, axis=1, keepdims=True)
    cnt_b = jnp.broadcast_to(cnt, (N_EXPERTS, LANES))
    gran = jnp.floor((cnt_b + (GRAN - 1)) * (1.0 / GRAN))
    off = jnp.dot(low_ref[...], gran.astype(BF16), preferred_element_type=F32) * GRAN
    base = off[:, 0:1] + rank
    poss = [jnp.sum(jnp.where(e_iota == ik, base, 0.0), axis=0, keepdims=True) for ik in idxs]

    rec = jnp.concatenate(idxs + poss + gates
                          + [jnp.zeros((ROUTE_ROWS - 3 * TOP_K, T), F32)], axis=0)
    route_ref[0] = rec
    rec_pad = jnp.concatenate([rec, jnp.zeros((LANES - ROUTE_ROWS, T), F32)], axis=0)
    routet_ref[...] = jnp.transpose(rec_pad)
    cnt_ref[0] = cnt_b


def _prompt_layer_kernel(x_ref, g1_ref, wq_ref, w_in_ref, bias_ref, sguw_ref, sgub_ref, lng_ref,
                         lnb_ref, woa_ref, wos_ref, wout_ref, g2_ref, wr_ref, br_ref, tri_ref, low_ref,
                         x1_ref, h_ref, route_ref, routet_ref, cnt_ref, kvwin_ref,
                         kvx_ref, oatt_ref, osgu_ref):
    T = TOK_TILE
    j = pl.program_id(1)
    x = x_ref[0]
    xn_b = _rms_norm(x, g1_ref[...]).astype(BF16)
    zq = jnp.dot(xn_b, wq_ref[...], preferred_element_type=F32).astype(BF16)
    zkv = jnp.dot(xn_b, w_in_ref[:, COL_KV:COL_U], preferred_element_type=F32)

    @pl.when(j == pl.num_programs(1) - 1)
    def _():
        kvwin_ref[0] = zkv[T - WINDOW:, :]

    @pl.when(j == 0)
    def _():
        kvx_ref[0:WINDOW, :] = jnp.zeros((WINDOW, 2 * D_KV), BF16)

    @pl.when(j > 0)
    def _():
        kvx_ref[0:WINDOW, :] = kvx_ref[T:T + WINDOW, :]

    kvx_ref[WINDOW:, :] = zkv.astype(BF16)

    col = lax.broadcasted_iota(I32, (1, KEY_PAD), 1)
    key_pad = jnp.zeros((KEY_PAD - BAND, 2 * D_KV), BF16)
    for c in range(T // CHUNK):
        r0 = c * CHUNK
        kvb = jnp.concatenate([kvx_ref[r0:r0 + BAND, :], key_pad], axis=0)
        valid = ((col + (j * T + r0 - WINDOW)) >= 0) | (col >= BAND) if r0 < WINDOW else None
        _attention_rows(zq, r0, CHUNK, kvb[:, :D_KV], kvb[:, D_KV:], bias_ref, valid, oatt_ref,
                        heads_per_unit=4)

    _layer_tail(x, xn_b, oatt_ref, osgu_ref, w_in_ref, sguw_ref, sgub_ref, lng_ref, lnb_ref,
                woa_ref, wos_ref, wout_ref, g2_ref, wr_ref, br_ref, tri_ref, low_ref,
                x1_ref, h_ref, route_ref, routet_ref, cnt_ref, None)


def _sample_layer_kernel(x1_in, h_in, route_in, routet_in, cnt_in,
                         x_ref, ck_ref, cv_ref, g1_ref, wq_ref, w_in_ref, bias_ref, sguw_ref, sgub_ref,
                         lng_ref, lnb_ref, woa_ref, wos_ref, wout_ref, g2_ref, wr_ref, br_ref, tri_ref,
                         low_ref,
                         x1_ref, h_ref, route_ref, routet_ref, cnt_ref, kvnew_ref, vn_ref,
                         oatt_ref, osgu_ref):
    del x1_in, h_in, route_in, routet_in, cnt_in
    T = TOK_TILE
    nq = x_ref.shape[0] // ck_ref.shape[0]
    x = x_ref[...]
    xn_b = _rms_norm(x, g1_ref[...]).astype(BF16)
    zq = jnp.dot(xn_b, wq_ref[...], preferred_element_type=F32).astype(BF16)
    zkv = jnp.dot(xn_b, w_in_ref[:, COL_KV:COL_U], preferred_element_type=F32)
    kvnew_ref[...] = zkv
    zkv_b = zkv.astype(BF16)
    n_cache = ck_ref.shape[1]
    key_pad = jnp.zeros((KEY_PAD - n_cache - nq, D_KV), BF16)
    for b in range(T // nq):
        r0 = b * nq
        kk = jnp.concatenate([ck_ref[b].astype(BF16), zkv_b[r0:r0 + nq, :D_KV], key_pad], axis=0)
        vv = jnp.concatenate([cv_ref[b].astype(BF16), zkv_b[r0:r0 + nq, D_KV:], key_pad], axis=0)
        _attention_rows(zq, r0, nq, kk, vv, bias_ref, None, oatt_ref, heads_per_unit=REP)

    _layer_tail(x, xn_b, oatt_ref, osgu_ref, w_in_ref, sguw_ref, sgub_ref, lng_ref, lnb_ref,
                woa_ref, wos_ref, wout_ref, g2_ref, wr_ref, br_ref, tri_ref, low_ref,
                x1_ref, h_ref, route_ref, routet_ref, cnt_ref, vn_ref)


def _segment_copies(n_gran, src_row, dst_row, bits, make_copy, act):
    for b in range(min(bits, 2)):
        rows = GRAN << b

        @pl.when(((n_gran >> b) & 1) == 1)
        def _(b=b, rows=rows):
            done = (n_gran & ((1 << b) - 1)) * GRAN
            s = pl.multiple_of(src_row + done, GRAN)
            d = pl.multiple_of(dst_row + done, GRAN)
            act(make_copy(s, d, rows))

    if bits > 2:
        def quad(q, carry):
            done = ((n_gran & 3) + 4 * q) * GRAN
            s = pl.multiple_of(src_row + done, GRAN)
            d = pl.multiple_of(dst_row + done, GRAN)
            act(make_copy(s, d, 4 * GRAN))
            return carry
        lax.fori_loop(0, n_gran >> 2, quad, 0)


def _wait_granules(n_gran, bits, make_copy):
    for b in range(bits):
        @pl.when(((n_gran >> b) & 1) == 1)
        def _(b=b):
            make_copy(GRAN << b).wait()


def _dispatch_kernel(cnt_ref, loff_ref, goff_ref, tot_ref, padn_ref, padoff_ref,
                     route_ref, h_ref, xs_hbm, stage_ref, zero_ref, sem):
    i = pl.program_id(0)
    last = pl.num_programs(0) - 1
    slot = lax.rem(i, 2)
    T = TOK_TILE
    pos = route_ref[0, TOP_K:2 * TOP_K, :].astype(I32)
    r_iota = lax.broadcasted_iota(I32, (LOCAL_ROWS, T), 0)
    p = jnp.zeros((LOCAL_ROWS, T), F32)
    for k in range(TOP_K):
        p = jnp.where(r_iota == pos[k:k + 1, :], 1.0, p)
    stage_ref[slot] = jnp.dot(p.astype(BF16), h_ref[...], preferred_element_type=F32).astype(BF16)

    def seg_copy(s, d, rows):
        return pltpu.make_async_copy(stage_ref.at[slot, pl.ds(s, rows)], xs_hbm.at[pl.ds(d, rows)],
                                     sem.at[slot])

    def start_segment(e, carry):
        t = i * N_EXPERTS + e
        _segment_copies(cnt_ref[t], loff_ref[t] * GRAN, goff_ref[t] * GRAN, SEG_BITS, seg_copy,
                        lambda cp: cp.start())
        return carry

    lax.fori_loop(0, N_EXPERTS, start_segment, 0)

    def wait_tile(tile, tile_slot):
        _wait_granules(tot_ref[tile], TILE_BITS, lambda rows: pltpu.make_async_copy(
            stage_ref.at[tile_slot, pl.ds(0, rows)], xs_hbm.at[pl.ds(0, rows)], sem.at[tile_slot]))

    @pl.when(i > 0)
    def _():
        wait_tile(i - 1, 1 - slot)

    @pl.when(i == last)
    def _():
        wait_tile(i, slot)
        zero_ref[...] = jnp.zeros(zero_ref.shape, BF16)

        def pad_copy(s, d, rows):
            return pltpu.make_async_copy(zero_ref.at[pl.ds(s, rows)], xs_hbm.at[pl.ds(d, rows)], sem.at[2])

        def for_pads(act):
            def body(e, carry):
                _segment_copies(padn_ref[e], 0, padoff_ref[e] * GRAN, PAD_BITS, pad_copy, act)
                return carry
            lax.fori_loop(0, N_EXPERTS, body, 0)

        for_pads(lambda cp: cp.start())
        for_pads(lambda cp: cp.wait())


def _expert_kernel(blk0_ref, nblk_ref, xs_hbm, w1_ref, b1_ref, w2_ref, b2_ref, ys_hbm,
                   w1b_ref, w2b_ref, xbuf_ref, ybuf_ref, sem_in, sem_out):
    e = pl.program_id(0)
    nb = nblk_ref[e]
    row0 = blk0_ref[e] * ROW_BLOCK

    def in_copy(k, slot):
        r = pl.multiple_of(row0 + k * ROW_BLOCK, ROW_BLOCK)
        return pltpu.make_async_copy(xs_hbm.at[pl.ds(r, ROW_BLOCK)], xbuf_ref.at[slot], sem_in.at[slot])

    def out_copy(k, slot):
        r = pl.multiple_of(row0 + k * ROW_BLOCK, ROW_BLOCK)
        return pltpu.make_async_copy(ybuf_ref.at[slot], ys_hbm.at[pl.ds(r, ROW_BLOCK)], sem_out.at[slot])

    @pl.when(nb > 0)
    def _():
        in_copy(0, 0).start(priority=BLOCK_DMA_PRIORITY)

    w1b_ref[...] = w1_ref[0].astype(BF16)
    w2b_ref[...] = w2_ref[0].astype(BF16)

    def block(k, carry):
        slot = lax.rem(k, 2)

        @pl.when(k + 1 < nb)
        def _():
            in_copy(k + 1, 1 - slot).start(priority=BLOCK_DMA_PRIORITY)

        in_copy(k, slot).wait()

        @pl.when(k >= 2)
        def _():
            out_copy(k - 2, slot).wait()

        h1 = jnp.dot(xbuf_ref[slot], w1b_ref[...], preferred_element_type=F32) + b1_ref[0]
        gate = jnp.minimum(h1[:, :D_FF], SWIGLU_LIMIT)
        up = jnp.clip(h1[:, D_FF:], -SWIGLU_LIMIT, SWIGLU_LIMIT)
        act = gate * jax.nn.sigmoid(SWIGLU_ALPHA * gate) * (up + 1.0)
        y = jnp.dot(act.astype(BF16), w2b_ref[...], preferred_element_type=F32) + b2_ref[0]
        ybuf_ref[slot] = y.astype(BF16)
        out_copy(k, slot).start(priority=BLOCK_DMA_PRIORITY)
        return carry

    lax.fori_loop(0, nb, block, 0)

    @pl.when(nb >= 2)
    def _():
        out_copy(nb - 2, lax.rem(nb, 2)).wait()

    @pl.when(nb >= 1)
    def _():
        out_copy(nb - 1, lax.rem(nb + 1, 2)).wait()


def _combine_kernel(n_prompt_tiles, cnt_ref, loff_ref, goff_ref, tot_ref,
                    routet_ref, x1_ref, gf_ref, ys_hbm, yp_ref, ysm_ref, stage_ref, sem):
    i = pl.program_id(0)
    slot = lax.rem(i, 2)
    T = TOK_TILE

    def start_tile(tile, tile_slot):
        def seg_copy(s, d, rows):
            return pltpu.make_async_copy(ys_hbm.at[pl.ds(d, rows)], stage_ref.at[tile_slot, pl.ds(s, rows)],
                                         sem.at[tile_slot])

        def body(e, carry):
            t = tile * N_EXPERTS + e
            _segment_copies(cnt_ref[t], loff_ref[t] * GRAN, goff_ref[t] * GRAN, SEG_BITS, seg_copy,
                            lambda cp: cp.start())
            return carry
        lax.fori_loop(0, N_EXPERTS, body, 0)

    @pl.when(i == 0)
    def _():
        stage_ref[...] = jnp.zeros(stage_ref.shape, BF16)
        start_tile(i, slot)

    @pl.when(i + 1 < pl.num_programs(0))
    def _():
        start_tile(i + 1, 1 - slot)

    rt = routet_ref[...]
    l_iota = lax.broadcasted_iota(I32, (T, LOCAL_ROWS), 1)
    pw = jnp.zeros((T, LOCAL_ROWS), F32)
    for k in range(TOP_K):
        pos_k = rt[:, TOP_K + k:TOP_K + k + 1].astype(I32)
        pw = jnp.where(l_iota == pos_k, rt[:, 2 * TOP_K + k:2 * TOP_K + k + 1], pw)

    _wait_granules(tot_ref[i], TILE_BITS, lambda rows: pltpu.make_async_copy(
        ys_hbm.at[pl.ds(0, rows)], stage_ref.at[slot, pl.ds(0, rows)], sem.at[slot]))
    moe = jnp.dot(pw.astype(BF16), stage_ref[slot], preferred_element_type=F32)
    y = _rms_norm(x1_ref[...] + moe, gf_ref[...])

    @pl.when(i < n_prompt_tiles)
    def _():
        yp_ref[...] = y

    @pl.when(i >= n_prompt_tiles)
    def _():
        ysm_ref[...] = y


def _t5_bucket(rel):
    half = NUM_BUCKETS // 2
    max_exact = half // 2
    ret = jnp.where(rel > 0, half, 0)
    n = jnp.abs(rel)
    nf = jnp.maximum(n, 1).astype(F32)
    large = max_exact + (jnp.log(nf / max_exact) / math.log(MAX_DISTANCE / max_exact)
                         * (half - max_exact)).astype(jnp.int32)
    large = jnp.minimum(large, half - 1)
    return ret + jnp.where(n < max_exact, n, large)


def _stacked_bias(table, sinks, q_pos, k_pos):
    nq, nk = q_pos.shape[0], k_pos.shape[0]
    bucket = _t5_bucket(k_pos[None, :] - q_pos[:, None])
    onehot = (bucket[:, :, None] == jnp.arange(NUM_BUCKETS)).astype(F32)
    bias = jnp.einsum('qkb,bh->hqk', onehot, table.astype(F32), precision=lax.Precision.HIGHEST)
    bias = bias.reshape(N_KV, REP * nq, nk)
    sink = jnp.repeat(sinks.astype(F32).reshape(N_KV, REP, 1), nq, axis=2).reshape(N_KV, REP * nq, 1)
    tail = jnp.full((N_KV, REP * nq, KEY_PAD - nk - 1), -jnp.inf, F32)
    return jnp.concatenate([bias, sink, tail], axis=-1)


def kernel(x_prompt, x_sample, cache_win_k, cache_win_v, norm1_g, w_in, attn_sinks, rel_bias_table, sgu_ln_g, sgu_ln_b, sgu_w, sgu_b, w_o_att, w_o_sgu, w_out, norm2_g, w_router, b_router, w_exp_in, b_exp_in, w_exp_out, b_exp_out, final_norm_g):
    batch, seq, _ = x_prompt.shape
    dec_batch, dec_seq, _ = x_sample.shape
    cache_rows = cache_win_k.shape[2]
    assert x_prompt.shape[2] == D_MODEL and w_in.shape == (1, D_MODEL, D_IN)
    assert seq % TOK_TILE == 0 and TOK_TILE % GMLP_CHUNK == 0 and TOK_TILE >= WINDOW
    assert TOK_TILE % dec_seq == 0 and (dec_batch * dec_seq) % TOK_TILE == 0
    assert dec_seq <= GMLP_CHUNK and GMLP_CHUNK % dec_seq == 0 and cache_rows == WINDOW
    T = TOK_TILE
    n_prompt = batch * seq
    n_sample = dec_batch * dec_seq
    n_tok = n_prompt + n_sample
    tiles_per_seq = seq // T
    n_ptiles = n_prompt // T
    n_stiles = n_sample // T
    n_tiles = n_ptiles + n_stiles
    seqs_per_tile = T // dec_seq

    w_q = w_in[0][:, :D_ATT].reshape(D_MODEL, N_KV, REP, HEAD_DIM).transpose(0, 2, 1, 3).reshape(D_MODEL, D_ATT)
    w_q_b = w_q.astype(BF16)
    w_in_b = w_in[0][:, D_ATT:].astype(BF16)
    woa_b = w_o_att[0].reshape(N_KV, REP, HEAD_DIM, D_MODEL).transpose(1, 0, 2, 3).reshape(
        D_ATT, D_MODEL).astype(BF16)
    wos_b = w_o_sgu[0].astype(BF16)
    wout_b = w_out[0].astype(BF16)
    wr_b = jnp.pad(w_router[0], ((0, 0), (0, LANES - N_EXPERTS))).astype(BF16)
    br_col = b_router[0].astype(F32).reshape(N_EXPERTS, 1)
    g1 = norm1_g[0].reshape(1, D_MODEL)
    g2 = norm2_g[0].reshape(1, D_MODEL)
    gf = final_norm_g.reshape(1, D_MODEL)
    lng = sgu_ln_g[0].reshape(1, D_GMLP)
    lnb = sgu_ln_b[0].reshape(1, D_GMLP)
    tril = jnp.tril(jnp.ones((GMLP_CHUNK, GMLP_CHUNK), dtype=bool))
    sguw_p = jnp.where(tril[None], sgu_w[0], 0).astype(BF16)
    sgub_p = jnp.broadcast_to(sgu_b[0][:, :, None], (N_GROUPS, GMLP_CHUNK, GROUP_W)).astype(F32)
    reps = GMLP_CHUNK // dec_seq
    corner = jnp.where(tril[None, :dec_seq, :dec_seq], sgu_w[0][:, :dec_seq, :dec_seq], 0)
    sguw_s = jnp.einsum('ab,gij->gaibj', jnp.eye(reps, dtype=F32), corner).reshape(
        N_GROUPS, GMLP_CHUNK, GMLP_CHUNK).astype(BF16)
    sgub_s = jnp.broadcast_to(jnp.tile(sgu_b[0][:, :dec_seq], (1, reps))[:, :, None],
                              (N_GROUPS, GMLP_CHUNK, GROUP_W)).astype(F32)
    bias_p = _stacked_bias(rel_bias_table, attn_sinks[0], jnp.arange(CHUNK) + WINDOW, jnp.arange(BAND))
    bias_s = _stacked_bias(rel_bias_table, attn_sinks[0], cache_rows + jnp.arange(dec_seq),
                           jnp.arange(cache_rows + dec_seq))
    tri = jnp.triu(jnp.ones((T, T), F32), k=1).astype(BF16)
    low = jnp.tril(jnp.ones((N_EXPERTS, N_EXPERTS), F32), k=-1).astype(BF16)

    layer_out_shapes = (
        jax.ShapeDtypeStruct((n_tok, D_MODEL), F32),
        jax.ShapeDtypeStruct((n_tok, D_MODEL), BF16),
        jax.ShapeDtypeStruct((n_tiles, ROUTE_ROWS, T), F32),
        jax.ShapeDtypeStruct((n_tok, LANES), F32),
        jax.ShapeDtypeStruct((n_tiles, N_EXPERTS, LANES), F32),
    )
    shared_consts = (w_q_b, w_in_b)
    tail_consts_p = (sguw_p, sgub_p, lng, lnb, woa_b, wos_b, wout_b, g2, wr_b, br_col, tri, low)
    tail_consts_s = (sguw_s, sgub_s, lng, lnb, woa_b, wos_b, wout_b, g2, wr_b, br_col, tri, low)

    def tile_specs(tile_of):
        return [
            pl.BlockSpec((T, D_MODEL), lambda *g: (tile_of(*g), 0)),
            pl.BlockSpec((T, D_MODEL), lambda *g: (tile_of(*g), 0)),
            pl.BlockSpec((1, ROUTE_ROWS, T), lambda *g: (tile_of(*g), 0, 0)),
            pl.BlockSpec((T, LANES), lambda *g: (tile_of(*g), 0)),
            pl.BlockSpec((1, N_EXPERTS, LANES), lambda *g: (tile_of(*g), 0, 0)),
        ]

    prompt_consts = (g1,) + shared_consts + (bias_p,) + tail_consts_p
    x1, h, route, routet, cnt, kvwin = pl.pallas_call(
        _prompt_layer_kernel,
        grid=(batch, tiles_per_seq),
        in_specs=[pl.BlockSpec((1, T, D_MODEL), lambda b, j: (b, j, 0))]
                 + [_const_spec(c.shape) for c in prompt_consts],
        out_specs=tile_specs(lambda b, j: b * tiles_per_seq + j)
                  + [pl.BlockSpec((1, WINDOW, 2 * D_KV), lambda b, j: (b, 0, 0))],
        out_shape=layer_out_shapes + (jax.ShapeDtypeStruct((batch, WINDOW, 2 * D_KV), F32),),
        scratch_shapes=[pltpu.VMEM((T + WINDOW, 2 * D_KV), BF16),
                        pltpu.VMEM((T, D_ATT), BF16),
                        pltpu.VMEM((T, D_GMLP), BF16)],
        compiler_params=pltpu.CompilerParams(dimension_semantics=("arbitrary", "arbitrary"),
                                             vmem_limit_bytes=VMEM_LIMIT),
        name="layer_prompt",
    )(x_prompt, *prompt_consts)

    xs_flat = x_sample.reshape(n_sample, D_MODEL)
    ck = cache_win_k[0].reshape(dec_batch, cache_rows, D_KV)
    cv = cache_win_v[0].reshape(dec_batch, cache_rows, D_KV)
    sample_consts = (g1,) + shared_consts + (bias_s,) + tail_consts_s
    any_spec = pl.BlockSpec(memory_space=pl.ANY)
    x1, h, route, routet, cnt, kvnew, vn_s = pl.pallas_call(
        _sample_layer_kernel,
        grid=(n_stiles,),
        in_specs=[any_spec] * 5
                 + [pl.BlockSpec((T, D_MODEL), lambda i: (i, 0)),
                    pl.BlockSpec((seqs_per_tile, cache_rows, D_KV), lambda i: (i, 0, 0)),
                    pl.BlockSpec((seqs_per_tile, cache_rows, D_KV), lambda i: (i, 0, 0))]
                 + [_const_spec(c.shape) for c in sample_consts],
        out_specs=tile_specs(lambda i: n_ptiles + i)
                  + [pl.BlockSpec((T, 2 * D_KV), lambda i: (i, 0)),
                     pl.BlockSpec((T, D_GMLP), lambda i: (i, 0))],
        out_shape=layer_out_shapes + (jax.ShapeDtypeStruct((n_sample, 2 * D_KV), F32),
                                      jax.ShapeDtypeStruct((n_sample, D_GMLP), F32)),
        scratch_shapes=[pltpu.VMEM((T, D_ATT), BF16), pltpu.VMEM((T, D_GMLP), BF16)],
        input_output_aliases={0: 0, 1: 1, 2: 2, 3: 3, 4: 4},
        compiler_params=pltpu.CompilerParams(dimension_semantics=("arbitrary",),
                                             vmem_limit_bytes=VMEM_LIMIT),
        name="layer_sample",
    )(x1, h, route, routet, cnt, xs_flat, ck, cv, *sample_consts)

    blk_gran = ROW_BLOCK // GRAN
    counts = cnt[:, :, 0].astype(I32)
    seg_gran = (counts + (GRAN - 1)) // GRAN
    local_off = jnp.cumsum(seg_gran, axis=1) - seg_gran
    tot_gran = jnp.sum(seg_gran, axis=0)
    ptot_gran = (tot_gran + (blk_gran - 1)) // blk_gran * blk_gran
    pend_gran = jnp.cumsum(ptot_gran)
    gstart = pend_gran - ptot_gran
    global_off = gstart[None, :] + jnp.cumsum(seg_gran, axis=0) - seg_gran
    pad_n = ptot_gran - tot_gran
    pad_off = gstart + tot_gran
    n_rows = -(-(TOP_K * n_tok + n_tiles * N_EXPERTS * (GRAN - 1) + N_EXPERTS * (ROW_BLOCK - GRAN))
               // ROW_BLOCK) * ROW_BLOCK
    blk0 = (gstart // blk_gran).astype(I32)
    nblk = (ptot_gran // blk_gran).astype(I32)
    tile_gran = jnp.sum(seg_gran, axis=1).astype(I32)
    seg_gran_f = seg_gran.reshape(-1).astype(I32)
    local_off_f = local_off.reshape(-1).astype(I32)
    global_off_f = global_off.reshape(-1).astype(I32)

    xs = pl.pallas_call(
        _dispatch_kernel,
        grid_spec=pltpu.PrefetchScalarGridSpec(
            num_scalar_prefetch=6,
            grid=(n_tiles,),
            in_specs=[pl.BlockSpec((1, ROUTE_ROWS, T), lambda i, *_: (i, 0, 0)),
                      pl.BlockSpec((T, D_MODEL), lambda i, *_: (i, 0))],
            out_specs=pl.BlockSpec(memory_space=pl.ANY),
            scratch_shapes=[pltpu.VMEM((2, LOCAL_ROWS, D_MODEL), BF16),
                            pltpu.VMEM((ROW_BLOCK, D_MODEL), BF16),
                            pltpu.SemaphoreType.DMA((3,))]),
        out_shape=jax.ShapeDtypeStruct((n_rows, D_MODEL), BF16),
        compiler_params=pltpu.CompilerParams(dimension_semantics=("arbitrary",),
                                             vmem_limit_bytes=VMEM_LIMIT),
        name="moe_dispatch",
    )(seg_gran_f, local_off_f, global_off_f, tile_gran, pad_n.astype(I32), pad_off.astype(I32), route, h)

    ys = pl.pallas_call(
        _expert_kernel,
        grid_spec=pltpu.PrefetchScalarGridSpec(
            num_scalar_prefetch=2,
            grid=(N_EXPERTS,),
            in_specs=[pl.BlockSpec(memory_space=pl.ANY),
                      pl.BlockSpec((1, D_MODEL, 2 * D_FF), lambda e, *_: (e, 0, 0)),
                      pl.BlockSpec((1, 1, 2 * D_FF), lambda e, *_: (e, 0, 0)),
                      pl.BlockSpec((1, D_FF, D_MODEL), lambda e, *_: (e, 0, 0)),
                      pl.BlockSpec((1, 1, D_MODEL), lambda e, *_: (e, 0, 0))],
            out_specs=pl.BlockSpec(memory_space=pl.ANY),
            scratch_shapes=[pltpu.VMEM((D_MODEL, 2 * D_FF), BF16), pltpu.VMEM((D_FF, D_MODEL), BF16),
                            pltpu.VMEM((2, ROW_BLOCK, D_MODEL), BF16), pltpu.VMEM((2, ROW_BLOCK, D_MODEL), BF16),
                            pltpu.SemaphoreType.DMA((2,)), pltpu.SemaphoreType.DMA((2,))]),
        out_shape=jax.ShapeDtypeStruct((n_rows, D_MODEL), BF16),
        compiler_params=pltpu.CompilerParams(dimension_semantics=("arbitrary",),
                                             vmem_limit_bytes=VMEM_LIMIT),
        name="moe_experts",
    )(blk0, nblk, xs, w_exp_in[0], b_exp_in[0].reshape(N_EXPERTS, 1, 2 * D_FF),
      w_exp_out[0], b_exp_out[0].reshape(N_EXPERTS, 1, D_MODEL))

    y_p, y_s = pl.pallas_call(
        functools.partial(_combine_kernel, n_ptiles),
        grid_spec=pltpu.PrefetchScalarGridSpec(
            num_scalar_prefetch=4,
            grid=(n_tiles,),
            in_specs=[pl.BlockSpec((T, LANES), lambda i, *_: (i, 0)),
                      pl.BlockSpec((T, D_MODEL), lambda i, *_: (i, 0)),
                      pl.BlockSpec((1, D_MODEL), lambda i, *_: (0, 0)),
                      pl.BlockSpec(memory_space=pl.ANY)],
            out_specs=[pl.BlockSpec((T, D_MODEL), lambda i, *_: (jnp.minimum(i, n_ptiles - 1), 0)),
                       pl.BlockSpec((T, D_MODEL), lambda i, *_: (jnp.maximum(i - n_ptiles, 0), 0))],
            scratch_shapes=[pltpu.VMEM((2, LOCAL_ROWS, D_MODEL), BF16), pltpu.SemaphoreType.DMA((2,))]),
        out_shape=(jax.ShapeDtypeStruct((n_prompt, D_MODEL), F32),
                   jax.ShapeDtypeStruct((n_sample, D_MODEL), F32)),
        compiler_params=pltpu.CompilerParams(dimension_semantics=("arbitrary",),
                                             vmem_limit_bytes=VMEM_LIMIT),
        name="moe_combine",
    )(seg_gran_f, local_off_f, global_off_f, tile_gran, routet, x1, gf, ys)

    y_prompt = y_p.reshape(batch, seq, D_MODEL)
    y_sample = y_s.reshape(dec_batch, dec_seq, D_MODEL)
    new_win_k_prompt = kvwin[:, :, :D_KV].reshape(1, batch, WINDOW, N_KV, HEAD_DIM)
    new_win_v_prompt = kvwin[:, :, D_KV:].reshape(1, batch, WINDOW, N_KV, HEAD_DIM)
    new_win_k_sample = kvnew[:, :D_KV].reshape(1, dec_batch, dec_seq, N_KV, HEAD_DIM)
    new_win_v_sample = kvnew[:, D_KV:].reshape(1, dec_batch, dec_seq, N_KV, HEAD_DIM)
    new_sgu_v_sample = vn_s.reshape(1, dec_batch, dec_seq, D_GMLP)
    return (y_prompt, y_sample, new_win_k_prompt, new_win_v_prompt, new_win_k_sample,
            new_win_v_sample, new_sgu_v_sample)
```

```python
import functools
import math

import numpy as np
import jax
import jax.numpy as jnp
from jax import lax
from jax.experimental import pallas as pl
from jax.experimental.pallas import tpu as pltpu

F32 = jnp.float32
BF16 = jnp.bfloat16
I32 = jnp.int32

D_MODEL = 1024
HEAD_DIM = 64
N_HEADS = 16
N_KV = 2
REP = N_HEADS // N_KV
CHUNK = 64
WINDOW = 128
BAND = WINDOW + CHUNK
KEY_PAD = 256
D_ATT = N_HEADS * HEAD_DIM
D_KV = N_KV * HEAD_DIM
NUM_BUCKETS = 32
MAX_DISTANCE = 128
GMLP_CHUNK = 128
D_GMLP = 1024
N_GROUPS = 4
GROUP_W = D_GMLP // N_GROUPS
N_EXPERTS = 32
TOP_K = 4
D_FF = 1024
SWIGLU_LIMIT = 7.0
SWIGLU_ALPHA = 1.702
NORM_EPS = 1e-5
D_IN = D_ATT + 2 * D_KV + 2 * D_GMLP + 2 * D_MODEL
D_REST = D_IN - D_ATT
COL_KV = 0
COL_U = COL_KV + 2 * D_KV
COL_VG = COL_U + D_GMLP
COL_GA = COL_VG + D_GMLP
COL_GB = COL_GA + D_MODEL
SQRT_HALF = float(np.sqrt(0.5))

LANES = 128
BF16_SUBLANES = 16
VMEM_LIMIT = 56 * 1024 * 1024

TOK_TILE = 256
GRAN = BF16_SUBLANES
SEG_BITS = (TOK_TILE // GRAN).bit_length()
LOCAL_ROWS = TOP_K * TOK_TILE + N_EXPERTS * GRAN
ROW_BLOCK = 512
PAD_BITS = (ROW_BLOCK // GRAN - 1).bit_length()
TILE_BITS = (LOCAL_ROWS // GRAN).bit_length()
BLOCK_DMA_PRIORITY = 1
ROUTE_ROWS = 16


def _const_spec(shape):
    nd = len(shape)
    return pl.BlockSpec(shape, lambda *_: (0,) * nd, pipeline_mode=pl.Buffered(1))


def _rms_norm(x, g):
    ms = jnp.mean(x * x, axis=-1, keepdims=True)
    return x * lax.rsqrt(ms + NORM_EPS) * g


def _gelu(x):
    return 0.5 * x * (1.0 + lax.erf(x * SQRT_HALF))


def _attend(q, k, v, bias, valid):
    logits = lax.dot_general(q, k, (((1,), (1,)), ((), ())), preferred_element_type=F32) + bias
    if valid is not None:
        logits = jnp.where(valid, logits, -jnp.inf)
    m = jnp.max(logits, axis=-1, keepdims=True)
    p = jnp.exp(logits - m)
    den = jnp.sum(p, axis=-1, keepdims=True)
    w = (p * (1.0 / den)).astype(BF16)
    return jnp.dot(w, v, preferred_element_type=F32)


def _attention_rows(zq, row0, n_rows, k_ext, v_ext, bias_ref, valid, oatt_ref, heads_per_unit):
    lane = lax.broadcasted_iota(I32, (1, LANES), 1)
    for g in range(N_KV):
        in_group = (lane >= g * HEAD_DIM) & (lane < (g + 1) * HEAD_DIM)
        qmask = jnp.where(in_group, HEAD_DIM ** -0.5, 0.0).astype(BF16)
        for part in range(REP // heads_per_unit):
            tiles = range(part * heads_per_unit, (part + 1) * heads_per_unit)
            q = jnp.concatenate([zq[row0:row0 + n_rows, t * LANES:(t + 1) * LANES] * qmask for t in tiles],
                                axis=0)
            b0 = part * heads_per_unit * n_rows
            o = _attend(q, k_ext, v_ext, bias_ref[g, b0:b0 + heads_per_unit * n_rows, :], valid)
            for u, t in enumerate(tiles):
                c0 = t * LANES + g * HEAD_DIM
                oatt_ref[row0:row0 + n_rows, c0:c0 + HEAD_DIM] = (
                    o[u * n_rows:(u + 1) * n_rows, g * HEAD_DIM:(g + 1) * HEAD_DIM].astype(BF16))


def _layer_tail(x, xn_b, oatt_ref, osgu_ref, w_in_ref, sguw_ref, sgub_ref, lng_ref, lnb_ref,
                woa_ref, wos_ref, wout_ref, g2_ref, wr_ref, br_ref, tri_ref, low_ref,
                x1_ref, h_ref, route_ref, routet_ref, cnt_ref, vn_ref):
    T = x.shape[0]
    u = _gelu(jnp.dot(xn_b, w_in_ref[:, COL_U:COL_VG], preferred_element_type=F32))
    vg = _gelu(jnp.dot(xn_b, w_in_ref[:, COL_VG:COL_GA], preferred_element_type=F32))
    mu = jnp.mean(vg, axis=-1, keepdims=True)
    var = jnp.mean(jnp.square(vg - mu), axis=-1, keepdims=True)
    vn = (vg - mu) * lax.rsqrt(var + NORM_EPS) * lng_ref[...] + lnb_ref[...]
    if vn_ref is not None:
        vn_ref[...] = vn
    vn_b = vn.astype(BF16)
    for j in range(T // GMLP_CHUNK):
        rows = slice(j * GMLP_CHUNK, (j + 1) * GMLP_CHUNK)
        for g in range(N_GROUPS):
            cols = slice(g * GROUP_W, (g + 1) * GROUP_W)
            mixed = jnp.dot(sguw_ref[g], vn_b[rows, cols], preferred_element_type=F32) + sgub_ref[g]
            osgu_ref[rows, cols] = (u[rows, cols] * mixed).astype(BF16)

    ga = jax.nn.sigmoid(jnp.dot(xn_b, w_in_ref[:, COL_GA:COL_GB], preferred_element_type=F32))
    merged = ga * jnp.dot(oatt_ref[...], woa_ref[...], preferred_element_type=F32)
    gb = jax.nn.sigmoid(jnp.dot(xn_b, w_in_ref[:, COL_GB:D_REST], preferred_element_type=F32))
    merged = merged + gb * jnp.dot(osgu_ref[...], wos_ref[...], preferred_element_type=F32)
    x1 = x + jnp.dot(merged.astype(BF16), wout_ref[...], preferred_element_type=F32)
    x1_ref[...] = x1

    h_b = _rms_norm(x1, g2_ref[...]).astype(BF16)
    h_ref[...] = h_b
    logits = jnp.dot(h_b, wr_ref[...], preferred_element_type=F32)
    lt = jnp.transpose(logits)[:N_EXPERTS, :] + br_ref[...]
    e_iota = lax.broadcasted_iota(I32, (N_EXPERTS, T), 0).astype(F32)
    cur = lt
    vals, idxs = [], []
    for _ in range(TOP_K):
        m = jnp.max(cur, axis=0, keepdims=True)
        ik = jnp.min(jnp.where(cur == m, e_iota, float(N_EXPERTS)), axis=0, keepdims=True)
        vals.append(m)
        idxs.append(ik)
        cur = jnp.where(e_iota == ik, -jnp.inf, cur)
    exps = [jnp.exp(v - vals[0]) for v in vals]
    den = exps[0] + exps[1] + exps[2] + exps[3]
    gates = [e / den for e in exps]

    onehot = jnp.zeros((N_EXPERTS, T), F32)
    for ik in idxs:
        onehot = onehot + jnp.where(e_iota == ik, 1.0, 0.0)
    rank = jnp.dot(onehot.astype(BF16), tri_ref[...], preferred_element_type=F32)
    cnt = jnp.sum(onehot, axis=1, keepdims=True)
    cnt_b = jnp.broadcast_to(cnt, (N_EXPERTS, LANES))
    gran = jnp.floor((cnt_b + (GRAN - 1)) * (1.0 / GRAN))
    off = jnp.dot(low_ref[...], gran.astype(BF16), preferred_element_type=F32) * GRAN
    base = off[:, 0:1] + rank
    poss = [jnp.sum(jnp.where(e_iota == ik, base, 0.0), axis=0, keepdims=True) for ik in idxs]

    rec = jnp.concatenate(idxs + poss + gates
                          + [jnp.zeros((ROUTE_ROWS - 3 * TOP_K, T), F32)], axis=0)
    route_ref[0] = rec
    rec_pad = jnp.concatenate([rec, jnp.zeros((LANES - ROUTE_ROWS, T), F32)], axis=0)
    routet_ref[...] = jnp.transpose(rec_pad)
    cnt_ref[0] = cnt_b


def _prompt_layer_kernel(x_ref, g1_ref, wq_ref, w_in_ref, bias_ref, sguw_ref, sgub_ref, lng_ref,
                         lnb_ref, woa_ref, wos_ref, wout_ref, g2_ref, wr_ref, br_ref, tri_ref, low_ref,
                         x1_ref, h_ref, route_ref, routet_ref, cnt_ref, kvwin_ref,
                         kvx_ref, oatt_ref, osgu_ref):
    T = TOK_TILE
    j = pl.program_id(1)
    x = x_ref[0]
    xn_b = _rms_norm(x, g1_ref[...]).astype(BF16)
    zq = jnp.dot(xn_b, wq_ref[...], preferred_element_type=F32).astype(BF16)
    zkv = jnp.dot(xn_b, w_in_ref[:, COL_KV:COL_U], preferred_element_type=F32)

    @pl.when(j == pl.num_programs(1) - 1)
    def _():
        kvwin_ref[0] = zkv[T - WINDOW:, :]

    @pl.when(j == 0)
    def _():
        kvx_ref[0:WINDOW, :] = jnp.zeros((WINDOW, 2 * D_KV), BF16)

    @pl.when(j > 0)
    def _():
        kvx_ref[0:WINDOW, :] = kvx_ref[T:T + WINDOW, :]

    kvx_ref[WINDOW:, :] = zkv.astype(BF16)

    col = lax.broadcasted_iota(I32, (1, KEY_PAD), 1)
    key_pad = jnp.zeros((KEY_PAD - BAND, 2 * D_KV), BF16)
    for c in range(T // CHUNK):
        r0 = c * CHUNK
        kvb = jnp.concatenate([kvx_ref[r0:r0 + BAND, :], key_pad], axis=0)
        valid = ((col + (j * T + r0 - WINDOW)) >= 0) | (col >= BAND) if r0 < WINDOW else None
        _attention_rows(zq, r0, CHUNK, kvb[:, :D_KV], kvb[:, D_KV:], bias_ref, valid, oatt_ref,
                        heads_per_unit=4)

    _layer_tail(x, xn_b, oatt_ref, osgu_ref, w_in_ref, sguw_ref, sgub_ref, lng_ref, lnb_ref,
                woa_ref, wos_ref, wout_ref, g2_ref, wr_ref, br_ref, tri_ref, low_ref,
                x1_ref, h_ref, route_ref, routet_ref, cnt_ref, None)


def _sample_layer_kernel(x1_in, h_in, route_in, routet_in, cnt_in,
                         x_ref, ck_ref, cv_ref, g1_ref, wq_ref, w_in_ref, bias_ref, sguw_ref, sgub_ref,
                         lng_ref, lnb_ref, woa_ref, wos_ref, wout_ref, g2_ref, wr_ref, br_ref, tri_ref,
                         low_ref,
                         x1_ref, h_ref, route_ref, routet_ref, cnt_ref, kvnew_ref, vn_ref,
                         oatt_ref, osgu_ref):
    del x1_in, h_in, route_in, routet_in, cnt_in
    T = TOK_TILE
    nq = x_ref.shape[0] // ck_ref.shape[0]
    x = x_ref[...]
    xn_b = _rms_norm(x, g1_ref[...]).astype(BF16)
    zq = jnp.dot(xn_b, wq_ref[...], preferred_element_type=F32).astype(BF16)
    zkv = jnp.dot(xn_b, w_in_ref[:, COL_KV:COL_U], preferred_element_type=F32)
    kvnew_ref[...] = zkv
    zkv_b = zkv.astype(BF16)
    n_cache = ck_ref.shape[1]
    key_pad = jnp.zeros((KEY_PAD - n_cache - nq, D_KV), BF16)
    for b in range(T // nq):
        r0 = b * nq
        kk = jnp.concatenate([ck_ref[b].astype(BF16), zkv_b[r0:r0 + nq, :D_KV], key_pad], axis=0)
        vv = jnp.concatenate([cv_ref[b].astype(BF16), zkv_b[r0:r0 + nq, D_KV:], key_pad], axis=0)
        _attention_rows(zq, r0, nq, kk, vv, bias_ref, None, oatt_ref, heads_per_unit=REP)

    _layer_tail(x, xn_b, oatt_ref, osgu_ref, w_in_ref, sguw_ref, sgub_ref, lng_ref, lnb_ref,
                woa_ref, wos_ref, wout_ref, g2_ref, wr_ref, br_ref, tri_ref, low_ref,
                x1_ref, h_ref, route_ref, routet_ref, cnt_ref, vn_ref)


def _segment_copies(n_gran, src_row, dst_row, bits, make_copy, act):
    for b in range(min(bits, 2)):
        rows = GRAN << b

        @pl.when(((n_gran >> b) & 1) == 1)
        def _(b=b, rows=rows):
            done = (n_gran & ((1 << b) - 1)) * GRAN
            s = pl.multiple_of(src_row + done, GRAN)
            d = pl.multiple_of(dst_row + done, GRAN)
            act(make_copy(s, d, rows))

    if bits > 2:
        def quad(q, carry):
            done = ((n_gran & 3) + 4 * q) * GRAN
            s = pl.multiple_of(src_row + done, GRAN)
            d = pl.multiple_of(dst_row + done, GRAN)
            act(make_copy(s, d, 4 * GRAN))
            return carry
        lax.fori_loop(0, n_gran >> 2, quad, 0)


def _wait_granules(n_gran, bits, make_copy):
    for b in range(bits):
        @pl.when(((n_gran >> b) & 1) == 1)
        def _(b=b):
            make_copy(GRAN << b).wait()


def _dispatch_kernel(cnt_ref, loff_ref, goff_ref, tot_ref, padn_ref, padoff_ref,
                     route_ref, h_ref, xs_hbm, stage_ref, zero_ref, sem):
    i = pl.program_id(0)
    last = pl.num_programs(0) - 1
    slot = lax.rem(i, 2)
    T = TOK_TILE
    pos = route_ref[0, TOP_K:2 * TOP_K, :].astype(I32)
    r_iota = lax.broadcasted_iota(I32, (LOCAL_ROWS, T), 0)
    p = jnp.zeros((LOCAL_ROWS, T), F32)
    for k in range(TOP_K):
        p = jnp.where(r_iota == pos[k:k + 1, :], 1.0, p)
    stage_ref[slot] = jnp.dot(p.astype(BF16), h_ref[...], preferred_element_type=F32).astype(BF16)

    def seg_copy(s, d, rows):
        return pltpu.make_async_copy(stage_ref.at[slot, pl.ds(s, rows)], xs_hbm.at[pl.ds(d, rows)],
                                     sem.at[slot])

    def start_segment(e, carry):
        t = i * N_EXPERTS + e
        _segment_copies(cnt_ref[t], loff_ref[t] * GRAN, goff_ref[t] * GRAN, SEG_BITS, seg_copy,
                        lambda cp: cp.start())
        return carry

    lax.fori_loop(0, N_EXPERTS, start_segment, 0)

    def wait_tile(tile, tile_slot):
        _wait_granules(tot_ref[tile], TILE_BITS, lambda rows: pltpu.make_async_copy(
            stage_ref.at[tile_slot, pl.ds(0, rows)], xs_hbm.at[pl.ds(0, rows)], sem.at[tile_slot]))

    @pl.when(i > 0)
    def _():
        wait_tile(i - 1, 1 - slot)

    @pl.when(i == last)
    def _():
        wait_tile(i, slot)
        zero_ref[...] = jnp.zeros(zero_ref.shape, BF16)

        def pad_copy(s, d, rows):
            return pltpu.make_async_copy(zero_ref.at[pl.ds(s, rows)], xs_hbm.at[pl.ds(d, rows)], sem.at[2])

        def for_pads(act):
            def body(e, carry):
                _segment_copies(padn_ref[e], 0, padoff_ref[e] * GRAN, PAD_BITS, pad_copy, act)
                return carry
            lax.fori_loop(0, N_EXPERTS, body, 0)

        for_pads(lambda cp: cp.start())
        for_pads(lambda cp: cp.wait())


def _expert_kernel(blk0_ref, nblk_ref, xs_hbm, w1_ref, b1_ref, w2_ref, b2_ref, ys_hbm,
                   w1b_ref, w2b_ref, xbuf_ref, ybuf_ref, sem_in, sem_out):
    e = pl.program_id(0)
    nb = nblk_ref[e]
    row0 = blk0_ref[e] * ROW_BLOCK

    def in_copy(k, slot):
        r = pl.multiple_of(row0 + k * ROW_BLOCK, ROW_BLOCK)
        return pltpu.make_async_copy(xs_hbm.at[pl.ds(r, ROW_BLOCK)], xbuf_ref.at[slot], sem_in.at[slot])

    def out_copy(k, slot):
        r = pl.multiple_of(row0 + k * ROW_BLOCK, ROW_BLOCK)
        return pltpu.make_async_copy(ybuf_ref.at[slot], ys_hbm.at[pl.ds(r, ROW_BLOCK)], sem_out.at[slot])

    @pl.when(nb > 0)
    def _():
        in_copy(0, 0).start(priority=BLOCK_DMA_PRIORITY)

    w1b_ref[...] = w1_ref[0].astype(BF16)
    w2b_ref[...] = w2_ref[0].astype(BF16)

    def block(k, carry):
        slot = lax.rem(k, 2)

        @pl.when(k + 1 < nb)
        def _():
            in_copy(k + 1, 1 - slot).start(priority=BLOCK_DMA_PRIORITY)

        in_copy(k, slot).wait()

        @pl.when(k >= 2)
        def _():
            out_copy(k - 2, slot).wait()

        h1 = jnp.dot(xbuf_ref[slot], w1b_ref[...], preferred_element_type=F32) + b1_ref[0]
        gate = jnp.minimum(h1[:, :D_FF], SWIGLU_LIMIT)
        up = jnp.clip(h1[:, D_FF:], -SWIGLU_LIMIT, SWIGLU_LIMIT)
        act = gate * jax.nn.sigmoid(SWIGLU_ALPHA * gate) * (up + 1.0)
        y = jnp.dot(act.astype(BF16), w2b_ref[...], preferred_element_type=F32) + b2_ref[0]
        ybuf_ref[slot] = y.astype(BF16)
        out_copy(k, slot).start(priority=BLOCK_DMA_PRIORITY)
        return carry

    lax.fori_loop(0, nb, block, 0)

    @pl.when(nb >= 2)
    def _():
        out_copy(nb - 2, lax.rem(nb, 2)).wait()

    @pl.when(nb >= 1)
    def _():
        out_copy(nb - 1, lax.rem(nb + 1, 2)).wait()


def _combine_kernel(n_prompt_tiles, cnt_ref, loff_ref, goff_ref, tot_ref,
                    routet_ref, x1_ref, gf_ref, ys_hbm, yp_ref, ysm_ref, stage_ref, sem):
    i = pl.program_id(0)
    slot = lax.rem(i, 2)
    T = TOK_TILE

    def start_tile(tile, tile_slot):
        def seg_copy(s, d, rows):
            return pltpu.make_async_copy(ys_hbm.at[pl.ds(d, rows)], stage_ref.at[tile_slot, pl.ds(s, rows)],
                                         sem.at[tile_slot])

        def body(e, carry):
            t = tile * N_EXPERTS + e
            _segment_copies(cnt_ref[t], loff_ref[t] * GRAN, goff_ref[t] * GRAN, SEG_BITS, seg_copy,
                            lambda cp: cp.start())
            return carry
        lax.fori_loop(0, N_EXPERTS, body, 0)

    @pl.when(i == 0)
    def _():
        stage_ref[...] = jnp.zeros(stage_ref.shape, BF16)
        start_tile(i, slot)

    @pl.when(i + 1 < pl.num_programs(0))
    def _():
        start_tile(i + 1, 1 - slot)

    rt = routet_ref[...]
    l_iota = lax.broadcasted_iota(I32, (T, LOCAL_ROWS), 1)
    pw = jnp.zeros((T, LOCAL_ROWS), F32)
    for k in range(TOP_K):
        pos_k = rt[:, TOP_K + k:TOP_K + k + 1].astype(I32)
        pw = jnp.where(l_iota == pos_k, rt[:, 2 * TOP_K + k:2 * TOP_K + k + 1], pw)

    _wait_granules(tot_ref[i], TILE_BITS, lambda rows: pltpu.make_async_copy(
        ys_hbm.at[pl.ds(0, rows)], stage_ref.at[slot, pl.ds(0, rows)], sem.at[slot]))
    moe = jnp.dot(pw.astype(BF16), stage_ref[slot], preferred_element_type=F32)
    y = _rms_norm(x1_ref[...] + moe, gf_ref[...])

    @pl.when(i < n_prompt_tiles)
    def _():
        yp_ref[...] = y

    @pl.when(i >= n_prompt_tiles)
    def _():
        ysm_ref[...] = y


def _t5_bucket(rel):
    half = NUM_BUCKETS // 2
    max_exact = half // 2
    ret = jnp.where(rel > 0, half, 0)
    n = jnp.abs(rel)
    nf = jnp.maximum(n, 1).astype(F32)
    large = max_exact + (jnp.log(nf / max_exact) / math.log(MAX_DISTANCE / max_exact)
                         * (half - max_exact)).astype(jnp.int32)
    large = jnp.minimum(large, half - 1)
    return ret + jnp.where(n < max_exact, n, large)


def _stacked_bias(table, sinks, q_pos, k_pos):
    nq, nk = q_pos.shape[0], k_pos.shape[0]
    bucket = _t5_bucket(k_pos[None, :] - q_pos[:, None])
    onehot = (bucket[:, :, None] == jnp.arange(NUM_BUCKETS)).astype(F32)
    bias = jnp.einsum('qkb,bh->hqk', onehot, table.astype(F32), precision=lax.Precision.HIGHEST)
    bias = bias.reshape(N_KV, REP * nq, nk)
    sink = jnp.repeat(sinks.astype(F32).reshape(N_KV, REP, 1), nq, axis=2).reshape(N_KV, REP * nq, 1)
    tail = jnp.full((N_KV, REP * nq, KEY_PAD - nk - 1), -jnp.inf, F32)
    return jnp.concatenate([bias, sink, tail], axis=-1)


def kernel(x_prompt, x_sample, cache_win_k, cache_win_v, norm1_g, w_in, attn_sinks, rel_bias_table, sgu_ln_g, sgu_ln_b, sgu_w, sgu_b, w_o_att, w_o_sgu, w_out, norm2_g, w_router, b_router, w_exp_in, b_exp_in, w_exp_out, b_exp_out, final_norm_g):
    batch, seq, _ = x_prompt.shape
    dec_batch, dec_seq, _ = x_sample.shape
    cache_rows = cache_win_k.shape[2]
    assert x_prompt.shape[2] == D_MODEL and w_in.shape == (1, D_MODEL, D_IN)
    assert seq % TOK_TILE == 0 and TOK_TILE % GMLP_CHUNK == 0 and TOK_TILE >= WINDOW
    assert TOK_TILE % dec_seq == 0 and (dec_batch * dec_seq) % TOK_TILE == 0
    assert dec_seq <= GMLP_CHUNK and GMLP_CHUNK % dec_seq == 0 and cache_rows == WINDOW
    T = TOK_TILE
    n_prompt = batch * seq
    n_sample = dec_batch * dec_seq
    n_tok = n_prompt + n_sample
    tiles_per_seq = seq // T
    n_ptiles = n_prompt // T
    n_stiles = n_sample // T
    n_tiles = n_ptiles + n_stiles
    seqs_per_tile = T // dec_seq

    w_q = w_in[0][:, :D_ATT].reshape(D_MODEL, N_KV, REP, HEAD_DIM).transpose(0, 2, 1, 3).reshape(D_MODEL, D_ATT)
    w_q_b = w_q.astype(BF16)
    w_in_b = w_in[0][:, D_ATT:].astype(BF16)
    woa_b = w_o_att[0].reshape(N_KV, REP, HEAD_DIM, D_MODEL).transpose(1, 0, 2, 3).reshape(
        D_ATT, D_MODEL).astype(BF16)
    wos_b = w_o_sgu[0].astype(BF16)
    wout_b = w_out[0].astype(BF16)
    wr_b = jnp.pad(w_router[0], ((0, 0), (0, LANES - N_EXPERTS))).astype(BF16)
    br_col = b_router[0].astype(F32).reshape(N_EXPERTS, 1)
    g1 = norm1_g[0].reshape(1, D_MODEL)
    g2 = norm2_g[0].reshape(1, D_MODEL)
    gf = final_norm_g.reshape(1, D_MODEL)
    lng = sgu_ln_g[0].reshape(1, D_GMLP)
    lnb = sgu_ln_b[0].reshape(1, D_GMLP)
    tril = jnp.tril(jnp.ones((GMLP_CHUNK, GMLP_CHUNK), dtype=bool))
    sguw_p = jnp.where(tril[None], sgu_w[0], 0).astype(BF16)
    sgub_p = jnp.broadcast_to(sgu_b[0][:, :, None], (N_GROUPS, GMLP_CHUNK, GROUP_W)).astype(F32)
    reps = GMLP_CHUNK // dec_seq
    corner = jnp.where(tril[None, :dec_seq, :dec_seq], sgu_w[0][:, :dec_seq, :dec_seq], 0)
    sguw_s = jnp.einsum('ab,gij->gaibj', jnp.eye(reps, dtype=F32), corner).reshape(
        N_GROUPS, GMLP_CHUNK, GMLP_CHUNK).astype(BF16)
    sgub_s = jnp.broadcast_to(jnp.tile(sgu_b[0][:, :dec_seq], (1, reps))[:, :, None],
                              (N_GROUPS, GMLP_CHUNK, GROUP_W)).astype(F32)
    bias_p = _stacked_bias(rel_bias_table, attn_sinks[0], jnp.arange(CHUNK) + WINDOW, jnp.arange(BAND))
    bias_s = _stacked_bias(rel_bias_table, attn_sinks[0], cache_rows + jnp.arange(dec_seq),
                           jnp.arange(cache_rows + dec_seq))
    tri = jnp.triu(jnp.ones((T, T), F32), k=1).astype(BF16)
    low = jnp.tril(jnp.ones((N_EXPERTS, N_EXPERTS), F32), k=-1).astype(BF16)

    layer_out_shapes = (
        jax.ShapeDtypeStruct((n_tok, D_MODEL), F32),
        jax.ShapeDtypeStruct((n_tok, D_MODEL), BF16),
        jax.ShapeDtypeStruct((n_tiles, ROUTE_ROWS, T), F32),
        jax.ShapeDtypeStruct((n_tok, LANES), F32),
        jax.ShapeDtypeStruct((n_tiles, N_EXPERTS, LANES), F32),
    )
    shared_consts = (w_q_b, w_in_b)
    tail_consts_p = (sguw_p, sgub_p, lng, lnb, woa_b, wos_b, wout_b, g2, wr_b, br_col, tri, low)
    tail_consts_s = (sguw_s, sgub_s, lng, lnb, woa_b, wos_b, wout_b, g2, wr_b, br_col, tri, low)

    def tile_specs(tile_of):
        return [
            pl.BlockSpec((T, D_MODEL), lambda *g: (tile_of(*g), 0)),
            pl.BlockSpec((T, D_MODEL), lambda *g: (tile_of(*g), 0)),
            pl.BlockSpec((1, ROUTE_ROWS, T), lambda *g: (tile_of(*g), 0, 0)),
            pl.BlockSpec((T, LANES), lambda *g: (tile_of(*g), 0)),
            pl.BlockSpec((1, N_EXPERTS, LANES), lambda *g: (tile_of(*g), 0, 0)),
        ]

    prompt_consts = (g1,) + shared_consts + (bias_p,) + tail_consts_p
    x1, h, route, routet, cnt, kvwin = pl.pallas_call(
        _prompt_layer_kernel,
        grid=(batch, tiles_per_seq),
        in_specs=[pl.BlockSpec((1, T, D_MODEL), lambda b, j: (b, j, 0))]
                 + [_const_spec(c.shape) for c in prompt_consts],
        out_specs=tile_specs(lambda b, j: b * tiles_per_seq + j)
                  + [pl.BlockSpec((1, WINDOW, 2 * D_KV), lambda b, j: (b, 0, 0))],
        out_shape=layer_out_shapes + (jax.ShapeDtypeStruct((batch, WINDOW, 2 * D_KV), F32),),
        scratch_shapes=[pltpu.VMEM((T + WINDOW, 2 * D_KV), BF16),
                        pltpu.VMEM((T, D_ATT), BF16),
                        pltpu.VMEM((T, D_GMLP), BF16)],
        compiler_params=pltpu.CompilerParams(dimension_semantics=("arbitrary", "arbitrary"),
                                             vmem_limit_bytes=VMEM_LIMIT),
        name="layer_prompt",
    )(x_prompt, *prompt_consts)

    xs_flat = x_sample.reshape(n_sample, D_MODEL)
    ck = cache_win_k[0].reshape(dec_batch, cache_rows, D_KV)
    cv = cache_win_v[0].reshape(dec_batch, cache_rows, D_KV)
    sample_consts = (g1,) + shared_consts + (bias_s,) + tail_consts_s
    any_spec = pl.BlockSpec(memory_space=pl.ANY)
    x1, h, route, routet, cnt, kvnew, vn_s = pl.pallas_call(
        _sample_layer_kernel,
        grid=(n_stiles,),
        in_specs=[any_spec] * 5
                 + [pl.BlockSpec((T, D_MODEL), lambda i: (i, 0)),
                    pl.BlockSpec((seqs_per_tile, cache_rows, D_KV), lambda i: (i, 0, 0)),
                    pl.BlockSpec((seqs_per_tile, cache_rows, D_KV), lambda i: (i, 0, 0))]
                 + [_const_spec(c.shape) for c in sample_consts],
        out_specs=tile_specs(lambda i: n_ptiles + i)
                  + [pl.BlockSpec((T, 2 * D_KV), lambda i: (i, 0)),
                     pl.BlockSpec((T, D_GMLP), lambda i: (i, 0))],
        out_shape=layer_out_shapes + (jax.ShapeDtypeStruct((n_sample, 2 * D_KV), F32),
                                      jax.ShapeDtypeStruct((n_sample, D_GMLP), F32)),
        scratch_shapes=[pltpu.VMEM((T, D_ATT), BF16), pltpu.VMEM((T, D_GMLP), BF16)],
        input_output_aliases={0: 0, 1: 1, 2: 2, 3: 3, 4: 4},
        compiler_params=pltpu.CompilerParams(dimension_semantics=("arbitrary",),
                                             vmem_limit_bytes=VMEM_LIMIT),
        name="layer_sample",
    )(x1, h, route, routet, cnt, xs_flat, ck, cv, *sample_consts)

    blk_gran = ROW_BLOCK // GRAN
    counts = cnt[:, :, 0].astype(I32)
    seg_gran = (counts + (GRAN - 1)) // GRAN
    local_off = jnp.cumsum(seg_gran, axis=1) - seg_gran
    tot_gran = jnp.sum(seg_gran, axis=0)
    ptot_gran = (tot_gran + (blk_gran - 1)) // blk_gran * blk_gran
    pend_gran = jnp.cumsum(ptot_gran)
    gstart = pend_gran - ptot_gran
    global_off = gstart[None, :] + jnp.cumsum(seg_gran, axis=0) - seg_gran
    pad_n = ptot_gran - tot_gran
    pad_off = gstart + tot_gran
    n_rows = -(-(TOP_K * n_tok + n_tiles * N_EXPERTS * (GRAN - 1) + N_EXPERTS * (ROW_BLOCK - GRAN))
               // ROW_BLOCK) * ROW_BLOCK
    blk0 = (gstart // blk_gran).astype(I32)
    nblk = (ptot_gran // blk_gran).astype(I32)
    tile_gran = jnp.sum(seg_gran, axis=1).astype(I32)
    seg_gran_f = seg_gran.reshape(-1).astype(I32)
    local_off_f = local_off.reshape(-1).astype(I32)
    global_off_f = global_off.reshape(-1).astype(I32)

    xs = pl.pallas_call(
        _dispatch_kernel,
        grid_spec=pltpu.PrefetchScalarGridSpec(
            num_scalar_prefetch=6,
            grid=(n_tiles,),
            in_specs=[pl.BlockSpec((1, ROUTE_ROWS, T), lambda i, *_: (i, 0, 0)),
                      pl.BlockSpec((T, D_MODEL), lambda i, *_: (i, 0))],
            out_specs=pl.BlockSpec(memory_space=pl.ANY),
            scratch_shapes=[pltpu.VMEM((2, LOCAL_ROWS, D_MODEL), BF16),
                            pltpu.VMEM((ROW_BLOCK, D_MODEL), BF16),
                            pltpu.SemaphoreType.DMA((3,))]),
        out_shape=jax.ShapeDtypeStruct((n_rows, D_MODEL), BF16),
        compiler_params=pltpu.CompilerParams(dimension_semantics=("arbitrary",),
                                             vmem_limit_bytes=VMEM_LIMIT),
        name="moe_dispatch",
    )(seg_gran_f, local_off_f, global_off_f, tile_gran, pad_n.astype(I32), pad_off.astype(I32), route, h)

    ys = pl.pallas_call(
        _expert_kernel,
        grid_spec=pltpu.PrefetchScalarGridSpec(
            num_scalar_prefetch=2,
            grid=(N_EXPERTS,),
            in_specs=[pl.BlockSpec(memory_space=pl.ANY),
                      pl.BlockSpec((1, D_MODEL, 2 * D_FF), lambda e, *_: (e, 0, 0)),
                      pl.BlockSpec((1, 1, 2 * D_FF), lambda e, *_: (e, 0, 0)),
                      pl.BlockSpec((1, D_FF, D_MODEL), lambda e, *_: (e, 0, 0)),
                      pl.BlockSpec((1, 1, D_MODEL), lambda e, *_: (e, 0, 0))],
            out_specs=pl.BlockSpec(memory_space=pl.ANY),
            scratch_shapes=[pltpu.VMEM((D_MODEL, 2 * D_FF), BF16), pltpu.VMEM((D_FF, D_MODEL), BF16),
                            pltpu.VMEM((2, ROW_BLOCK, D_MODEL), BF16), pltpu.VMEM((2, ROW_BLOCK, D_MODEL), BF16),
                            pltpu.SemaphoreType.DMA((2,)), pltpu.SemaphoreType.DMA((2,))]),
        out_shape=jax.ShapeDtypeStruct((n_rows, D_MODEL), BF16),
        compiler_params=pltpu.CompilerParams(dimension_semantics=("arbitrary",),
                                             vmem_limit_bytes=VMEM_LIMIT),
        name="moe_experts",
    )(blk0, nblk, xs, w_exp_in[0], b_exp_in[0].reshape(N_EXPERTS, 1, 2 * D_FF),
      w_exp_out[0], b_exp_out[0].reshape(N_EXPERTS, 1, D_MODEL))

    y_p, y_s = pl.pallas_call(
        functools.partial(_combine_kernel, n_ptiles),
        grid_spec=pltpu.PrefetchScalarGridSpec(
            num_scalar_prefetch=4,
            grid=(n_tiles,),
            in_specs=[pl.BlockSpec((T, LANES), lambda i, *_: (i, 0)),
                      pl.BlockSpec((T, D_MODEL), lambda i, *_: (i, 0)),
                      pl.BlockSpec((1, D_MODEL), lambda i, *_: (0, 0)),
                      pl.BlockSpec(memory_space=pl.ANY)],
            out_specs=[pl.BlockSpec((T, D_MODEL), lambda i, *_: (jnp.minimum(i, n_ptiles - 1), 0)),
                       pl.BlockSpec((T, D_MODEL), lambda i, *_: (jnp.maximum(i - n_ptiles, 0), 0))],
            scratch_shapes=[pltpu.VMEM((2, LOCAL_ROWS, D_MODEL), BF16), pltpu.SemaphoreType.DMA((2,))]),
        out_shape=(jax.ShapeDtypeStruct((n_prompt, D_MODEL), F32),
                   jax.ShapeDtypeStruct((n_sample, D_MODEL), F32)),
        compiler_params=pltpu.CompilerParams(dimension_semantics=("arbitrary",),
                                             vmem_limit_bytes=VMEM_LIMIT),
        name="moe_combine",
    )(seg_gran_f, local_off_f, global_off_f, tile_gran, routet, x1, gf, ys)

    y_prompt = y_p.reshape(batch, seq, D_MODEL)
    y_sample = y_s.reshape(dec_batch, dec_seq, D_MODEL)
    new_win_k_prompt = kvwin[:, :, :D_KV].reshape(1, batch, WINDOW, N_KV, HEAD_DIM)
    new_win_v_prompt = kvwin[:, :, D_KV:].reshape(1, batch, WINDOW, N_KV, HEAD_DIM)
    new_win_k_sample = kvnew[:, :D_KV].reshape(1, dec_batch, dec_seq, N_KV, HEAD_DIM)
    new_win_v_sample = kvnew[:, D_KV:].reshape(1, dec_batch, dec_seq, N_KV, HEAD_DIM)
    new_sgu_v_sample = vn_s.reshape(1, dec_batch, dec_seq, D_GMLP)
    return (y_prompt, y_sample, new_win_k_prompt, new_win_v_prompt, new_win_k_sample,
            new_win_v_sample, new_sgu_v_sample)
```

```python
import functools
import math

import numpy as np
import jax
import jax.numpy as jnp
from jax import lax
from jax.experimental import pallas as pl
from jax.experimental.pallas import tpu as pltpu

F32 = jnp.float32
BF16 = jnp.bfloat16
I32 = jnp.int32
U32 = jnp.uint32

D_MODEL = 1024
HEAD_DIM = 64
N_HEADS = 16
N_KV = 2
REP = N_HEADS // N_KV
CHUNK = 64
WINDOW = 128
BAND = WINDOW + CHUNK
KEY_PAD = 256
D_ATT = N_HEADS * HEAD_DIM
D_KV = N_KV * HEAD_DIM
NUM_BUCKETS = 32
MAX_DISTANCE = 128
GMLP_CHUNK = 128
D_GMLP = 1024
N_GROUPS = 4
GROUP_W = D_GMLP // N_GROUPS
N_EXPERTS = 32
TOP_K = 4
D_FF = 1024
SWIGLU_LIMIT = 7.0
SWIGLU_ALPHA = 1.702
NORM_EPS = 1e-5
D_IN = D_ATT + 2 * D_KV + 2 * D_GMLP + 2 * D_MODEL
D_REST = D_IN - D_ATT
COL_KV = 0
COL_U = COL_KV + 2 * D_KV
COL_VG = COL_U + D_GMLP
COL_GA = COL_VG + D_GMLP
COL_GB = COL_GA + D_MODEL
SQRT_HALF = float(np.sqrt(0.5))

LANES = 128
WORD_SUBLANES = 8
VMEM_LIMIT = 56 * 1024 * 1024

TOK_TILE = 256
GRAN = WORD_SUBLANES
SLOT_WORDS = D_MODEL // 2
SEG_BITS = (TOK_TILE // GRAN).bit_length()
LOCAL_ROWS = TOP_K * TOK_TILE + N_EXPERTS * GRAN
ROW_BLOCK = 256
PAD_BITS = (ROW_BLOCK // GRAN - 1).bit_length()
TILE_BITS = (LOCAL_ROWS // GRAN).bit_length()
BLOCK_DMA_PRIORITY = 1
ROUTE_ROWS = 16


def _const_spec(shape):
    nd = len(shape)
    return pl.BlockSpec(shape, lambda *_: (0,) * nd, pipeline_mode=pl.Buffered(1))


def _rms_norm(x, g):
    ms = jnp.mean(x * x, axis=-1, keepdims=True)
    return x * lax.rsqrt(ms + NORM_EPS) * g


def _gelu(x):
    return 0.5 * x * (1.0 + lax.erf(x * SQRT_HALF))


def _attend(q, k, v, bias, valid):
    logits = lax.dot_general(q, k, (((1,), (1,)), ((), ())), preferred_element_type=F32) + bias
    if valid is not None:
        logits = jnp.where(valid, logits, -jnp.inf)
    m = jnp.max(logits, axis=-1, keepdims=True)
    p = jnp.exp(logits - m)
    den = jnp.sum(p, axis=-1, keepdims=True)
    w = (p * (1.0 / den)).astype(BF16)
    return jnp.dot(w, v, preferred_element_type=F32)


def _attention_rows(zq, row0, n_rows, k_ext, v_ext, bias_ref, valid, oatt_ref, heads_per_unit):
    lane = lax.broadcasted_iota(I32, (1, LANES), 1)
    for g in range(N_KV):
        in_group = (lane >= g * HEAD_DIM) & (lane < (g + 1) * HEAD_DIM)
        qmask = jnp.where(in_group, HEAD_DIM ** -0.5, 0.0).astype(BF16)
        for part in range(REP // heads_per_unit):
            tiles = range(part * heads_per_unit, (part + 1) * heads_per_unit)
            q = jnp.concatenate([zq[row0:row0 + n_rows, t * LANES:(t + 1) * LANES] * qmask for t in tiles],
                                axis=0)
            b0 = part * heads_per_unit * n_rows
            o = _attend(q, k_ext, v_ext, bias_ref[g, b0:b0 + heads_per_unit * n_rows, :], valid)
            for u, t in enumerate(tiles):
                c0 = t * LANES + g * HEAD_DIM
                oatt_ref[row0:row0 + n_rows, c0:c0 + HEAD_DIM] = (
                    o[u * n_rows:(u + 1) * n_rows, g * HEAD_DIM:(g + 1) * HEAD_DIM].astype(BF16))


def _layer_tail(x, xn_b, oatt_ref, osgu_ref, w_in_ref, sguw_ref, sgub_ref, lng_ref, lnb_ref,
                woa_ref, wos_ref, wout_ref, g2_ref, wr_ref, br_ref, tri_ref, low_ref,
                x1_ref, h_ref, route_ref, routet_ref, cnt_ref, vn_ref):
    T = x.shape[0]
    u = _gelu(jnp.dot(xn_b, w_in_ref[:, COL_U:COL_VG], preferred_element_type=F32))
    vg = _gelu(jnp.dot(xn_b, w_in_ref[:, COL_VG:COL_GA], preferred_element_type=F32))
    mu = jnp.mean(vg, axis=-1, keepdims=True)
    var = jnp.mean(jnp.square(vg - mu), axis=-1, keepdims=True)
    vn = (vg - mu) * lax.rsqrt(var + NORM_EPS) * lng_ref[...] + lnb_ref[...]
    if vn_ref is not None:
        vn_ref[...] = vn
    vn_b = vn.astype(BF16)
    for j in range(T // GMLP_CHUNK):
        rows = slice(j * GMLP_CHUNK, (j + 1) * GMLP_CHUNK)
        for g in range(N_GROUPS):
            cols = slice(g * GROUP_W, (g + 1) * GROUP_W)
            mixed = jnp.dot(sguw_ref[g], vn_b[rows, cols], preferred_element_type=F32) + sgub_ref[g]
            osgu_ref[rows, cols] = (u[rows, cols] * mixed).astype(BF16)

    ga = jax.nn.sigmoid(jnp.dot(xn_b, w_in_ref[:, COL_GA:COL_GB], preferred_element_type=F32))
    merged = ga * jnp.dot(oatt_ref[...], woa_ref[...], preferred_element_type=F32)
    gb = jax.nn.sigmoid(jnp.dot(xn_b, w_in_ref[:, COL_GB:D_REST], preferred_element_type=F32))
    merged = merged + gb * jnp.dot(osgu_ref[...], wos_ref[...], preferred_element_type=F32)
    x1 = x + jnp.dot(merged.astype(BF16), wout_ref[...], preferred_element_type=F32)
    x1_ref[...] = x1

    h_b = _rms_norm(x1, g2_ref[...]).astype(BF16)
    h_ref[...] = h_b
    logits = jnp.dot(h_b, wr_ref[...], preferred_element_type=F32)
    lt = jnp.transpose(logits)[:N_EXPERTS, :] + br_ref[...]
    e_iota = lax.broadcasted_iota(I32, (N_EXPERTS, T), 0).astype(F32)
    cur = lt
    vals, idxs = [], []
    for _ in range(TOP_K):
        m = jnp.max(cur, axis=0, keepdims=True)
        ik = jnp.min(jnp.where(cur == m, e_iota, float(N_EXPERTS)), axis=0, keepdims=True)
        vals.append(m)
        idxs.append(ik)
        cur = jnp.where(e_iota == ik, -jnp.inf, cur)
    exps = [jnp.exp(v - vals[0]) for v in vals]
    den = exps[0] + exps[1] + exps[2] + exps[3]
    gates = [e / den for e in exps]

    onehot = jnp.zeros((N_EXPERTS, T), F32)
    for ik in idxs:
        onehot = onehot + jnp.where(e_iota == ik, 1.0, 0.0)
    rank = jnp.dot(onehot.astype(BF16), tri_ref[...], preferred_element_type=F32)
    cnt = jnp.sum(onehot, axis=1, keepdims=True)
    cnt_b = jnp.broadcast_to(cnt, (N_EXPERTS, LANES))
    gran = jnp.floor((cnt_b + (GRAN - 1)) * (1.0 / GRAN))
    off = jnp.dot(low_ref[...], gran.astype(BF16), preferred_element_type=F32) * GRAN
    base = off[:, 0:1] + rank
    poss = [jnp.sum(jnp.where(e_iota == ik, base, 0.0), axis=0, keepdims=True) for ik in idxs]

    rec = jnp.concatenate(idxs + poss + gates
                          + [jnp.zeros((ROUTE_ROWS - 3 * TOP_K, T), F32)], axis=0)
    route_ref[0] = rec
    rec_pad = jnp.concatenate([rec, jnp.zeros((LANES - ROUTE_ROWS, T), F32)], axis=0)
    routet_ref[...] = jnp.transpose(rec_pad)
    cnt_ref[0] = cnt_b


def _prompt_layer_kernel(x_ref, g1_ref, wq_ref, w_in_ref, bias_ref, sguw_ref, sgub_ref, lng_ref,
                         lnb_ref, woa_ref, wos_ref, wout_ref, g2_ref, wr_ref, br_ref, tri_ref, low_ref,
                         x1_ref, h_ref, route_ref, routet_ref, cnt_ref, kvwin_ref,
                         kvx_ref, oatt_ref, osgu_ref):
    T = TOK_TILE
    j = pl.program_id(1)
    x = x_ref[0]
    xn_b = _rms_norm(x, g1_ref[...]).astype(BF16)
    zq = jnp.dot(xn_b, wq_ref[...], preferred_element_type=F32).astype(BF16)
    zkv = jnp.dot(xn_b, w_in_ref[:, COL_KV:COL_U], preferred_element_type=F32)

    @pl.when(j == pl.num_programs(1) - 1)
    def _():
        kvwin_ref[0] = zkv[T - WINDOW:, :]

    @pl.when(j == 0)
    def _():
        kvx_ref[0:WINDOW, :] = jnp.zeros((WINDOW, 2 * D_KV), BF16)

    @pl.when(j > 0)
    def _():
        kvx_ref[0:WINDOW, :] = kvx_ref[T:T + WINDOW, :]

    kvx_ref[WINDOW:, :] = zkv.astype(BF16)

    col = lax.broadcasted_iota(I32, (1, KEY_PAD), 1)
    key_pad = jnp.zeros((KEY_PAD - BAND, 2 * D_KV), BF16)
    for c in range(T // CHUNK):
        r0 = c * CHUNK
        kvb = jnp.concatenate([kvx_ref[r0:r0 + BAND, :], key_pad], axis=0)
        valid = ((col + (j * T + r0 - WINDOW)) >= 0) | (col >= BAND) if r0 < WINDOW else None
        _attention_rows(zq, r0, CHUNK, kvb[:, :D_KV], kvb[:, D_KV:], bias_ref, valid, oatt_ref,
                        heads_per_unit=4)

    _layer_tail(x, xn_b, oatt_ref, osgu_ref, w_in_ref, sguw_ref, sgub_ref, lng_ref, lnb_ref,
                woa_ref, wos_ref, wout_ref, g2_ref, wr_ref, br_ref, tri_ref, low_ref,
                x1_ref, h_ref, route_ref, routet_ref, cnt_ref, None)


def _sample_layer_kernel(x1_in, h_in, route_in, routet_in, cnt_in,
                         x_ref, ck_ref, cv_ref, g1_ref, wq_ref, w_in_ref, bias_ref, sguw_ref, sgub_ref,
                         lng_ref, lnb_ref, woa_ref, wos_ref, wout_ref, g2_ref, wr_ref, br_ref, tri_ref,
                         low_ref,
                         x1_ref, h_ref, route_ref, routet_ref, cnt_ref, kvnew_ref, vn_ref,
                         oatt_ref, osgu_ref):
    del x1_in, h_in, route_in, routet_in, cnt_in
    T = TOK_TILE
    nq = x_ref.shape[0] // ck_ref.shape[0]
    x = x_ref[...]
    xn_b = _rms_norm(x, g1_ref[...]).astype(BF16)
    zq = jnp.dot(xn_b, wq_ref[...], preferred_element_type=F32).astype(BF16)
    zkv = jnp.dot(xn_b, w_in_ref[:, COL_KV:COL_U], preferred_element_type=F32)
    kvnew_ref[...] = zkv
    zkv_b = zkv.astype(BF16)
    n_cache = ck_ref.shape[1]
    key_pad = jnp.zeros((KEY_PAD - n_cache - nq, D_KV), BF16)
    for b in range(T // nq):
        r0 = b * nq
        kk = jnp.concatenate([ck_ref[b].astype(BF16), zkv_b[r0:r0 + nq, :D_KV], key_pad], axis=0)
        vv = jnp.concatenate([cv_ref[b].astype(BF16), zkv_b[r0:r0 + nq, D_KV:], key_pad], axis=0)
        _attention_rows(zq, r0, nq, kk, vv, bias_ref, None, oatt_ref, heads_per_unit=REP)

    _layer_tail(x, xn_b, oatt_ref, osgu_ref, w_in_ref, sguw_ref, sgub_ref, lng_ref, lnb_ref,
                woa_ref, wos_ref, wout_ref, g2_ref, wr_ref, br_ref, tri_ref, low_ref,
                x1_ref, h_ref, route_ref, routet_ref, cnt_ref, vn_ref)


def _pack_slot_rows(x):
    lo = lax.bitcast_convert_type(x[:, :SLOT_WORDS].astype(BF16).astype(F32), U32)
    hi = lax.bitcast_convert_type(x[:, SLOT_WORDS:].astype(BF16).astype(F32), U32)
    return (lo >> 16) | hi


def _unpack_slot_rows(w):
    lo = lax.bitcast_convert_type(w << 16, F32)
    hi = lax.bitcast_convert_type(w & jnp.uint32(0xFFFF0000), F32)
    return jnp.concatenate([lo, hi], axis=1).astype(BF16)


def _segment_copies(n_gran, src_row, dst_row, bits, make_copy, act):
    for b in range(min(bits, 2)):
        rows = GRAN << b

        @pl.when(((n_gran >> b) & 1) == 1)
        def _(b=b, rows=rows):
            done = (n_gran & ((1 << b) - 1)) * GRAN
            s = pl.multiple_of(src_row + done, GRAN)
            d = pl.multiple_of(dst_row + done, GRAN)
            act(make_copy(s, d, rows))

    if bits > 2:
        def quad(q, carry):
            done = ((n_gran & 3) + 4 * q) * GRAN
            s = pl.multiple_of(src_row + done, GRAN)
            d = pl.multiple_of(dst_row + done, GRAN)
            act(make_copy(s, d, 4 * GRAN))
            return carry
        lax.fori_loop(0, n_gran >> 2, quad, 0)


def _wait_granules(n_gran, bits, make_copy):
    for b in range(bits):
        @pl.when(((n_gran >> b) & 1) == 1)
        def _(b=b):
            make_copy(GRAN << b).wait()


def _dispatch_kernel(cnt_ref, loff_ref, goff_ref, tot_ref, padn_ref, padoff_ref,
                     route_ref, h_ref, xs_hbm, stage_ref, zero_ref, sem):
    i = pl.program_id(0)
    last = pl.num_programs(0) - 1
    slot = lax.rem(i, 2)
    T = TOK_TILE
    pos = route_ref[0, TOP_K:2 * TOP_K, :].astype(I32)
    r_iota = lax.broadcasted_iota(I32, (LOCAL_ROWS, T), 0)
    p = jnp.zeros((LOCAL_ROWS, T), F32)
    for k in range(TOP_K):
        p = jnp.where(r_iota == pos[k:k + 1, :], 1.0, p)
    stage_ref[slot] = _pack_slot_rows(jnp.dot(p.astype(BF16), h_ref[...], preferred_element_type=F32))

    def seg_copy(s, d, rows):
        return pltpu.make_async_copy(stage_ref.at[slot, pl.ds(s, rows)], xs_hbm.at[pl.ds(d, rows)],
                                     sem.at[slot])

    def start_segment(e, carry):
        t = i * N_EXPERTS + e
        _segment_copies(cnt_ref[t], loff_ref[t] * GRAN, goff_ref[t] * GRAN, SEG_BITS, seg_copy,
                        lambda cp: cp.start())
        return carry

    lax.fori_loop(0, N_EXPERTS, start_segment, 0)

    def wait_tile(tile, tile_slot):
        _wait_granules(tot_ref[tile], TILE_BITS, lambda rows: pltpu.make_async_copy(
            stage_ref.at[tile_slot, pl.ds(0, rows)], xs_hbm.at[pl.ds(0, rows)], sem.at[tile_slot]))

    @pl.when(i > 0)
    def _():
        wait_tile(i - 1, 1 - slot)

    @pl.when(i == last)
    def _():
        wait_tile(i, slot)
        zero_ref[...] = jnp.zeros(zero_ref.shape, U32)

        def pad_copy(s, d, rows):
            return pltpu.make_async_copy(zero_ref.at[pl.ds(s, rows)], xs_hbm.at[pl.ds(d, rows)], sem.at[2])

        def for_pads(act):
            def body(e, carry):
                _segment_copies(padn_ref[e], 0, padoff_ref[e] * GRAN, PAD_BITS, pad_copy, act)
                return carry
            lax.fori_loop(0, N_EXPERTS, body, 0)

        for_pads(lambda cp: cp.start())
        for_pads(lambda cp: cp.wait())


def _expert_kernel(blk0_ref, nblk_ref, xs_hbm, w1_ref, b1_ref, w2_ref, b2_ref, ys_hbm,
                   w1b_ref, w2b_ref, xbuf_ref, ybuf_ref, sem_in, sem_out):
    e = pl.program_id(0)
    nb = nblk_ref[e]
    row0 = blk0_ref[e] * ROW_BLOCK

    def in_copy(k, slot):
        r = pl.multiple_of(row0 + k * ROW_BLOCK, ROW_BLOCK)
        return pltpu.make_async_copy(xs_hbm.at[pl.ds(r, ROW_BLOCK)], xbuf_ref.at[slot], sem_in.at[slot])

    def out_copy(k, slot):
        r = pl.multiple_of(row0 + k * ROW_BLOCK, ROW_BLOCK)
        return pltpu.make_async_copy(ybuf_ref.at[slot], ys_hbm.at[pl.ds(r, ROW_BLOCK)], sem_out.at[slot])

    @pl.when(nb > 0)
    def _():
        in_copy(0, 0).start(priority=BLOCK_DMA_PRIORITY)

    w1b_ref[...] = w1_ref[0].astype(BF16)
    w2b_ref[...] = w2_ref[0].astype(BF16)

    def block(k, carry):
        slot = lax.rem(k, 2)

        @pl.when(k + 1 < nb)
        def _():
            in_copy(k + 1, 1 - slot).start(priority=BLOCK_DMA_PRIORITY)

        in_copy(k, slot).wait()

        @pl.when(k >= 2)
        def _():
            out_copy(k - 2, slot).wait()

        h1 = jnp.dot(_unpack_slot_rows(xbuf_ref[slot]), w1b_ref[...], preferred_element_type=F32) + b1_ref[0]
        gate = jnp.minimum(h1[:, :D_FF], SWIGLU_LIMIT)
        up = jnp.clip(h1[:, D_FF:], -SWIGLU_LIMIT, SWIGLU_LIMIT)
        act = gate * jax.nn.sigmoid(SWIGLU_ALPHA * gate) * (up + 1.0)
        y = jnp.dot(act.astype(BF16), w2b_ref[...], preferred_element_type=F32) + b2_ref[0]
        ybuf_ref[slot] = _pack_slot_rows(y)
        out_copy(k, slot).start(priority=BLOCK_DMA_PRIORITY)
        return carry

    lax.fori_loop(0, nb, block, 0)

    @pl.when(nb >= 2)
    def _():
        out_copy(nb - 2, lax.rem(nb, 2)).wait()

    @pl.when(nb >= 1)
    def _():
        out_copy(nb - 1, lax.rem(nb + 1, 2)).wait()


def _combine_kernel(n_prompt_tiles, cnt_ref, loff_ref, goff_ref, tot_ref,
                    routet_ref, x1_ref, gf_ref, ys_hbm, yp_ref, ysm_ref, stage_ref, sem):
    i = pl.program_id(0)
    slot = lax.rem(i, 2)
    T = TOK_TILE

    def start_tile(tile, tile_slot):
        def seg_copy(s, d, rows):
            return pltpu.make_async_copy(ys_hbm.at[pl.ds(d, rows)], stage_ref.at[tile_slot, pl.ds(s, rows)],
                                         sem.at[tile_slot])

        def body(e, carry):
            t = tile * N_EXPERTS + e
            _segment_copies(cnt_ref[t], loff_ref[t] * GRAN, goff_ref[t] * GRAN, SEG_BITS, seg_copy,
                            lambda cp: cp.start())
            return carry
        lax.fori_loop(0, N_EXPERTS, body, 0)

    @pl.when(i == 0)
    def _():
        stage_ref[...] = jnp.zeros(stage_ref.shape, U32)
        start_tile(i, slot)

    @pl.when(i + 1 < pl.num_programs(0))
    def _():
        start_tile(i + 1, 1 - slot)

    rt = routet_ref[...]
    l_iota = lax.broadcasted_iota(I32, (T, LOCAL_ROWS), 1)
    pw = jnp.zeros((T, LOCAL_ROWS), F32)
    for k in range(TOP_K):
        pos_k = rt[:, TOP_K + k:TOP_K + k + 1].astype(I32)
        pw = jnp.where(l_iota == pos_k, rt[:, 2 * TOP_K + k:2 * TOP_K + k + 1], pw)

    _wait_granules(tot_ref[i], TILE_BITS, lambda rows: pltpu.make_async_copy(
        ys_hbm.at[pl.ds(0, rows)], stage_ref.at[slot, pl.ds(0, rows)], sem.at[slot]))
    moe = jnp.dot(pw.astype(BF16), _unpack_slot_rows(stage_ref[slot]), preferred_element_type=F32)
    y = _rms_norm(x1_ref[...] + moe, gf_ref[...])

    @pl.when(i < n_prompt_tiles)
    def _():
        yp_ref[...] = y

    @pl.when(i >= n_prompt_tiles)
    def _():
        ysm_ref[...] = y


def _t5_bucket(rel):
    half = NUM_BUCKETS // 2
    max_exact = half // 2
    ret = jnp.where(rel > 0, half, 0)
    n = jnp.abs(rel)
    nf = jnp.maximum(n, 1).astype(F32)
    large = max_exact + (jnp.log(nf / max_exact) / math.log(MAX_DISTANCE / max_exact)
                         * (half - max_exact)).astype(jnp.int32)
    large = jnp.minimum(large, half - 1)
    return ret + jnp.where(n < max_exact, n, large)


def _stacked_bias(table, sinks, q_pos, k_pos):
    nq, nk = q_pos.shape[0], k_pos.shape[0]
    bucket = _t5_bucket(k_pos[None, :] - q_pos[:, None])
    onehot = (bucket[:, :, None] == jnp.arange(NUM_BUCKETS)).astype(F32)
    bias = jnp.einsum('qkb,bh->hqk', onehot, table.astype(F32), precision=lax.Precision.HIGHEST)
    bias = bias.reshape(N_KV, REP * nq, nk)
    sink = jnp.repeat(sinks.astype(F32).reshape(N_KV, REP, 1), nq, axis=2).reshape(N_KV, REP * nq, 1)
    tail = jnp.full((N_KV, REP * nq, KEY_PAD - nk - 1), -jnp.inf, F32)
    return jnp.concatenate([bias, sink, tail], axis=-1)


def kernel(x_prompt, x_sample, cache_win_k, cache_win_v, norm1_g, w_in, attn_sinks, rel_bias_table, sgu_ln_g, sgu_ln_b, sgu_w, sgu_b, w_o_att, w_o_sgu, w_out, norm2_g, w_router, b_router, w_exp_in, b_exp_in, w_exp_out, b_exp_out, final_norm_g):
    batch, seq, _ = x_prompt.shape
    dec_batch, dec_seq, _ = x_sample.shape
    cache_rows = cache_win_k.shape[2]
    assert x_prompt.shape[2] == D_MODEL and w_in.shape == (1, D_MODEL, D_IN)
    assert seq % TOK_TILE == 0 and TOK_TILE % GMLP_CHUNK == 0 and TOK_TILE >= WINDOW
    assert TOK_TILE % dec_seq == 0 and (dec_batch * dec_seq) % TOK_TILE == 0
    assert dec_seq <= GMLP_CHUNK and GMLP_CHUNK % dec_seq == 0 and cache_rows == WINDOW
    T = TOK_TILE
    n_prompt = batch * seq
    n_sample = dec_batch * dec_seq
    n_tok = n_prompt + n_sample
    tiles_per_seq = seq // T
    n_ptiles = n_prompt // T
    n_stiles = n_sample // T
    n_tiles = n_ptiles + n_stiles
    seqs_per_tile = T // dec_seq

    w_q = w_in[0][:, :D_ATT].reshape(D_MODEL, N_KV, REP, HEAD_DIM).transpose(0, 2, 1, 3).reshape(D_MODEL, D_ATT)
    w_q_b = w_q.astype(BF16)
    w_in_b = w_in[0][:, D_ATT:].astype(BF16)
    woa_b = w_o_att[0].reshape(N_KV, REP, HEAD_DIM, D_MODEL).transpose(1, 0, 2, 3).reshape(
        D_ATT, D_MODEL).astype(BF16)
    wos_b = w_o_sgu[0].astype(BF16)
    wout_b = w_out[0].astype(BF16)
    wr_b = jnp.pad(w_router[0], ((0, 0), (0, LANES - N_EXPERTS))).astype(BF16)
    br_col = b_router[0].astype(F32).reshape(N_EXPERTS, 1)
    g1 = norm1_g[0].reshape(1, D_MODEL)
    g2 = norm2_g[0].reshape(1, D_MODEL)
    gf = final_norm_g.reshape(1, D_MODEL)
    lng = sgu_ln_g[0].reshape(1, D_GMLP)
    lnb = sgu_ln_b[0].reshape(1, D_GMLP)
    tril = jnp.tril(jnp.ones((GMLP_CHUNK, GMLP_CHUNK), dtype=bool))
    sguw_p = jnp.where(tril[None], sgu_w[0], 0).astype(BF16)
    sgub_p = jnp.broadcast_to(sgu_b[0][:, :, None], (N_GROUPS, GMLP_CHUNK, GROUP_W)).astype(F32)
    reps = GMLP_CHUNK // dec_seq
    corner = jnp.where(tril[None, :dec_seq, :dec_seq], sgu_w[0][:, :dec_seq, :dec_seq], 0)
    sguw_s = jnp.einsum('ab,gij->gaibj', jnp.eye(reps, dtype=F32), corner).reshape(
        N_GROUPS, GMLP_CHUNK, GMLP_CHUNK).astype(BF16)
    sgub_s = jnp.broadcast_to(jnp.tile(sgu_b[0][:, :dec_seq], (1, reps))[:, :, None],
                              (N_GROUPS, GMLP_CHUNK, GROUP_W)).astype(F32)
    bias_p = _stacked_bias(rel_bias_table, attn_sinks[0], jnp.arange(CHUNK) + WINDOW, jnp.arange(BAND))
    bias_s = _stacked_bias(rel_bias_table, attn_sinks[0], cache_rows + jnp.arange(dec_seq),
                           jnp.arange(cache_rows + dec_seq))
    tri = jnp.triu(jnp.ones((T, T), F32), k=1).astype(BF16)
    low = jnp.tril(jnp.ones((N_EXPERTS, N_EXPERTS), F32), k=-1).astype(BF16)

    layer_out_shapes = (
        jax.ShapeDtypeStruct((n_tok, D_MODEL), F32),
        jax.ShapeDtypeStruct((n_tok, D_MODEL), BF16),
        jax.ShapeDtypeStruct((n_tiles, ROUTE_ROWS, T), F32),
        jax.ShapeDtypeStruct((n_tok, LANES), F32),
        jax.ShapeDtypeStruct((n_tiles, N_EXPERTS, LANES), F32),
    )
    shared_consts = (w_q_b, w_in_b)
    tail_consts_p = (sguw_p, sgub_p, lng, lnb, woa_b, wos_b, wout_b, g2, wr_b, br_col, tri, low)
    tail_consts_s = (sguw_s, sgub_s, lng, lnb, woa_b, wos_b, wout_b, g2, wr_b, br_col, tri, low)

    def tile_specs(tile_of):
        return [
            pl.BlockSpec((T, D_MODEL), lambda *g: (tile_of(*g), 0)),
            pl.BlockSpec((T, D_MODEL), lambda *g: (tile_of(*g), 0)),
            pl.BlockSpec((1, ROUTE_ROWS, T), lambda *g: (tile_of(*g), 0, 0)),
            pl.BlockSpec((T, LANES), lambda *g: (tile_of(*g), 0)),
            pl.BlockSpec((1, N_EXPERTS, LANES), lambda *g: (tile_of(*g), 0, 0)),
        ]

    prompt_consts = (g1,) + shared_consts + (bias_p,) + tail_consts_p
    x1, h, route, routet, cnt, kvwin = pl.pallas_call(
        _prompt_layer_kernel,
        grid=(batch, tiles_per_seq),
        in_specs=[pl.BlockSpec((1, T, D_MODEL), lambda b, j: (b, j, 0))]
                 + [_const_spec(c.shape) for c in prompt_consts],
        out_specs=tile_specs(lambda b, j: b * tiles_per_seq + j)
                  + [pl.BlockSpec((1, WINDOW, 2 * D_KV), lambda b, j: (b, 0, 0))],
        out_shape=layer_out_shapes + (jax.ShapeDtypeStruct((batch, WINDOW, 2 * D_KV), F32),),
        scratch_shapes=[pltpu.VMEM((T + WINDOW, 2 * D_KV), BF16),
                        pltpu.VMEM((T, D_ATT), BF16),
                        pltpu.VMEM((T, D_GMLP), BF16)],
        compiler_params=pltpu.CompilerParams(dimension_semantics=("arbitrary", "arbitrary"),
                                             vmem_limit_bytes=VMEM_LIMIT),
        name="layer_prompt",
    )(x_prompt, *prompt_consts)

    xs_flat = x_sample.reshape(n_sample, D_MODEL)
    ck = cache_win_k[0].reshape(dec_batch, cache_rows, D_KV)
    cv = cache_win_v[0].reshape(dec_batch, cache_rows, D_KV)
    sample_consts = (g1,) + shared_consts + (bias_s,) + tail_consts_s
    any_spec = pl.BlockSpec(memory_space=pl.ANY)
    x1, h, route, routet, cnt, kvnew, vn_s = pl.pallas_call(
        _sample_layer_kernel,
        grid=(n_stiles,),
        in_specs=[any_spec] * 5
                 + [pl.BlockSpec((T, D_MODEL), lambda i: (i, 0)),
                    pl.BlockSpec((seqs_per_tile, cache_rows, D_KV), lambda i: (i, 0, 0)),
                    pl.BlockSpec((seqs_per_tile, cache_rows, D_KV), lambda i: (i, 0, 0))]
                 + [_const_spec(c.shape) for c in sample_consts],
        out_specs=tile_specs(lambda i: n_ptiles + i)
                  + [pl.BlockSpec((T, 2 * D_KV), lambda i: (i, 0)),
                     pl.BlockSpec((T, D_GMLP), lambda i: (i, 0))],
        out_shape=layer_out_shapes + (jax.ShapeDtypeStruct((n_sample, 2 * D_KV), F32),
                                      jax.ShapeDtypeStruct((n_sample, D_GMLP), F32)),
        scratch_shapes=[pltpu.VMEM((T, D_ATT), BF16), pltpu.VMEM((T, D_GMLP), BF16)],
        input_output_aliases={0: 0, 1: 1, 2: 2, 3: 3, 4: 4},
        compiler_params=pltpu.CompilerParams(dimension_semantics=("arbitrary",),
                                             vmem_limit_bytes=VMEM_LIMIT),
        name="layer_sample",
    )(x1, h, route, routet, cnt, xs_flat, ck, cv, *sample_consts)

    blk_gran = ROW_BLOCK // GRAN
    counts = cnt[:, :, 0].astype(I32)
    seg_gran = (counts + (GRAN - 1)) // GRAN
    local_off = jnp.cumsum(seg_gran, axis=1) - seg_gran
    tot_gran = jnp.sum(seg_gran, axis=0)
    ptot_gran = (tot_gran + (blk_gran - 1)) // blk_gran * blk_gran
    pend_gran = jnp.cumsum(ptot_gran)
    gstart = pend_gran - ptot_gran
    global_off = gstart[None, :] + jnp.cumsum(seg_gran, axis=0) - seg_gran
    pad_n = ptot_gran - tot_gran
    pad_off = gstart + tot_gran
    n_rows = -(-(TOP_K * n_tok + n_tiles * N_EXPERTS * (GRAN - 1) + N_EXPERTS * (ROW_BLOCK - GRAN))
               // ROW_BLOCK) * ROW_BLOCK
    blk0 = (gstart // blk_gran).astype(I32)
    nblk = (ptot_gran // blk_gran).astype(I32)
    tile_gran = jnp.sum(seg_gran, axis=1).astype(I32)
    seg_gran_f = seg_gran.reshape(-1).astype(I32)
    local_off_f = local_off.reshape(-1).astype(I32)
    global_off_f = global_off.reshape(-1).astype(I32)

    xs = pl.pallas_call(
        _dispatch_kernel,
        grid_spec=pltpu.PrefetchScalarGridSpec(
            num_scalar_prefetch=6,
            grid=(n_tiles,),
            in_specs=[pl.BlockSpec((1, ROUTE_ROWS, T), lambda i, *_: (i, 0, 0)),
                      pl.BlockSpec((T, D_MODEL), lambda i, *_: (i, 0))],
            out_specs=pl.BlockSpec(memory_space=pl.ANY),
            scratch_shapes=[pltpu.VMEM((2, LOCAL_ROWS, SLOT_WORDS), U32),
                            pltpu.VMEM((ROW_BLOCK, SLOT_WORDS), U32),
                            pltpu.SemaphoreType.DMA((3,))]),
        out_shape=jax.ShapeDtypeStruct((n_rows, SLOT_WORDS), U32),
        compiler_params=pltpu.CompilerParams(dimension_semantics=("arbitrary",),
                                             vmem_limit_bytes=VMEM_LIMIT),
        name="moe_dispatch",
    )(seg_gran_f, local_off_f, global_off_f, tile_gran, pad_n.astype(I32), pad_off.astype(I32), route, h)

    ys = pl.pallas_call(
        _expert_kernel,
        grid_spec=pltpu.PrefetchScalarGridSpec(
            num_scalar_prefetch=2,
            grid=(N_EXPERTS,),
            in_specs=[pl.BlockSpec(memory_space=pl.ANY),
                      pl.BlockSpec((1, D_MODEL, 2 * D_FF), lambda e, *_: (e, 0, 0)),
                      pl.BlockSpec((1, 1, 2 * D_FF), lambda e, *_: (e, 0, 0)),
                      pl.BlockSpec((1, D_FF, D_MODEL), lambda e, *_: (e, 0, 0)),
                      pl.BlockSpec((1, 1, D_MODEL), lambda e, *_: (e, 0, 0))],
            out_specs=pl.BlockSpec(memory_space=pl.ANY),
            scratch_shapes=[pltpu.VMEM((D_MODEL, 2 * D_FF), BF16), pltpu.VMEM((D_FF, D_MODEL), BF16),
                            pltpu.VMEM((2, ROW_BLOCK, SLOT_WORDS), U32), pltpu.VMEM((2, ROW_BLOCK, SLOT_WORDS), U32),
                            pltpu.SemaphoreType.DMA((2,)), pltpu.SemaphoreType.DMA((2,))]),
        out_shape=jax.ShapeDtypeStruct((n_rows, SLOT_WORDS), U32),
        compiler_params=pltpu.CompilerParams(dimension_semantics=("arbitrary",),
                                             vmem_limit_bytes=VMEM_LIMIT),
        name="moe_experts",
    )(blk0, nblk, xs, w_exp_in[0], b_exp_in[0].reshape(N_EXPERTS, 1, 2 * D_FF),
      w_exp_out[0], b_exp_out[0].reshape(N_EXPERTS, 1, D_MODEL))

    y_p, y_s = pl.pallas_call(
        functools.partial(_combine_kernel, n_ptiles),
        grid_spec=pltpu.PrefetchScalarGridSpec(
            num_scalar_prefetch=4,
            grid=(n_tiles,),
            in_specs=[pl.BlockSpec((T, LANES), lambda i, *_: (i, 0)),
                      pl.BlockSpec((T, D_MODEL), lambda i, *_: (i, 0)),
                      pl.BlockSpec((1, D_MODEL), lambda i, *_: (0, 0)),
                      pl.BlockSpec(memory_space=pl.ANY)],
            out_specs=[pl.BlockSpec((T, D_MODEL), lambda i, *_: (jnp.minimum(i, n_ptiles - 1), 0)),
                       pl.BlockSpec((T, D_MODEL), lambda i, *_: (jnp.maximum(i - n_ptiles, 0), 0))],
            scratch_shapes=[pltpu.VMEM((2, LOCAL_ROWS, SLOT_WORDS), U32), pltpu.SemaphoreType.DMA((2,))]),
        out_shape=(jax.ShapeDtypeStruct((n_prompt, D_MODEL), F32),
                   jax.ShapeDtypeStruct((n_sample, D_MODEL), F32)),
        compiler_params=pltpu.CompilerParams(dimension_semantics=("arbitrary",),
                                             vmem_limit_bytes=VMEM_LIMIT),
        name="moe_combine",
    )(seg_gran_f, local_off_f, global_off_f, tile_gran, routet, x1, gf, ys)

    y_prompt = y_p.reshape(batch, seq, D_MODEL)
    y_sample = y_s.reshape(dec_batch, dec_seq, D_MODEL)
    new_win_k_prompt = kvwin[:, :, :D_KV].reshape(1, batch, WINDOW, N_KV, HEAD_DIM)
    new_win_v_prompt = kvwin[:, :, D_KV:].reshape(1, batch, WINDOW, N_KV, HEAD_DIM)
    new_win_k_sample = kvnew[:, :D_KV].reshape(1, dec_batch, dec_seq, N_KV, HEAD_DIM)
    new_win_v_sample = kvnew[:, D_KV:].reshape(1, dec_batch, dec_seq, N_KV, HEAD_DIM)
    new_sgu_v_sample = vn_s.reshape(1, dec_batch, dec_seq, D_GMLP)
    return (y_prompt, y_sample, new_win_k_prompt, new_win_v_prompt, new_win_k_sample,
            new_win_v_sample, new_sgu_v_sample)
```

```python
import functools
import math

import numpy as np
import jax
import jax.numpy as jnp
from jax import lax
from jax.experimental import pallas as pl
from jax.experimental.pallas import tpu as pltpu

F32 = jnp.float32
BF16 = jnp.bfloat16
I32 = jnp.int32
U32 = jnp.uint32

D_MODEL = 1024
HEAD_DIM = 64
N_HEADS = 16
N_KV = 2
REP = N_HEADS // N_KV
CHUNK = 64
WINDOW = 128
BAND = WINDOW + CHUNK
KEY_PAD = 256
D_ATT = N_HEADS * HEAD_DIM
D_KV = N_KV * HEAD_DIM
NUM_BUCKETS = 32
MAX_DISTANCE = 128
GMLP_CHUNK = 128
D_GMLP = 1024
N_GROUPS = 4
GROUP_W = D_GMLP // N_GROUPS
N_EXPERTS = 32
TOP_K = 4
D_FF = 1024
SWIGLU_LIMIT = 7.0
SWIGLU_ALPHA = 1.702
NORM_EPS = 1e-5
D_IN = D_ATT + 2 * D_KV + 2 * D_GMLP + 2 * D_MODEL
D_REST = D_IN - D_ATT
COL_KV = 0
COL_U = COL_KV + 2 * D_KV
COL_VG = COL_U + D_GMLP
COL_GA = COL_VG + D_GMLP
COL_GB = COL_GA + D_MODEL
SQRT_HALF = float(np.sqrt(0.5))

LANES = 128
WORD_SUBLANES = 8
VMEM_LIMIT = 56 * 1024 * 1024

TOK_TILE = 256
GRAN = WORD_SUBLANES
SLOT_WORDS = D_MODEL // 2
SEG_BITS = (TOK_TILE // GRAN).bit_length()
LOCAL_ROWS = TOP_K * TOK_TILE + N_EXPERTS * GRAN
ROW_BLOCK = 256
PAD_BITS = (ROW_BLOCK // GRAN - 1).bit_length()
TILE_BITS = (LOCAL_ROWS // GRAN).bit_length()
BLOCK_DMA_PRIORITY = 1
ROUTE_ROWS = 16


def _const_spec(shape):
    nd = len(shape)
    return pl.BlockSpec(shape, lambda *_: (0,) * nd, pipeline_mode=pl.Buffered(1))


def _rms_norm(x, g):
    ms = jnp.mean(x * x, axis=-1, keepdims=True)
    return x * lax.rsqrt(ms + NORM_EPS) * g


def _gelu(x):
    return 0.5 * x * (1.0 + lax.erf(x * SQRT_HALF))


def _attend(q, k, v, bias, valid):
    logits = lax.dot_general(q, k, (((1,), (1,)), ((), ())), preferred_element_type=F32) + bias
    if valid is not None:
        logits = jnp.where(valid, logits, -jnp.inf)
    m = jnp.max(logits, axis=-1, keepdims=True)
    p = jnp.exp(logits - m)
    den = jnp.sum(p, axis=-1, keepdims=True)
    w = (p * (1.0 / den)).astype(BF16)
    return jnp.dot(w, v, preferred_element_type=F32)


def _attention_rows(zq, row0, n_rows, k_ext, v_ext, bias_ref, valid, oatt_ref, heads_per_unit):
    lane = lax.broadcasted_iota(I32, (1, LANES), 1)
    for g in range(N_KV):
        in_group = (lane >= g * HEAD_DIM) & (lane < (g + 1) * HEAD_DIM)
        qmask = jnp.where(in_group, HEAD_DIM ** -0.5, 0.0).astype(BF16)
        for part in range(REP // heads_per_unit):
            tiles = range(part * heads_per_unit, (part + 1) * heads_per_unit)
            q = jnp.concatenate([zq[row0:row0 + n_rows, t * LANES:(t + 1) * LANES] * qmask for t in tiles],
                                axis=0)
            b0 = part * heads_per_unit * n_rows
            o = _attend(q, k_ext, v_ext, bias_ref[g, b0:b0 + heads_per_unit * n_rows, :], valid)
            for u, t in enumerate(tiles):
                c0 = t * LANES + g * HEAD_DIM
                oatt_ref[row0:row0 + n_rows, c0:c0 + HEAD_DIM] = (
                    o[u * n_rows:(u + 1) * n_rows, g * HEAD_DIM:(g + 1) * HEAD_DIM].astype(BF16))


def _layer_tail(x, xn_b, oatt_ref, osgu_ref, w_in_ref, sguw_ref, sgub_ref, lng_ref, lnb_ref,
                woa_ref, wos_ref, wout_ref, g2_ref, wr_ref, br_ref, tri_ref, low_ref,
                x1_ref, h_ref, route_ref, routet_ref, cnt_ref, vn_ref):
    T = x.shape[0]
    u = _gelu(jnp.dot(xn_b, w_in_ref[:, COL_U:COL_VG], preferred_element_type=F32))
    vg = _gelu(jnp.dot(xn_b, w_in_ref[:, COL_VG:COL_GA], preferred_element_type=F32))
    mu = jnp.mean(vg, axis=-1, keepdims=True)
    var = jnp.mean(jnp.square(vg - mu), axis=-1, keepdims=True)
    vn = (vg - mu) * lax.rsqrt(var + NORM_EPS) * lng_ref[...] + lnb_ref[...]
    if vn_ref is not None:
        vn_ref[...] = vn
    vn_b = vn.astype(BF16)
    for j in range(T // GMLP_CHUNK):
        rows = slice(j * GMLP_CHUNK, (j + 1) * GMLP_CHUNK)
        for g in range(N_GROUPS):
            cols = slice(g * GROUP_W, (g + 1) * GROUP_W)
            mixed = jnp.dot(sguw_ref[g], vn_b[rows, cols], preferred_element_type=F32) + sgub_ref[g]
            osgu_ref[rows, cols] = (u[rows, cols] * mixed).astype(BF16)

    ga = jax.nn.sigmoid(jnp.dot(xn_b, w_in_ref[:, COL_GA:COL_GB], preferred_element_type=F32))
    merged = ga * jnp.dot(oatt_ref[...], woa_ref[...], preferred_element_type=F32)
    gb = jax.nn.sigmoid(jnp.dot(xn_b, w_in_ref[:, COL_GB:D_REST], preferred_element_type=F32))
    merged = merged + gb * jnp.dot(osgu_ref[...], wos_ref[...], preferred_element_type=F32)
    x1 = x + jnp.dot(merged.astype(BF16), wout_ref[...], preferred_element_type=F32)
    x1_ref[...] = x1

    h_b = _rms_norm(x1, g2_ref[...]).astype(BF16)
    h_ref[...] = h_b
    logits = jnp.dot(h_b, wr_ref[...], preferred_element_type=F32)
    lt = jnp.transpose(logits)[:N_EXPERTS, :] + br_ref[...]
    e_iota = lax.broadcasted_iota(I32, (N_EXPERTS, T), 0).astype(F32)
    cur = lt
    vals, idxs = [], []
    for _ in range(TOP_K):
        m = jnp.max(cur, axis=0, keepdims=True)
        ik = jnp.min(jnp.where(cur == m, e_iota, float(N_EXPERTS)), axis=0, keepdims=True)
        vals.append(m)
        idxs.append(ik)
        cur = jnp.where(e_iota == ik, -jnp.inf, cur)
    exps = [jnp.exp(v - vals[0]) for v in vals]
    den = exps[0] + exps[1] + exps[2] + exps[3]
    gates = [e / den for e in exps]

    onehot = jnp.zeros((N_EXPERTS, T), F32)
    for ik in idxs:
        onehot = onehot + jnp.where(e_iota == ik, 1.0, 0.0)
    rank = jnp.dot(onehot.astype(BF16), tri_ref[...], preferred_element_type=F32)
    cnt = jnp.sum(onehot, axis=1, keepdims=True)
    cnt_b = jnp.broadcast_to(cnt, (N_EXPERTS, LANES))
    gran = jnp.floor((cnt_b + (GRAN - 1)) * (1.0 / GRAN))
    off = jnp.dot(low_ref[...], gran.astype(BF16), preferred_element_type=F32) * GRAN
    base = off[:, 0:1] + rank
    poss = [jnp.sum(jnp.where(e_iota == ik, base, 0.0), axis=0, keepdims=True) for ik in idxs]

    rec = jnp.concatenate(idxs + poss + gates
                          + [jnp.zeros((ROUTE_ROWS - 3 * TOP_K, T), F32)], axis=0)
    route_ref[0] = rec
    rec_pad = jnp.concatenate([rec, jnp.zeros((LANES - ROUTE_ROWS, T), F32)], axis=0)
    routet_ref[...] = jnp.transpose(rec_pad)
    cnt_ref[0] = cnt_b


def _prompt_layer_kernel(x_ref, g1_ref, wq_ref, w_in_ref, bias_ref, sguw_ref, sgub_ref, lng_ref,
                         lnb_ref, woa_ref, wos_ref, wout_ref, g2_ref, wr_ref, br_ref, tri_ref, low_ref,
                         x1_ref, h_ref, route_ref, routet_ref, cnt_ref, kvwin_ref,
                         kvx_ref, oatt_ref, osgu_ref):
    T = TOK_TILE
    j = pl.program_id(1)
    x = x_ref[0]
    xn_b = _rms_norm(x, g1_ref[...]).astype(BF16)
    zq = jnp.dot(xn_b, wq_ref[...], preferred_element_type=F32).astype(BF16)
    zkv = jnp.dot(xn_b, w_in_ref[:, COL_KV:COL_U], preferred_element_type=F32)

    @pl.when(j == pl.num_programs(1) - 1)
    def _():
        kvwin_ref[0] = zkv[T - WINDOW:, :]

    @pl.when(j == 0)
    def _():
        kvx_ref[0:WINDOW, :] = jnp.zeros((WINDOW, 2 * D_KV), BF16)

    @pl.when(j > 0)
    def _():
        kvx_ref[0:WINDOW, :] = kvx_ref[T:T + WINDOW, :]

    kvx_ref[WINDOW:, :] = zkv.astype(BF16)

    col = lax.broadcasted_iota(I32, (1, KEY_PAD), 1)
    key_pad = jnp.zeros((KEY_PAD - BAND, 2 * D_KV), BF16)
    for c in range(T // CHUNK):
        r0 = c * CHUNK
        kvb = jnp.concatenate([kvx_ref[r0:r0 + BAND, :], key_pad], axis=0)
        valid = ((col + (j * T + r0 - WINDOW)) >= 0) | (col >= BAND) if r0 < WINDOW else None
        _attention_rows(zq, r0, CHUNK, kvb[:, :D_KV], kvb[:, D_KV:], bias_ref, valid, oatt_ref,
                        heads_per_unit=4)

    _layer_tail(x, xn_b, oatt_ref, osgu_ref, w_in_ref, sguw_ref, sgub_ref, lng_ref, lnb_ref,
                woa_ref, wos_ref, wout_ref, g2_ref, wr_ref, br_ref, tri_ref, low_ref,
                x1_ref, h_ref, route_ref, routet_ref, cnt_ref, None)


def _sample_layer_kernel(x1_in, h_in, route_in, routet_in, cnt_in,
                         x_ref, ck_ref, cv_ref, g1_ref, wq_ref, w_in_ref, bias_ref, sguw_ref, sgub_ref,
                         lng_ref, lnb_ref, woa_ref, wos_ref, wout_ref, g2_ref, wr_ref, br_ref, tri_ref,
                         low_ref,
                         x1_ref, h_ref, route_ref, routet_ref, cnt_ref, kvnew_ref, vn_ref,
                         oatt_ref, osgu_ref):
    del x1_in, h_in, route_in, routet_in, cnt_in
    T = TOK_TILE
    nq = x_ref.shape[0] // ck_ref.shape[0]
    x = x_ref[...]
    xn_b = _rms_norm(x, g1_ref[...]).astype(BF16)
    zq = jnp.dot(xn_b, wq_ref[...], preferred_element_type=F32).astype(BF16)
    zkv = jnp.dot(xn_b, w_in_ref[:, COL_KV:COL_U], preferred_element_type=F32)
    kvnew_ref[...] = zkv
    zkv_b = zkv.astype(BF16)
    n_cache = ck_ref.shape[1]
    key_pad = jnp.zeros((KEY_PAD - n_cache - nq, D_KV), BF16)
    for b in range(T // nq):
        r0 = b * nq
        kk = jnp.concatenate([ck_ref[b].astype(BF16), zkv_b[r0:r0 + nq, :D_KV], key_pad], axis=0)
        vv = jnp.concatenate([cv_ref[b].astype(BF16), zkv_b[r0:r0 + nq, D_KV:], key_pad], axis=0)
        _attention_rows(zq, r0, nq, kk, vv, bias_ref, None, oatt_ref, heads_per_unit=REP)

    _layer_tail(x, xn_b, oatt_ref, osgu_ref, w_in_ref, sguw_ref, sgub_ref, lng_ref, lnb_ref,
                woa_ref, wos_ref, wout_ref, g2_ref, wr_ref, br_ref, tri_ref, low_ref,
                x1_ref, h_ref, route_ref, routet_ref, cnt_ref, vn_ref)


def _pack_slot_rows(x):
    lo = lax.bitcast_convert_type(x[:, :SLOT_WORDS].astype(BF16).astype(F32), U32)
    hi = lax.bitcast_convert_type(x[:, SLOT_WORDS:].astype(BF16).astype(F32), U32)
    return (lo >> 16) | hi


def _unpack_slot_rows(w):
    lo = lax.bitcast_convert_type(w << 16, F32)
    hi = lax.bitcast_convert_type(w & jnp.uint32(0xFFFF0000), F32)
    return jnp.concatenate([lo, hi], axis=1).astype(BF16)


def _as_granules(x):
    return x.reshape(x.shape[0] // GRAN, GRAN, SLOT_WORDS)


def _as_rows(x):
    return x.reshape(x.shape[0] * GRAN, SLOT_WORDS)


def _segment_copies(n_gran, src_gran, dst_gran, bits, make_copy, act):
    for b in range(min(bits, 2)):
        @pl.when(((n_gran >> b) & 1) == 1)
        def _(b=b):
            done = n_gran & ((1 << b) - 1)
            act(make_copy(src_gran + done, dst_gran + done, 1 << b))

    if bits > 2:
        def quad(q, carry):
            done = (n_gran & 3) + 4 * q
            act(make_copy(src_gran + done, dst_gran + done, 4))
            return carry
        lax.fori_loop(0, n_gran >> 2, quad, 0)


def _wait_granules(n_gran, bits, make_copy):
    for b in range(bits):
        @pl.when(((n_gran >> b) & 1) == 1)
        def _(b=b):
            make_copy(1 << b).wait()


def _dispatch_kernel(cnt_ref, loff_ref, goff_ref, tot_ref, padn_ref, padoff_ref,
                     route_ref, h_ref, xs_hbm, stage_ref, zero_ref, sem):
    i = pl.program_id(0)
    last = pl.num_programs(0) - 1
    slot = lax.rem(i, 2)
    T = TOK_TILE
    pos = route_ref[0, TOP_K:2 * TOP_K, :].astype(I32)
    r_iota = lax.broadcasted_iota(I32, (LOCAL_ROWS, T), 0)
    p = jnp.zeros((LOCAL_ROWS, T), F32)
    for k in range(TOP_K):
        p = jnp.where(r_iota == pos[k:k + 1, :], 1.0, p)
    stage_ref[slot] = _as_granules(
        _pack_slot_rows(jnp.dot(p.astype(BF16), h_ref[...], preferred_element_type=F32)))

    def seg_copy(s, d, n):
        return pltpu.make_async_copy(stage_ref.at[slot, pl.ds(s, n)], xs_hbm.at[pl.ds(d, n)], sem.at[slot])

    def start_segment(e, carry):
        t = i * N_EXPERTS + e
        _segment_copies(cnt_ref[t], loff_ref[t], goff_ref[t], SEG_BITS, seg_copy, lambda cp: cp.start())
        return carry

    lax.fori_loop(0, N_EXPERTS, start_segment, 0)

    def wait_tile(tile, tile_slot):
        _wait_granules(tot_ref[tile], TILE_BITS, lambda n: pltpu.make_async_copy(
            stage_ref.at[tile_slot, pl.ds(0, n)], xs_hbm.at[pl.ds(0, n)], sem.at[tile_slot]))

    @pl.when(i > 0)
    def _():
        wait_tile(i - 1, 1 - slot)

    @pl.when(i == last)
    def _():
        wait_tile(i, slot)
        zero_ref[...] = jnp.zeros(zero_ref.shape, U32)

        def pad_copy(s, d, n):
            return pltpu.make_async_copy(zero_ref.at[pl.ds(s, n)], xs_hbm.at[pl.ds(d, n)], sem.at[2])

        def for_pads(act):
            def body(e, carry):
                _segment_copies(padn_ref[e], 0, padoff_ref[e], PAD_BITS, pad_copy, act)
                return carry
            lax.fori_loop(0, N_EXPERTS, body, 0)

        for_pads(lambda cp: cp.start())
        for_pads(lambda cp: cp.wait())


def _expert_kernel(blk0_ref, nblk_ref, xs_hbm, w1_ref, b1_ref, w2_ref, b2_ref, ys_hbm,
                   w1b_ref, w2b_ref, xbuf_ref, ybuf_ref, sem_in, sem_out):
    e = pl.program_id(0)
    nb = nblk_ref[e]
    blk_gran = ROW_BLOCK // GRAN
    gran0 = blk0_ref[e] * blk_gran

    def in_copy(k, slot):
        return pltpu.make_async_copy(xs_hbm.at[pl.ds(gran0 + k * blk_gran, blk_gran)], xbuf_ref.at[slot],
                                     sem_in.at[slot])

    def out_copy(k, slot):
        return pltpu.make_async_copy(ybuf_ref.at[slot], ys_hbm.at[pl.ds(gran0 + k * blk_gran, blk_gran)],
                                     sem_out.at[slot])

    @pl.when(nb > 0)
    def _():
        in_copy(0, 0).start(priority=BLOCK_DMA_PRIORITY)

    w1b_ref[...] = w1_ref[0].astype(BF16)
    w2b_ref[...] = w2_ref[0].astype(BF16)

    def block(k, carry):
        slot = lax.rem(k, 2)

        @pl.when(k + 1 < nb)
        def _():
            in_copy(k + 1, 1 - slot).start(priority=BLOCK_DMA_PRIORITY)

        in_copy(k, slot).wait()

        @pl.when(k >= 2)
        def _():
            out_copy(k - 2, slot).wait()

        x = _unpack_slot_rows(_as_rows(xbuf_ref[slot]))
        h1 = jnp.dot(x, w1b_ref[...], preferred_element_type=F32) + b1_ref[0]
        gate = jnp.minimum(h1[:, :D_FF], SWIGLU_LIMIT)
        up = jnp.clip(h1[:, D_FF:], -SWIGLU_LIMIT, SWIGLU_LIMIT)
        act = gate * jax.nn.sigmoid(SWIGLU_ALPHA * gate) * (up + 1.0)
        y = jnp.dot(act.astype(BF16), w2b_ref[...], preferred_element_type=F32) + b2_ref[0]
        ybuf_ref[slot] = _as_granules(_pack_slot_rows(y))
        out_copy(k, slot).start(priority=BLOCK_DMA_PRIORITY)
        return carry

    lax.fori_loop(0, nb, block, 0)

    @pl.when(nb >= 2)
    def _():
        out_copy(nb - 2, lax.rem(nb, 2)).wait()

    @pl.when(nb >= 1)
    def _():
        out_copy(nb - 1, lax.rem(nb + 1, 2)).wait()


def _combine_kernel(n_prompt_tiles, cnt_ref, loff_ref, goff_ref, tot_ref,
                    routet_ref, x1_ref, gf_ref, ys_hbm, yp_ref, ysm_ref, stage_ref, sem):
    i = pl.program_id(0)
    slot = lax.rem(i, 2)
    T = TOK_TILE

    def start_tile(tile, tile_slot):
        def seg_copy(s, d, n):
            return pltpu.make_async_copy(ys_hbm.at[pl.ds(d, n)], stage_ref.at[tile_slot, pl.ds(s, n)],
                                         sem.at[tile_slot])

        def body(e, carry):
            t = tile * N_EXPERTS + e
            _segment_copies(cnt_ref[t], loff_ref[t], goff_ref[t], SEG_BITS, seg_copy, lambda cp: cp.start())
            return carry
        lax.fori_loop(0, N_EXPERTS, body, 0)

    @pl.when(i == 0)
    def _():
        stage_ref[...] = jnp.zeros(stage_ref.shape, U32)
        start_tile(i, slot)

    @pl.when(i + 1 < pl.num_programs(0))
    def _():
        start_tile(i + 1, 1 - slot)

    rt = routet_ref[...]
    l_iota = lax.broadcasted_iota(I32, (T, LOCAL_ROWS), 1)
    pw = jnp.zeros((T, LOCAL_ROWS), F32)
    for k in range(TOP_K):
        pos_k = rt[:, TOP_K + k:TOP_K + k + 1].astype(I32)
        pw = jnp.where(l_iota == pos_k, rt[:, 2 * TOP_K + k:2 * TOP_K + k + 1], pw)

    _wait_granules(tot_ref[i], TILE_BITS, lambda n: pltpu.make_async_copy(
        ys_hbm.at[pl.ds(0, n)], stage_ref.at[slot, pl.ds(0, n)], sem.at[slot]))
    moe = jnp.dot(pw.astype(BF16), _unpack_slot_rows(_as_rows(stage_ref[slot])), preferred_element_type=F32)
    y = _rms_norm(x1_ref[...] + moe, gf_ref[...])

    @pl.when(i < n_prompt_tiles)
    def _():
        yp_ref[...] = y

    @pl.when(i >= n_prompt_tiles)
    def _():
        ysm_ref[...] = y


def _t5_bucket(rel):
    half = NUM_BUCKETS // 2
    max_exact = half // 2
    ret = jnp.where(rel > 0, half, 0)
    n = jnp.abs(rel)
    nf = jnp.maximum(n, 1).astype(F32)
    large = max_exact + (jnp.log(nf / max_exact) / math.log(MAX_DISTANCE / max_exact)
                         * (half - max_exact)).astype(jnp.int32)
    large = jnp.minimum(large, half - 1)
    return ret + jnp.where(n < max_exact, n, large)


def _stacked_bias(table, sinks, q_pos, k_pos):
    nq, nk = q_pos.shape[0], k_pos.shape[0]
    bucket = _t5_bucket(k_pos[None, :] - q_pos[:, None])
    onehot = (bucket[:, :, None] == jnp.arange(NUM_BUCKETS)).astype(F32)
    bias = jnp.einsum('qkb,bh->hqk', onehot, table.astype(F32), precision=lax.Precision.HIGHEST)
    bias = bias.reshape(N_KV, REP * nq, nk)
    sink = jnp.repeat(sinks.astype(F32).reshape(N_KV, REP, 1), nq, axis=2).reshape(N_KV, REP * nq, 1)
    tail = jnp.full((N_KV, REP * nq, KEY_PAD - nk - 1), -jnp.inf, F32)
    return jnp.concatenate([bias, sink, tail], axis=-1)


def kernel(x_prompt, x_sample, cache_win_k, cache_win_v, norm1_g, w_in, attn_sinks, rel_bias_table, sgu_ln_g, sgu_ln_b, sgu_w, sgu_b, w_o_att, w_o_sgu, w_out, norm2_g, w_router, b_router, w_exp_in, b_exp_in, w_exp_out, b_exp_out, final_norm_g):
    batch, seq, _ = x_prompt.shape
    dec_batch, dec_seq, _ = x_sample.shape
    cache_rows = cache_win_k.shape[2]
    assert x_prompt.shape[2] == D_MODEL and w_in.shape == (1, D_MODEL, D_IN)
    assert seq % TOK_TILE == 0 and TOK_TILE % GMLP_CHUNK == 0 and TOK_TILE >= WINDOW
    assert TOK_TILE % dec_seq == 0 and (dec_batch * dec_seq) % TOK_TILE == 0
    assert dec_seq <= GMLP_CHUNK and GMLP_CHUNK % dec_seq == 0 and cache_rows == WINDOW
    T = TOK_TILE
    n_prompt = batch * seq
    n_sample = dec_batch * dec_seq
    n_tok = n_prompt + n_sample
    tiles_per_seq = seq // T
    n_ptiles = n_prompt // T
    n_stiles = n_sample // T
    n_tiles = n_ptiles + n_stiles
    seqs_per_tile = T // dec_seq

    w_q = w_in[0][:, :D_ATT].reshape(D_MODEL, N_KV, REP, HEAD_DIM).transpose(0, 2, 1, 3).reshape(D_MODEL, D_ATT)
    w_q_b = w_q.astype(BF16)
    w_in_b = w_in[0][:, D_ATT:].astype(BF16)
    woa_b = w_o_att[0].reshape(N_KV, REP, HEAD_DIM, D_MODEL).transpose(1, 0, 2, 3).reshape(
        D_ATT, D_MODEL).astype(BF16)
    wos_b = w_o_sgu[0].astype(BF16)
    wout_b = w_out[0].astype(BF16)
    wr_b = jnp.pad(w_router[0], ((0, 0), (0, LANES - N_EXPERTS))).astype(BF16)
    br_col = b_router[0].astype(F32).reshape(N_EXPERTS, 1)
    g1 = norm1_g[0].reshape(1, D_MODEL)
    g2 = norm2_g[0].reshape(1, D_MODEL)
    gf = final_norm_g.reshape(1, D_MODEL)
    lng = sgu_ln_g[0].reshape(1, D_GMLP)
    lnb = sgu_ln_b[0].reshape(1, D_GMLP)
    tril = jnp.tril(jnp.ones((GMLP_CHUNK, GMLP_CHUNK), dtype=bool))
    sguw_p = jnp.where(tril[None], sgu_w[0], 0).astype(BF16)
    sgub_p = jnp.broadcast_to(sgu_b[0][:, :, None], (N_GROUPS, GMLP_CHUNK, GROUP_W)).astype(F32)
    reps = GMLP_CHUNK // dec_seq
    corner = jnp.where(tril[None, :dec_seq, :dec_seq], sgu_w[0][:, :dec_seq, :dec_seq], 0)
    sguw_s = jnp.einsum('ab,gij->gaibj', jnp.eye(reps, dtype=F32), corner).reshape(
        N_GROUPS, GMLP_CHUNK, GMLP_CHUNK).astype(BF16)
    sgub_s = jnp.broadcast_to(jnp.tile(sgu_b[0][:, :dec_seq], (1, reps))[:, :, None],
                              (N_GROUPS, GMLP_CHUNK, GROUP_W)).astype(F32)
    bias_p = _stacked_bias(rel_bias_table, attn_sinks[0], jnp.arange(CHUNK) + WINDOW, jnp.arange(BAND))
    bias_s = _stacked_bias(rel_bias_table, attn_sinks[0], cache_rows + jnp.arange(dec_seq),
                           jnp.arange(cache_rows + dec_seq))
    tri = jnp.triu(jnp.ones((T, T), F32), k=1).astype(BF16)
    low = jnp.tril(jnp.ones((N_EXPERTS, N_EXPERTS), F32), k=-1).astype(BF16)

    layer_out_shapes = (
        jax.ShapeDtypeStruct((n_tok, D_MODEL), F32),
        jax.ShapeDtypeStruct((n_tok, D_MODEL), BF16),
        jax.ShapeDtypeStruct((n_tiles, ROUTE_ROWS, T), F32),
        jax.ShapeDtypeStruct((n_tok, LANES), F32),
        jax.ShapeDtypeStruct((n_tiles, N_EXPERTS, LANES), F32),
    )
    shared_consts = (w_q_b, w_in_b)
    tail_consts_p = (sguw_p, sgub_p, lng, lnb, woa_b, wos_b, wout_b, g2, wr_b, br_col, tri, low)
    tail_consts_s = (sguw_s, sgub_s, lng, lnb, woa_b, wos_b, wout_b, g2, wr_b, br_col, tri, low)

    def tile_specs(tile_of):
        return [
            pl.BlockSpec((T, D_MODEL), lambda *g: (tile_of(*g), 0)),
            pl.BlockSpec((T, D_MODEL), lambda *g: (tile_of(*g), 0)),
            pl.BlockSpec((1, ROUTE_ROWS, T), lambda *g: (tile_of(*g), 0, 0)),
            pl.BlockSpec((T, LANES), lambda *g: (tile_of(*g), 0)),
            pl.BlockSpec((1, N_EXPERTS, LANES), lambda *g: (tile_of(*g), 0, 0)),
        ]

    prompt_consts = (g1,) + shared_consts + (bias_p,) + tail_consts_p
    x1, h, route, routet, cnt, kvwin = pl.pallas_call(
        _prompt_layer_kernel,
        grid=(batch, tiles_per_seq),
        in_specs=[pl.BlockSpec((1, T, D_MODEL), lambda b, j: (b, j, 0))]
                 + [_const_spec(c.shape) for c in prompt_consts],
        out_specs=tile_specs(lambda b, j: b * tiles_per_seq + j)
                  + [pl.BlockSpec((1, WINDOW, 2 * D_KV), lambda b, j: (b, 0, 0))],
        out_shape=layer_out_shapes + (jax.ShapeDtypeStruct((batch, WINDOW, 2 * D_KV), F32),),
        scratch_shapes=[pltpu.VMEM((T + WINDOW, 2 * D_KV), BF16),
                        pltpu.VMEM((T, D_ATT), BF16),
                        pltpu.VMEM((T, D_GMLP), BF16)],
        compiler_params=pltpu.CompilerParams(dimension_semantics=("arbitrary", "arbitrary"),
                                             vmem_limit_bytes=VMEM_LIMIT),
        name="layer_prompt",
    )(x_prompt, *prompt_consts)

    xs_flat = x_sample.reshape(n_sample, D_MODEL)
    ck = cache_win_k[0].reshape(dec_batch, cache_rows, D_KV)
    cv = cache_win_v[0].reshape(dec_batch, cache_rows, D_KV)
    sample_consts = (g1,) + shared_consts + (bias_s,) + tail_consts_s
    any_spec = pl.BlockSpec(memory_space=pl.ANY)
    x1, h, route, routet, cnt, kvnew, vn_s = pl.pallas_call(
        _sample_layer_kernel,
        grid=(n_stiles,),
        in_specs=[any_spec] * 5
                 + [pl.BlockSpec((T, D_MODEL), lambda i: (i, 0)),
                    pl.BlockSpec((seqs_per_tile, cache_rows, D_KV), lambda i: (i, 0, 0)),
                    pl.BlockSpec((seqs_per_tile, cache_rows, D_KV), lambda i: (i, 0, 0))]
                 + [_const_spec(c.shape) for c in sample_consts],
        out_specs=tile_specs(lambda i: n_ptiles + i)
                  + [pl.BlockSpec((T, 2 * D_KV), lambda i: (i, 0)),
                     pl.BlockSpec((T, D_GMLP), lambda i: (i, 0))],
        out_shape=layer_out_shapes + (jax.ShapeDtypeStruct((n_sample, 2 * D_KV), F32),
                                      jax.ShapeDtypeStruct((n_sample, D_GMLP), F32)),
        scratch_shapes=[pltpu.VMEM((T, D_ATT), BF16), pltpu.VMEM((T, D_GMLP), BF16)],
        input_output_aliases={0: 0, 1: 1, 2: 2, 3: 3, 4: 4},
        compiler_params=pltpu.CompilerParams(dimension_semantics=("arbitrary",),
                                             vmem_limit_bytes=VMEM_LIMIT),
        name="layer_sample",
    )(x1, h, route, routet, cnt, xs_flat, ck, cv, *sample_consts)

    blk_gran = ROW_BLOCK // GRAN
    counts = cnt[:, :, 0].astype(I32)
    seg_gran = (counts + (GRAN - 1)) // GRAN
    local_off = jnp.cumsum(seg_gran, axis=1) - seg_gran
    tot_gran = jnp.sum(seg_gran, axis=0)
    ptot_gran = (tot_gran + (blk_gran - 1)) // blk_gran * blk_gran
    pend_gran = jnp.cumsum(ptot_gran)
    gstart = pend_gran - ptot_gran
    global_off = gstart[None, :] + jnp.cumsum(seg_gran, axis=0) - seg_gran
    pad_n = ptot_gran - tot_gran
    pad_off = gstart + tot_gran
    n_rows = -(-(TOP_K * n_tok + n_tiles * N_EXPERTS * (GRAN - 1) + N_EXPERTS * (ROW_BLOCK - GRAN))
               // ROW_BLOCK) * ROW_BLOCK
    blk0 = (gstart // blk_gran).astype(I32)
    nblk = (ptot_gran // blk_gran).astype(I32)
    tile_gran = jnp.sum(seg_gran, axis=1).astype(I32)
    seg_gran_f = seg_gran.reshape(-1).astype(I32)
    local_off_f = local_off.reshape(-1).astype(I32)
    global_off_f = global_off.reshape(-1).astype(I32)

    xs = pl.pallas_call(
        _dispatch_kernel,
        grid_spec=pltpu.PrefetchScalarGridSpec(
            num_scalar_prefetch=6,
            grid=(n_tiles,),
            in_specs=[pl.BlockSpec((1, ROUTE_ROWS, T), lambda i, *_: (i, 0, 0)),
                      pl.BlockSpec((T, D_MODEL), lambda i, *_: (i, 0))],
            out_specs=pl.BlockSpec(memory_space=pl.ANY),
            scratch_shapes=[pltpu.VMEM((2, LOCAL_ROWS // GRAN, GRAN, SLOT_WORDS), U32),
                            pltpu.VMEM((ROW_BLOCK // GRAN, GRAN, SLOT_WORDS), U32),
                            pltpu.SemaphoreType.DMA((3,))]),
        out_shape=jax.ShapeDtypeStruct((n_rows // GRAN, GRAN, SLOT_WORDS), U32),
        compiler_params=pltpu.CompilerParams(dimension_semantics=("arbitrary",),
                                             vmem_limit_bytes=VMEM_LIMIT),
        name="moe_dispatch",
    )(seg_gran_f, local_off_f, global_off_f, tile_gran, pad_n.astype(I32), pad_off.astype(I32), route, h)

    ys = pl.pallas_call(
        _expert_kernel,
        grid_spec=pltpu.PrefetchScalarGridSpec(
            num_scalar_prefetch=2,
            grid=(N_EXPERTS,),
            in_specs=[pl.BlockSpec(memory_space=pl.ANY),
                      pl.BlockSpec((1, D_MODEL, 2 * D_FF), lambda e, *_: (e, 0, 0)),
                      pl.BlockSpec((1, 1, 2 * D_FF), lambda e, *_: (e, 0, 0)),
                      pl.BlockSpec((1, D_FF, D_MODEL), lambda e, *_: (e, 0, 0)),
                      pl.BlockSpec((1, 1, D_MODEL), lambda e, *_: (e, 0, 0))],
            out_specs=pl.BlockSpec(memory_space=pl.ANY),
            scratch_shapes=[pltpu.VMEM((D_MODEL, 2 * D_FF), BF16), pltpu.VMEM((D_FF, D_MODEL), BF16),
                            pltpu.VMEM((2, ROW_BLOCK // GRAN, GRAN, SLOT_WORDS), U32),
                            pltpu.VMEM((2, ROW_BLOCK // GRAN, GRAN, SLOT_WORDS), U32),
                            pltpu.SemaphoreType.DMA((2,)), pltpu.SemaphoreType.DMA((2,))]),
        out_shape=jax.ShapeDtypeStruct((n_rows // GRAN, GRAN, SLOT_WORDS), U32),
        compiler_params=pltpu.CompilerParams(dimension_semantics=("arbitrary",),
                                             vmem_limit_bytes=VMEM_LIMIT),
        name="moe_experts",
    )(blk0, nblk, xs, w_exp_in[0], b_exp_in[0].reshape(N_EXPERTS, 1, 2 * D_FF),
      w_exp_out[0], b_exp_out[0].reshape(N_EXPERTS, 1, D_MODEL))

    y_p, y_s = pl.pallas_call(
        functools.partial(_combine_kernel, n_ptiles),
        grid_spec=pltpu.PrefetchScalarGridSpec(
            num_scalar_prefetch=4,
            grid=(n_tiles,),
            in_specs=[pl.BlockSpec((T, LANES), lambda i, *_: (i, 0)),
                      pl.BlockSpec((T, D_MODEL), lambda i, *_: (i, 0)),
                      pl.BlockSpec((1, D_MODEL), lambda i, *_: (0, 0)),
                      pl.BlockSpec(memory_space=pl.ANY)],
            out_specs=[pl.BlockSpec((T, D_MODEL), lambda i, *_: (jnp.minimum(i, n_ptiles - 1), 0)),
                       pl.BlockSpec((T, D_MODEL), lambda i, *_: (jnp.maximum(i - n_ptiles, 0), 0))],
            scratch_shapes=[pltpu.VMEM((2, LOCAL_ROWS // GRAN, GRAN, SLOT_WORDS), U32), pltpu.SemaphoreType.DMA((2,))]),
        out_shape=(jax.ShapeDtypeStruct((n_prompt, D_MODEL), F32),
                   jax.ShapeDtypeStruct((n_sample, D_MODEL), F32)),
        compiler_params=pltpu.CompilerParams(dimension_semantics=("arbitrary",),
                                             vmem_limit_bytes=VMEM_LIMIT),
        name="moe_combine",
    )(seg_gran_f, local_off_f, global_off_f, tile_gran, routet, x1, gf, ys)

    y_prompt = y_p.reshape(batch, seq, D_MODEL)
    y_sample = y_s.reshape(dec_batch, dec_seq, D_MODEL)
    new_win_k_prompt = kvwin[:, :, :D_KV].reshape(1, batch, WINDOW, N_KV, HEAD_DIM)
    new_win_v_prompt = kvwin[:, :, D_KV:].reshape(1, batch, WINDOW, N_KV, HEAD_DIM)
    new_win_k_sample = kvnew[:, :D_KV].reshape(1, dec_batch, dec_seq, N_KV, HEAD_DIM)
    new_win_v_sample = kvnew[:, D_KV:].reshape(1, dec_batch, dec_seq, N_KV, HEAD_DIM)
    new_sgu_v_sample = vn_s.reshape(1, dec_batch, dec_seq, D_GMLP)
    return (y_prompt, y_sample, new_win_k_prompt, new_win_v_prompt, new_win_k_sample,
            new_win_v_sample, new_sgu_v_sample)
```

```python
import functools
import math

import numpy as np
import jax
import jax.numpy as jnp
from jax import lax
from jax.experimental import pallas as pl
from jax.experimental.pallas import tpu as pltpu

F32 = jnp.float32
BF16 = jnp.bfloat16
I32 = jnp.int32
U32 = jnp.uint32

D_MODEL = 1024
HEAD_DIM = 64
N_HEADS = 16
N_KV = 2
REP = N_HEADS // N_KV
CHUNK = 64
WINDOW = 128
BAND = WINDOW + CHUNK
KEY_PAD = 256
D_ATT = N_HEADS * HEAD_DIM
D_KV = N_KV * HEAD_DIM
NUM_BUCKETS = 32
MAX_DISTANCE = 128
GMLP_CHUNK = 128
D_GMLP = 1024
N_GROUPS = 4
GROUP_W = D_GMLP // N_GROUPS
N_EXPERTS = 32
TOP_K = 4
D_FF = 1024
SWIGLU_LIMIT = 7.0
SWIGLU_ALPHA = 1.702
NORM_EPS = 1e-5
D_IN = D_ATT + 2 * D_KV + 2 * D_GMLP + 2 * D_MODEL
COL_KV = D_ATT
COL_U = COL_KV + 2 * D_KV
COL_VG = COL_U + D_GMLP
COL_GA = COL_VG + D_GMLP
COL_GB = COL_GA + D_MODEL
SQRT_HALF = float(np.sqrt(0.5))

LANES = 128
WORD_SUBLANES = 8
VMEM_LIMIT = 56 * 1024 * 1024

TOK_TILE = 256
GRAN = WORD_SUBLANES
SLOT_WORDS = D_MODEL // 2
SEG_BITS = (TOK_TILE // GRAN).bit_length()
LOCAL_ROWS = TOP_K * TOK_TILE + N_EXPERTS * GRAN
ROW_BLOCK = 128
PAD_BITS = (ROW_BLOCK // GRAN - 1).bit_length()
TILE_BITS = (LOCAL_ROWS // GRAN).bit_length()
BLOCK_DMA_PRIORITY = 1
ROUTE_ROWS = 16


def _const_spec(shape):
    nd = len(shape)
    return pl.BlockSpec(shape, lambda *_: (0,) * nd, pipeline_mode=pl.Buffered(1))


def _rms_norm(x, g):
    ms = jnp.mean(x * x, axis=-1, keepdims=True)
    return x * lax.rsqrt(ms + NORM_EPS) * g


def _gelu(x):
    return 0.5 * x * (1.0 + lax.erf(x * SQRT_HALF))


def _attend(q, k, v, bias, valid):
    logits = lax.dot_general(q, k, (((1,), (1,)), ((), ())), preferred_element_type=F32) + bias
    if valid is not None:
        logits = jnp.where(valid, logits, -jnp.inf)
    m = jnp.max(logits, axis=-1, keepdims=True)
    p = jnp.exp(logits - m)
    den = jnp.sum(p, axis=-1, keepdims=True)
    w = (p * (1.0 / den)).astype(BF16)
    return jnp.dot(w, v, preferred_element_type=F32)


def _attention_rows(zq, row0, n_rows, k_ext, v_ext, bias_ref, valid, oatt_ref, heads_per_unit):
    lane = lax.broadcasted_iota(I32, (1, LANES), 1)
    for g in range(N_KV):
        in_group = (lane >= g * HEAD_DIM) & (lane < (g + 1) * HEAD_DIM)
        qmask = jnp.where(in_group, HEAD_DIM ** -0.5, 0.0).astype(BF16)
        for part in range(REP // heads_per_unit):
            tiles = range(part * heads_per_unit, (part + 1) * heads_per_unit)
            q = jnp.concatenate([zq[row0:row0 + n_rows, t * LANES:(t + 1) * LANES] * qmask for t in tiles],
                                axis=0)
            b0 = part * heads_per_unit * n_rows
            o = _attend(q, k_ext, v_ext, bias_ref[g, b0:b0 + heads_per_unit * n_rows, :], valid)
            for u, t in enumerate(tiles):
                c0 = t * LANES + g * HEAD_DIM
                oatt_ref[row0:row0 + n_rows, c0:c0 + HEAD_DIM] = (
                    o[u * n_rows:(u + 1) * n_rows, g * HEAD_DIM:(g + 1) * HEAD_DIM].astype(BF16))


def _layer_tail(x, xn_b, oatt_ref, osgu_ref, w_in_ref, sguw_ref, sgub_ref, lng_ref, lnb_ref,
                woa_ref, wos_ref, wout_ref, g2_ref, wr_ref, br_ref, tri_ref, low_ref,
                x1_ref, h_ref, route_ref, routet_ref, cnt_ref, vn_ref):
    T = x.shape[0]
    u = _gelu(jnp.dot(xn_b, w_in_ref[:, COL_U:COL_VG], preferred_element_type=F32))
    vg = _gelu(jnp.dot(xn_b, w_in_ref[:, COL_VG:COL_GA], preferred_element_type=F32))
    mu = jnp.mean(vg, axis=-1, keepdims=True)
    var = jnp.mean(jnp.square(vg - mu), axis=-1, keepdims=True)
    vn = (vg - mu) * lax.rsqrt(var + NORM_EPS) * lng_ref[...] + lnb_ref[...]
    if vn_ref is not None:
        vn_ref[...] = vn
    vn_b = vn.astype(BF16)
    for j in range(T // GMLP_CHUNK):
        rows = slice(j * GMLP_CHUNK, (j + 1) * GMLP_CHUNK)
        for g in range(N_GROUPS):
            cols = slice(g * GROUP_W, (g + 1) * GROUP_W)
            mixed = jnp.dot(sguw_ref[g], vn_b[rows, cols], preferred_element_type=F32) + sgub_ref[g]
            osgu_ref[rows, cols] = (u[rows, cols] * mixed).astype(BF16)

    ga = jax.nn.sigmoid(jnp.dot(xn_b, w_in_ref[:, COL_GA:COL_GB], preferred_element_type=F32))
    merged = ga * jnp.dot(oatt_ref[...], woa_ref[...], preferred_element_type=F32)
    gb = jax.nn.sigmoid(jnp.dot(xn_b, w_in_ref[:, COL_GB:D_IN], preferred_element_type=F32))
    merged = merged + gb * jnp.dot(osgu_ref[...], wos_ref[...], preferred_element_type=F32)
    x1 = x + jnp.dot(merged.astype(BF16), wout_ref[...], preferred_element_type=F32)
    x1_ref[...] = x1

    h_b = _rms_norm(x1, g2_ref[...]).astype(BF16)
    h_ref[...] = h_b
    logits = jnp.dot(h_b, wr_ref[...], preferred_element_type=F32)
    lt = jnp.transpose(logits)[:N_EXPERTS, :] + br_ref[...]
    e_iota = lax.broadcasted_iota(I32, (N_EXPERTS, T), 0).astype(F32)
    cur = lt
    vals, idxs = [], []
    for _ in range(TOP_K):
        m = jnp.max(cur, axis=0, keepdims=True)
        ik = jnp.min(jnp.where(cur == m, e_iota, float(N_EXPERTS)), axis=0, keepdims=True)
        vals.append(m)
        idxs.append(ik)
        cur = jnp.where(e_iota == ik, -jnp.inf, cur)
    exps = [jnp.exp(v - vals[0]) for v in vals]
    den = exps[0] + exps[1] + exps[2] + exps[3]
    gates = [e / den for e in exps]

    onehot = jnp.zeros((N_EXPERTS, T), F32)
    for ik in idxs:
        onehot = onehot + jnp.where(e_iota == ik, 1.0, 0.0)
    rank = jnp.dot(onehot.astype(BF16), tri_ref[...], preferred_element_type=F32)
    cnt = jnp.sum(onehot, axis=1, keepdims=True)
    cnt_b = jnp.broadcast_to(cnt, (N_EXPERTS, LANES))
    gran = jnp.floor((cnt_b + (GRAN - 1)) * (1.0 / GRAN))
    off = jnp.dot(low_ref[...], gran.astype(BF16), preferred_element_type=F32) * GRAN
    base = off[:, 0:1] + rank
    poss = [jnp.sum(jnp.where(e_iota == ik, base, 0.0), axis=0, keepdims=True) for ik in idxs]

    rec = jnp.concatenate(idxs + poss + gates
                          + [jnp.zeros((ROUTE_ROWS - 3 * TOP_K, T), F32)], axis=0)
    route_ref[0] = rec
    rec_pad = jnp.concatenate([rec, jnp.zeros((LANES - ROUTE_ROWS, T), F32)], axis=0)
    routet_ref[...] = jnp.transpose(rec_pad)
    cnt_ref[0] = cnt_b


def _prompt_layer_kernel(x_ref, g1_ref, wq_ref, w_in_ref, bias_ref, sguw_ref, sgub_ref, lng_ref,
                         lnb_ref, woa_ref, wos_ref, wout_ref, g2_ref, wr_ref, br_ref, tri_ref, low_ref,
                         x1_ref, h_ref, route_ref, routet_ref, cnt_ref, kvwin_ref,
                         kvx_ref, oatt_ref, osgu_ref):
    T = TOK_TILE
    j = pl.program_id(1)
    x = x_ref[0]
    xn_b = _rms_norm(x, g1_ref[...]).astype(BF16)
    zq = jnp.dot(xn_b, wq_ref[...], preferred_element_type=F32).astype(BF16)
    zkv = jnp.dot(xn_b, w_in_ref[:, COL_KV:COL_U], preferred_element_type=F32)

    @pl.when(j == pl.num_programs(1) - 1)
    def _():
        kvwin_ref[0] = zkv[T - WINDOW:, :]

    @pl.when(j == 0)
    def _():
        kvx_ref[0:WINDOW, :] = jnp.zeros((WINDOW, 2 * D_KV), BF16)

    @pl.when(j > 0)
    def _():
        kvx_ref[0:WINDOW, :] = kvx_ref[T:T + WINDOW, :]

    kvx_ref[WINDOW:, :] = zkv.astype(BF16)

    col = lax.broadcasted_iota(I32, (1, KEY_PAD), 1)
    key_pad = jnp.zeros((KEY_PAD - BAND, 2 * D_KV), BF16)
    for c in range(T // CHUNK):
        r0 = c * CHUNK
        kvb = jnp.concatenate([kvx_ref[r0:r0 + BAND, :], key_pad], axis=0)
        valid = ((col + (j * T + r0 - WINDOW)) >= 0) | (col >= BAND) if r0 < WINDOW else None
        _attention_rows(zq, r0, CHUNK, kvb[:, :D_KV], kvb[:, D_KV:], bias_ref, valid, oatt_ref,
                        heads_per_unit=4)

    _layer_tail(x, xn_b, oatt_ref, osgu_ref, w_in_ref, sguw_ref, sgub_ref, lng_ref, lnb_ref,
                woa_ref, wos_ref, wout_ref, g2_ref, wr_ref, br_ref, tri_ref, low_ref,
                x1_ref, h_ref, route_ref, routet_ref, cnt_ref, None)


def _sample_layer_kernel(x1_in, h_in, route_in, routet_in, cnt_in,
                         x_ref, ck_ref, cv_ref, g1_ref, wq_ref, w_in_ref, bias_ref, sguw_ref, sgub_ref,
                         lng_ref, lnb_ref, woa_ref, wos_ref, wout_ref, g2_ref, wr_ref, br_ref, tri_ref,
                         low_ref,
                         x1_ref, h_ref, route_ref, routet_ref, cnt_ref, kvnew_ref, vn_ref,
                         oatt_ref, osgu_ref):
    del x1_in, h_in, route_in, routet_in, cnt_in
    T = TOK_TILE
    nq = x_ref.shape[0] // ck_ref.shape[0]
    x = x_ref[...]
    xn_b = _rms_norm(x, g1_ref[...]).astype(BF16)
    zq = jnp.dot(xn_b, wq_ref[...], preferred_element_type=F32).astype(BF16)
    zkv = jnp.dot(xn_b, w_in_ref[:, COL_KV:COL_U], preferred_element_type=F32)
    kvnew_ref[...] = zkv
    zkv_b = zkv.astype(BF16)
    n_cache = ck_ref.shape[1]
    key_pad = jnp.zeros((KEY_PAD - n_cache - nq, D_KV), BF16)
    for b in range(T // nq):
        r0 = b * nq
        kk = jnp.concatenate([ck_ref[b].astype(BF16), zkv_b[r0:r0 + nq, :D_KV], key_pad], axis=0)
        vv = jnp.concatenate([cv_ref[b].astype(BF16), zkv_b[r0:r0 + nq, D_KV:], key_pad], axis=0)
        _attention_rows(zq, r0, nq, kk, vv, bias_ref, None, oatt_ref, heads_per_unit=REP)

    _layer_tail(x, xn_b, oatt_ref, osgu_ref, w_in_ref, sguw_ref, sgub_ref, lng_ref, lnb_ref,
                woa_ref, wos_ref, wout_ref, g2_ref, wr_ref, br_ref, tri_ref, low_ref,
                x1_ref, h_ref, route_ref, routet_ref, cnt_ref, vn_ref)


def _pack_slot_rows(x):
    lo = lax.bitcast_convert_type(x[:, :SLOT_WORDS].astype(BF16).astype(F32), U32)
    hi = lax.bitcast_convert_type(x[:, SLOT_WORDS:].astype(BF16).astype(F32), U32)
    return (lo >> 16) | hi


def _unpack_slot_rows(w):
    lo = lax.bitcast_convert_type(w << 16, F32)
    hi = lax.bitcast_convert_type(w & jnp.uint32(0xFFFF0000), F32)
    return jnp.concatenate([lo, hi], axis=1).astype(BF16)


def _as_granules(x):
    return x.reshape(x.shape[0] // GRAN, GRAN, SLOT_WORDS)


def _as_rows(x):
    return x.reshape(x.shape[0] * GRAN, SLOT_WORDS)


def _segment_copies(n_gran, src_gran, dst_gran, bits, make_copy, act):
    for b in range(min(bits, 2)):
        @pl.when(((n_gran >> b) & 1) == 1)
        def _(b=b):
            done = n_gran & ((1 << b) - 1)
            act(make_copy(src_gran + done, dst_gran + done, 1 << b))

    if bits > 2:
        def quad(q, carry):
            done = (n_gran & 3) + 4 * q
            act(make_copy(src_gran + done, dst_gran + done, 4))
            return carry
        lax.fori_loop(0, n_gran >> 2, quad, 0)


def _wait_granules(n_gran, bits, make_copy):
    for b in range(bits):
        @pl.when(((n_gran >> b) & 1) == 1)
        def _(b=b):
            make_copy(1 << b).wait()


def _dispatch_kernel(cnt_ref, loff_ref, goff_ref, tot_ref, padn_ref, padoff_ref,
                     route_ref, h_ref, xs_hbm, stage_ref, zero_ref, sem):
    i = pl.program_id(0)
    last = pl.num_programs(0) - 1
    slot = lax.rem(i, 2)
    T = TOK_TILE
    pos = route_ref[0, TOP_K:2 * TOP_K, :].astype(I32)
    r_iota = lax.broadcasted_iota(I32, (LOCAL_ROWS, T), 0)
    p = jnp.zeros((LOCAL_ROWS, T), F32)
    for k in range(TOP_K):
        p = jnp.where(r_iota == pos[k:k + 1, :], 1.0, p)
    stage_ref[slot] = _as_granules(
        _pack_slot_rows(jnp.dot(p.astype(BF16), h_ref[...], preferred_element_type=F32)))

    def seg_copy(s, d, n):
        return pltpu.make_async_copy(stage_ref.at[slot, pl.ds(s, n)], xs_hbm.at[pl.ds(d, n)], sem.at[slot])

    def start_segment(e, carry):
        t = i * N_EXPERTS + e
        _segment_copies(cnt_ref[t], loff_ref[t], goff_ref[t], SEG_BITS, seg_copy, lambda cp: cp.start())
        return carry

    lax.fori_loop(0, N_EXPERTS, start_segment, 0)

    def wait_tile(tile, tile_slot):
        _wait_granules(tot_ref[tile], TILE_BITS, lambda n: pltpu.make_async_copy(
            stage_ref.at[tile_slot, pl.ds(0, n)], xs_hbm.at[pl.ds(0, n)], sem.at[tile_slot]))

    @pl.when(i > 0)
    def _():
        wait_tile(i - 1, 1 - slot)

    @pl.when(i == last)
    def _():
        wait_tile(i, slot)
        zero_ref[...] = jnp.zeros(zero_ref.shape, U32)

        def pad_copy(s, d, n):
            return pltpu.make_async_copy(zero_ref.at[pl.ds(s, n)], xs_hbm.at[pl.ds(d, n)], sem.at[2])

        def for_pads(act):
            def body(e, carry):
                _segment_copies(padn_ref[e], 0, padoff_ref[e], PAD_BITS, pad_copy, act)
                return carry
            lax.fori_loop(0, N_EXPERTS, body, 0)

        for_pads(lambda cp: cp.start())
        for_pads(lambda cp: cp.wait())


def _expert_kernel(blk0_ref, nblk_ref, xs_hbm, w1_ref, b1_ref, w2_ref, b2_ref, ys_hbm,
                   w1b_ref, w2b_ref, xbuf_ref, ybuf_ref, sem_in, sem_out):
    e = pl.program_id(0)
    nb = nblk_ref[e]
    blk_gran = ROW_BLOCK // GRAN
    gran0 = blk0_ref[e] * blk_gran

    def in_copy(k, slot):
        return pltpu.make_async_copy(xs_hbm.at[pl.ds(gran0 + k * blk_gran, blk_gran)], xbuf_ref.at[slot],
                                     sem_in.at[slot])

    def out_copy(k, slot):
        return pltpu.make_async_copy(ybuf_ref.at[slot], ys_hbm.at[pl.ds(gran0 + k * blk_gran, blk_gran)],
                                     sem_out.at[slot])

    @pl.when(nb > 0)
    def _():
        in_copy(0, 0).start(priority=BLOCK_DMA_PRIORITY)

    w1b_ref[...] = w1_ref[0].astype(BF16)
    w2b_ref[...] = w2_ref[0].astype(BF16)

    def block(k, carry):
        slot = lax.rem(k, 2)

        @pl.when(k + 1 < nb)
        def _():
            in_copy(k + 1, 1 - slot).start(priority=BLOCK_DMA_PRIORITY)

        in_copy(k, slot).wait()

        @pl.when(k >= 2)
        def _():
            out_copy(k - 2, slot).wait()

        x = _unpack_slot_rows(_as_rows(xbuf_ref[slot]))
        h1 = jnp.dot(x, w1b_ref[...], preferred_element_type=F32) + b1_ref[0]
        gate = jnp.minimum(h1[:, :D_FF], SWIGLU_LIMIT)
        up = jnp.clip(h1[:, D_FF:], -SWIGLU_LIMIT, SWIGLU_LIMIT)
        act = gate * jax.nn.sigmoid(SWIGLU_ALPHA * gate) * (up + 1.0)
        y = jnp.dot(act.astype(BF16), w2b_ref[...], preferred_element_type=F32) + b2_ref[0]
        ybuf_ref[slot] = _as_granules(_pack_slot_rows(y))
        out_copy(k, slot).start(priority=BLOCK_DMA_PRIORITY)
        return carry

    lax.fori_loop(0, nb, block, 0)

    @pl.when(nb >= 2)
    def _():
        out_copy(nb - 2, lax.rem(nb, 2)).wait()

    @pl.when(nb >= 1)
    def _():
        out_copy(nb - 1, lax.rem(nb + 1, 2)).wait()


def _combine_kernel(n_prompt_tiles, cnt_ref, loff_ref, goff_ref, tot_ref,
                    routet_ref, x1_ref, gf_ref, ys_hbm, yp_ref, ysm_ref, stage_ref, sem):
    i = pl.program_id(0)
    slot = lax.rem(i, 2)
    T = TOK_TILE

    def start_tile(tile, tile_slot):
        def seg_copy(s, d, n):
            return pltpu.make_async_copy(ys_hbm.at[pl.ds(d, n)], stage_ref.at[tile_slot, pl.ds(s, n)],
                                         sem.at[tile_slot])

        def body(e, carry):
            t = tile * N_EXPERTS + e
            _segment_copies(cnt_ref[t], loff_ref[t], goff_ref[t], SEG_BITS, seg_copy, lambda cp: cp.start())
            return carry
        lax.fori_loop(0, N_EXPERTS, body, 0)

    @pl.when(i == 0)
    def _():
        stage_ref[...] = jnp.zeros(stage_ref.shape, U32)
        start_tile(i, slot)

    @pl.when(i + 1 < pl.num_programs(0))
    def _():
        start_tile(i + 1, 1 - slot)

    rt = routet_ref[...]
    l_iota = lax.broadcasted_iota(I32, (T, LOCAL_ROWS), 1)
    pw = jnp.zeros((T, LOCAL_ROWS), F32)
    for k in range(TOP_K):
        pos_k = rt[:, TOP_K + k:TOP_K + k + 1].astype(I32)
        pw = jnp.where(l_iota == pos_k, rt[:, 2 * TOP_K + k:2 * TOP_K + k + 1], pw)

    _wait_granules(tot_ref[i], TILE_BITS, lambda n: pltpu.make_async_copy(
        ys_hbm.at[pl.ds(0, n)], stage_ref.at[slot, pl.ds(0, n)], sem.at[slot]))
    moe = jnp.dot(pw.astype(BF16), _unpack_slot_rows(_as_rows(stage_ref[slot])), preferred_element_type=F32)
    y = _rms_norm(x1_ref[...] + moe, gf_ref[...])

    @pl.when(i < n_prompt_tiles)
    def _():
        yp_ref[...] = y

    @pl.when(i >= n_prompt_tiles)
    def _():
        ysm_ref[...] = y


def _t5_bucket(rel):
    half = NUM_BUCKETS // 2
    max_exact = half // 2
    ret = jnp.where(rel > 0, half, 0)
    n = jnp.abs(rel)
    nf = jnp.maximum(n, 1).astype(F32)
    large = max_exact + (jnp.log(nf / max_exact) / math.log(MAX_DISTANCE / max_exact)
                         * (half - max_exact)).astype(jnp.int32)
    large = jnp.minimum(large, half - 1)
    return ret + jnp.where(n < max_exact, n, large)


def _stacked_bias(table, sinks, q_pos, k_pos):
    nq, nk = q_pos.shape[0], k_pos.shape[0]
    bucket = _t5_bucket(k_pos[None, :] - q_pos[:, None])
    onehot = (bucket[:, :, None] == jnp.arange(NUM_BUCKETS)).astype(F32)
    bias = jnp.einsum('qkb,bh->hqk', onehot, table.astype(F32), precision=lax.Precision.HIGHEST)
    bias = bias.reshape(N_KV, REP * nq, nk)
    sink = jnp.repeat(sinks.astype(F32).reshape(N_KV, REP, 1), nq, axis=2).reshape(N_KV, REP * nq, 1)
    tail = jnp.full((N_KV, REP * nq, KEY_PAD - nk - 1), -jnp.inf, F32)
    return jnp.concatenate([bias, sink, tail], axis=-1)


def kernel(x_prompt, x_sample, cache_win_k, cache_win_v, norm1_g, w_in, attn_sinks, rel_bias_table, sgu_ln_g, sgu_ln_b, sgu_w, sgu_b, w_o_att, w_o_sgu, w_out, norm2_g, w_router, b_router, w_exp_in, b_exp_in, w_exp_out, b_exp_out, final_norm_g):
    batch, seq, _ = x_prompt.shape
    dec_batch, dec_seq, _ = x_sample.shape
    cache_rows = cache_win_k.shape[2]
    assert x_prompt.shape[2] == D_MODEL and w_in.shape == (1, D_MODEL, D_IN)
    assert seq % TOK_TILE == 0 and TOK_TILE % GMLP_CHUNK == 0 and TOK_TILE >= WINDOW
    assert TOK_TILE % dec_seq == 0 and (dec_batch * dec_seq) % TOK_TILE == 0
    assert dec_seq <= GMLP_CHUNK and GMLP_CHUNK % dec_seq == 0 and cache_rows == WINDOW
    T = TOK_TILE
    n_prompt = batch * seq
    n_sample = dec_batch * dec_seq
    n_tok = n_prompt + n_sample
    tiles_per_seq = seq // T
    n_ptiles = n_prompt // T
    n_stiles = n_sample // T
    n_tiles = n_ptiles + n_stiles
    seqs_per_tile = T // dec_seq

    w_in_b = w_in[0].astype(BF16)
    w_q_b = w_in_b[:, :D_ATT].reshape(D_MODEL, N_KV, REP, HEAD_DIM).transpose(0, 2, 1, 3).reshape(
        D_MODEL, D_ATT)
    woa_b = w_o_att[0].reshape(N_KV, REP, HEAD_DIM, D_MODEL).transpose(1, 0, 2, 3).reshape(
        D_ATT, D_MODEL).astype(BF16)
    wos_b = w_o_sgu[0].astype(BF16)
    wout_b = w_out[0].astype(BF16)
    wr_b = jnp.pad(w_router[0], ((0, 0), (0, LANES - N_EXPERTS))).astype(BF16)
    br_col = b_router[0].astype(F32).reshape(N_EXPERTS, 1)
    g1 = norm1_g[0].reshape(1, D_MODEL)
    g2 = norm2_g[0].reshape(1, D_MODEL)
    gf = final_norm_g.reshape(1, D_MODEL)
    lng = sgu_ln_g[0].reshape(1, D_GMLP)
    lnb = sgu_ln_b[0].reshape(1, D_GMLP)
    tril = jnp.tril(jnp.ones((GMLP_CHUNK, GMLP_CHUNK), dtype=bool))
    sguw_p = jnp.where(tril[None], sgu_w[0], 0).astype(BF16)
    sgub_p = jnp.broadcast_to(sgu_b[0][:, :, None], (N_GROUPS, GMLP_CHUNK, GROUP_W)).astype(F32)
    reps = GMLP_CHUNK // dec_seq
    corner = jnp.where(tril[None, :dec_seq, :dec_seq], sgu_w[0][:, :dec_seq, :dec_seq], 0)
    sguw_s = jnp.einsum('ab,gij->gaibj', jnp.eye(reps, dtype=F32), corner).reshape(
        N_GROUPS, GMLP_CHUNK, GMLP_CHUNK).astype(BF16)
    sgub_s = jnp.broadcast_to(jnp.tile(sgu_b[0][:, :dec_seq], (1, reps))[:, :, None],
                              (N_GROUPS, GMLP_CHUNK, GROUP_W)).astype(F32)
    bias_p = _stacked_bias(rel_bias_table, attn_sinks[0], jnp.arange(CHUNK) + WINDOW, jnp.arange(BAND))
    bias_s = _stacked_bias(rel_bias_table, attn_sinks[0], cache_rows + jnp.arange(dec_seq),
                           jnp.arange(cache_rows + dec_seq))
    tri = jnp.triu(jnp.ones((T, T), F32), k=1).astype(BF16)
    low = jnp.tril(jnp.ones((N_EXPERTS, N_EXPERTS), F32), k=-1).astype(BF16)

    layer_out_shapes = (
        jax.ShapeDtypeStruct((n_tok, D_MODEL), F32),
        jax.ShapeDtypeStruct((n_tok, D_MODEL), BF16),
        jax.ShapeDtypeStruct((n_tiles, ROUTE_ROWS, T), F32),
        jax.ShapeDtypeStruct((n_tok, LANES), F32),
        jax.ShapeDtypeStruct((n_tiles, N_EXPERTS, LANES), F32),
    )
    shared_consts = (w_q_b, w_in_b)
    tail_consts_p = (sguw_p, sgub_p, lng, lnb, woa_b, wos_b, wout_b, g2, wr_b, br_col, tri, low)
    tail_consts_s = (sguw_s, sgub_s, lng, lnb, woa_b, wos_b, wout_b, g2, wr_b, br_col, tri, low)

    def tile_specs(tile_of):
        return [
            pl.BlockSpec((T, D_MODEL), lambda *g: (tile_of(*g), 0)),
            pl.BlockSpec((T, D_MODEL), lambda *g: (tile_of(*g), 0)),
            pl.BlockSpec((1, ROUTE_ROWS, T), lambda *g: (tile_of(*g), 0, 0)),
            pl.BlockSpec((T, LANES), lambda *g: (tile_of(*g), 0)),
            pl.BlockSpec((1, N_EXPERTS, LANES), lambda *g: (tile_of(*g), 0, 0)),
        ]

    prompt_consts = (g1,) + shared_consts + (bias_p,) + tail_consts_p
    x1, h, route, routet, cnt, kvwin = pl.pallas_call(
        _prompt_layer_kernel,
        grid=(batch, tiles_per_seq),
        in_specs=[pl.BlockSpec((1, T, D_MODEL), lambda b, j: (b, j, 0))]
                 + [_const_spec(c.shape) for c in prompt_consts],
        out_specs=tile_specs(lambda b, j: b * tiles_per_seq + j)
                  + [pl.BlockSpec((1, WINDOW, 2 * D_KV), lambda b, j: (b, 0, 0))],
        out_shape=layer_out_shapes + (jax.ShapeDtypeStruct((batch, WINDOW, 2 * D_KV), F32),),
        scratch_shapes=[pltpu.VMEM((T + WINDOW, 2 * D_KV), BF16),
                        pltpu.VMEM((T, D_ATT), BF16),
                        pltpu.VMEM((T, D_GMLP), BF16)],
        compiler_params=pltpu.CompilerParams(dimension_semantics=("arbitrary", "arbitrary"),
                                             vmem_limit_bytes=VMEM_LIMIT),
        name="layer_prompt",
    )(x_prompt, *prompt_consts)

    xs_flat = x_sample.reshape(n_sample, D_MODEL)
    ck = cache_win_k[0].reshape(dec_batch, cache_rows, D_KV)
    cv = cache_win_v[0].reshape(dec_batch, cache_rows, D_KV)
    sample_consts = (g1,) + shared_consts + (bias_s,) + tail_consts_s
    any_spec = pl.BlockSpec(memory_space=pl.ANY)
    x1, h, route, routet, cnt, kvnew, vn_s = pl.pallas_call(
        _sample_layer_kernel,
        grid=(n_stiles,),
        in_specs=[any_spec] * 5
                 + [pl.BlockSpec((T, D_MODEL), lambda i: (i, 0)),
                    pl.BlockSpec((seqs_per_tile, cache_rows, D_KV), lambda i: (i, 0, 0)),
                    pl.BlockSpec((seqs_per_tile, cache_rows, D_KV), lambda i: (i, 0, 0))]
                 + [_const_spec(c.shape) for c in sample_consts],
        out_specs=tile_specs(lambda i: n_ptiles + i)
                  + [pl.BlockSpec((T, 2 * D_KV), lambda i: (i, 0)),
                     pl.BlockSpec((T, D_GMLP), lambda i: (i, 0))],
        out_shape=layer_out_shapes + (jax.ShapeDtypeStruct((n_sample, 2 * D_KV), F32),
                                      jax.ShapeDtypeStruct((n_sample, D_GMLP), F32)),
        scratch_shapes=[pltpu.VMEM((T, D_ATT), BF16), pltpu.VMEM((T, D_GMLP), BF16)],
        input_output_aliases={0: 0, 1: 1, 2: 2, 3: 3, 4: 4},
        compiler_params=pltpu.CompilerParams(dimension_semantics=("arbitrary",),
                                             vmem_limit_bytes=VMEM_LIMIT),
        name="layer_sample",
    )(x1, h, route, routet, cnt, xs_flat, ck, cv, *sample_consts)

    blk_gran = ROW_BLOCK // GRAN
    counts = cnt[:, :, 0].astype(I32)
    seg_gran = (counts + (GRAN - 1)) // GRAN
    local_off = jnp.cumsum(seg_gran, axis=1) - seg_gran
    tot_gran = jnp.sum(seg_gran, axis=0)
    ptot_gran = (tot_gran + (blk_gran - 1)) // blk_gran * blk_gran
    pend_gran = jnp.cumsum(ptot_gran)
    gstart = pend_gran - ptot_gran
    global_off = gstart[None, :] + jnp.cumsum(seg_gran, axis=0) - seg_gran
    pad_n = ptot_gran - tot_gran
    pad_off = gstart + tot_gran
    n_rows = -(-(TOP_K * n_tok + n_tiles * N_EXPERTS * (GRAN - 1) + N_EXPERTS * (ROW_BLOCK - GRAN))
               // ROW_BLOCK) * ROW_BLOCK
    blk0 = (gstart // blk_gran).astype(I32)
    nblk = (ptot_gran // blk_gran).astype(I32)
    tile_gran = jnp.sum(seg_gran, axis=1).astype(I32)
    seg_gran_f = seg_gran.reshape(-1).astype(I32)
    local_off_f = local_off.reshape(-1).astype(I32)
    global_off_f = global_off.reshape(-1).astype(I32)

    xs = pl.pallas_call(
        _dispatch_kernel,
        grid_spec=pltpu.PrefetchScalarGridSpec(
            num_scalar_prefetch=6,
            grid=(n_tiles,),
            in_specs=[pl.BlockSpec((1, ROUTE_ROWS, T), lambda i, *_: (i, 0, 0)),
                      pl.BlockSpec((T, D_MODEL), lambda i, *_: (i, 0))],
            out_specs=pl.BlockSpec(memory_space=pl.ANY),
            scratch_shapes=[pltpu.VMEM((2, LOCAL_ROWS // GRAN, GRAN, SLOT_WORDS), U32),
                            pltpu.VMEM((ROW_BLOCK // GRAN, GRAN, SLOT_WORDS), U32),
                            pltpu.SemaphoreType.DMA((3,))]),
        out_shape=jax.ShapeDtypeStruct((n_rows // GRAN, GRAN, SLOT_WORDS), U32),
        compiler_params=pltpu.CompilerParams(dimension_semantics=("arbitrary",),
                                             vmem_limit_bytes=VMEM_LIMIT),
        name="moe_dispatch",
    )(seg_gran_f, local_off_f, global_off_f, tile_gran, pad_n.astype(I32), pad_off.astype(I32), route, h)

    ys = pl.pallas_call(
        _expert_kernel,
        grid_spec=pltpu.PrefetchScalarGridSpec(
            num_scalar_prefetch=2,
            grid=(N_EXPERTS,),
            in_specs=[pl.BlockSpec(memory_space=pl.ANY),
                      pl.BlockSpec((1, D_MODEL, 2 * D_FF), lambda e, *_: (e, 0, 0)),
                      pl.BlockSpec((1, 1, 2 * D_FF), lambda e, *_: (e, 0, 0)),
                      pl.BlockSpec((1, D_FF, D_MODEL), lambda e, *_: (e, 0, 0)),
                      pl.BlockSpec((1, 1, D_MODEL), lambda e, *_: (e, 0, 0))],
            out_specs=pl.BlockSpec(memory_space=pl.ANY),
            scratch_shapes=[pltpu.VMEM((D_MODEL, 2 * D_FF), BF16), pltpu.VMEM((D_FF, D_MODEL), BF16),
                            pltpu.VMEM((2, ROW_BLOCK // GRAN, GRAN, SLOT_WORDS), U32),
                            pltpu.VMEM((2, ROW_BLOCK // GRAN, GRAN, SLOT_WORDS), U32),
                            pltpu.SemaphoreType.DMA((2,)), pltpu.SemaphoreType.DMA((2,))]),
        out_shape=jax.ShapeDtypeStruct((n_rows // GRAN, GRAN, SLOT_WORDS), U32),
        compiler_params=pltpu.CompilerParams(dimension_semantics=("arbitrary",),
                                             vmem_limit_bytes=VMEM_LIMIT),
        name="moe_experts",
    )(blk0, nblk, xs, w_exp_in[0], b_exp_in[0].reshape(N_EXPERTS, 1, 2 * D_FF),
      w_exp_out[0], b_exp_out[0].reshape(N_EXPERTS, 1, D_MODEL))

    y_p, y_s = pl.pallas_call(
        functools.partial(_combine_kernel, n_ptiles),
        grid_spec=pltpu.PrefetchScalarGridSpec(
            num_scalar_prefetch=4,
            grid=(n_tiles,),
            in_specs=[pl.BlockSpec((T, LANES), lambda i, *_: (i, 0)),
                      pl.BlockSpec((T, D_MODEL), lambda i, *_: (i, 0)),
                      pl.BlockSpec((1, D_MODEL), lambda i, *_: (0, 0)),
                      pl.BlockSpec(memory_space=pl.ANY)],
            out_specs=[pl.BlockSpec((T, D_MODEL), lambda i, *_: (jnp.minimum(i, n_ptiles - 1), 0)),
                       pl.BlockSpec((T, D_MODEL), lambda i, *_: (jnp.maximum(i - n_ptiles, 0), 0))],
            scratch_shapes=[pltpu.VMEM((2, LOCAL_ROWS // GRAN, GRAN, SLOT_WORDS), U32), pltpu.SemaphoreType.DMA((2,))]),
        out_shape=(jax.ShapeDtypeStruct((n_prompt, D_MODEL), F32),
                   jax.ShapeDtypeStruct((n_sample, D_MODEL), F32)),
        compiler_params=pltpu.CompilerParams(dimension_semantics=("arbitrary",),
                                             vmem_limit_bytes=VMEM_LIMIT),
        name="moe_combine",
    )(seg_gran_f, local_off_f, global_off_f, tile_gran, routet, x1, gf, ys)

    y_prompt = y_p.reshape(batch, seq, D_MODEL)
    y_sample = y_s.reshape(dec_batch, dec_seq, D_MODEL)
    new_win_k_prompt = kvwin[:, :, :D_KV].reshape(1, batch, WINDOW, N_KV, HEAD_DIM)
    new_win_v_prompt = kvwin[:, :, D_KV:].reshape(1, batch, WINDOW, N_KV, HEAD_DIM)
    new_win_k_sample = kvnew[:, :D_KV].reshape(1, dec_batch, dec_seq, N_KV, HEAD_DIM)
    new_win_v_sample = kvnew[:, D_KV:].reshape(1, dec_batch, dec_seq, N_KV, HEAD_DIM)
    new_sgu_v_sample = vn_s.reshape(1, dec_batch, dec_seq, D_GMLP)
    return (y_prompt, y_sample, new_win_k_prompt, new_win_v_prompt, new_win_k_sample,
            new_win_v_sample, new_sgu_v_sample)
```

```python
import functools
import math

import numpy as np
import jax
import jax.numpy as jnp
from jax import lax
from jax.experimental import pallas as pl
from jax.experimental.pallas import tpu as pltpu

F32 = jnp.float32
BF16 = jnp.bfloat16
I32 = jnp.int32
U32 = jnp.uint32

D_MODEL = 1024
HEAD_DIM = 64
N_HEADS = 16
N_KV = 2
REP = N_HEADS // N_KV
CHUNK = 64
WINDOW = 128
BAND = WINDOW + CHUNK
KEY_PAD = 256
D_ATT = N_HEADS * HEAD_DIM
D_KV = N_KV * HEAD_DIM
NUM_BUCKETS = 32
MAX_DISTANCE = 128
GMLP_CHUNK = 128
D_GMLP = 1024
N_GROUPS = 4
GROUP_W = D_GMLP // N_GROUPS
N_EXPERTS = 32
TOP_K = 4
D_FF = 1024
SWIGLU_LIMIT = 7.0
SWIGLU_ALPHA = 1.702
NORM_EPS = 1e-5
D_IN = D_ATT + 2 * D_KV + 2 * D_GMLP + 2 * D_MODEL
COL_KV = D_ATT
COL_U = COL_KV + 2 * D_KV
COL_VG = COL_U + D_GMLP
COL_GA = COL_VG + D_GMLP
COL_GB = COL_GA + D_MODEL
SQRT_HALF = float(np.sqrt(0.5))

LANES = 128
WORD_SUBLANES = 8
VMEM_LIMIT = 56 * 1024 * 1024

TOK_TILE = 256
GRAN = WORD_SUBLANES
SLOT_WORDS = D_MODEL // 2
SEG_BITS = (TOK_TILE // GRAN).bit_length()
LOCAL_ROWS = TOP_K * TOK_TILE + N_EXPERTS * GRAN
ROW_BLOCK = 512
SUB_BLOCK = 128
PAD_BITS = (SUB_BLOCK // GRAN - 1).bit_length()
TILE_BITS = (LOCAL_ROWS // GRAN).bit_length()
BLOCK_DMA_PRIORITY = 1
ROUTE_ROWS = 16


def _const_spec(shape):
    nd = len(shape)
    return pl.BlockSpec(shape, lambda *_: (0,) * nd, pipeline_mode=pl.Buffered(1))


def _rms_norm(x, g):
    ms = jnp.mean(x * x, axis=-1, keepdims=True)
    return x * lax.rsqrt(ms + NORM_EPS) * g


def _gelu(x):
    return 0.5 * x * (1.0 + lax.erf(x * SQRT_HALF))


def _attend(q, k, v, bias, valid):
    logits = lax.dot_general(q, k, (((1,), (1,)), ((), ())), preferred_element_type=F32) + bias
    if valid is not None:
        logits = jnp.where(valid, logits, -jnp.inf)
    m = jnp.max(logits, axis=-1, keepdims=True)
    p = jnp.exp(logits - m)
    den = jnp.sum(p, axis=-1, keepdims=True)
    w = (p * (1.0 / den)).astype(BF16)
    return jnp.dot(w, v, preferred_element_type=F32)


def _attention_rows(zq, row0, n_rows, k_ext, v_ext, bias_ref, valid, oatt_ref, heads_per_unit):
    lane = lax.broadcasted_iota(I32, (1, LANES), 1)
    for g in range(N_KV):
        in_group = (lane >= g * HEAD_DIM) & (lane < (g + 1) * HEAD_DIM)
        qmask = jnp.where(in_group, HEAD_DIM ** -0.5, 0.0).astype(BF16)
        for part in range(REP // heads_per_unit):
            tiles = range(part * heads_per_unit, (part + 1) * heads_per_unit)
            q = jnp.concatenate([zq[row0:row0 + n_rows, t * LANES:(t + 1) * LANES] * qmask for t in tiles],
                                axis=0)
            b0 = part * heads_per_unit * n_rows
            o = _attend(q, k_ext, v_ext, bias_ref[g, b0:b0 + heads_per_unit * n_rows, :], valid)
            for u, t in enumerate(tiles):
                c0 = t * LANES + g * HEAD_DIM
                oatt_ref[row0:row0 + n_rows, c0:c0 + HEAD_DIM] = (
                    o[u * n_rows:(u + 1) * n_rows, g * HEAD_DIM:(g + 1) * HEAD_DIM].astype(BF16))


def _layer_tail(x, xn_b, oatt_ref, osgu_ref, w_in_ref, sguw_ref, sgub_ref, lng_ref, lnb_ref,
                woa_ref, wos_ref, wout_ref, g2_ref, wr_ref, br_ref, tri_ref, low_ref,
                x1_ref, h_ref, route_ref, routet_ref, cnt_ref, vn_ref):
    T = x.shape[0]
    u = _gelu(jnp.dot(xn_b, w_in_ref[:, COL_U:COL_VG], preferred_element_type=F32))
    vg = _gelu(jnp.dot(xn_b, w_in_ref[:, COL_VG:COL_GA], preferred_element_type=F32))
    mu = jnp.mean(vg, axis=-1, keepdims=True)
    var = jnp.mean(jnp.square(vg - mu), axis=-1, keepdims=True)
    vn = (vg - mu) * lax.rsqrt(var + NORM_EPS) * lng_ref[...] + lnb_ref[...]
    if vn_ref is not None:
        vn_ref[...] = vn
    vn_b = vn.astype(BF16)
    for j in range(T // GMLP_CHUNK):
        rows = slice(j * GMLP_CHUNK, (j + 1) * GMLP_CHUNK)
        for g in range(N_GROUPS):
            cols = slice(g * GROUP_W, (g + 1) * GROUP_W)
            mixed = jnp.dot(sguw_ref[g], vn_b[rows, cols], preferred_element_type=F32) + sgub_ref[g]
            osgu_ref[rows, cols] = (u[rows, cols] * mixed).astype(BF16)

    ga = jax.nn.sigmoid(jnp.dot(xn_b, w_in_ref[:, COL_GA:COL_GB], preferred_element_type=F32))
    merged = ga * jnp.dot(oatt_ref[...], woa_ref[...], preferred_element_type=F32)
    gb = jax.nn.sigmoid(jnp.dot(xn_b, w_in_ref[:, COL_GB:D_IN], preferred_element_type=F32))
    merged = merged + gb * jnp.dot(osgu_ref[...], wos_ref[...], preferred_element_type=F32)
    x1 = x + jnp.dot(merged.astype(BF16), wout_ref[...], preferred_element_type=F32)
    x1_ref[...] = x1

    h_b = _rms_norm(x1, g2_ref[...]).astype(BF16)
    h_ref[...] = h_b
    logits = jnp.dot(h_b, wr_ref[...], preferred_element_type=F32)
    lt = jnp.transpose(logits)[:N_EXPERTS, :] + br_ref[...]
    e_iota = lax.broadcasted_iota(I32, (N_EXPERTS, T), 0).astype(F32)
    cur = lt
    vals, idxs = [], []
    for _ in range(TOP_K):
        m = jnp.max(cur, axis=0, keepdims=True)
        ik = jnp.min(jnp.where(cur == m, e_iota, float(N_EXPERTS)), axis=0, keepdims=True)
        vals.append(m)
        idxs.append(ik)
        cur = jnp.where(e_iota == ik, -jnp.inf, cur)
    exps = [jnp.exp(v - vals[0]) for v in vals]
    den = exps[0] + exps[1] + exps[2] + exps[3]
    gates = [e / den for e in exps]

    onehot = jnp.zeros((N_EXPERTS, T), F32)
    for ik in idxs:
        onehot = onehot + jnp.where(e_iota == ik, 1.0, 0.0)
    rank = jnp.dot(onehot.astype(BF16), tri_ref[...], preferred_element_type=F32)
    cnt = jnp.sum(onehot, axis=1, keepdims=True)
    cnt_b = jnp.broadcast_to(cnt, (N_EXPERTS, LANES))
    gran = jnp.floor((cnt_b + (GRAN - 1)) * (1.0 / GRAN))
    off = jnp.dot(low_ref[...], gran.astype(BF16), preferred_element_type=F32) * GRAN
    base = off[:, 0:1] + rank
    poss = [jnp.sum(jnp.where(e_iota == ik, base, 0.0), axis=0, keepdims=True) for ik in idxs]

    rec = jnp.concatenate(idxs + poss + gates
                          + [jnp.zeros((ROUTE_ROWS - 3 * TOP_K, T), F32)], axis=0)
    route_ref[0] = rec
    rec_pad = jnp.concatenate([rec, jnp.zeros((LANES - ROUTE_ROWS, T), F32)], axis=0)
    routet_ref[...] = jnp.transpose(rec_pad)
    cnt_ref[0] = cnt_b


def _prompt_layer_kernel(x_ref, g1_ref, wq_ref, w_in_ref, bias_ref, sguw_ref, sgub_ref, lng_ref,
                         lnb_ref, woa_ref, wos_ref, wout_ref, g2_ref, wr_ref, br_ref, tri_ref, low_ref,
                         x1_ref, h_ref, route_ref, routet_ref, cnt_ref, kvwin_ref,
                         kvx_ref, oatt_ref, osgu_ref):
    T = TOK_TILE
    j = pl.program_id(1)
    x = x_ref[0]
    xn_b = _rms_norm(x, g1_ref[...]).astype(BF16)
    zq = jnp.dot(xn_b, wq_ref[...], preferred_element_type=F32).astype(BF16)
    zkv = jnp.dot(xn_b, w_in_ref[:, COL_KV:COL_U], preferred_element_type=F32)

    @pl.when(j == pl.num_programs(1) - 1)
    def _():
        kvwin_ref[0] = zkv[T - WINDOW:, :]

    @pl.when(j == 0)
    def _():
        kvx_ref[0:WINDOW, :] = jnp.zeros((WINDOW, 2 * D_KV), BF16)

    @pl.when(j > 0)
    def _():
        kvx_ref[0:WINDOW, :] = kvx_ref[T:T + WINDOW, :]

    kvx_ref[WINDOW:, :] = zkv.astype(BF16)

    col = lax.broadcasted_iota(I32, (1, KEY_PAD), 1)
    key_pad = jnp.zeros((KEY_PAD - BAND, 2 * D_KV), BF16)
    for c in range(T // CHUNK):
        r0 = c * CHUNK
        kvb = jnp.concatenate([kvx_ref[r0:r0 + BAND, :], key_pad], axis=0)
        valid = ((col + (j * T + r0 - WINDOW)) >= 0) | (col >= BAND) if r0 < WINDOW else None
        _attention_rows(zq, r0, CHUNK, kvb[:, :D_KV], kvb[:, D_KV:], bias_ref, valid, oatt_ref,
                        heads_per_unit=4)

    _layer_tail(x, xn_b, oatt_ref, osgu_ref, w_in_ref, sguw_ref, sgub_ref, lng_ref, lnb_ref,
                woa_ref, wos_ref, wout_ref, g2_ref, wr_ref, br_ref, tri_ref, low_ref,
                x1_ref, h_ref, route_ref, routet_ref, cnt_ref, None)


def _sample_layer_kernel(x1_in, h_in, route_in, routet_in, cnt_in,
                         x_ref, ck_ref, cv_ref, g1_ref, wq_ref, w_in_ref, bias_ref, sguw_ref, sgub_ref,
                         lng_ref, lnb_ref, woa_ref, wos_ref, wout_ref, g2_ref, wr_ref, br_ref, tri_ref,
                         low_ref,
                         x1_ref, h_ref, route_ref, routet_ref, cnt_ref, kvnew_ref, vn_ref,
                         oatt_ref, osgu_ref):
    del x1_in, h_in, route_in, routet_in, cnt_in
    T = TOK_TILE
    nq = x_ref.shape[0] // ck_ref.shape[0]
    x = x_ref[...]
    xn_b = _rms_norm(x, g1_ref[...]).astype(BF16)
    zq = jnp.dot(xn_b, wq_ref[...], preferred_element_type=F32).astype(BF16)
    zkv = jnp.dot(xn_b, w_in_ref[:, COL_KV:COL_U], preferred_element_type=F32)
    kvnew_ref[...] = zkv
    zkv_b = zkv.astype(BF16)
    n_cache = ck_ref.shape[1]
    key_pad = jnp.zeros((KEY_PAD - n_cache - nq, D_KV), BF16)
    for b in range(T // nq):
        r0 = b * nq
        kk = jnp.concatenate([ck_ref[b].astype(BF16), zkv_b[r0:r0 + nq, :D_KV], key_pad], axis=0)
        vv = jnp.concatenate([cv_ref[b].astype(BF16), zkv_b[r0:r0 + nq, D_KV:], key_pad], axis=0)
        _attention_rows(zq, r0, nq, kk, vv, bias_ref, None, oatt_ref, heads_per_unit=REP)

    _layer_tail(x, xn_b, oatt_ref, osgu_ref, w_in_ref, sguw_ref, sgub_ref, lng_ref, lnb_ref,
                woa_ref, wos_ref, wout_ref, g2_ref, wr_ref, br_ref, tri_ref, low_ref,
                x1_ref, h_ref, route_ref, routet_ref, cnt_ref, vn_ref)


def _pack_slot_rows(x):
    lo = lax.bitcast_convert_type(x[:, :SLOT_WORDS].astype(BF16).astype(F32), U32)
    hi = lax.bitcast_convert_type(x[:, SLOT_WORDS:].astype(BF16).astype(F32), U32)
    return (lo >> 16) | hi


def _unpack_slot_rows(w):
    lo = lax.bitcast_convert_type(w << 16, F32)
    hi = lax.bitcast_convert_type(w & jnp.uint32(0xFFFF0000), F32)
    return jnp.concatenate([lo, hi], axis=1).astype(BF16)


def _as_granules(x):
    return x.reshape(x.shape[0] // GRAN, GRAN, SLOT_WORDS)


def _as_rows(x):
    return x.reshape(x.shape[0] * GRAN, SLOT_WORDS)


def _segment_copies(n_gran, src_gran, dst_gran, bits, make_copy, act):
    for b in range(min(bits, 2)):
        @pl.when(((n_gran >> b) & 1) == 1)
        def _(b=b):
            done = n_gran & ((1 << b) - 1)
            act(make_copy(src_gran + done, dst_gran + done, 1 << b))

    if bits > 2:
        def quad(q, carry):
            done = (n_gran & 3) + 4 * q
            act(make_copy(src_gran + done, dst_gran + done, 4))
            return carry
        lax.fori_loop(0, n_gran >> 2, quad, 0)


def _wait_granules(n_gran, bits, make_copy):
    for b in range(bits):
        @pl.when(((n_gran >> b) & 1) == 1)
        def _(b=b):
            make_copy(1 << b).wait()


def _dispatch_kernel(cnt_ref, loff_ref, goff_ref, tot_ref, padn_ref, padoff_ref,
                     route_ref, h_ref, xs_hbm, stage_ref, zero_ref, sem):
    i = pl.program_id(0)
    last = pl.num_programs(0) - 1
    slot = lax.rem(i, 2)
    T = TOK_TILE
    pos = route_ref[0, TOP_K:2 * TOP_K, :].astype(I32)
    r_iota = lax.broadcasted_iota(I32, (LOCAL_ROWS, T), 0)
    p = jnp.zeros((LOCAL_ROWS, T), F32)
    for k in range(TOP_K):
        p = jnp.where(r_iota == pos[k:k + 1, :], 1.0, p)
    stage_ref[slot] = _as_granules(
        _pack_slot_rows(jnp.dot(p.astype(BF16), h_ref[...], preferred_element_type=F32)))

    def seg_copy(s, d, n):
        return pltpu.make_async_copy(stage_ref.at[slot, pl.ds(s, n)], xs_hbm.at[pl.ds(d, n)], sem.at[slot])

    def start_segment(e, carry):
        t = i * N_EXPERTS + e
        _segment_copies(cnt_ref[t], loff_ref[t], goff_ref[t], SEG_BITS, seg_copy, lambda cp: cp.start())
        return carry

    lax.fori_loop(0, N_EXPERTS, start_segment, 0)

    def wait_tile(tile, tile_slot):
        _wait_granules(tot_ref[tile], TILE_BITS, lambda n: pltpu.make_async_copy(
            stage_ref.at[tile_slot, pl.ds(0, n)], xs_hbm.at[pl.ds(0, n)], sem.at[tile_slot]))

    @pl.when(i > 0)
    def _():
        wait_tile(i - 1, 1 - slot)

    @pl.when(i == last)
    def _():
        wait_tile(i, slot)
        zero_ref[...] = jnp.zeros(zero_ref.shape, U32)

        def pad_copy(s, d, n):
            return pltpu.make_async_copy(zero_ref.at[pl.ds(s, n)], xs_hbm.at[pl.ds(d, n)], sem.at[2])

        def for_pads(act):
            def body(e, carry):
                _segment_copies(padn_ref[e], 0, padoff_ref[e], PAD_BITS, pad_copy, act)
                return carry
            lax.fori_loop(0, N_EXPERTS, body, 0)

        for_pads(lambda cp: cp.start())
        for_pads(lambda cp: cp.wait())


def _expert_kernel(gran0_ref, nsub_ref, xs_hbm, w1_ref, b1_ref, w2_ref, b2_ref, ys_hbm,
                   w1b_ref, w2b_ref, xbuf_ref, ybuf_ref, sem_in, sem_out):
    e = pl.program_id(0)
    subs = ROW_BLOCK // SUB_BLOCK
    sub_gran = SUB_BLOCK // GRAN
    blk_gran = ROW_BLOCK // GRAN
    gran0 = gran0_ref[e]
    nsub = nsub_ref[e]
    nb_full = nsub // subs
    rem = nsub - nb_full * subs
    nb = nb_full + jnp.where(rem > 0, 1, 0)
    last_slot = lax.rem(nb + 1, 2)

    def in_copy(k, slot):
        return pltpu.make_async_copy(xs_hbm.at[pl.ds(gran0 + k * blk_gran, blk_gran)], xbuf_ref.at[slot],
                                     sem_in.at[slot])

    def out_copy(k, slot, m_sub):
        n = m_sub * sub_gran
        return pltpu.make_async_copy(ybuf_ref.at[slot, pl.ds(0, n)], ys_hbm.at[pl.ds(gran0 + k * blk_gran, n)],
                                     sem_out.at[slot])

    @pl.when(nb > 0)
    def _():
        in_copy(0, 0).start(priority=BLOCK_DMA_PRIORITY)

    w1b_ref[...] = w1_ref[0].astype(BF16)
    w2b_ref[...] = w2_ref[0].astype(BF16)

    def process(k, m_sub):
        n = m_sub * sub_gran
        slot = lax.rem(k, 2)

        @pl.when(k + 1 < nb)
        def _():
            in_copy(k + 1, 1 - slot).start(priority=BLOCK_DMA_PRIORITY)

        in_copy(k, slot).wait()

        @pl.when(k >= 2)
        def _():
            out_copy(k - 2, slot, subs).wait()

        x = _unpack_slot_rows(_as_rows(xbuf_ref[slot, 0:n]))
        h1 = jnp.dot(x, w1b_ref[...], preferred_element_type=F32) + b1_ref[0]
        gate = jnp.minimum(h1[:, :D_FF], SWIGLU_LIMIT)
        up = jnp.clip(h1[:, D_FF:], -SWIGLU_LIMIT, SWIGLU_LIMIT)
        act = gate * jax.nn.sigmoid(SWIGLU_ALPHA * gate) * (up + 1.0)
        y = jnp.dot(act.astype(BF16), w2b_ref[...], preferred_element_type=F32) + b2_ref[0]
        ybuf_ref[slot, 0:n] = _as_granules(_pack_slot_rows(y))
        out_copy(k, slot, m_sub).start(priority=BLOCK_DMA_PRIORITY)

    def full_block(k, carry):
        process(k, subs)
        return carry

    lax.fori_loop(0, nb_full, full_block, 0)

    for m_sub in range(1, subs):
        @pl.when(rem == m_sub)
        def _(m_sub=m_sub):
            process(nb_full, m_sub)

    @pl.when(nb >= 2)
    def _():
        out_copy(nb - 2, 1 - last_slot, subs).wait()

    @pl.when((nb >= 1) & (rem == 0))
    def _():
        out_copy(nb - 1, last_slot, subs).wait()

    for m_sub in range(1, subs):
        @pl.when(rem == m_sub)
        def _(m_sub=m_sub):
            out_copy(nb - 1, last_slot, m_sub).wait()


def _combine_kernel(n_prompt_tiles, cnt_ref, loff_ref, goff_ref, tot_ref,
                    routet_ref, x1_ref, gf_ref, ys_hbm, yp_ref, ysm_ref, stage_ref, sem):
    i = pl.program_id(0)
    slot = lax.rem(i, 2)
    T = TOK_TILE

    def start_tile(tile, tile_slot):
        def seg_copy(s, d, n):
            return pltpu.make_async_copy(ys_hbm.at[pl.ds(d, n)], stage_ref.at[tile_slot, pl.ds(s, n)],
                                         sem.at[tile_slot])

        def body(e, carry):
            t = tile * N_EXPERTS + e
            _segment_copies(cnt_ref[t], loff_ref[t], goff_ref[t], SEG_BITS, seg_copy, lambda cp: cp.start())
            return carry
        lax.fori_loop(0, N_EXPERTS, body, 0)

    @pl.when(i == 0)
    def _():
        stage_ref[...] = jnp.zeros(stage_ref.shape, U32)
        start_tile(i, slot)

    @pl.when(i + 1 < pl.num_programs(0))
    def _():
        start_tile(i + 1, 1 - slot)

    rt = routet_ref[...]
    l_iota = lax.broadcasted_iota(I32, (T, LOCAL_ROWS), 1)
    pw = jnp.zeros((T, LOCAL_ROWS), F32)
    for k in range(TOP_K):
        pos_k = rt[:, TOP_K + k:TOP_K + k + 1].astype(I32)
        pw = jnp.where(l_iota == pos_k, rt[:, 2 * TOP_K + k:2 * TOP_K + k + 1], pw)

    _wait_granules(tot_ref[i], TILE_BITS, lambda n: pltpu.make_async_copy(
        ys_hbm.at[pl.ds(0, n)], stage_ref.at[slot, pl.ds(0, n)], sem.at[slot]))
    moe = jnp.dot(pw.astype(BF16), _unpack_slot_rows(_as_rows(stage_ref[slot])), preferred_element_type=F32)
    y = _rms_norm(x1_ref[...] + moe, gf_ref[...])

    @pl.when(i < n_prompt_tiles)
    def _():
        yp_ref[...] = y

    @pl.when(i >= n_prompt_tiles)
    def _():
        ysm_ref[...] = y


def _t5_bucket(rel):
    half = NUM_BUCKETS // 2
    max_exact = half // 2
    ret = jnp.where(rel > 0, half, 0)
    n = jnp.abs(rel)
    nf = jnp.maximum(n, 1).astype(F32)
    large = max_exact + (jnp.log(nf / max_exact) / math.log(MAX_DISTANCE / max_exact)
                         * (half - max_exact)).astype(jnp.int32)
    large = jnp.minimum(large, half - 1)
    return ret + jnp.where(n < max_exact, n, large)


def _stacked_bias(table, sinks, q_pos, k_pos):
    nq, nk = q_pos.shape[0], k_pos.shape[0]
    bucket = _t5_bucket(k_pos[None, :] - q_pos[:, None])
    onehot = (bucket[:, :, None] == jnp.arange(NUM_BUCKETS)).astype(F32)
    bias = jnp.einsum('qkb,bh->hqk', onehot, table.astype(F32), precision=lax.Precision.HIGHEST)
    bias = bias.reshape(N_KV, REP * nq, nk)
    sink = jnp.repeat(sinks.astype(F32).reshape(N_KV, REP, 1), nq, axis=2).reshape(N_KV, REP * nq, 1)
    tail = jnp.full((N_KV, REP * nq, KEY_PAD - nk - 1), -jnp.inf, F32)
    return jnp.concatenate([bias, sink, tail], axis=-1)


def kernel(x_prompt, x_sample, cache_win_k, cache_win_v, norm1_g, w_in, attn_sinks, rel_bias_table, sgu_ln_g, sgu_ln_b, sgu_w, sgu_b, w_o_att, w_o_sgu, w_out, norm2_g, w_router, b_router, w_exp_in, b_exp_in, w_exp_out, b_exp_out, final_norm_g):
    batch, seq, _ = x_prompt.shape
    dec_batch, dec_seq, _ = x_sample.shape
    cache_rows = cache_win_k.shape[2]
    assert x_prompt.shape[2] == D_MODEL and w_in.shape == (1, D_MODEL, D_IN)
    assert seq % TOK_TILE == 0 and TOK_TILE % GMLP_CHUNK == 0 and TOK_TILE >= WINDOW
    assert TOK_TILE % dec_seq == 0 and (dec_batch * dec_seq) % TOK_TILE == 0
    assert dec_seq <= GMLP_CHUNK and GMLP_CHUNK % dec_seq == 0 and cache_rows == WINDOW
    T = TOK_TILE
    n_prompt = batch * seq
    n_sample = dec_batch * dec_seq
    n_tok = n_prompt + n_sample
    tiles_per_seq = seq // T
    n_ptiles = n_prompt // T
    n_stiles = n_sample // T
    n_tiles = n_ptiles + n_stiles
    seqs_per_tile = T // dec_seq

    w_in_b = w_in[0].astype(BF16)
    w_q_b = w_in_b[:, :D_ATT].reshape(D_MODEL, N_KV, REP, HEAD_DIM).transpose(0, 2, 1, 3).reshape(
        D_MODEL, D_ATT)
    woa_b = w_o_att[0].reshape(N_KV, REP, HEAD_DIM, D_MODEL).transpose(1, 0, 2, 3).reshape(
        D_ATT, D_MODEL).astype(BF16)
    wos_b = w_o_sgu[0].astype(BF16)
    wout_b = w_out[0].astype(BF16)
    wr_b = jnp.pad(w_router[0], ((0, 0), (0, LANES - N_EXPERTS))).astype(BF16)
    br_col = b_router[0].astype(F32).reshape(N_EXPERTS, 1)
    g1 = norm1_g[0].reshape(1, D_MODEL)
    g2 = norm2_g[0].reshape(1, D_MODEL)
    gf = final_norm_g.reshape(1, D_MODEL)
    lng = sgu_ln_g[0].reshape(1, D_GMLP)
    lnb = sgu_ln_b[0].reshape(1, D_GMLP)
    tril = jnp.tril(jnp.ones((GMLP_CHUNK, GMLP_CHUNK), dtype=bool))
    sguw_p = jnp.where(tril[None], sgu_w[0], 0).astype(BF16)
    sgub_p = jnp.broadcast_to(sgu_b[0][:, :, None], (N_GROUPS, GMLP_CHUNK, GROUP_W)).astype(F32)
    reps = GMLP_CHUNK // dec_seq
    corner = jnp.where(tril[None, :dec_seq, :dec_seq], sgu_w[0][:, :dec_seq, :dec_seq], 0)
    sguw_s = jnp.einsum('ab,gij->gaibj', jnp.eye(reps, dtype=F32), corner).reshape(
        N_GROUPS, GMLP_CHUNK, GMLP_CHUNK).astype(BF16)
    sgub_s = jnp.broadcast_to(jnp.tile(sgu_b[0][:, :dec_seq], (1, reps))[:, :, None],
                              (N_GROUPS, GMLP_CHUNK, GROUP_W)).astype(F32)
    bias_p = _stacked_bias(rel_bias_table, attn_sinks[0], jnp.arange(CHUNK) + WINDOW, jnp.arange(BAND))
    bias_s = _stacked_bias(rel_bias_table, attn_sinks[0], cache_rows + jnp.arange(dec_seq),
                           jnp.arange(cache_rows + dec_seq))
    tri = jnp.triu(jnp.ones((T, T), F32), k=1).astype(BF16)
    low = jnp.tril(jnp.ones((N_EXPERTS, N_EXPERTS), F32), k=-1).astype(BF16)

    layer_out_shapes = (
        jax.ShapeDtypeStruct((n_tok, D_MODEL), F32),
        jax.ShapeDtypeStruct((n_tok, D_MODEL), BF16),
        jax.ShapeDtypeStruct((n_tiles, ROUTE_ROWS, T), F32),
        jax.ShapeDtypeStruct((n_tok, LANES), F32),
        jax.ShapeDtypeStruct((n_tiles, N_EXPERTS, LANES), F32),
    )
    shared_consts = (w_q_b, w_in_b)
    tail_consts_p = (sguw_p, sgub_p, lng, lnb, woa_b, wos_b, wout_b, g2, wr_b, br_col, tri, low)
    tail_consts_s = (sguw_s, sgub_s, lng, lnb, woa_b, wos_b, wout_b, g2, wr_b, br_col, tri, low)

    def tile_specs(tile_of):
        return [
            pl.BlockSpec((T, D_MODEL), lambda *g: (tile_of(*g), 0)),
            pl.BlockSpec((T, D_MODEL), lambda *g: (tile_of(*g), 0)),
            pl.BlockSpec((1, ROUTE_ROWS, T), lambda *g: (tile_of(*g), 0, 0)),
            pl.BlockSpec((T, LANES), lambda *g: (tile_of(*g), 0)),
            pl.BlockSpec((1, N_EXPERTS, LANES), lambda *g: (tile_of(*g), 0, 0)),
        ]

    prompt_consts = (g1,) + shared_consts + (bias_p,) + tail_consts_p
    x1, h, route, routet, cnt, kvwin = pl.pallas_call(
        _prompt_layer_kernel,
        grid=(batch, tiles_per_seq),
        in_specs=[pl.BlockSpec((1, T, D_MODEL), lambda b, j: (b, j, 0))]
                 + [_const_spec(c.shape) for c in prompt_consts],
        out_specs=tile_specs(lambda b, j: b * tiles_per_seq + j)
                  + [pl.BlockSpec((1, WINDOW, 2 * D_KV), lambda b, j: (b, 0, 0))],
        out_shape=layer_out_shapes + (jax.ShapeDtypeStruct((batch, WINDOW, 2 * D_KV), F32),),
        scratch_shapes=[pltpu.VMEM((T + WINDOW, 2 * D_KV), BF16),
                        pltpu.VMEM((T, D_ATT), BF16),
                        pltpu.VMEM((T, D_GMLP), BF16)],
        compiler_params=pltpu.CompilerParams(dimension_semantics=("arbitrary", "arbitrary"),
                                             vmem_limit_bytes=VMEM_LIMIT),
        name="layer_prompt",
    )(x_prompt, *prompt_consts)

    xs_flat = x_sample.reshape(n_sample, D_MODEL)
    ck = cache_win_k[0].reshape(dec_batch, cache_rows, D_KV)
    cv = cache_win_v[0].reshape(dec_batch, cache_rows, D_KV)
    sample_consts = (g1,) + shared_consts + (bias_s,) + tail_consts_s
    any_spec = pl.BlockSpec(memory_space=pl.ANY)
    x1, h, route, routet, cnt, kvnew, vn_s = pl.pallas_call(
        _sample_layer_kernel,
        grid=(n_stiles,),
        in_specs=[any_spec] * 5
                 + [pl.BlockSpec((T, D_MODEL), lambda i: (i, 0)),
                    pl.BlockSpec((seqs_per_tile, cache_rows, D_KV), lambda i: (i, 0, 0)),
                    pl.BlockSpec((seqs_per_tile, cache_rows, D_KV), lambda i: (i, 0, 0))]
                 + [_const_spec(c.shape) for c in sample_consts],
        out_specs=tile_specs(lambda i: n_ptiles + i)
                  + [pl.BlockSpec((T, 2 * D_KV), lambda i: (i, 0)),
                     pl.BlockSpec((T, D_GMLP), lambda i: (i, 0))],
        out_shape=layer_out_shapes + (jax.ShapeDtypeStruct((n_sample, 2 * D_KV), F32),
                                      jax.ShapeDtypeStruct((n_sample, D_GMLP), F32)),
        scratch_shapes=[pltpu.VMEM((T, D_ATT), BF16), pltpu.VMEM((T, D_GMLP), BF16)],
        input_output_aliases={0: 0, 1: 1, 2: 2, 3: 3, 4: 4},
        compiler_params=pltpu.CompilerParams(dimension_semantics=("arbitrary",),
                                             vmem_limit_bytes=VMEM_LIMIT),
        name="layer_sample",
    )(x1, h, route, routet, cnt, xs_flat, ck, cv, *sample_consts)

    blk_gran = SUB_BLOCK // GRAN
    counts =cnt[:, :, 0].astype(I32)
    seg_gran = (counts + (GRAN - 1)) // GRAN
    local_off = jnp.cumsum(seg_gran, axis=1) - seg_gran
    tot_gran = jnp.sum(seg_gran, axis=0)
    ptot_gran = (tot_gran + (blk_gran - 1)) // blk_gran * blk_gran
    pend_gran = jnp.cumsum(ptot_gran)
    gstart = pend_gran - ptot_gran
    global_off = gstart[None, :] + jnp.cumsum(seg_gran, axis=0) - seg_gran
    pad_n = ptot_gran - tot_gran
    pad_off = gstart + tot_gran
    n_rows = -(-(TOP_K * n_tok + n_tiles * N_EXPERTS * (GRAN - 1) + N_EXPERTS * (SUB_BLOCK - GRAN))
               // SUB_BLOCK) * SUB_BLOCK + ROW_BLOCK
    region_gran0 = gstart.astype(I32)
    region_subs = (ptot_gran // blk_gran).astype(I32)
    tile_gran = jnp.sum(seg_gran, axis=1).astype(I32)
    seg_gran_f = seg_gran.reshape(-1).astype(I32)
    local_off_f = local_off.reshape(-1).astype(I32)
    global_off_f = global_off.reshape(-1).astype(I32)

    xs = pl.pallas_call(
        _dispatch_kernel,
        grid_spec=pltpu.PrefetchScalarGridSpec(
            num_scalar_prefetch=6,
            grid=(n_tiles,),
            in_specs=[pl.BlockSpec((1, ROUTE_ROWS, T), lambda i, *_: (i, 0, 0)),
                      pl.BlockSpec((T, D_MODEL), lambda i, *_: (i, 0))],
            out_specs=pl.BlockSpec(memory_space=pl.ANY),
            scratch_shapes=[pltpu.VMEM((2, LOCAL_ROWS // GRAN, GRAN, SLOT_WORDS), U32),
                            pltpu.VMEM((SUB_BLOCK // GRAN, GRAN, SLOT_WORDS), U32),
                            pltpu.SemaphoreType.DMA((3,))]),
        out_shape=jax.ShapeDtypeStruct((n_rows // GRAN, GRAN, SLOT_WORDS), U32),
        compiler_params=pltpu.CompilerParams(dimension_semantics=("arbitrary",),
                                             vmem_limit_bytes=VMEM_LIMIT),
        name="moe_dispatch",
    )(seg_gran_f, local_off_f, global_off_f, tile_gran, pad_n.astype(I32), pad_off.astype(I32), route, h)

    ys = pl.pallas_call(
        _expert_kernel,
        grid_spec=pltpu.PrefetchScalarGridSpec(
            num_scalar_prefetch=2,
            grid=(N_EXPERTS,),
            in_specs=[pl.BlockSpec(memory_space=pl.ANY),
                      pl.BlockSpec((1, D_MODEL, 2 * D_FF), lambda e, *_: (e, 0, 0)),
                      pl.BlockSpec((1, 1, 2 * D_FF), lambda e, *_: (e, 0, 0)),
                      pl.BlockSpec((1, D_FF, D_MODEL), lambda e, *_: (e, 0, 0)),
                      pl.BlockSpec((1, 1, D_MODEL), lambda e, *_: (e, 0, 0))],
            out_specs=pl.BlockSpec(memory_space=pl.ANY),
            scratch_shapes=[pltpu.VMEM((D_MODEL, 2 * D_FF), BF16), pltpu.VMEM((D_FF, D_MODEL), BF16),
                            pltpu.VMEM((2, ROW_BLOCK // GRAN, GRAN, SLOT_WORDS), U32),
                            pltpu.VMEM((2, ROW_BLOCK // GRAN, GRAN, SLOT_WORDS), U32),
                            pltpu.SemaphoreType.DMA((2,)), pltpu.SemaphoreType.DMA((2,))]),
        out_shape=jax.ShapeDtypeStruct((n_rows // GRAN, GRAN, SLOT_WORDS), U32),
        compiler_params=pltpu.CompilerParams(dimension_semantics=("arbitrary",),
                                             vmem_limit_bytes=VMEM_LIMIT),
        name="moe_experts",
    )(region_gran0, region_subs, xs, w_exp_in[0], b_exp_in[0].reshape(N_EXPERTS, 1, 2 * D_FF),
      w_exp_out[0], b_exp_out[0].reshape(N_EXPERTS, 1, D_MODEL))

    y_p, y_s = pl.pallas_call(
        functools.partial(_combine_kernel, n_ptiles),
        grid_spec=pltpu.PrefetchScalarGridSpec(
            num_scalar_prefetch=4,
            grid=(n_tiles,),
            in_specs=[pl.BlockSpec((T, LANES), lambda i, *_: (i, 0)),
                      pl.BlockSpec((T, D_MODEL), lambda i, *_: (i, 0)),
                      pl.BlockSpec((1, D_MODEL), lambda i, *_: (0, 0)),
                      pl.BlockSpec(memory_space=pl.ANY)],
            out_specs=[pl.BlockSpec((T, D_MODEL), lambda i, *_: (jnp.minimum(i, n_ptiles - 1), 0)),
                       pl.BlockSpec((T, D_MODEL), lambda i, *_: (jnp.maximum(i - n_ptiles, 0), 0))],
            scratch_shapes=[pltpu.VMEM((2, LOCAL_ROWS // GRAN, GRAN, SLOT_WORDS), U32), pltpu.SemaphoreType.DMA((2,))]),
        out_shape=(jax.ShapeDtypeStruct((n_prompt, D_MODEL), F32),
                   jax.ShapeDtypeStruct((n_sample, D_MODEL), F32)),
        compiler_params=pltpu.CompilerParams(dimension_semantics=("arbitrary",),
                                             vmem_limit_bytes=VMEM_LIMIT),
        name="moe_combine",
    )(seg_gran_f, local_off_f, global_off_f, tile_gran, routet, x1, gf, ys)

    y_prompt = y_p.reshape(batch, seq, D_MODEL)
    y_sample = y_s.reshape(dec_batch, dec_seq, D_MODEL)
    new_win_k_prompt = kvwin[:, :, :D_KV].reshape(1, batch, WINDOW, N_KV, HEAD_DIM)
    new_win_v_prompt = kvwin[:, :, D_KV:].reshape(1, batch, WINDOW, N_KV, HEAD_DIM)
    new_win_k_sample = kvnew[:, :D_KV].reshape(1, dec_batch, dec_seq, N_KV, HEAD_DIM)
    new_win_v_sample = kvnew[:, D_KV:].reshape(1, dec_batch, dec_seq, N_KV, HEAD_DIM)
    new_sgu_v_sample = vn_s.reshape(1, dec_batch, dec_seq, D_GMLP)
    return (y_prompt, y_sample, new_win_k_prompt, new_win_v_prompt, new_win_k_sample,
            new_win_v_sample, new_sgu_v_sample)
```

```python
import functools
import math

import numpy as np
import jax
import jax.numpy as jnp
from jax import lax
from jax.experimental import pallas as pl
from jax.experimental.pallas import tpu as pltpu

F32 = jnp.float32
BF16 = jnp.bfloat16
I32 = jnp.int32
U32 = jnp.uint32

D_MODEL = 1024
HEAD_DIM = 64
N_HEADS = 16
N_KV = 2
REP = N_HEADS // N_KV
CHUNK = 64
WINDOW = 128
BAND = WINDOW + CHUNK
KEY_PAD = 256
D_ATT = N_HEADS * HEAD_DIM
D_KV = N_KV * HEAD_DIM
NUM_BUCKETS = 32
MAX_DISTANCE = 128
GMLP_CHUNK = 128
D_GMLP = 1024
N_GROUPS = 4
GROUP_W = D_GMLP // N_GROUPS
N_EXPERTS = 32
TOP_K = 4
D_FF = 1024
SWIGLU_LIMIT = 7.0
SWIGLU_ALPHA = 1.702
NORM_EPS = 1e-5
D_IN = D_ATT + 2 * D_KV + 2 * D_GMLP + 2 * D_MODEL
COL_KV = D_ATT
COL_U = COL_KV + 2 * D_KV
COL_VG = COL_U + D_GMLP
COL_GA = COL_VG + D_GMLP
COL_GB = COL_GA + D_MODEL
SQRT_HALF = float(np.sqrt(0.5))

LANES = 128
WORD_SUBLANES = 8
VMEM_LIMIT = 56 * 1024 * 1024

TOK_TILE = 256
GRAN = WORD_SUBLANES
SLOT_WORDS = D_MODEL // 2
FLAT_QUADS = 3
LOCAL_ROWS = TOP_K * TOK_TILE + N_EXPERTS * GRAN
ROW_BLOCK = 512
SUB_BLOCK = 128
PAD_BITS = (SUB_BLOCK // GRAN - 1).bit_length()
TILE_BITS = (LOCAL_ROWS // GRAN).bit_length()
BLOCK_DMA_PRIORITY = 1
ROUTE_ROWS = 16


def _const_spec(shape):
    nd = len(shape)
    return pl.BlockSpec(shape, lambda *_: (0,) * nd, pipeline_mode=pl.Buffered(1))


def _rms_norm(x, g):
    ms = jnp.mean(x * x, axis=-1, keepdims=True)
    return x * lax.rsqrt(ms + NORM_EPS) * g


def _gelu(x):
    return 0.5 * x * (1.0 + lax.erf(x * SQRT_HALF))


def _attend(q, k, v, bias, valid):
    logits = lax.dot_general(q, k, (((1,), (1,)), ((), ())), preferred_element_type=F32) + bias
    if valid is not None:
        logits = jnp.where(valid, logits, -jnp.inf)
    m = jnp.max(logits, axis=-1, keepdims=True)
    p = jnp.exp(logits - m)
    den = jnp.sum(p, axis=-1, keepdims=True)
    w = (p * (1.0 / den)).astype(BF16)
    return jnp.dot(w, v, preferred_element_type=F32)


def _attention_rows(zq, row0, n_rows, k_ext, v_ext, bias_ref, valid, oatt_ref, heads_per_unit):
    lane = lax.broadcasted_iota(I32, (1, LANES), 1)
    for g in range(N_KV):
        in_group = (lane >= g * HEAD_DIM) & (lane < (g + 1) * HEAD_DIM)
        qmask = jnp.where(in_group, HEAD_DIM ** -0.5, 0.0).astype(BF16)
        for part in range(REP // heads_per_unit):
            tiles = range(part * heads_per_unit, (part + 1) * heads_per_unit)
            q = jnp.concatenate([zq[row0:row0 + n_rows, t * LANES:(t + 1) * LANES] * qmask for t in tiles],
                                axis=0)
            b0 = part * heads_per_unit * n_rows
            o = _attend(q, k_ext, v_ext, bias_ref[g, b0:b0 + heads_per_unit * n_rows, :], valid)
            for u, t in enumerate(tiles):
                c0 = t * LANES + g * HEAD_DIM
                oatt_ref[row0:row0 + n_rows, c0:c0 + HEAD_DIM] = (
                    o[u * n_rows:(u + 1) * n_rows, g * HEAD_DIM:(g + 1) * HEAD_DIM].astype(BF16))


def _layer_tail(x, xn_b, oatt_ref, osgu_ref, w_in_ref, sguw_ref, sgub_ref, lng_ref, lnb_ref,
                woa_ref, wos_ref, wout_ref, g2_ref, wr_ref, br_ref, tri_ref, low_ref,
                x1_ref, h_ref, route_ref, routet_ref, cnt_ref, vn_ref):
    T = x.shape[0]
    u = _gelu(jnp.dot(xn_b, w_in_ref[:, COL_U:COL_VG], preferred_element_type=F32))
    vg = _gelu(jnp.dot(xn_b, w_in_ref[:, COL_VG:COL_GA], preferred_element_type=F32))
    mu = jnp.mean(vg, axis=-1, keepdims=True)
    var = jnp.mean(jnp.square(vg - mu), axis=-1, keepdims=True)
    vn = (vg - mu) * lax.rsqrt(var + NORM_EPS) * lng_ref[...] + lnb_ref[...]
    if vn_ref is not None:
        vn_ref[...] = vn
    vn_b = vn.astype(BF16)
    for j in range(T // GMLP_CHUNK):
        rows = slice(j * GMLP_CHUNK, (j + 1) * GMLP_CHUNK)
        for g in range(N_GROUPS):
            cols = slice(g * GROUP_W, (g + 1) * GROUP_W)
            mixed = jnp.dot(sguw_ref[g], vn_b[rows, cols], preferred_element_type=F32) + sgub_ref[g]
            osgu_ref[rows, cols] = (u[rows, cols] * mixed).astype(BF16)

    ga = jax.nn.sigmoid(jnp.dot(xn_b, w_in_ref[:, COL_GA:COL_GB], preferred_element_type=F32))
    merged = ga * jnp.dot(oatt_ref[...], woa_ref[...], preferred_element_type=F32)
    gb = jax.nn.sigmoid(jnp.dot(xn_b, w_in_ref[:, COL_GB:D_IN], preferred_element_type=F32))
    merged = merged + gb * jnp.dot(osgu_ref[...], wos_ref[...], preferred_element_type=F32)
    x1 = x + jnp.dot(merged.astype(BF16), wout_ref[...], preferred_element_type=F32)
    x1_ref[...] = x1

    h_b = _rms_norm(x1, g2_ref[...]).astype(BF16)
    h_ref[...] = h_b
    logits = jnp.dot(h_b, wr_ref[...], preferred_element_type=F32)
    lt = jnp.transpose(logits)[:N_EXPERTS, :] + br_ref[...]
    e_iota = lax.broadcasted_iota(I32, (N_EXPERTS, T), 0).astype(F32)
    cur = lt
    vals, idxs = [], []
    for _ in range(TOP_K):
        m = jnp.max(cur, axis=0, keepdims=True)
        ik = jnp.min(jnp.where(cur == m, e_iota, float(N_EXPERTS)), axis=0, keepdims=True)
        vals.append(m)
        idxs.append(ik)
        cur = jnp.where(e_iota == ik, -jnp.inf, cur)
    exps = [jnp.exp(v - vals[0]) for v in vals]
    den = exps[0] + exps[1] + exps[2] + exps[3]
    gates = [e / den for e in exps]

    onehot = jnp.zeros((N_EXPERTS, T), F32)
    for ik in idxs:
        onehot = onehot + jnp.where(e_iota == ik, 1.0, 0.0)
    rank = jnp.dot(onehot.astype(BF16), tri_ref[...], preferred_element_type=F32)
    cnt = jnp.sum(onehot, axis=1, keepdims=True)
    cnt_b = jnp.broadcast_to(cnt, (N_EXPERTS, LANES))
    gran = jnp.floor((cnt_b + (GRAN - 1)) * (1.0 / GRAN))
    off = jnp.dot(low_ref[...], gran.astype(BF16), preferred_element_type=F32) * GRAN
    base = off[:, 0:1] + rank
    poss = [jnp.sum(jnp.where(e_iota == ik, base, 0.0), axis=0, keepdims=True) for ik in idxs]

    rec = jnp.concatenate(idxs + poss + gates
                          + [jnp.zeros((ROUTE_ROWS - 3 * TOP_K, T), F32)], axis=0)
    route_ref[0] = rec
    rec_pad = jnp.concatenate([rec, jnp.zeros((LANES - ROUTE_ROWS, T), F32)], axis=0)
    routet_ref[...] = jnp.transpose(rec_pad)
    cnt_ref[0] = cnt_b


def _prompt_layer_kernel(x_ref, g1_ref, wq_ref, w_in_ref, bias_ref, sguw_ref, sgub_ref, lng_ref,
                         lnb_ref, woa_ref, wos_ref, wout_ref, g2_ref, wr_ref, br_ref, tri_ref, low_ref,
                         x1_ref, h_ref, route_ref, routet_ref, cnt_ref, kvwin_ref,
                         kvx_ref, oatt_ref, osgu_ref):
    T = TOK_TILE
    j = pl.program_id(1)
    x = x_ref[0]
    xn_b = _rms_norm(x, g1_ref[...]).astype(BF16)
    zq = jnp.dot(xn_b, wq_ref[...], preferred_element_type=F32).astype(BF16)
    zkv = jnp.dot(xn_b, w_in_ref[:, COL_KV:COL_U], preferred_element_type=F32)

    @pl.when(j == pl.num_programs(1) - 1)
    def _():
        kvwin_ref[0] = zkv[T - WINDOW:, :]

    @pl.when(j == 0)
    def _():
        kvx_ref[0:WINDOW, :] = jnp.zeros((WINDOW, 2 * D_KV), BF16)

    @pl.when(j > 0)
    def _():
        kvx_ref[0:WINDOW, :] = kvx_ref[T:T + WINDOW, :]

    kvx_ref[WINDOW:, :] = zkv.astype(BF16)

    col = lax.broadcasted_iota(I32, (1, KEY_PAD), 1)
    key_pad = jnp.zeros((KEY_PAD - BAND, 2 * D_KV), BF16)
    for c in range(T // CHUNK):
        r0 = c * CHUNK
        kvb = jnp.concatenate([kvx_ref[r0:r0 + BAND, :], key_pad], axis=0)
        valid = ((col + (j * T + r0 - WINDOW)) >= 0) | (col >= BAND) if r0 < WINDOW else None
        _attention_rows(zq, r0, CHUNK, kvb[:, :D_KV], kvb[:, D_KV:], bias_ref, valid, oatt_ref,
                        heads_per_unit=4)

    _layer_tail(x, xn_b, oatt_ref, osgu_ref, w_in_ref, sguw_ref, sgub_ref, lng_ref, lnb_ref,
                woa_ref, wos_ref, wout_ref, g2_ref, wr_ref, br_ref, tri_ref, low_ref,
                x1_ref, h_ref, route_ref, routet_ref, cnt_ref, None)


def _sample_layer_kernel(x1_in, h_in, route_in, routet_in, cnt_in,
                         x_ref, ck_ref, cv_ref, g1_ref, wq_ref, w_in_ref, bias_ref, sguw_ref, sgub_ref,
                         lng_ref, lnb_ref, woa_ref, wos_ref, wout_ref, g2_ref, wr_ref, br_ref, tri_ref,
                         low_ref,
                         x1_ref, h_ref, route_ref, routet_ref, cnt_ref, kvnew_ref, vn_ref,
                         oatt_ref, osgu_ref):
    del x1_in, h_in, route_in, routet_in, cnt_in
    T = TOK_TILE
    nq = x_ref.shape[0] // ck_ref.shape[0]
    x = x_ref[...]
    xn_b = _rms_norm(x, g1_ref[...]).astype(BF16)
    zq = jnp.dot(xn_b, wq_ref[...], preferred_element_type=F32).astype(BF16)
    zkv = jnp.dot(xn_b, w_in_ref[:, COL_KV:COL_U], preferred_element_type=F32)
    kvnew_ref[...] = zkv
    zkv_b = zkv.astype(BF16)
    n_cache = ck_ref.shape[1]
    key_pad = jnp.zeros((KEY_PAD - n_cache - nq, D_KV), BF16)
    for b in range(T // nq):
        r0 = b * nq
        kk = jnp.concatenate([ck_ref[b].astype(BF16), zkv_b[r0:r0 + nq, :D_KV], key_pad], axis=0)
        vv = jnp.concatenate([cv_ref[b].astype(BF16), zkv_b[r0:r0 + nq, D_KV:], key_pad], axis=0)
        _attention_rows(zq, r0, nq, kk, vv, bias_ref, None, oatt_ref, heads_per_unit=REP)

    _layer_tail(x, xn_b, oatt_ref, osgu_ref, w_in_ref, sguw_ref, sgub_ref, lng_ref, lnb_ref,
                woa_ref, wos_ref, wout_ref, g2_ref, wr_ref, br_ref, tri_ref, low_ref,
                x1_ref, h_ref, route_ref, routet_ref, cnt_ref, vn_ref)


def _pack_slot_rows(x):
    lo = lax.bitcast_convert_type(x[:, :SLOT_WORDS].astype(BF16).astype(F32), U32)
    hi = lax.bitcast_convert_type(x[:, SLOT_WORDS:].astype(BF16).astype(F32), U32)
    return (lo >> 16) | hi


def _unpack_slot_rows(w):
    lo = lax.bitcast_convert_type(w << 16, F32)
    hi = lax.bitcast_convert_type(w & jnp.uint32(0xFFFF0000), F32)
    return jnp.concatenate([lo, hi], axis=1).astype(BF16)


def _as_granules(x):
    return x.reshape(x.shape[0] // GRAN, GRAN, SLOT_WORDS)


def _as_rows(x):
    return x.reshape(x.shape[0] * GRAN, SLOT_WORDS)


def _segment_copies(n_gran, src_gran, dst_gran, bits, make_copy, act):
    for b in range(min(bits, 2)):
        @pl.when(((n_gran >> b) & 1) == 1)
        def _(b=b):
            done = n_gran & ((1 << b) - 1)
            act(make_copy(src_gran + done, dst_gran + done, 1 << b))

    if bits > 2:
        def quad(q, carry):
            done = (n_gran & 3) + 4 * q
            act(make_copy(src_gran + done, dst_gran + done, 4))
            return carry
        lax.fori_loop(0, n_gran >> 2, quad, 0)


def _start_tile_copies(tile, live, cnt_ref, loff_ref, goff_ref, big_ref, make_copy):
    for e in range(N_EXPERTS):
        t = tile * N_EXPERTS + e
        n = jnp.where(live, cnt_ref[t], 0)
        src, dst = loff_ref[t], goff_ref[t]
        for b in range(2):
            @pl.when(((n >> b) & 1) == 1)
            def _(b=b, n=n, src=src, dst=dst):
                done = n & ((1 << b) - 1)
                make_copy(src + done, dst + done, 1 << b).start()
        for q in range(FLAT_QUADS):
            @pl.when((n >> 2) > q)
            def _(q=q, n=n, src=src, dst=dst):
                done = (n & 3) + 4 * q
                make_copy(src + done, dst + done, 4).start()

    @pl.when(live & (big_ref[tile] > 0))
    def _():
        def rest(e, carry):
            t = tile * N_EXPERTS + e
            n, src, dst = cnt_ref[t], loff_ref[t], goff_ref[t]

            def quad(q, c):
                done = (n & 3) + 4 * q
                make_copy(src + done, dst + done, 4).start()
                return c
            lax.fori_loop(FLAT_QUADS, n >> 2, quad, 0)
            return carry
        lax.fori_loop(0, N_EXPERTS, rest, 0)


def _wait_granules(n_gran, bits, make_copy):
    for b in range(bits):
        @pl.when(((n_gran >> b) & 1) == 1)
        def _(b=b):
            make_copy(1 << b).wait()


def _dispatch_kernel(cnt_ref, loff_ref, goff_ref, tot_ref, big_ref, padn_ref, padoff_ref,
                     route_ref, h_ref, xs_hbm, stage_ref, zero_ref, sem):
    i = pl.program_id(0)
    last = pl.num_programs(0) - 1
    slot = lax.rem(i, 2)
    T = TOK_TILE

    def seg_copy(tile_slot):
        return lambda s, d, n: pltpu.make_async_copy(
            stage_ref.at[tile_slot, pl.ds(s, n)], xs_hbm.at[pl.ds(d, n)], sem.at[tile_slot])

    def wait_tile(tile, tile_slot):
        _wait_granules(tot_ref[tile], TILE_BITS, lambda n: pltpu.make_async_copy(
            stage_ref.at[tile_slot, pl.ds(0, n)], xs_hbm.at[pl.ds(0, n)], sem.at[tile_slot]))

    @pl.when(i > 1)
    def _():
        wait_tile(i - 2, slot)

    _start_tile_copies(jnp.maximum(i - 1, 0), i > 0, cnt_ref, loff_ref, goff_ref, big_ref, seg_copy(1 - slot))

    pos = route_ref[0, TOP_K:2 * TOP_K, :].astype(I32)
    r_iota = lax.broadcasted_iota(I32, (LOCAL_ROWS, T), 0)
    p = jnp.zeros((LOCAL_ROWS, T), F32)
    for k in range(TOP_K):
        p = jnp.where(r_iota == pos[k:k + 1, :], 1.0, p)
    stage_ref[slot] = _as_granules(
        _pack_slot_rows(jnp.dot(p.astype(BF16), h_ref[...], preferred_element_type=F32)))

    @pl.when(i == last)
    def _():
        _start_tile_copies(i, i >= 0, cnt_ref, loff_ref, goff_ref, big_ref, seg_copy(slot))

        @pl.when(i > 0)
        def _():
            wait_tile(i - 1, 1 - slot)

        wait_tile(i, slot)
        zero_ref[...] = jnp.zeros(zero_ref.shape, U32)

        def pad_copy(s, d, n):
            return pltpu.make_async_copy(zero_ref.at[pl.ds(s, n)], xs_hbm.at[pl.ds(d, n)], sem.at[2])

        def for_pads(act):
            def body(e, carry):
                _segment_copies(padn_ref[e], 0, padoff_ref[e], PAD_BITS, pad_copy, act)
                return carry
            lax.fori_loop(0, N_EXPERTS, body, 0)

        for_pads(lambda cp: cp.start())
        for_pads(lambda cp: cp.wait())


def _expert_kernel(gran0_ref, nsub_ref, xs_hbm, w1_ref, b1_ref, w2_ref, b2_ref, ys_hbm,
                   w1b_ref, w2b_ref, xbuf_ref, ybuf_ref, sem_in, sem_out):
    e = pl.program_id(0)
    subs = ROW_BLOCK // SUB_BLOCK
    sub_gran = SUB_BLOCK // GRAN
    blk_gran = ROW_BLOCK // GRAN
    gran0 = gran0_ref[e]
    nsub = nsub_ref[e]
    nb_full = nsub // subs
    rem = nsub - nb_full * subs
    nb = nb_full + jnp.where(rem > 0, 1, 0)
    last_slot = lax.rem(nb + 1, 2)

    def in_copy(k, slot):
        return pltpu.make_async_copy(xs_hbm.at[pl.ds(gran0 + k * blk_gran, blk_gran)], xbuf_ref.at[slot],
                                     sem_in.at[slot])

    def out_copy(k, slot, m_sub):
        n = m_sub * sub_gran
        return pltpu.make_async_copy(ybuf_ref.at[slot, pl.ds(0, n)], ys_hbm.at[pl.ds(gran0 + k * blk_gran, n)],
                                     sem_out.at[slot])

    @pl.when(nb > 0)
    def _():
        in_copy(0, 0).start(priority=BLOCK_DMA_PRIORITY)

    w1b_ref[...] = w1_ref[0].astype(BF16)
    w2b_ref[...] = w2_ref[0].astype(BF16)

    def process(k, m_sub):
        n = m_sub * sub_gran
        slot = lax.rem(k, 2)

        @pl.when(k + 1 < nb)
        def _():
            in_copy(k + 1, 1 - slot).start(priority=BLOCK_DMA_PRIORITY)

        in_copy(k, slot).wait()

        @pl.when(k >= 2)
        def _():
            out_copy(k - 2, slot, subs).wait()

        x = _unpack_slot_rows(_as_rows(xbuf_ref[slot, 0:n]))
        h1 = jnp.dot(x, w1b_ref[...], preferred_element_type=F32) + b1_ref[0]
        gate = jnp.minimum(h1[:, :D_FF], SWIGLU_LIMIT)
        up = jnp.clip(h1[:, D_FF:], -SWIGLU_LIMIT, SWIGLU_LIMIT)
        act = gate * jax.nn.sigmoid(SWIGLU_ALPHA * gate) * (up + 1.0)
        y = jnp.dot(act.astype(BF16), w2b_ref[...], preferred_element_type=F32) + b2_ref[0]
        ybuf_ref[slot, 0:n] = _as_granules(_pack_slot_rows(y))
        out_copy(k, slot, m_sub).start(priority=BLOCK_DMA_PRIORITY)

    def full_block(k, carry):
        process(k, subs)
        return carry

    lax.fori_loop(0, nb_full, full_block, 0)

    for m_sub in range(1, subs):
        @pl.when(rem == m_sub)
        def _(m_sub=m_sub):
            process(nb_full, m_sub)

    @pl.when(nb >= 2)
    def _():
        out_copy(nb - 2, 1 - last_slot, subs).wait()

    @pl.when((nb >= 1) & (rem == 0))
    def _():
        out_copy(nb - 1, last_slot, subs).wait()

    for m_sub in range(1, subs):
        @pl.when(rem == m_sub)
        def _(m_sub=m_sub):
            out_copy(nb - 1, last_slot, m_sub).wait()


def _combine_kernel(n_prompt_tiles, cnt_ref, loff_ref, goff_ref, tot_ref, big_ref,
                    routet_ref, x1_ref, gf_ref, ys_hbm, yp_ref, ysm_ref, stage_ref, sem):
    i = pl.program_id(0)
    n_steps = pl.num_programs(0)
    slot = lax.rem(i, 2)
    T = TOK_TILE

    def seg_copy(tile_slot):
        return lambda s, d, n: pltpu.make_async_copy(
            ys_hbm.at[pl.ds(d, n)], stage_ref.at[tile_slot, pl.ds(s, n)], sem.at[tile_slot])

    @pl.when(i == 0)
    def _():
        stage_ref[...] = jnp.zeros(stage_ref.shape, U32)
        _start_tile_copies(i, i >= 0, cnt_ref, loff_ref, goff_ref, big_ref, seg_copy(slot))

    _start_tile_copies(jnp.minimum(i + 1, n_steps - 1), i + 1 < n_steps, cnt_ref, loff_ref, goff_ref, big_ref,
                       seg_copy(1 - slot))

    rt = routet_ref[...]
    l_iota = lax.broadcasted_iota(I32, (T, LOCAL_ROWS), 1)
    pw = jnp.zeros((T, LOCAL_ROWS), F32)
    for k in range(TOP_K):
        pos_k = rt[:, TOP_K + k:TOP_K + k + 1].astype(I32)
        pw = jnp.where(l_iota == pos_k, rt[:, 2 * TOP_K + k:2 * TOP_K + k + 1], pw)

    _wait_granules(tot_ref[i], TILE_BITS, lambda n: pltpu.make_async_copy(
        ys_hbm.at[pl.ds(0, n)], stage_ref.at[slot, pl.ds(0, n)], sem.at[slot]))
    moe = jnp.dot(pw.astype(BF16), _unpack_slot_rows(_as_rows(stage_ref[slot])), preferred_element_type=F32)
    y = _rms_norm(x1_ref[...] + moe, gf_ref[...])

    @pl.when(i < n_prompt_tiles)
    def _():
        yp_ref[...] = y

    @pl.when(i >= n_prompt_tiles)
    def _():
        ysm_ref[...] = y


def _t5_bucket(rel):
    half = NUM_BUCKETS // 2
    max_exact = half // 2
    ret = jnp.where(rel > 0, half, 0)
    n = jnp.abs(rel)
    nf = jnp.maximum(n, 1).astype(F32)
    large = max_exact + (jnp.log(nf / max_exact) / math.log(MAX_DISTANCE / max_exact)
                         * (half - max_exact)).astype(jnp.int32)
    large = jnp.minimum(large, half - 1)
    return ret + jnp.where(n < max_exact, n, large)


def _stacked_bias(table, sinks, q_pos, k_pos):
    nq, nk = q_pos.shape[0], k_pos.shape[0]
    bucket = _t5_bucket(k_pos[None, :] - q_pos[:, None])
    onehot = (bucket[:, :, None] == jnp.arange(NUM_BUCKETS)).astype(F32)
    bias = jnp.einsum('qkb,bh->hqk', onehot, table.astype(F32), precision=lax.Precision.HIGHEST)
    bias = bias.reshape(N_KV, REP * nq, nk)
    sink = jnp.repeat(sinks.astype(F32).reshape(N_KV, REP, 1), nq, axis=2).reshape(N_KV, REP * nq, 1)
    tail = jnp.full((N_KV, REP * nq, KEY_PAD - nk - 1), -jnp.inf, F32)
    return jnp.concatenate([bias, sink, tail], axis=-1)


def kernel(x_prompt, x_sample, cache_win_k, cache_win_v, norm1_g, w_in, attn_sinks, rel_bias_table, sgu_ln_g, sgu_ln_b, sgu_w, sgu_b, w_o_att, w_o_sgu, w_out, norm2_g, w_router, b_router, w_exp_in, b_exp_in, w_exp_out, b_exp_out, final_norm_g):
    batch, seq, _ = x_prompt.shape
    dec_batch, dec_seq, _ = x_sample.shape
    cache_rows = cache_win_k.shape[2]
    assert x_prompt.shape[2] == D_MODEL and w_in.shape == (1, D_MODEL, D_IN)
    assert seq % TOK_TILE == 0 and TOK_TILE % GMLP_CHUNK == 0 and TOK_TILE >= WINDOW
    assert TOK_TILE % dec_seq == 0 and (dec_batch * dec_seq) % TOK_TILE == 0
    assert dec_seq <= GMLP_CHUNK and GMLP_CHUNK % dec_seq == 0 and cache_rows == WINDOW
    T = TOK_TILE
    n_prompt = batch * seq
    n_sample = dec_batch * dec_seq
    n_tok = n_prompt + n_sample
    tiles_per_seq = seq // T
    n_ptiles = n_prompt // T
    n_stiles = n_sample // T
    n_tiles = n_ptiles + n_stiles
    seqs_per_tile = T // dec_seq

    w_in_b = w_in[0].astype(BF16)
    w_q_b = w_in_b[:, :D_ATT].reshape(D_MODEL, N_KV, REP, HEAD_DIM).transpose(0, 2, 1, 3).reshape(
        D_MODEL, D_ATT)
    woa_b = w_o_att[0].reshape(N_KV, REP, HEAD_DIM, D_MODEL).transpose(1, 0, 2, 3).reshape(
        D_ATT, D_MODEL).astype(BF16)
    wos_b = w_o_sgu[0].astype(BF16)
    wout_b = w_out[0].astype(BF16)
    wr_b = jnp.pad(w_router[0], ((0, 0), (0, LANES - N_EXPERTS))).astype(BF16)
    br_col = b_router[0].astype(F32).reshape(N_EXPERTS, 1)
    g1 = norm1_g[0].reshape(1, D_MODEL)
    g2 = norm2_g[0].reshape(1, D_MODEL)
    gf = final_norm_g.reshape(1, D_MODEL)
    lng = sgu_ln_g[0].reshape(1, D_GMLP)
    lnb = sgu_ln_b[0].reshape(1, D_GMLP)
    tril = jnp.tril(jnp.ones((GMLP_CHUNK, GMLP_CHUNK), dtype=bool))
    sguw_p = jnp.where(tril[None], sgu_w[0], 0).astype(BF16)
    sgub_p = jnp.broadcast_to(sgu_b[0][:, :, None], (N_GROUPS, GMLP_CHUNK, GROUP_W)).astype(F32)
    reps = GMLP_CHUNK // dec_seq
    corner = jnp.where(tril[None, :dec_seq, :dec_seq], sgu_w[0][:, :dec_seq, :dec_seq], 0)
    sguw_s = jnp.einsum('ab,gij->gaibj', jnp.eye(reps, dtype=F32), corner).reshape(
        N_GROUPS, GMLP_CHUNK, GMLP_CHUNK).astype(BF16)
    sgub_s = jnp.broadcast_to(jnp.tile(sgu_b[0][:, :dec_seq], (1, reps))[:, :, None],
                              (N_GROUPS, GMLP_CHUNK, GROUP_W)).astype(F32)
    bias_p = _stacked_bias(rel_bias_table, attn_sinks[0], jnp.arange(CHUNK) + WINDOW, jnp.arange(BAND))
    bias_s = _stacked_bias(rel_bias_table, attn_sinks[0], cache_rows + jnp.arange(dec_seq),
                           jnp.arange(cache_rows + dec_seq))
    tri = jnp.triu(jnp.ones((T, T), F32), k=1).astype(BF16)
    low = jnp.tril(jnp.ones((N_EXPERTS, N_EXPERTS), F32), k=-1).astype(BF16)

    layer_out_shapes = (
        jax.ShapeDtypeStruct((n_tok, D_MODEL), F32),
        jax.ShapeDtypeStruct((n_tok, D_MODEL), BF16),
        jax.ShapeDtypeStruct((n_tiles, ROUTE_ROWS, T), F32),
        jax.ShapeDtypeStruct((n_tok, LANES), F32),
        jax.ShapeDtypeStruct((n_tiles, N_EXPERTS, LANES), F32),
    )
    shared_consts = (w_q_b, w_in_b)
    tail_consts_p = (sguw_p, sgub_p, lng, lnb, woa_b, wos_b, wout_b, g2, wr_b, br_col, tri, low)
    tail_consts_s = (sguw_s, sgub_s, lng, lnb, woa_b, wos_b, wout_b, g2, wr_b, br_col, tri, low)

    def tile_specs(tile_of):
        return [
            pl.BlockSpec((T, D_MODEL), lambda *g: (tile_of(*g), 0)),
            pl.BlockSpec((T, D_MODEL), lambda *g: (tile_of(*g), 0)),
            pl.BlockSpec((1, ROUTE_ROWS, T), lambda *g: (tile_of(*g), 0, 0)),
            pl.BlockSpec((T, LANES), lambda *g: (tile_of(*g), 0)),
            pl.BlockSpec((1, N_EXPERTS, LANES), lambda *g: (tile_of(*g), 0, 0)),
        ]

    prompt_consts = (g1,) + shared_consts + (bias_p,) + tail_consts_p
    x1, h, route, routet, cnt, kvwin = pl.pallas_call(
        _prompt_layer_kernel,
        grid=(batch, tiles_per_seq),
        in_specs=[pl.BlockSpec((1, T, D_MODEL), lambda b, j: (b, j, 0))]
                 + [_const_spec(c.shape) for c in prompt_consts],
        out_specs=tile_specs(lambda b, j: b * tiles_per_seq + j)
                  + [pl.BlockSpec((1, WINDOW, 2 * D_KV), lambda b, j: (b, 0, 0))],
        out_shape=layer_out_shapes + (jax.ShapeDtypeStruct((batch, WINDOW, 2 * D_KV), F32),),
        scratch_shapes=[pltpu.VMEM((T + WINDOW, 2 * D_KV), BF16),
                        pltpu.VMEM((T, D_ATT), BF16),
                        pltpu.VMEM((T, D_GMLP), BF16)],
        compiler_params=pltpu.CompilerParams(dimension_semantics=("arbitrary", "arbitrary"),
                                             vmem_limit_bytes=VMEM_LIMIT),
        name="layer_prompt",
    )(x_prompt, *prompt_consts)

    xs_flat = x_sample.reshape(n_sample, D_MODEL)
    ck = cache_win_k[0].reshape(dec_batch, cache_rows, D_KV)
    cv = cache_win_v[0].reshape(dec_batch, cache_rows, D_KV)
    sample_consts = (g1,) + shared_consts + (bias_s,) + tail_consts_s
    any_spec = pl.BlockSpec(memory_space=pl.ANY)
    x1, h, route, routet, cnt, kvnew, vn_s = pl.pallas_call(
        _sample_layer_kernel,
        grid=(n_stiles,),
        in_specs=[any_spec] * 5
                 + [pl.BlockSpec((T, D_MODEL), lambda i: (i, 0)),
                    pl.BlockSpec((seqs_per_tile, cache_rows, D_KV), lambda i: (i, 0, 0)),
                    pl.BlockSpec((seqs_per_tile, cache_rows, D_KV), lambda i: (i, 0, 0))]
                 + [_const_spec(c.shape) for c in sample_consts],
        out_specs=tile_specs(lambda i: n_ptiles + i)
                  + [pl.BlockSpec((T, 2 * D_KV), lambda i: (i, 0)),
                     pl.BlockSpec((T, D_GMLP), lambda i: (i, 0))],
        out_shape=layer_out_shapes + (jax.ShapeDtypeStruct((n_sample, 2 * D_KV), F32),
                                      jax.ShapeDtypeStruct((n_sample, D_GMLP), F32)),
        scratch_shapes=[pltpu.VMEM((T, D_ATT), BF16), pltpu.VMEM((T, D_GMLP), BF16)],
        input_output_aliases={0: 0, 1: 1, 2: 2, 3: 3, 4: 4},
        compiler_params=pltpu.CompilerParams(dimension_semantics=("arbitrary",),
                                             vmem_limit_bytes=VMEM_LIMIT),
        name="layer_sample",
    )(x1, h, route, routet, cnt, xs_flat, ck, cv, *sample_consts)

    blk_gran = SUB_BLOCK // GRAN
    counts =cnt[:, :, 0].astype(I32)
    seg_gran = (counts + (GRAN - 1)) // GRAN
    local_off = jnp.cumsum(seg_gran, axis=1) - seg_gran
    tot_gran = jnp.sum(seg_gran, axis=0)
    ptot_gran = (tot_gran + (blk_gran - 1)) // blk_gran * blk_gran
    pend_gran = jnp.cumsum(ptot_gran)
    gstart = pend_gran - ptot_gran
    global_off = gstart[None, :] + jnp.cumsum(seg_gran, axis=0) - seg_gran
    pad_n = ptot_gran - tot_gran
    pad_off = gstart + tot_gran
    n_rows = -(-(TOP_K * n_tok + n_tiles * N_EXPERTS * (GRAN - 1) + N_EXPERTS * (SUB_BLOCK - GRAN))
               // SUB_BLOCK) * SUB_BLOCK + ROW_BLOCK
    region_gran0 = gstart.astype(I32)
    region_subs = (ptot_gran // blk_gran).astype(I32)
    tile_gran = jnp.sum(seg_gran, axis=1).astype(I32)
    tile_big = jnp.any(seg_gran >= 4 * (FLAT_QUADS + 1), axis=1).astype(I32)
    seg_gran_f = seg_gran.reshape(-1).astype(I32)
    local_off_f = local_off.reshape(-1).astype(I32)
    global_off_f = global_off.reshape(-1).astype(I32)

    xs = pl.pallas_call(
        _dispatch_kernel,
        grid_spec=pltpu.PrefetchScalarGridSpec(
            num_scalar_prefetch=7,
            grid=(n_tiles,),
            in_specs=[pl.BlockSpec((1, ROUTE_ROWS, T), lambda i, *_: (i, 0, 0)),
                      pl.BlockSpec((T, D_MODEL), lambda i, *_: (i, 0))],
            out_specs=pl.BlockSpec(memory_space=pl.ANY),
            scratch_shapes=[pltpu.VMEM((2, LOCAL_ROWS // GRAN, GRAN, SLOT_WORDS), U32),
                            pltpu.VMEM((SUB_BLOCK // GRAN, GRAN, SLOT_WORDS), U32),
                            pltpu.SemaphoreType.DMA((3,))]),
        out_shape=jax.ShapeDtypeStruct((n_rows // GRAN, GRAN, SLOT_WORDS), U32),
        compiler_params=pltpu.CompilerParams(dimension_semantics=("arbitrary",),
                                             vmem_limit_bytes=VMEM_LIMIT),
        name="moe_dispatch",
    )(seg_gran_f, local_off_f, global_off_f, tile_gran, tile_big, pad_n.astype(I32), pad_off.astype(I32),
      route, h)

    ys = pl.pallas_call(
        _expert_kernel,
        grid_spec=pltpu.PrefetchScalarGridSpec(
            num_scalar_prefetch=2,
            grid=(N_EXPERTS,),
            in_specs=[pl.BlockSpec(memory_space=pl.ANY),
                      pl.BlockSpec((1, D_MODEL, 2 * D_FF), lambda e, *_: (e, 0, 0)),
                      pl.BlockSpec((1, 1, 2 * D_FF), lambda e, *_: (e, 0, 0)),
                      pl.BlockSpec((1, D_FF, D_MODEL), lambda e, *_: (e, 0, 0)),
                      pl.BlockSpec((1, 1, D_MODEL), lambda e, *_: (e, 0, 0))],
            out_specs=pl.BlockSpec(memory_space=pl.ANY),
            scratch_shapes=[pltpu.VMEM((D_MODEL, 2 * D_FF), BF16), pltpu.VMEM((D_FF, D_MODEL), BF16),
                            pltpu.VMEM((2, ROW_BLOCK // GRAN, GRAN, SLOT_WORDS), U32),
                            pltpu.VMEM((2, ROW_BLOCK // GRAN, GRAN, SLOT_WORDS), U32),
                            pltpu.SemaphoreType.DMA((2,)), pltpu.SemaphoreType.DMA((2,))]),
        out_shape=jax.ShapeDtypeStruct((n_rows // GRAN, GRAN, SLOT_WORDS), U32),
        compiler_params=pltpu.CompilerParams(dimension_semantics=("arbitrary",),
                                             vmem_limit_bytes=VMEM_LIMIT),
        name="moe_experts",
    )(region_gran0, region_subs, xs, w_exp_in[0], b_exp_in[0].reshape(N_EXPERTS, 1, 2 * D_FF),
      w_exp_out[0], b_exp_out[0].reshape(N_EXPERTS, 1, D_MODEL))

    y_p, y_s = pl.pallas_call(
        functools.partial(_combine_kernel, n_ptiles),
        grid_spec=pltpu.PrefetchScalarGridSpec(
            num_scalar_prefetch=5,
            grid=(n_tiles,),
            in_specs=[pl.BlockSpec((T, LANES), lambda i, *_: (i, 0)),
                      pl.BlockSpec((T, D_MODEL), lambda i, *_: (i, 0)),
                      pl.BlockSpec((1, D_MODEL), lambda i, *_: (0, 0)),
                      pl.BlockSpec(memory_space=pl.ANY)],
            out_specs=[pl.BlockSpec((T, D_MODEL), lambda i, *_: (jnp.minimum(i, n_ptiles - 1), 0)),
                       pl.BlockSpec((T, D_MODEL), lambda i, *_: (jnp.maximum(i - n_ptiles, 0), 0))],
            scratch_shapes=[pltpu.VMEM((2, LOCAL_ROWS // GRAN, GRAN, SLOT_WORDS), U32), pltpu.SemaphoreType.DMA((2,))]),
        out_shape=(jax.ShapeDtypeStruct((n_prompt, D_MODEL), F32),
                   jax.ShapeDtypeStruct((n_sample, D_MODEL), F32)),
        compiler_params=pltpu.CompilerParams(dimension_semantics=("arbitrary",),
                                             vmem_limit_bytes=VMEM_LIMIT),
        name="moe_combine",
    )(seg_gran_f, local_off_f, global_off_f, tile_gran, tile_big, routet, x1, gf, ys)

    y_prompt = y_p.reshape(batch, seq, D_MODEL)
    y_sample = y_s.reshape(dec_batch, dec_seq, D_MODEL)
    new_win_k_prompt = kvwin[:, :, :D_KV].reshape(1, batch, WINDOW, N_KV, HEAD_DIM)
    new_win_v_prompt = kvwin[:, :, D_KV:].reshape(1, batch, WINDOW, N_KV, HEAD_DIM)
    new_win_k_sample = kvnew[:, :D_KV].reshape(1, dec_batch, dec_seq, N_KV, HEAD_DIM)
    new_win_v_sample = kvnew[:, D_KV:].reshape(1, dec_batch, dec_seq, N_KV, HEAD_DIM)
    new_sgu_v_sample = vn_s.reshape(1, dec_batch, dec_seq, D_GMLP)
    return (y_prompt, y_sample, new_win_k_prompt, new_win_v_prompt, new_win_k_sample,
            new_win_v_sample, new_sgu_v_sample)
```

```python
import functools
import math

import numpy as np
import jax
import jax.numpy as jnp
from jax import lax
from jax.experimental import pallas as pl
from jax.experimental.pallas import tpu as pltpu

F32 = jnp.float32
BF16 = jnp.bfloat16
I32 = jnp.int32
U32 = jnp.uint32

D_MODEL = 1024
HEAD_DIM = 64
N_HEADS = 16
N_KV = 2
REP = N_HEADS // N_KV
CHUNK = 64
WINDOW = 128
BAND = WINDOW + CHUNK
KEY_PAD = 256
D_ATT = N_HEADS * HEAD_DIM
D_KV = N_KV * HEAD_DIM
NUM_BUCKETS = 32
MAX_DISTANCE = 128
GMLP_CHUNK = 128
D_GMLP = 1024
N_GROUPS = 4
GROUP_W = D_GMLP // N_GROUPS
N_EXPERTS = 32
TOP_K = 4
D_FF = 1024
SWIGLU_LIMIT = 7.0
SWIGLU_ALPHA = 1.702
NORM_EPS = 1e-5
D_IN = D_ATT + 2 * D_KV + 2 * D_GMLP + 2 * D_MODEL
COL_KV = D_ATT
COL_U = COL_KV + 2 * D_KV
COL_VG = COL_U + D_GMLP
COL_GA = COL_VG + D_GMLP
COL_GB = COL_GA + D_MODEL
SQRT_HALF = float(np.sqrt(0.5))

LANES = 128
WORD_SUBLANES = 8
VMEM_LIMIT = 56 * 1024 * 1024

TOK_TILE = 256
LAYER_TILE = 512
GRAN = WORD_SUBLANES
SLOT_WORDS = D_MODEL // 2
FLAT_QUADS = 3
LOCAL_ROWS = TOP_K * TOK_TILE + N_EXPERTS * GRAN
ROW_BLOCK = 512
SUB_BLOCK = 128
PAD_BITS = (SUB_BLOCK // GRAN - 1).bit_length()
TILE_BITS = (LOCAL_ROWS // GRAN).bit_length()
BLOCK_DMA_PRIORITY = 1
ROUTE_ROWS = 16


def _const_spec(shape):
    nd = len(shape)
    return pl.BlockSpec(shape, lambda *_: (0,) * nd, pipeline_mode=pl.Buffered(1))


def _rms_norm(x, g):
    ms = jnp.mean(x * x, axis=-1, keepdims=True)
    return x * lax.rsqrt(ms + NORM_EPS) * g


def _gelu(x):
    return 0.5 * x * (1.0 + lax.erf(x * SQRT_HALF))


def _attend(q, k, v, bias, valid):
    logits = lax.dot_general(q, k, (((1,), (1,)), ((), ())), preferred_element_type=F32) + bias
    if valid is not None:
        logits = jnp.where(valid, logits, -jnp.inf)
    m = jnp.max(logits, axis=-1, keepdims=True)
    p = jnp.exp(logits - m)
    den = jnp.sum(p, axis=-1, keepdims=True)
    w = (p * (1.0 / den)).astype(BF16)
    return jnp.dot(w, v, preferred_element_type=F32)


def _attention_rows(zq, row0, n_rows, k_ext, v_ext, bias_ref, valid, oatt_ref, heads_per_unit):
    lane = lax.broadcasted_iota(I32, (1, LANES), 1)
    for g in range(N_KV):
        in_group = (lane >= g * HEAD_DIM) & (lane < (g + 1) * HEAD_DIM)
        qmask = jnp.where(in_group, HEAD_DIM ** -0.5, 0.0).astype(BF16)
        for part in range(REP // heads_per_unit):
            tiles = range(part * heads_per_unit, (part + 1) * heads_per_unit)
            q = jnp.concatenate([zq[row0:row0 + n_rows, t * LANES:(t + 1) * LANES] * qmask for t in tiles],
                                axis=0)
            b0 = part * heads_per_unit * n_rows
            o = _attend(q, k_ext, v_ext, bias_ref[g, b0:b0 + heads_per_unit * n_rows, :], valid)
            for u, t in enumerate(tiles):
                c0 = t * LANES + g * HEAD_DIM
                oatt_ref[row0:row0 + n_rows, c0:c0 + HEAD_DIM] = (
                    o[u * n_rows:(u + 1) * n_rows, g * HEAD_DIM:(g + 1) * HEAD_DIM].astype(BF16))


def _layer_tail(x, xn_b, oatt_ref, osgu_ref, w_in_ref, sguw_ref, sgub_ref, lng_ref, lnb_ref,
                woa_ref, wos_ref, wout_ref, g2_ref, wr_ref, br_ref, tri_ref, low_ref,
                x1_ref, h_ref, route_ref, routet_ref, cnt_ref, vn_ref):
    T = x.shape[0]
    u = _gelu(jnp.dot(xn_b, w_in_ref[:, COL_U:COL_VG], preferred_element_type=F32))
    vg = _gelu(jnp.dot(xn_b, w_in_ref[:, COL_VG:COL_GA], preferred_element_type=F32))
    mu = jnp.mean(vg, axis=-1, keepdims=True)
    var = jnp.mean(jnp.square(vg - mu), axis=-1, keepdims=True)
    vn = (vg - mu) * lax.rsqrt(var + NORM_EPS) * lng_ref[...] + lnb_ref[...]
    if vn_ref is not None:
        vn_ref[...] = vn
    vn_b = vn.astype(BF16)
    for j in range(T // GMLP_CHUNK):
        rows = slice(j * GMLP_CHUNK, (j + 1) * GMLP_CHUNK)
        for g in range(N_GROUPS):
            cols = slice(g * GROUP_W, (g + 1) * GROUP_W)
            mixed = jnp.dot(sguw_ref[g], vn_b[rows, cols], preferred_element_type=F32) + sgub_ref[g]
            osgu_ref[rows, cols] = (u[rows, cols] * mixed).astype(BF16)

    ga = jax.nn.sigmoid(jnp.dot(xn_b, w_in_ref[:, COL_GA:COL_GB], preferred_element_type=F32))
    merged = ga * jnp.dot(oatt_ref[...], woa_ref[...], preferred_element_type=F32)
    gb = jax.nn.sigmoid(jnp.dot(xn_b, w_in_ref[:, COL_GB:D_IN], preferred_element_type=F32))
    merged = merged + gb * jnp.dot(osgu_ref[...], wos_ref[...], preferred_element_type=F32)
    x1 = x + jnp.dot(merged.astype(BF16), wout_ref[...], preferred_element_type=F32)
    x1_ref[...] = x1

    h_b = _rms_norm(x1, g2_ref[...]).astype(BF16)
    h_ref[...] = h_b
    logits = jnp.dot(h_b, wr_ref[...], preferred_element_type=F32)
    lt = jnp.transpose(logits)[:N_EXPERTS, :] + br_ref[...]
    for s in range(T // TOK_TILE):
        rec, cnt_b = _route_sort_tile(lt[:, s * TOK_TILE:(s + 1) * TOK_TILE], tri_ref, low_ref)
        route_ref[s] = rec
        rec_pad = jnp.concatenate([rec, jnp.zeros((LANES - ROUTE_ROWS, TOK_TILE), F32)], axis=0)
        routet_ref[s * TOK_TILE:(s + 1) * TOK_TILE, :] = jnp.transpose(rec_pad)
        cnt_ref[s] = cnt_b


def _route_sort_tile(lt, tri_ref, low_ref):
    T = lt.shape[1]
    e_iota = lax.broadcasted_iota(I32, (N_EXPERTS, T), 0).astype(F32)
    cur = lt
    vals, idxs = [], []
    for _ in range(TOP_K):
        m = jnp.max(cur, axis=0, keepdims=True)
        ik = jnp.min(jnp.where(cur == m, e_iota, float(N_EXPERTS)), axis=0, keepdims=True)
        vals.append(m)
        idxs.append(ik)
        cur = jnp.where(e_iota == ik, -jnp.inf, cur)
    exps = [jnp.exp(v - vals[0]) for v in vals]
    den = exps[0] + exps[1] + exps[2] + exps[3]
    gates = [e / den for e in exps]

    onehot = jnp.zeros((N_EXPERTS, T), F32)
    for ik in idxs:
        onehot = onehot + jnp.where(e_iota == ik, 1.0, 0.0)
    rank = jnp.dot(onehot.astype(BF16), tri_ref[...], preferred_element_type=F32)
    cnt = jnp.sum(onehot, axis=1, keepdims=True)
    cnt_b = jnp.broadcast_to(cnt, (N_EXPERTS, LANES))
    gran = jnp.floor((cnt_b + (GRAN - 1)) * (1.0 / GRAN))
    off = jnp.dot(low_ref[...], gran.astype(BF16), preferred_element_type=F32) * GRAN
    base = off[:, 0:1] + rank
    poss = [jnp.sum(jnp.where(e_iota == ik, base, 0.0), axis=0, keepdims=True) for ik in idxs]

    rec = jnp.concatenate(idxs + poss + gates
                          + [jnp.zeros((ROUTE_ROWS - 3 * TOP_K, T), F32)], axis=0)
    return rec, cnt_b


def _prompt_layer_kernel(x_ref, g1_ref, wq_ref, w_in_ref, bias_ref, sguw_ref, sgub_ref, lng_ref,
                         lnb_ref, woa_ref, wos_ref, wout_ref, g2_ref, wr_ref, br_ref, tri_ref, low_ref,
                         x1_ref, h_ref, route_ref, routet_ref, cnt_ref, kvwin_ref,
                         kvx_ref, oatt_ref, osgu_ref):
    T = LAYER_TILE
    j = pl.program_id(1)
    x = x_ref[0]
    xn_b = _rms_norm(x, g1_ref[...]).astype(BF16)
    zq = jnp.dot(xn_b, wq_ref[...], preferred_element_type=F32).astype(BF16)
    zkv = jnp.dot(xn_b, w_in_ref[:, COL_KV:COL_U], preferred_element_type=F32)

    @pl.when(j == pl.num_programs(1) - 1)
    def _():
        kvwin_ref[0] = zkv[T - WINDOW:, :]

    @pl.when(j == 0)
    def _():
        kvx_ref[0:WINDOW, :] = jnp.zeros((WINDOW, 2 * D_KV), BF16)

    @pl.when(j > 0)
    def _():
        kvx_ref[0:WINDOW, :] = kvx_ref[T:T + WINDOW, :]

    kvx_ref[WINDOW:, :] = zkv.astype(BF16)

    col = lax.broadcasted_iota(I32, (1, KEY_PAD), 1)
    key_pad = jnp.zeros((KEY_PAD - BAND, 2 * D_KV), BF16)
    for c in range(T // CHUNK):
        r0 = c * CHUNK
        kvb = jnp.concatenate([kvx_ref[r0:r0 + BAND, :], key_pad], axis=0)
        valid = ((col + (j * T + r0 - WINDOW)) >= 0) | (col >= BAND) if r0 < WINDOW else None
        _attention_rows(zq, r0, CHUNK, kvb[:, :D_KV], kvb[:, D_KV:], bias_ref, valid, oatt_ref,
                        heads_per_unit=4)

    _layer_tail(x, xn_b, oatt_ref, osgu_ref, w_in_ref, sguw_ref, sgub_ref, lng_ref, lnb_ref,
                woa_ref, wos_ref, wout_ref, g2_ref, wr_ref, br_ref, tri_ref, low_ref,
                x1_ref, h_ref, route_ref, routet_ref, cnt_ref, None)


def _sample_layer_kernel(x1_in, h_in, route_in, routet_in, cnt_in,
                         x_ref, ck_ref, cv_ref, g1_ref, wq_ref, w_in_ref, bias_ref, sguw_ref, sgub_ref,
                         lng_ref, lnb_ref, woa_ref, wos_ref, wout_ref, g2_ref, wr_ref, br_ref, tri_ref,
                         low_ref,
                         x1_ref, h_ref, route_ref, routet_ref, cnt_ref, kvnew_ref, vn_ref,
                         oatt_ref, osgu_ref):
    del x1_in, h_in, route_in, routet_in, cnt_in
    T = LAYER_TILE
    nq = x_ref.shape[0] // ck_ref.shape[0]
    x = x_ref[...]
    xn_b = _rms_norm(x, g1_ref[...]).astype(BF16)
    zq = jnp.dot(xn_b, wq_ref[...], preferred_element_type=F32).astype(BF16)
    zkv = jnp.dot(xn_b, w_in_ref[:, COL_KV:COL_U], preferred_element_type=F32)
    kvnew_ref[...] = zkv
    zkv_b = zkv.astype(BF16)
    n_cache = ck_ref.shape[1]
    key_pad = jnp.zeros((KEY_PAD - n_cache - nq, D_KV), BF16)
    for b in range(T // nq):
        r0 = b * nq
        kk = jnp.concatenate([ck_ref[b].astype(BF16), zkv_b[r0:r0 + nq, :D_KV], key_pad], axis=0)
        vv = jnp.concatenate([cv_ref[b].astype(BF16), zkv_b[r0:r0 + nq, D_KV:], key_pad], axis=0)
        _attention_rows(zq, r0, nq, kk, vv, bias_ref, None, oatt_ref, heads_per_unit=REP)

    _layer_tail(x, xn_b, oatt_ref, osgu_ref, w_in_ref, sguw_ref, sgub_ref, lng_ref, lnb_ref,
                woa_ref, wos_ref, wout_ref, g2_ref, wr_ref, br_ref, tri_ref, low_ref,
                x1_ref, h_ref, route_ref, routet_ref, cnt_ref, vn_ref)


def _pack_slot_rows(x):
    lo = lax.bitcast_convert_type(x[:, :SLOT_WORDS].astype(BF16).astype(F32), U32)
    hi = lax.bitcast_convert_type(x[:, SLOT_WORDS:].astype(BF16).astype(F32), U32)
    return (lo >> 16) | hi


def _unpack_slot_rows(w):
    lo = lax.bitcast_convert_type(w << 16, F32)
    hi = lax.bitcast_convert_type(w & jnp.uint32(0xFFFF0000), F32)
    return jnp.concatenate([lo, hi], axis=1).astype(BF16)


def _as_granules(x):
    return x.reshape(x.shape[0] // GRAN, GRAN, SLOT_WORDS)


def _as_rows(x):
    return x.reshape(x.shape[0] * GRAN, SLOT_WORDS)


def _segment_copies(n_gran, src_gran, dst_gran, bits, make_copy, act):
    for b in range(min(bits, 2)):
        @pl.when(((n_gran >> b) & 1) == 1)
        def _(b=b):
            done = n_gran & ((1 << b) - 1)
            act(make_copy(src_gran + done, dst_gran + done, 1 << b))

    if bits > 2:
        def quad(q, carry):
            done = (n_gran & 3) + 4 * q
            act(make_copy(src_gran + done, dst_gran + done, 4))
            return carry
        lax.fori_loop(0, n_gran >> 2, quad, 0)


def _start_tile_copies(tile, live, cnt_ref, loff_ref, goff_ref, big_ref, make_copy):
    for e in range(N_EXPERTS):
        t = tile * N_EXPERTS + e
        n = jnp.where(live, cnt_ref[t], 0)
        src, dst = loff_ref[t], goff_ref[t]
        for b in range(2):
            @pl.when(((n >> b) & 1) == 1)
            def _(b=b, n=n, src=src, dst=dst):
                done = n & ((1 << b) - 1)
                make_copy(src + done, dst + done, 1 << b).start()
        for q in range(FLAT_QUADS):
            @pl.when((n >> 2) > q)
            def _(q=q, n=n, src=src, dst=dst):
                done = (n & 3) + 4 * q
                make_copy(src + done, dst + done, 4).start()

    @pl.when(live & (big_ref[tile] > 0))
    def _():
        def rest(e, carry):
            t = tile * N_EXPERTS + e
            n, src, dst = cnt_ref[t], loff_ref[t], goff_ref[t]

            def quad(q, c):
                done = (n & 3) + 4 * q
                make_copy(src + done, dst + done, 4).start()
                return c
            lax.fori_loop(FLAT_QUADS, n >> 2, quad, 0)
            return carry
        lax.fori_loop(0, N_EXPERTS, rest, 0)


def _wait_granules(n_gran, bits, make_copy):
    for b in range(bits):
        @pl.when(((n_gran >> b) & 1) == 1)
        def _(b=b):
            make_copy(1 << b).wait()


def _dispatch_kernel(cnt_ref, loff_ref, goff_ref, tot_ref, big_ref, padn_ref, padoff_ref,
                     route_ref, h_ref, xs_hbm, stage_ref, zero_ref, sem):
    i = pl.program_id(0)
    last = pl.num_programs(0) - 1
    slot = lax.rem(i, 2)
    T = TOK_TILE

    def seg_copy(tile_slot):
        return lambda s, d, n: pltpu.make_async_copy(
            stage_ref.at[tile_slot, pl.ds(s, n)], xs_hbm.at[pl.ds(d, n)], sem.at[tile_slot])

    def wait_tile(tile, tile_slot):
        _wait_granules(tot_ref[tile], TILE_BITS, lambda n: pltpu.make_async_copy(
            stage_ref.at[tile_slot, pl.ds(0, n)], xs_hbm.at[pl.ds(0, n)], sem.at[tile_slot]))

    @pl.when(i > 1)
    def _():
        wait_tile(i - 2, slot)

    _start_tile_copies(jnp.maximum(i - 1, 0), i > 0, cnt_ref, loff_ref, goff_ref, big_ref, seg_copy(1 - slot))

    pos = route_ref[0, TOP_K:2 * TOP_K, :].astype(I32)
    r_iota = lax.broadcasted_iota(I32, (LOCAL_ROWS, T), 0)
    p = jnp.zeros((LOCAL_ROWS, T), F32)
    for k in range(TOP_K):
        p = jnp.where(r_iota == pos[k:k + 1, :], 1.0, p)
    stage_ref[slot] = _as_granules(
        _pack_slot_rows(jnp.dot(p.astype(BF16), h_ref[...], preferred_element_type=F32)))

    @pl.when(i == last)
    def _():
        _start_tile_copies(i, i >= 0, cnt_ref, loff_ref, goff_ref, big_ref, seg_copy(slot))

        @pl.when(i > 0)
        def _():
            wait_tile(i - 1, 1 - slot)

        wait_tile(i, slot)
        zero_ref[...] = jnp.zeros(zero_ref.shape, U32)

        def pad_copy(s, d, n):
            return pltpu.make_async_copy(zero_ref.at[pl.ds(s, n)], xs_hbm.at[pl.ds(d, n)], sem.at[2])

        def for_pads(act):
            def body(e, carry):
                _segment_copies(padn_ref[e], 0, padoff_ref[e], PAD_BITS, pad_copy, act)
                return carry
            lax.fori_loop(0, N_EXPERTS, body, 0)

        for_pads(lambda cp: cp.start())
        for_pads(lambda cp: cp.wait())


def _expert_kernel(gran0_ref, nsub_ref, xs_hbm, w1_ref, b1_ref, w2_ref, b2_ref, ys_hbm,
                   w1b_ref, w2b_ref, xbuf_ref, ybuf_ref, sem_in, sem_out):
    e = pl.program_id(0)
    subs = ROW_BLOCK // SUB_BLOCK
    sub_gran = SUB_BLOCK // GRAN
    blk_gran = ROW_BLOCK // GRAN
    gran0 = gran0_ref[e]
    nsub = nsub_ref[e]
    nb_full = nsub // subs
    rem = nsub - nb_full * subs
    nb = nb_full + jnp.where(rem > 0, 1, 0)
    last_slot = lax.rem(nb + 1, 2)

    def in_copy(k, slot):
        return pltpu.make_async_copy(xs_hbm.at[pl.ds(gran0 + k * blk_gran, blk_gran)], xbuf_ref.at[slot],
                                     sem_in.at[slot])

    def out_copy(k, slot, m_sub):
        n = m_sub * sub_gran
        return pltpu.make_async_copy(ybuf_ref.at[slot, pl.ds(0, n)], ys_hbm.at[pl.ds(gran0 + k * blk_gran, n)],
                                     sem_out.at[slot])

    @pl.when(nb > 0)
    def _():
        in_copy(0, 0).start(priority=BLOCK_DMA_PRIORITY)

    w1b_ref[...] = w1_ref[0].astype(BF16)
    w2b_ref[...] = w2_ref[0].astype(BF16)

    def process(k, m_sub):
        n = m_sub * sub_gran
        slot = lax.rem(k, 2)

        @pl.when(k + 1 < nb)
        def _():
            in_copy(k + 1, 1 - slot).start(priority=BLOCK_DMA_PRIORITY)

        in_copy(k, slot).wait()

        @pl.when(k >= 2)
        def _():
            out_copy(k - 2, slot, subs).wait()

        x = _unpack_slot_rows(_as_rows(xbuf_ref[slot, 0:n]))
        h1 = jnp.dot(x, w1b_ref[...], preferred_element_type=F32) + b1_ref[0]
        gate = jnp.minimum(h1[:, :D_FF], SWIGLU_LIMIT)
        up = jnp.clip(h1[:, D_FF:], -SWIGLU_LIMIT, SWIGLU_LIMIT)
        act = gate * jax.nn.sigmoid(SWIGLU_ALPHA * gate) * (up + 1.0)
        y = jnp.dot(act.astype(BF16), w2b_ref[...], preferred_element_type=F32) + b2_ref[0]
        ybuf_ref[slot, 0:n] = _as_granules(_pack_slot_rows(y))
        out_copy(k, slot, m_sub).start(priority=BLOCK_DMA_PRIORITY)

    def full_block(k, carry):
        process(k, subs)
        return carry

    lax.fori_loop(0, nb_full, full_block, 0)

    for m_sub in range(1, subs):
        @pl.when(rem == m_sub)
        def _(m_sub=m_sub):
            process(nb_full, m_sub)

    @pl.when(nb >= 2)
    def _():
        out_copy(nb - 2, 1 - last_slot, subs).wait()

    @pl.when((nb >= 1) & (rem == 0))
    def _():
        out_copy(nb - 1, last_slot, subs).wait()

    for m_sub in range(1, subs):
        @pl.when(rem == m_sub)
        def _(m_sub=m_sub):
            out_copy(nb - 1, last_slot, m_sub).wait()


def _combine_kernel(n_prompt_tiles, cnt_ref, loff_ref, goff_ref, tot_ref, big_ref,
                    routet_ref, x1_ref, gf_ref, ys_hbm, yp_ref, ysm_ref, stage_ref, sem):
    i = pl.program_id(0)
    n_steps = pl.num_programs(0)
    slot = lax.rem(i, 2)
    T = TOK_TILE

    def seg_copy(tile_slot):
        return lambda s, d, n: pltpu.make_async_copy(
            ys_hbm.at[pl.ds(d, n)], stage_ref.at[tile_slot, pl.ds(s, n)], sem.at[tile_slot])

    @pl.when(i == 0)
    def _():
        stage_ref[...] = jnp.zeros(stage_ref.shape, U32)
        _start_tile_copies(i, i >= 0, cnt_ref, loff_ref, goff_ref, big_ref, seg_copy(slot))

    _start_tile_copies(jnp.minimum(i + 1, n_steps - 1), i + 1 < n_steps, cnt_ref, loff_ref, goff_ref, big_ref,
                       seg_copy(1 - slot))

    rt = routet_ref[...]
    l_iota = lax.broadcasted_iota(I32, (T, LOCAL_ROWS), 1)
    pw = jnp.zeros((T, LOCAL_ROWS), F32)
    for k in range(TOP_K):
        pos_k = rt[:, TOP_K + k:TOP_K + k + 1].astype(I32)
        pw = jnp.where(l_iota == pos_k, rt[:, 2 * TOP_K + k:2 * TOP_K + k + 1], pw)

    _wait_granules(tot_ref[i], TILE_BITS, lambda n: pltpu.make_async_copy(
        ys_hbm.at[pl.ds(0, n)], stage_ref.at[slot, pl.ds(0, n)], sem.at[slot]))
    moe = jnp.dot(pw.astype(BF16), _unpack_slot_rows(_as_rows(stage_ref[slot])), preferred_element_type=F32)
    y = _rms_norm(x1_ref[...] + moe, gf_ref[...])

    @pl.when(i < n_prompt_tiles)
    def _():
        yp_ref[...] = y

    @pl.when(i >= n_prompt_tiles)
    def _():
        ysm_ref[...] = y


def _t5_bucket(rel):
    half = NUM_BUCKETS // 2
    max_exact = half // 2
    ret = jnp.where(rel > 0, half, 0)
    n = jnp.abs(rel)
    nf = jnp.maximum(n, 1).astype(F32)
    large = max_exact + (jnp.log(nf / max_exact) / math.log(MAX_DISTANCE / max_exact)
                         * (half - max_exact)).astype(jnp.int32)
    large = jnp.minimum(large, half - 1)
    return ret + jnp.where(n < max_exact, n, large)


def _stacked_bias(table, sinks, q_pos, k_pos):
    nq, nk = q_pos.shape[0], k_pos.shape[0]
    bucket = _t5_bucket(k_pos[None, :] - q_pos[:, None])
    onehot = (bucket[:, :, None] == jnp.arange(NUM_BUCKETS)).astype(F32)
    bias = jnp.einsum('qkb,bh->hqk', onehot, table.astype(F32), precision=lax.Precision.HIGHEST)
    bias = bias.reshape(N_KV, REP * nq, nk)
    sink = jnp.repeat(sinks.astype(F32).reshape(N_KV, REP, 1), nq, axis=2).reshape(N_KV, REP * nq, 1)
    tail = jnp.full((N_KV, REP * nq, KEY_PAD - nk - 1), -jnp.inf, F32)
    return jnp.concatenate([bias, sink, tail], axis=-1)


def kernel(x_prompt, x_sample, cache_win_k, cache_win_v, norm1_g, w_in, attn_sinks, rel_bias_table, sgu_ln_g, sgu_ln_b, sgu_w, sgu_b, w_o_att, w_o_sgu, w_out, norm2_g, w_router, b_router, w_exp_in, b_exp_in, w_exp_out, b_exp_out, final_norm_g):
    batch, seq, _ = x_prompt.shape
    dec_batch, dec_seq, _ = x_sample.shape
    cache_rows = cache_win_k.shape[2]
    assert x_prompt.shape[2] == D_MODEL and w_in.shape == (1, D_MODEL, D_IN)
    assert seq % LAYER_TILE == 0 and LAYER_TILE % TOK_TILE == 0 and TOK_TILE % GMLP_CHUNK == 0
    assert LAYER_TILE >= WINDOW and LAYER_TILE % dec_seq == 0 and (dec_batch * dec_seq) % LAYER_TILE == 0
    assert dec_seq <= GMLP_CHUNK and GMLP_CHUNK % dec_seq == 0 and cache_rows == WINDOW
    T = TOK_TILE
    n_prompt = batch * seq
    n_sample = dec_batch * dec_seq
    n_tok = n_prompt + n_sample
    LT = LAYER_TILE
    sorts_per_step = LT // T
    tiles_per_seq = seq // LT
    n_psteps = n_prompt // LT
    n_ssteps = n_sample // LT
    n_ptiles = n_prompt // T
    n_tiles = n_tok // T
    seqs_per_tile = LT // dec_seq

    w_in_b = w_in[0].astype(BF16)
    w_q_b = w_in_b[:, :D_ATT].reshape(D_MODEL, N_KV, REP, HEAD_DIM).transpose(0, 2, 1, 3).reshape(
        D_MODEL, D_ATT)
    woa_b = w_o_att[0].reshape(N_KV, REP, HEAD_DIM, D_MODEL).transpose(1, 0, 2, 3).reshape(
        D_ATT, D_MODEL).astype(BF16)
    wos_b = w_o_sgu[0].astype(BF16)
    wout_b = w_out[0].astype(BF16)
    wr_b = jnp.pad(w_router[0], ((0, 0), (0, LANES - N_EXPERTS))).astype(BF16)
    br_col = b_router[0].astype(F32).reshape(N_EXPERTS, 1)
    g1 = norm1_g[0].reshape(1, D_MODEL)
    g2 = norm2_g[0].reshape(1, D_MODEL)
    gf = final_norm_g.reshape(1, D_MODEL)
    lng = sgu_ln_g[0].reshape(1, D_GMLP)
    lnb = sgu_ln_b[0].reshape(1, D_GMLP)
    tril = jnp.tril(jnp.ones((GMLP_CHUNK, GMLP_CHUNK), dtype=bool))
    sguw_p = jnp.where(tril[None], sgu_w[0], 0).astype(BF16)
    sgub_p = jnp.broadcast_to(sgu_b[0][:, :, None], (N_GROUPS, GMLP_CHUNK, GROUP_W)).astype(F32)
    reps = GMLP_CHUNK // dec_seq
    corner = jnp.where(tril[None, :dec_seq, :dec_seq], sgu_w[0][:, :dec_seq, :dec_seq], 0)
    sguw_s = jnp.einsum('ab,gij->gaibj', jnp.eye(reps, dtype=F32), corner).reshape(
        N_GROUPS, GMLP_CHUNK, GMLP_CHUNK).astype(BF16)
    sgub_s = jnp.broadcast_to(jnp.tile(sgu_b[0][:, :dec_seq], (1, reps))[:, :, None],
                              (N_GROUPS, GMLP_CHUNK, GROUP_W)).astype(F32)
    bias_p = _stacked_bias(rel_bias_table, attn_sinks[0], jnp.arange(CHUNK) + WINDOW, jnp.arange(BAND))
    bias_s = _stacked_bias(rel_bias_table, attn_sinks[0], cache_rows + jnp.arange(dec_seq),
                           jnp.arange(cache_rows + dec_seq))
    tri = jnp.triu(jnp.ones((T, T), F32), k=1).astype(BF16)
    low = jnp.tril(jnp.ones((N_EXPERTS, N_EXPERTS), F32), k=-1).astype(BF16)

    layer_out_shapes = (
        jax.ShapeDtypeStruct((n_tok, D_MODEL), F32),
        jax.ShapeDtypeStruct((n_tok, D_MODEL), BF16),
        jax.ShapeDtypeStruct((n_tiles, ROUTE_ROWS, T), F32),
        jax.ShapeDtypeStruct((n_tok, LANES), F32),
        jax.ShapeDtypeStruct((n_tiles, N_EXPERTS, LANES), F32),
    )
    shared_consts = (w_q_b, w_in_b)
    tail_consts_p = (sguw_p, sgub_p, lng, lnb, woa_b, wos_b, wout_b, g2, wr_b, br_col, tri, low)
    tail_consts_s = (sguw_s, sgub_s, lng, lnb, woa_b, wos_b, wout_b, g2, wr_b, br_col, tri, low)

    def tile_specs(tile_of):
        return [
            pl.BlockSpec((LT, D_MODEL), lambda *g: (tile_of(*g), 0)),
            pl.BlockSpec((LT, D_MODEL), lambda *g: (tile_of(*g), 0)),
            pl.BlockSpec((sorts_per_step, ROUTE_ROWS, T), lambda *g: (tile_of(*g), 0, 0)),
            pl.BlockSpec((LT, LANES), lambda *g: (tile_of(*g), 0)),
            pl.BlockSpec((sorts_per_step, N_EXPERTS, LANES), lambda *g: (tile_of(*g), 0, 0)),
        ]

    prompt_consts = (g1,) + shared_consts + (bias_p,) + tail_consts_p
    x1, h, route, routet, cnt, kvwin = pl.pallas_call(
        _prompt_layer_kernel,
        grid=(batch, tiles_per_seq),
        in_specs=[pl.BlockSpec((1, LT, D_MODEL), lambda b, j: (b, j, 0))]
                 + [_const_spec(c.shape) for c in prompt_consts],
        out_specs=tile_specs(lambda b, j: b * tiles_per_seq + j)
                  + [pl.BlockSpec((1, WINDOW, 2 * D_KV), lambda b, j: (b, 0, 0))],
        out_shape=layer_out_shapes + (jax.ShapeDtypeStruct((batch, WINDOW, 2 * D_KV), F32),),
        scratch_shapes=[pltpu.VMEM((LT + WINDOW, 2 * D_KV), BF16),
                        pltpu.VMEM((LT, D_ATT), BF16),
                        pltpu.VMEM((LT, D_GMLP), BF16)],
        compiler_params=pltpu.CompilerParams(dimension_semantics=("arbitrary", "arbitrary"),
                                             vmem_limit_bytes=VMEM_LIMIT),
        name="layer_prompt",
    )(x_prompt, *prompt_consts)

    xs_flat = x_sample.reshape(n_sample, D_MODEL)
    ck = cache_win_k[0].reshape(dec_batch, cache_rows, D_KV)
    cv = cache_win_v[0].reshape(dec_batch, cache_rows, D_KV)
    sample_consts = (g1,) + shared_consts + (bias_s,) + tail_consts_s
    any_spec = pl.BlockSpec(memory_space=pl.ANY)
    x1, h, route, routet, cnt, kvnew, vn_s = pl.pallas_call(
        _sample_layer_kernel,
        grid=(n_ssteps,),
        in_specs=[any_spec] * 5
                 + [pl.BlockSpec((LT, D_MODEL), lambda i: (i, 0)),
                    pl.BlockSpec((seqs_per_tile, cache_rows, D_KV), lambda i: (i, 0, 0)),
                    pl.BlockSpec((seqs_per_tile, cache_rows, D_KV), lambda i: (i, 0, 0))]
                 + [_const_spec(c.shape) for c in sample_consts],
        out_specs=tile_specs(lambda i: n_psteps + i)
                  + [pl.BlockSpec((LT, 2 * D_KV), lambda i: (i, 0)),
                     pl.BlockSpec((LT, D_GMLP), lambda i: (i, 0))],
        out_shape=layer_out_shapes + (jax.ShapeDtypeStruct((n_sample, 2 * D_KV), F32),
                                      jax.ShapeDtypeStruct((n_sample, D_GMLP), F32)),
        scratch_shapes=[pltpu.VMEM((LT, D_ATT), BF16), pltpu.VMEM((LT, D_GMLP), BF16)],
        input_output_aliases={0: 0, 1: 1, 2: 2, 3: 3, 4: 4},
        compiler_params=pltpu.CompilerParams(dimension_semantics=("arbitrary",),
                                             vmem_limit_bytes=VMEM_LIMIT),
        name="layer_sample",
    )(x1, h, route, routet, cnt, xs_flat, ck, cv, *sample_consts)

    blk_gran = SUB_BLOCK // GRAN
    counts =cnt[:, :, 0].astype(I32)
    seg_gran = (counts + (GRAN - 1)) // GRAN
    local_off = jnp.cumsum(seg_gran, axis=1) - seg_gran
    tot_gran = jnp.sum(seg_gran, axis=0)
    ptot_gran = (tot_gran + (blk_gran - 1)) // blk_gran * blk_gran
    pend_gran = jnp.cumsum(ptot_gran)
    gstart = pend_gran - ptot_gran
    global_off = gstart[None, :] + jnp.cumsum(seg_gran, axis=0) - seg_gran
    pad_n = ptot_gran - tot_gran
    pad_off = gstart + tot_gran
    n_rows = -(-(TOP_K * n_tok + n_tiles * N_EXPERTS * (GRAN - 1) + N_EXPERTS * (SUB_BLOCK - GRAN))
               // SUB_BLOCK) * SUB_BLOCK + ROW_BLOCK
    region_gran0 = gstart.astype(I32)
    region_subs = (ptot_gran // blk_gran).astype(I32)
    tile_gran = jnp.sum(seg_gran, axis=1).astype(I32)
    tile_big = jnp.any(seg_gran >= 4 * (FLAT_QUADS + 1), axis=1).astype(I32)
    seg_gran_f = seg_gran.reshape(-1).astype(I32)
    local_off_f = local_off.reshape(-1).astype(I32)
    global_off_f = global_off.reshape(-1).astype(I32)

    xs = pl.pallas_call(
        _dispatch_kernel,
        grid_spec=pltpu.PrefetchScalarGridSpec(
            num_scalar_prefetch=7,
            grid=(n_tiles,),
            in_specs=[pl.BlockSpec((1, ROUTE_ROWS, T), lambda i, *_: (i, 0, 0)),
                      pl.BlockSpec((T, D_MODEL), lambda i, *_: (i, 0))],
            out_specs=pl.BlockSpec(memory_space=pl.ANY),
            scratch_shapes=[pltpu.VMEM((2, LOCAL_ROWS // GRAN, GRAN, SLOT_WORDS), U32),
                            pltpu.VMEM((SUB_BLOCK // GRAN, GRAN, SLOT_WORDS), U32),
                            pltpu.SemaphoreType.DMA((3,))]),
        out_shape=jax.ShapeDtypeStruct((n_rows // GRAN, GRAN, SLOT_WORDS), U32),
        compiler_params=pltpu.CompilerParams(dimension_semantics=("arbitrary",),
                                             vmem_limit_bytes=VMEM_LIMIT),
        name="moe_dispatch",
    )(seg_gran_f, local_off_f, global_off_f, tile_gran, tile_big, pad_n.astype(I32), pad_off.astype(I32),
      route, h)

    ys = pl.pallas_call(
        _expert_kernel,
        grid_spec=pltpu.PrefetchScalarGridSpec(
            num_scalar_prefetch=2,
            grid=(N_EXPERTS,),
            in_specs=[pl.BlockSpec(memory_space=pl.ANY),
                      pl.BlockSpec((1, D_MODEL, 2 * D_FF), lambda e, *_: (e, 0, 0)),
                      pl.BlockSpec((1, 1, 2 * D_FF), lambda e, *_: (e, 0, 0)),
                      pl.BlockSpec((1, D_FF, D_MODEL), lambda e, *_: (e, 0, 0)),
                      pl.BlockSpec((1, 1, D_MODEL), lambda e, *_: (e, 0, 0))],
            out_specs=pl.BlockSpec(memory_space=pl.ANY),
            scratch_shapes=[pltpu.VMEM((D_MODEL, 2 * D_FF), BF16), pltpu.VMEM((D_FF, D_MODEL), BF16),
                            pltpu.VMEM((2, ROW_BLOCK // GRAN, GRAN, SLOT_WORDS), U32),
                            pltpu.VMEM((2, ROW_BLOCK // GRAN, GRAN, SLOT_WORDS), U32),
                            pltpu.SemaphoreType.DMA((2,)), pltpu.SemaphoreType.DMA((2,))]),
        out_shape=jax.ShapeDtypeStruct((n_rows // GRAN, GRAN, SLOT_WORDS), U32),
        compiler_params=pltpu.CompilerParams(dimension_semantics=("arbitrary",),
                                             vmem_limit_bytes=VMEM_LIMIT),
        name="moe_experts",
    )(region_gran0, region_subs, xs, w_exp_in[0], b_exp_in[0].reshape(N_EXPERTS, 1, 2 * D_FF),
      w_exp_out[0], b_exp_out[0].reshape(N_EXPERTS, 1, D_MODEL))

    y_p, y_s = pl.pallas_call(
        functools.partial(_combine_kernel, n_ptiles),
        grid_spec=pltpu.PrefetchScalarGridSpec(
            num_scalar_prefetch=5,
            grid=(n_tiles,),
            in_specs=[pl.BlockSpec((T, LANES), lambda i, *_: (i, 0)),
                      pl.BlockSpec((T, D_MODEL), lambda i, *_: (i, 0)),
                      pl.BlockSpec((1, D_MODEL), lambda i, *_: (0, 0)),
                      pl.BlockSpec(memory_space=pl.ANY)],
            out_specs=[pl.BlockSpec((T, D_MODEL), lambda i, *_: (jnp.minimum(i, n_ptiles - 1), 0)),
                       pl.BlockSpec((T, D_MODEL), lambda i, *_: (jnp.maximum(i - n_ptiles, 0), 0))],
            scratch_shapes=[pltpu.VMEM((2, LOCAL_ROWS // GRAN, GRAN, SLOT_WORDS), U32), pltpu.SemaphoreType.DMA((2,))]),
        out_shape=(jax.ShapeDtypeStruct((n_prompt, D_MODEL), F32),
                   jax.ShapeDtypeStruct((n_sample, D_MODEL), F32)),
        compiler_params=pltpu.CompilerParams(dimension_semantics=("arbitrary",),
                                             vmem_limit_bytes=VMEM_LIMIT),
        name="moe_combine",
    )(seg_gran_f, local_off_f, global_off_f, tile_gran, tile_big, routet, x1, gf, ys)

    y_prompt = y_p.reshape(batch, seq, D_MODEL)
    y_sample = y_s.reshape(dec_batch, dec_seq, D_MODEL)
    new_win_k_prompt = kvwin[:, :, :D_KV].reshape(1, batch, WINDOW, N_KV, HEAD_DIM)
    new_win_v_prompt = kvwin[:, :, D_KV:].reshape(1, batch, WINDOW, N_KV, HEAD_DIM)
    new_win_k_sample = kvnew[:, :D_KV].reshape(1, dec_batch, dec_seq, N_KV, HEAD_DIM)
    new_win_v_sample = kvnew[:, D_KV:].reshape(1, dec_batch, dec_seq, N_KV, HEAD_DIM)
    new_sgu_v_sample = vn_s.reshape(1, dec_batch, dec_seq, D_GMLP)
    return (y_prompt, y_sample, new_win_k_prompt, new_win_v_prompt, new_win_k_sample,
            new_win_v_sample, new_sgu_v_sample)
```

```python
import functools
import math

import numpy as np
import jax
import jax.numpy as jnp
from jax import lax
from jax.experimental import pallas as pl
from jax.experimental.pallas import tpu as pltpu

F32 = jnp.float32
BF16 = jnp.bfloat16
I32 = jnp.int32
U32 = jnp.uint32

D_MODEL = 1024
HEAD_DIM = 64
N_HEADS = 16
N_KV = 2
REP = N_HEADS // N_KV
CHUNK = 64
WINDOW = 128
BAND = WINDOW + CHUNK
KEY_PAD = 256
D_ATT = N_HEADS * HEAD_DIM
D_KV = N_KV * HEAD_DIM
NUM_BUCKETS = 32
MAX_DISTANCE = 128
GMLP_CHUNK = 128
D_GMLP = 1024
N_GROUPS = 4
GROUP_W = D_GMLP // N_GROUPS
N_EXPERTS = 32
TOP_K = 4
D_FF = 1024
SWIGLU_LIMIT = 7.0
SWIGLU_ALPHA = 1.702
NORM_EPS = 1e-5
D_IN = D_ATT + 2 * D_KV + 2 * D_GMLP + 2 * D_MODEL
COL_KV = D_ATT
COL_U = COL_KV + 2 * D_KV
COL_VG = COL_U + D_GMLP
COL_GA = COL_VG + D_GMLP
COL_GB = COL_GA + D_MODEL
SQRT_HALF = float(np.sqrt(0.5))

LANES = 128
WORD_SUBLANES = 8
VMEM_LIMIT = 56 * 1024 * 1024

TOK_TILE = 256
LAYER_TILE = 512
GRAN = WORD_SUBLANES
SLOT_WORDS = D_MODEL // 2
FLAT_QUADS = 3
LOCAL_ROWS = TOP_K * TOK_TILE + N_EXPERTS * GRAN
ROW_BLOCK = 512
SUB_BLOCK = 128
PAD_BITS = (SUB_BLOCK // GRAN - 1).bit_length()
TILE_BITS = (LOCAL_ROWS // GRAN).bit_length()
BLOCK_DMA_PRIORITY = 1
ROUTE_ROWS = 16


def _const_spec(shape):
    nd = len(shape)
    return pl.BlockSpec(shape, lambda *_: (0,) * nd, pipeline_mode=pl.Buffered(1))


def _rms_norm(x, g):
    ms = jnp.mean(x * x, axis=-1, keepdims=True)
    return x * lax.rsqrt(ms + NORM_EPS) * g


def _gelu(x):
    return 0.5 * x * (1.0 + lax.erf(x * SQRT_HALF))


def _attend(q, k, v, bias, valid):
    logits = lax.dot_general(q, k, (((1,), (1,)), ((), ())), preferred_element_type=F32) + bias
    if valid is not None:
        logits = jnp.where(valid, logits, -jnp.inf)
    m = jnp.max(logits, axis=-1, keepdims=True)
    p = jnp.exp(logits - m)
    den = jnp.sum(p, axis=-1, keepdims=True)
    w = (p * (1.0 / den)).astype(BF16)
    return jnp.dot(w, v, preferred_element_type=F32)


def _attention_rows(zq, row0, n_rows, k_ext, v_ext, bias_ref, valid, oatt_ref, heads_per_unit):
    lane = lax.broadcasted_iota(I32, (1, LANES), 1)
    for g in range(N_KV):
        in_group = (lane >= g * HEAD_DIM) & (lane < (g + 1) * HEAD_DIM)
        qmask = jnp.where(in_group, HEAD_DIM ** -0.5, 0.0).astype(BF16)
        for part in range(REP // heads_per_unit):
            tiles = range(part * heads_per_unit, (part + 1) * heads_per_unit)
            q = jnp.concatenate([zq[row0:row0 + n_rows, t * LANES:(t + 1) * LANES] * qmask for t in tiles],
                                axis=0)
            b0 = part * heads_per_unit * n_rows
            o = _attend(q, k_ext, v_ext, bias_ref[g, b0:b0 + heads_per_unit * n_rows, :], valid)
            for u, t in enumerate(tiles):
                c0 = t * LANES + g * HEAD_DIM
                oatt_ref[row0:row0 + n_rows, c0:c0 + HEAD_DIM] = (
                    o[u * n_rows:(u + 1) * n_rows, g * HEAD_DIM:(g + 1) * HEAD_DIM].astype(BF16))


def _layer_tail(x, xn_b, oatt_ref, osgu_ref, w_in_ref, sguw_ref, sgub_ref, lng_ref, lnb_ref,
                woa_ref, wos_ref, wout_ref, g2_ref, wr_ref, br_ref, tri_ref, low_ref,
                x1_ref, h_ref, pdisp_ref, pcomb_ref, cnt_ref, vn_ref):
    T = x.shape[0]
    u = _gelu(jnp.dot(xn_b, w_in_ref[:, COL_U:COL_VG], preferred_element_type=F32))
    vg = _gelu(jnp.dot(xn_b, w_in_ref[:, COL_VG:COL_GA], preferred_element_type=F32))
    mu = jnp.mean(vg, axis=-1, keepdims=True)
    var = jnp.mean(jnp.square(vg - mu), axis=-1, keepdims=True)
    vn = (vg - mu) * lax.rsqrt(var + NORM_EPS) * lng_ref[...] + lnb_ref[...]
    if vn_ref is not None:
        vn_ref[...] = vn
    vn_b = vn.astype(BF16)
    for j in range(T // GMLP_CHUNK):
        rows = slice(j * GMLP_CHUNK, (j + 1) * GMLP_CHUNK)
        for g in range(N_GROUPS):
            cols = slice(g * GROUP_W, (g + 1) * GROUP_W)
            mixed = jnp.dot(sguw_ref[g], vn_b[rows, cols], preferred_element_type=F32) + sgub_ref[g]
            osgu_ref[rows, cols] = (u[rows, cols] * mixed).astype(BF16)

    ga = jax.nn.sigmoid(jnp.dot(xn_b, w_in_ref[:, COL_GA:COL_GB], preferred_element_type=F32))
    merged = ga * jnp.dot(oatt_ref[...], woa_ref[...], preferred_element_type=F32)
    gb = jax.nn.sigmoid(jnp.dot(xn_b, w_in_ref[:, COL_GB:D_IN], preferred_element_type=F32))
    merged = merged + gb * jnp.dot(osgu_ref[...], wos_ref[...], preferred_element_type=F32)
    x1 = x + jnp.dot(merged.astype(BF16), wout_ref[...], preferred_element_type=F32)
    x1_ref[...] = x1

    h_b = _rms_norm(x1, g2_ref[...]).astype(BF16)
    h_ref[...] = h_b
    logits = jnp.dot(h_b, wr_ref[...], preferred_element_type=F32)
    lt = jnp.transpose(logits)[:N_EXPERTS, :] + br_ref[...]
    for s in range(T // TOK_TILE):
        rec, cnt_b = _route_sort_tile(lt[:, s * TOK_TILE:(s + 1) * TOK_TILE], tri_ref, low_ref)
        cnt_ref[s] = cnt_b
        pos = rec[TOP_K:2 * TOP_K, :].astype(I32)
        r_iota = lax.broadcasted_iota(I32, (LOCAL_ROWS, TOK_TILE), 0)
        p = jnp.zeros((LOCAL_ROWS, TOK_TILE), F32)
        for k in range(TOP_K):
            p = jnp.where(r_iota == pos[k:k + 1, :], 1.0, p)
        pdisp_ref[s] = p.astype(BF16)
        rec_pad = jnp.concatenate([rec, jnp.zeros((LANES - ROUTE_ROWS, TOK_TILE), F32)], axis=0)
        rt = jnp.transpose(rec_pad)
        l_iota = lax.broadcasted_iota(I32, (TOK_TILE, LOCAL_ROWS), 1)
        pw = jnp.zeros((TOK_TILE, LOCAL_ROWS), F32)
        for k in range(TOP_K):
            pos_k = rt[:, TOP_K + k:TOP_K + k + 1].astype(I32)
            pw = jnp.where(l_iota == pos_k, rt[:, 2 * TOP_K + k:2 * TOP_K + k + 1], pw)
        pcomb_ref[s * TOK_TILE:(s + 1) * TOK_TILE, :] = pw.astype(BF16)


def _route_sort_tile(lt, tri_ref, low_ref):
    T = lt.shape[1]
    e_iota = lax.broadcasted_iota(I32, (N_EXPERTS, T), 0).astype(F32)
    cur = lt
    vals, idxs = [], []
    for _ in range(TOP_K):
        m = jnp.max(cur, axis=0, keepdims=True)
        ik = jnp.min(jnp.where(cur == m, e_iota, float(N_EXPERTS)), axis=0, keepdims=True)
        vals.append(m)
        idxs.append(ik)
        cur = jnp.where(e_iota == ik, -jnp.inf, cur)
    exps = [jnp.exp(v - vals[0]) for v in vals]
    den = exps[0] + exps[1] + exps[2] + exps[3]
    gates = [e / den for e in exps]

    onehot = jnp.zeros((N_EXPERTS, T), F32)
    for ik in idxs:
        onehot = onehot + jnp.where(e_iota == ik, 1.0, 0.0)
    rank = jnp.dot(onehot.astype(BF16), tri_ref[...], preferred_element_type=F32)
    cnt = jnp.sum(onehot, axis=1, keepdims=True)
    cnt_b = jnp.broadcast_to(cnt, (N_EXPERTS, LANES))
    gran = jnp.floor((cnt_b + (GRAN - 1)) * (1.0 / GRAN))
    off = jnp.dot(low_ref[...], gran.astype(BF16), preferred_element_type=F32) * GRAN
    base = off[:, 0:1] + rank
    poss = [jnp.sum(jnp.where(e_iota == ik, base, 0.0), axis=0, keepdims=True) for ik in idxs]

    rec = jnp.concatenate(idxs + poss + gates
                          + [jnp.zeros((ROUTE_ROWS - 3 * TOP_K, T), F32)], axis=0)
    return rec, cnt_b


def _prompt_layer_kernel(x_ref, g1_ref, wq_ref, w_in_ref, bias_ref, sguw_ref, sgub_ref, lng_ref,
                         lnb_ref, woa_ref, wos_ref, wout_ref, g2_ref, wr_ref, br_ref, tri_ref, low_ref,
                         x1_ref, h_ref, pdisp_ref, pcomb_ref, cnt_ref, kvwin_ref,
                         kvx_ref, oatt_ref, osgu_ref):
    T = LAYER_TILE
    j = pl.program_id(1)
    x = x_ref[0]
    xn_b = _rms_norm(x, g1_ref[...]).astype(BF16)
    zq = jnp.dot(xn_b, wq_ref[...], preferred_element_type=F32).astype(BF16)
    zkv = jnp.dot(xn_b, w_in_ref[:, COL_KV:COL_U], preferred_element_type=F32)

    @pl.when(j == pl.num_programs(1) - 1)
    def _():
        kvwin_ref[0] = zkv[T - WINDOW:, :]

    @pl.when(j == 0)
    def _():
        kvx_ref[0:WINDOW, :] = jnp.zeros((WINDOW, 2 * D_KV), BF16)

    @pl.when(j > 0)
    def _():
        kvx_ref[0:WINDOW, :] = kvx_ref[T:T + WINDOW, :]

    kvx_ref[WINDOW:, :] = zkv.astype(BF16)

    col = lax.broadcasted_iota(I32, (1, KEY_PAD), 1)
    key_pad = jnp.zeros((KEY_PAD - BAND, 2 * D_KV), BF16)
    for c in range(T // CHUNK):
        r0 = c * CHUNK
        kvb = jnp.concatenate([kvx_ref[r0:r0 + BAND, :], key_pad], axis=0)
        valid = ((col + (j * T + r0 - WINDOW)) >= 0) | (col >= BAND) if r0 < WINDOW else None
        _attention_rows(zq, r0, CHUNK, kvb[:, :D_KV], kvb[:, D_KV:], bias_ref, valid, oatt_ref,
                        heads_per_unit=4)

    _layer_tail(x, xn_b, oatt_ref, osgu_ref, w_in_ref, sguw_ref, sgub_ref, lng_ref, lnb_ref,
                woa_ref, wos_ref, wout_ref, g2_ref, wr_ref, br_ref, tri_ref, low_ref,
                x1_ref, h_ref, pdisp_ref, pcomb_ref, cnt_ref, None)


def _sample_layer_kernel(x1_in, h_in, pdisp_in, pcomb_in, cnt_in,
                         x_ref, ck_ref, cv_ref, g1_ref, wq_ref, w_in_ref, bias_ref, sguw_ref, sgub_ref,
                         lng_ref, lnb_ref, woa_ref, wos_ref, wout_ref, g2_ref, wr_ref, br_ref, tri_ref,
                         low_ref,
                         x1_ref, h_ref, pdisp_ref, pcomb_ref, cnt_ref, kvnew_ref, vn_ref,
                         oatt_ref, osgu_ref):
    del x1_in, h_in, pdisp_in, pcomb_in, cnt_in
    T = LAYER_TILE
    nq = x_ref.shape[0] // ck_ref.shape[0]
    x = x_ref[...]
    xn_b = _rms_norm(x, g1_ref[...]).astype(BF16)
    zq = jnp.dot(xn_b, wq_ref[...], preferred_element_type=F32).astype(BF16)
    zkv = jnp.dot(xn_b, w_in_ref[:, COL_KV:COL_U], preferred_element_type=F32)
    kvnew_ref[...] = zkv
    zkv_b = zkv.astype(BF16)
    n_cache = ck_ref.shape[1]
    key_pad = jnp.zeros((KEY_PAD - n_cache - nq, D_KV), BF16)
    for b in range(T // nq):
        r0 = b * nq
        kk = jnp.concatenate([ck_ref[b].astype(BF16), zkv_b[r0:r0 + nq, :D_KV], key_pad], axis=0)
        vv = jnp.concatenate([cv_ref[b].astype(BF16), zkv_b[r0:r0 + nq, D_KV:], key_pad], axis=0)
        _attention_rows(zq, r0, nq, kk, vv, bias_ref, None, oatt_ref, heads_per_unit=REP)

    _layer_tail(x, xn_b, oatt_ref, osgu_ref, w_in_ref, sguw_ref, sgub_ref, lng_ref, lnb_ref,
                woa_ref, wos_ref, wout_ref, g2_ref, wr_ref, br_ref, tri_ref, low_ref,
                x1_ref, h_ref, pdisp_ref, pcomb_ref, cnt_ref, vn_ref)


def _pack_slot_rows(x):
    lo = lax.bitcast_convert_type(x[:, :SLOT_WORDS].astype(BF16).astype(F32), U32)
    hi = lax.bitcast_convert_type(x[:, SLOT_WORDS:].astype(BF16).astype(F32), U32)
    return (lo >> 16) | hi


def _unpack_slot_rows(w):
    lo = lax.bitcast_convert_type(w << 16, F32)
    hi = lax.bitcast_convert_type(w & jnp.uint32(0xFFFF0000), F32)
    return jnp.concatenate([lo, hi], axis=1).astype(BF16)


def _as_granules(x):
    return x.reshape(x.shape[0] // GRAN, GRAN, SLOT_WORDS)


def _as_rows(x):
    return x.reshape(x.shape[0] * GRAN, SLOT_WORDS)


def _segment_copies(n_gran, src_gran, dst_gran, bits, make_copy, act):
    for b in range(min(bits, 2)):
        @pl.when(((n_gran >> b) & 1) == 1)
        def _(b=b):
            done = n_gran & ((1 << b) - 1)
            act(make_copy(src_gran + done, dst_gran + done, 1 << b))

    if bits > 2:
        def quad(q, carry):
            done = (n_gran & 3) + 4 * q
            act(make_copy(src_gran + done, dst_gran + done, 4))
            return carry
        lax.fori_loop(0, n_gran >> 2, quad, 0)


def _start_tile_copies(tile, live, cnt_ref, loff_ref, goff_ref, big_ref, make_copy):
    for e in range(N_EXPERTS):
        t = tile * N_EXPERTS + e
        n = jnp.where(live, cnt_ref[t], 0)
        src, dst = loff_ref[t], goff_ref[t]
        for b in range(2):
            @pl.when(((n >> b) & 1) == 1)
            def _(b=b, n=n, src=src, dst=dst):
                done = n & ((1 << b) - 1)
                make_copy(src + done, dst + done, 1 << b).start()
        for q in range(FLAT_QUADS):
            @pl.when((n >> 2) > q)
            def _(q=q, n=n, src=src, dst=dst):
                done = (n & 3) + 4 * q
                make_copy(src + done, dst + done, 4).start()

    @pl.when(live & (big_ref[tile] > 0))
    def _():
        def rest(e, carry):
            t = tile * N_EXPERTS + e
            n, src, dst = cnt_ref[t], loff_ref[t], goff_ref[t]

            def quad(q, c):
                done = (n & 3) + 4 * q
                make_copy(src + done, dst + done, 4).start()
                return c
            lax.fori_loop(FLAT_QUADS, n >> 2, quad, 0)
            return carry
        lax.fori_loop(0, N_EXPERTS, rest, 0)


def _wait_granules(n_gran, bits, make_copy):
    for b in range(bits):
        @pl.when(((n_gran >> b) & 1) == 1)
        def _(b=b):
            make_copy(1 << b).wait()


def _dispatch_kernel(cnt_ref, loff_ref, goff_ref, tot_ref, big_ref, padn_ref, padoff_ref,
                     pdisp_ref, h_ref, xs_hbm, stage_ref, zero_ref, sem):
    i = pl.program_id(0)
    last = pl.num_programs(0) - 1
    slot = lax.rem(i, 2)
    T = TOK_TILE

    def seg_copy(tile_slot):
        return lambda s, d, n: pltpu.make_async_copy(
            stage_ref.at[tile_slot, pl.ds(s, n)], xs_hbm.at[pl.ds(d, n)], sem.at[tile_slot])

    def wait_tile(tile, tile_slot):
        _wait_granules(tot_ref[tile], TILE_BITS, lambda n: pltpu.make_async_copy(
            stage_ref.at[tile_slot, pl.ds(0, n)], xs_hbm.at[pl.ds(0, n)], sem.at[tile_slot]))

    @pl.when(i > 1)
    def _():
        wait_tile(i - 2, slot)

    _start_tile_copies(jnp.maximum(i - 1, 0), i > 0, cnt_ref, loff_ref, goff_ref, big_ref, seg_copy(1 - slot))

    stage_ref[slot] = _as_granules(
        _pack_slot_rows(jnp.dot(pdisp_ref[0], h_ref[...], preferred_element_type=F32)))

    @pl.when(i == last)
    def _():
        _start_tile_copies(i, i >= 0, cnt_ref, loff_ref, goff_ref, big_ref, seg_copy(slot))

        @pl.when(i > 0)
        def _():
            wait_tile(i - 1, 1 - slot)

        wait_tile(i, slot)
        zero_ref[...] = jnp.zeros(zero_ref.shape, U32)

        def pad_copy(s, d, n):
            return pltpu.make_async_copy(zero_ref.at[pl.ds(s, n)], xs_hbm.at[pl.ds(d, n)], sem.at[2])

        def for_pads(act):
            def body(e, carry):
                _segment_copies(padn_ref[e], 0, padoff_ref[e], PAD_BITS, pad_copy, act)
                return carry
            lax.fori_loop(0, N_EXPERTS, body, 0)

        for_pads(lambda cp: cp.start())
        for_pads(lambda cp: cp.wait())


def _expert_kernel(gran0_ref, nsub_ref, xs_hbm, w1_ref, b1_ref, w2_ref, b2_ref, ys_hbm,
                   w1b_ref, w2b_ref, xbuf_ref, ybuf_ref, sem_in, sem_out):
    e = pl.program_id(0)
    subs = ROW_BLOCK // SUB_BLOCK
    sub_gran = SUB_BLOCK // GRAN
    blk_gran = ROW_BLOCK // GRAN
    gran0 = gran0_ref[e]
    nsub = nsub_ref[e]
    nb_full = nsub // subs
    rem = nsub - nb_full * subs
    nb = nb_full + jnp.where(rem > 0, 1, 0)
    last_slot = lax.rem(nb + 1, 2)

    def in_copy(k, slot):
        return pltpu.make_async_copy(xs_hbm.at[pl.ds(gran0 + k * blk_gran, blk_gran)], xbuf_ref.at[slot],
                                     sem_in.at[slot])

    def out_copy(k, slot, m_sub):
        n = m_sub * sub_gran
        return pltpu.make_async_copy(ybuf_ref.at[slot, pl.ds(0, n)], ys_hbm.at[pl.ds(gran0 + k * blk_gran, n)],
                                     sem_out.at[slot])

    @pl.when(nb > 0)
    def _():
        in_copy(0, 0).start(priority=BLOCK_DMA_PRIORITY)

    w1b_ref[...] = w1_ref[0].astype(BF16)
    w2b_ref[...] = w2_ref[0].astype(BF16)

    def process(k, m_sub):
        n = m_sub * sub_gran
        slot = lax.rem(k, 2)

        @pl.when(k + 1 < nb)
        def _():
            in_copy(k + 1, 1 - slot).start(priority=BLOCK_DMA_PRIORITY)

        in_copy(k, slot).wait()

        @pl.when(k >= 2)
        def _():
            out_copy(k - 2, slot, subs).wait()

        x = _unpack_slot_rows(_as_rows(xbuf_ref[slot, 0:n]))
        h1 = jnp.dot(x, w1b_ref[...], preferred_element_type=F32) + b1_ref[0]
        gate = jnp.minimum(h1[:, :D_FF], SWIGLU_LIMIT)
        up = jnp.clip(h1[:, D_FF:], -SWIGLU_LIMIT, SWIGLU_LIMIT)
        act = gate * jax.nn.sigmoid(SWIGLU_ALPHA * gate) * (up + 1.0)
        y = jnp.dot(act.astype(BF16), w2b_ref[...], preferred_element_type=F32) + b2_ref[0]
        ybuf_ref[slot, 0:n] = _as_granules(_pack_slot_rows(y))
        out_copy(k, slot, m_sub).start(priority=BLOCK_DMA_PRIORITY)

    def full_block(k, carry):
        process(k, subs)
        return carry

    lax.fori_loop(0, nb_full, full_block, 0)

    for m_sub in range(1, subs):
        @pl.when(rem == m_sub)
        def _(m_sub=m_sub):
            process(nb_full, m_sub)

    @pl.when(nb >= 2)
    def _():
        out_copy(nb - 2, 1 - last_slot, subs).wait()

    @pl.when((nb >= 1) & (rem == 0))
    def _():
        out_copy(nb - 1, last_slot, subs).wait()

    for m_sub in range(1, subs):
        @pl.when(rem == m_sub)
        def _(m_sub=m_sub):
            out_copy(nb - 1, last_slot, m_sub).wait()


def _combine_kernel(n_prompt_tiles, cnt_ref, loff_ref, goff_ref, tot_ref, big_ref,
                    pcomb_ref, x1_ref, gf_ref, ys_hbm, yp_ref, ysm_ref, stage_ref, sem):
    i = pl.program_id(0)
    n_steps = pl.num_programs(0)
    slot = lax.rem(i, 2)
    T = TOK_TILE

    def seg_copy(tile_slot):
        return lambda s, d, n: pltpu.make_async_copy(
            ys_hbm.at[pl.ds(d, n)], stage_ref.at[tile_slot, pl.ds(s, n)], sem.at[tile_slot])

    @pl.when(i == 0)
    def _():
        stage_ref[...] = jnp.zeros(stage_ref.shape, U32)
        _start_tile_copies(i, i >= 0, cnt_ref, loff_ref, goff_ref, big_ref, seg_copy(slot))

    _start_tile_copies(jnp.minimum(i + 1, n_steps - 1), i + 1 < n_steps, cnt_ref, loff_ref, goff_ref, big_ref,
                       seg_copy(1 - slot))

    _wait_granules(tot_ref[i], TILE_BITS, lambda n: pltpu.make_async_copy(
        ys_hbm.at[pl.ds(0, n)], stage_ref.at[slot, pl.ds(0, n)], sem.at[slot]))
    moe = jnp.dot(pcomb_ref[...], _unpack_slot_rows(_as_rows(stage_ref[slot])), preferred_element_type=F32)
    y = _rms_norm(x1_ref[...] + moe, gf_ref[...])

    @pl.when(i < n_prompt_tiles)
    def _():
        yp_ref[...] = y

    @pl.when(i >= n_prompt_tiles)
    def _():
        ysm_ref[...] = y


def _t5_bucket(rel):
    half = NUM_BUCKETS // 2
    max_exact = half // 2
    ret = jnp.where(rel > 0, half, 0)
    n = jnp.abs(rel)
    nf = jnp.maximum(n, 1).astype(F32)
    large = max_exact + (jnp.log(nf / max_exact) / math.log(MAX_DISTANCE / max_exact)
                         * (half - max_exact)).astype(jnp.int32)
    large = jnp.minimum(large, half - 1)
    return ret + jnp.where(n < max_exact, n, large)


def _stacked_bias(table, sinks, q_pos, k_pos):
    nq, nk = q_pos.shape[0], k_pos.shape[0]
    bucket = _t5_bucket(k_pos[None, :] - q_pos[:, None])
    onehot = (bucket[:, :, None] == jnp.arange(NUM_BUCKETS)).astype(F32)
    bias = jnp.einsum('qkb,bh->hqk', onehot, table.astype(F32), precision=lax.Precision.HIGHEST)
    bias = bias.reshape(N_KV, REP * nq, nk)
    sink = jnp.repeat(sinks.astype(F32).reshape(N_KV, REP, 1), nq, axis=2).reshape(N_KV, REP * nq, 1)
    tail = jnp.full((N_KV, REP * nq, KEY_PAD - nk - 1), -jnp.inf, F32)
    return jnp.concatenate([bias, sink, tail], axis=-1)


def kernel(x_prompt, x_sample, cache_win_k, cache_win_v, norm1_g, w_in, attn_sinks, rel_bias_table, sgu_ln_g, sgu_ln_b, sgu_w, sgu_b, w_o_att, w_o_sgu, w_out, norm2_g, w_router, b_router, w_exp_in, b_exp_in, w_exp_out, b_exp_out, final_norm_g):
    batch, seq, _ = x_prompt.shape
    dec_batch, dec_seq, _ = x_sample.shape
    cache_rows = cache_win_k.shape[2]
    assert x_prompt.shape[2] == D_MODEL and w_in.shape == (1, D_MODEL, D_IN)
    assert seq % LAYER_TILE == 0 and LAYER_TILE % TOK_TILE == 0 and TOK_TILE % GMLP_CHUNK == 0
    assert LAYER_TILE >= WINDOW and LAYER_TILE % dec_seq == 0 and (dec_batch * dec_seq) % LAYER_TILE == 0
    assert dec_seq <= GMLP_CHUNK and GMLP_CHUNK % dec_seq == 0 and cache_rows == WINDOW
    T = TOK_TILE
    n_prompt = batch * seq
    n_sample = dec_batch * dec_seq
    n_tok = n_prompt + n_sample
    LT = LAYER_TILE
    sorts_per_step = LT // T
    tiles_per_seq = seq // LT
    n_psteps = n_prompt // LT
    n_ssteps = n_sample // LT
    n_ptiles = n_prompt // T
    n_tiles = n_tok // T
    seqs_per_tile = LT // dec_seq

    w_in_b = w_in[0].astype(BF16)
    w_q_b = w_in_b[:, :D_ATT].reshape(D_MODEL, N_KV, REP, HEAD_DIM).transpose(0, 2, 1, 3).reshape(
        D_MODEL, D_ATT)
    woa_b = w_o_att[0].reshape(N_KV, REP, HEAD_DIM, D_MODEL).transpose(1, 0, 2, 3).reshape(
        D_ATT, D_MODEL).astype(BF16)
    wos_b = w_o_sgu[0].astype(BF16)
    wout_b = w_out[0].astype(BF16)
    wr_b = jnp.pad(w_router[0], ((0, 0), (0, LANES - N_EXPERTS))).astype(BF16)
    br_col = b_router[0].astype(F32).reshape(N_EXPERTS, 1)
    g1 = norm1_g[0].reshape(1, D_MODEL)
    g2 = norm2_g[0].reshape(1, D_MODEL)
    gf = final_norm_g.reshape(1, D_MODEL)
    lng = sgu_ln_g[0].reshape(1, D_GMLP)
    lnb = sgu_ln_b[0].reshape(1, D_GMLP)
    tril = jnp.tril(jnp.ones((GMLP_CHUNK, GMLP_CHUNK), dtype=bool))
    sguw_p = jnp.where(tril[None], sgu_w[0], 0).astype(BF16)
    sgub_p = jnp.broadcast_to(sgu_b[0][:, :, None], (N_GROUPS, GMLP_CHUNK, GROUP_W)).astype(F32)
    reps = GMLP_CHUNK // dec_seq
    corner = jnp.where(tril[None, :dec_seq, :dec_seq], sgu_w[0][:, :dec_seq, :dec_seq], 0)
    sguw_s = jnp.einsum('ab,gij->gaibj', jnp.eye(reps, dtype=F32), corner).reshape(
        N_GROUPS, GMLP_CHUNK, GMLP_CHUNK).astype(BF16)
    sgub_s = jnp.broadcast_to(jnp.tile(sgu_b[0][:, :dec_seq], (1, reps))[:, :, None],
                              (N_GROUPS, GMLP_CHUNK, GROUP_W)).astype(F32)
    bias_p = _stacked_bias(rel_bias_table, attn_sinks[0], jnp.arange(CHUNK) + WINDOW, jnp.arange(BAND))
    bias_s = _stacked_bias(rel_bias_table, attn_sinks[0], cache_rows + jnp.arange(dec_seq),
                           jnp.arange(cache_rows + dec_seq))
    tri = jnp.triu(jnp.ones((T, T), F32), k=1).astype(BF16)
    low = jnp.tril(jnp.ones((N_EXPERTS, N_EXPERTS), F32), k=-1).astype(BF16)

    layer_out_shapes = (
        jax.ShapeDtypeStruct((n_tok, D_MODEL), F32),
        jax.ShapeDtypeStruct((n_tok, D_MODEL), BF16),
        jax.ShapeDtypeStruct((n_tiles, LOCAL_ROWS, T), BF16),
        jax.ShapeDtypeStruct((n_tok, LOCAL_ROWS), BF16),
        jax.ShapeDtypeStruct((n_tiles, N_EXPERTS, LANES), F32),
    )
    shared_consts = (w_q_b, w_in_b)
    tail_consts_p = (sguw_p, sgub_p, lng, lnb, woa_b, wos_b, wout_b, g2, wr_b, br_col, tri, low)
    tail_consts_s = (sguw_s, sgub_s, lng, lnb, woa_b, wos_b, wout_b, g2, wr_b, br_col, tri, low)

    def tile_specs(tile_of):
        return [
            pl.BlockSpec((LT, D_MODEL), lambda *g: (tile_of(*g), 0)),
            pl.BlockSpec((LT, D_MODEL), lambda *g: (tile_of(*g), 0)),
            pl.BlockSpec((sorts_per_step, LOCAL_ROWS, T), lambda *g: (tile_of(*g), 0, 0)),
            pl.BlockSpec((LT, LOCAL_ROWS), lambda *g: (tile_of(*g), 0)),
            pl.BlockSpec((sorts_per_step, N_EXPERTS, LANES), lambda *g: (tile_of(*g), 0, 0)),
        ]

    prompt_consts = (g1,) + shared_consts + (bias_p,) + tail_consts_p
    x1, h, pdisp, pcomb, cnt, kvwin = pl.pallas_call(
        _prompt_layer_kernel,
        grid=(batch, tiles_per_seq),
        in_specs=[pl.BlockSpec((1, LT, D_MODEL), lambda b, j: (b, j, 0))]
                 + [_const_spec(c.shape) for c in prompt_consts],
        out_specs=tile_specs(lambda b, j: b * tiles_per_seq + j)
                  + [pl.BlockSpec((1, WINDOW, 2 * D_KV), lambda b, j: (b, 0, 0))],
        out_shape=layer_out_shapes + (jax.ShapeDtypeStruct((batch, WINDOW, 2 * D_KV), F32),),
        scratch_shapes=[pltpu.VMEM((LT + WINDOW, 2 * D_KV), BF16),
                        pltpu.VMEM((LT, D_ATT), BF16),
                        pltpu.VMEM((LT, D_GMLP), BF16)],
        compiler_params=pltpu.CompilerParams(dimension_semantics=("arbitrary", "arbitrary"),
                                             vmem_limit_bytes=VMEM_LIMIT),
        name="layer_prompt",
    )(x_prompt, *prompt_consts)

    xs_flat = x_sample.reshape(n_sample, D_MODEL)
    ck = cache_win_k[0].reshape(dec_batch, cache_rows, D_KV)
    cv = cache_win_v[0].reshape(dec_batch, cache_rows, D_KV)
    sample_consts = (g1,) + shared_consts + (bias_s,) + tail_consts_s
    any_spec = pl.BlockSpec(memory_space=pl.ANY)
    x1, h, pdisp, pcomb, cnt, kvnew, vn_s = pl.pallas_call(
        _sample_layer_kernel,
        grid=(n_ssteps,),
        in_specs=[any_spec] * 5
                 + [pl.BlockSpec((LT, D_MODEL), lambda i: (i, 0)),
                    pl.BlockSpec((seqs_per_tile, cache_rows, D_KV), lambda i: (i, 0, 0)),
                    pl.BlockSpec((seqs_per_tile, cache_rows, D_KV), lambda i: (i, 0, 0))]
                 + [_const_spec(c.shape) for c in sample_consts],
        out_specs=tile_specs(lambda i: n_psteps + i)
                  + [pl.BlockSpec((LT, 2 * D_KV), lambda i: (i, 0)),
                     pl.BlockSpec((LT, D_GMLP), lambda i: (i, 0))],
        out_shape=layer_out_shapes + (jax.ShapeDtypeStruct((n_sample, 2 * D_KV), F32),
                                      jax.ShapeDtypeStruct((n_sample, D_GMLP), F32)),
        scratch_shapes=[pltpu.VMEM((LT, D_ATT), BF16), pltpu.VMEM((LT, D_GMLP), BF16)],
        input_output_aliases={0: 0, 1: 1, 2: 2, 3: 3, 4: 4},
        compiler_params=pltpu.CompilerParams(dimension_semantics=("arbitrary",),
                                             vmem_limit_bytes=VMEM_LIMIT),
        name="layer_sample",
    )(x1, h, pdisp, pcomb, cnt, xs_flat, ck, cv, *sample_consts)

    blk_gran = SUB_BLOCK // GRAN
    counts =cnt[:, :, 0].astype(I32)
    seg_gran = (counts + (GRAN - 1)) // GRAN
    local_off = jnp.cumsum(seg_gran, axis=1) - seg_gran
    tot_gran = jnp.sum(seg_gran, axis=0)
    ptot_gran = (tot_gran + (blk_gran - 1)) // blk_gran * blk_gran
    pend_gran = jnp.cumsum(ptot_gran)
    gstart = pend_gran - ptot_gran
    global_off = gstart[None, :] + jnp.cumsum(seg_gran, axis=0) - seg_gran
    pad_n = ptot_gran - tot_gran
    pad_off = gstart + tot_gran
    n_rows = -(-(TOP_K * n_tok + n_tiles * N_EXPERTS * (GRAN - 1) + N_EXPERTS * (SUB_BLOCK - GRAN))
               // SUB_BLOCK) * SUB_BLOCK + ROW_BLOCK
    region_gran0 = gstart.astype(I32)
    region_subs = (ptot_gran // blk_gran).astype(I32)
    tile_gran = jnp.sum(seg_gran, axis=1).astype(I32)
    tile_big = jnp.any(seg_gran >= 4 * (FLAT_QUADS + 1), axis=1).astype(I32)
    seg_gran_f = seg_gran.reshape(-1).astype(I32)
    local_off_f = local_off.reshape(-1).astype(I32)
    global_off_f = global_off.reshape(-1).astype(I32)

    xs = pl.pallas_call(
        _dispatch_kernel,
        grid_spec=pltpu.PrefetchScalarGridSpec(
            num_scalar_prefetch=7,
            grid=(n_tiles,),
            in_specs=[pl.BlockSpec((1, LOCAL_ROWS, T), lambda i, *_: (i, 0, 0)),
                      pl.BlockSpec((T, D_MODEL), lambda i, *_: (i, 0))],
            out_specs=pl.BlockSpec(memory_space=pl.ANY),
            scratch_shapes=[pltpu.VMEM((2, LOCAL_ROWS // GRAN, GRAN, SLOT_WORDS), U32),
                            pltpu.VMEM((SUB_BLOCK // GRAN, GRAN, SLOT_WORDS), U32),
                            pltpu.SemaphoreType.DMA((3,))]),
        out_shape=jax.ShapeDtypeStruct((n_rows // GRAN, GRAN, SLOT_WORDS), U32),
        compiler_params=pltpu.CompilerParams(dimension_semantics=("arbitrary",),
                                             vmem_limit_bytes=VMEM_LIMIT),
        name="moe_dispatch",
    )(seg_gran_f, local_off_f, global_off_f, tile_gran, tile_big, pad_n.astype(I32), pad_off.astype(I32),
      pdisp, h)

    ys = pl.pallas_call(
        _expert_kernel,
        grid_spec=pltpu.PrefetchScalarGridSpec(
            num_scalar_prefetch=2,
            grid=(N_EXPERTS,),
            in_specs=[pl.BlockSpec(memory_space=pl.ANY),
                      pl.BlockSpec((1, D_MODEL, 2 * D_FF), lambda e, *_: (e, 0, 0)),
                      pl.BlockSpec((1, 1, 2 * D_FF), lambda e, *_: (e, 0, 0)),
                      pl.BlockSpec((1, D_FF, D_MODEL), lambda e, *_: (e, 0, 0)),
                      pl.BlockSpec((1, 1, D_MODEL), lambda e, *_: (e, 0, 0))],
            out_specs=pl.BlockSpec(memory_space=pl.ANY),
            scratch_shapes=[pltpu.VMEM((D_MODEL, 2 * D_FF), BF16), pltpu.VMEM((D_FF, D_MODEL), BF16),
                            pltpu.VMEM((2, ROW_BLOCK // GRAN, GRAN, SLOT_WORDS), U32),
                            pltpu.VMEM((2, ROW_BLOCK // GRAN, GRAN, SLOT_WORDS), U32),
                            pltpu.SemaphoreType.DMA((2,)), pltpu.SemaphoreType.DMA((2,))]),
        out_shape=jax.ShapeDtypeStruct((n_rows // GRAN, GRAN, SLOT_WORDS), U32),
        compiler_params=pltpu.CompilerParams(dimension_semantics=("arbitrary",),
                                             vmem_limit_bytes=VMEM_LIMIT),
        name="moe_experts",
    )(region_gran0, region_subs, xs, w_exp_in[0], b_exp_in[0].reshape(N_EXPERTS, 1, 2 * D_FF),
      w_exp_out[0], b_exp_out[0].reshape(N_EXPERTS, 1, D_MODEL))

    y_p, y_s = pl.pallas_call(
        functools.partial(_combine_kernel, n_ptiles),
        grid_spec=pltpu.PrefetchScalarGridSpec(
            num_scalar_prefetch=5,
            grid=(n_tiles,),
            in_specs=[pl.BlockSpec((T, LOCAL_ROWS), lambda i, *_: (i, 0)),
                      pl.BlockSpec((T, D_MODEL), lambda i, *_: (i, 0)),
                      pl.BlockSpec((1, D_MODEL), lambda i, *_: (0, 0)),
                      pl.BlockSpec(memory_space=pl.ANY)],
            out_specs=[pl.BlockSpec((T, D_MODEL), lambda i, *_: (jnp.minimum(i, n_ptiles - 1), 0)),
                       pl.BlockSpec((T, D_MODEL), lambda i, *_: (jnp.maximum(i - n_ptiles, 0), 0))],
            scratch_shapes=[pltpu.VMEM((2, LOCAL_ROWS // GRAN, GRAN, SLOT_WORDS), U32), pltpu.SemaphoreType.DMA((2,))]),
        out_shape=(jax.ShapeDtypeStruct((n_prompt, D_MODEL), F32),
                   jax.ShapeDtypeStruct((n_sample, D_MODEL), F32)),
        compiler_params=pltpu.CompilerParams(dimension_semantics=("arbitrary",),
                                             vmem_limit_bytes=VMEM_LIMIT),
        name="moe_combine",
    )(seg_gran_f, local_off_f, global_off_f, tile_gran, tile_big, pcomb, x1, gf, ys)

    y_prompt = y_p.reshape(batch, seq, D_MODEL)
    y_sample = y_s.reshape(dec_batch, dec_seq, D_MODEL)
    new_win_k_prompt = kvwin[:, :, :D_KV].reshape(1, batch, WINDOW, N_KV, HEAD_DIM)
    new_win_v_prompt = kvwin[:, :, D_KV:].reshape(1, batch, WINDOW, N_KV, HEAD_DIM)
    new_win_k_sample = kvnew[:, :D_KV].reshape(1, dec_batch, dec_seq, N_KV, HEAD_DIM)
    new_win_v_sample = kvnew[:, D_KV:].reshape(1, dec_batch, dec_seq, N_KV, HEAD_DIM)
    new_sgu_v_sample = vn_s.reshape(1, dec_batch, dec_seq, D_GMLP)
    return (y_prompt, y_sample, new_win_k_prompt, new_win_v_prompt, new_win_k_sample,
            new_win_v_sample, new_sgu_v_sample)
```

```python
import functools
import math

import numpy as np
import jax
import jax.numpy as jnp
from jax import lax
from jax.experimental import pallas as pl
from jax.experimental.pallas import tpu as pltpu

F32 = jnp.float32
BF16 = jnp.bfloat16
I32 = jnp.int32
U32 = jnp.uint32

D_MODEL = 1024
HEAD_DIM = 64
N_HEADS = 16
N_KV = 2
REP = N_HEADS // N_KV
CHUNK = 64
WINDOW = 128
BAND = WINDOW + CHUNK
KEY_PAD = 256
D_ATT = N_HEADS * HEAD_DIM
D_KV = N_KV * HEAD_DIM
NUM_BUCKETS = 32
MAX_DISTANCE = 128
GMLP_CHUNK = 128
D_GMLP = 1024
N_GROUPS = 4
GROUP_W = D_GMLP // N_GROUPS
N_EXPERTS = 32
TOP_K = 4
D_FF = 1024
SWIGLU_LIMIT = 7.0
SWIGLU_ALPHA = 1.702
NORM_EPS = 1e-5
D_IN = D_ATT + 2 * D_KV + 2 * D_GMLP + 2 * D_MODEL
COL_KV = D_ATT
COL_U = COL_KV + 2 * D_KV
COL_VG = COL_U + D_GMLP
COL_GA = COL_VG + D_GMLP
COL_GB = COL_GA + D_MODEL
SQRT_HALF = float(np.sqrt(0.5))

LANES = 128
WORD_SUBLANES = 8
VMEM_LIMIT = 56 * 1024 * 1024

TOK_TILE = 256
LAYER_TILE = 512
GRAN = WORD_SUBLANES
SLOT_WORDS = D_MODEL // 2
FLAT_QUADS = 3
LOCAL_ROWS = TOP_K * TOK_TILE + N_EXPERTS * GRAN
ROW_BLOCK = 512
SUB_BLOCK = 128
PAD_BITS = (SUB_BLOCK // GRAN - 1).bit_length()
TILE_BITS = (LOCAL_ROWS // GRAN).bit_length()
DMA_QUEUES = 2
BLOCK_DMA_PRIORITY = 1
ROUTE_ROWS = 16


def _const_spec(shape):
    nd = len(shape)
    return pl.BlockSpec(shape, lambda *_: (0,) * nd, pipeline_mode=pl.Buffered(1))


def _rms_norm(x, g):
    ms = jnp.mean(x * x, axis=-1, keepdims=True)
    return x * lax.rsqrt(ms + NORM_EPS) * g


def _gelu(x):
    return 0.5 * x * (1.0 + lax.erf(x * SQRT_HALF))


def _attend(q, k, v, bias, valid):
    logits = lax.dot_general(q, k, (((1,), (1,)), ((), ())), preferred_element_type=F32) + bias
    if valid is not None:
        logits = jnp.where(valid, logits, -jnp.inf)
    m = jnp.max(logits, axis=-1, keepdims=True)
    p = jnp.exp(logits - m)
    den = jnp.sum(p, axis=-1, keepdims=True)
    w = (p * (1.0 / den)).astype(BF16)
    return jnp.dot(w, v, preferred_element_type=F32)


def _attention_rows(zq, row0, n_rows, k_ext, v_ext, bias_ref, valid, oatt_ref, heads_per_unit):
    lane = lax.broadcasted_iota(I32, (1, LANES), 1)
    for g in range(N_KV):
        in_group = (lane >= g * HEAD_DIM) & (lane < (g + 1) * HEAD_DIM)
        qmask = jnp.where(in_group, HEAD_DIM ** -0.5, 0.0).astype(BF16)
        for part in range(REP // heads_per_unit):
            tiles = range(part * heads_per_unit, (part + 1) * heads_per_unit)
            q = jnp.concatenate([zq[row0:row0 + n_rows, t * LANES:(t + 1) * LANES] * qmask for t in tiles],
                                axis=0)
            b0 = part * heads_per_unit * n_rows
            o = _attend(q, k_ext, v_ext, bias_ref[g, b0:b0 + heads_per_unit * n_rows, :], valid)
            for u, t in enumerate(tiles):
                c0 = t * LANES + g * HEAD_DIM
                oatt_ref[row0:row0 + n_rows, c0:c0 + HEAD_DIM] = (
                    o[u * n_rows:(u + 1) * n_rows, g * HEAD_DIM:(g + 1) * HEAD_DIM].astype(BF16))


def _layer_tail(x, xn_b, oatt_ref, osgu_ref, w_in_ref, sguw_ref, sgub_ref, lng_ref, lnb_ref,
                woa_ref, wos_ref, wout_ref, g2_ref, wr_ref, br_ref, tri_ref, low_ref,
                x1_ref, h_ref, route_ref, routet_ref, cnt_ref, vn_ref):
    T = x.shape[0]
    u = _gelu(jnp.dot(xn_b, w_in_ref[:, COL_U:COL_VG], preferred_element_type=F32))
    vg = _gelu(jnp.dot(xn_b, w_in_ref[:, COL_VG:COL_GA], preferred_element_type=F32))
    mu = jnp.mean(vg, axis=-1, keepdims=True)
    var = jnp.mean(jnp.square(vg - mu), axis=-1, keepdims=True)
    vn = (vg - mu) * lax.rsqrt(var + NORM_EPS) * lng_ref[...] + lnb_ref[...]
    if vn_ref is not None:
        vn_ref[...] = vn
    vn_b = vn.astype(BF16)
    for j in range(T // GMLP_CHUNK):
        rows = slice(j * GMLP_CHUNK, (j + 1) * GMLP_CHUNK)
        for g in range(N_GROUPS):
            cols = slice(g * GROUP_W, (g + 1) * GROUP_W)
            mixed = jnp.dot(sguw_ref[g], vn_b[rows, cols], preferred_element_type=F32) + sgub_ref[g]
            osgu_ref[rows, cols] = (u[rows, cols] * mixed).astype(BF16)

    ga = jax.nn.sigmoid(jnp.dot(xn_b, w_in_ref[:, COL_GA:COL_GB], preferred_element_type=F32))
    merged = ga * jnp.dot(oatt_ref[...], woa_ref[...], preferred_element_type=F32)
    gb = jax.nn.sigmoid(jnp.dot(xn_b, w_in_ref[:, COL_GB:D_IN], preferred_element_type=F32))
    merged = merged + gb * jnp.dot(osgu_ref[...], wos_ref[...], preferred_element_type=F32)
    x1 = x + jnp.dot(merged.astype(BF16), wout_ref[...], preferred_element_type=F32)
    x1_ref[...] = x1

    h_b = _rms_norm(x1, g2_ref[...]).astype(BF16)
    h_ref[...] = h_b
    logits = jnp.dot(h_b, wr_ref[...], preferred_element_type=F32)
    lt = jnp.transpose(logits)[:N_EXPERTS, :] + br_ref[...]
    for s in range(T // TOK_TILE):
        rec, cnt_b = _route_sort_tile(lt[:, s * TOK_TILE:(s + 1) * TOK_TILE], tri_ref, low_ref)
        route_ref[s] = rec
        rec_pad = jnp.concatenate([rec, jnp.zeros((LANES - ROUTE_ROWS, TOK_TILE), F32)], axis=0)
        routet_ref[s * TOK_TILE:(s + 1) * TOK_TILE, :] = jnp.transpose(rec_pad)
        cnt_ref[s] = cnt_b


def _route_sort_tile(lt, tri_ref, low_ref):
    T = lt.shape[1]
    e_iota = lax.broadcasted_iota(I32, (N_EXPERTS, T), 0).astype(F32)
    cur = lt
    vals, idxs = [], []
    for _ in range(TOP_K):
        m = jnp.max(cur, axis=0, keepdims=True)
        ik = jnp.min(jnp.where(cur == m, e_iota, float(N_EXPERTS)), axis=0, keepdims=True)
        vals.append(m)
        idxs.append(ik)
        cur = jnp.where(e_iota == ik, -jnp.inf, cur)
    exps = [jnp.exp(v - vals[0]) for v in vals]
    den = exps[0] + exps[1] + exps[2] + exps[3]
    gates = [e / den for e in exps]

    onehot = jnp.zeros((N_EXPERTS, T), F32)
    for ik in idxs:
        onehot = onehot + jnp.where(e_iota == ik, 1.0, 0.0)
    rank = jnp.dot(onehot.astype(BF16), tri_ref[...], preferred_element_type=F32)
    cnt = jnp.sum(onehot, axis=1, keepdims=True)
    cnt_b = jnp.broadcast_to(cnt, (N_EXPERTS, LANES))
    gran = jnp.floor((cnt_b + (GRAN - 1)) * (1.0 / GRAN))
    off = jnp.dot(low_ref[...], gran.astype(BF16), preferred_element_type=F32) * GRAN
    base = off[:, 0:1] + rank
    poss = [jnp.sum(jnp.where(e_iota == ik, base, 0.0), axis=0, keepdims=True) for ik in idxs]

    rec = jnp.concatenate(idxs + poss + gates
                          + [jnp.zeros((ROUTE_ROWS - 3 * TOP_K, T), F32)], axis=0)
    return rec, cnt_b


def _prompt_layer_kernel(x_ref, g1_ref, wq_ref, w_in_ref, bias_ref, sguw_ref, sgub_ref, lng_ref,
                         lnb_ref, woa_ref, wos_ref, wout_ref, g2_ref, wr_ref, br_ref, tri_ref, low_ref,
                         x1_ref, h_ref, route_ref, routet_ref, cnt_ref, kvwin_ref,
                         kvx_ref, oatt_ref, osgu_ref):
    T = LAYER_TILE
    j = pl.program_id(1)
    x = x_ref[0]
    xn_b = _rms_norm(x, g1_ref[...]).astype(BF16)
    zq = jnp.dot(xn_b, wq_ref[...], preferred_element_type=F32).astype(BF16)
    zkv = jnp.dot(xn_b, w_in_ref[:, COL_KV:COL_U], preferred_element_type=F32)

    @pl.when(j == pl.num_programs(1) - 1)
    def _():
        kvwin_ref[0] = zkv[T - WINDOW:, :]

    @pl.when(j == 0)
    def _():
        kvx_ref[0:WINDOW, :] = jnp.zeros((WINDOW, 2 * D_KV), BF16)

    @pl.when(j > 0)
    def _():
        kvx_ref[0:WINDOW, :] = kvx_ref[T:T + WINDOW, :]

    kvx_ref[WINDOW:, :] = zkv.astype(BF16)

    col = lax.broadcasted_iota(I32, (1, KEY_PAD), 1)
    key_pad = jnp.zeros((KEY_PAD - BAND, 2 * D_KV), BF16)
    for c in range(T // CHUNK):
        r0 = c * CHUNK
        kvb = jnp.concatenate([kvx_ref[r0:r0 + BAND, :], key_pad], axis=0)
        valid = ((col + (j * T + r0 - WINDOW)) >= 0) | (col >= BAND) if r0 < WINDOW else None
        _attention_rows(zq, r0, CHUNK, kvb[:, :D_KV], kvb[:, D_KV:], bias_ref, valid, oatt_ref,
                        heads_per_unit=4)

    _layer_tail(x, xn_b, oatt_ref, osgu_ref, w_in_ref, sguw_ref, sgub_ref, lng_ref, lnb_ref,
                woa_ref, wos_ref, wout_ref, g2_ref, wr_ref, br_ref, tri_ref, low_ref,
                x1_ref, h_ref, route_ref, routet_ref, cnt_ref, None)


def _sample_layer_kernel(x1_in, h_in, route_in, routet_in, cnt_in,
                         x_ref, ck_ref, cv_ref, g1_ref, wq_ref, w_in_ref, bias_ref, sguw_ref, sgub_ref,
                         lng_ref, lnb_ref, woa_ref, wos_ref, wout_ref, g2_ref, wr_ref, br_ref, tri_ref,
                         low_ref,
                         x1_ref, h_ref, route_ref, routet_ref, cnt_ref, kvnew_ref, vn_ref,
                         oatt_ref, osgu_ref):
    del x1_in, h_in, route_in, routet_in, cnt_in
    T = LAYER_TILE
    nq = x_ref.shape[0] // ck_ref.shape[0]
    x = x_ref[...]
    xn_b = _rms_norm(x, g1_ref[...]).astype(BF16)
    zq = jnp.dot(xn_b, wq_ref[...], preferred_element_type=F32).astype(BF16)
    zkv = jnp.dot(xn_b, w_in_ref[:, COL_KV:COL_U], preferred_element_type=F32)
    kvnew_ref[...] = zkv
    zkv_b = zkv.astype(BF16)
    n_cache = ck_ref.shape[1]
    key_pad = jnp.zeros((KEY_PAD - n_cache - nq, D_KV), BF16)
    for b in range(T // nq):
        r0 = b * nq
        kk = jnp.concatenate([ck_ref[b].astype(BF16), zkv_b[r0:r0 + nq, :D_KV], key_pad], axis=0)
        vv = jnp.concatenate([cv_ref[b].astype(BF16), zkv_b[r0:r0 + nq, D_KV:], key_pad], axis=0)
        _attention_rows(zq, r0, nq, kk, vv, bias_ref, None, oatt_ref, heads_per_unit=REP)

    _layer_tail(x, xn_b, oatt_ref, osgu_ref, w_in_ref, sguw_ref, sgub_ref, lng_ref, lnb_ref,
                woa_ref, wos_ref, wout_ref, g2_ref, wr_ref, br_ref, tri_ref, low_ref,
                x1_ref, h_ref, route_ref, routet_ref, cnt_ref, vn_ref)


def _pack_slot_rows(x):
    lo = lax.bitcast_convert_type(x[:, :SLOT_WORDS].astype(BF16).astype(F32), U32)
    hi = lax.bitcast_convert_type(x[:, SLOT_WORDS:].astype(BF16).astype(F32), U32)
    return (lo >> 16) | hi


def _unpack_slot_rows(w):
    lo = lax.bitcast_convert_type(w << 16, F32)
    hi = lax.bitcast_convert_type(w & jnp.uint32(0xFFFF0000), F32)
    return jnp.concatenate([lo, hi], axis=1).astype(BF16)


def _as_granules(x):
    return x.reshape(x.shape[0] // GRAN, GRAN, SLOT_WORDS)


def _as_rows(x):
    return x.reshape(x.shape[0] * GRAN, SLOT_WORDS)


def _segment_copies(n_gran, src_gran, dst_gran, bits, make_copy, act):
    for b in range(min(bits, 2)):
        @pl.when(((n_gran >> b) & 1) == 1)
        def _(b=b):
            done = n_gran & ((1 << b) - 1)
            act(make_copy(src_gran + done, dst_gran + done, 1 << b))

    if bits > 2:
        def quad(q, carry):
            done = (n_gran & 3) + 4 * q
            act(make_copy(src_gran + done, dst_gran + done, 4))
            return carry
        lax.fori_loop(0, n_gran >> 2, quad, 0)


def _start_tile_copies(tile, live, cnt_ref, loff_ref, goff_ref, big_ref, make_copy):
    for e in range(N_EXPERTS):
        t = tile * N_EXPERTS + e
        n = jnp.where(live, cnt_ref[t], 0)
        src, dst = loff_ref[t], goff_ref[t]
        for b in range(2):
            @pl.when(((n >> b) & 1) == 1)
            def _(b=b, n=n, src=src, dst=dst):
                done = n & ((1 << b) - 1)
                make_copy(src + done, dst + done, 1 << b).start(priority=e % DMA_QUEUES)
        for q in range(FLAT_QUADS):
            @pl.when((n >> 2) > q)
            def _(q=q, n=n, src=src, dst=dst):
                done = (n & 3) + 4 * q
                make_copy(src + done, dst + done, 4).start(priority=e % DMA_QUEUES)

    @pl.when(live & (big_ref[tile] > 0))
    def _():
        def rest(e, carry):
            t = tile * N_EXPERTS + e
            n, src, dst = cnt_ref[t], loff_ref[t], goff_ref[t]

            def quad(q, c):
                done = (n & 3) + 4 * q
                make_copy(src + done, dst + done, 4).start()
                return c
            lax.fori_loop(FLAT_QUADS, n >> 2, quad, 0)
            return carry
        lax.fori_loop(0, N_EXPERTS, rest, 0)


def _wait_granules(n_gran, bits, make_copy):
    for b in range(bits):
        @pl.when(((n_gran >> b) & 1) == 1)
        def _(b=b):
            make_copy(1 << b).wait()


def _dispatch_kernel(cnt_ref, loff_ref, goff_ref, tot_ref, big_ref, padn_ref, padoff_ref,
                     route_ref, h_ref, xs_hbm, stage_ref, zero_ref, sem):
    i = pl.program_id(0)
    last = pl.num_programs(0) - 1
    slot = lax.rem(i, 2)
    T = TOK_TILE

    def seg_copy(tile_slot):
        return lambda s, d, n: pltpu.make_async_copy(
            stage_ref.at[tile_slot, pl.ds(s, n)], xs_hbm.at[pl.ds(d, n)], sem.at[tile_slot])

    def wait_tile(tile, tile_slot):
        _wait_granules(tot_ref[tile], TILE_BITS, lambda n: pltpu.make_async_copy(
            stage_ref.at[tile_slot, pl.ds(0, n)], xs_hbm.at[pl.ds(0, n)], sem.at[tile_slot]))

    @pl.when(i > 1)
    def _():
        wait_tile(i - 2, slot)

    _start_tile_copies(jnp.maximum(i - 1, 0), i > 0, cnt_ref, loff_ref, goff_ref, big_ref, seg_copy(1 - slot))

    pos = route_ref[0, TOP_K:2 * TOP_K, :].astype(I32)
    r_iota = lax.broadcasted_iota(I32, (LOCAL_ROWS, T), 0)
    p = jnp.zeros((LOCAL_ROWS, T), F32)
    for k in range(TOP_K):
        p = jnp.where(r_iota == pos[k:k + 1, :], 1.0, p)
    stage_ref[slot] = _as_granules(
        _pack_slot_rows(jnp.dot(p.astype(BF16), h_ref[...], preferred_element_type=F32)))

    @pl.when(i == last)
    def _():
        _start_tile_copies(i, i >= 0, cnt_ref, loff_ref, goff_ref, big_ref, seg_copy(slot))

        @pl.when(i > 0)
        def _():
            wait_tile(i - 1, 1 - slot)

        wait_tile(i, slot)
        zero_ref[...] = jnp.zeros(zero_ref.shape, U32)

        def pad_copy(s, d, n):
            return pltpu.make_async_copy(zero_ref.at[pl.ds(s, n)], xs_hbm.at[pl.ds(d, n)], sem.at[2])

        def for_pads(act):
            def body(e, carry):
                _segment_copies(padn_ref[e], 0, padoff_ref[e], PAD_BITS, pad_copy, act)
                return carry
            lax.fori_loop(0, N_EXPERTS, body, 0)

        for_pads(lambda cp: cp.start())
        for_pads(lambda cp: cp.wait())


def _expert_kernel(gran0_ref, nsub_ref, xs_hbm, w1_ref, b1_ref, w2_ref, b2_ref, ys_hbm,
                   w1b_ref, w2b_ref, xbuf_ref, ybuf_ref, sem_in, sem_out):
    e = pl.program_id(0)
    subs = ROW_BLOCK // SUB_BLOCK
    sub_gran = SUB_BLOCK // GRAN
    blk_gran = ROW_BLOCK // GRAN
    gran0 = gran0_ref[e]
    nsub = nsub_ref[e]
    nb_full = nsub // subs
    rem = nsub - nb_full * subs
    nb = nb_full + jnp.where(rem > 0, 1, 0)
    last_slot = lax.rem(nb + 1, 2)

    def in_copy(k, slot):
        return pltpu.make_async_copy(xs_hbm.at[pl.ds(gran0 + k * blk_gran, blk_gran)], xbuf_ref.at[slot],
                                     sem_in.at[slot])

    def out_copy(k, slot, m_sub):
        n = m_sub * sub_gran
        return pltpu.make_async_copy(ybuf_ref.at[slot, pl.ds(0, n)], ys_hbm.at[pl.ds(gran0 + k * blk_gran, n)],
                                     sem_out.at[slot])

    @pl.when(nb > 0)
    def _():
        in_copy(0, 0).start(priority=BLOCK_DMA_PRIORITY)

    w1b_ref[...] = w1_ref[0].astype(BF16)
    w2b_ref[...] = w2_ref[0].astype(BF16)

    def process(k, m_sub):
        n = m_sub * sub_gran
        slot = lax.rem(k, 2)

        @pl.when(k + 1 < nb)
        def _():
            in_copy(k + 1, 1 - slot).start(priority=BLOCK_DMA_PRIORITY)

        in_copy(k, slot).wait()

        @pl.when(k >= 2)
        def _():
            out_copy(k - 2, slot, subs).wait()

        x = _unpack_slot_rows(_as_rows(xbuf_ref[slot, 0:n]))
        h1 = jnp.dot(x, w1b_ref[...], preferred_element_type=F32) + b1_ref[0]
        gate = jnp.minimum(h1[:, :D_FF], SWIGLU_LIMIT)
        up = jnp.clip(h1[:, D_FF:], -SWIGLU_LIMIT, SWIGLU_LIMIT)
        act = gate * jax.nn.sigmoid(SWIGLU_ALPHA * gate) * (up + 1.0)
        y = jnp.dot(act.astype(BF16), w2b_ref[...], preferred_element_type=F32) + b2_ref[0]
        ybuf_ref[slot, 0:n] = _as_granules(_pack_slot_rows(y))
        out_copy(k, slot, m_sub).start(priority=BLOCK_DMA_PRIORITY)

    def full_block(k, carry):
        process(k, subs)
        return carry

    lax.fori_loop(0, nb_full, full_block, 0)

    for m_sub in range(1, subs):
        @pl.when(rem == m_sub)
        def _(m_sub=m_sub):
            process(nb_full, m_sub)

    @pl.when(nb >= 2)
    def _():
        out_copy(nb - 2, 1 - last_slot, subs).wait()

    @pl.when((nb >= 1) & (rem == 0))
    def _():
        out_copy(nb - 1, last_slot, subs).wait()

    for m_sub in range(1, subs):
        @pl.when(rem == m_sub)
        def _(m_sub=m_sub):
            out_copy(nb - 1, last_slot, m_sub).wait()


def _combine_kernel(n_prompt_tiles, cnt_ref, loff_ref, goff_ref, tot_ref, big_ref,
                    routet_ref, x1_ref, gf_ref, ys_hbm, yp_ref, ysm_ref, stage_ref, sem):
    i = pl.program_id(0)
    n_steps = pl.num_programs(0)
    slot = lax.rem(i, 2)
    T = TOK_TILE

    def seg_copy(tile_slot):
        return lambda s, d, n: pltpu.make_async_copy(
            ys_hbm.at[pl.ds(d, n)], stage_ref.at[tile_slot, pl.ds(s, n)], sem.at[tile_slot])

    @pl.when(i == 0)
    def _():
        stage_ref[...] = jnp.zeros(stage_ref.shape, U32)
        _start_tile_copies(i, i >= 0, cnt_ref, loff_ref, goff_ref, big_ref, seg_copy(slot))

    _start_tile_copies(jnp.minimum(i + 1, n_steps - 1), i + 1 < n_steps, cnt_ref, loff_ref, goff_ref, big_ref,
                       seg_copy(1 - slot))

    rt = routet_ref[...]
    l_iota = lax.broadcasted_iota(I32, (T, LOCAL_ROWS), 1)
    pw = jnp.zeros((T, LOCAL_ROWS), F32)
    for k in range(TOP_K):
        pos_k = rt[:, TOP_K + k:TOP_K + k + 1].astype(I32)
        pw = jnp.where(l_iota == pos_k, rt[:, 2 * TOP_K + k:2 * TOP_K + k + 1], pw)

    _wait_granules(tot_ref[i], TILE_BITS, lambda n: pltpu.make_async_copy(
        ys_hbm.at[pl.ds(0, n)], stage_ref.at[slot, pl.ds(0, n)], sem.at[slot]))
    moe = jnp.dot(pw.astype(BF16), _unpack_slot_rows(_as_rows(stage_ref[slot])), preferred_element_type=F32)
    y = _rms_norm(x1_ref[...] + moe, gf_ref[...])

    @pl.when(i < n_prompt_tiles)
    def _():
        yp_ref[...] = y

    @pl.when(i >= n_prompt_tiles)
    def _():
        ysm_ref[...] = y


def _t5_bucket(rel):
    half = NUM_BUCKETS // 2
    max_exact = half // 2
    ret = jnp.where(rel > 0, half, 0)
    n = jnp.abs(rel)
    nf = jnp.maximum(n, 1).astype(F32)
    large = max_exact + (jnp.log(nf / max_exact) / math.log(MAX_DISTANCE / max_exact)
                         * (half - max_exact)).astype(jnp.int32)
    large = jnp.minimum(large, half - 1)
    return ret + jnp.where(n < max_exact, n, large)


def _stacked_bias(table, sinks, q_pos, k_pos):
    nq, nk = q_pos.shape[0], k_pos.shape[0]
    bucket = _t5_bucket(k_pos[None, :] - q_pos[:, None])
    onehot = (bucket[:, :, None] == jnp.arange(NUM_BUCKETS)).astype(F32)
    bias = jnp.einsum('qkb,bh->hqk', onehot, table.astype(F32), precision=lax.Precision.HIGHEST)
    bias = bias.reshape(N_KV, REP * nq, nk)
    sink = jnp.repeat(sinks.astype(F32).reshape(N_KV, REP, 1), nq, axis=2).reshape(N_KV, REP * nq, 1)
    tail = jnp.full((N_KV, REP * nq, KEY_PAD - nk - 1), -jnp.inf, F32)
    return jnp.concatenate([bias, sink, tail], axis=-1)


def kernel(x_prompt, x_sample, cache_win_k, cache_win_v, norm1_g, w_in, attn_sinks, rel_bias_table, sgu_ln_g, sgu_ln_b, sgu_w, sgu_b, w_o_att, w_o_sgu, w_out, norm2_g, w_router, b_router, w_exp_in, b_exp_in, w_exp_out, b_exp_out, final_norm_g):
    batch, seq, _ = x_prompt.shape
    dec_batch, dec_seq, _ = x_sample.shape
    cache_rows = cache_win_k.shape[2]
    assert x_prompt.shape[2] == D_MODEL and w_in.shape == (1, D_MODEL, D_IN)
    assert seq % LAYER_TILE == 0 and LAYER_TILE % TOK_TILE == 0 and TOK_TILE % GMLP_CHUNK == 0
    assert LAYER_TILE >= WINDOW and LAYER_TILE % dec_seq == 0 and (dec_batch * dec_seq) % LAYER_TILE == 0
    assert dec_seq <= GMLP_CHUNK and GMLP_CHUNK % dec_seq == 0 and cache_rows == WINDOW
    T = TOK_TILE
    n_prompt = batch * seq
    n_sample = dec_batch * dec_seq
    n_tok = n_prompt + n_sample
    LT = LAYER_TILE
    sorts_per_step = LT // T
    tiles_per_seq = seq // LT
    n_psteps = n_prompt // LT
    n_ssteps = n_sample // LT
    n_ptiles = n_prompt // T
    n_tiles = n_tok // T
    seqs_per_tile = LT // dec_seq

    w_in_b = w_in[0].astype(BF16)
    w_q_b = w_in_b[:, :D_ATT].reshape(D_MODEL, N_KV, REP, HEAD_DIM).transpose(0, 2, 1, 3).reshape(
        D_MODEL, D_ATT)
    woa_b = w_o_att[0].reshape(N_KV, REP, HEAD_DIM, D_MODEL).transpose(1, 0, 2, 3).reshape(
        D_ATT, D_MODEL).astype(BF16)
    wos_b = w_o_sgu[0].astype(BF16)
    wout_b = w_out[0].astype(BF16)
    wr_b = jnp.pad(w_router[0], ((0, 0), (0, LANES - N_EXPERTS))).astype(BF16)
    br_col = b_router[0].astype(F32).reshape(N_EXPERTS, 1)
    g1 = norm1_g[0].reshape(1, D_MODEL)
    g2 = norm2_g[0].reshape(1, D_MODEL)
    gf = final_norm_g.reshape(1, D_MODEL)
    lng = sgu_ln_g[0].reshape(1, D_GMLP)
    lnb = sgu_ln_b[0].reshape(1, D_GMLP)
    tril = jnp.tril(jnp.ones((GMLP_CHUNK, GMLP_CHUNK), dtype=bool))
    sguw_p = jnp.where(tril[None], sgu_w[0], 0).astype(BF16)
    sgub_p = jnp.broadcast_to(sgu_b[0][:, :, None], (N_GROUPS, GMLP_CHUNK, GROUP_W)).astype(F32)
    reps = GMLP_CHUNK // dec_seq
    corner = jnp.where(tril[None, :dec_seq, :dec_seq], sgu_w[0][:, :dec_seq, :dec_seq], 0)
    sguw_s = jnp.einsum('ab,gij->gaibj', jnp.eye(reps, dtype=F32), corner).reshape(
        N_GROUPS, GMLP_CHUNK, GMLP_CHUNK).astype(BF16)
    sgub_s = jnp.broadcast_to(jnp.tile(sgu_b[0][:, :dec_seq], (1, reps))[:, :, None],
                              (N_GROUPS, GMLP_CHUNK, GROUP_W)).astype(F32)
    bias_p = _stacked_bias(rel_bias_table, attn_sinks[0], jnp.arange(CHUNK) + WINDOW, jnp.arange(BAND))
    bias_s = _stacked_bias(rel_bias_table, attn_sinks[0], cache_rows + jnp.arange(dec_seq),
                           jnp.arange(cache_rows + dec_seq))
    tri = jnp.triu(jnp.ones((T, T), F32), k=1).astype(BF16)
    low = jnp.tril(jnp.ones((N_EXPERTS, N_EXPERTS), F32), k=-1).astype(BF16)

    layer_out_shapes = (
        jax.ShapeDtypeStruct((n_tok, D_MODEL), F32),
        jax.ShapeDtypeStruct((n_tok, D_MODEL), BF16),
        jax.ShapeDtypeStruct((n_tiles, ROUTE_ROWS, T), F32),
        jax.ShapeDtypeStruct((n_tok, LANES), F32),
        jax.ShapeDtypeStruct((n_tiles, N_EXPERTS, LANES), F32),
    )
    shared_consts = (w_q_b, w_in_b)
    tail_consts_p = (sguw_p, sgub_p, lng, lnb, woa_b, wos_b, wout_b, g2, wr_b, br_col, tri, low)
    tail_consts_s = (sguw_s, sgub_s, lng, lnb, woa_b, wos_b, wout_b, g2, wr_b, br_col, tri, low)

    def tile_specs(tile_of):
        return [
            pl.BlockSpec((LT, D_MODEL), lambda *g: (tile_of(*g), 0)),
            pl.BlockSpec((LT, D_MODEL), lambda *g: (tile_of(*g), 0)),
            pl.BlockSpec((sorts_per_step, ROUTE_ROWS, T), lambda *g: (tile_of(*g), 0, 0)),
            pl.BlockSpec((LT, LANES), lambda *g: (tile_of(*g), 0)),
            pl.BlockSpec((sorts_per_step, N_EXPERTS, LANES), lambda *g: (tile_of(*g), 0, 0)),
        ]

    prompt_consts = (g1,) + shared_consts + (bias_p,) + tail_consts_p
    x1, h, route, routet, cnt, kvwin = pl.pallas_call(
        _prompt_layer_kernel,
        grid=(batch, tiles_per_seq),
        in_specs=[pl.BlockSpec((1, LT, D_MODEL), lambda b, j: (b, j, 0))]
                 + [_const_spec(c.shape) for c in prompt_consts],
        out_specs=tile_specs(lambda b, j: b * tiles_per_seq + j)
                  + [pl.BlockSpec((1, WINDOW, 2 * D_KV), lambda b, j: (b, 0, 0))],
        out_shape=layer_out_shapes + (jax.ShapeDtypeStruct((batch, WINDOW, 2 * D_KV), F32),),
        scratch_shapes=[pltpu.VMEM((LT + WINDOW, 2 * D_KV), BF16),
                        pltpu.VMEM((LT, D_ATT), BF16),
                        pltpu.VMEM((LT, D_GMLP), BF16)],
        compiler_params=pltpu.CompilerParams(dimension_semantics=("arbitrary", "arbitrary"),
                                             vmem_limit_bytes=VMEM_LIMIT),
        name="layer_prompt",
    )(x_prompt, *prompt_consts)

    xs_flat = x_sample.reshape(n_sample, D_MODEL)
    ck = cache_win_k[0].reshape(dec_batch, cache_rows, D_KV)
    cv = cache_win_v[0].reshape(dec_batch, cache_rows, D_KV)
    sample_consts = (g1,) + shared_consts + (bias_s,) + tail_consts_s
    any_spec = pl.BlockSpec(memory_space=pl.ANY)
    x1, h, route, routet, cnt, kvnew, vn_s = pl.pallas_call(
        _sample_layer_kernel,
        grid=(n_ssteps,),
        in_specs=[any_spec] * 5
                 + [pl.BlockSpec((LT, D_MODEL), lambda i: (i, 0)),
                    pl.BlockSpec((seqs_per_tile, cache_rows, D_KV), lambda i: (i, 0, 0)),
                    pl.BlockSpec((seqs_per_tile, cache_rows, D_KV), lambda i: (i, 0, 0))]
                 + [_const_spec(c.shape) for c in sample_consts],
        out_specs=tile_specs(lambda i: n_psteps + i)
                  + [pl.BlockSpec((LT, 2 * D_KV), lambda i: (i, 0)),
                     pl.BlockSpec((LT, D_GMLP), lambda i: (i, 0))],
        out_shape=layer_out_shapes + (jax.ShapeDtypeStruct((n_sample, 2 * D_KV), F32),
                                      jax.ShapeDtypeStruct((n_sample, D_GMLP), F32)),
        scratch_shapes=[pltpu.VMEM((LT, D_ATT), BF16), pltpu.VMEM((LT, D_GMLP), BF16)],
        input_output_aliases={0: 0, 1: 1, 2: 2, 3: 3, 4: 4},
        compiler_params=pltpu.CompilerParams(dimension_semantics=("arbitrary",),
                                             vmem_limit_bytes=VMEM_LIMIT),
        name="layer_sample",
    )(x1, h, route, routet, cnt, xs_flat, ck, cv, *sample_consts)

    blk_gran = SUB_BLOCK // GRAN
    counts =cnt[:, :, 0].astype(I32)
    seg_gran = (counts + (GRAN - 1)) // GRAN
    local_off = jnp.cumsum(seg_gran, axis=1) - seg_gran
    tot_gran = jnp.sum(seg_gran, axis=0)
    ptot_gran = (tot_gran + (blk_gran - 1)) // blk_gran * blk_gran
    pend_gran = jnp.cumsum(ptot_gran)
    gstart = pend_gran - ptot_gran
    global_off = gstart[None, :] + jnp.cumsum(seg_gran, axis=0) - seg_gran
    pad_n = ptot_gran - tot_gran
    pad_off = gstart + tot_gran
    n_rows = -(-(TOP_K * n_tok + n_tiles * N_EXPERTS * (GRAN - 1) + N_EXPERTS * (SUB_BLOCK - GRAN))
               // SUB_BLOCK) * SUB_BLOCK + ROW_BLOCK
    region_gran0 = gstart.astype(I32)
    region_subs = (ptot_gran // blk_gran).astype(I32)
    tile_gran = jnp.sum(seg_gran, axis=1).astype(I32)
    tile_big = jnp.any(seg_gran >= 4 * (FLAT_QUADS + 1), axis=1).astype(I32)
    seg_gran_f = seg_gran.reshape(-1).astype(I32)
    local_off_f = local_off.reshape(-1).astype(I32)
    global_off_f = global_off.reshape(-1).astype(I32)

    xs = pl.pallas_call(
        _dispatch_kernel,
        grid_spec=pltpu.PrefetchScalarGridSpec(
            num_scalar_prefetch=7,
            grid=(n_tiles,),
            in_specs=[pl.BlockSpec((1, ROUTE_ROWS, T), lambda i, *_: (i, 0, 0)),
                      pl.BlockSpec((T, D_MODEL), lambda i, *_: (i, 0))],
            out_specs=pl.BlockSpec(memory_space=pl.ANY),
            scratch_shapes=[pltpu.VMEM((2, LOCAL_ROWS // GRAN, GRAN, SLOT_WORDS), U32),
                            pltpu.VMEM((SUB_BLOCK // GRAN, GRAN, SLOT_WORDS), U32),
                            pltpu.SemaphoreType.DMA((3,))]),
        out_shape=jax.ShapeDtypeStruct((n_rows // GRAN, GRAN, SLOT_WORDS), U32),
        compiler_params=pltpu.CompilerParams(dimension_semantics=("arbitrary",),
                                             vmem_limit_bytes=VMEM_LIMIT),
        name="moe_dispatch",
    )(seg_gran_f, local_off_f, global_off_f, tile_gran, tile_big, pad_n.astype(I32), pad_off.astype(I32),
      route, h)

    ys = pl.pallas_call(
        _expert_kernel,
        grid_spec=pltpu.PrefetchScalarGridSpec(
            num_scalar_prefetch=2,
            grid=(N_EXPERTS,),
            in_specs=[pl.BlockSpec(memory_space=pl.ANY),
                      pl.BlockSpec((1, D_MODEL, 2 * D_FF), lambda e, *_: (e, 0, 0)),
                      pl.BlockSpec((1, 1, 2 * D_FF), lambda e, *_: (e, 0, 0)),
                      pl.BlockSpec((1, D_FF, D_MODEL), lambda e, *_: (e, 0, 0)),
                      pl.BlockSpec((1, 1, D_MODEL), lambda e, *_: (e, 0, 0))],
            out_specs=pl.BlockSpec(memory_space=pl.ANY),
            scratch_shapes=[pltpu.VMEM((D_MODEL, 2 * D_FF), BF16), pltpu.VMEM((D_FF, D_MODEL), BF16),
                            pltpu.VMEM((2, ROW_BLOCK // GRAN, GRAN, SLOT_WORDS), U32),
                            pltpu.VMEM((2, ROW_BLOCK // GRAN, GRAN, SLOT_WORDS), U32),
                            pltpu.SemaphoreType.DMA((2,)), pltpu.SemaphoreType.DMA((2,))]),
        out_shape=jax.ShapeDtypeStruct((n_rows // GRAN, GRAN, SLOT_WORDS), U32),
        compiler_params=pltpu.CompilerParams(dimension_semantics=("arbitrary",),
                                             vmem_limit_bytes=VMEM_LIMIT),
        name="moe_experts",
    )(region_gran0, region_subs, xs, w_exp_in[0], b_exp_in[0].reshape(N_EXPERTS, 1, 2 * D_FF),
      w_exp_out[0], b_exp_out[0].reshape(N_EXPERTS, 1, D_MODEL))

    y_p, y_s = pl.pallas_call(
        functools.partial(_combine_kernel, n_ptiles),
        grid_spec=pltpu.PrefetchScalarGridSpec(
            num_scalar_prefetch=5,
            grid=(n_tiles,),
            in_specs=[pl.BlockSpec((T, LANES), lambda i, *_: (i, 0)),
                      pl.BlockSpec((T, D_MODEL), lambda i, *_: (i, 0)),
                      pl.BlockSpec((1, D_MODEL), lambda i, *_: (0, 0)),
                      pl.BlockSpec(memory_space=pl.ANY)],
            out_specs=[pl.BlockSpec((T, D_MODEL), lambda i, *_: (jnp.minimum(i, n_ptiles - 1), 0)),
                       pl.BlockSpec((T, D_MODEL), lambda i, *_: (jnp.maximum(i - n_ptiles, 0), 0))],
            scratch_shapes=[pltpu.VMEM((2, LOCAL_ROWS // GRAN, GRAN, SLOT_WORDS), U32), pltpu.SemaphoreType.DMA((2,))]),
        out_shape=(jax.ShapeDtypeStruct((n_prompt, D_MODEL), F32),
                   jax.ShapeDtypeStruct((n_sample, D_MODEL), F32)),
        compiler_params=pltpu.CompilerParams(dimension_semantics=("arbitrary",),
                                             vmem_limit_bytes=VMEM_LIMIT),
        name="moe_combine",
    )(seg_gran_f, local_off_f, global_off_f, tile_gran, tile_big, routet, x1, gf, ys)

    y_prompt = y_p.reshape(batch, seq, D_MODEL)
    y_sample = y_s.reshape(dec_batch, dec_seq, D_MODEL)
    new_win_k_prompt = kvwin[:, :, :D_KV].reshape(1, batch, WINDOW, N_KV, HEAD_DIM)
    new_win_v_prompt = kvwin[:, :, D_KV:].reshape(1, batch, WINDOW, N_KV, HEAD_DIM)
    new_win_k_sample = kvnew[:, :D_KV].reshape(1, dec_batch, dec_seq, N_KV, HEAD_DIM)
    new_win_v_sample = kvnew[:, D_KV:].reshape(1, dec_batch, dec_seq, N_KV, HEAD_DIM)
    new_sgu_v_sample = vn_s.reshape(1, dec_batch, dec_seq, D_GMLP)
    return (y_prompt, y_sample, new_win_k_prompt, new_win_v_prompt, new_win_k_sample,
            new_win_v_sample, new_sgu_v_sample)
```

```python
import functools
import math

import numpy as np
import jax
import jax.numpy as jnp
from jax import lax
from jax.experimental import pallas as pl
from jax.experimental.pallas import tpu as pltpu

F32 = jnp.float32
BF16 = jnp.bfloat16
I32 = jnp.int32
U32 = jnp.uint32

D_MODEL = 1024
HEAD_DIM = 64
N_HEADS = 16
N_KV = 2
REP = N_HEADS // N_KV
CHUNK = 64
WINDOW = 128
BAND = WINDOW + CHUNK
KEY_PAD = 256
D_ATT = N_HEADS * HEAD_DIM
D_KV = N_KV * HEAD_DIM
NUM_BUCKETS = 32
MAX_DISTANCE = 128
GMLP_CHUNK = 128
D_GMLP = 1024
N_GROUPS = 4
GROUP_W = D_GMLP // N_GROUPS
N_EXPERTS = 32
TOP_K = 4
D_FF = 1024
SWIGLU_LIMIT = 7.0
SWIGLU_ALPHA = 1.702
NORM_EPS = 1e-5
D_IN = D_ATT + 2 * D_KV + 2 * D_GMLP + 2 * D_MODEL
COL_KV = D_ATT
COL_U = COL_KV + 2 * D_KV
COL_VG = COL_U + D_GMLP
COL_GA = COL_VG + D_GMLP
COL_GB = COL_GA + D_MODEL
SQRT_HALF = float(np.sqrt(0.5))

LANES = 128
WORD_SUBLANES = 8
VMEM_LIMIT = 56 * 1024 * 1024

TOK_TILE = 256
LAYER_TILE = 512
GRAN = WORD_SUBLANES
SLOT_WORDS = D_MODEL // 2
FLAT_QUADS = 1
LOCAL_ROWS = TOP_K * TOK_TILE + N_EXPERTS * GRAN
ROW_BLOCK = 512
SUB_BLOCK = 128
PAD_BITS = (SUB_BLOCK // GRAN - 1).bit_length()
TILE_BITS = (LOCAL_ROWS // GRAN).bit_length()
BLOCK_DMA_PRIORITY = 1
ROUTE_ROWS = 16


def _const_spec(shape):
    nd = len(shape)
    return pl.BlockSpec(shape, lambda *_: (0,) * nd, pipeline_mode=pl.Buffered(1))


def _rms_norm(x, g):
    ms = jnp.mean(x * x, axis=-1, keepdims=True)
    return x * lax.rsqrt(ms + NORM_EPS) * g


def _gelu(x):
    return 0.5 * x * (1.0 + lax.erf(x * SQRT_HALF))


def _attend(q, k, v, bias, valid):
    logits = lax.dot_general(q, k, (((1,), (1,)), ((), ())), preferred_element_type=F32) + bias
    if valid is not None:
        logits = jnp.where(valid, logits, -jnp.inf)
    m = jnp.max(logits, axis=-1, keepdims=True)
    p = jnp.exp(logits - m)
    den = jnp.sum(p, axis=-1, keepdims=True)
    w = (p * (1.0 / den)).astype(BF16)
    return jnp.dot(w, v, preferred_element_type=F32)


def _attention_rows(zq, row0, n_rows, k_ext, v_ext, bias_ref, valid, oatt_ref, heads_per_unit):
    lane = lax.broadcasted_iota(I32, (1, LANES), 1)
    for g in range(N_KV):
        in_group = (lane >= g * HEAD_DIM) & (lane < (g + 1) * HEAD_DIM)
        qmask = jnp.where(in_group, HEAD_DIM ** -0.5, 0.0).astype(BF16)
        for part in range(REP // heads_per_unit):
            tiles = range(part * heads_per_unit, (part + 1) * heads_per_unit)
            q = jnp.concatenate([zq[row0:row0 + n_rows, t * LANES:(t + 1) * LANES] * qmask for t in tiles],
                                axis=0)
            b0 = part * heads_per_unit * n_rows
            o = _attend(q, k_ext, v_ext, bias_ref[g, b0:b0 + heads_per_unit * n_rows, :], valid)
            for u, t in enumerate(tiles):
                c0 = t * LANES + g * HEAD_DIM
                oatt_ref[row0:row0 + n_rows, c0:c0 + HEAD_DIM] = (
                    o[u * n_rows:(u + 1) * n_rows, g * HEAD_DIM:(g + 1) * HEAD_DIM].astype(BF16))


def _layer_tail(x, xn_b, oatt_ref, osgu_ref, w_in_ref, sguw_ref, sgub_ref, lng_ref, lnb_ref,
                woa_ref, wos_ref, wout_ref, g2_ref, wr_ref, br_ref, tri_ref, low_ref,
                x1_ref, h_ref, route_ref, routet_ref, cnt_ref, vn_ref):
    T = x.shape[0]
    u = _gelu(jnp.dot(xn_b, w_in_ref[:, COL_U:COL_VG], preferred_element_type=F32))
    vg = _gelu(jnp.dot(xn_b, w_in_ref[:, COL_VG:COL_GA], preferred_element_type=F32))
    mu = jnp.mean(vg, axis=-1, keepdims=True)
    var = jnp.mean(jnp.square(vg - mu), axis=-1, keepdims=True)
    vn = (vg - mu) * lax.rsqrt(var + NORM_EPS) * lng_ref[...] + lnb_ref[...]
    if vn_ref is not None:
        vn_ref[...] = vn
    vn_b = vn.astype(BF16)
    for j in range(T // GMLP_CHUNK):
        rows = slice(j * GMLP_CHUNK, (j + 1) * GMLP_CHUNK)
        for g in range(N_GROUPS):
            cols = slice(g * GROUP_W, (g + 1) * GROUP_W)
            mixed = jnp.dot(sguw_ref[g], vn_b[rows, cols], preferred_element_type=F32) + sgub_ref[g]
            osgu_ref[rows, cols] = (u[rows, cols] * mixed).astype(BF16)

    ga = jax.nn.sigmoid(jnp.dot(xn_b, w_in_ref[:, COL_GA:COL_GB], preferred_element_type=F32))
    merged = ga * jnp.dot(oatt_ref[...], woa_ref[...], preferred_element_type=F32)
    gb = jax.nn.sigmoid(jnp.dot(xn_b, w_in_ref[:, COL_GB:D_IN], preferred_element_type=F32))
    merged = merged + gb * jnp.dot(osgu_ref[...], wos_ref[...], preferred_element_type=F32)
    x1 = x + jnp.dot(merged.astype(BF16), wout_ref[...], preferred_element_type=F32)
    x1_ref[...] = x1

    h_b = _rms_norm(x1, g2_ref[...]).astype(BF16)
    h_ref[...] = h_b
    lt = lax.dot_general(wr_ref[...], h_b, (((1,), (1,)), ((), ())),
                         preferred_element_type=F32) + br_ref[...]
    for s in range(T // TOK_TILE):
        rec, cnt_b = _route_sort_tile(lt[:, s * TOK_TILE:(s + 1) * TOK_TILE], tri_ref, low_ref)
        route_ref[s] = rec
        rec_pad = jnp.concatenate([rec, jnp.zeros((LANES - ROUTE_ROWS, TOK_TILE), F32)], axis=0)
        routet_ref[s * TOK_TILE:(s + 1) * TOK_TILE, :] = jnp.transpose(rec_pad)
        cnt_ref[s] = cnt_b


def _route_sort_tile(lt, tri_ref, low_ref):
    T = lt.shape[1]
    e_iota = lax.broadcasted_iota(I32, (N_EXPERTS, T), 0).astype(F32)
    cur = lt
    vals, idxs = [], []
    for _ in range(TOP_K):
        m = jnp.max(cur, axis=0, keepdims=True)
        ik = jnp.min(jnp.where(cur == m, e_iota, float(N_EXPERTS)), axis=0, keepdims=True)
        vals.append(m)
        idxs.append(ik)
        cur = jnp.where(e_iota == ik, -jnp.inf, cur)
    exps = [jnp.exp(v - vals[0]) for v in vals]
    den = exps[0] + exps[1] + exps[2] + exps[3]
    gates = [e / den for e in exps]

    onehot = jnp.zeros((N_EXPERTS, T), F32)
    for ik in idxs:
        onehot = onehot + jnp.where(e_iota == ik, 1.0, 0.0)
    rank = jnp.dot(onehot.astype(BF16), tri_ref[...], preferred_element_type=F32)
    cnt = jnp.sum(onehot, axis=1, keepdims=True)
    cnt_b = jnp.broadcast_to(cnt, (N_EXPERTS, LANES))
    gran = jnp.floor((cnt_b + (GRAN - 1)) * (1.0 / GRAN))
    off = jnp.dot(low_ref[...], gran.astype(BF16), preferred_element_type=F32) * GRAN
    base = off[:, 0:1] + rank
    poss = [jnp.sum(jnp.where(e_iota == ik, base, 0.0), axis=0, keepdims=True) for ik in idxs]

    rec = jnp.concatenate(idxs + poss + gates
                          + [jnp.zeros((ROUTE_ROWS - 3 * TOP_K, T), F32)], axis=0)
    return rec, cnt_b


def _prompt_layer_kernel(x_ref, g1_ref, wq_ref, w_in_ref, bias_ref, sguw_ref, sgub_ref, lng_ref,
                         lnb_ref, woa_ref, wos_ref, wout_ref, g2_ref, wr_ref, br_ref, tri_ref, low_ref,
                         x1_ref, h_ref, route_ref, routet_ref, cnt_ref, kvwin_ref,
                         kvx_ref, oatt_ref, osgu_ref):
    T = LAYER_TILE
    j = pl.program_id(1)
    x = x_ref[0]
    xn_b = _rms_norm(x, g1_ref[...]).astype(BF16)
    zq = jnp.dot(xn_b, wq_ref[...], preferred_element_type=F32).astype(BF16)
    zkv = jnp.dot(xn_b, w_in_ref[:, COL_KV:COL_U], preferred_element_type=F32)

    @pl.when(j == pl.num_programs(1) - 1)
    def _():
        kvwin_ref[0] = zkv[T - WINDOW:, :]

    @pl.when(j == 0)
    def _():
        kvx_ref[0:WINDOW, :] = jnp.zeros((WINDOW, 2 * D_KV), BF16)

    @pl.when(j > 0)
    def _():
        kvx_ref[0:WINDOW, :] = kvx_ref[T:T + WINDOW, :]

    kvx_ref[WINDOW:, :] = zkv.astype(BF16)

    col = lax.broadcasted_iota(I32, (1, KEY_PAD), 1)
    key_pad = jnp.zeros((KEY_PAD - BAND, 2 * D_KV), BF16)
    for c in range(T // CHUNK):
        r0 = c * CHUNK
        kvb = jnp.concatenate([kvx_ref[r0:r0 + BAND, :], key_pad], axis=0)
        valid = ((col + (j * T + r0 - WINDOW)) >= 0) | (col >= BAND) if r0 < WINDOW else None
        _attention_rows(zq, r0, CHUNK, kvb[:, :D_KV], kvb[:, D_KV:], bias_ref, valid, oatt_ref,
                        heads_per_unit=4)

    _layer_tail(x, xn_b, oatt_ref, osgu_ref, w_in_ref, sguw_ref, sgub_ref, lng_ref, lnb_ref,
                woa_ref, wos_ref, wout_ref, g2_ref, wr_ref, br_ref, tri_ref, low_ref,
                x1_ref, h_ref, route_ref, routet_ref, cnt_ref, None)


def _sample_layer_kernel(x1_in, h_in, route_in, routet_in, cnt_in,
                         x_ref, ck_ref, cv_ref, g1_ref, wq_ref, w_in_ref, bias_ref, sguw_ref, sgub_ref,
                         lng_ref, lnb_ref, woa_ref, wos_ref, wout_ref, g2_ref, wr_ref, br_ref, tri_ref,
                         low_ref,
                         x1_ref, h_ref, route_ref, routet_ref, cnt_ref, kvnew_ref, vn_ref,
                         oatt_ref, osgu_ref):
    del x1_in, h_in, route_in, routet_in, cnt_in
    T = LAYER_TILE
    nq = x_ref.shape[0] // ck_ref.shape[0]
    x = x_ref[...]
    xn_b = _rms_norm(x, g1_ref[...]).astype(BF16)
    zq = jnp.dot(xn_b, wq_ref[...], preferred_element_type=F32).astype(BF16)
    zkv = jnp.dot(xn_b, w_in_ref[:, COL_KV:COL_U], preferred_element_type=F32)
    kvnew_ref[...] = zkv
    zkv_b = zkv.astype(BF16)
    n_cache = ck_ref.shape[1]
    key_pad = jnp.zeros((KEY_PAD - n_cache - nq, D_KV), BF16)
    for b in range(T // nq):
        r0 = b * nq
        kk = jnp.concatenate([ck_ref[b].astype(BF16), zkv_b[r0:r0 + nq, :D_KV], key_pad], axis=0)
        vv = jnp.concatenate([cv_ref[b].astype(BF16), zkv_b[r0:r0 + nq, D_KV:], key_pad], axis=0)
        _attention_rows(zq, r0, nq, kk, vv, bias_ref, None, oatt_ref, heads_per_unit=REP)

    _layer_tail(x, xn_b, oatt_ref, osgu_ref, w_in_ref, sguw_ref, sgub_ref, lng_ref, lnb_ref,
                woa_ref, wos_ref, wout_ref, g2_ref, wr_ref, br_ref, tri_ref, low_ref,
                x1_ref, h_ref, route_ref, routet_ref, cnt_ref, vn_ref)


def _pack_slot_rows(x):
    lo = lax.bitcast_convert_type(x[:, :SLOT_WORDS].astype(BF16).astype(F32), U32)
    hi = lax.bitcast_convert_type(x[:, SLOT_WORDS:].astype(BF16).astype(F32), U32)
    return (lo >> 16) | hi


def _unpack_slot_rows(w):
    lo = lax.bitcast_convert_type(w << 16, F32)
    hi = lax.bitcast_convert_type(w & jnp.uint32(0xFFFF0000), F32)
    return jnp.concatenate([lo, hi], axis=1).astype(BF16)


def _as_granules(x):
    return x.reshape(x.shape[0] // GRAN, GRAN, SLOT_WORDS)


def _as_rows(x):
    return x.reshape(x.shape[0] * GRAN, SLOT_WORDS)


def _segment_copies(n_gran, src_gran, dst_gran, bits, make_copy, act):
    for b in range(min(bits, 2)):
        @pl.when(((n_gran >> b) & 1) == 1)
        def _(b=b):
            done = n_gran & ((1 << b) - 1)
            act(make_copy(src_gran + done, dst_gran + done, 1 << b))

    if bits > 2:
        def quad(q, carry):
            done = (n_gran & 3) + 4 * q
            act(make_copy(src_gran + done, dst_gran + done, 4))
            return carry
        lax.fori_loop(0, n_gran >> 2, quad, 0)


def _start_tile_copies(tile, live, cnt_ref, loff_ref, goff_ref, big_ref, make_copy):
    for e in range(N_EXPERTS):
        t = tile * N_EXPERTS + e
        n = jnp.where(live, cnt_ref[t], 0)
        src, dst = loff_ref[t], goff_ref[t]
        for b in range(2):
            @pl.when(((n >> b) & 1) == 1)
            def _(b=b, n=n, src=src, dst=dst):
                done = n & ((1 << b) - 1)
                make_copy(src + done, dst + done, 1 << b).start()
        for q in range(FLAT_QUADS):
            @pl.when((n >> 2) > q)
            def _(q=q, n=n, src=src, dst=dst):
                done = (n & 3) + 4 * q
                make_copy(src + done, dst + done, 4).start()

    @pl.when(live & (big_ref[tile] > 0))
    def _():
        def rest(e, carry):
            t = tile * N_EXPERTS + e
            n, src, dst = cnt_ref[t], loff_ref[t], goff_ref[t]

            def quad(q, c):
                done = (n & 3) + 4 * q
                make_copy(src + done, dst + done, 4).start()
                return c
            lax.fori_loop(FLAT_QUADS, n >> 2, quad, 0)
            return carry
        lax.fori_loop(0, N_EXPERTS, rest, 0)


def _wait_granules(n_gran, bits, make_copy):
    for b in range(bits):
        @pl.when(((n_gran >> b) & 1) == 1)
        def _(b=b):
            make_copy(1 << b).wait()


def _dispatch_kernel(cnt_ref, loff_ref, goff_ref, tot_ref, big_ref, padn_ref, padoff_ref,
                     route_ref, h_ref, xs_hbm, stage_ref, zero_ref, sem):
    i = pl.program_id(0)
    last = pl.num_programs(0) - 1
    slot = lax.rem(i, 2)
    T = TOK_TILE

    def seg_copy(tile_slot):
        return lambda s, d, n: pltpu.make_async_copy(
            stage_ref.at[tile_slot, pl.ds(s, n)], xs_hbm.at[pl.ds(d, n)], sem.at[tile_slot])

    def wait_tile(tile, tile_slot):
        _wait_granules(tot_ref[tile], TILE_BITS, lambda n: pltpu.make_async_copy(
            stage_ref.at[tile_slot, pl.ds(0, n)], xs_hbm.at[pl.ds(0, n)], sem.at[tile_slot]))

    @pl.when(i > 1)
    def _():
        wait_tile(i - 2, slot)

    _start_tile_copies(jnp.maximum(i - 1, 0), i > 0, cnt_ref, loff_ref, goff_ref, big_ref, seg_copy(1 - slot))

    pos = route_ref[0, TOP_K:2 * TOP_K, :].astype(I32)
    r_iota = lax.broadcasted_iota(I32, (LOCAL_ROWS, T), 0)
    p = jnp.zeros((LOCAL_ROWS, T), F32)
    for k in range(TOP_K):
        p = jnp.where(r_iota == pos[k:k + 1, :], 1.0, p)
    stage_ref[slot] = _as_granules(
        _pack_slot_rows(jnp.dot(p.astype(BF16), h_ref[...], preferred_element_type=F32)))

    @pl.when(i == last)
    def _():
        _start_tile_copies(i, i >= 0, cnt_ref, loff_ref, goff_ref, big_ref, seg_copy(slot))

        @pl.when(i > 0)
        def _():
            wait_tile(i - 1, 1 - slot)

        wait_tile(i, slot)
        zero_ref[...] = jnp.zeros(zero_ref.shape, U32)

        def pad_copy(s, d, n):
            return pltpu.make_async_copy(zero_ref.at[pl.ds(s, n)], xs_hbm.at[pl.ds(d, n)], sem.at[2])

        def for_pads(act):
            def body(e, carry):
                _segment_copies(padn_ref[e], 0, padoff_ref[e], PAD_BITS, pad_copy, act)
                return carry
            lax.fori_loop(0, N_EXPERTS, body, 0)

        for_pads(lambda cp: cp.start())
        for_pads(lambda cp: cp.wait())


def _expert_kernel(gran0_ref, nsub_ref, xs_hbm, w1_ref, b1_ref, w2_ref, b2_ref, ys_hbm,
                   w1b_ref, w2b_ref, xbuf_ref, ybuf_ref, sem_in, sem_out):
    e = pl.program_id(0)
    subs = ROW_BLOCK // SUB_BLOCK
    sub_gran = SUB_BLOCK // GRAN
    blk_gran = ROW_BLOCK // GRAN
    gran0 = gran0_ref[e]
    nsub = nsub_ref[e]
    nb_full = nsub // subs
    rem = nsub - nb_full * subs
    nb = nb_full + jnp.where(rem > 0, 1, 0)
    last_slot = lax.rem(nb + 1, 2)

    def in_copy(k, slot):
        return pltpu.make_async_copy(xs_hbm.at[pl.ds(gran0 + k * blk_gran, blk_gran)], xbuf_ref.at[slot],
                                     sem_in.at[slot])

    def out_copy(k, slot, m_sub):
        n = m_sub * sub_gran
        return pltpu.make_async_copy(ybuf_ref.at[slot, pl.ds(0, n)], ys_hbm.at[pl.ds(gran0 + k * blk_gran, n)],
                                     sem_out.at[slot])

    @pl.when(nb > 0)
    def _():
        in_copy(0, 0).start(priority=BLOCK_DMA_PRIORITY)

    w1b_ref[...] = w1_ref[0].astype(BF16)
    w2b_ref[...] = w2_ref[0].astype(BF16)

    def process(k, m_sub):
        n = m_sub * sub_gran
        slot = lax.rem(k, 2)

        @pl.when(k + 1 < nb)
        def _():
            in_copy(k + 1, 1 - slot).start(priority=BLOCK_DMA_PRIORITY)

        in_copy(k, slot).wait()

        @pl.when(k >= 2)
        def _():
            out_copy(k - 2, slot, subs).wait()

        x = _unpack_slot_rows(_as_rows(xbuf_ref[slot, 0:n]))
        h1 = jnp.dot(x, w1b_ref[...], preferred_element_type=F32) + b1_ref[0]
        gate = jnp.minimum(h1[:, :D_FF], SWIGLU_LIMIT)
        up = jnp.clip(h1[:, D_FF:], -SWIGLU_LIMIT, SWIGLU_LIMIT)
        act = gate * jax.nn.sigmoid(SWIGLU_ALPHA * gate) * (up + 1.0)
        y = jnp.dot(act.astype(BF16), w2b_ref[...], preferred_element_type=F32) + b2_ref[0]
        ybuf_ref[slot, 0:n] = _as_granules(_pack_slot_rows(y))
        out_copy(k, slot, m_sub).start(priority=BLOCK_DMA_PRIORITY)

    def full_block(k, carry):
        process(k, subs)
        return carry

    lax.fori_loop(0, nb_full, full_block, 0)

    for m_sub in range(1, subs):
        @pl.when(rem == m_sub)
        def _(m_sub=m_sub):
            process(nb_full, m_sub)

    @pl.when(nb >= 2)
    def _():
        out_copy(nb - 2, 1 - last_slot, subs).wait()

    @pl.when((nb >= 1) & (rem == 0))
    def _():
        out_copy(nb - 1, last_slot, subs).wait()

    for m_sub in range(1, subs):
        @pl.when(rem == m_sub)
        def _(m_sub=m_sub):
            out_copy(nb - 1, last_slot, m_sub).wait()


def _combine_kernel(n_prompt_tiles, cnt_ref, loff_ref, goff_ref, tot_ref, big_ref,
                    routet_ref, x1_ref, gf_ref, ys_hbm, yp_ref, ysm_ref, stage_ref, sem):
    i = pl.program_id(0)
    n_steps = pl.num_programs(0)
    slot = lax.rem(i, 2)
    T = TOK_TILE

    def seg_copy(tile_slot):
        return lambda s, d, n: pltpu.make_async_copy(
            ys_hbm.at[pl.ds(d, n)], stage_ref.at[tile_slot, pl.ds(s, n)], sem.at[tile_slot])

    @pl.when(i == 0)
    def _():
        stage_ref[...] = jnp.zeros(stage_ref.shape, U32)
        _start_tile_copies(i, i >= 0, cnt_ref, loff_ref, goff_ref, big_ref, seg_copy(slot))

    _start_tile_copies(jnp.minimum(i + 1, n_steps - 1), i + 1 < n_steps, cnt_ref, loff_ref, goff_ref, big_ref,
                       seg_copy(1 - slot))

    rt = routet_ref[...]
    l_iota = lax.broadcasted_iota(I32, (T, LOCAL_ROWS), 1)
    pw = jnp.zeros((T, LOCAL_ROWS), F32)
    for k in range(TOP_K):
        pos_k = rt[:, TOP_K + k:TOP_K + k + 1].astype(I32)
        pw = jnp.where(l_iota == pos_k, rt[:, 2 * TOP_K + k:2 * TOP_K + k + 1], pw)

    _wait_granules(tot_ref[i], TILE_BITS, lambda n: pltpu.make_async_copy(
        ys_hbm.at[pl.ds(0, n)], stage_ref.at[slot, pl.ds(0, n)], sem.at[slot]))
    moe = jnp.dot(pw.astype(BF16), _unpack_slot_rows(_as_rows(stage_ref[slot])), preferred_element_type=F32)
    y = _rms_norm(x1_ref[...] + moe, gf_ref[...])

    @pl.when(i < n_prompt_tiles)
    def _():
        yp_ref[...] = y

    @pl.when(i >= n_prompt_tiles)
    def _():
        ysm_ref[...] = y


def _t5_bucket(rel):
    half = NUM_BUCKETS // 2
    max_exact = half // 2
    ret = jnp.where(rel > 0, half, 0)
    n = jnp.abs(rel)
    nf = jnp.maximum(n, 1).astype(F32)
    large = max_exact + (jnp.log(nf / max_exact) / math.log(MAX_DISTANCE / max_exact)
                         * (half - max_exact)).astype(jnp.int32)
    large = jnp.minimum(large, half - 1)
    return ret + jnp.where(n < max_exact, n, large)


def _stacked_bias(table, sinks, q_pos, k_pos):
    nq, nk = q_pos.shape[0], k_pos.shape[0]
    bucket = _t5_bucket(k_pos[None, :] - q_pos[:, None])
    onehot = (bucket[:, :, None] == jnp.arange(NUM_BUCKETS)).astype(F32)
    bias = jnp.einsum('qkb,bh->hqk', onehot, table.astype(F32), precision=lax.Precision.HIGHEST)
    bias = bias.reshape(N_KV, REP * nq, nk)
    sink = jnp.repeat(sinks.astype(F32).reshape(N_KV, REP, 1), nq, axis=2).reshape(N_KV, REP * nq, 1)
    tail = jnp.full((N_KV, REP * nq, KEY_PAD - nk - 1), -jnp.inf, F32)
    return jnp.concatenate([bias, sink, tail], axis=-1)


def kernel(x_prompt, x_sample, cache_win_k, cache_win_v, norm1_g, w_in, attn_sinks, rel_bias_table, sgu_ln_g, sgu_ln_b, sgu_w, sgu_b, w_o_att, w_o_sgu, w_out, norm2_g, w_router, b_router, w_exp_in, b_exp_in, w_exp_out, b_exp_out, final_norm_g):
    batch, seq, _ = x_prompt.shape
    dec_batch, dec_seq, _ = x_sample.shape
    cache_rows = cache_win_k.shape[2]
    assert x_prompt.shape[2] == D_MODEL and w_in.shape == (1, D_MODEL, D_IN)
    assert seq % LAYER_TILE == 0 and LAYER_TILE % TOK_TILE == 0 and TOK_TILE % GMLP_CHUNK == 0
    assert LAYER_TILE >= WINDOW and LAYER_TILE % dec_seq == 0 and (dec_batch * dec_seq) % LAYER_TILE == 0
    assert dec_seq <= GMLP_CHUNK and GMLP_CHUNK % dec_seq == 0 and cache_rows == WINDOW
    T = TOK_TILE
    n_prompt = batch * seq
    n_sample = dec_batch * dec_seq
    n_tok = n_prompt + n_sample
    LT = LAYER_TILE
    sorts_per_step = LT // T
    tiles_per_seq = seq // LT
    n_psteps = n_prompt // LT
    n_ssteps = n_sample // LT
    n_ptiles = n_prompt // T
    n_tiles = n_tok // T
    seqs_per_tile = LT // dec_seq

    w_in_b = w_in[0].astype(BF16)
    w_q_b = w_in_b[:, :D_ATT].reshape(D_MODEL, N_KV, REP, HEAD_DIM).transpose(0, 2, 1, 3).reshape(
        D_MODEL, D_ATT)
    woa_b = w_o_att[0].reshape(N_KV, REP, HEAD_DIM, D_MODEL).transpose(1, 0, 2, 3).reshape(
        D_ATT, D_MODEL).astype(BF16)
    wos_b = w_o_sgu[0].astype(BF16)
    wout_b = w_out[0].astype(BF16)
    wr_b = jnp.transpose(w_router[0]).astype(BF16)
    br_col = b_router[0].astype(F32).reshape(N_EXPERTS, 1)
    g1 = norm1_g[0].reshape(1, D_MODEL)
    g2 = norm2_g[0].reshape(1, D_MODEL)
    gf = final_norm_g.reshape(1, D_MODEL)
    lng = sgu_ln_g[0].reshape(1, D_GMLP)
    lnb = sgu_ln_b[0].reshape(1, D_GMLP)
    tril = jnp.tril(jnp.ones((GMLP_CHUNK, GMLP_CHUNK), dtype=bool))
    sguw_p = jnp.where(tril[None], sgu_w[0], 0).astype(BF16)
    sgub_p = jnp.broadcast_to(sgu_b[0][:, :, None], (N_GROUPS, GMLP_CHUNK, GROUP_W)).astype(F32)
    reps = GMLP_CHUNK // dec_seq
    corner = jnp.where(tril[None, :dec_seq, :dec_seq], sgu_w[0][:, :dec_seq, :dec_seq], 0)
    sguw_s = jnp.einsum('ab,gij->gaibj', jnp.eye(reps, dtype=F32), corner).reshape(
        N_GROUPS, GMLP_CHUNK, GMLP_CHUNK).astype(BF16)
    sgub_s = jnp.broadcast_to(jnp.tile(sgu_b[0][:, :dec_seq], (1, reps))[:, :, None],
                              (N_GROUPS, GMLP_CHUNK, GROUP_W)).astype(F32)
    bias_p = _stacked_bias(rel_bias_table, attn_sinks[0], jnp.arange(CHUNK) + WINDOW, jnp.arange(BAND))
    bias_s = _stacked_bias(rel_bias_table, attn_sinks[0], cache_rows + jnp.arange(dec_seq),
                           jnp.arange(cache_rows + dec_seq))
    tri = jnp.triu(jnp.ones((T, T), F32), k=1).astype(BF16)
    low = jnp.tril(jnp.ones((N_EXPERTS, N_EXPERTS), F32), k=-1).astype(BF16)

    layer_out_shapes = (
        jax.ShapeDtypeStruct((n_tok, D_MODEL), F32),
        jax.ShapeDtypeStruct((n_tok, D_MODEL), BF16),
        jax.ShapeDtypeStruct((n_tiles, ROUTE_ROWS, T), F32),
        jax.ShapeDtypeStruct((n_tok, LANES), F32),
        jax.ShapeDtypeStruct((n_tiles, N_EXPERTS, LANES), F32),
    )
    shared_consts = (w_q_b, w_in_b)
    tail_consts_p = (sguw_p, sgub_p, lng, lnb, woa_b, wos_b, wout_b, g2, wr_b, br_col, tri, low)
    tail_consts_s = (sguw_s, sgub_s, lng, lnb, woa_b, wos_b, wout_b, g2, wr_b, br_col, tri, low)

    def tile_specs(tile_of):
        return [
            pl.BlockSpec((LT, D_MODEL), lambda *g: (tile_of(*g), 0)),
            pl.BlockSpec((LT, D_MODEL), lambda *g: (tile_of(*g), 0)),
            pl.BlockSpec((sorts_per_step, ROUTE_ROWS, T), lambda *g: (tile_of(*g), 0, 0)),
            pl.BlockSpec((LT, LANES), lambda *g: (tile_of(*g), 0)),
            pl.BlockSpec((sorts_per_step, N_EXPERTS, LANES), lambda *g: (tile_of(*g), 0, 0)),
        ]

    prompt_consts = (g1,) + shared_consts + (bias_p,) + tail_consts_p
    x1, h, route, routet, cnt, kvwin = pl.pallas_call(
        _prompt_layer_kernel,
        grid=(batch, tiles_per_seq),
        in_specs=[pl.BlockSpec((1, LT, D_MODEL), lambda b, j: (b, j, 0))]
                 + [_const_spec(c.shape) for c in prompt_consts],
        out_specs=tile_specs(lambda b, j: b * tiles_per_seq + j)
                  + [pl.BlockSpec((1, WINDOW, 2 * D_KV), lambda b, j: (b, 0, 0))],
        out_shape=layer_out_shapes + (jax.ShapeDtypeStruct((batch, WINDOW, 2 * D_KV), F32),),
        scratch_shapes=[pltpu.VMEM((LT + WINDOW, 2 * D_KV), BF16),
                        pltpu.VMEM((LT, D_ATT), BF16),
                        pltpu.VMEM((LT, D_GMLP), BF16)],
        compiler_params=pltpu.CompilerParams(dimension_semantics=("arbitrary", "arbitrary"),
                                             vmem_limit_bytes=VMEM_LIMIT),
        name="layer_prompt",
    )(x_prompt, *prompt_consts)

    xs_flat = x_sample.reshape(n_sample, D_MODEL)
    ck = cache_win_k[0].reshape(dec_batch, cache_rows, D_KV)
    cv = cache_win_v[0].reshape(dec_batch, cache_rows, D_KV)
    sample_consts = (g1,) + shared_consts + (bias_s,) + tail_consts_s
    any_spec = pl.BlockSpec(memory_space=pl.ANY)
    x1, h, route, routet, cnt, kvnew, vn_s = pl.pallas_call(
        _sample_layer_kernel,
        grid=(n_ssteps,),
        in_specs=[any_spec] * 5
                 + [pl.BlockSpec((LT, D_MODEL), lambda i: (i, 0)),
                    pl.BlockSpec((seqs_per_tile, cache_rows, D_KV), lambda i: (i, 0, 0)),
                    pl.BlockSpec((seqs_per_tile, cache_rows, D_KV), lambda i: (i, 0, 0))]
                 + [_const_spec(c.shape) for c in sample_consts],
        out_specs=tile_specs(lambda i: n_psteps + i)
                  + [pl.BlockSpec((LT, 2 * D_KV), lambda i: (i, 0)),
                     pl.BlockSpec((LT, D_GMLP), lambda i: (i, 0))],
        out_shape=layer_out_shapes + (jax.ShapeDtypeStruct((n_sample, 2 * D_KV), F32),
                                      jax.ShapeDtypeStruct((n_sample, D_GMLP), F32)),
        scratch_shapes=[pltpu.VMEM((LT, D_ATT), BF16), pltpu.VMEM((LT, D_GMLP), BF16)],
        input_output_aliases={0: 0, 1: 1, 2: 2, 3: 3, 4: 4},
        compiler_params=pltpu.CompilerParams(dimension_semantics=("arbitrary",),
                                             vmem_limit_bytes=VMEM_LIMIT),
        name="layer_sample",
    )(x1, h, route, routet, cnt, xs_flat, ck, cv, *sample_consts)

    blk_gran = SUB_BLOCK // GRAN
    counts =cnt[:, :, 0].astype(I32)
    seg_gran = (counts + (GRAN - 1)) // GRAN
    local_off = jnp.cumsum(seg_gran, axis=1) - seg_gran
    tot_gran = jnp.sum(seg_gran, axis=0)
    ptot_gran = (tot_gran + (blk_gran - 1)) // blk_gran * blk_gran
    pend_gran = jnp.cumsum(ptot_gran)
    gstart = pend_gran - ptot_gran
    global_off = gstart[None, :] + jnp.cumsum(seg_gran, axis=0) - seg_gran
    pad_n = ptot_gran - tot_gran
    pad_off = gstart + tot_gran
    n_rows = -(-(TOP_K * n_tok + n_tiles * N_EXPERTS * (GRAN - 1) + N_EXPERTS * (SUB_BLOCK - GRAN))
               // SUB_BLOCK) * SUB_BLOCK + ROW_BLOCK
    region_gran0 = gstart.astype(I32)
    region_subs = (ptot_gran // blk_gran).astype(I32)
    tile_gran = jnp.sum(seg_gran, axis=1).astype(I32)
    tile_big = jnp.any(seg_gran >= 4 * (FLAT_QUADS + 1), axis=1).astype(I32)
    seg_gran_f = seg_gran.reshape(-1).astype(I32)
    local_off_f = local_off.reshape(-1).astype(I32)
    global_off_f = global_off.reshape(-1).astype(I32)

    xs = pl.pallas_call(
        _dispatch_kernel,
        grid_spec=pltpu.PrefetchScalarGridSpec(
            num_scalar_prefetch=7,
            grid=(n_tiles,),
            in_specs=[pl.BlockSpec((1, ROUTE_ROWS, T), lambda i, *_: (i, 0, 0)),
                      pl.BlockSpec((T, D_MODEL), lambda i, *_: (i, 0))],
            out_specs=pl.BlockSpec(memory_space=pl.ANY),
            scratch_shapes=[pltpu.VMEM((2, LOCAL_ROWS // GRAN, GRAN, SLOT_WORDS), U32),
                            pltpu.VMEM((SUB_BLOCK // GRAN, GRAN, SLOT_WORDS), U32),
                            pltpu.SemaphoreType.DMA((3,))]),
        out_shape=jax.ShapeDtypeStruct((n_rows // GRAN, GRAN, SLOT_WORDS), U32),
        compiler_params=pltpu.CompilerParams(dimension_semantics=("arbitrary",),
                                             vmem_limit_bytes=VMEM_LIMIT),
        name="moe_dispatch",
    )(seg_gran_f, local_off_f, global_off_f, tile_gran, tile_big, pad_n.astype(I32), pad_off.astype(I32),
      route, h)

    ys = pl.pallas_call(
        _expert_kernel,
        grid_spec=pltpu.PrefetchScalarGridSpec(
            num_scalar_prefetch=2,
            grid=(N_EXPERTS,),
            in_specs=[pl.BlockSpec(memory_space=pl.ANY),
                      pl.BlockSpec((1, D_MODEL, 2 * D_FF), lambda e, *_: (e, 0, 0)),
                      pl.BlockSpec((1, 1, 2 * D_FF), lambda e, *_: (e, 0, 0)),
                      pl.BlockSpec((1, D_FF, D_MODEL), lambda e, *_: (e, 0, 0)),
                      pl.BlockSpec((1, 1, D_MODEL), lambda e, *_: (e, 0, 0))],
            out_specs=pl.BlockSpec(memory_space=pl.ANY),
            scratch_shapes=[pltpu.VMEM((D_MODEL, 2 * D_FF), BF16), pltpu.VMEM((D_FF, D_MODEL), BF16),
                            pltpu.VMEM((2, ROW_BLOCK // GRAN, GRAN, SLOT_WORDS), U32),
                            pltpu.VMEM((2, ROW_BLOCK // GRAN, GRAN, SLOT_WORDS), U32),
                            pltpu.SemaphoreType.DMA((2,)), pltpu.SemaphoreType.DMA((2,))]),
        out_shape=jax.ShapeDtypeStruct((n_rows // GRAN, GRAN, SLOT_WORDS), U32),
        compiler_params=pltpu.CompilerParams(dimension_semantics=("arbitrary",),
                                             vmem_limit_bytes=VMEM_LIMIT),
        name="moe_experts",
    )(region_gran0, region_subs, xs, w_exp_in[0], b_exp_in[0].reshape(N_EXPERTS, 1, 2 * D_FF),
      w_exp_out[0], b_exp_out[0].reshape(N_EXPERTS, 1, D_MODEL))

    y_p, y_s = pl.pallas_call(
        functools.partial(_combine_kernel, n_ptiles),
        grid_spec=pltpu.PrefetchScalarGridSpec(
            num_scalar_prefetch=5,
            grid=(n_tiles,),
            in_specs=[pl.BlockSpec((T, LANES), lambda i, *_: (i, 0)),
                      pl.BlockSpec((T, D_MODEL), lambda i, *_: (i, 0)),
                      pl.BlockSpec((1, D_MODEL), lambda i, *_: (0, 0)),
                      pl.BlockSpec(memory_space=pl.ANY)],
            out_specs=[pl.BlockSpec((T, D_MODEL), lambda i, *_: (jnp.minimum(i, n_ptiles - 1), 0)),
                       pl.BlockSpec((T, D_MODEL), lambda i, *_: (jnp.maximum(i - n_ptiles, 0), 0))],
            scratch_shapes=[pltpu.VMEM((2, LOCAL_ROWS // GRAN, GRAN, SLOT_WORDS), U32), pltpu.SemaphoreType.DMA((2,))]),
        out_shape=(jax.ShapeDtypeStruct((n_prompt, D_MODEL), F32),
                   jax.ShapeDtypeStruct((n_sample, D_MODEL), F32)),
        compiler_params=pltpu.CompilerParams(dimension_semantics=("arbitrary",),
                                             vmem_limit_bytes=VMEM_LIMIT),
        name="moe_combine",
    )(seg_gran_f, local_off_f, global_off_f, tile_gran, tile_big, routet, x1, gf, ys)

    y_prompt = y_p.reshape(batch, seq, D_MODEL)
    y_sample = y_s.reshape(dec_batch, dec_seq, D_MODEL)
    new_win_k_prompt = kvwin[:, :, :D_KV].reshape(1, batch, WINDOW, N_KV, HEAD_DIM)
    new_win_v_prompt = kvwin[:, :, D_KV:].reshape(1, batch, WINDOW, N_KV, HEAD_DIM)
    new_win_k_sample = kvnew[:, :D_KV].reshape(1, dec_batch, dec_seq, N_KV, HEAD_DIM)
    new_win_v_sample = kvnew[:, D_KV:].reshape(1, dec_batch, dec_seq, N_KV, HEAD_DIM)
    new_sgu_v_sample = vn_s.reshape(1, dec_batch, dec_seq, D_GMLP)
    return (y_prompt, y_sample, new_win_k_prompt, new_win_v_prompt, new_win_k_sample,
            new_win_v_sample, new_sgu_v_sample)
```

```python
import functools
import math

import numpy as np
import jax
import jax.numpy as jnp
from jax import lax
from jax.experimental import pallas as pl
from jax.experimental.pallas import tpu as pltpu

F32 = jnp.float32
BF16 = jnp.bfloat16
I32 = jnp.int32
U32 = jnp.uint32

D_MODEL = 1024
HEAD_DIM = 64
N_HEADS = 16
N_KV = 2
REP = N_HEADS // N_KV
CHUNK = 64
WINDOW = 128
BAND = WINDOW + CHUNK
KEY_PAD = 256
D_ATT = N_HEADS * HEAD_DIM
D_KV = N_KV * HEAD_DIM
NUM_BUCKETS = 32
MAX_DISTANCE = 128
GMLP_CHUNK = 128
D_GMLP = 1024
N_GROUPS = 4
GROUP_W = D_GMLP // N_GROUPS
N_EXPERTS = 32
TOP_K = 4
D_FF = 1024
SWIGLU_LIMIT = 7.0
SWIGLU_ALPHA = 1.702
NORM_EPS = 1e-5
D_IN = D_ATT + 2 * D_KV + 2 * D_GMLP + 2 * D_MODEL
COL_KV = D_ATT
COL_U = COL_KV + 2 * D_KV
COL_VG = COL_U + D_GMLP
COL_GA = COL_VG + D_GMLP
COL_GB = COL_GA + D_MODEL
SQRT_HALF = float(np.sqrt(0.5))

LANES = 128
WORD_SUBLANES = 8
VMEM_LIMIT = 56 * 1024 * 1024

TOK_TILE = 256
LAYER_TILE = 512
GRAN = WORD_SUBLANES
SLOT_WORDS = D_MODEL // 2
FLAT_QUADS = 1
LOCAL_ROWS = TOP_K * TOK_TILE + N_EXPERTS * GRAN
ROW_BLOCK = 512
SUB_BLOCK = 128
PAD_BITS = (SUB_BLOCK // GRAN - 1).bit_length()
TILE_BITS = (LOCAL_ROWS // GRAN).bit_length()
BLOCK_DMA_PRIORITY = 1
ROUTE_ROWS = 16


def _const_spec(shape):
    nd = len(shape)
    return pl.BlockSpec(shape, lambda *_: (0,) * nd, pipeline_mode=pl.Buffered(1))


def _rms_norm(x, g):
    ms = jnp.mean(x * x, axis=-1, keepdims=True)
    return x * lax.rsqrt(ms + NORM_EPS) * g


def _gelu(x):
    return 0.5 * x * (1.0 + lax.erf(x * SQRT_HALF))


def _attend(q, k, v, bias, valid):
    logits = lax.dot_general(q, k, (((1,), (1,)), ((), ())), preferred_element_type=F32) + bias
    if valid is not None:
        logits = jnp.where(valid, logits, -jnp.inf)
    m = jnp.max(logits, axis=-1, keepdims=True)
    p = jnp.exp(logits - m)
    den = jnp.sum(p, axis=-1, keepdims=True)
    w = (p * (1.0 / den)).astype(BF16)
    return jnp.dot(w, v, preferred_element_type=F32)


def _attention_rows(zq, row0, n_rows, k_ext, v_ext, bias_ref, valid, oatt_ref, heads_per_unit):
    lane = lax.broadcasted_iota(I32, (1, LANES), 1)
    for g in range(N_KV):
        in_group = (lane >= g * HEAD_DIM) & (lane < (g + 1) * HEAD_DIM)
        qmask = jnp.where(in_group, HEAD_DIM ** -0.5, 0.0).astype(BF16)
        for part in range(REP // heads_per_unit):
            tiles = range(part * heads_per_unit, (part + 1) * heads_per_unit)
            q = jnp.concatenate([zq[row0:row0 + n_rows, t * LANES:(t + 1) * LANES] * qmask for t in tiles],
                                axis=0)
            b0 = part * heads_per_unit * n_rows
            o = _attend(q, k_ext, v_ext, bias_ref[g, b0:b0 + heads_per_unit * n_rows, :], valid)
            for u, t in enumerate(tiles):
                c0 = t * LANES + g * HEAD_DIM
                oatt_ref[row0:row0 + n_rows, c0:c0 + HEAD_DIM] = (
                    o[u * n_rows:(u + 1) * n_rows, g * HEAD_DIM:(g + 1) * HEAD_DIM].astype(BF16))


def _layer_tail(x, xn_b, oatt_ref, osgu_ref, w_in_ref, sguw_ref, sgub_ref, lng_ref, lnb_ref,
                woa_ref, wos_ref, wout_ref, g2_ref, wr_ref, br_ref, tri_ref, low_ref,
                x1_ref, h_ref, route_ref, routet_ref, cnt_ref, vn_ref):
    T = x.shape[0]
    u = _gelu(jnp.dot(xn_b, w_in_ref[:, COL_U:COL_VG], preferred_element_type=F32))
    vg = _gelu(jnp.dot(xn_b, w_in_ref[:, COL_VG:COL_GA], preferred_element_type=F32))
    mu = jnp.mean(vg, axis=-1, keepdims=True)
    var = jnp.mean(jnp.square(vg - mu), axis=-1, keepdims=True)
    vn = (vg - mu) * lax.rsqrt(var + NORM_EPS) * lng_ref[...] + lnb_ref[...]
    if vn_ref is not None:
        vn_ref[...] = vn
    vn_b = vn.astype(BF16)
    for j in range(T // GMLP_CHUNK):
        rows = slice(j * GMLP_CHUNK, (j + 1) * GMLP_CHUNK)
        for g in range(N_GROUPS):
            cols = slice(g * GROUP_W, (g + 1) * GROUP_W)
            mixed = jnp.dot(sguw_ref[g], vn_b[rows, cols], preferred_element_type=F32) + sgub_ref[g]
            osgu_ref[rows, cols] = (u[rows, cols] * mixed).astype(BF16)

    ga = jax.nn.sigmoid(jnp.dot(xn_b, w_in_ref[:, COL_GA:COL_GB], preferred_element_type=F32))
    merged = ga * jnp.dot(oatt_ref[...], woa_ref[...], preferred_element_type=F32)
    gb = jax.nn.sigmoid(jnp.dot(xn_b, w_in_ref[:, COL_GB:D_IN], preferred_element_type=F32))
    merged = merged + gb * jnp.dot(osgu_ref[...], wos_ref[...], preferred_element_type=F32)
    x1 = x + jnp.dot(merged.astype(BF16), wout_ref[...], preferred_element_type=F32)
    x1_ref[...] = x1

    h_b = _rms_norm(x1, g2_ref[...]).astype(BF16)
    h_ref[...] = h_b
    lt = lax.dot_general(wr_ref[...], h_b, (((1,), (1,)), ((), ())),
                         preferred_element_type=F32) + br_ref[...]
    for s in range(T // TOK_TILE):
        rec, cnt_b = _route_sort_tile(lt[:, s * TOK_TILE:(s + 1) * TOK_TILE], tri_ref, low_ref)
        route_ref[s] = rec
        rec_pad = jnp.concatenate([rec, jnp.zeros((LANES - ROUTE_ROWS, TOK_TILE), F32)], axis=0)
        routet_ref[s * TOK_TILE:(s + 1) * TOK_TILE, :] = jnp.transpose(rec_pad)
        cnt_ref[s] = cnt_b


def _route_sort_tile(lt, tri_ref, low_ref):
    T = lt.shape[1]
    e_iota = lax.broadcasted_iota(I32, (N_EXPERTS, T), 0).astype(F32)
    cur = lt
    vals, idxs = [], []
    for _ in range(TOP_K):
        m = jnp.max(cur, axis=0, keepdims=True)
        ik = jnp.min(jnp.where(cur == m, e_iota, float(N_EXPERTS)), axis=0, keepdims=True)
        vals.append(m)
        idxs.append(ik)
        cur = jnp.where(e_iota == ik, -jnp.inf, cur)
    exps = [jnp.exp(v - vals[0]) for v in vals]
    den = exps[0] + exps[1] + exps[2] + exps[3]
    gates = [e / den for e in exps]

    onehot = jnp.zeros((N_EXPERTS, T), F32)
    for ik in idxs:
        onehot = onehot + jnp.where(e_iota == ik, 1.0, 0.0)
    rank = jnp.dot(onehot.astype(BF16), tri_ref[...], preferred_element_type=F32)
    cnt = jnp.sum(onehot, axis=1, keepdims=True)
    cnt_b = jnp.broadcast_to(cnt, (N_EXPERTS, LANES))
    gran = jnp.floor((cnt_b + (GRAN - 1)) * (1.0 / GRAN))
    off = jnp.dot(low_ref[...], gran.astype(BF16), preferred_element_type=F32) * GRAN
    base = off[:, 0:1] + rank
    poss = [jnp.sum(jnp.where(e_iota == ik, base, 0.0), axis=0, keepdims=True) for ik in idxs]

    rec = jnp.concatenate(idxs + poss + gates
                          + [jnp.zeros((ROUTE_ROWS - 3 * TOP_K, T), F32)], axis=0)
    return rec, cnt_b


def _prompt_layer_kernel(x_ref, g1_ref, wq_ref, w_in_ref, bias_ref, sguw_ref, sgub_ref, lng_ref,
                         lnb_ref, woa_ref, wos_ref, wout_ref, g2_ref, wr_ref, br_ref, tri_ref, low_ref,
                         x1_ref, h_ref, route_ref, routet_ref, cnt_ref, kvwin_ref,
                         kvx_ref, oatt_ref, osgu_ref):
    T = LAYER_TILE
    j = pl.program_id(1)
    x = x_ref[0]
    xn_b = _rms_norm(x, g1_ref[...]).astype(BF16)
    zq = jnp.dot(xn_b, wq_ref[...], preferred_element_type=F32).astype(BF16)
    zkv = jnp.dot(xn_b, w_in_ref[:, COL_KV:COL_U], preferred_element_type=F32)

    @pl.when(j == pl.num_programs(1) - 1)
    def _():
        kvwin_ref[0] = zkv[T - WINDOW:, :]

    @pl.when(j == 0)
    def _():
        kvx_ref[0:WINDOW, :] = jnp.zeros((WINDOW, 2 * D_KV), BF16)

    @pl.when(j > 0)
    def _():
        kvx_ref[0:WINDOW, :] = kvx_ref[T:T + WINDOW, :]

    kvx_ref[WINDOW:, :] = zkv.astype(BF16)

    col = lax.broadcasted_iota(I32, (1, KEY_PAD), 1)
    key_pad = jnp.zeros((KEY_PAD - BAND, 2 * D_KV), BF16)
    for c in range(T // CHUNK):
        r0 = c * CHUNK
        kvb = jnp.concatenate([kvx_ref[r0:r0 + BAND, :], key_pad], axis=0)
        valid = ((col + (j * T + r0 - WINDOW)) >= 0) | (col >= BAND) if r0 < WINDOW else None
        _attention_rows(zq, r0, CHUNK, kvb[:, :D_KV], kvb[:, D_KV:], bias_ref, valid, oatt_ref,
                        heads_per_unit=4)

    _layer_tail(x, xn_b, oatt_ref, osgu_ref, w_in_ref, sguw_ref, sgub_ref, lng_ref, lnb_ref,
                woa_ref, wos_ref, wout_ref, g2_ref, wr_ref, br_ref, tri_ref, low_ref,
                x1_ref, h_ref, route_ref, routet_ref, cnt_ref, None)


def _sample_layer_kernel(x1_in, h_in, route_in, routet_in, cnt_in,
                         x_ref, ck_ref, cv_ref, g1_ref, wq_ref, w_in_ref, bias_ref, sguw_ref, sgub_ref,
                         lng_ref, lnb_ref, woa_ref, wos_ref, wout_ref, g2_ref, wr_ref, br_ref, tri_ref,
                         low_ref,
                         x1_ref, h_ref, route_ref, routet_ref, cnt_ref, kvnew_ref, vn_ref,
                         oatt_ref, osgu_ref):
    del x1_in, h_in, route_in, routet_in, cnt_in
    T = LAYER_TILE
    nq = x_ref.shape[0] // ck_ref.shape[0]
    x = x_ref[...]
    xn_b = _rms_norm(x, g1_ref[...]).astype(BF16)
    zq = jnp.dot(xn_b, wq_ref[...], preferred_element_type=F32).astype(BF16)
    zkv = jnp.dot(xn_b, w_in_ref[:, COL_KV:COL_U], preferred_element_type=F32)
    kvnew_ref[...] = zkv
    zkv_b = zkv.astype(BF16)
    n_cache = ck_ref.shape[1]
    key_pad = jnp.zeros((KEY_PAD - n_cache - nq, D_KV), BF16)
    for b in range(T // nq):
        r0 = b * nq
        kk = jnp.concatenate([ck_ref[b].astype(BF16), zkv_b[r0:r0 + nq, :D_KV], key_pad], axis=0)
        vv = jnp.concatenate([cv_ref[b].astype(BF16), zkv_b[r0:r0 + nq, D_KV:], key_pad], axis=0)
        _attention_rows(zq, r0, nq, kk, vv, bias_ref, None, oatt_ref, heads_per_unit=REP)

    _layer_tail(x, xn_b, oatt_ref, osgu_ref, w_in_ref, sguw_ref, sgub_ref, lng_ref, lnb_ref,
                woa_ref, wos_ref, wout_ref, g2_ref, wr_ref, br_ref, tri_ref, low_ref,
                x1_ref, h_ref, route_ref, routet_ref, cnt_ref, vn_ref)


def _pack_slot_rows(x):
    lo = lax.bitcast_convert_type(x[:, :SLOT_WORDS].astype(BF16).astype(F32), U32)
    hi = lax.bitcast_convert_type(x[:, SLOT_WORDS:].astype(BF16).astype(F32), U32)
    return (lo >> 16) | hi


def _pack_slot_rows_exact(x):
    lo = lax.bitcast_convert_type(x[:, :SLOT_WORDS], U32)
    hi = lax.bitcast_convert_type(x[:, SLOT_WORDS:], U32)
    return (lo >> 16) | hi


def _unpack_slot_rows(w):
    lo = lax.bitcast_convert_type(w << 16, F32)
    hi = lax.bitcast_convert_type(w & jnp.uint32(0xFFFF0000), F32)
    return jnp.concatenate([lo, hi], axis=1).astype(BF16)


def _as_granules(x):
    return x.reshape(x.shape[0] // GRAN, GRAN, SLOT_WORDS)


def _as_rows(x):
    return x.reshape(x.shape[0] * GRAN, SLOT_WORDS)


def _segment_copies(n_gran, src_gran, dst_gran, bits, make_copy, act):
    for b in range(min(bits, 2)):
        @pl.when(((n_gran >> b) & 1) == 1)
        def _(b=b):
            done = n_gran & ((1 << b) - 1)
            act(make_copy(src_gran + done, dst_gran + done, 1 << b))

    if bits > 2:
        def quad(q, carry):
            done = (n_gran & 3) + 4 * q
            act(make_copy(src_gran + done, dst_gran + done, 4))
            return carry
        lax.fori_loop(0, n_gran >> 2, quad, 0)


def _start_tile_copies(tile, live, cnt_ref, loff_ref, goff_ref, big_ref, make_copy):
    for e in range(N_EXPERTS):
        t = tile * N_EXPERTS + e
        n = jnp.where(live, cnt_ref[t], 0)
        src, dst = loff_ref[t], goff_ref[t]
        for b in range(2):
            @pl.when(((n >> b) & 1) == 1)
            def _(b=b, n=n, src=src, dst=dst):
                done = n & ((1 << b) - 1)
                make_copy(src + done, dst + done, 1 << b).start()
        for q in range(FLAT_QUADS):
            @pl.when((n >> 2) > q)
            def _(q=q, n=n, src=src, dst=dst):
                done = (n & 3) + 4 * q
                make_copy(src + done, dst + done, 4).start()

    @pl.when(live & (big_ref[tile] > 0))
    def _():
        def rest(e, carry):
            t = tile * N_EXPERTS + e
            n, src, dst = cnt_ref[t], loff_ref[t], goff_ref[t]

            def quad(q, c):
                done = (n & 3) + 4 * q
                make_copy(src + done, dst + done, 4).start()
                return c
            lax.fori_loop(FLAT_QUADS, n >> 2, quad, 0)
            return carry
        lax.fori_loop(0, N_EXPERTS, rest, 0)


def _wait_granules(n_gran, bits, make_copy):
    for b in range(bits):
        @pl.when(((n_gran >> b) & 1) == 1)
        def _(b=b):
            make_copy(1 << b).wait()


def _dispatch_kernel(cnt_ref, loff_ref, goff_ref, tot_ref, big_ref, padn_ref, padoff_ref,
                     route_ref, h_ref, xs_hbm, stage_ref, zero_ref, sem):
    i = pl.program_id(0)
    last = pl.num_programs(0) - 1
    slot = lax.rem(i, 2)
    T = TOK_TILE

    def seg_copy(tile_slot):
        return lambda s, d, n: pltpu.make_async_copy(
            stage_ref.at[tile_slot, pl.ds(s, n)], xs_hbm.at[pl.ds(d, n)], sem.at[tile_slot])

    def wait_tile(tile, tile_slot):
        _wait_granules(tot_ref[tile], TILE_BITS, lambda n: pltpu.make_async_copy(
            stage_ref.at[tile_slot, pl.ds(0, n)], xs_hbm.at[pl.ds(0, n)], sem.at[tile_slot]))

    @pl.when(i > 1)
    def _():
        wait_tile(i - 2, slot)

    _start_tile_copies(jnp.maximum(i - 1, 0), i > 0, cnt_ref, loff_ref, goff_ref, big_ref, seg_copy(1 - slot))

    pos = route_ref[0, TOP_K:2 * TOP_K, :].astype(I32)
    r_iota = lax.broadcasted_iota(I32, (LOCAL_ROWS, T), 0)
    p = jnp.zeros((LOCAL_ROWS, T), F32)
    for k in range(TOP_K):
        p = jnp.where(r_iota == pos[k:k + 1, :], 1.0, p)
    stage_ref[slot] = _as_granules(
        _pack_slot_rows_exact(jnp.dot(p.astype(BF16), h_ref[...], preferred_element_type=F32)))

    @pl.when(i == last)
    def _():
        _start_tile_copies(i, i >= 0, cnt_ref, loff_ref, goff_ref, big_ref, seg_copy(slot))

        @pl.when(i > 0)
        def _():
            wait_tile(i - 1, 1 - slot)

        wait_tile(i, slot)
        zero_ref[...] = jnp.zeros(zero_ref.shape, U32)

        def pad_copy(s, d, n):
            return pltpu.make_async_copy(zero_ref.at[pl.ds(s, n)], xs_hbm.at[pl.ds(d, n)], sem.at[2])

        def for_pads(act):
            def body(e, carry):
                _segment_copies(padn_ref[e], 0, padoff_ref[e], PAD_BITS, pad_copy, act)
                return carry
            lax.fori_loop(0, N_EXPERTS, body, 0)

        for_pads(lambda cp: cp.start())
        for_pads(lambda cp: cp.wait())


def _expert_kernel(gran0_ref, nsub_ref, xs_hbm, w1_ref, b1_ref, w2_ref, b2_ref, ys_hbm,
                   w1b_ref, w2b_ref, xbuf_ref, ybuf_ref, sem_in, sem_out):
    e = pl.program_id(0)
    subs = ROW_BLOCK // SUB_BLOCK
    sub_gran = SUB_BLOCK // GRAN
    blk_gran = ROW_BLOCK // GRAN
    gran0 = gran0_ref[e]
    nsub = nsub_ref[e]
    nb_full = nsub // subs
    rem = nsub - nb_full * subs
    nb = nb_full + jnp.where(rem > 0, 1, 0)
    last_slot = lax.rem(nb + 1, 2)

    def in_copy(k, slot):
        return pltpu.make_async_copy(xs_hbm.at[pl.ds(gran0 + k * blk_gran, blk_gran)], xbuf_ref.at[slot],
                                     sem_in.at[slot])

    def out_copy(k, slot, m_sub):
        n = m_sub * sub_gran
        return pltpu.make_async_copy(ybuf_ref.at[slot, pl.ds(0, n)], ys_hbm.at[pl.ds(gran0 + k * blk_gran, n)],
                                     sem_out.at[slot])

    @pl.when(nb > 0)
    def _():
        in_copy(0, 0).start(priority=BLOCK_DMA_PRIORITY)

    w1b_ref[...] = w1_ref[0].astype(BF16)
    w2b_ref[...] = w2_ref[0].astype(BF16)

    def process(k, m_sub):
        n = m_sub * sub_gran
        slot = lax.rem(k, 2)

        @pl.when(k + 1 < nb)
        def _():
            in_copy(k + 1, 1 - slot).start(priority=BLOCK_DMA_PRIORITY)

        in_copy(k, slot).wait()

        @pl.when(k >= 2)
        def _():
            out_copy(k - 2, slot, subs).wait()

        x = _unpack_slot_rows(_as_rows(xbuf_ref[slot, 0:n]))
        h1 = jnp.dot(x, w1b_ref[...], preferred_element_type=F32) + b1_ref[0]
        gate = jnp.minimum(h1[:, :D_FF], SWIGLU_LIMIT)
        up = jnp.clip(h1[:, D_FF:], -SWIGLU_LIMIT, SWIGLU_LIMIT)
        act = gate * jax.nn.sigmoid(SWIGLU_ALPHA * gate) * (up + 1.0)
        y = jnp.dot(act.astype(BF16), w2b_ref[...], preferred_element_type=F32) + b2_ref[0]
        ybuf_ref[slot, 0:n] = _as_granules(_pack_slot_rows(y))
        out_copy(k, slot, m_sub).start(priority=BLOCK_DMA_PRIORITY)

    def full_block(k, carry):
        process(k, subs)
        return carry

    lax.fori_loop(0, nb_full, full_block, 0)

    for m_sub in range(1, subs):
        @pl.when(rem == m_sub)
        def _(m_sub=m_sub):
            process(nb_full, m_sub)

    @pl.when(nb >= 2)
    def _():
        out_copy(nb - 2, 1 - last_slot, subs).wait()

    @pl.when((nb >= 1) & (rem == 0))
    def _():
        out_copy(nb - 1, last_slot, subs).wait()

    for m_sub in range(1, subs):
        @pl.when(rem == m_sub)
        def _(m_sub=m_sub):
            out_copy(nb - 1, last_slot, m_sub).wait()


def _combine_kernel(n_prompt_tiles, cnt_ref, loff_ref, goff_ref, tot_ref, big_ref,
                    routet_ref, x1_ref, gf_ref, ys_hbm, yp_ref, ysm_ref, stage_ref, sem):
    i = pl.program_id(0)
    n_steps = pl.num_programs(0)
    slot = lax.rem(i, 2)
    T = TOK_TILE

    def seg_copy(tile_slot):
        return lambda s, d, n: pltpu.make_async_copy(
            ys_hbm.at[pl.ds(d, n)], stage_ref.at[tile_slot, pl.ds(s, n)], sem.at[tile_slot])

    @pl.when(i == 0)
    def _():
        stage_ref[...] = jnp.zeros(stage_ref.shape, U32)
        _start_tile_copies(i, i >= 0, cnt_ref, loff_ref, goff_ref, big_ref, seg_copy(slot))

    _start_tile_copies(jnp.minimum(i + 1, n_steps - 1), i + 1 < n_steps, cnt_ref, loff_ref, goff_ref, big_ref,
                       seg_copy(1 - slot))

    rt = routet_ref[...]
    l_iota = lax.broadcasted_iota(I32, (T, LOCAL_ROWS), 1)
    pw = jnp.zeros((T, LOCAL_ROWS), F32)
    for k in range(TOP_K):
        pos_k = rt[:, TOP_K + k:TOP_K + k + 1].astype(I32)
        pw = jnp.where(l_iota == pos_k, rt[:, 2 * TOP_K + k:2 * TOP_K + k + 1], pw)

    _wait_granules(tot_ref[i], TILE_BITS, lambda n: pltpu.make_async_copy(
        ys_hbm.at[pl.ds(0, n)], stage_ref.at[slot, pl.ds(0, n)], sem.at[slot]))
    moe = jnp.dot(pw.astype(BF16), _unpack_slot_rows(_as_rows(stage_ref[slot])), preferred_element_type=F32)
    y = _rms_norm(x1_ref[...] + moe, gf_ref[...])

    @pl.when(i < n_prompt_tiles)
    def _():
        yp_ref[...] = y

    @pl.when(i >= n_prompt_tiles)
    def _():
        ysm_ref[...] = y


def _t5_bucket(rel):
    half = NUM_BUCKETS // 2
    max_exact = half // 2
    ret = jnp.where(rel > 0, half, 0)
    n = jnp.abs(rel)
    nf = jnp.maximum(n, 1).astype(F32)
    large = max_exact + (jnp.log(nf / max_exact) / math.log(MAX_DISTANCE / max_exact)
                         * (half - max_exact)).astype(jnp.int32)
    large = jnp.minimum(large, half - 1)
    return ret + jnp.where(n < max_exact, n, large)


def _stacked_bias(table, sinks, q_pos, k_pos):
    nq, nk = q_pos.shape[0], k_pos.shape[0]
    bucket = _t5_bucket(k_pos[None, :] - q_pos[:, None])
    onehot = (bucket[:, :, None] == jnp.arange(NUM_BUCKETS)).astype(F32)
    bias = jnp.einsum('qkb,bh->hqk', onehot, table.astype(F32), precision=lax.Precision.HIGHEST)
    bias = bias.reshape(N_KV, REP * nq, nk)
    sink = jnp.repeat(sinks.astype(F32).reshape(N_KV, REP, 1), nq, axis=2).reshape(N_KV, REP * nq, 1)
    tail = jnp.full((N_KV, REP * nq, KEY_PAD - nk - 1), -jnp.inf, F32)
    return jnp.concatenate([bias, sink, tail], axis=-1)


def kernel(x_prompt, x_sample, cache_win_k, cache_win_v, norm1_g, w_in, attn_sinks, rel_bias_table, sgu_ln_g, sgu_ln_b, sgu_w, sgu_b, w_o_att, w_o_sgu, w_out, norm2_g, w_router, b_router, w_exp_in, b_exp_in, w_exp_out, b_exp_out, final_norm_g):
    batch, seq, _ = x_prompt.shape
    dec_batch, dec_seq, _ = x_sample.shape
    cache_rows = cache_win_k.shape[2]
    assert x_prompt.shape[2] == D_MODEL and w_in.shape == (1, D_MODEL, D_IN)
    assert seq % LAYER_TILE == 0 and LAYER_TILE % TOK_TILE == 0 and TOK_TILE % GMLP_CHUNK == 0
    assert LAYER_TILE >= WINDOW and LAYER_TILE % dec_seq == 0 and (dec_batch * dec_seq) % LAYER_TILE == 0
    assert dec_seq <= GMLP_CHUNK and GMLP_CHUNK % dec_seq == 0 and cache_rows == WINDOW
    T = TOK_TILE
    n_prompt = batch * seq
    n_sample = dec_batch * dec_seq
    n_tok = n_prompt + n_sample
    LT = LAYER_TILE
    sorts_per_step = LT // T
    tiles_per_seq = seq // LT
    n_psteps = n_prompt // LT
    n_ssteps = n_sample // LT
    n_ptiles = n_prompt // T
    n_tiles = n_tok // T
    seqs_per_tile = LT // dec_seq

    w_in_b = w_in[0].astype(BF16)
    w_q_b = w_in_b[:, :D_ATT].reshape(D_MODEL, N_KV, REP, HEAD_DIM).transpose(0, 2, 1, 3).reshape(
        D_MODEL, D_ATT)
    woa_b = w_o_att[0].reshape(N_KV, REP, HEAD_DIM, D_MODEL).transpose(1, 0, 2, 3).reshape(
        D_ATT, D_MODEL).astype(BF16)
    wos_b = w_o_sgu[0].astype(BF16)
    wout_b = w_out[0].astype(BF16)
    wr_b = jnp.transpose(w_router[0]).astype(BF16)
    br_col = b_router[0].astype(F32).reshape(N_EXPERTS, 1)
    g1 = norm1_g[0].reshape(1, D_MODEL)
    g2 = norm2_g[0].reshape(1, D_MODEL)
    gf = final_norm_g.reshape(1, D_MODEL)
    lng = sgu_ln_g[0].reshape(1, D_GMLP)
    lnb = sgu_ln_b[0].reshape(1, D_GMLP)
    tril = jnp.tril(jnp.ones((GMLP_CHUNK, GMLP_CHUNK), dtype=bool))
    sguw_p = jnp.where(tril[None], sgu_w[0], 0).astype(BF16)
    sgub_p = jnp.broadcast_to(sgu_b[0][:, :, None], (N_GROUPS, GMLP_CHUNK, GROUP_W)).astype(F32)
    reps = GMLP_CHUNK // dec_seq
    corner = jnp.where(tril[None, :dec_seq, :dec_seq], sgu_w[0][:, :dec_seq, :dec_seq], 0)
    sguw_s = jnp.einsum('ab,gij->gaibj', jnp.eye(reps, dtype=F32), corner).reshape(
        N_GROUPS, GMLP_CHUNK, GMLP_CHUNK).astype(BF16)
    sgub_s = jnp.broadcast_to(jnp.tile(sgu_b[0][:, :dec_seq], (1, reps))[:, :, None],
                              (N_GROUPS, GMLP_CHUNK, GROUP_W)).astype(F32)
    bias_p = _stacked_bias(rel_bias_table, attn_sinks[0], jnp.arange(CHUNK) + WINDOW, jnp.arange(BAND))
    bias_s = _stacked_bias(rel_bias_table, attn_sinks[0], cache_rows + jnp.arange(dec_seq),
                           jnp.arange(cache_rows + dec_seq))
    tri = jnp.triu(jnp.ones((T, T), F32), k=1).astype(BF16)
    low = jnp.tril(jnp.ones((N_EXPERTS, N_EXPERTS), F32), k=-1).astype(BF16)

    layer_out_shapes = (
        jax.ShapeDtypeStruct((n_tok, D_MODEL), F32),
        jax.ShapeDtypeStruct((n_tok, D_MODEL), BF16),
        jax.ShapeDtypeStruct((n_tiles, ROUTE_ROWS, T), F32),
        jax.ShapeDtypeStruct((n_tok, LANES), F32),
        jax.ShapeDtypeStruct((n_tiles, N_EXPERTS, LANES), F32),
    )
    shared_consts = (w_q_b, w_in_b)
    tail_consts_p = (sguw_p, sgub_p, lng, lnb, woa_b, wos_b, wout_b, g2, wr_b, br_col, tri, low)
    tail_consts_s = (sguw_s, sgub_s, lng, lnb, woa_b, wos_b, wout_b, g2, wr_b, br_col, tri, low)

    def tile_specs(tile_of):
        return [
            pl.BlockSpec((LT, D_MODEL), lambda *g: (tile_of(*g), 0)),
            pl.BlockSpec((LT, D_MODEL), lambda *g: (tile_of(*g), 0)),
            pl.BlockSpec((sorts_per_step, ROUTE_ROWS, T), lambda *g: (tile_of(*g), 0, 0)),
            pl.BlockSpec((LT, LANES), lambda *g: (tile_of(*g), 0)),
            pl.BlockSpec((sorts_per_step, N_EXPERTS, LANES), lambda *g: (tile_of(*g), 0, 0)),
        ]

    prompt_consts = (g1,) + shared_consts + (bias_p,) + tail_consts_p
    x1, h, route, routet, cnt, kvwin = pl.pallas_call(
        _prompt_layer_kernel,
        grid=(batch, tiles_per_seq),
        in_specs=[pl.BlockSpec((1, LT, D_MODEL), lambda b, j: (b, j, 0))]
                 + [_const_spec(c.shape) for c in prompt_consts],
        out_specs=tile_specs(lambda b, j: b * tiles_per_seq + j)
                  + [pl.BlockSpec((1, WINDOW, 2 * D_KV), lambda b, j: (b, 0, 0))],
        out_shape=layer_out_shapes + (jax.ShapeDtypeStruct((batch, WINDOW, 2 * D_KV), F32),),
        scratch_shapes=[pltpu.VMEM((LT + WINDOW, 2 * D_KV), BF16),
                        pltpu.VMEM((LT, D_ATT), BF16),
                        pltpu.VMEM((LT, D_GMLP), BF16)],
        compiler_params=pltpu.CompilerParams(dimension_semantics=("arbitrary", "arbitrary"),
                                             vmem_limit_bytes=VMEM_LIMIT),
        name="layer_prompt",
    )(x_prompt, *prompt_consts)

    xs_flat = x_sample.reshape(n_sample, D_MODEL)
    ck = cache_win_k[0].reshape(dec_batch, cache_rows, D_KV)
    cv = cache_win_v[0].reshape(dec_batch, cache_rows, D_KV)
    sample_consts = (g1,) + shared_consts + (bias_s,) + tail_consts_s
    any_spec = pl.BlockSpec(memory_space=pl.ANY)
    x1, h, route, routet, cnt, kvnew, vn_s = pl.pallas_call(
        _sample_layer_kernel,
        grid=(n_ssteps,),
        in_specs=[any_spec] * 5
                 + [pl.BlockSpec((LT, D_MODEL), lambda i: (i, 0)),
                    pl.BlockSpec((seqs_per_tile, cache_rows, D_KV), lambda i: (i, 0, 0)),
                    pl.BlockSpec((seqs_per_tile, cache_rows, D_KV), lambda i: (i, 0, 0))]
                 + [_const_spec(c.shape) for c in sample_consts],
        out_specs=tile_specs(lambda i: n_psteps + i)
                  + [pl.BlockSpec((LT, 2 * D_KV), lambda i: (i, 0)),
                     pl.BlockSpec((LT, D_GMLP), lambda i: (i, 0))],
        out_shape=layer_out_shapes + (jax.ShapeDtypeStruct((n_sample, 2 * D_KV), F32),
                                      jax.ShapeDtypeStruct((n_sample, D_GMLP), F32)),
        scratch_shapes=[pltpu.VMEM((LT, D_ATT), BF16), pltpu.VMEM((LT, D_GMLP), BF16)],
        input_output_aliases={0: 0, 1: 1, 2: 2, 3: 3, 4: 4},
        compiler_params=pltpu.CompilerParams(dimension_semantics=("arbitrary",),
                                             vmem_limit_bytes=VMEM_LIMIT),
        name="layer_sample",
    )(x1, h, route, routet, cnt, xs_flat, ck, cv, *sample_consts)

    blk_gran = SUB_BLOCK // GRAN
    counts =cnt[:, :, 0].astype(I32)
    seg_gran = (counts + (GRAN - 1)) // GRAN
    local_off = jnp.cumsum(seg_gran, axis=1) - seg_gran
    tot_gran = jnp.sum(seg_gran, axis=0)
    ptot_gran = (tot_gran + (blk_gran - 1)) // blk_gran * blk_gran
    pend_gran = jnp.cumsum(ptot_gran)
    gstart = pend_gran - ptot_gran
    global_off = gstart[None, :] + jnp.cumsum(seg_gran, axis=0) - seg_gran
    pad_n = ptot_gran - tot_gran
    pad_off = gstart + tot_gran
    n_rows = -(-(TOP_K * n_tok + n_tiles * N_EXPERTS * (GRAN - 1) + N_EXPERTS * (SUB_BLOCK - GRAN))
               // SUB_BLOCK) * SUB_BLOCK + ROW_BLOCK
    region_gran0 = gstart.astype(I32)
    region_subs = (ptot_gran // blk_gran).astype(I32)
    tile_gran = jnp.sum(seg_gran, axis=1).astype(I32)
    tile_big = jnp.any(seg_gran >= 4 * (FLAT_QUADS + 1), axis=1).astype(I32)
    seg_gran_f = seg_gran.reshape(-1).astype(I32)
    local_off_f = local_off.reshape(-1).astype(I32)
    global_off_f = global_off.reshape(-1).astype(I32)

    xs = pl.pallas_call(
        _dispatch_kernel,
        grid_spec=pltpu.PrefetchScalarGridSpec(
            num_scalar_prefetch=7,
            grid=(n_tiles,),
            in_specs=[pl.BlockSpec((1, ROUTE_ROWS, T), lambda i, *_: (i, 0, 0)),
                      pl.BlockSpec((T, D_MODEL), lambda i, *_: (i, 0))],
            out_specs=pl.BlockSpec(memory_space=pl.ANY),
            scratch_shapes=[pltpu.VMEM((2, LOCAL_ROWS // GRAN, GRAN, SLOT_WORDS), U32),
                            pltpu.VMEM((SUB_BLOCK // GRAN, GRAN, SLOT_WORDS), U32),
                            pltpu.SemaphoreType.DMA((3,))]),
        out_shape=jax.ShapeDtypeStruct((n_rows // GRAN, GRAN, SLOT_WORDS), U32),
        compiler_params=pltpu.CompilerParams(dimension_semantics=("arbitrary",),
                                             vmem_limit_bytes=VMEM_LIMIT),
        name="moe_dispatch",
    )(seg_gran_f, local_off_f, global_off_f, tile_gran, tile_big, pad_n.astype(I32), pad_off.astype(I32),
      route, h)

    ys = pl.pallas_call(
        _expert_kernel,
        grid_spec=pltpu.PrefetchScalarGridSpec(
            num_scalar_prefetch=2,
            grid=(N_EXPERTS,),
            in_specs=[pl.BlockSpec(memory_space=pl.ANY),
                      pl.BlockSpec((1, D_MODEL, 2 * D_FF), lambda e, *_: (e, 0, 0)),
                      pl.BlockSpec((1, 1, 2 * D_FF), lambda e, *_: (e, 0, 0)),
                      pl.BlockSpec((1, D_FF, D_MODEL), lambda e, *_: (e, 0, 0)),
                      pl.BlockSpec((1, 1, D_MODEL), lambda e, *_: (e, 0, 0))],
            out_specs=pl.BlockSpec(memory_space=pl.ANY),
            scratch_shapes=[pltpu.VMEM((D_MODEL, 2 * D_FF), BF16), pltpu.VMEM((D_FF, D_MODEL), BF16),
                            pltpu.VMEM((2, ROW_BLOCK // GRAN, GRAN, SLOT_WORDS), U32),
                            pltpu.VMEM((2, ROW_BLOCK // GRAN, GRAN, SLOT_WORDS), U32),
                            pltpu.SemaphoreType.DMA((2,)), pltpu.SemaphoreType.DMA((2,))]),
        out_shape=jax.ShapeDtypeStruct((n_rows // GRAN, GRAN, SLOT_WORDS), U32),
        compiler_params=pltpu.CompilerParams(dimension_semantics=("arbitrary",),
                                             vmem_limit_bytes=VMEM_LIMIT),
        name="moe_experts",
    )(region_gran0, region_subs, xs, w_exp_in[0], b_exp_in[0].reshape(N_EXPERTS, 1, 2 * D_FF),
      w_exp_out[0], b_exp_out[0].reshape(N_EXPERTS, 1, D_MODEL))

    y_p, y_s = pl.pallas_call(
        functools.partial(_combine_kernel, n_ptiles),
        grid_spec=pltpu.PrefetchScalarGridSpec(
            num_scalar_prefetch=5,
            grid=(n_tiles,),
            in_specs=[pl.BlockSpec((T, LANES), lambda i, *_: (i, 0)),
                      pl.BlockSpec((T, D_MODEL), lambda i, *_: (i, 0)),
                      pl.BlockSpec((1, D_MODEL), lambda i, *_: (0, 0)),
                      pl.BlockSpec(memory_space=pl.ANY)],
            out_specs=[pl.BlockSpec((T, D_MODEL), lambda i, *_: (jnp.minimum(i, n_ptiles - 1), 0)),
                       pl.BlockSpec((T, D_MODEL), lambda i, *_: (jnp.maximum(i - n_ptiles, 0), 0))],
            scratch_shapes=[pltpu.VMEM((2, LOCAL_ROWS // GRAN, GRAN, SLOT_WORDS), U32), pltpu.SemaphoreType.DMA((2,))]),
        out_shape=(jax.ShapeDtypeStruct((n_prompt, D_MODEL), F32),
                   jax.ShapeDtypeStruct((n_sample, D_MODEL), F32)),
        compiler_params=pltpu.CompilerParams(dimension_semantics=("arbitrary",),
                                             vmem_limit_bytes=VMEM_LIMIT),
        name="moe_combine",
    )(seg_gran_f, local_off_f, global_off_f, tile_gran, tile_big, routet, x1, gf, ys)

    y_prompt = y_p.reshape(batch, seq, D_MODEL)
    y_sample = y_s.reshape(dec_batch, dec_seq, D_MODEL)
    new_win_k_prompt = kvwin[:, :, :D_KV].reshape(1, batch, WINDOW, N_KV, HEAD_DIM)
    new_win_v_prompt = kvwin[:, :, D_KV:].reshape(1, batch, WINDOW, N_KV, HEAD_DIM)
    new_win_k_sample = kvnew[:, :D_KV].reshape(1, dec_batch, dec_seq, N_KV, HEAD_DIM)
    new_win_v_sample = kvnew[:, D_KV:].reshape(1, dec_batch, dec_seq, N_KV, HEAD_DIM)
    new_sgu_v_sample = vn_s.reshape(1, dec_batch, dec_seq, D_GMLP)
    return (y_prompt, y_sample, new_win_k_prompt, new_win_v_prompt, new_win_k_sample,
            new_win_v_sample, new_sgu_v_sample)
```

```python
import functools
import math

import numpy as np
import jax
import jax.numpy as jnp
from jax import lax
from jax.experimental import pallas as pl
from jax.experimental.pallas import tpu as pltpu

F32 = jnp.float32
BF16 = jnp.bfloat16
I32 = jnp.int32
U32 = jnp.uint32

D_MODEL = 1024
HEAD_DIM = 64
N_HEADS = 16
N_KV = 2
REP = N_HEADS // N_KV
CHUNK = 64
WINDOW = 128
BAND = WINDOW + CHUNK
KEY_PAD = 256
D_ATT = N_HEADS * HEAD_DIM
D_KV = N_KV * HEAD_DIM
NUM_BUCKETS = 32
MAX_DISTANCE = 128
GMLP_CHUNK = 128
D_GMLP = 1024
N_GROUPS = 4
GROUP_W = D_GMLP // N_GROUPS
N_EXPERTS = 32
TOP_K = 4
D_FF = 1024
SWIGLU_LIMIT = 7.0
SWIGLU_ALPHA = 1.702
NORM_EPS = 1e-5
D_IN = D_ATT + 2 * D_KV + 2 * D_GMLP + 2 * D_MODEL
COL_KV = D_ATT
COL_U = COL_KV + 2 * D_KV
COL_VG = COL_U + D_GMLP
COL_GA = COL_VG + D_GMLP
COL_GB = COL_GA + D_MODEL
SQRT_HALF = float(np.sqrt(0.5))

LANES = 128
WORD_SUBLANES = 8
VMEM_LIMIT = 56 * 1024 * 1024

TOK_TILE = 256
LAYER_TILE = 512
GRAN = WORD_SUBLANES
SLOT_WORDS = D_MODEL // 2
FLAT_QUADS = 1
LOCAL_ROWS = TOP_K * TOK_TILE + N_EXPERTS * GRAN
ROW_BLOCK = 512
SUB_BLOCK = 128
PAD_BITS = (SUB_BLOCK // GRAN - 1).bit_length()
TILE_BITS = (LOCAL_ROWS // GRAN).bit_length()
BLOCK_DMA_PRIORITY = 1
ROUTE_ROWS = 16


def _const_spec(shape):
    nd = len(shape)
    return pl.BlockSpec(shape, lambda *_: (0,) * nd, pipeline_mode=pl.Buffered(1))


def _rms_norm(x, g):
    ms = jnp.mean(x * x, axis=-1, keepdims=True)
    return x * lax.rsqrt(ms + NORM_EPS) * g


def _gelu(x):
    return 0.5 * x * (1.0 + lax.erf(x * SQRT_HALF))


def _attend(q, k, v, bias, valid):
    logits = lax.dot_general(q, k, (((1,), (1,)), ((), ())), preferred_element_type=F32) + bias
    if valid is not None:
        logits = jnp.where(valid, logits, -jnp.inf)
    m = jnp.max(logits, axis=-1, keepdims=True)
    p = jnp.exp(logits - m)
    den = jnp.sum(p, axis=-1, keepdims=True)
    w = (p * (1.0 / den)).astype(BF16)
    return jnp.dot(w, v, preferred_element_type=F32)


def _attention_rows(zq, row0, n_rows, k_ext, v_ext, bias_ref, valid, oatt_ref, heads_per_unit):
    lane = lax.broadcasted_iota(I32, (1, LANES), 1)
    for g in range(N_KV):
        in_group = (lane >= g * HEAD_DIM) & (lane < (g + 1) * HEAD_DIM)
        qmask = jnp.where(in_group, HEAD_DIM ** -0.5, 0.0).astype(BF16)
        for part in range(REP // heads_per_unit):
            tiles = range(part * heads_per_unit, (part + 1) * heads_per_unit)
            q = jnp.concatenate([zq[row0:row0 + n_rows, t * LANES:(t + 1) * LANES] * qmask for t in tiles],
                                axis=0)
            b0 = part * heads_per_unit * n_rows
            o = _attend(q, k_ext, v_ext, bias_ref[g, b0:b0 + heads_per_unit * n_rows, :], valid)
            for u, t in enumerate(tiles):
                c0 = t * LANES + g * HEAD_DIM
                oatt_ref[row0:row0 + n_rows, c0:c0 + HEAD_DIM] = (
                    o[u * n_rows:(u + 1) * n_rows, g * HEAD_DIM:(g + 1) * HEAD_DIM].astype(BF16))


def _layer_tail(x, xn_b, oatt_ref, osgu_ref, w_in_ref, sguw_ref, sgub_ref, lng_ref, lnb_ref,
                woa_ref, wos_ref, wout_ref, g2_ref, wr_ref, br_ref, tri_ref, low_ref,
                x1_ref, h_ref, route_ref, routet_ref, cnt_ref, vn_ref):
    T = x.shape[0]
    u = _gelu(jnp.dot(xn_b, w_in_ref[:, COL_U:COL_VG], preferred_element_type=F32))
    vg = _gelu(jnp.dot(xn_b, w_in_ref[:, COL_VG:COL_GA], preferred_element_type=F32))
    mu = jnp.mean(vg, axis=-1, keepdims=True)
    var = jnp.mean(jnp.square(vg - mu), axis=-1, keepdims=True)
    vn = (vg - mu) * lax.rsqrt(var + NORM_EPS) * lng_ref[...] + lnb_ref[...]
    if vn_ref is not None:
        vn_ref[...] = vn
    vn_b = vn.astype(BF16)
    for j in range(T // GMLP_CHUNK):
        rows = slice(j * GMLP_CHUNK, (j + 1) * GMLP_CHUNK)
        for g in range(N_GROUPS):
            cols = slice(g * GROUP_W, (g + 1) * GROUP_W)
            mixed = jnp.dot(sguw_ref[g], vn_b[rows, cols], preferred_element_type=F32) + sgub_ref[g]
            osgu_ref[rows, cols] = (u[rows, cols] * mixed).astype(BF16)

    ga = jax.nn.sigmoid(jnp.dot(xn_b, w_in_ref[:, COL_GA:COL_GB], preferred_element_type=F32))
    merged = ga * jnp.dot(oatt_ref[...], woa_ref[...], preferred_element_type=F32)
    gb = jax.nn.sigmoid(jnp.dot(xn_b, w_in_ref[:, COL_GB:D_IN], preferred_element_type=F32))
    merged = merged + gb * jnp.dot(osgu_ref[...], wos_ref[...], preferred_element_type=F32)
    x1 = x + jnp.dot(merged.astype(BF16), wout_ref[...], preferred_element_type=F32)
    x1_ref[...] = x1

    h_b = _rms_norm(x1, g2_ref[...]).astype(BF16)
    h_ref[...] = h_b
    lt = lax.dot_general(wr_ref[...], h_b, (((1,), (1,)), ((), ())),
                         preferred_element_type=F32) + br_ref[...]
    for s in range(T // TOK_TILE):
        rec, cnt_b = _route_sort_tile(lt[:, s * TOK_TILE:(s + 1) * TOK_TILE], tri_ref, low_ref)
        route_ref[s] = rec
        rec_pad = jnp.concatenate([rec, jnp.zeros((LANES - ROUTE_ROWS, TOK_TILE), F32)], axis=0)
        routet_ref[s * TOK_TILE:(s + 1) * TOK_TILE, :] = jnp.transpose(rec_pad)
        cnt_ref[s] = cnt_b


def _route_sort_tile(lt, tri_ref, low_ref):
    T = lt.shape[1]
    e_iota = lax.broadcasted_iota(I32, (N_EXPERTS, T), 0).astype(F32)
    cur = lt
    vals, idxs = [], []
    for _ in range(TOP_K):
        m = jnp.max(cur, axis=0, keepdims=True)
        ik = jnp.min(jnp.where(cur == m, e_iota, float(N_EXPERTS)), axis=0, keepdims=True)
        vals.append(m)
        idxs.append(ik)
        cur = jnp.where(e_iota == ik, -jnp.inf, cur)
    exps = [jnp.exp(v - vals[0]) for v in vals]
    den = exps[0] + exps[1] + exps[2] + exps[3]
    gates = [e / den for e in exps]

    onehot = jnp.zeros((N_EXPERTS, T), F32)
    for ik in idxs:
        onehot = onehot + jnp.where(e_iota == ik, 1.0, 0.0)
    rank = jnp.dot(onehot.astype(BF16), tri_ref[...], preferred_element_type=F32)
    cnt = jnp.sum(onehot, axis=1, keepdims=True)
    cnt_b = jnp.broadcast_to(cnt, (N_EXPERTS, LANES))
    gran = jnp.floor((cnt_b + (GRAN - 1)) * (1.0 / GRAN))
    off = jnp.dot(low_ref[...], gran.astype(BF16), preferred_element_type=F32) * GRAN
    base = off[:, 0:1] + rank
    poss = [jnp.sum(jnp.where(e_iota == ik, base, 0.0), axis=0, keepdims=True) for ik in idxs]

    rec = jnp.concatenate(idxs + poss + gates
                          + [jnp.zeros((ROUTE_ROWS - 3 * TOP_K, T), F32)], axis=0)
    return rec, cnt_b


def _prompt_layer_kernel(x_ref, g1_ref, wq_ref, w_in_ref, bias_ref, sguw_ref, sgub_ref, lng_ref,
                         lnb_ref, woa_ref, wos_ref, wout_ref, g2_ref, wr_ref, br_ref, tri_ref, low_ref,
                         x1_ref, h_ref, route_ref, routet_ref, cnt_ref, kvwin_ref,
                         kvx_ref, oatt_ref, osgu_ref):
    T = LAYER_TILE
    j = pl.program_id(1)
    x = x_ref[0]
    xn_b = _rms_norm(x, g1_ref[...]).astype(BF16)
    zq = jnp.dot(xn_b, wq_ref[...], preferred_element_type=F32).astype(BF16)
    zkv = jnp.dot(xn_b, w_in_ref[:, COL_KV:COL_U], preferred_element_type=F32)

    @pl.when(j == pl.num_programs(1) - 1)
    def _():
        kvwin_ref[0] = zkv[T - WINDOW:, :]

    @pl.when(j == 0)
    def _():
        kvx_ref[0:WINDOW, :] = jnp.zeros((WINDOW, 2 * D_KV), BF16)

    @pl.when(j > 0)
    def _():
        kvx_ref[0:WINDOW, :] = kvx_ref[T:T + WINDOW, :]

    kvx_ref[WINDOW:, :] = zkv.astype(BF16)

    col = lax.broadcasted_iota(I32, (1, KEY_PAD), 1)
    key_pad = jnp.zeros((KEY_PAD - BAND, 2 * D_KV), BF16)
    for c in range(T // CHUNK):
        r0 = c * CHUNK
        kvb = jnp.concatenate([kvx_ref[r0:r0 + BAND, :], key_pad], axis=0)
        valid = ((col + (j * T + r0 - WINDOW)) >= 0) | (col >= BAND) if r0 < WINDOW else None
        _attention_rows(zq, r0, CHUNK, kvb[:, :D_KV], kvb[:, D_KV:], bias_ref, valid, oatt_ref,
                        heads_per_unit=4)

    _layer_tail(x, xn_b, oatt_ref, osgu_ref, w_in_ref, sguw_ref, sgub_ref, lng_ref, lnb_ref,
                woa_ref, wos_ref, wout_ref, g2_ref, wr_ref, br_ref, tri_ref, low_ref,
                x1_ref, h_ref, route_ref, routet_ref, cnt_ref, None)


def _sample_layer_kernel(x1_in, h_in, route_in, routet_in, cnt_in,
                         x_ref, ck_ref, cv_ref, g1_ref, wq_ref, w_in_ref, bias_ref, sguw_ref, sgub_ref,
                         lng_ref, lnb_ref, woa_ref, wos_ref, wout_ref, g2_ref, wr_ref, br_ref, tri_ref,
                         low_ref,
                         x1_ref, h_ref, route_ref, routet_ref, cnt_ref, kvnew_ref, vn_ref,
                         oatt_ref, osgu_ref):
    del x1_in, h_in, route_in, routet_in, cnt_in
    T = LAYER_TILE
    nq = x_ref.shape[0] // ck_ref.shape[0]
    x = x_ref[...]
    xn_b = _rms_norm(x, g1_ref[...]).astype(BF16)
    zq = jnp.dot(xn_b, wq_ref[...], preferred_element_type=F32).astype(BF16)
    zkv = jnp.dot(xn_b, w_in_ref[:, COL_KV:COL_U], preferred_element_type=F32)
    kvnew_ref[...] = zkv
    zkv_b = zkv.astype(BF16)
    n_cache = ck_ref.shape[1]
    key_pad = jnp.zeros((KEY_PAD - n_cache - nq, D_KV), BF16)
    for b in range(T // nq):
        r0 = b * nq
        kk = jnp.concatenate([ck_ref[b].astype(BF16), zkv_b[r0:r0 + nq, :D_KV], key_pad], axis=0)
        vv = jnp.concatenate([cv_ref[b].astype(BF16), zkv_b[r0:r0 + nq, D_KV:], key_pad], axis=0)
        _attention_rows(zq, r0, nq, kk, vv, bias_ref, None, oatt_ref, heads_per_unit=REP)

    _layer_tail(x, xn_b, oatt_ref, osgu_ref, w_in_ref, sguw_ref, sgub_ref, lng_ref, lnb_ref,
                woa_ref, wos_ref, wout_ref, g2_ref, wr_ref, br_ref, tri_ref, low_ref,
                x1_ref, h_ref, route_ref, routet_ref, cnt_ref, vn_ref)


def _pack_slot_rows(x):
    lo = lax.bitcast_convert_type(x[:, :SLOT_WORDS].astype(BF16).astype(F32), U32)
    hi = lax.bitcast_convert_type(x[:, SLOT_WORDS:].astype(BF16).astype(F32), U32)
    return (lo >> 16) | hi


def _pack_slot_rows_exact(x):
    lo = lax.bitcast_convert_type(x[:, :SLOT_WORDS], U32)
    hi = lax.bitcast_convert_type(x[:, SLOT_WORDS:], U32)
    return (lo >> 16) | hi


def _unpack_slot_rows(w):
    lo = lax.bitcast_convert_type(w << 16, F32)
    hi = lax.bitcast_convert_type(w & jnp.uint32(0xFFFF0000), F32)
    return jnp.concatenate([lo, hi], axis=1).astype(BF16)


def _as_granules(x):
    return x.reshape(x.shape[0] // GRAN, GRAN, SLOT_WORDS)


def _as_rows(x):
    return x.reshape(x.shape[0] * GRAN, SLOT_WORDS)


def _segment_copies(n_gran, src_gran, dst_gran, bits, make_copy, act):
    for b in range(min(bits, 2)):
        @pl.when(((n_gran >> b) & 1) == 1)
        def _(b=b):
            done = n_gran & ((1 << b) - 1)
            act(make_copy(src_gran + done, dst_gran + done, 1 << b))

    if bits > 2:
        def quad(q, carry):
            done = (n_gran & 3) + 4 * q
            act(make_copy(src_gran + done, dst_gran + done, 4))
            return carry
        lax.fori_loop(0, n_gran >> 2, quad, 0)


def _start_tile_copies(tile, live, cnt_ref, loff_ref, goff_ref, big_ref, make_copy):
    for e in range(N_EXPERTS):
        t = tile * N_EXPERTS + e
        n = jnp.where(live, cnt_ref[t], 0)
        src, dst = loff_ref[t], goff_ref[t]
        for b in range(2):
            @pl.when(((n >> b) & 1) == 1)
            def _(b=b, n=n, src=src, dst=dst):
                done = n & ((1 << b) - 1)
                make_copy(src + done, dst + done, 1 << b).start()
        for q in range(FLAT_QUADS):
            @pl.when((n >> 2) > q)
            def _(q=q, n=n, src=src, dst=dst):
                done = (n & 3) + 4 * q
                make_copy(src + done, dst + done, 4).start()

    @pl.when(live & (big_ref[tile] > 0))
    def _():
        def rest(e, carry):
            t = tile * N_EXPERTS + e
            n, src, dst = cnt_ref[t], loff_ref[t], goff_ref[t]

            def quad(q, c):
                done = (n & 3) + 4 * q
                make_copy(src + done, dst + done, 4).start()
                return c
            lax.fori_loop(FLAT_QUADS, n >> 2, quad, 0)
            return carry
        lax.fori_loop(0, N_EXPERTS, rest, 0)


def _wait_granules(n_gran, bits, make_copy):
    for b in range(bits):
        @pl.when(((n_gran >> b) & 1) == 1)
        def _(b=b):
            make_copy(1 << b).wait()


def _dispatch_kernel(cnt_ref, loff_ref, goff_ref, tot_ref, big_ref, padn_ref, padoff_ref,
                     route_ref, h_ref, xs_hbm, stage_ref, zero_ref, sem):
    i = pl.program_id(0)
    last = pl.num_programs(0) - 1
    slot = lax.rem(i, 2)
    T = TOK_TILE

    def seg_copy(tile_slot):
        return lambda s, d, n: pltpu.make_async_copy(
            stage_ref.at[tile_slot, pl.ds(s, n)], xs_hbm.at[pl.ds(d, n)], sem.at[tile_slot])

    def wait_tile(tile, tile_slot):
        _wait_granules(tot_ref[tile], TILE_BITS, lambda n: pltpu.make_async_copy(
            stage_ref.at[tile_slot, pl.ds(0, n)], xs_hbm.at[pl.ds(0, n)], sem.at[tile_slot]))

    @pl.when(i > 1)
    def _():
        wait_tile(i - 2, slot)

    _start_tile_copies(jnp.maximum(i - 1, 0), i > 0, cnt_ref, loff_ref, goff_ref, big_ref, seg_copy(1 - slot))

    pos = route_ref[0, TOP_K:2 * TOP_K, :].astype(I32)
    r_iota = lax.broadcasted_iota(I32, (LOCAL_ROWS, T), 0)
    p = jnp.zeros((LOCAL_ROWS, T), F32)
    for k in range(TOP_K):
        p = jnp.where(r_iota == pos[k:k + 1, :], 1.0, p)
    stage_ref[slot] = _as_granules(
        _pack_slot_rows_exact(jnp.dot(p.astype(BF16), h_ref[...], preferred_element_type=F32)))

    @pl.when(i == last)
    def _():
        _start_tile_copies(i, i >= 0, cnt_ref, loff_ref, goff_ref, big_ref, seg_copy(slot))

        @pl.when(i > 0)
        def _():
            wait_tile(i - 1, 1 - slot)

        wait_tile(i, slot)
        zero_ref[...] = jnp.zeros(zero_ref.shape, U32)

        def pad_copy(s, d, n):
            return pltpu.make_async_copy(zero_ref.at[pl.ds(s, n)], xs_hbm.at[pl.ds(d, n)], sem.at[2])

        def for_pads(act):
            def body(e, carry):
                _segment_copies(padn_ref[e], 0, padoff_ref[e], PAD_BITS, pad_copy, act)
                return carry
            lax.fori_loop(0, N_EXPERTS, body, 0)

        for_pads(lambda cp: cp.start())
        for_pads(lambda cp: cp.wait())


def _expert_kernel(gran0_ref, nsub_ref, xs_hbm, w1_ref, b1_ref, w2_ref, b2_ref, ys_hbm,
                   w1b_ref, w2b_ref, xbuf_ref, ybuf_ref, sem_in, sem_out):
    e = pl.program_id(0)
    subs = ROW_BLOCK // SUB_BLOCK
    sub_gran = SUB_BLOCK // GRAN
    blk_gran = ROW_BLOCK // GRAN
    gran0 = gran0_ref[e]
    nsub = nsub_ref[e]
    nb_full = nsub // subs
    rem = nsub - nb_full * subs
    nb = nb_full + jnp.where(rem > 0, 1, 0)
    last_slot = lax.rem(nb + 1, 2)

    def in_copy(k, slot):
        return pltpu.make_async_copy(xs_hbm.at[pl.ds(gran0 + k * blk_gran, blk_gran)], xbuf_ref.at[slot],
                                     sem_in.at[slot])

    def out_copy(k, slot, m_sub):
        n = m_sub * sub_gran
        return pltpu.make_async_copy(ybuf_ref.at[slot, pl.ds(0, n)], ys_hbm.at[pl.ds(gran0 + k * blk_gran, n)],
                                     sem_out.at[slot])

    @pl.when(nb > 0)
    def _():
        in_copy(0, 0).start(priority=BLOCK_DMA_PRIORITY)

    w1b_ref[...] = w1_ref[0].astype(BF16)
    w2b_ref[...] = w2_ref[0].astype(BF16)

    def process(k, m_sub):
        n = m_sub * sub_gran
        slot = lax.rem(k, 2)

        @pl.when(k + 1 < nb)
        def _():
            in_copy(k + 1, 1 - slot).start(priority=BLOCK_DMA_PRIORITY)

        in_copy(k, slot).wait()

        @pl.when(k >= 2)
        def _():
            out_copy(k - 2, slot, subs).wait()

        x = _unpack_slot_rows(_as_rows(xbuf_ref[slot, 0:n]))
        h1 = jnp.dot(x, w1b_ref[...], preferred_element_type=F32) + b1_ref[0]
        gate = jnp.minimum(h1[:, :D_FF], SWIGLU_LIMIT)
        up = jnp.clip(h1[:, D_FF:], -SWIGLU_LIMIT, SWIGLU_LIMIT)
        act = gate * jax.nn.sigmoid(SWIGLU_ALPHA * gate) * (up + 1.0)
        y = jnp.dot(act.astype(BF16), w2b_ref[...], preferred_element_type=F32) + b2_ref[0]
        ybuf_ref[slot, 0:n] = _as_granules(_pack_slot_rows(y))
        out_copy(k, slot, m_sub).start(priority=BLOCK_DMA_PRIORITY)

    def full_block(k, carry):
        process(k, subs)
        return carry

    lax.fori_loop(0, nb_full, full_block, 0)

    for m_sub in range(1, subs):
        @pl.when(rem == m_sub)
        def _(m_sub=m_sub):
            process(nb_full, m_sub)

    @pl.when(nb >= 2)
    def _():
        out_copy(nb - 2, 1 - last_slot, subs).wait()

    @pl.when((nb >= 1) & (rem == 0))
    def _():
        out_copy(nb - 1, last_slot, subs).wait()

    for m_sub in range(1, subs):
        @pl.when(rem == m_sub)
        def _(m_sub=m_sub):
            out_copy(nb - 1, last_slot, m_sub).wait()


def _combine_weights(rt):
    T = rt.shape[0]
    l_iota = lax.broadcasted_iota(I32, (T, LOCAL_ROWS), 1)
    pw = jnp.zeros((T, LOCAL_ROWS), F32)
    for k in range(TOP_K):
        pos_k = rt[:, TOP_K + k:TOP_K + k + 1].astype(I32)
        pw = jnp.where(l_iota == pos_k, rt[:, 2 * TOP_K + k:2 * TOP_K + k + 1], pw)
    return pw.astype(BF16)


def _combine_kernel(n_prompt_tiles, cnt_ref, loff_ref, goff_ref, tot_ref, big_ref,
                    routet_ref, routet_next_ref, x1_ref, gf_ref, ys_hbm, yp_ref, ysm_ref,
                    stage_ref, pw_even_ref, pw_odd_ref, sem):
    i = pl.program_id(0)
    n_steps = pl.num_programs(0)
    slot = lax.rem(i, 2)
    odd = slot == 1

    def seg_copy(tile_slot):
        return lambda s, d, n: pltpu.make_async_copy(
            ys_hbm.at[pl.ds(d, n)], stage_ref.at[tile_slot, pl.ds(s, n)], sem.at[tile_slot])

    @pl.when(i == 0)
    def _():
        stage_ref[...] = jnp.zeros(stage_ref.shape, U32)
        _start_tile_copies(i, i >= 0, cnt_ref, loff_ref, goff_ref, big_ref, seg_copy(slot))
        pw_even_ref[...] = _combine_weights(routet_ref[...])
        pw_odd_ref[...] = jnp.zeros(pw_odd_ref.shape, BF16)

    _start_tile_copies(jnp.minimum(i + 1, n_steps - 1), i + 1 < n_steps, cnt_ref, loff_ref, goff_ref, big_ref,
                       seg_copy(1 - slot))

    _wait_granules(tot_ref[i], TILE_BITS, lambda n: pltpu.make_async_copy(
        ys_hbm.at[pl.ds(0, n)], stage_ref.at[slot, pl.ds(0, n)], sem.at[slot]))
    pw_even, pw_odd = pw_even_ref[...], pw_odd_ref[...]
    moe = jnp.dot(jnp.where(odd, pw_odd, pw_even), _unpack_slot_rows(_as_rows(stage_ref[slot])),
                  preferred_element_type=F32)
    pw_next = _combine_weights(routet_next_ref[...])
    pw_even_ref[...] = jnp.where(odd, pw_next, pw_even)
    pw_odd_ref[...] = jnp.where(odd, pw_odd, pw_next)
    y = _rms_norm(x1_ref[...] + moe, gf_ref[...])

    @pl.when(i < n_prompt_tiles)
    def _():
        yp_ref[...] = y

    @pl.when(i >= n_prompt_tiles)
    def _():
        ysm_ref[...] = y


def _t5_bucket(rel):
    half = NUM_BUCKETS // 2
    max_exact = half // 2
    ret = jnp.where(rel > 0, half, 0)
    n = jnp.abs(rel)
    nf = jnp.maximum(n, 1).astype(F32)
    large = max_exact + (jnp.log(nf / max_exact) / math.log(MAX_DISTANCE / max_exact)
                         * (half - max_exact)).astype(jnp.int32)
    large = jnp.minimum(large, half - 1)
    return ret + jnp.where(n < max_exact, n, large)


def _stacked_bias(table, sinks, q_pos, k_pos):
    nq, nk = q_pos.shape[0], k_pos.shape[0]
    bucket = _t5_bucket(k_pos[None, :] - q_pos[:, None])
    onehot = (bucket[:, :, None] == jnp.arange(NUM_BUCKETS)).astype(F32)
    bias = jnp.einsum('qkb,bh->hqk', onehot, table.astype(F32), precision=lax.Precision.HIGHEST)
    bias = bias.reshape(N_KV, REP * nq, nk)
    sink = jnp.repeat(sinks.astype(F32).reshape(N_KV, REP, 1), nq, axis=2).reshape(N_KV, REP * nq, 1)
    tail = jnp.full((N_KV, REP * nq, KEY_PAD - nk - 1), -jnp.inf, F32)
    return jnp.concatenate([bias, sink, tail], axis=-1)


def kernel(x_prompt, x_sample, cache_win_k, cache_win_v, norm1_g, w_in, attn_sinks, rel_bias_table, sgu_ln_g, sgu_ln_b, sgu_w, sgu_b, w_o_att, w_o_sgu, w_out, norm2_g, w_router, b_router, w_exp_in, b_exp_in, w_exp_out, b_exp_out, final_norm_g):
    batch, seq, _ = x_prompt.shape
    dec_batch, dec_seq, _ = x_sample.shape
    cache_rows = cache_win_k.shape[2]
    assert x_prompt.shape[2] == D_MODEL and w_in.shape == (1, D_MODEL, D_IN)
    assert seq % LAYER_TILE == 0 and LAYER_TILE % TOK_TILE == 0 and TOK_TILE % GMLP_CHUNK == 0
    assert LAYER_TILE >= WINDOW and LAYER_TILE % dec_seq == 0 and (dec_batch * dec_seq) % LAYER_TILE == 0
    assert dec_seq <= GMLP_CHUNK and GMLP_CHUNK % dec_seq == 0 and cache_rows == WINDOW
    T = TOK_TILE
    n_prompt = batch * seq
    n_sample = dec_batch * dec_seq
    n_tok = n_prompt + n_sample
    LT = LAYER_TILE
    sorts_per_step = LT // T
    tiles_per_seq = seq // LT
    n_psteps = n_prompt // LT
    n_ssteps = n_sample // LT
    n_ptiles = n_prompt // T
    n_tiles = n_tok // T
    seqs_per_tile = LT // dec_seq

    w_in_b = w_in[0].astype(BF16)
    w_q_b = w_in_b[:, :D_ATT].reshape(D_MODEL, N_KV, REP, HEAD_DIM).transpose(0, 2, 1, 3).reshape(
        D_MODEL, D_ATT)
    woa_b = w_o_att[0].reshape(N_KV, REP, HEAD_DIM, D_MODEL).transpose(1, 0, 2, 3).reshape(
        D_ATT, D_MODEL).astype(BF16)
    wos_b = w_o_sgu[0].astype(BF16)
    wout_b = w_out[0].astype(BF16)
    wr_b = jnp.transpose(w_router[0]).astype(BF16)
    br_col = b_router[0].astype(F32).reshape(N_EXPERTS, 1)
    g1 = norm1_g[0].reshape(1, D_MODEL)
    g2 = norm2_g[0].reshape(1, D_MODEL)
    gf = final_norm_g.reshape(1, D_MODEL)
    lng = sgu_ln_g[0].reshape(1, D_GMLP)
    lnb = sgu_ln_b[0].reshape(1, D_GMLP)
    tril = jnp.tril(jnp.ones((GMLP_CHUNK, GMLP_CHUNK), dtype=bool))
    sguw_p = jnp.where(tril[None], sgu_w[0], 0).astype(BF16)
    sgub_p = jnp.broadcast_to(sgu_b[0][:, :, None], (N_GROUPS, GMLP_CHUNK, GROUP_W)).astype(F32)
    reps = GMLP_CHUNK // dec_seq
    corner = jnp.where(tril[None, :dec_seq, :dec_seq], sgu_w[0][:, :dec_seq, :dec_seq], 0)
    sguw_s = jnp.einsum('ab,gij->gaibj', jnp.eye(reps, dtype=F32), corner).reshape(
        N_GROUPS, GMLP_CHUNK, GMLP_CHUNK).astype(BF16)
    sgub_s = jnp.broadcast_to(jnp.tile(sgu_b[0][:, :dec_seq], (1, reps))[:, :, None],
                              (N_GROUPS, GMLP_CHUNK, GROUP_W)).astype(F32)
    bias_p = _stacked_bias(rel_bias_table, attn_sinks[0], jnp.arange(CHUNK) + WINDOW, jnp.arange(BAND))
    bias_s = _stacked_bias(rel_bias_table, attn_sinks[0], cache_rows + jnp.arange(dec_seq),
                           jnp.arange(cache_rows + dec_seq))
    tri = jnp.triu(jnp.ones((T, T), F32), k=1).astype(BF16)
    low = jnp.tril(jnp.ones((N_EXPERTS, N_EXPERTS), F32), k=-1).astype(BF16)

    layer_out_shapes = (
        jax.ShapeDtypeStruct((n_tok, D_MODEL), F32),
        jax.ShapeDtypeStruct((n_tok, D_MODEL), BF16),
        jax.ShapeDtypeStruct((n_tiles, ROUTE_ROWS, T), F32),
        jax.ShapeDtypeStruct((n_tok, LANES), F32),
        jax.ShapeDtypeStruct((n_tiles, N_EXPERTS, LANES), F32),
    )
    shared_consts = (w_q_b, w_in_b)
    tail_consts_p = (sguw_p, sgub_p, lng, lnb, woa_b, wos_b, wout_b, g2, wr_b, br_col, tri, low)
    tail_consts_s = (sguw_s, sgub_s, lng, lnb, woa_b, wos_b, wout_b, g2, wr_b, br_col, tri, low)

    def tile_specs(tile_of):
        return [
            pl.BlockSpec((LT, D_MODEL), lambda *g: (tile_of(*g), 0)),
            pl.BlockSpec((LT, D_MODEL), lambda *g: (tile_of(*g), 0)),
            pl.BlockSpec((sorts_per_step, ROUTE_ROWS, T), lambda *g: (tile_of(*g), 0, 0)),
            pl.BlockSpec((LT, LANES), lambda *g: (tile_of(*g), 0)),
            pl.BlockSpec((sorts_per_step, N_EXPERTS, LANES), lambda *g: (tile_of(*g), 0, 0)),
        ]

    prompt_consts = (g1,) + shared_consts + (bias_p,) + tail_consts_p
    x1, h, route, routet, cnt, kvwin = pl.pallas_call(
        _prompt_layer_kernel,
        grid=(batch, tiles_per_seq),
        in_specs=[pl.BlockSpec((1, LT, D_MODEL), lambda b, j: (b, j, 0))]
                 + [_const_spec(c.shape) for c in prompt_consts],
        out_specs=tile_specs(lambda b, j: b * tiles_per_seq + j)
                  + [pl.BlockSpec((1, WINDOW, 2 * D_KV), lambda b, j: (b, 0, 0))],
        out_shape=layer_out_shapes + (jax.ShapeDtypeStruct((batch, WINDOW, 2 * D_KV), F32),),
        scratch_shapes=[pltpu.VMEM((LT + WINDOW, 2 * D_KV), BF16),
                        pltpu.VMEM((LT, D_ATT), BF16),
                        pltpu.VMEM((LT, D_GMLP), BF16)],
        compiler_params=pltpu.CompilerParams(dimension_semantics=("arbitrary", "arbitrary"),
                                             vmem_limit_bytes=VMEM_LIMIT),
        name="layer_prompt",
    )(x_prompt, *prompt_consts)

    xs_flat = x_sample.reshape(n_sample, D_MODEL)
    ck = cache_win_k[0].reshape(dec_batch, cache_rows, D_KV)
    cv = cache_win_v[0].reshape(dec_batch, cache_rows, D_KV)
    sample_consts = (g1,) + shared_consts + (bias_s,) + tail_consts_s
    any_spec = pl.BlockSpec(memory_space=pl.ANY)
    x1, h, route, routet, cnt, kvnew, vn_s = pl.pallas_call(
        _sample_layer_kernel,
        grid=(n_ssteps,),
        in_specs=[any_spec] * 5
                 + [pl.BlockSpec((LT, D_MODEL), lambda i: (i, 0)),
                    pl.BlockSpec((seqs_per_tile, cache_rows, D_KV), lambda i: (i, 0, 0)),
                    pl.BlockSpec((seqs_per_tile, cache_rows, D_KV), lambda i: (i, 0, 0))]
                 + [_const_spec(c.shape) for c in sample_consts],
        out_specs=tile_specs(lambda i: n_psteps + i)
                  + [pl.BlockSpec((LT, 2 * D_KV), lambda i: (i, 0)),
                     pl.BlockSpec((LT, D_GMLP), lambda i: (i, 0))],
        out_shape=layer_out_shapes + (jax.ShapeDtypeStruct((n_sample, 2 * D_KV), F32),
                                      jax.ShapeDtypeStruct((n_sample, D_GMLP), F32)),
        scratch_shapes=[pltpu.VMEM((LT, D_ATT), BF16), pltpu.VMEM((LT, D_GMLP), BF16)],
        input_output_aliases={0: 0, 1: 1, 2: 2, 3: 3, 4: 4},
        compiler_params=pltpu.CompilerParams(dimension_semantics=("arbitrary",),
                                             vmem_limit_bytes=VMEM_LIMIT),
        name="layer_sample",
    )(x1, h, route, routet, cnt, xs_flat, ck, cv, *sample_consts)

    blk_gran = SUB_BLOCK // GRAN
    counts =cnt[:, :, 0].astype(I32)
    seg_gran = (counts + (GRAN - 1)) // GRAN
    local_off = jnp.cumsum(seg_gran, axis=1) - seg_gran
    tot_gran = jnp.sum(seg_gran, axis=0)
    ptot_gran = (tot_gran + (blk_gran - 1)) // blk_gran * blk_gran
    pend_gran = jnp.cumsum(ptot_gran)
    gstart = pend_gran - ptot_gran
    global_off = gstart[None, :] + jnp.cumsum(seg_gran, axis=0) - seg_gran
    pad_n = ptot_gran - tot_gran
    pad_off = gstart + tot_gran
    n_rows = -(-(TOP_K * n_tok + n_tiles * N_EXPERTS * (GRAN - 1) + N_EXPERTS * (SUB_BLOCK - GRAN))
               // SUB_BLOCK) * SUB_BLOCK + ROW_BLOCK
    region_gran0 = gstart.astype(I32)
    region_subs = (ptot_gran // blk_gran).astype(I32)
    tile_gran = jnp.sum(seg_gran, axis=1).astype(I32)
    tile_big = jnp.any(seg_gran >= 4 * (FLAT_QUADS + 1), axis=1).astype(I32)
    seg_gran_f = seg_gran.reshape(-1).astype(I32)
    local_off_f = local_off.reshape(-1).astype(I32)
    global_off_f = global_off.reshape(-1).astype(I32)

    xs = pl.pallas_call(
        _dispatch_kernel,
        grid_spec=pltpu.PrefetchScalarGridSpec(
            num_scalar_prefetch=7,
            grid=(n_tiles,),
            in_specs=[pl.BlockSpec((1, ROUTE_ROWS, T), lambda i, *_: (i, 0, 0)),
                      pl.BlockSpec((T, D_MODEL), lambda i, *_: (i, 0))],
            out_specs=pl.BlockSpec(memory_space=pl.ANY),
            scratch_shapes=[pltpu.VMEM((2, LOCAL_ROWS // GRAN, GRAN, SLOT_WORDS), U32),
                            pltpu.VMEM((SUB_BLOCK // GRAN, GRAN, SLOT_WORDS), U32),
                            pltpu.SemaphoreType.DMA((3,))]),
        out_shape=jax.ShapeDtypeStruct((n_rows // GRAN, GRAN, SLOT_WORDS), U32),
        compiler_params=pltpu.CompilerParams(dimension_semantics=("arbitrary",),
                                             vmem_limit_bytes=VMEM_LIMIT),
        name="moe_dispatch",
    )(seg_gran_f, local_off_f, global_off_f, tile_gran, tile_big, pad_n.astype(I32), pad_off.astype(I32),
      route, h)

    ys = pl.pallas_call(
        _expert_kernel,
        grid_spec=pltpu.PrefetchScalarGridSpec(
            num_scalar_prefetch=2,
            grid=(N_EXPERTS,),
            in_specs=[pl.BlockSpec(memory_space=pl.ANY),
                      pl.BlockSpec((1, D_MODEL, 2 * D_FF), lambda e, *_: (e, 0, 0)),
                      pl.BlockSpec((1, 1, 2 * D_FF), lambda e, *_: (e, 0, 0)),
                      pl.BlockSpec((1, D_FF, D_MODEL), lambda e, *_: (e, 0, 0)),
                      pl.BlockSpec((1, 1, D_MODEL), lambda e, *_: (e, 0, 0))],
            out_specs=pl.BlockSpec(memory_space=pl.ANY),
            scratch_shapes=[pltpu.VMEM((D_MODEL, 2 * D_FF), BF16), pltpu.VMEM((D_FF, D_MODEL), BF16),
                            pltpu.VMEM((2, ROW_BLOCK // GRAN, GRAN, SLOT_WORDS), U32),
                            pltpu.VMEM((2, ROW_BLOCK // GRAN, GRAN, SLOT_WORDS), U32),
                            pltpu.SemaphoreType.DMA((2,)), pltpu.SemaphoreType.DMA((2,))]),
        out_shape=jax.ShapeDtypeStruct((n_rows // GRAN, GRAN, SLOT_WORDS), U32),
        compiler_params=pltpu.CompilerParams(dimension_semantics=("arbitrary",),
                                             vmem_limit_bytes=VMEM_LIMIT),
        name="moe_experts",
    )(region_gran0, region_subs, xs, w_exp_in[0], b_exp_in[0].reshape(N_EXPERTS, 1, 2 * D_FF),
      w_exp_out[0], b_exp_out[0].reshape(N_EXPERTS, 1, D_MODEL))

    y_p, y_s = pl.pallas_call(
        functools.partial(_combine_kernel, n_ptiles),
        grid_spec=pltpu.PrefetchScalarGridSpec(
            num_scalar_prefetch=5,
            grid=(n_tiles,),
            in_specs=[pl.BlockSpec((T, LANES), lambda i, *_: (i, 0)),
                      pl.BlockSpec((T, LANES), lambda i, *_: (jnp.minimum(i + 1, n_tiles - 1), 0)),
                      pl.BlockSpec((T, D_MODEL), lambda i, *_: (i, 0)),
                      pl.BlockSpec((1, D_MODEL), lambda i, *_: (0, 0)),
                      pl.BlockSpec(memory_space=pl.ANY)],
            out_specs=[pl.BlockSpec((T, D_MODEL), lambda i, *_: (jnp.minimum(i, n_ptiles - 1), 0)),
                       pl.BlockSpec((T, D_MODEL), lambda i, *_: (jnp.maximum(i - n_ptiles, 0), 0))],
            scratch_shapes=[pltpu.VMEM((2, LOCAL_ROWS // GRAN, GRAN, SLOT_WORDS), U32),
                            pltpu.VMEM((T, LOCAL_ROWS), BF16), pltpu.VMEM((T, LOCAL_ROWS), BF16),
                            pltpu.SemaphoreType.DMA((2,))]),
        out_shape=(jax.ShapeDtypeStruct((n_prompt, D_MODEL), F32),
                   jax.ShapeDtypeStruct((n_sample, D_MODEL), F32)),
        compiler_params=pltpu.CompilerParams(dimension_semantics=("arbitrary",),
                                             vmem_limit_bytes=VMEM_LIMIT),
        name="moe_combine",
    )(seg_gran_f, local_off_f, global_off_f, tile_gran, tile_big, routet, routet, x1, gf, ys)

    y_prompt = y_p.reshape(batch, seq, D_MODEL)
    y_sample = y_s.reshape(dec_batch, dec_seq, D_MODEL)
    new_win_k_prompt = kvwin[:, :, :D_KV].reshape(1, batch, WINDOW, N_KV, HEAD_DIM)
    new_win_v_prompt = kvwin[:, :, D_KV:].reshape(1, batch, WINDOW, N_KV, HEAD_DIM)
    new_win_k_sample = kvnew[:, :D_KV].reshape(1, dec_batch, dec_seq, N_KV, HEAD_DIM)
    new_win_v_sample = kvnew[:, D_KV:].reshape(1, dec_batch, dec_seq, N_KV, HEAD_DIM)
    new_sgu_v_sample = vn_s.reshape(1, dec_batch, dec_seq, D_GMLP)
    return (y_prompt, y_sample, new_win_k_prompt, new_win_v_prompt, new_win_k_sample,
            new_win_v_sample, new_sgu_v_sample)
```

```python
import functools
import math

import numpy as np
import jax
import jax.numpy as jnp
from jax import lax
from jax.experimental import pallas as pl
from jax.experimental.pallas import tpu as pltpu

F32 = jnp.float32
BF16 = jnp.bfloat16
I32 = jnp.int32
U32 = jnp.uint32

D_MODEL = 1024
HEAD_DIM = 64
N_HEADS = 16
N_KV = 2
REP = N_HEADS // N_KV
CHUNK = 64
WINDOW = 128
BAND = WINDOW + CHUNK
KEY_PAD = 256
D_ATT = N_HEADS * HEAD_DIM
D_KV = N_KV * HEAD_DIM
NUM_BUCKETS = 32
MAX_DISTANCE = 128
GMLP_CHUNK = 128
D_GMLP = 1024
N_GROUPS = 4
GROUP_W = D_GMLP // N_GROUPS
N_EXPERTS = 32
TOP_K = 4
D_FF = 1024
SWIGLU_LIMIT = 7.0
SWIGLU_ALPHA = 1.702
NORM_EPS = 1e-5
D_IN = D_ATT + 2 * D_KV + 2 * D_GMLP + 2 * D_MODEL
COL_KV = D_ATT
COL_U = COL_KV + 2 * D_KV
COL_VG = COL_U + D_GMLP
COL_GA = COL_VG + D_GMLP
COL_GB = COL_GA + D_MODEL
SQRT_HALF = float(np.sqrt(0.5))

LANES = 128
WORD_SUBLANES = 8
VMEM_LIMIT = 56 * 1024 * 1024

TOK_TILE = 256
LAYER_TILE = 512
GRAN = WORD_SUBLANES
SLOT_WORDS = D_MODEL // 2
FLAT_QUADS = 1
LOCAL_ROWS = TOP_K * TOK_TILE + N_EXPERTS * GRAN
ROW_BLOCK = 512
SUB_BLOCK = 128
PAD_BITS = (SUB_BLOCK // GRAN - 1).bit_length()
TILE_BITS = (LOCAL_ROWS // GRAN).bit_length()
BLOCK_DMA_PRIORITY = 1
ROUTE_ROWS = 16


def _const_spec(shape):
    nd = len(shape)
    return pl.BlockSpec(shape, lambda *_: (0,) * nd, pipeline_mode=pl.Buffered(1))


def _rms_norm(x, g):
    ms = jnp.mean(x * x, axis=-1, keepdims=True)
    return x * lax.rsqrt(ms + NORM_EPS) * g


def _gelu(x):
    return 0.5 * x * (1.0 + lax.erf(x * SQRT_HALF))


def _attend(q, k_t, v, bias, valid):
    logits = jnp.dot(q, k_t, preferred_element_type=F32) + bias
    if valid is not None:
        logits = jnp.where(valid, logits, -jnp.inf)
    m = jnp.max(logits, axis=-1, keepdims=True)
    p = jnp.exp(logits - m)
    den = jnp.sum(p, axis=-1, keepdims=True)
    w = (p * (1.0 / den)).astype(BF16)
    return jnp.dot(w, v, preferred_element_type=F32)


def _attention_rows(zq, row0, n_rows, k_ext, v_ext, bias_ref, valid, oatt_ref, heads_per_unit):
    k_t = jnp.transpose(k_ext.astype(F32)).astype(BF16)
    lane = lax.broadcasted_iota(I32, (1, LANES), 1)
    for g in range(N_KV):
        in_group = (lane >= g * HEAD_DIM) & (lane < (g + 1) * HEAD_DIM)
        qmask = jnp.where(in_group, HEAD_DIM ** -0.5, 0.0).astype(BF16)
        for part in range(REP // heads_per_unit):
            tiles = range(part * heads_per_unit, (part + 1) * heads_per_unit)
            q = jnp.concatenate([zq[row0:row0 + n_rows, t * LANES:(t + 1) * LANES] * qmask for t in tiles],
                                axis=0)
            b0 = part * heads_per_unit * n_rows
            o = _attend(q, k_t, v_ext, bias_ref[g, b0:b0 + heads_per_unit * n_rows, :], valid)
            for u, t in enumerate(tiles):
                c0 = t * LANES + g * HEAD_DIM
                oatt_ref[row0:row0 + n_rows, c0:c0 + HEAD_DIM] = (
                    o[u * n_rows:(u + 1) * n_rows, g * HEAD_DIM:(g + 1) * HEAD_DIM].astype(BF16))


def _layer_tail(x, xn_b, oatt_ref, osgu_ref, w_in_ref, sguw_ref, sgub_ref, lng_ref, lnb_ref,
                woa_ref, wos_ref, wout_ref, g2_ref, wr_ref, br_ref, tri_ref, low_ref,
                x1_ref, h_ref, route_ref, routet_ref, cnt_ref, vn_ref):
    T = x.shape[0]
    u = _gelu(jnp.dot(xn_b, w_in_ref[:, COL_U:COL_VG], preferred_element_type=F32))
    vg = _gelu(jnp.dot(xn_b, w_in_ref[:, COL_VG:COL_GA], preferred_element_type=F32))
    mu = jnp.mean(vg, axis=-1, keepdims=True)
    var = jnp.mean(jnp.square(vg - mu), axis=-1, keepdims=True)
    vn = (vg - mu) * lax.rsqrt(var + NORM_EPS) * lng_ref[...] + lnb_ref[...]
    if vn_ref is not None:
        vn_ref[...] = vn
    vn_b = vn.astype(BF16)
    for j in range(T // GMLP_CHUNK):
        rows = slice(j * GMLP_CHUNK, (j + 1) * GMLP_CHUNK)
        for g in range(N_GROUPS):
            cols = slice(g * GROUP_W, (g + 1) * GROUP_W)
            mixed = jnp.dot(sguw_ref[g], vn_b[rows, cols], preferred_element_type=F32) + sgub_ref[g]
            osgu_ref[rows, cols] = (u[rows, cols] * mixed).astype(BF16)

    ga = jax.nn.sigmoid(jnp.dot(xn_b, w_in_ref[:, COL_GA:COL_GB], preferred_element_type=F32))
    merged = ga * jnp.dot(oatt_ref[...], woa_ref[...], preferred_element_type=F32)
    gb = jax.nn.sigmoid(jnp.dot(xn_b, w_in_ref[:, COL_GB:D_IN], preferred_element_type=F32))
    merged = merged + gb * jnp.dot(osgu_ref[...], wos_ref[...], preferred_element_type=F32)
    x1 = x + jnp.dot(merged.astype(BF16), wout_ref[...], preferred_element_type=F32)
    x1_ref[...] = x1

    h_b = _rms_norm(x1, g2_ref[...]).astype(BF16)
    h_ref[...] = h_b
    lt = lax.dot_general(wr_ref[...], h_b, (((1,), (1,)), ((), ())),
                         preferred_element_type=F32) + br_ref[...]
    for s in range(T // TOK_TILE):
        rec, cnt_b = _route_sort_tile(lt[:, s * TOK_TILE:(s + 1) * TOK_TILE], tri_ref, low_ref)
        route_ref[s] = rec
        rec_pad = jnp.concatenate([rec, jnp.zeros((LANES - ROUTE_ROWS, TOK_TILE), F32)], axis=0)
        routet_ref[s * TOK_TILE:(s + 1) * TOK_TILE, :] = jnp.transpose(rec_pad)
        cnt_ref[s] = cnt_b


def _route_sort_tile(lt, tri_ref, low_ref):
    T = lt.shape[1]
    e_iota = lax.broadcasted_iota(I32, (N_EXPERTS, T), 0).astype(F32)
    cur = lt
    vals, idxs = [], []
    for _ in range(TOP_K):
        m = jnp.max(cur, axis=0, keepdims=True)
        ik = jnp.min(jnp.where(cur == m, e_iota, float(N_EXPERTS)), axis=0, keepdims=True)
        vals.append(m)
        idxs.append(ik)
        cur = jnp.where(e_iota == ik, -jnp.inf, cur)
    exps = [jnp.exp(v - vals[0]) for v in vals]
    den = exps[0] + exps[1] + exps[2] + exps[3]
    gates = [e / den for e in exps]

    onehot = jnp.zeros((N_EXPERTS, T), F32)
    for ik in idxs:
        onehot = onehot + jnp.where(e_iota == ik, 1.0, 0.0)
    rank = jnp.dot(onehot.astype(BF16), tri_ref[...], preferred_element_type=F32)
    cnt = jnp.sum(onehot, axis=1, keepdims=True)
    cnt_b = jnp.broadcast_to(cnt, (N_EXPERTS, LANES))
    gran = jnp.floor((cnt_b + (GRAN - 1)) * (1.0 / GRAN))
    off = jnp.dot(low_ref[...], gran.astype(BF16), preferred_element_type=F32) * GRAN
    base = off[:, 0:1] + rank
    poss = [jnp.sum(jnp.where(e_iota == ik, base, 0.0), axis=0, keepdims=True) for ik in idxs]

    rec = jnp.concatenate(idxs + poss + gates
                          + [jnp.zeros((ROUTE_ROWS - 3 * TOP_K, T), F32)], axis=0)
    return rec, cnt_b


def _prompt_layer_kernel(x_ref, g1_ref, wq_ref, w_in_ref, bias_ref, sguw_ref, sgub_ref, lng_ref,
                         lnb_ref, woa_ref, wos_ref, wout_ref, g2_ref, wr_ref, br_ref, tri_ref, low_ref,
                         x1_ref, h_ref, route_ref, routet_ref, cnt_ref, kvwin_ref,
                         kvx_ref, oatt_ref, osgu_ref):
    T = LAYER_TILE
    j = pl.program_id(1)
    x = x_ref[0]
    xn_b = _rms_norm(x, g1_ref[...]).astype(BF16)
    zq = jnp.dot(xn_b, wq_ref[...], preferred_element_type=F32).astype(BF16)
    zkv = jnp.dot(xn_b, w_in_ref[:, COL_KV:COL_U], preferred_element_type=F32)

    @pl.when(j == pl.num_programs(1) - 1)
    def _():
        kvwin_ref[0] = zkv[T - WINDOW:, :]

    @pl.when(j == 0)
    def _():
        kvx_ref[0:WINDOW, :] = jnp.zeros((WINDOW, 2 * D_KV), BF16)

    @pl.when(j > 0)
    def _():
        kvx_ref[0:WINDOW, :] = kvx_ref[T:T + WINDOW, :]

    kvx_ref[WINDOW:, :] = zkv.astype(BF16)

    col = lax.broadcasted_iota(I32, (1, KEY_PAD), 1)
    key_pad = jnp.zeros((KEY_PAD - BAND, 2 * D_KV), BF16)
    for c in range(T // CHUNK):
        r0 = c * CHUNK
        kvb = jnp.concatenate([kvx_ref[r0:r0 + BAND, :], key_pad], axis=0)
        valid = ((col + (j * T + r0 - WINDOW)) >= 0) | (col >= BAND) if r0 < WINDOW else None
        _attention_rows(zq, r0, CHUNK, kvb[:, :D_KV], kvb[:, D_KV:], bias_ref, valid, oatt_ref,
                        heads_per_unit=4)

    _layer_tail(x, xn_b, oatt_ref, osgu_ref, w_in_ref, sguw_ref, sgub_ref, lng_ref, lnb_ref,
                woa_ref, wos_ref, wout_ref, g2_ref, wr_ref, br_ref, tri_ref, low_ref,
                x1_ref, h_ref, route_ref, routet_ref, cnt_ref, None)


def _sample_layer_kernel(x1_in, h_in, route_in, routet_in, cnt_in,
                         x_ref, ck_ref, cv_ref, g1_ref, wq_ref, w_in_ref, bias_ref, sguw_ref, sgub_ref,
                         lng_ref, lnb_ref, woa_ref, wos_ref, wout_ref, g2_ref, wr_ref, br_ref, tri_ref,
                         low_ref,
                         x1_ref, h_ref, route_ref, routet_ref, cnt_ref, kvnew_ref, vn_ref,
                         oatt_ref, osgu_ref):
    del x1_in, h_in, route_in, routet_in, cnt_in
    T = LAYER_TILE
    nq = x_ref.shape[0] // ck_ref.shape[0]
    x = x_ref[...]
    xn_b = _rms_norm(x, g1_ref[...]).astype(BF16)
    zq = jnp.dot(xn_b, wq_ref[...], preferred_element_type=F32).astype(BF16)
    zkv = jnp.dot(xn_b, w_in_ref[:, COL_KV:COL_U], preferred_element_type=F32)
    kvnew_ref[...] = zkv
    zkv_b = zkv.astype(BF16)
    n_cache = ck_ref.shape[1]
    key_pad = jnp.zeros((KEY_PAD - n_cache - nq, D_KV), BF16)
    for b in range(T // nq):
        r0 = b * nq
        kk = jnp.concatenate([ck_ref[b].astype(BF16), zkv_b[r0:r0 + nq, :D_KV], key_pad], axis=0)
        vv = jnp.concatenate([cv_ref[b].astype(BF16), zkv_b[r0:r0 + nq, D_KV:], key_pad], axis=0)
        _attention_rows(zq, r0, nq, kk, vv, bias_ref, None, oatt_ref, heads_per_unit=REP)

    _layer_tail(x, xn_b, oatt_ref, osgu_ref, w_in_ref, sguw_ref, sgub_ref, lng_ref, lnb_ref,
                woa_ref, wos_ref, wout_ref, g2_ref, wr_ref, br_ref, tri_ref, low_ref,
                x1_ref, h_ref, route_ref, routet_ref, cnt_ref, vn_ref)


def _pack_slot_rows(x):
    lo = lax.bitcast_convert_type(x[:, :SLOT_WORDS].astype(BF16).astype(F32), U32)
    hi = lax.bitcast_convert_type(x[:, SLOT_WORDS:].astype(BF16).astype(F32), U32)
    return (lo >> 16) | hi


def _pack_slot_rows_exact(x):
    lo = lax.bitcast_convert_type(x[:, :SLOT_WORDS], U32)
    hi = lax.bitcast_convert_type(x[:, SLOT_WORDS:], U32)
    return (lo >> 16) | hi


def _unpack_slot_rows(w):
    lo = lax.bitcast_convert_type(w << 16, F32)
    hi = lax.bitcast_convert_type(w & jnp.uint32(0xFFFF0000), F32)
    return jnp.concatenate([lo, hi], axis=1).astype(BF16)


def _as_granules(x):
    return x.reshape(x.shape[0] // GRAN, GRAN, SLOT_WORDS)


def _as_rows(x):
    return x.reshape(x.shape[0] * GRAN, SLOT_WORDS)


def _segment_copies(n_gran, src_gran, dst_gran, bits, make_copy, act):
    for b in range(min(bits, 2)):
        @pl.when(((n_gran >> b) & 1) == 1)
        def _(b=b):
            done = n_gran & ((1 << b) - 1)
            act(make_copy(src_gran + done, dst_gran + done, 1 << b))

    if bits > 2:
        def quad(q, carry):
            done = (n_gran & 3) + 4 * q
            act(make_copy(src_gran + done, dst_gran + done, 4))
            return carry
        lax.fori_loop(0, n_gran >> 2, quad, 0)


def _start_tile_copies(tile, live, cnt_ref, loff_ref, goff_ref, big_ref, make_copy):
    for e in range(N_EXPERTS):
        t = tile * N_EXPERTS + e
        n = jnp.where(live, cnt_ref[t], 0)
        src, dst = loff_ref[t], goff_ref[t]
        for b in range(2):
            @pl.when(((n >> b) & 1) == 1)
            def _(b=b, n=n, src=src, dst=dst):
                done = n & ((1 << b) - 1)
                make_copy(src + done, dst + done, 1 << b).start()
        for q in range(FLAT_QUADS):
            @pl.when((n >> 2) > q)
            def _(q=q, n=n, src=src, dst=dst):
                done = (n & 3) + 4 * q
                make_copy(src + done, dst + done, 4).start()

    @pl.when(live & (big_ref[tile] > 0))
    def _():
        def rest(e, carry):
            t = tile * N_EXPERTS + e
            n, src, dst = cnt_ref[t], loff_ref[t], goff_ref[t]

            def quad(q, c):
                done = (n & 3) + 4 * q
                make_copy(src + done, dst + done, 4).start()
                return c
            lax.fori_loop(FLAT_QUADS, n >> 2, quad, 0)
            return carry
        lax.fori_loop(0, N_EXPERTS, rest, 0)


def _wait_granules(n_gran, bits, make_copy):
    for b in range(bits):
        @pl.when(((n_gran >> b) & 1) == 1)
        def _(b=b):
            make_copy(1 << b).wait()


def _dispatch_kernel(cnt_ref, loff_ref, goff_ref, tot_ref, big_ref, padn_ref, padoff_ref,
                     route_ref, h_ref, xs_hbm, stage_ref, zero_ref, sem):
    i = pl.program_id(0)
    last = pl.num_programs(0) - 1
    slot = lax.rem(i, 2)
    T = TOK_TILE

    def seg_copy(tile_slot):
        return lambda s, d, n: pltpu.make_async_copy(
            stage_ref.at[tile_slot, pl.ds(s, n)], xs_hbm.at[pl.ds(d, n)], sem.at[tile_slot])

    def wait_tile(tile, tile_slot):
        _wait_granules(tot_ref[tile], TILE_BITS, lambda n: pltpu.make_async_copy(
            stage_ref.at[tile_slot, pl.ds(0, n)], xs_hbm.at[pl.ds(0, n)], sem.at[tile_slot]))

    @pl.when(i > 1)
    def _():
        wait_tile(i - 2, slot)

    _start_tile_copies(jnp.maximum(i - 1, 0), i > 0, cnt_ref, loff_ref, goff_ref, big_ref, seg_copy(1 - slot))

    pos = route_ref[0, TOP_K:2 * TOP_K, :].astype(I32)
    r_iota = lax.broadcasted_iota(I32, (LOCAL_ROWS, T), 0)
    p = jnp.zeros((LOCAL_ROWS, T), F32)
    for k in range(TOP_K):
        p = jnp.where(r_iota == pos[k:k + 1, :], 1.0, p)
    stage_ref[slot] = _as_granules(
        _pack_slot_rows_exact(jnp.dot(p.astype(BF16), h_ref[...], preferred_element_type=F32)))

    @pl.when(i == last)
    def _():
        _start_tile_copies(i, i >= 0, cnt_ref, loff_ref, goff_ref, big_ref, seg_copy(slot))

        @pl.when(i > 0)
        def _():
            wait_tile(i - 1, 1 - slot)

        wait_tile(i, slot)
        zero_ref[...] = jnp.zeros(zero_ref.shape, U32)

        def pad_copy(s, d, n):
            return pltpu.make_async_copy(zero_ref.at[pl.ds(s, n)], xs_hbm.at[pl.ds(d, n)], sem.at[2])

        def for_pads(act):
            def body(e, carry):
                _segment_copies(padn_ref[e], 0, padoff_ref[e], PAD_BITS, pad_copy, act)
                return carry
            lax.fori_loop(0, N_EXPERTS, body, 0)

        for_pads(lambda cp: cp.start())
        for_pads(lambda cp: cp.wait())


def _expert_kernel(gran0_ref, nsub_ref, xs_hbm, w1_ref, b1_ref, w2_ref, b2_ref, ys_hbm,
                   w1b_ref, w2b_ref, xbuf_ref, ybuf_ref, sem_in, sem_out):
    e = pl.program_id(0)
    subs = ROW_BLOCK // SUB_BLOCK
    sub_gran = SUB_BLOCK // GRAN
    blk_gran = ROW_BLOCK // GRAN
    gran0 = gran0_ref[e]
    nsub = nsub_ref[e]
    nb_full = nsub // subs
    rem = nsub - nb_full * subs
    nb = nb_full + jnp.where(rem > 0, 1, 0)
    last_slot = lax.rem(nb + 1, 2)

    def in_copy(k, slot):
        return pltpu.make_async_copy(xs_hbm.at[pl.ds(gran0 + k * blk_gran, blk_gran)], xbuf_ref.at[slot],
                                     sem_in.at[slot])

    def out_copy(k, slot, m_sub):
        n = m_sub * sub_gran
        return pltpu.make_async_copy(ybuf_ref.at[slot, pl.ds(0, n)], ys_hbm.at[pl.ds(gran0 + k * blk_gran, n)],
                                     sem_out.at[slot])

    @pl.when(nb > 0)
    def _():
        in_copy(0, 0).start(priority=BLOCK_DMA_PRIORITY)

    w1b_ref[...] = w1_ref[0].astype(BF16)
    w2b_ref[...] = w2_ref[0].astype(BF16)

    def process(k, m_sub):
        n = m_sub * sub_gran
        slot = lax.rem(k, 2)

        @pl.when(k + 1 < nb)
        def _():
            in_copy(k + 1, 1 - slot).start(priority=BLOCK_DMA_PRIORITY)

        in_copy(k, slot).wait()

        @pl.when(k >= 2)
        def _():
            out_copy(k - 2, slot, subs).wait()

        x = _unpack_slot_rows(_as_rows(xbuf_ref[slot, 0:n]))
        h1 = jnp.dot(x, w1b_ref[...], preferred_element_type=F32) + b1_ref[0]
        gate = jnp.minimum(h1[:, :D_FF], SWIGLU_LIMIT)
        up = jnp.clip(h1[:, D_FF:], -SWIGLU_LIMIT, SWIGLU_LIMIT)
        act = gate * jax.nn.sigmoid(SWIGLU_ALPHA * gate) * (up + 1.0)
        y = jnp.dot(act.astype(BF16), w2b_ref[...], preferred_element_type=F32) + b2_ref[0]
        ybuf_ref[slot, 0:n] = _as_granules(_pack_slot_rows(y))
        out_copy(k, slot, m_sub).start(priority=BLOCK_DMA_PRIORITY)

    def full_block(k, carry):
        process(k, subs)
        return carry

    lax.fori_loop(0, nb_full, full_block, 0)

    for m_sub in range(1, subs):
        @pl.when(rem == m_sub)
        def _(m_sub=m_sub):
            process(nb_full, m_sub)

    @pl.when(nb >= 2)
    def _():
        out_copy(nb - 2, 1 - last_slot, subs).wait()

    @pl.when((nb >= 1) & (rem == 0))
    def _():
        out_copy(nb - 1, last_slot, subs).wait()

    for m_sub in range(1, subs):
        @pl.when(rem == m_sub)
        def _(m_sub=m_sub):
            out_copy(nb - 1, last_slot, m_sub).wait()


def _combine_kernel(n_prompt_tiles, cnt_ref, loff_ref, goff_ref, tot_ref, big_ref,
                    routet_ref, x1_ref, gf_ref, ys_hbm, yp_ref, ysm_ref, stage_ref, sem):
    i = pl.program_id(0)
    n_steps = pl.num_programs(0)
    slot = lax.rem(i, 2)
    T = TOK_TILE

    def seg_copy(tile_slot):
        return lambda s, d, n: pltpu.make_async_copy(
            ys_hbm.at[pl.ds(d, n)], stage_ref.at[tile_slot, pl.ds(s, n)], sem.at[tile_slot])

    @pl.when(i == 0)
    def _():
        stage_ref[...] = jnp.zeros(stage_ref.shape, U32)
        _start_tile_copies(i, i >= 0, cnt_ref, loff_ref, goff_ref, big_ref, seg_copy(slot))

    _start_tile_copies(jnp.minimum(i + 1, n_steps - 1), i + 1 < n_steps, cnt_ref, loff_ref, goff_ref, big_ref,
                       seg_copy(1 - slot))

    rt = routet_ref[...]
    l_iota = lax.broadcasted_iota(I32, (T, LOCAL_ROWS), 1)
    pw = jnp.zeros((T, LOCAL_ROWS), F32)
    for k in range(TOP_K):
        pos_k = rt[:, TOP_K + k:TOP_K + k + 1].astype(I32)
        pw = jnp.where(l_iota == pos_k, rt[:, 2 * TOP_K + k:2 * TOP_K + k + 1], pw)

    _wait_granules(tot_ref[i], TILE_BITS, lambda n: pltpu.make_async_copy(
        ys_hbm.at[pl.ds(0, n)], stage_ref.at[slot, pl.ds(0, n)], sem.at[slot]))
    moe = jnp.dot(pw.astype(BF16), _unpack_slot_rows(_as_rows(stage_ref[slot])), preferred_element_type=F32)
    y = _rms_norm(x1_ref[...] + moe, gf_ref[...])

    @pl.when(i < n_prompt_tiles)
    def _():
        yp_ref[...] = y

    @pl.when(i >= n_prompt_tiles)
    def _():
        ysm_ref[...] = y


def _t5_bucket(rel):
    half = NUM_BUCKETS // 2
    max_exact = half // 2
    ret = jnp.where(rel > 0, half, 0)
    n = jnp.abs(rel)
    nf = jnp.maximum(n, 1).astype(F32)
    large = max_exact + (jnp.log(nf / max_exact) / math.log(MAX_DISTANCE / max_exact)
                         * (half - max_exact)).astype(jnp.int32)
    large = jnp.minimum(large, half - 1)
    return ret + jnp.where(n < max_exact, n, large)


def _stacked_bias(table, sinks, q_pos, k_pos):
    nq, nk = q_pos.shape[0], k_pos.shape[0]
    bucket = _t5_bucket(k_pos[None, :] - q_pos[:, None])
    onehot = (bucket[:, :, None] == jnp.arange(NUM_BUCKETS)).astype(F32)
    bias = jnp.einsum('qkb,bh->hqk', onehot, table.astype(F32), precision=lax.Precision.HIGHEST)
    bias = bias.reshape(N_KV, REP * nq, nk)
    sink = jnp.repeat(sinks.astype(F32).reshape(N_KV, REP, 1), nq, axis=2).reshape(N_KV, REP * nq, 1)
    tail = jnp.full((N_KV, REP * nq, KEY_PAD - nk - 1), -jnp.inf, F32)
    return jnp.concatenate([bias, sink, tail], axis=-1)


def kernel(x_prompt, x_sample, cache_win_k, cache_win_v, norm1_g, w_in, attn_sinks, rel_bias_table, sgu_ln_g, sgu_ln_b, sgu_w, sgu_b, w_o_att, w_o_sgu, w_out, norm2_g, w_router, b_router, w_exp_in, b_exp_in, w_exp_out, b_exp_out, final_norm_g):
    batch, seq, _ = x_prompt.shape
    dec_batch, dec_seq, _ = x_sample.shape
    cache_rows = cache_win_k.shape[2]
    assert x_prompt.shape[2] == D_MODEL and w_in.shape == (1, D_MODEL, D_IN)
    assert seq % LAYER_TILE == 0 and LAYER_TILE % TOK_TILE == 0 and TOK_TILE % GMLP_CHUNK == 0
    assert LAYER_TILE >= WINDOW and LAYER_TILE % dec_seq == 0 and (dec_batch * dec_seq) % LAYER_TILE == 0
    assert dec_seq <= GMLP_CHUNK and GMLP_CHUNK % dec_seq == 0 and cache_rows == WINDOW
    T = TOK_TILE
    n_prompt = batch * seq
    n_sample = dec_batch * dec_seq
    n_tok = n_prompt + n_sample
    LT = LAYER_TILE
    sorts_per_step = LT // T
    tiles_per_seq = seq // LT
    n_psteps = n_prompt // LT
    n_ssteps = n_sample // LT
    n_ptiles = n_prompt // T
    n_tiles = n_tok // T
    seqs_per_tile = LT // dec_seq

    w_in_b = w_in[0].astype(BF16)
    w_q_b = w_in_b[:, :D_ATT].reshape(D_MODEL, N_KV, REP, HEAD_DIM).transpose(0, 2, 1, 3).reshape(
        D_MODEL, D_ATT)
    woa_b = w_o_att[0].reshape(N_KV, REP, HEAD_DIM, D_MODEL).transpose(1, 0, 2, 3).reshape(
        D_ATT, D_MODEL).astype(BF16)
    wos_b = w_o_sgu[0].astype(BF16)
    wout_b = w_out[0].astype(BF16)
    wr_b = jnp.transpose(w_router[0]).astype(BF16)
    br_col = b_router[0].astype(F32).reshape(N_EXPERTS, 1)
    g1 = norm1_g[0].reshape(1, D_MODEL)
    g2 = norm2_g[0].reshape(1, D_MODEL)
    gf = final_norm_g.reshape(1, D_MODEL)
    lng = sgu_ln_g[0].reshape(1, D_GMLP)
    lnb = sgu_ln_b[0].reshape(1, D_GMLP)
    tril = jnp.tril(jnp.ones((GMLP_CHUNK, GMLP_CHUNK), dtype=bool))
    sguw_p = jnp.where(tril[None], sgu_w[0], 0).astype(BF16)
    sgub_p = jnp.broadcast_to(sgu_b[0][:, :, None], (N_GROUPS, GMLP_CHUNK, GROUP_W)).astype(F32)
    reps = GMLP_CHUNK // dec_seq
    corner = jnp.where(tril[None, :dec_seq, :dec_seq], sgu_w[0][:, :dec_seq, :dec_seq], 0)
    sguw_s = jnp.einsum('ab,gij->gaibj', jnp.eye(reps, dtype=F32), corner).reshape(
        N_GROUPS, GMLP_CHUNK, GMLP_CHUNK).astype(BF16)
    sgub_s = jnp.broadcast_to(jnp.tile(sgu_b[0][:, :dec_seq], (1, reps))[:, :, None],
                              (N_GROUPS, GMLP_CHUNK, GROUP_W)).astype(F32)
    bias_p = _stacked_bias(rel_bias_table, attn_sinks[0], jnp.arange(CHUNK) + WINDOW, jnp.arange(BAND))
    bias_s = _stacked_bias(rel_bias_table, attn_sinks[0], cache_rows + jnp.arange(dec_seq),
                           jnp.arange(cache_rows + dec_seq))
    tri = jnp.triu(jnp.ones((T, T), F32), k=1).astype(BF16)
    low = jnp.tril(jnp.ones((N_EXPERTS, N_EXPERTS), F32), k=-1).astype(BF16)

    layer_out_shapes = (
        jax.ShapeDtypeStruct((n_tok, D_MODEL), F32),
        jax.ShapeDtypeStruct((n_tok, D_MODEL), BF16),
        jax.ShapeDtypeStruct((n_tiles, ROUTE_ROWS, T), F32),
        jax.ShapeDtypeStruct((n_tok, LANES), F32),
        jax.ShapeDtypeStruct((n_tiles, N_EXPERTS, LANES), F32),
    )
    shared_consts = (w_q_b, w_in_b)
    tail_consts_p = (sguw_p, sgub_p, lng, lnb, woa_b, wos_b, wout_b, g2, wr_b, br_col, tri, low)
    tail_consts_s = (sguw_s, sgub_s, lng, lnb, woa_b, wos_b, wout_b, g2, wr_b, br_col, tri, low)

    def tile_specs(tile_of):
        return [
            pl.BlockSpec((LT, D_MODEL), lambda *g: (tile_of(*g), 0)),
            pl.BlockSpec((LT, D_MODEL), lambda *g: (tile_of(*g), 0)),
            pl.BlockSpec((sorts_per_step, ROUTE_ROWS, T), lambda *g: (tile_of(*g), 0, 0)),
            pl.BlockSpec((LT, LANES), lambda *g: (tile_of(*g), 0)),
            pl.BlockSpec((sorts_per_step, N_EXPERTS, LANES), lambda *g: (tile_of(*g), 0, 0)),
        ]

    prompt_consts = (g1,) + shared_consts + (bias_p,) + tail_consts_p
    x1, h, route, routet, cnt, kvwin = pl.pallas_call(
        _prompt_layer_kernel,
        grid=(batch, tiles_per_seq),
        in_specs=[pl.BlockSpec((1, LT, D_MODEL), lambda b, j: (b, j, 0))]
                 + [_const_spec(c.shape) for c in prompt_consts],
        out_specs=tile_specs(lambda b, j: b * tiles_per_seq + j)
                  + [pl.BlockSpec((1, WINDOW, 2 * D_KV), lambda b, j: (b, 0, 0))],
        out_shape=layer_out_shapes + (jax.ShapeDtypeStruct((batch, WINDOW, 2 * D_KV), F32),),
        scratch_shapes=[pltpu.VMEM((LT + WINDOW, 2 * D_KV), BF16),
                        pltpu.VMEM((LT, D_ATT), BF16),
                        pltpu.VMEM((LT, D_GMLP), BF16)],
        compiler_params=pltpu.CompilerParams(dimension_semantics=("arbitrary", "arbitrary"),
                                             vmem_limit_bytes=VMEM_LIMIT),
        name="layer_prompt",
    )(x_prompt, *prompt_consts)

    xs_flat = x_sample.reshape(n_sample, D_MODEL)
    ck = cache_win_k[0].reshape(dec_batch, cache_rows, D_KV)
    cv = cache_win_v[0].reshape(dec_batch, cache_rows, D_KV)
    sample_consts = (g1,) + shared_consts + (bias_s,) + tail_consts_s
    any_spec = pl.BlockSpec(memory_space=pl.ANY)
    x1, h, route, routet, cnt, kvnew, vn_s = pl.pallas_call(
        _sample_layer_kernel,
        grid=(n_ssteps,),
        in_specs=[any_spec] * 5
                 + [pl.BlockSpec((LT, D_MODEL), lambda i: (i, 0)),
                    pl.BlockSpec((seqs_per_tile, cache_rows, D_KV), lambda i: (i, 0, 0)),
                    pl.BlockSpec((seqs_per_tile, cache_rows, D_KV), lambda i: (i, 0, 0))]
                 + [_const_spec(c.shape) for c in sample_consts],
        out_specs=tile_specs(lambda i: n_psteps + i)
                  + [pl.BlockSpec((LT, 2 * D_KV), lambda i: (i, 0)),
                     pl.BlockSpec((LT, D_GMLP), lambda i: (i, 0))],
        out_shape=layer_out_shapes + (jax.ShapeDtypeStruct((n_sample, 2 * D_KV), F32),
                                      jax.ShapeDtypeStruct((n_sample, D_GMLP), F32)),
        scratch_shapes=[pltpu.VMEM((LT, D_ATT), BF16), pltpu.VMEM((LT, D_GMLP), BF16)],
        input_output_aliases={0: 0, 1: 1, 2: 2, 3: 3, 4: 4},
        compiler_params=pltpu.CompilerParams(dimension_semantics=("arbitrary",),
                                             vmem_limit_bytes=VMEM_LIMIT),
        name="layer_sample",
    )(x1, h, route, routet, cnt, xs_flat, ck, cv, *sample_consts)

    blk_gran = SUB_BLOCK // GRAN
    counts =cnt[:, :, 0].astype(I32)
    seg_gran = (counts + (GRAN - 1)) // GRAN
    local_off = jnp.cumsum(seg_gran, axis=1) - seg_gran
    tot_gran = jnp.sum(seg_gran, axis=0)
    ptot_gran = (tot_gran + (blk_gran - 1)) // blk_gran * blk_gran
    pend_gran = jnp.cumsum(ptot_gran)
    gstart = pend_gran - ptot_gran
    global_off = gstart[None, :] + jnp.cumsum(seg_gran, axis=0) - seg_gran
    pad_n = ptot_gran - tot_gran
    pad_off = gstart + tot_gran
    n_rows = -(-(TOP_K * n_tok + n_tiles * N_EXPERTS * (GRAN - 1) + N_EXPERTS * (SUB_BLOCK - GRAN))
               // SUB_BLOCK) * SUB_BLOCK + ROW_BLOCK
    region_gran0 = gstart.astype(I32)
    region_subs = (ptot_gran // blk_gran).astype(I32)
    tile_gran = jnp.sum(seg_gran, axis=1).astype(I32)
    tile_big = jnp.any(seg_gran >= 4 * (FLAT_QUADS + 1), axis=1).astype(I32)
    seg_gran_f = seg_gran.reshape(-1).astype(I32)
    local_off_f = local_off.reshape(-1).astype(I32)
    global_off_f = global_off.reshape(-1).astype(I32)

    xs = pl.pallas_call(
        _dispatch_kernel,
        grid_spec=pltpu.PrefetchScalarGridSpec(
            num_scalar_prefetch=7,
            grid=(n_tiles,),
            in_specs=[pl.BlockSpec((1, ROUTE_ROWS, T), lambda i, *_: (i, 0, 0)),
                      pl.BlockSpec((T, D_MODEL), lambda i, *_: (i, 0))],
            out_specs=pl.BlockSpec(memory_space=pl.ANY),
            scratch_shapes=[pltpu.VMEM((2, LOCAL_ROWS // GRAN, GRAN, SLOT_WORDS), U32),
                            pltpu.VMEM((SUB_BLOCK // GRAN, GRAN, SLOT_WORDS), U32),
                            pltpu.SemaphoreType.DMA((3,))]),
        out_shape=jax.ShapeDtypeStruct((n_rows // GRAN, GRAN, SLOT_WORDS), U32),
        compiler_params=pltpu.CompilerParams(dimension_semantics=("arbitrary",),
                                             vmem_limit_bytes=VMEM_LIMIT),
        name="moe_dispatch",
    )(seg_gran_f, local_off_f, global_off_f, tile_gran, tile_big, pad_n.astype(I32), pad_off.astype(I32),
      route, h)

    ys = pl.pallas_call(
        _expert_kernel,
        grid_spec=pltpu.PrefetchScalarGridSpec(
            num_scalar_prefetch=2,
            grid=(N_EXPERTS,),
            in_specs=[pl.BlockSpec(memory_space=pl.ANY),
                      pl.BlockSpec((1, D_MODEL, 2 * D_FF), lambda e, *_: (e, 0, 0)),
                      pl.BlockSpec((1, 1, 2 * D_FF), lambda e, *_: (e, 0, 0)),
                      pl.BlockSpec((1, D_FF, D_MODEL), lambda e, *_: (e, 0, 0)),
                      pl.BlockSpec((1, 1, D_MODEL), lambda e, *_: (e, 0, 0))],
            out_specs=pl.BlockSpec(memory_space=pl.ANY),
            scratch_shapes=[pltpu.VMEM((D_MODEL, 2 * D_FF), BF16), pltpu.VMEM((D_FF, D_MODEL), BF16),
                            pltpu.VMEM((2, ROW_BLOCK // GRAN, GRAN, SLOT_WORDS), U32),
                            pltpu.VMEM((2, ROW_BLOCK // GRAN, GRAN, SLOT_WORDS), U32),
                            pltpu.SemaphoreType.DMA((2,)), pltpu.SemaphoreType.DMA((2,))]),
        out_shape=jax.ShapeDtypeStruct((n_rows // GRAN, GRAN, SLOT_WORDS), U32),
        compiler_params=pltpu.CompilerParams(dimension_semantics=("arbitrary",),
                                             vmem_limit_bytes=VMEM_LIMIT),
        name="moe_experts",
    )(region_gran0, region_subs, xs, w_exp_in[0], b_exp_in[0].reshape(N_EXPERTS, 1, 2 * D_FF),
      w_exp_out[0], b_exp_out[0].reshape(N_EXPERTS, 1, D_MODEL))

    y_p, y_s = pl.pallas_call(
        functools.partial(_combine_kernel, n_ptiles),
        grid_spec=pltpu.PrefetchScalarGridSpec(
            num_scalar_prefetch=5,
            grid=(n_tiles,),
            in_specs=[pl.BlockSpec((T, LANES), lambda i, *_: (i, 0)),
                      pl.BlockSpec((T, D_MODEL), lambda i, *_: (i, 0)),
                      pl.BlockSpec((1, D_MODEL), lambda i, *_: (0, 0)),
                      pl.BlockSpec(memory_space=pl.ANY)],
            out_specs=[pl.BlockSpec((T, D_MODEL), lambda i, *_: (jnp.minimum(i, n_ptiles - 1), 0)),
                       pl.BlockSpec((T, D_MODEL), lambda i, *_: (jnp.maximum(i - n_ptiles, 0), 0))],
            scratch_shapes=[pltpu.VMEM((2, LOCAL_ROWS // GRAN, GRAN, SLOT_WORDS), U32), pltpu.SemaphoreType.DMA((2,))]),
        out_shape=(jax.ShapeDtypeStruct((n_prompt, D_MODEL), F32),
                   jax.ShapeDtypeStruct((n_sample, D_MODEL), F32)),
        compiler_params=pltpu.CompilerParams(dimension_semantics=("arbitrary",),
                                             vmem_limit_bytes=VMEM_LIMIT),
        name="moe_combine",
    )(seg_gran_f, local_off_f, global_off_f, tile_gran, tile_big, routet, x1, gf, ys)

    y_prompt = y_p.reshape(batch, seq, D_MODEL)
    y_sample = y_s.reshape(dec_batch, dec_seq, D_MODEL)
    new_win_k_prompt = kvwin[:, :, :D_KV].reshape(1, batch, WINDOW, N_KV, HEAD_DIM)
    new_win_v_prompt = kvwin[:, :, D_KV:].reshape(1, batch, WINDOW, N_KV, HEAD_DIM)
    new_win_k_sample = kvnew[:, :D_KV].reshape(1, dec_batch, dec_seq, N_KV, HEAD_DIM)
    new_win_v_sample = kvnew[:, D_KV:].reshape(1, dec_batch, dec_seq, N_KV, HEAD_DIM)
    new_sgu_v_sample = vn_s.reshape(1, dec_batch, dec_seq, D_GMLP)
    return (y_prompt, y_sample, new_win_k_prompt, new_win_v_prompt, new_win_k_sample,
            new_win_v_sample, new_sgu_v_sample)
```

```python
import functools
import math

import numpy as np
import jax
import jax.numpy as jnp
from jax import lax
from jax.experimental import pallas as pl
from jax.experimental.pallas import tpu as pltpu

F32 = jnp.float32
BF16 = jnp.bfloat16
I32 = jnp.int32
U32 = jnp.uint32

D_MODEL = 1024
HEAD_DIM = 64
N_HEADS = 16
N_KV = 2
REP = N_HEADS // N_KV
CHUNK = 64
WINDOW = 128
BAND = WINDOW + CHUNK
KEY_PAD = 256
D_ATT = N_HEADS * HEAD_DIM
D_KV = N_KV * HEAD_DIM
NUM_BUCKETS = 32
MAX_DISTANCE = 128
GMLP_CHUNK = 128
D_GMLP = 1024
N_GROUPS = 4
GROUP_W = D_GMLP // N_GROUPS
N_EXPERTS = 32
TOP_K = 4
D_FF = 1024
SWIGLU_LIMIT = 7.0
SWIGLU_ALPHA = 1.702
NORM_EPS = 1e-5
D_IN = D_ATT + 2 * D_KV + 2 * D_GMLP + 2 * D_MODEL
COL_KV = D_ATT
COL_U = COL_KV + 2 * D_KV
COL_VG = COL_U + D_GMLP
COL_GA = COL_VG + D_GMLP
COL_GB = COL_GA + D_MODEL
SQRT_HALF = float(np.sqrt(0.5))

LANES = 128
WORD_SUBLANES = 8
VMEM_LIMIT = 56 * 1024 * 1024

TOK_TILE = 256
LAYER_TILE = 512
GRAN = WORD_SUBLANES
SLOT_WORDS = D_MODEL // 2
FLAT_QUADS = 1
LOCAL_ROWS = TOP_K * TOK_TILE + N_EXPERTS * GRAN
ROW_BLOCK = 512
SUB_BLOCK = 128
PAD_BITS = (SUB_BLOCK // GRAN - 1).bit_length()
TILE_BITS = (LOCAL_ROWS // GRAN).bit_length()
BLOCK_DMA_PRIORITY = 1
ROUTE_ROWS = 16


def _const_spec(shape):
    nd = len(shape)
    return pl.BlockSpec(shape, lambda *_: (0,) * nd, pipeline_mode=pl.Buffered(1))


def _rms_norm(x, g):
    ms = jnp.mean(x * x, axis=-1, keepdims=True)
    return x * lax.rsqrt(ms + NORM_EPS) * g


def _gelu(x):
    return 0.5 * x * (1.0 + lax.erf(x * SQRT_HALF))


def _attend(q, k, v, bias, valid):
    logits = lax.dot_general(q, k, (((1,), (1,)), ((), ())), preferred_element_type=F32) + bias
    if valid is not None:
        logits = jnp.where(valid, logits, -jnp.inf)
    m = jnp.max(logits, axis=-1, keepdims=True)
    p = jnp.exp(logits - m)
    den = jnp.sum(p, axis=-1, keepdims=True)
    w = (p * (1.0 / den)).astype(BF16)
    return jnp.dot(w, v, preferred_element_type=F32)


def _attention_rows(zq, row0, n_rows, k_ext, v_ext, bias_ref, valid, oatt_ref, heads_per_unit):
    lane = lax.broadcasted_iota(I32, (1, LANES), 1)
    for g in range(N_KV):
        in_group = (lane >= g * HEAD_DIM) & (lane < (g + 1) * HEAD_DIM)
        qmask = jnp.where(in_group, HEAD_DIM ** -0.5, 0.0).astype(BF16)
        for part in range(REP // heads_per_unit):
            tiles = range(part * heads_per_unit, (part + 1) * heads_per_unit)
            q = jnp.concatenate([zq[row0:row0 + n_rows, t * LANES:(t + 1) * LANES] * qmask for t in tiles],
                                axis=0)
            b0 = part * heads_per_unit * n_rows
            o = _attend(q, k_ext, v_ext, bias_ref[g, b0:b0 + heads_per_unit * n_rows, :], valid)
            for u, t in enumerate(tiles):
                c0 = t * LANES + g * HEAD_DIM
                oatt_ref[row0:row0 + n_rows, c0:c0 + HEAD_DIM] = (
                    o[u * n_rows:(u + 1) * n_rows, g * HEAD_DIM:(g + 1) * HEAD_DIM].astype(BF16))


def _layer_tail(x, xn_b, oatt_ref, osgu_ref, w_in_ref, sguw_ref, sgub_ref, lng_ref, lnb_ref,
                woa_ref, wos_ref, wout_ref, g2_ref, wr_ref, br_ref, tri_ref, low_ref,
                x1_ref, h_ref, route_ref, routet_ref, cnt_ref, vn_ref):
    T = x.shape[0]
    u = _gelu(jnp.dot(xn_b, w_in_ref[:, COL_U:COL_VG], preferred_element_type=F32))
    vg = _gelu(jnp.dot(xn_b, w_in_ref[:, COL_VG:COL_GA], preferred_element_type=F32))
    mu = jnp.mean(vg, axis=-1, keepdims=True)
    var = jnp.mean(jnp.square(vg - mu), axis=-1, keepdims=True)
    vn = (vg - mu) * lax.rsqrt(var + NORM_EPS) * lng_ref[...] + lnb_ref[...]
    if vn_ref is not None:
        vn_ref[...] = vn
    vn_b = vn.astype(BF16)
    for j in range(T // GMLP_CHUNK):
        rows = slice(j * GMLP_CHUNK, (j + 1) * GMLP_CHUNK)
        for g in range(N_GROUPS):
            cols = slice(g * GROUP_W, (g + 1) * GROUP_W)
            mixed = jnp.dot(sguw_ref[g], vn_b[rows, cols], preferred_element_type=F32) + sgub_ref[g]
            osgu_ref[rows, cols] = (u[rows, cols] * mixed).astype(BF16)

    ga = jax.nn.sigmoid(jnp.dot(xn_b, w_in_ref[:, COL_GA:COL_GB], preferred_element_type=F32))
    merged = ga * jnp.dot(oatt_ref[...], woa_ref[...], preferred_element_type=F32)
    gb = jax.nn.sigmoid(jnp.dot(xn_b, w_in_ref[:, COL_GB:D_IN], preferred_element_type=F32))
    merged = merged + gb * jnp.dot(osgu_ref[...], wos_ref[...], preferred_element_type=F32)
    x1 = x + jnp.dot(merged.astype(BF16), wout_ref[...], preferred_element_type=F32)
    x1_ref[...] = x1

    h_b = _rms_norm(x1, g2_ref[...]).astype(BF16)
    h_ref[...] = h_b
    lt = lax.dot_general(wr_ref[...], h_b, (((1,), (1,)), ((), ())),
                         preferred_element_type=F32) + br_ref[...]
    for s in range(T // TOK_TILE):
        rec, cnt_b = _route_sort_tile(lt[:, s * TOK_TILE:(s + 1) * TOK_TILE], tri_ref, low_ref)
        route_ref[s] = rec
        rec_pad = jnp.concatenate([rec, jnp.zeros((LANES - ROUTE_ROWS, TOK_TILE), F32)], axis=0)
        routet_ref[s * TOK_TILE:(s + 1) * TOK_TILE, :] = jnp.transpose(rec_pad)
        cnt_ref[s] = cnt_b


def _route_sort_tile(lt, tri_ref, low_ref):
    T = lt.shape[1]
    e_iota = lax.broadcasted_iota(I32, (N_EXPERTS, T), 0).astype(F32)
    cur = lt
    vals, idxs = [], []
    for _ in range(TOP_K):
        m = jnp.max(cur, axis=0, keepdims=True)
        ik = jnp.min(jnp.where(cur == m, e_iota, float(N_EXPERTS)), axis=0, keepdims=True)
        vals.append(m)
        idxs.append(ik)
        cur = jnp.where(e_iota == ik, -jnp.inf, cur)
    exps = [jnp.exp(v - vals[0]) for v in vals]
    den = exps[0] + exps[1] + exps[2] + exps[3]
    gates = [e / den for e in exps]

    onehot = jnp.zeros((N_EXPERTS, T), F32)
    for ik in idxs:
        onehot = onehot + jnp.where(e_iota == ik, 1.0, 0.0)
    rank = jnp.dot(onehot.astype(BF16), tri_ref[...], preferred_element_type=F32)
    cnt = jnp.sum(onehot, axis=1, keepdims=True)
    cnt_b = jnp.broadcast_to(cnt, (N_EXPERTS, LANES))
    gran = jnp.floor((cnt_b + (GRAN - 1)) * (1.0 / GRAN))
    off = jnp.dot(low_ref[...], gran.astype(BF16), preferred_element_type=F32) * GRAN
    base = off[:, 0:1] + rank
    poss = [jnp.sum(jnp.where(e_iota == ik, base, 0.0), axis=0, keepdims=True) for ik in idxs]

    rec = jnp.concatenate(idxs + poss + gates
                          + [jnp.zeros((ROUTE_ROWS - 3 * TOP_K, T), F32)], axis=0)
    return rec, cnt_b


def _prompt_layer_kernel(x_ref, g1_ref, wq_ref, w_in_ref, bias_ref, sguw_ref, sgub_ref, lng_ref,
                         lnb_ref, woa_ref, wos_ref, wout_ref, g2_ref, wr_ref, br_ref, tri_ref, low_ref,
                         x1_ref, h_ref, route_ref, routet_ref, cnt_ref, kvwin_ref,
                         kvx_ref, oatt_ref, osgu_ref):
    T = LAYER_TILE
    j = pl.program_id(1)
    x = x_ref[0]
    xn_b = _rms_norm(x, g1_ref[...]).astype(BF16)
    zq = jnp.dot(xn_b, wq_ref[...], preferred_element_type=F32).astype(BF16)
    zkv = jnp.dot(xn_b, w_in_ref[:, COL_KV:COL_U], preferred_element_type=F32)

    @pl.when(j == pl.num_programs(1) - 1)
    def _():
        kvwin_ref[0] = zkv[T - WINDOW:, :]

    @pl.when(j == 0)
    def _():
        kvx_ref[0:WINDOW, :] = jnp.zeros((WINDOW, 2 * D_KV), BF16)

    @pl.when(j > 0)
    def _():
        kvx_ref[0:WINDOW, :] = kvx_ref[T:T + WINDOW, :]

    kvx_ref[WINDOW:, :] = zkv.astype(BF16)

    col = lax.broadcasted_iota(I32, (1, KEY_PAD), 1)
    key_pad = jnp.zeros((KEY_PAD - BAND, 2 * D_KV), BF16)
    for c in range(T // CHUNK):
        r0 = c * CHUNK
        kvb = jnp.concatenate([kvx_ref[r0:r0 + BAND, :], key_pad], axis=0)
        valid = ((col + (j * T + r0 - WINDOW)) >= 0) | (col >= BAND) if r0 < WINDOW else None
        _attention_rows(zq, r0, CHUNK, kvb[:, :D_KV], kvb[:, D_KV:], bias_ref, valid, oatt_ref,
                        heads_per_unit=REP)

    _layer_tail(x, xn_b, oatt_ref, osgu_ref, w_in_ref, sguw_ref, sgub_ref, lng_ref, lnb_ref,
                woa_ref, wos_ref, wout_ref, g2_ref, wr_ref, br_ref, tri_ref, low_ref,
                x1_ref, h_ref, route_ref, routet_ref, cnt_ref, None)


def _sample_layer_kernel(x1_in, h_in, route_in, routet_in, cnt_in,
                         x_ref, ck_ref, cv_ref, g1_ref, wq_ref, w_in_ref, bias_ref, sguw_ref, sgub_ref,
                         lng_ref, lnb_ref, woa_ref, wos_ref, wout_ref, g2_ref, wr_ref, br_ref, tri_ref,
                         low_ref,
                         x1_ref, h_ref, route_ref, routet_ref, cnt_ref, kvnew_ref, vn_ref,
                         oatt_ref, osgu_ref):
    del x1_in, h_in, route_in, routet_in, cnt_in
    T = LAYER_TILE
    nq = x_ref.shape[0] // ck_ref.shape[0]
    x = x_ref[...]
    xn_b = _rms_norm(x, g1_ref[...]).astype(BF16)
    zq = jnp.dot(xn_b, wq_ref[...], preferred_element_type=F32).astype(BF16)
    zkv = jnp.dot(xn_b, w_in_ref[:, COL_KV:COL_U], preferred_element_type=F32)
    kvnew_ref[...] = zkv
    zkv_b = zkv.astype(BF16)
    n_cache = ck_ref.shape[1]
    key_pad = jnp.zeros((KEY_PAD - n_cache - nq, D_KV), BF16)
    for b in range(T // nq):
        r0 = b * nq
        kk = jnp.concatenate([ck_ref[b].astype(BF16), zkv_b[r0:r0 + nq, :D_KV], key_pad], axis=0)
        vv = jnp.concatenate([cv_ref[b].astype(BF16), zkv_b[r0:r0 + nq, D_KV:], key_pad], axis=0)
        _attention_rows(zq, r0, nq, kk, vv, bias_ref, None, oatt_ref, heads_per_unit=REP)

    _layer_tail(x, xn_b, oatt_ref, osgu_ref, w_in_ref, sguw_ref, sgub_ref, lng_ref, lnb_ref,
                woa_ref, wos_ref, wout_ref, g2_ref, wr_ref, br_ref, tri_ref, low_ref,
                x1_ref, h_ref, route_ref, routet_ref, cnt_ref, vn_ref)


def _pack_slot_rows(x):
    lo = lax.bitcast_convert_type(x[:, :SLOT_WORDS].astype(BF16).astype(F32), U32)
    hi = lax.bitcast_convert_type(x[:, SLOT_WORDS:].astype(BF16).astype(F32), U32)
    return (lo >> 16) | hi


def _pack_slot_rows_exact(x):
    lo = lax.bitcast_convert_type(x[:, :SLOT_WORDS], U32)
    hi = lax.bitcast_convert_type(x[:, SLOT_WORDS:], U32)
    return (lo >> 16) | hi


def _unpack_slot_rows(w):
    lo = lax.bitcast_convert_type(w << 16, F32)
    hi = lax.bitcast_convert_type(w & jnp.uint32(0xFFFF0000), F32)
    return jnp.concatenate([lo, hi], axis=1).astype(BF16)


def _as_granules(x):
    return x.reshape(x.shape[0] // GRAN, GRAN, SLOT_WORDS)


def _as_rows(x):
    return x.reshape(x.shape[0] * GRAN, SLOT_WORDS)


def _segment_copies(n_gran, src_gran, dst_gran, bits, make_copy, act):
    for b in range(min(bits, 2)):
        @pl.when(((n_gran >> b) & 1) == 1)
        def _(b=b):
            done = n_gran & ((1 << b) - 1)
            act(make_copy(src_gran + done, dst_gran + done, 1 << b))

    if bits > 2:
        def quad(q, carry):
            done = (n_gran & 3) + 4 * q
            act(make_copy(src_gran + done, dst_gran + done, 4))
            return carry
        lax.fori_loop(0, n_gran >> 2, quad, 0)


def _start_tile_copies(tile, live, cnt_ref, loff_ref, goff_ref, big_ref, make_copy):
    for e in range(N_EXPERTS):
        t = tile * N_EXPERTS + e
        n = jnp.where(live, cnt_ref[t], 0)
        src, dst = loff_ref[t], goff_ref[t]
        for b in range(2):
            @pl.when(((n >> b) & 1) == 1)
            def _(b=b, n=n, src=src, dst=dst):
                done = n & ((1 << b) - 1)
                make_copy(src + done, dst + done, 1 << b).start()
        for q in range(FLAT_QUADS):
            @pl.when((n >> 2) > q)
            def _(q=q, n=n, src=src, dst=dst):
                done = (n & 3) + 4 * q
                make_copy(src + done, dst + done, 4).start()

    @pl.when(live & (big_ref[tile] > 0))
    def _():
        def rest(e, carry):
            t = tile * N_EXPERTS + e
            n, src, dst = cnt_ref[t], loff_ref[t], goff_ref[t]

            def quad(q, c):
                done = (n & 3) + 4 * q
                make_copy(src + done, dst + done, 4).start()
                return c
            lax.fori_loop(FLAT_QUADS, n >> 2, quad, 0)
            return carry
        lax.fori_loop(0, N_EXPERTS, rest, 0)


def _wait_granules(n_gran, bits, make_copy):
    for b in range(bits):
        @pl.when(((n_gran >> b) & 1) == 1)
        def _(b=b):
            make_copy(1 << b).wait()


def _dispatch_kernel(cnt_ref, loff_ref, goff_ref, tot_ref, big_ref, padn_ref, padoff_ref,
                     route_ref, h_ref, xs_hbm, stage_ref, zero_ref, sem):
    i = pl.program_id(0)
    last = pl.num_programs(0) - 1
    slot = lax.rem(i, 2)
    T = TOK_TILE

    def seg_copy(tile_slot):
        return lambda s, d, n: pltpu.make_async_copy(
            stage_ref.at[tile_slot, pl.ds(s, n)], xs_hbm.at[pl.ds(d, n)], sem.at[tile_slot])

    def wait_tile(tile, tile_slot):
        _wait_granules(tot_ref[tile], TILE_BITS, lambda n: pltpu.make_async_copy(
            stage_ref.at[tile_slot, pl.ds(0, n)], xs_hbm.at[pl.ds(0, n)], sem.at[tile_slot]))

    @pl.when(i > 1)
    def _():
        wait_tile(i - 2, slot)

    _start_tile_copies(jnp.maximum(i - 1, 0), i > 0, cnt_ref, loff_ref, goff_ref, big_ref, seg_copy(1 - slot))

    pos = route_ref[0, TOP_K:2 * TOP_K, :].astype(I32)
    r_iota = lax.broadcasted_iota(I32, (LOCAL_ROWS, T), 0)
    p = jnp.zeros((LOCAL_ROWS, T), F32)
    for k in range(TOP_K):
        p = jnp.where(r_iota == pos[k:k + 1, :], 1.0, p)
    stage_ref[slot] = _as_granules(
        _pack_slot_rows_exact(jnp.dot(p.astype(BF16), h_ref[...], preferred_element_type=F32)))

    @pl.when(i == last)
    def _():
        _start_tile_copies(i, i >= 0, cnt_ref, loff_ref, goff_ref, big_ref, seg_copy(slot))

        @pl.when(i > 0)
        def _():
            wait_tile(i - 1, 1 - slot)

        wait_tile(i, slot)
        zero_ref[...] = jnp.zeros(zero_ref.shape, U32)

        def pad_copy(s, d, n):
            return pltpu.make_async_copy(zero_ref.at[pl.ds(s, n)], xs_hbm.at[pl.ds(d, n)], sem.at[2])

        def for_pads(act):
            def body(e, carry):
                _segment_copies(padn_ref[e], 0, padoff_ref[e], PAD_BITS, pad_copy, act)
                return carry
            lax.fori_loop(0, N_EXPERTS, body, 0)

        for_pads(lambda cp: cp.start())
        for_pads(lambda cp: cp.wait())


def _expert_kernel(gran0_ref, nsub_ref, xs_hbm, w1_ref, b1_ref, w2_ref, b2_ref, ys_hbm,
                   w1b_ref, w2b_ref, xbuf_ref, ybuf_ref, sem_in, sem_out):
    e = pl.program_id(0)
    subs = ROW_BLOCK // SUB_BLOCK
    sub_gran = SUB_BLOCK // GRAN
    blk_gran = ROW_BLOCK // GRAN
    gran0 = gran0_ref[e]
    nsub = nsub_ref[e]
    nb_full = nsub // subs
    rem = nsub - nb_full * subs
    nb = nb_full + jnp.where(rem > 0, 1, 0)
    last_slot = lax.rem(nb + 1, 2)

    def in_copy(k, slot):
        return pltpu.make_async_copy(xs_hbm.at[pl.ds(gran0 + k * blk_gran, blk_gran)], xbuf_ref.at[slot],
                                     sem_in.at[slot])

    def out_copy(k, slot, m_sub):
        n = m_sub * sub_gran
        return pltpu.make_async_copy(ybuf_ref.at[slot, pl.ds(0, n)], ys_hbm.at[pl.ds(gran0 + k * blk_gran, n)],
                                     sem_out.at[slot])

    @pl.when(nb > 0)
    def _():
        in_copy(0, 0).start(priority=BLOCK_DMA_PRIORITY)

    w1b_ref[...] = w1_ref[0].astype(BF16)
    w2b_ref[...] = w2_ref[0].astype(BF16)

    def process(k, m_sub):
        n = m_sub * sub_gran
        slot = lax.rem(k, 2)

        @pl.when(k + 1 < nb)
        def _():
            in_copy(k + 1, 1 - slot).start(priority=BLOCK_DMA_PRIORITY)

        in_copy(k, slot).wait()

        @pl.when(k >= 2)
        def _():
            out_copy(k - 2, slot, subs).wait()

        x = _unpack_slot_rows(_as_rows(xbuf_ref[slot, 0:n]))
        h1 = jnp.dot(x, w1b_ref[...], preferred_element_type=F32) + b1_ref[0]
        gate = jnp.minimum(h1[:, :D_FF], SWIGLU_LIMIT)
        up = jnp.clip(h1[:, D_FF:], -SWIGLU_LIMIT, SWIGLU_LIMIT)
        act = gate * jax.nn.sigmoid(SWIGLU_ALPHA * gate) * (up + 1.0)
        y = jnp.dot(act.astype(BF16), w2b_ref[...], preferred_element_type=F32) + b2_ref[0]
        ybuf_ref[slot, 0:n] = _as_granules(_pack_slot_rows(y))
        out_copy(k, slot, m_sub).start(priority=BLOCK_DMA_PRIORITY)

    def full_block(k, carry):
        process(k, subs)
        return carry

    lax.fori_loop(0, nb_full, full_block, 0)

    for m_sub in range(1, subs):
        @pl.when(rem == m_sub)
        def _(m_sub=m_sub):
            process(nb_full, m_sub)

    @pl.when(nb >= 2)
    def _():
        out_copy(nb - 2, 1 - last_slot, subs).wait()

    @pl.when((nb >= 1) & (rem == 0))
    def _():
        out_copy(nb - 1, last_slot, subs).wait()

    for m_sub in range(1, subs):
        @pl.when(rem == m_sub)
        def _(m_sub=m_sub):
            out_copy(nb - 1, last_slot, m_sub).wait()


def _combine_kernel(n_prompt_tiles, cnt_ref, loff_ref, goff_ref, tot_ref, big_ref,
                    routet_ref, x1_ref, gf_ref, ys_hbm, yp_ref, ysm_ref, stage_ref, sem):
    i = pl.program_id(0)
    n_steps = pl.num_programs(0)
    slot = lax.rem(i, 2)
    T = TOK_TILE

    def seg_copy(tile_slot):
        return lambda s, d, n: pltpu.make_async_copy(
            ys_hbm.at[pl.ds(d, n)], stage_ref.at[tile_slot, pl.ds(s, n)], sem.at[tile_slot])

    @pl.when(i == 0)
    def _():
        stage_ref[...] = jnp.zeros(stage_ref.shape, U32)
        _start_tile_copies(i, i >= 0, cnt_ref, loff_ref, goff_ref, big_ref, seg_copy(slot))

    _start_tile_copies(jnp.minimum(i + 1, n_steps - 1), i + 1 < n_steps, cnt_ref, loff_ref, goff_ref, big_ref,
                       seg_copy(1 - slot))

    rt = routet_ref[...]
    l_iota = lax.broadcasted_iota(I32, (T, LOCAL_ROWS), 1)
    pw = jnp.zeros((T, LOCAL_ROWS), F32)
    for k in range(TOP_K):
        pos_k = rt[:, TOP_K + k:TOP_K + k + 1].astype(I32)
        pw = jnp.where(l_iota == pos_k, rt[:, 2 * TOP_K + k:2 * TOP_K + k + 1], pw)

    _wait_granules(tot_ref[i], TILE_BITS, lambda n: pltpu.make_async_copy(
        ys_hbm.at[pl.ds(0, n)], stage_ref.at[slot, pl.ds(0, n)], sem.at[slot]))
    moe = jnp.dot(pw.astype(BF16), _unpack_slot_rows(_as_rows(stage_ref[slot])), preferred_element_type=F32)
    y = _rms_norm(x1_ref[...] + moe, gf_ref[...])

    @pl.when(i < n_prompt_tiles)
    def _():
        yp_ref[...] = y

    @pl.when(i >= n_prompt_tiles)
    def _():
        ysm_ref[...] = y


def _t5_bucket(rel):
    half = NUM_BUCKETS // 2
    max_exact = half // 2
    ret = jnp.where(rel > 0, half, 0)
    n = jnp.abs(rel)
    nf = jnp.maximum(n, 1).astype(F32)
    large = max_exact + (jnp.log(nf / max_exact) / math.log(MAX_DISTANCE / max_exact)
                         * (half - max_exact)).astype(jnp.int32)
    large = jnp.minimum(large, half - 1)
    return ret + jnp.where(n < max_exact, n, large)


def _stacked_bias(table, sinks, q_pos, k_pos):
    nq, nk = q_pos.shape[0], k_pos.shape[0]
    bucket = _t5_bucket(k_pos[None, :] - q_pos[:, None])
    onehot = (bucket[:, :, None] == jnp.arange(NUM_BUCKETS)).astype(F32)
    bias = jnp.einsum('qkb,bh->hqk', onehot, table.astype(F32), precision=lax.Precision.HIGHEST)
    bias = bias.reshape(N_KV, REP * nq, nk)
    sink = jnp.repeat(sinks.astype(F32).reshape(N_KV, REP, 1), nq, axis=2).reshape(N_KV, REP * nq, 1)
    tail = jnp.full((N_KV, REP * nq, KEY_PAD - nk - 1), -jnp.inf, F32)
    return jnp.concatenate([bias, sink, tail], axis=-1)


def kernel(x_prompt, x_sample, cache_win_k, cache_win_v, norm1_g, w_in, attn_sinks, rel_bias_table, sgu_ln_g, sgu_ln_b, sgu_w, sgu_b, w_o_att, w_o_sgu, w_out, norm2_g, w_router, b_router, w_exp_in, b_exp_in, w_exp_out, b_exp_out, final_norm_g):
    batch, seq, _ = x_prompt.shape
    dec_batch, dec_seq, _ = x_sample.shape
    cache_rows = cache_win_k.shape[2]
    assert x_prompt.shape[2] == D_MODEL and w_in.shape == (1, D_MODEL, D_IN)
    assert seq % LAYER_TILE == 0 and LAYER_TILE % TOK_TILE == 0 and TOK_TILE % GMLP_CHUNK == 0
    assert LAYER_TILE >= WINDOW and LAYER_TILE % dec_seq == 0 and (dec_batch * dec_seq) % LAYER_TILE == 0
    assert dec_seq <= GMLP_CHUNK and GMLP_CHUNK % dec_seq == 0 and cache_rows == WINDOW
    T = TOK_TILE
    n_prompt = batch * seq
    n_sample = dec_batch * dec_seq
    n_tok = n_prompt + n_sample
    LT = LAYER_TILE
    sorts_per_step = LT // T
    tiles_per_seq = seq // LT
    n_psteps = n_prompt // LT
    n_ssteps = n_sample // LT
    n_ptiles = n_prompt // T
    n_tiles = n_tok // T
    seqs_per_tile = LT // dec_seq

    w_in_b = w_in[0].astype(BF16)
    w_q_b = w_in_b[:, :D_ATT].reshape(D_MODEL, N_KV, REP, HEAD_DIM).transpose(0, 2, 1, 3).reshape(
        D_MODEL, D_ATT)
    woa_b = w_o_att[0].reshape(N_KV, REP, HEAD_DIM, D_MODEL).transpose(1, 0, 2, 3).reshape(
        D_ATT, D_MODEL).astype(BF16)
    wos_b = w_o_sgu[0].astype(BF16)
    wout_b = w_out[0].astype(BF16)
    wr_b = jnp.transpose(w_router[0]).astype(BF16)
    br_col = b_router[0].astype(F32).reshape(N_EXPERTS, 1)
    g1 = norm1_g[0].reshape(1, D_MODEL)
    g2 = norm2_g[0].reshape(1, D_MODEL)
    gf = final_norm_g.reshape(1, D_MODEL)
    lng = sgu_ln_g[0].reshape(1, D_GMLP)
    lnb = sgu_ln_b[0].reshape(1, D_GMLP)
    tril = jnp.tril(jnp.ones((GMLP_CHUNK, GMLP_CHUNK), dtype=bool))
    sguw_p = jnp.where(tril[None], sgu_w[0], 0).astype(BF16)
    sgub_p = jnp.broadcast_to(sgu_b[0][:, :, None], (N_GROUPS, GMLP_CHUNK, GROUP_W)).astype(F32)
    reps = GMLP_CHUNK // dec_seq
    corner = jnp.where(tril[None, :dec_seq, :dec_seq], sgu_w[0][:, :dec_seq, :dec_seq], 0)
    sguw_s = jnp.einsum('ab,gij->gaibj', jnp.eye(reps, dtype=F32), corner).reshape(
        N_GROUPS, GMLP_CHUNK, GMLP_CHUNK).astype(BF16)
    sgub_s = jnp.broadcast_to(jnp.tile(sgu_b[0][:, :dec_seq], (1, reps))[:, :, None],
                              (N_GROUPS, GMLP_CHUNK, GROUP_W)).astype(F32)
    bias_p = _stacked_bias(rel_bias_table, attn_sinks[0], jnp.arange(CHUNK) + WINDOW, jnp.arange(BAND))
    bias_s = _stacked_bias(rel_bias_table, attn_sinks[0], cache_rows + jnp.arange(dec_seq),
                           jnp.arange(cache_rows + dec_seq))
    tri = jnp.triu(jnp.ones((T, T), F32), k=1).astype(BF16)
    low = jnp.tril(jnp.ones((N_EXPERTS, N_EXPERTS), F32), k=-1).astype(BF16)

    layer_out_shapes = (
        jax.ShapeDtypeStruct((n_tok, D_MODEL), F32),
        jax.ShapeDtypeStruct((n_tok, D_MODEL), BF16),
        jax.ShapeDtypeStruct((n_tiles, ROUTE_ROWS, T), F32),
        jax.ShapeDtypeStruct((n_tok, LANES), F32),
        jax.ShapeDtypeStruct((n_tiles, N_EXPERTS, LANES), F32),
    )
    shared_consts = (w_q_b, w_in_b)
    tail_consts_p = (sguw_p, sgub_p, lng, lnb, woa_b, wos_b, wout_b, g2, wr_b, br_col, tri, low)
    tail_consts_s = (sguw_s, sgub_s, lng, lnb, woa_b, wos_b, wout_b, g2, wr_b, br_col, tri, low)

    def tile_specs(tile_of):
        return [
            pl.BlockSpec((LT, D_MODEL), lambda *g: (tile_of(*g), 0)),
            pl.BlockSpec((LT, D_MODEL), lambda *g: (tile_of(*g), 0)),
            pl.BlockSpec((sorts_per_step, ROUTE_ROWS, T), lambda *g: (tile_of(*g), 0, 0)),
            pl.BlockSpec((LT, LANES), lambda *g: (tile_of(*g), 0)),
            pl.BlockSpec((sorts_per_step, N_EXPERTS, LANES), lambda *g: (tile_of(*g), 0, 0)),
        ]

    prompt_consts = (g1,) + shared_consts + (bias_p,) + tail_consts_p
    x1, h, route, routet, cnt, kvwin = pl.pallas_call(
        _prompt_layer_kernel,
        grid=(batch, tiles_per_seq),
        in_specs=[pl.BlockSpec((1, LT, D_MODEL), lambda b, j: (b, j, 0))]
                 + [_const_spec(c.shape) for c in prompt_consts],
        out_specs=tile_specs(lambda b, j: b * tiles_per_seq + j)
                  + [pl.BlockSpec((1, WINDOW, 2 * D_KV), lambda b, j: (b, 0, 0))],
        out_shape=layer_out_shapes + (jax.ShapeDtypeStruct((batch, WINDOW, 2 * D_KV), F32),),
        scratch_shapes=[pltpu.VMEM((LT + WINDOW, 2 * D_KV), BF16),
                        pltpu.VMEM((LT, D_ATT), BF16),
                        pltpu.VMEM((LT, D_GMLP), BF16)],
        compiler_params=pltpu.CompilerParams(dimension_semantics=("arbitrary", "arbitrary"),
                                             vmem_limit_bytes=VMEM_LIMIT),
        name="layer_prompt",
    )(x_prompt, *prompt_consts)

    xs_flat = x_sample.reshape(n_sample, D_MODEL)
    ck = cache_win_k[0].reshape(dec_batch, cache_rows, D_KV)
    cv = cache_win_v[0].reshape(dec_batch, cache_rows, D_KV)
    sample_consts = (g1,) + shared_consts + (bias_s,) + tail_consts_s
    any_spec = pl.BlockSpec(memory_space=pl.ANY)
    x1, h, route, routet, cnt, kvnew, vn_s = pl.pallas_call(
        _sample_layer_kernel,
        grid=(n_ssteps,),
        in_specs=[any_spec] * 5
                 + [pl.BlockSpec((LT, D_MODEL), lambda i: (i, 0)),
                    pl.BlockSpec((seqs_per_tile, cache_rows, D_KV), lambda i: (i, 0, 0)),
                    pl.BlockSpec((seqs_per_tile, cache_rows, D_KV), lambda i: (i, 0, 0))]
                 + [_const_spec(c.shape) for c in sample_consts],
        out_specs=tile_specs(lambda i: n_psteps + i)
                  + [pl.BlockSpec((LT, 2 * D_KV), lambda i: (i, 0)),
                     pl.BlockSpec((LT, D_GMLP), lambda i: (i, 0))],
        out_shape=layer_out_shapes + (jax.ShapeDtypeStruct((n_sample, 2 * D_KV), F32),
                                      jax.ShapeDtypeStruct((n_sample, D_GMLP), F32)),
        scratch_shapes=[pltpu.VMEM((LT, D_ATT), BF16), pltpu.VMEM((LT, D_GMLP), BF16)],
        input_output_aliases={0: 0, 1: 1, 2: 2, 3: 3, 4: 4},
        compiler_params=pltpu.CompilerParams(dimension_semantics=("arbitrary",),
                                             vmem_limit_bytes=VMEM_LIMIT),
        name="layer_sample",
    )(x1, h, route, routet, cnt, xs_flat, ck, cv, *sample_consts)

    blk_gran = SUB_BLOCK // GRAN
    counts =cnt[:, :, 0].astype(I32)
    seg_gran = (counts + (GRAN - 1)) // GRAN
    local_off = jnp.cumsum(seg_gran, axis=1) - seg_gran
    tot_gran = jnp.sum(seg_gran, axis=0)
    ptot_gran = (tot_gran + (blk_gran - 1)) // blk_gran * blk_gran
    pend_gran = jnp.cumsum(ptot_gran)
    gstart = pend_gran - ptot_gran
    global_off = gstart[None, :] + jnp.cumsum(seg_gran, axis=0) - seg_gran
    pad_n = ptot_gran - tot_gran
    pad_off = gstart + tot_gran
    n_rows = -(-(TOP_K * n_tok + n_tiles * N_EXPERTS * (GRAN - 1) + N_EXPERTS * (SUB_BLOCK - GRAN))
               // SUB_BLOCK) * SUB_BLOCK + ROW_BLOCK
    region_gran0 = gstart.astype(I32)
    region_subs = (ptot_gran // blk_gran).astype(I32)
    tile_gran = jnp.sum(seg_gran, axis=1).astype(I32)
    tile_big = jnp.any(seg_gran >= 4 * (FLAT_QUADS + 1), axis=1).astype(I32)
    seg_gran_f = seg_gran.reshape(-1).astype(I32)
    local_off_f = local_off.reshape(-1).astype(I32)
    global_off_f = global_off.reshape(-1).astype(I32)

    xs = pl.pallas_call(
        _dispatch_kernel,
        grid_spec=pltpu.PrefetchScalarGridSpec(
            num_scalar_prefetch=7,
            grid=(n_tiles,),
            in_specs=[pl.BlockSpec((1, ROUTE_ROWS, T), lambda i, *_: (i, 0, 0)),
                      pl.BlockSpec((T, D_MODEL), lambda i, *_: (i, 0))],
            out_specs=pl.BlockSpec(memory_space=pl.ANY),
            scratch_shapes=[pltpu.VMEM((2, LOCAL_ROWS // GRAN, GRAN, SLOT_WORDS), U32),
                            pltpu.VMEM((SUB_BLOCK // GRAN, GRAN, SLOT_WORDS), U32),
                            pltpu.SemaphoreType.DMA((3,))]),
        out_shape=jax.ShapeDtypeStruct((n_rows // GRAN, GRAN, SLOT_WORDS), U32),
        compiler_params=pltpu.CompilerParams(dimension_semantics=("arbitrary",),
                                             vmem_limit_bytes=VMEM_LIMIT),
        name="moe_dispatch",
    )(seg_gran_f, local_off_f, global_off_f, tile_gran, tile_big, pad_n.astype(I32), pad_off.astype(I32),
      route, h)

    ys = pl.pallas_call(
        _expert_kernel,
        grid_spec=pltpu.PrefetchScalarGridSpec(
            num_scalar_prefetch=2,
            grid=(N_EXPERTS,),
            in_specs=[pl.BlockSpec(memory_space=pl.ANY),
                      pl.BlockSpec((1, D_MODEL, 2 * D_FF), lambda e, *_: (e, 0, 0)),
                      pl.BlockSpec((1, 1, 2 * D_FF), lambda e, *_: (e, 0, 0)),
                      pl.BlockSpec((1, D_FF, D_MODEL), lambda e, *_: (e, 0, 0)),
                      pl.BlockSpec((1, 1, D_MODEL), lambda e, *_: (e, 0, 0))],
            out_specs=pl.BlockSpec(memory_space=pl.ANY),
            scratch_shapes=[pltpu.VMEM((D_MODEL, 2 * D_FF), BF16), pltpu.VMEM((D_FF, D_MODEL), BF16),
                            pltpu.VMEM((2, ROW_BLOCK // GRAN, GRAN, SLOT_WORDS), U32),
                            pltpu.VMEM((2, ROW_BLOCK // GRAN, GRAN, SLOT_WORDS), U32),
                            pltpu.SemaphoreType.DMA((2,)), pltpu.SemaphoreType.DMA((2,))]),
        out_shape=jax.ShapeDtypeStruct((n_rows // GRAN, GRAN, SLOT_WORDS), U32),
        compiler_params=pltpu.CompilerParams(dimension_semantics=("arbitrary",),
                                             vmem_limit_bytes=VMEM_LIMIT),
        name="moe_experts",
    )(region_gran0, region_subs, xs, w_exp_in[0], b_exp_in[0].reshape(N_EXPERTS, 1, 2 * D_FF),
      w_exp_out[0], b_exp_out[0].reshape(N_EXPERTS, 1, D_MODEL))

    y_p, y_s = pl.pallas_call(
        functools.partial(_combine_kernel, n_ptiles),
        grid_spec=pltpu.PrefetchScalarGridSpec(
            num_scalar_prefetch=5,
            grid=(n_tiles,),
            in_specs=[pl.BlockSpec((T, LANES), lambda i, *_: (i, 0)),
                      pl.BlockSpec((T, D_MODEL), lambda i, *_: (i, 0)),
                      pl.BlockSpec((1, D_MODEL), lambda i, *_: (0, 0)),
                      pl.BlockSpec(memory_space=pl.ANY)],
            out_specs=[pl.BlockSpec((T, D_MODEL), lambda i, *_: (jnp.minimum(i, n_ptiles - 1), 0)),
                       pl.BlockSpec((T, D_MODEL), lambda i, *_: (jnp.maximum(i - n_ptiles, 0), 0))],
            scratch_shapes=[pltpu.VMEM((2, LOCAL_ROWS // GRAN, GRAN, SLOT_WORDS), U32), pltpu.SemaphoreType.DMA((2,))]),
        out_shape=(jax.ShapeDtypeStruct((n_prompt, D_MODEL), F32),
                   jax.ShapeDtypeStruct((n_sample, D_MODEL), F32)),
        compiler_params=pltpu.CompilerParams(dimension_semantics=("arbitrary",),
                                             vmem_limit_bytes=VMEM_LIMIT),
        name="moe_combine",
    )(seg_gran_f, local_off_f, global_off_f, tile_gran, tile_big, routet, x1, gf, ys)

    y_prompt = y_p.reshape(batch, seq, D_MODEL)
    y_sample = y_s.reshape(dec_batch, dec_seq, D_MODEL)
    new_win_k_prompt = kvwin[:, :, :D_KV].reshape(1, batch, WINDOW, N_KV, HEAD_DIM)
    new_win_v_prompt = kvwin[:, :, D_KV:].reshape(1, batch, WINDOW, N_KV, HEAD_DIM)
    new_win_k_sample = kvnew[:, :D_KV].reshape(1, dec_batch, dec_seq, N_KV, HEAD_DIM)
    new_win_v_sample = kvnew[:, D_KV:].reshape(1, dec_batch, dec_seq, N_KV, HEAD_DIM)
    new_sgu_v_sample = vn_s.reshape(1, dec_batch, dec_seq, D_GMLP)
    return (y_prompt, y_sample, new_win_k_prompt, new_win_v_prompt, new_win_k_sample,
            new_win_v_sample, new_sgu_v_sample)
```

```python
import functools
import math

import numpy as np
import jax
import jax.numpy as jnp
from jax import lax
from jax.experimental import pallas as pl
from jax.experimental.pallas import tpu as pltpu

F32 = jnp.float32
BF16 = jnp.bfloat16
I32 = jnp.int32
U32 = jnp.uint32

D_MODEL = 1024
HEAD_DIM = 64
N_HEADS = 16
N_KV = 2
REP = N_HEADS // N_KV
CHUNK = 64
WINDOW = 128
BAND = WINDOW + CHUNK
KEY_PAD = 256
D_ATT = N_HEADS * HEAD_DIM
D_KV = N_KV * HEAD_DIM
NUM_BUCKETS = 32
MAX_DISTANCE = 128
GMLP_CHUNK = 128
D_GMLP = 1024
N_GROUPS = 4
GROUP_W = D_GMLP // N_GROUPS
N_EXPERTS = 32
TOP_K = 4
D_FF = 1024
SWIGLU_LIMIT = 7.0
SWIGLU_ALPHA = 1.702
NORM_EPS = 1e-5
D_IN = D_ATT + 2 * D_KV + 2 * D_GMLP + 2 * D_MODEL
COL_KV = D_ATT
COL_U = COL_KV + 2 * D_KV
COL_VG = COL_U + D_GMLP
COL_GA = COL_VG + D_GMLP
COL_GB = COL_GA + D_MODEL
SQRT_HALF = float(np.sqrt(0.5))

LANES = 128
WORD_SUBLANES = 8
VMEM_LIMIT = 56 * 1024 * 1024

TOK_TILE = 256
LAYER_TILE = 512
GRAN = WORD_SUBLANES
SLOT_WORDS = D_MODEL // 2
FLAT_QUADS = 1
LOCAL_ROWS = TOP_K * TOK_TILE + N_EXPERTS * GRAN
ROW_BLOCK = 512
SUB_BLOCK = 128
PAD_BITS = (SUB_BLOCK // GRAN - 1).bit_length()
TILE_BITS = (LOCAL_ROWS // GRAN).bit_length()
BLOCK_DMA_PRIORITY = 1
ROUTE_ROWS = 16


def _const_spec(shape):
    nd = len(shape)
    return pl.BlockSpec(shape, lambda *_: (0,) * nd, pipeline_mode=pl.Buffered(1))


def _rms_norm(x, g):
    ms = jnp.mean(x * x, axis=-1, keepdims=True)
    return x * lax.rsqrt(ms + NORM_EPS) * g


def _gelu(x):
    return 0.5 * x * (1.0 + lax.erf(x * SQRT_HALF))


def _attend(q, k, v, bias, valid):
    logits = lax.dot_general(q, k, (((1,), (1,)), ((), ())), preferred_element_type=F32) + bias
    if valid is not None:
        logits = jnp.where(valid, logits, -jnp.inf)
    m = jnp.max(logits, axis=-1, keepdims=True)
    p = jnp.exp(logits - m)
    den = jnp.sum(p, axis=-1, keepdims=True)
    w = (p * (1.0 / den)).astype(BF16)
    return jnp.dot(w, v, preferred_element_type=F32)


def _attention_rows(zq, row0, n_rows, k_ext, v_ext, bias_ref, valid, oatt_ref):
    lane = lax.broadcasted_iota(I32, (1, LANES), 1)
    stack = []
    for g in range(N_KV):
        in_group = (lane >= g * HEAD_DIM) & (lane < (g + 1) * HEAD_DIM)
        qmask = jnp.where(in_group, HEAD_DIM ** -0.5, 0.0).astype(BF16)
        stack += [zq[row0:row0 + n_rows, t * LANES:(t + 1) * LANES] * qmask for t in range(REP)]
    bias = bias_ref[...].reshape(N_HEADS * n_rows, KEY_PAD)
    o = _attend(jnp.concatenate(stack, axis=0), k_ext, v_ext, bias, valid)
    for g in range(N_KV):
        for t in range(REP):
            r = (g * REP + t) * n_rows
            c0 = t * LANES + g * HEAD_DIM
            oatt_ref[row0:row0 + n_rows, c0:c0 + HEAD_DIM] = (
                o[r:r + n_rows, g * HEAD_DIM:(g + 1) * HEAD_DIM].astype(BF16))


def _layer_tail(x, xn_b, oatt_ref, osgu_ref, w_in_ref, sguw_ref, sgub_ref, lng_ref, lnb_ref,
                woa_ref, wos_ref, wout_ref, g2_ref, wr_ref, br_ref, tri_ref, low_ref,
                x1_ref, h_ref, route_ref, routet_ref, cnt_ref, vn_ref):
    T = x.shape[0]
    u = _gelu(jnp.dot(xn_b, w_in_ref[:, COL_U:COL_VG], preferred_element_type=F32))
    vg = _gelu(jnp.dot(xn_b, w_in_ref[:, COL_VG:COL_GA], preferred_element_type=F32))
    mu = jnp.mean(vg, axis=-1, keepdims=True)
    var = jnp.mean(jnp.square(vg - mu), axis=-1, keepdims=True)
    vn = (vg - mu) * lax.rsqrt(var + NORM_EPS) * lng_ref[...] + lnb_ref[...]
    if vn_ref is not None:
        vn_ref[...] = vn
    vn_b = vn.astype(BF16)
    for j in range(T // GMLP_CHUNK):
        rows = slice(j * GMLP_CHUNK, (j + 1) * GMLP_CHUNK)
        for g in range(N_GROUPS):
            cols = slice(g * GROUP_W, (g + 1) * GROUP_W)
            mixed = jnp.dot(sguw_ref[g], vn_b[rows, cols], preferred_element_type=F32) + sgub_ref[g]
            osgu_ref[rows, cols] = (u[rows, cols] * mixed).astype(BF16)

    ga = jax.nn.sigmoid(jnp.dot(xn_b, w_in_ref[:, COL_GA:COL_GB], preferred_element_type=F32))
    merged = ga * jnp.dot(oatt_ref[...], woa_ref[...], preferred_element_type=F32)
    gb = jax.nn.sigmoid(jnp.dot(xn_b, w_in_ref[:, COL_GB:D_IN], preferred_element_type=F32))
    merged = merged + gb * jnp.dot(osgu_ref[...], wos_ref[...], preferred_element_type=F32)
    x1 = x + jnp.dot(merged.astype(BF16), wout_ref[...], preferred_element_type=F32)
    x1_ref[...] = x1

    h_b = _rms_norm(x1, g2_ref[...]).astype(BF16)
    h_ref[...] = h_b
    lt = lax.dot_general(wr_ref[...], h_b, (((1,), (1,)), ((), ())),
                         preferred_element_type=F32) + br_ref[...]
    for s in range(T // TOK_TILE):
        rec, cnt_b = _route_sort_tile(lt[:, s * TOK_TILE:(s + 1) * TOK_TILE], tri_ref, low_ref)
        route_ref[s] = rec
        rec_pad = jnp.concatenate([rec, jnp.zeros((LANES - ROUTE_ROWS, TOK_TILE), F32)], axis=0)
        routet_ref[s * TOK_TILE:(s + 1) * TOK_TILE, :] = jnp.transpose(rec_pad)
        cnt_ref[s] = cnt_b


def _route_sort_tile(lt, tri_ref, low_ref):
    T = lt.shape[1]
    e_iota = lax.broadcasted_iota(I32, (N_EXPERTS, T), 0).astype(F32)
    cur = lt
    vals, idxs = [], []
    for _ in range(TOP_K):
        m = jnp.max(cur, axis=0, keepdims=True)
        ik = jnp.min(jnp.where(cur == m, e_iota, float(N_EXPERTS)), axis=0, keepdims=True)
        vals.append(m)
        idxs.append(ik)
        cur = jnp.where(e_iota == ik, -jnp.inf, cur)
    exps = [jnp.exp(v - vals[0]) for v in vals]
    den = exps[0] + exps[1] + exps[2] + exps[3]
    gates = [e / den for e in exps]

    onehot = jnp.zeros((N_EXPERTS, T), F32)
    for ik in idxs:
        onehot = onehot + jnp.where(e_iota == ik, 1.0, 0.0)
    rank = jnp.dot(onehot.astype(BF16), tri_ref[...], preferred_element_type=F32)
    cnt = jnp.sum(onehot, axis=1, keepdims=True)
    cnt_b = jnp.broadcast_to(cnt, (N_EXPERTS, LANES))
    gran = jnp.floor((cnt_b + (GRAN - 1)) * (1.0 / GRAN))
    off = jnp.dot(low_ref[...], gran.astype(BF16), preferred_element_type=F32) * GRAN
    base = off[:, 0:1] + rank
    poss = [jnp.sum(jnp.where(e_iota == ik, base, 0.0), axis=0, keepdims=True) for ik in idxs]

    rec = jnp.concatenate(idxs + poss + gates
                          + [jnp.zeros((ROUTE_ROWS - 3 * TOP_K, T), F32)], axis=0)
    return rec, cnt_b


def _prompt_layer_kernel(x_ref, g1_ref, wq_ref, w_in_ref, bias_ref, sguw_ref, sgub_ref, lng_ref,
                         lnb_ref, woa_ref, wos_ref, wout_ref, g2_ref, wr_ref, br_ref, tri_ref, low_ref,
                         x1_ref, h_ref, route_ref, routet_ref, cnt_ref, kvwin_ref,
                         kvx_ref, oatt_ref, osgu_ref):
    T = LAYER_TILE
    j = pl.program_id(1)
    x = x_ref[0]
    xn_b = _rms_norm(x, g1_ref[...]).astype(BF16)
    zq = jnp.dot(xn_b, wq_ref[...], preferred_element_type=F32).astype(BF16)
    zkv = jnp.dot(xn_b, w_in_ref[:, COL_KV:COL_U], preferred_element_type=F32)

    @pl.when(j == pl.num_programs(1) - 1)
    def _():
        kvwin_ref[0] = zkv[T - WINDOW:, :]

    @pl.when(j == 0)
    def _():
        kvx_ref[0:WINDOW, :] = jnp.zeros((WINDOW, 2 * D_KV), BF16)

    @pl.when(j > 0)
    def _():
        kvx_ref[0:WINDOW, :] = kvx_ref[T:T + WINDOW, :]

    kvx_ref[WINDOW:, :] = zkv.astype(BF16)

    col = lax.broadcasted_iota(I32, (1, KEY_PAD), 1)
    key_pad = jnp.zeros((KEY_PAD - BAND, 2 * D_KV), BF16)
    for c in range(T // CHUNK):
        r0 = c * CHUNK
        kvb = jnp.concatenate([kvx_ref[r0:r0 + BAND, :], key_pad], axis=0)
        valid = ((col + (j * T + r0 - WINDOW)) >= 0) | (col >= BAND) if r0 < WINDOW else None
        _attention_rows(zq, r0, CHUNK, kvb[:, :D_KV], kvb[:, D_KV:], bias_ref, valid, oatt_ref)

    _layer_tail(x, xn_b, oatt_ref, osgu_ref, w_in_ref, sguw_ref, sgub_ref, lng_ref, lnb_ref,
                woa_ref, wos_ref, wout_ref, g2_ref, wr_ref, br_ref, tri_ref, low_ref,
                x1_ref, h_ref, route_ref, routet_ref, cnt_ref, None)


def _sample_layer_kernel(x1_in, h_in, route_in, routet_in, cnt_in,
                         x_ref, ck_ref, cv_ref, g1_ref, wq_ref, w_in_ref, bias_ref, sguw_ref, sgub_ref,
                         lng_ref, lnb_ref, woa_ref, wos_ref, wout_ref, g2_ref, wr_ref, br_ref, tri_ref,
                         low_ref,
                         x1_ref, h_ref, route_ref, routet_ref, cnt_ref, kvnew_ref, vn_ref,
                         oatt_ref, osgu_ref):
    del x1_in, h_in, route_in, routet_in, cnt_in
    T = LAYER_TILE
    nq = x_ref.shape[0] // ck_ref.shape[0]
    x = x_ref[...]
    xn_b = _rms_norm(x, g1_ref[...]).astype(BF16)
    zq = jnp.dot(xn_b, wq_ref[...], preferred_element_type=F32).astype(BF16)
    zkv = jnp.dot(xn_b, w_in_ref[:, COL_KV:COL_U], preferred_element_type=F32)
    kvnew_ref[...] = zkv
    zkv_b = zkv.astype(BF16)
    n_cache = ck_ref.shape[1]
    key_pad = jnp.zeros((KEY_PAD - n_cache - nq, D_KV), BF16)
    for b in range(T // nq):
        r0 = b * nq
        kk = jnp.concatenate([ck_ref[b].astype(BF16), zkv_b[r0:r0 + nq, :D_KV], key_pad], axis=0)
        vv = jnp.concatenate([cv_ref[b].astype(BF16), zkv_b[r0:r0 + nq, D_KV:], key_pad], axis=0)
        _attention_rows(zq, r0, nq, kk, vv, bias_ref, None, oatt_ref)

    _layer_tail(x, xn_b, oatt_ref, osgu_ref, w_in_ref, sguw_ref, sgub_ref, lng_ref, lnb_ref,
                woa_ref, wos_ref, wout_ref, g2_ref, wr_ref, br_ref, tri_ref, low_ref,
                x1_ref, h_ref, route_ref, routet_ref, cnt_ref, vn_ref)


def _pack_slot_rows(x):
    lo = lax.bitcast_convert_type(x[:, :SLOT_WORDS].astype(BF16).astype(F32), U32)
    hi = lax.bitcast_convert_type(x[:, SLOT_WORDS:].astype(BF16).astype(F32), U32)
    return (lo >> 16) | hi


def _pack_slot_rows_exact(x):
    lo = lax.bitcast_convert_type(x[:, :SLOT_WORDS], U32)
    hi = lax.bitcast_convert_type(x[:, SLOT_WORDS:], U32)
    return (lo >> 16) | hi


def _unpack_slot_rows(w):
    lo = lax.bitcast_convert_type(w << 16, F32)
    hi = lax.bitcast_convert_type(w & jnp.uint32(0xFFFF0000), F32)
    return jnp.concatenate([lo, hi], axis=1).astype(BF16)


def _as_granules(x):
    return x.reshape(x.shape[0] // GRAN, GRAN, SLOT_WORDS)


def _as_rows(x):
    return x.reshape(x.shape[0] * GRAN, SLOT_WORDS)


def _segment_copies(n_gran, src_gran, dst_gran, bits, make_copy, act):
    for b in range(min(bits, 2)):
        @pl.when(((n_gran >> b) & 1) == 1)
        def _(b=b):
            done = n_gran & ((1 << b) - 1)
            act(make_copy(src_gran + done, dst_gran + done, 1 << b))

    if bits > 2:
        def quad(q, carry):
            done = (n_gran & 3) + 4 * q
            act(make_copy(src_gran + done, dst_gran + done, 4))
            return carry
        lax.fori_loop(0, n_gran >> 2, quad, 0)


def _start_tile_copies(tile, live, cnt_ref, loff_ref, goff_ref, big_ref, make_copy):
    for e in range(N_EXPERTS):
        t = tile * N_EXPERTS + e
        n = jnp.where(live, cnt_ref[t], 0)
        src, dst = loff_ref[t], goff_ref[t]
        for b in range(2):
            @pl.when(((n >> b) & 1) == 1)
            def _(b=b, n=n, src=src, dst=dst):
                done = n & ((1 << b) - 1)
                make_copy(src + done, dst + done, 1 << b).start()
        for q in range(FLAT_QUADS):
            @pl.when((n >> 2) > q)
            def _(q=q, n=n, src=src, dst=dst):
                done = (n & 3) + 4 * q
                make_copy(src + done, dst + done, 4).start()

    @pl.when(live & (big_ref[tile] > 0))
    def _():
        def rest(e, carry):
            t = tile * N_EXPERTS + e
            n, src, dst = cnt_ref[t], loff_ref[t], goff_ref[t]

            def quad(q, c):
                done = (n & 3) + 4 * q
                make_copy(src + done, dst + done, 4).start()
                return c
            lax.fori_loop(FLAT_QUADS, n >> 2, quad, 0)
            return carry
        lax.fori_loop(0, N_EXPERTS, rest, 0)


def _wait_granules(n_gran, bits, make_copy):
    for b in range(bits):
        @pl.when(((n_gran >> b) & 1) == 1)
        def _(b=b):
            make_copy(1 << b).wait()


def _dispatch_kernel(cnt_ref, loff_ref, goff_ref, tot_ref, big_ref, padn_ref, padoff_ref,
                     route_ref, h_ref, xs_hbm, stage_ref, zero_ref, sem):
    i = pl.program_id(0)
    last = pl.num_programs(0) - 1
    slot = lax.rem(i, 2)
    T = TOK_TILE

    def seg_copy(tile_slot):
        return lambda s, d, n: pltpu.make_async_copy(
            stage_ref.at[tile_slot, pl.ds(s, n)], xs_hbm.at[pl.ds(d, n)], sem.at[tile_slot])

    def wait_tile(tile, tile_slot):
        _wait_granules(tot_ref[tile], TILE_BITS, lambda n: pltpu.make_async_copy(
            stage_ref.at[tile_slot, pl.ds(0, n)], xs_hbm.at[pl.ds(0, n)], sem.at[tile_slot]))

    @pl.when(i > 1)
    def _():
        wait_tile(i - 2, slot)

    _start_tile_copies(jnp.maximum(i - 1, 0), i > 0, cnt_ref, loff_ref, goff_ref, big_ref, seg_copy(1 - slot))

    pos = route_ref[0, TOP_K:2 * TOP_K, :].astype(I32)
    r_iota = lax.broadcasted_iota(I32, (LOCAL_ROWS, T), 0)
    p = jnp.zeros((LOCAL_ROWS, T), F32)
    for k in range(TOP_K):
        p = jnp.where(r_iota == pos[k:k + 1, :], 1.0, p)
    stage_ref[slot] = _as_granules(
        _pack_slot_rows_exact(jnp.dot(p.astype(BF16), h_ref[...], preferred_element_type=F32)))

    @pl.when(i == last)
    def _():
        _start_tile_copies(i, i >= 0, cnt_ref, loff_ref, goff_ref, big_ref, seg_copy(slot))

        @pl.when(i > 0)
        def _():
            wait_tile(i - 1, 1 - slot)

        wait_tile(i, slot)
        zero_ref[...] = jnp.zeros(zero_ref.shape, U32)

        def pad_copy(s, d, n):
            return pltpu.make_async_copy(zero_ref.at[pl.ds(s, n)], xs_hbm.at[pl.ds(d, n)], sem.at[2])

        def for_pads(act):
            def body(e, carry):
                _segment_copies(padn_ref[e], 0, padoff_ref[e], PAD_BITS, pad_copy, act)
                return carry
            lax.fori_loop(0, N_EXPERTS, body, 0)

        for_pads(lambda cp: cp.start())
        for_pads(lambda cp: cp.wait())


def _expert_kernel(gran0_ref, nsub_ref, xs_hbm, w1_ref, b1_ref, w2_ref, b2_ref, ys_hbm,
                   w1b_ref, w2b_ref, xbuf_ref, ybuf_ref, sem_in, sem_out):
    e = pl.program_id(0)
    subs = ROW_BLOCK // SUB_BLOCK
    sub_gran = SUB_BLOCK // GRAN
    blk_gran = ROW_BLOCK // GRAN
    gran0 = gran0_ref[e]
    nsub = nsub_ref[e]
    nb_full = nsub // subs
    rem = nsub - nb_full * subs
    nb = nb_full + jnp.where(rem > 0, 1, 0)
    last_slot = lax.rem(nb + 1, 2)

    def in_copy(k, slot):
        return pltpu.make_async_copy(xs_hbm.at[pl.ds(gran0 + k * blk_gran, blk_gran)], xbuf_ref.at[slot],
                                     sem_in.at[slot])

    def out_copy(k, slot, m_sub):
        n = m_sub * sub_gran
        return pltpu.make_async_copy(ybuf_ref.at[slot, pl.ds(0, n)], ys_hbm.at[pl.ds(gran0 + k * blk_gran, n)],
                                     sem_out.at[slot])

    @pl.when(nb > 0)
    def _():
        in_copy(0, 0).start(priority=BLOCK_DMA_PRIORITY)

    w1b_ref[...] = w1_ref[0].astype(BF16)
    w2b_ref[...] = w2_ref[0].astype(BF16)

    def process(k, m_sub):
        n = m_sub * sub_gran
        slot = lax.rem(k, 2)

        @pl.when(k + 1 < nb)
        def _():
            in_copy(k + 1, 1 - slot).start(priority=BLOCK_DMA_PRIORITY)

        in_copy(k, slot).wait()

        @pl.when(k >= 2)
        def _():
            out_copy(k - 2, slot, subs).wait()

        x = _unpack_slot_rows(_as_rows(xbuf_ref[slot, 0:n]))
        h1 = jnp.dot(x, w1b_ref[...], preferred_element_type=F32) + b1_ref[0]
        gate = jnp.minimum(h1[:, :D_FF], SWIGLU_LIMIT)
        up = jnp.clip(h1[:, D_FF:], -SWIGLU_LIMIT, SWIGLU_LIMIT)
        act = gate * jax.nn.sigmoid(SWIGLU_ALPHA * gate) * (up + 1.0)
        y = jnp.dot(act.astype(BF16), w2b_ref[...], preferred_element_type=F32) + b2_ref[0]
        ybuf_ref[slot, 0:n] = _as_granules(_pack_slot_rows(y))
        out_copy(k, slot, m_sub).start(priority=BLOCK_DMA_PRIORITY)

    def full_block(k, carry):
        process(k, subs)
        return carry

    lax.fori_loop(0, nb_full, full_block, 0)

    for m_sub in range(1, subs):
        @pl.when(rem == m_sub)
        def _(m_sub=m_sub):
            process(nb_full, m_sub)

    @pl.when(nb >= 2)
    def _():
        out_copy(nb - 2, 1 - last_slot, subs).wait()

    @pl.when((nb >= 1) & (rem == 0))
    def _():
        out_copy(nb - 1, last_slot, subs).wait()

    for m_sub in range(1, subs):
        @pl.when(rem == m_sub)
        def _(m_sub=m_sub):
            out_copy(nb - 1, last_slot, m_sub).wait()


def _combine_kernel(n_prompt_tiles, cnt_ref, loff_ref, goff_ref, tot_ref, big_ref,
                    routet_ref, x1_ref, gf_ref, ys_hbm, yp_ref, ysm_ref, stage_ref, sem):
    i = pl.program_id(0)
    n_steps = pl.num_programs(0)
    slot = lax.rem(i, 2)
    T = TOK_TILE

    def seg_copy(tile_slot):
        return lambda s, d, n: pltpu.make_async_copy(
            ys_hbm.at[pl.ds(d, n)], stage_ref.at[tile_slot, pl.ds(s, n)], sem.at[tile_slot])

    @pl.when(i == 0)
    def _():
        stage_ref[...] = jnp.zeros(stage_ref.shape, U32)
        _start_tile_copies(i, i >= 0, cnt_ref, loff_ref, goff_ref, big_ref, seg_copy(slot))

    _start_tile_copies(jnp.minimum(i + 1, n_steps - 1), i + 1 < n_steps, cnt_ref, loff_ref, goff_ref, big_ref,
                       seg_copy(1 - slot))

    rt = routet_ref[...]
    l_iota = lax.broadcasted_iota(I32, (T, LOCAL_ROWS), 1)
    pw = jnp.zeros((T, LOCAL_ROWS), F32)
    for k in range(TOP_K):
        pos_k = rt[:, TOP_K + k:TOP_K + k + 1].astype(I32)
        pw = jnp.where(l_iota == pos_k, rt[:, 2 * TOP_K + k:2 * TOP_K + k + 1], pw)

    _wait_granules(tot_ref[i], TILE_BITS, lambda n: pltpu.make_async_copy(
        ys_hbm.at[pl.ds(0, n)], stage_ref.at[slot, pl.ds(0, n)], sem.at[slot]))
    moe = jnp.dot(pw.astype(BF16), _unpack_slot_rows(_as_rows(stage_ref[slot])), preferred_element_type=F32)
    y = _rms_norm(x1_ref[...] + moe, gf_ref[...])

    @pl.when(i < n_prompt_tiles)
    def _():
        yp_ref[...] = y

    @pl.when(i >= n_prompt_tiles)
    def _():
        ysm_ref[...] = y


def _t5_bucket(rel):
    half = NUM_BUCKETS // 2
    max_exact = half // 2
    ret = jnp.where(rel > 0, half, 0)
    n = jnp.abs(rel)
    nf = jnp.maximum(n, 1).astype(F32)
    large = max_exact + (jnp.log(nf / max_exact) / math.log(MAX_DISTANCE / max_exact)
                         * (half - max_exact)).astype(jnp.int32)
    large = jnp.minimum(large, half - 1)
    return ret + jnp.where(n < max_exact, n, large)


def _stacked_bias(table, sinks, q_pos, k_pos):
    nq, nk = q_pos.shape[0], k_pos.shape[0]
    bucket = _t5_bucket(k_pos[None, :] - q_pos[:, None])
    onehot = (bucket[:, :, None] == jnp.arange(NUM_BUCKETS)).astype(F32)
    bias = jnp.einsum('qkb,bh->hqk', onehot, table.astype(F32), precision=lax.Precision.HIGHEST)
    bias = bias.reshape(N_KV, REP * nq, nk)
    sink = jnp.repeat(sinks.astype(F32).reshape(N_KV, REP, 1), nq, axis=2).reshape(N_KV, REP * nq, 1)
    tail = jnp.full((N_KV, REP * nq, KEY_PAD - nk - 1), -jnp.inf, F32)
    return jnp.concatenate([bias, sink, tail], axis=-1)


def kernel(x_prompt, x_sample, cache_win_k, cache_win_v, norm1_g, w_in, attn_sinks, rel_bias_table, sgu_ln_g, sgu_ln_b, sgu_w, sgu_b, w_o_att, w_o_sgu, w_out, norm2_g, w_router, b_router, w_exp_in, b_exp_in, w_exp_out, b_exp_out, final_norm_g):
    batch, seq, _ = x_prompt.shape
    dec_batch, dec_seq, _ = x_sample.shape
    cache_rows = cache_win_k.shape[2]
    assert x_prompt.shape[2] == D_MODEL and w_in.shape == (1, D_MODEL, D_IN)
    assert seq % LAYER_TILE == 0 and LAYER_TILE % TOK_TILE == 0 and TOK_TILE % GMLP_CHUNK == 0
    assert LAYER_TILE >= WINDOW and LAYER_TILE % dec_seq == 0 and (dec_batch * dec_seq) % LAYER_TILE == 0
    assert dec_seq <= GMLP_CHUNK and GMLP_CHUNK % dec_seq == 0 and cache_rows == WINDOW
    T = TOK_TILE
    n_prompt = batch * seq
    n_sample = dec_batch * dec_seq
    n_tok = n_prompt + n_sample
    LT = LAYER_TILE
    sorts_per_step = LT // T
    tiles_per_seq = seq // LT
    n_psteps = n_prompt // LT
    n_ssteps = n_sample // LT
    n_ptiles = n_prompt // T
    n_tiles = n_tok // T
    seqs_per_tile = LT // dec_seq

    w_in_b = w_in[0].astype(BF16)
    w_q_b = w_in_b[:, :D_ATT].reshape(D_MODEL, N_KV, REP, HEAD_DIM).transpose(0, 2, 1, 3).reshape(
        D_MODEL, D_ATT)
    woa_b = w_o_att[0].reshape(N_KV, REP, HEAD_DIM, D_MODEL).transpose(1, 0, 2, 3).reshape(
        D_ATT, D_MODEL).astype(BF16)
    wos_b = w_o_sgu[0].astype(BF16)
    wout_b = w_out[0].astype(BF16)
    wr_b = jnp.transpose(w_router[0]).astype(BF16)
    br_col = b_router[0].astype(F32).reshape(N_EXPERTS, 1)
    g1 = norm1_g[0].reshape(1, D_MODEL)
    g2 = norm2_g[0].reshape(1, D_MODEL)
    gf = final_norm_g.reshape(1, D_MODEL)
    lng = sgu_ln_g[0].reshape(1, D_GMLP)
    lnb = sgu_ln_b[0].reshape(1, D_GMLP)
    tril = jnp.tril(jnp.ones((GMLP_CHUNK, GMLP_CHUNK), dtype=bool))
    sguw_p = jnp.where(tril[None], sgu_w[0], 0).astype(BF16)
    sgub_p = jnp.broadcast_to(sgu_b[0][:, :, None], (N_GROUPS, GMLP_CHUNK, GROUP_W)).astype(F32)
    reps = GMLP_CHUNK // dec_seq
    corner = jnp.where(tril[None, :dec_seq, :dec_seq], sgu_w[0][:, :dec_seq, :dec_seq], 0)
    sguw_s = jnp.einsum('ab,gij->gaibj', jnp.eye(reps, dtype=F32), corner).reshape(
        N_GROUPS, GMLP_CHUNK, GMLP_CHUNK).astype(BF16)
    sgub_s = jnp.broadcast_to(jnp.tile(sgu_b[0][:, :dec_seq], (1, reps))[:, :, None],
                              (N_GROUPS, GMLP_CHUNK, GROUP_W)).astype(F32)
    bias_p = _stacked_bias(rel_bias_table, attn_sinks[0], jnp.arange(CHUNK) + WINDOW, jnp.arange(BAND))
    bias_s = _stacked_bias(rel_bias_table, attn_sinks[0], cache_rows + jnp.arange(dec_seq),
                           jnp.arange(cache_rows + dec_seq))
    tri = jnp.triu(jnp.ones((T, T), F32), k=1).astype(BF16)
    low = jnp.tril(jnp.ones((N_EXPERTS, N_EXPERTS), F32), k=-1).astype(BF16)

    layer_out_shapes = (
        jax.ShapeDtypeStruct((n_tok, D_MODEL), F32),
        jax.ShapeDtypeStruct((n_tok, D_MODEL), BF16),
        jax.ShapeDtypeStruct((n_tiles, ROUTE_ROWS, T), F32),
        jax.ShapeDtypeStruct((n_tok, LANES), F32),
        jax.ShapeDtypeStruct((n_tiles, N_EXPERTS, LANES), F32),
    )
    shared_consts = (w_q_b, w_in_b)
    tail_consts_p = (sguw_p, sgub_p, lng, lnb, woa_b, wos_b, wout_b, g2, wr_b, br_col, tri, low)
    tail_consts_s = (sguw_s, sgub_s, lng, lnb, woa_b, wos_b, wout_b, g2, wr_b, br_col, tri, low)

    def tile_specs(tile_of):
        return [
            pl.BlockSpec((LT, D_MODEL), lambda *g: (tile_of(*g), 0)),
            pl.BlockSpec((LT, D_MODEL), lambda *g: (tile_of(*g), 0)),
            pl.BlockSpec((sorts_per_step, ROUTE_ROWS, T), lambda *g: (tile_of(*g), 0, 0)),
            pl.BlockSpec((LT, LANES), lambda *g: (tile_of(*g), 0)),
            pl.BlockSpec((sorts_per_step, N_EXPERTS, LANES), lambda *g: (tile_of(*g), 0, 0)),
        ]

    prompt_consts = (g1,) + shared_consts + (bias_p,) + tail_consts_p
    x1, h, route, routet, cnt, kvwin = pl.pallas_call(
        _prompt_layer_kernel,
        grid=(batch, tiles_per_seq),
        in_specs=[pl.BlockSpec((1, LT, D_MODEL), lambda b, j: (b, j, 0))]
                 + [_const_spec(c.shape) for c in prompt_consts],
        out_specs=tile_specs(lambda b, j: b * tiles_per_seq + j)
                  + [pl.BlockSpec((1, WINDOW, 2 * D_KV), lambda b, j: (b, 0, 0))],
        out_shape=layer_out_shapes + (jax.ShapeDtypeStruct((batch, WINDOW, 2 * D_KV), F32),),
        scratch_shapes=[pltpu.VMEM((LT + WINDOW, 2 * D_KV), BF16),
                        pltpu.VMEM((LT, D_ATT), BF16),
                        pltpu.VMEM((LT, D_GMLP), BF16)],
        compiler_params=pltpu.CompilerParams(dimension_semantics=("arbitrary", "arbitrary"),
                                             vmem_limit_bytes=VMEM_LIMIT),
        name="layer_prompt",
    )(x_prompt, *prompt_consts)

    xs_flat = x_sample.reshape(n_sample, D_MODEL)
    ck = cache_win_k[0].reshape(dec_batch, cache_rows, D_KV)
    cv = cache_win_v[0].reshape(dec_batch, cache_rows, D_KV)
    sample_consts = (g1,) + shared_consts + (bias_s,) + tail_consts_s
    any_spec = pl.BlockSpec(memory_space=pl.ANY)
    x1, h, route, routet, cnt, kvnew, vn_s = pl.pallas_call(
        _sample_layer_kernel,
        grid=(n_ssteps,),
        in_specs=[any_spec] * 5
                 + [pl.BlockSpec((LT, D_MODEL), lambda i: (i, 0)),
                    pl.BlockSpec((seqs_per_tile, cache_rows, D_KV), lambda i: (i, 0, 0)),
                    pl.BlockSpec((seqs_per_tile, cache_rows, D_KV), lambda i: (i, 0, 0))]
                 + [_const_spec(c.shape) for c in sample_consts],
        out_specs=tile_specs(lambda i: n_psteps + i)
                  + [pl.BlockSpec((LT, 2 * D_KV), lambda i: (i, 0)),
                     pl.BlockSpec((LT, D_GMLP), lambda i: (i, 0))],
        out_shape=layer_out_shapes + (jax.ShapeDtypeStruct((n_sample, 2 * D_KV), F32),
                                      jax.ShapeDtypeStruct((n_sample, D_GMLP), F32)),
        scratch_shapes=[pltpu.VMEM((LT, D_ATT), BF16), pltpu.VMEM((LT, D_GMLP), BF16)],
        input_output_aliases={0: 0, 1: 1, 2: 2, 3: 3, 4: 4},
        compiler_params=pltpu.CompilerParams(dimension_semantics=("arbitrary",),
                                             vmem_limit_bytes=VMEM_LIMIT),
        name="layer_sample",
    )(x1, h, route, routet, cnt, xs_flat, ck, cv, *sample_consts)

    blk_gran = SUB_BLOCK // GRAN
    counts =cnt[:, :, 0].astype(I32)
    seg_gran = (counts + (GRAN - 1)) // GRAN
    local_off = jnp.cumsum(seg_gran, axis=1) - seg_gran
    tot_gran = jnp.sum(seg_gran, axis=0)
    ptot_gran = (tot_gran + (blk_gran - 1)) // blk_gran * blk_gran
    pend_gran = jnp.cumsum(ptot_gran)
    gstart = pend_gran - ptot_gran
    global_off = gstart[None, :] + jnp.cumsum(seg_gran, axis=0) - seg_gran
    pad_n = ptot_gran - tot_gran
    pad_off = gstart + tot_gran
    n_rows = -(-(TOP_K * n_tok + n_tiles * N_EXPERTS * (GRAN - 1) + N_EXPERTS * (SUB_BLOCK - GRAN))
               // SUB_BLOCK) * SUB_BLOCK + ROW_BLOCK
    region_gran0 = gstart.astype(I32)
    region_subs = (ptot_gran // blk_gran).astype(I32)
    tile_gran = jnp.sum(seg_gran, axis=1).astype(I32)
    tile_big = jnp.any(seg_gran >= 4 * (FLAT_QUADS + 1), axis=1).astype(I32)
    seg_gran_f = seg_gran.reshape(-1).astype(I32)
    local_off_f = local_off.reshape(-1).astype(I32)
    global_off_f = global_off.reshape(-1).astype(I32)

    xs = pl.pallas_call(
        _dispatch_kernel,
        grid_spec=pltpu.PrefetchScalarGridSpec(
            num_scalar_prefetch=7,
            grid=(n_tiles,),
            in_specs=[pl.BlockSpec((1, ROUTE_ROWS, T), lambda i, *_: (i, 0, 0)),
                      pl.BlockSpec((T, D_MODEL), lambda i, *_: (i, 0))],
            out_specs=pl.BlockSpec(memory_space=pl.ANY),
            scratch_shapes=[pltpu.VMEM((2, LOCAL_ROWS // GRAN, GRAN, SLOT_WORDS), U32),
                            pltpu.VMEM((SUB_BLOCK // GRAN, GRAN, SLOT_WORDS), U32),
                            pltpu.SemaphoreType.DMA((3,))]),
        out_shape=jax.ShapeDtypeStruct((n_rows // GRAN, GRAN, SLOT_WORDS), U32),
        compiler_params=pltpu.CompilerParams(dimension_semantics=("arbitrary",),
                                             vmem_limit_bytes=VMEM_LIMIT),
        name="moe_dispatch",
    )(seg_gran_f, local_off_f, global_off_f, tile_gran, tile_big, pad_n.astype(I32), pad_off.astype(I32),
      route, h)

    ys = pl.pallas_call(
        _expert_kernel,
        grid_spec=pltpu.PrefetchScalarGridSpec(
            num_scalar_prefetch=2,
            grid=(N_EXPERTS,),
            in_specs=[pl.BlockSpec(memory_space=pl.ANY),
                      pl.BlockSpec((1, D_MODEL, 2 * D_FF), lambda e, *_: (e, 0, 0)),
                      pl.BlockSpec((1, 1, 2 * D_FF), lambda e, *_: (e, 0, 0)),
                      pl.BlockSpec((1, D_FF, D_MODEL), lambda e, *_: (e, 0, 0)),
                      pl.BlockSpec((1, 1, D_MODEL), lambda e, *_: (e, 0, 0))],
            out_specs=pl.BlockSpec(memory_space=pl.ANY),
            scratch_shapes=[pltpu.VMEM((D_MODEL, 2 * D_FF), BF16), pltpu.VMEM((D_FF, D_MODEL), BF16),
                            pltpu.VMEM((2, ROW_BLOCK // GRAN, GRAN, SLOT_WORDS), U32),
                            pltpu.VMEM((2, ROW_BLOCK // GRAN, GRAN, SLOT_WORDS), U32),
                            pltpu.SemaphoreType.DMA((2,)), pltpu.SemaphoreType.DMA((2,))]),
        out_shape=jax.ShapeDtypeStruct((n_rows // GRAN, GRAN, SLOT_WORDS), U32),
        compiler_params=pltpu.CompilerParams(dimension_semantics=("arbitrary",),
                                             vmem_limit_bytes=VMEM_LIMIT),
        name="moe_experts",
    )(region_gran0, region_subs, xs, w_exp_in[0], b_exp_in[0].reshape(N_EXPERTS, 1, 2 * D_FF),
      w_exp_out[0], b_exp_out[0].reshape(N_EXPERTS, 1, D_MODEL))

    y_p, y_s = pl.pallas_call(
        functools.partial(_combine_kernel, n_ptiles),
        grid_spec=pltpu.PrefetchScalarGridSpec(
            num_scalar_prefetch=5,
            grid=(n_tiles,),
            in_specs=[pl.BlockSpec((T, LANES), lambda i, *_: (i, 0)),
                      pl.BlockSpec((T, D_MODEL), lambda i, *_: (i, 0)),
                      pl.BlockSpec((1, D_MODEL), lambda i, *_: (0, 0)),
                      pl.BlockSpec(memory_space=pl.ANY)],
            out_specs=[pl.BlockSpec((T, D_MODEL), lambda i, *_: (jnp.minimum(i, n_ptiles - 1), 0)),
                       pl.BlockSpec((T, D_MODEL), lambda i, *_: (jnp.maximum(i - n_ptiles, 0), 0))],
            scratch_shapes=[pltpu.VMEM((2, LOCAL_ROWS // GRAN, GRAN, SLOT_WORDS), U32), pltpu.SemaphoreType.DMA((2,))]),
        out_shape=(jax.ShapeDtypeStruct((n_prompt, D_MODEL), F32),
                   jax.ShapeDtypeStruct((n_sample, D_MODEL), F32)),
        compiler_params=pltpu.CompilerParams(dimension_semantics=("arbitrary",),
                                             vmem_limit_bytes=VMEM_LIMIT),
        name="moe_combine",
    )(seg_gran_f, local_off_f, global_off_f, tile_gran, tile_big, routet, x1, gf, ys)

    y_prompt = y_p.reshape(batch, seq, D_MODEL)
    y_sample = y_s.reshape(dec_batch, dec_seq, D_MODEL)
    new_win_k_prompt = kvwin[:, :, :D_KV].reshape(1, batch, WINDOW, N_KV, HEAD_DIM)
    new_win_v_prompt = kvwin[:, :, D_KV:].reshape(1, batch, WINDOW, N_KV, HEAD_DIM)
    new_win_k_sample = kvnew[:, :D_KV].reshape(1, dec_batch, dec_seq, N_KV, HEAD_DIM)
    new_win_v_sample = kvnew[:, D_KV:].reshape(1, dec_batch, dec_seq, N_KV, HEAD_DIM)
    new_sgu_v_sample = vn_s.reshape(1, dec_batch, dec_seq, D_GMLP)
    return (y_prompt, y_sample, new_win_k_prompt, new_win_v_prompt, new_win_k_sample,
            new_win_v_sample, new_sgu_v_sample)
```

```python
import functools
import math

import numpy as np
import jax
import jax.numpy as jnp
from jax import lax
from jax.experimental import pallas as pl
from jax.experimental.pallas import tpu as pltpu

F32 = jnp.float32
BF16 = jnp.bfloat16
I32 = jnp.int32
U32 = jnp.uint32

D_MODEL = 1024
HEAD_DIM = 64
N_HEADS = 16
N_KV = 2
REP = N_HEADS // N_KV
CHUNK = 64
WINDOW = 128
BAND = WINDOW + CHUNK
KEY_PAD = 256
D_ATT = N_HEADS * HEAD_DIM
D_KV = N_KV * HEAD_DIM
NUM_BUCKETS = 32
MAX_DISTANCE = 128
GMLP_CHUNK = 128
D_GMLP = 1024
N_GROUPS = 4
GROUP_W = D_GMLP // N_GROUPS
N_EXPERTS = 32
TOP_K = 4
D_FF = 1024
SWIGLU_LIMIT = 7.0
SWIGLU_ALPHA = 1.702
NORM_EPS = 1e-5
D_IN = D_ATT + 2 * D_KV + 2 * D_GMLP + 2 * D_MODEL
COL_KV = D_ATT
COL_U = COL_KV + 2 * D_KV
COL_VG = COL_U + D_GMLP
COL_GA = COL_VG + D_GMLP
COL_GB = COL_GA + D_MODEL
SQRT_HALF = float(np.sqrt(0.5))

LANES = 128
WORD_SUBLANES = 8
VMEM_LIMIT = 56 * 1024 * 1024

TOK_TILE = 256
LAYER_TILE = 512
GRAN = WORD_SUBLANES
SLOT_WORDS = D_MODEL // 2
FLAT_QUADS = 1
LOCAL_ROWS = TOP_K * TOK_TILE + N_EXPERTS * GRAN
ROW_BLOCK = 512
SUB_BLOCK = 128
PAD_BITS = (SUB_BLOCK // GRAN - 1).bit_length()
TILE_BITS = (LOCAL_ROWS // GRAN).bit_length()
BLOCK_DMA_PRIORITY = 1
ROUTE_ROWS = 16


def _const_spec(shape):
    nd = len(shape)
    return pl.BlockSpec(shape, lambda *_: (0,) * nd, pipeline_mode=pl.Buffered(1))


def _rms_norm(x, g):
    ms = jnp.mean(x * x, axis=-1, keepdims=True)
    return x * lax.rsqrt(ms + NORM_EPS) * g


def _gelu(x):
    return 0.5 * x * (1.0 + lax.erf(x * SQRT_HALF))


def _attend(q, k, v, bias, valid):
    logits = lax.dot_general(q, k, (((1,), (1,)), ((), ())), preferred_element_type=F32) + bias
    if valid is not None:
        logits = jnp.where(valid, logits, -jnp.inf)
    m = jnp.max(logits, axis=-1, keepdims=True)
    p = jnp.exp(logits - m)
    den = jnp.sum(p, axis=-1, keepdims=True)
    w = (p * (1.0 / den)).astype(BF16)
    return jnp.dot(w, v, preferred_element_type=F32)


def _attention_rows(zq, row0, n_rows, k_ext, v_ext, bias_ref, valid, oatt_ref):
    lane = lax.broadcasted_iota(I32, (1, LANES), 1)
    stack = []
    for g in range(N_KV):
        in_group = (lane >= g * HEAD_DIM) & (lane < (g + 1) * HEAD_DIM)
        qmask = jnp.where(in_group, HEAD_DIM ** -0.5, 0.0).astype(BF16)
        stack += [zq[row0:row0 + n_rows, t * LANES:(t + 1) * LANES] * qmask for t in range(REP)]
    bias = bias_ref[...].reshape(N_HEADS * n_rows, KEY_PAD)
    o = _attend(jnp.concatenate(stack, axis=0), k_ext, v_ext, bias, valid)
    for g in range(N_KV):
        for t in range(REP):
            r = (g * REP + t) * n_rows
            c0 = t * LANES + g * HEAD_DIM
            oatt_ref[row0:row0 + n_rows, c0:c0 + HEAD_DIM] = (
                o[r:r + n_rows, g * HEAD_DIM:(g + 1) * HEAD_DIM].astype(BF16))


PROJ_PIECES = 16


def _proj_piece(xn_b, w_in_ref, k):
    width = (D_IN - COL_U) // PROJ_PIECES
    c0 = COL_U + k * width
    z = jnp.dot(xn_b, w_in_ref[:, c0:c0 + width], preferred_element_type=F32)
    return _gelu(z) if c0 < COL_GA else jax.nn.sigmoid(z)


def _sgu_branch(pieces, osgu_ref, sguw_ref, sgub_ref, lng_ref, lnb_ref, vn_ref):
    u = jnp.concatenate(pieces[:PROJ_PIECES // 4], axis=1)
    vg = jnp.concatenate(pieces[PROJ_PIECES // 4:PROJ_PIECES // 2], axis=1)
    T = u.shape[0]
    mu = jnp.mean(vg, axis=-1, keepdims=True)
    var = jnp.mean(jnp.square(vg - mu), axis=-1, keepdims=True)
    vn = (vg - mu) * lax.rsqrt(var + NORM_EPS) * lng_ref[...] + lnb_ref[...]
    if vn_ref is not None:
        vn_ref[...] = vn
    vn_b = vn.astype(BF16)
    for j in range(T // GMLP_CHUNK):
        rows = slice(j * GMLP_CHUNK, (j + 1) * GMLP_CHUNK)
        for g in range(N_GROUPS):
            cols = slice(g * GROUP_W, (g + 1) * GROUP_W)
            mixed = jnp.dot(sguw_ref[g], vn_b[rows, cols], preferred_element_type=F32) + sgub_ref[g]
            osgu_ref[rows, cols] = (u[rows, cols] * mixed).astype(BF16)


def _layer_tail(x, pieces, oatt_ref, osgu_ref, woa_ref, wos_ref, wout_ref, g2_ref, wr_ref, br_ref,
                tri_ref, low_ref, x1_ref, h_ref, route_ref, routet_ref, cnt_ref):
    T = x.shape[0]
    ga = jnp.concatenate(pieces[PROJ_PIECES // 2:3 * PROJ_PIECES // 4], axis=1)
    gb = jnp.concatenate(pieces[3 * PROJ_PIECES // 4:], axis=1)
    merged = ga * jnp.dot(oatt_ref[...], woa_ref[...], preferred_element_type=F32)
    merged = merged + gb * jnp.dot(osgu_ref[...], wos_ref[...], preferred_element_type=F32)
    x1 = x + jnp.dot(merged.astype(BF16), wout_ref[...], preferred_element_type=F32)
    x1_ref[...] = x1

    h_b = _rms_norm(x1, g2_ref[...]).astype(BF16)
    h_ref[...] = h_b
    lt = lax.dot_general(wr_ref[...], h_b, (((1,), (1,)), ((), ())),
                         preferred_element_type=F32) + br_ref[...]
    for s in range(T // TOK_TILE):
        rec, cnt_b = _route_sort_tile(lt[:, s * TOK_TILE:(s + 1) * TOK_TILE], tri_ref, low_ref)
        route_ref[s] = rec
        rec_pad = jnp.concatenate([rec, jnp.zeros((LANES - ROUTE_ROWS, TOK_TILE), F32)], axis=0)
        routet_ref[s * TOK_TILE:(s + 1) * TOK_TILE, :] = jnp.transpose(rec_pad)
        cnt_ref[s] = cnt_b


def _route_sort_tile(lt, tri_ref, low_ref):
    T = lt.shape[1]
    e_iota = lax.broadcasted_iota(I32, (N_EXPERTS, T), 0).astype(F32)
    cur = lt
    vals, idxs = [], []
    for _ in range(TOP_K):
        m = jnp.max(cur, axis=0, keepdims=True)
        ik = jnp.min(jnp.where(cur == m, e_iota, float(N_EXPERTS)), axis=0, keepdims=True)
        vals.append(m)
        idxs.append(ik)
        cur = jnp.where(e_iota == ik, -jnp.inf, cur)
    exps = [jnp.exp(v - vals[0]) for v in vals]
    den = exps[0] + exps[1] + exps[2] + exps[3]
    gates = [e / den for e in exps]

    onehot = jnp.zeros((N_EXPERTS, T), F32)
    for ik in idxs:
        onehot = onehot + jnp.where(e_iota == ik, 1.0, 0.0)
    rank = jnp.dot(onehot.astype(BF16), tri_ref[...], preferred_element_type=F32)
    cnt = jnp.sum(onehot, axis=1, keepdims=True)
    cnt_b = jnp.broadcast_to(cnt, (N_EXPERTS, LANES))
    gran = jnp.floor((cnt_b + (GRAN - 1)) * (1.0 / GRAN))
    off = jnp.dot(low_ref[...], gran.astype(BF16), preferred_element_type=F32) * GRAN
    base = off[:, 0:1] + rank
    poss = [jnp.sum(jnp.where(e_iota == ik, base, 0.0), axis=0, keepdims=True) for ik in idxs]

    rec = jnp.concatenate(idxs + poss + gates
                          + [jnp.zeros((ROUTE_ROWS - 3 * TOP_K, T), F32)], axis=0)
    return rec, cnt_b


def _prompt_layer_kernel(x_ref, g1_ref, wq_ref, w_in_ref, bias_ref, sguw_ref, sgub_ref, lng_ref,
                         lnb_ref, woa_ref, wos_ref, wout_ref, g2_ref, wr_ref, br_ref, tri_ref, low_ref,
                         x1_ref, h_ref, route_ref, routet_ref, cnt_ref, kvwin_ref,
                         kvx_ref, oatt_ref, osgu_ref):
    T = LAYER_TILE
    j = pl.program_id(1)
    x = x_ref[0]
    xn_b = _rms_norm(x, g1_ref[...]).astype(BF16)
    zq = jnp.dot(xn_b, wq_ref[...], preferred_element_type=F32).astype(BF16)
    zkv = jnp.dot(xn_b, w_in_ref[:, COL_KV:COL_U], preferred_element_type=F32)

    @pl.when(j == pl.num_programs(1) - 1)
    def _():
        kvwin_ref[0] = zkv[T - WINDOW:, :]

    @pl.when(j == 0)
    def _():
        kvx_ref[0:WINDOW, :] = jnp.zeros((WINDOW, 2 * D_KV), BF16)

    @pl.when(j > 0)
    def _():
        kvx_ref[0:WINDOW, :] = kvx_ref[T:T + WINDOW, :]

    kvx_ref[WINDOW:, :] = zkv.astype(BF16)

    col = lax.broadcasted_iota(I32, (1, KEY_PAD), 1)
    key_pad = jnp.zeros((KEY_PAD - BAND, 2 * D_KV), BF16)
    n_chunks = T // CHUNK
    pieces = []
    for c in range(n_chunks):
        r0 = c * CHUNK
        kvb = jnp.concatenate([kvx_ref[r0:r0 + BAND, :], key_pad], axis=0)
        valid = ((col + (j * T + r0 - WINDOW)) >= 0) | (col >= BAND) if r0 < WINDOW else None
        _attention_rows(zq, r0, CHUNK, kvb[:, :D_KV], kvb[:, D_KV:], bias_ref, valid, oatt_ref)
        pieces += [_proj_piece(xn_b, w_in_ref, k) for k in range(len(pieces), (c + 1) * PROJ_PIECES // n_chunks)]

    _sgu_branch(pieces, osgu_ref, sguw_ref, sgub_ref, lng_ref, lnb_ref, None)
    _layer_tail(x, pieces, oatt_ref, osgu_ref, woa_ref, wos_ref, wout_ref, g2_ref, wr_ref, br_ref,
                tri_ref, low_ref, x1_ref, h_ref, route_ref, routet_ref, cnt_ref)


def _sample_layer_kernel(x1_in, h_in, route_in, routet_in, cnt_in,
                         x_ref, ck_ref, cv_ref, g1_ref, wq_ref, w_in_ref, bias_ref, sguw_ref, sgub_ref,
                         lng_ref, lnb_ref, woa_ref, wos_ref, wout_ref, g2_ref, wr_ref, br_ref, tri_ref,
                         low_ref,
                         x1_ref, h_ref, route_ref, routet_ref, cnt_ref, kvnew_ref, vn_ref,
                         oatt_ref, osgu_ref):
    del x1_in, h_in, route_in, routet_in, cnt_in
    T = LAYER_TILE
    nq = x_ref.shape[0] // ck_ref.shape[0]
    x = x_ref[...]
    xn_b = _rms_norm(x, g1_ref[...]).astype(BF16)
    zq = jnp.dot(xn_b, wq_ref[...], preferred_element_type=F32).astype(BF16)
    zkv = jnp.dot(xn_b, w_in_ref[:, COL_KV:COL_U], preferred_element_type=F32)
    kvnew_ref[...] = zkv
    zkv_b = zkv.astype(BF16)
    n_cache = ck_ref.shape[1]
    key_pad = jnp.zeros((KEY_PAD - n_cache - nq, D_KV), BF16)
    n_seqs = T // nq
    pieces = []
    for b in range(n_seqs):
        r0 = b * nq
        kk = jnp.concatenate([ck_ref[b].astype(BF16), zkv_b[r0:r0 + nq, :D_KV], key_pad], axis=0)
        vv = jnp.concatenate([cv_ref[b].astype(BF16), zkv_b[r0:r0 + nq, D_KV:], key_pad], axis=0)
        _attention_rows(zq, r0, nq, kk, vv, bias_ref, None, oatt_ref)
        pieces += [_proj_piece(xn_b, w_in_ref, k) for k in range(len(pieces), (b + 1) * PROJ_PIECES // n_seqs)]

    _sgu_branch(pieces, osgu_ref, sguw_ref, sgub_ref, lng_ref, lnb_ref, vn_ref)
    _layer_tail(x, pieces, oatt_ref, osgu_ref, woa_ref, wos_ref, wout_ref, g2_ref, wr_ref, br_ref,
                tri_ref, low_ref, x1_ref, h_ref, route_ref, routet_ref, cnt_ref)


def _pack_slot_rows(x):
    lo = lax.bitcast_convert_type(x[:, :SLOT_WORDS].astype(BF16).astype(F32), U32)
    hi = lax.bitcast_convert_type(x[:, SLOT_WORDS:].astype(BF16).astype(F32), U32)
    return (lo >> 16) | hi


def _pack_slot_rows_exact(x):
    lo = lax.bitcast_convert_type(x[:, :SLOT_WORDS], U32)
    hi = lax.bitcast_convert_type(x[:, SLOT_WORDS:], U32)
    return (lo >> 16) | hi


def _unpack_slot_rows(w):
    lo = lax.bitcast_convert_type(w << 16, F32)
    hi = lax.bitcast_convert_type(w & jnp.uint32(0xFFFF0000), F32)
    return jnp.concatenate([lo, hi], axis=1).astype(BF16)


def _as_granules(x):
    return x.reshape(x.shape[0] // GRAN, GRAN, SLOT_WORDS)


def _as_rows(x):
    return x.reshape(x.shape[0] * GRAN, SLOT_WORDS)


def _segment_copies(n_gran, src_gran, dst_gran, bits, make_copy, act):
    for b in range(min(bits, 2)):
        @pl.when(((n_gran >> b) & 1) == 1)
        def _(b=b):
            done = n_gran & ((1 << b) - 1)
            act(make_copy(src_gran + done, dst_gran + done, 1 << b))

    if bits > 2:
        def quad(q, carry):
            done = (n_gran & 3) + 4 * q
            act(make_copy(src_gran + done, dst_gran + done, 4))
            return carry
        lax.fori_loop(0, n_gran >> 2, quad, 0)


def _start_tile_copies(tile, live, cnt_ref, loff_ref, goff_ref, big_ref, make_copy):
    for e in range(N_EXPERTS):
        t = tile * N_EXPERTS + e
        n = jnp.where(live, cnt_ref[t], 0)
        src, dst = loff_ref[t], goff_ref[t]
        for b in range(2):
            @pl.when(((n >> b) & 1) == 1)
            def _(b=b, n=n, src=src, dst=dst):
                done = n & ((1 << b) - 1)
                make_copy(src + done, dst + done, 1 << b).start()
        for q in range(FLAT_QUADS):
            @pl.when((n >> 2) > q)
            def _(q=q, n=n, src=src, dst=dst):
                done = (n & 3) + 4 * q
                make_copy(src + done, dst + done, 4).start()

    @pl.when(live & (big_ref[tile] > 0))
    def _():
        def rest(e, carry):
            t = tile * N_EXPERTS + e
            n, src, dst = cnt_ref[t], loff_ref[t], goff_ref[t]

            def quad(q, c):
                done = (n & 3) + 4 * q
                make_copy(src + done, dst + done, 4).start()
                return c
            lax.fori_loop(FLAT_QUADS, n >> 2, quad, 0)
            return carry
        lax.fori_loop(0, N_EXPERTS, rest, 0)


def _wait_granules(n_gran, bits, make_copy):
    for b in range(bits):
        @pl.when(((n_gran >> b) & 1) == 1)
        def _(b=b):
            make_copy(1 << b).wait()


def _dispatch_kernel(cnt_ref, loff_ref, goff_ref, tot_ref, big_ref, padn_ref, padoff_ref,
                     route_ref, h_ref, xs_hbm, stage_ref, zero_ref, sem):
    i = pl.program_id(0)
    last = pl.num_programs(0) - 1
    slot = lax.rem(i, 2)
    T = TOK_TILE

    def seg_copy(tile_slot):
        return lambda s, d, n: pltpu.make_async_copy(
            stage_ref.at[tile_slot, pl.ds(s, n)], xs_hbm.at[pl.ds(d, n)], sem.at[tile_slot])

    def wait_tile(tile, tile_slot):
        _wait_granules(tot_ref[tile], TILE_BITS, lambda n: pltpu.make_async_copy(
            stage_ref.at[tile_slot, pl.ds(0, n)], xs_hbm.at[pl.ds(0, n)], sem.at[tile_slot]))

    @pl.when(i > 1)
    def _():
        wait_tile(i - 2, slot)

    _start_tile_copies(jnp.maximum(i - 1, 0), i > 0, cnt_ref, loff_ref, goff_ref, big_ref, seg_copy(1 - slot))

    pos = route_ref[0, TOP_K:2 * TOP_K, :].astype(I32)
    r_iota = lax.broadcasted_iota(I32, (LOCAL_ROWS, T), 0)
    p = jnp.zeros((LOCAL_ROWS, T), F32)
    for k in range(TOP_K):
        p = jnp.where(r_iota == pos[k:k + 1, :], 1.0, p)
    stage_ref[slot] = _as_granules(
        _pack_slot_rows_exact(jnp.dot(p.astype(BF16), h_ref[...], preferred_element_type=F32)))

    @pl.when(i == last)
    def _():
        _start_tile_copies(i, i >= 0, cnt_ref, loff_ref, goff_ref, big_ref, seg_copy(slot))

        @pl.when(i > 0)
        def _():
            wait_tile(i - 1, 1 - slot)

        wait_tile(i, slot)
        zero_ref[...] = jnp.zeros(zero_ref.shape, U32)

        def pad_copy(s, d, n):
            return pltpu.make_async_copy(zero_ref.at[pl.ds(s, n)], xs_hbm.at[pl.ds(d, n)], sem.at[2])

        def for_pads(act):
            def body(e, carry):
                _segment_copies(padn_ref[e], 0, padoff_ref[e], PAD_BITS, pad_copy, act)
                return carry
            lax.fori_loop(0, N_EXPERTS, body, 0)

        for_pads(lambda cp: cp.start())
        for_pads(lambda cp: cp.wait())


def _expert_kernel(gran0_ref, nsub_ref, xs_hbm, w1_ref, b1_ref, w2_ref, b2_ref, ys_hbm,
                   w1b_ref, w2b_ref, xbuf_ref, ybuf_ref, sem_in, sem_out):
    e = pl.program_id(0)
    subs = ROW_BLOCK // SUB_BLOCK
    sub_gran = SUB_BLOCK // GRAN
    blk_gran = ROW_BLOCK // GRAN
    gran0 = gran0_ref[e]
    nsub = nsub_ref[e]
    nb_full = nsub // subs
    rem = nsub - nb_full * subs
    nb = nb_full + jnp.where(rem > 0, 1, 0)
    last_slot = lax.rem(nb + 1, 2)

    def in_copy(k, slot):
        return pltpu.make_async_copy(xs_hbm.at[pl.ds(gran0 + k * blk_gran, blk_gran)], xbuf_ref.at[slot],
                                     sem_in.at[slot])

    def out_copy(k, slot, m_sub):
        n = m_sub * sub_gran
        return pltpu.make_async_copy(ybuf_ref.at[slot, pl.ds(0, n)], ys_hbm.at[pl.ds(gran0 + k * blk_gran, n)],
                                     sem_out.at[slot])

    @pl.when(nb > 0)
    def _():
        in_copy(0, 0).start(priority=BLOCK_DMA_PRIORITY)

    w1b_ref[...] = w1_ref[0].astype(BF16)
    w2b_ref[...] = w2_ref[0].astype(BF16)

    def process(k, m_sub):
        n = m_sub * sub_gran
        slot = lax.rem(k, 2)

        @pl.when(k + 1 < nb)
        def _():
            in_copy(k + 1, 1 - slot).start(priority=BLOCK_DMA_PRIORITY)

        in_copy(k, slot).wait()

        @pl.when(k >= 2)
        def _():
            out_copy(k - 2, slot, subs).wait()

        x = _unpack_slot_rows(_as_rows(xbuf_ref[slot, 0:n]))
        h1 = jnp.dot(x, w1b_ref[...], preferred_element_type=F32) + b1_ref[0]
        gate = jnp.minimum(h1[:, :D_FF], SWIGLU_LIMIT)
        up = jnp.clip(h1[:, D_FF:], -SWIGLU_LIMIT, SWIGLU_LIMIT)
        act = gate * jax.nn.sigmoid(SWIGLU_ALPHA * gate) * (up + 1.0)
        y = jnp.dot(act.astype(BF16), w2b_ref[...], preferred_element_type=F32) + b2_ref[0]
        ybuf_ref[slot, 0:n] = _as_granules(_pack_slot_rows(y))
        out_copy(k, slot, m_sub).start(priority=BLOCK_DMA_PRIORITY)

    def full_block(k, carry):
        process(k, subs)
        return carry

    lax.fori_loop(0, nb_full, full_block, 0)

    for m_sub in range(1, subs):
        @pl.when(rem == m_sub)
        def _(m_sub=m_sub):
            process(nb_full, m_sub)

    @pl.when(nb >= 2)
    def _():
        out_copy(nb - 2, 1 - last_slot, subs).wait()

    @pl.when((nb >= 1) & (rem == 0))
    def _():
        out_copy(nb - 1, last_slot, subs).wait()

    for m_sub in range(1, subs):
        @pl.when(rem == m_sub)
        def _(m_sub=m_sub):
            out_copy(nb - 1, last_slot, m_sub).wait()


def _combine_kernel(n_prompt_tiles, cnt_ref, loff_ref, goff_ref, tot_ref, big_ref,
                    routet_ref, x1_ref, gf_ref, ys_hbm, yp_ref, ysm_ref, stage_ref, sem):
    i = pl.program_id(0)
    n_steps = pl.num_programs(0)
    slot = lax.rem(i, 2)
    T = TOK_TILE

    def seg_copy(tile_slot):
        return lambda s, d, n: pltpu.make_async_copy(
            ys_hbm.at[pl.ds(d, n)], stage_ref.at[tile_slot, pl.ds(s, n)], sem.at[tile_slot])

    @pl.when(i == 0)
    def _():
        stage_ref[...] = jnp.zeros(stage_ref.shape, U32)
        _start_tile_copies(i, i >= 0, cnt_ref, loff_ref, goff_ref, big_ref, seg_copy(slot))

    _start_tile_copies(jnp.minimum(i + 1, n_steps - 1), i + 1 < n_steps, cnt_ref, loff_ref, goff_ref, big_ref,
                       seg_copy(1 - slot))

    rt = routet_ref[...]
    l_iota = lax.broadcasted_iota(I32, (T, LOCAL_ROWS), 1)
    pw = jnp.zeros((T, LOCAL_ROWS), F32)
    for k in range(TOP_K):
        pos_k = rt[:, TOP_K + k:TOP_K + k + 1].astype(I32)
        pw = jnp.where(l_iota == pos_k, rt[:, 2 * TOP_K + k:2 * TOP_K + k + 1], pw)

    _wait_granules(tot_ref[i], TILE_BITS, lambda n: pltpu.make_async_copy(
        ys_hbm.at[pl.ds(0, n)], stage_ref.at[slot, pl.ds(0, n)], sem.at[slot]))
    moe = jnp.dot(pw.astype(BF16), _unpack_slot_rows(_as_rows(stage_ref[slot])), preferred_element_type=F32)
    y = _rms_norm(x1_ref[...] + moe, gf_ref[...])

    @pl.when(i < n_prompt_tiles)
    def _():
        yp_ref[...] = y

    @pl.when(i >= n_prompt_tiles)
    def _():
        ysm_ref[...] = y


def _t5_bucket(rel):
    half = NUM_BUCKETS // 2
    max_exact = half // 2
    ret = jnp.where(rel > 0, half, 0)
    n = jnp.abs(rel)
    nf = jnp.maximum(n, 1).astype(F32)
    large = max_exact + (jnp.log(nf / max_exact) / math.log(MAX_DISTANCE / max_exact)
                         * (half - max_exact)).astype(jnp.int32)
    large = jnp.minimum(large, half - 1)
    return ret + jnp.where(n < max_exact, n, large)


def _stacked_bias(table, sinks, q_pos, k_pos):
    nq, nk = q_pos.shape[0], k_pos.shape[0]
    bucket = _t5_bucket(k_pos[None, :] - q_pos[:, None])
    onehot = (bucket[:, :, None] == jnp.arange(NUM_BUCKETS)).astype(F32)
    bias = jnp.einsum('qkb,bh->hqk', onehot, table.astype(F32), precision=lax.Precision.HIGHEST)
    bias = bias.reshape(N_KV, REP * nq, nk)
    sink = jnp.repeat(sinks.astype(F32).reshape(N_KV, REP, 1), nq, axis=2).reshape(N_KV, REP * nq, 1)
    tail = jnp.full((N_KV, REP * nq, KEY_PAD - nk - 1), -jnp.inf, F32)
    return jnp.concatenate([bias, sink, tail], axis=-1)


def kernel(x_prompt, x_sample, cache_win_k, cache_win_v, norm1_g, w_in, attn_sinks, rel_bias_table, sgu_ln_g, sgu_ln_b, sgu_w, sgu_b, w_o_att, w_o_sgu, w_out, norm2_g, w_router, b_router, w_exp_in, b_exp_in, w_exp_out, b_exp_out, final_norm_g):
    batch, seq, _ = x_prompt.shape
    dec_batch, dec_seq, _ = x_sample.shape
    cache_rows = cache_win_k.shape[2]
    assert x_prompt.shape[2] == D_MODEL and w_in.shape == (1, D_MODEL, D_IN)
    assert seq % LAYER_TILE == 0 and LAYER_TILE % TOK_TILE == 0 and TOK_TILE % GMLP_CHUNK == 0
    assert LAYER_TILE >= WINDOW and LAYER_TILE % dec_seq == 0 and (dec_batch * dec_seq) % LAYER_TILE == 0
    assert dec_seq <= GMLP_CHUNK and GMLP_CHUNK % dec_seq == 0 and cache_rows == WINDOW
    T = TOK_TILE
    n_prompt = batch * seq
    n_sample = dec_batch * dec_seq
    n_tok = n_prompt + n_sample
    LT = LAYER_TILE
    sorts_per_step = LT // T
    tiles_per_seq = seq // LT
    n_psteps = n_prompt // LT
    n_ssteps = n_sample // LT
    n_ptiles = n_prompt // T
    n_tiles = n_tok // T
    seqs_per_tile = LT // dec_seq

    w_in_b = w_in[0].astype(BF16)
    w_q_b = w_in_b[:, :D_ATT].reshape(D_MODEL, N_KV, REP, HEAD_DIM).transpose(0, 2, 1, 3).reshape(
        D_MODEL, D_ATT)
    woa_b = w_o_att[0].reshape(N_KV, REP, HEAD_DIM, D_MODEL).transpose(1, 0, 2, 3).reshape(
        D_ATT, D_MODEL).astype(BF16)
    wos_b = w_o_sgu[0].astype(BF16)
    wout_b = w_out[0].astype(BF16)
    wr_b = jnp.transpose(w_router[0]).astype(BF16)
    br_col = b_router[0].astype(F32).reshape(N_EXPERTS, 1)
    g1 = norm1_g[0].reshape(1, D_MODEL)
    g2 = norm2_g[0].reshape(1, D_MODEL)
    gf = final_norm_g.reshape(1, D_MODEL)
    lng = sgu_ln_g[0].reshape(1, D_GMLP)
    lnb = sgu_ln_b[0].reshape(1, D_GMLP)
    tril = jnp.tril(jnp.ones((GMLP_CHUNK, GMLP_CHUNK), dtype=bool))
    sguw_p = jnp.where(tril[None], sgu_w[0], 0).astype(BF16)
    sgub_p = jnp.broadcast_to(sgu_b[0][:, :, None], (N_GROUPS, GMLP_CHUNK, GROUP_W)).astype(F32)
    reps = GMLP_CHUNK // dec_seq
    corner = jnp.where(tril[None, :dec_seq, :dec_seq], sgu_w[0][:, :dec_seq, :dec_seq], 0)
    sguw_s = jnp.einsum('ab,gij->gaibj', jnp.eye(reps, dtype=F32), corner).reshape(
        N_GROUPS, GMLP_CHUNK, GMLP_CHUNK).astype(BF16)
    sgub_s = jnp.broadcast_to(jnp.tile(sgu_b[0][:, :dec_seq], (1, reps))[:, :, None],
                              (N_GROUPS, GMLP_CHUNK, GROUP_W)).astype(F32)
    bias_p = _stacked_bias(rel_bias_table, attn_sinks[0], jnp.arange(CHUNK) + WINDOW, jnp.arange(BAND))
    bias_s = _stacked_bias(rel_bias_table, attn_sinks[0], cache_rows + jnp.arange(dec_seq),
                           jnp.arange(cache_rows + dec_seq))
    tri = jnp.triu(jnp.ones((T, T), F32), k=1).astype(BF16)
    low = jnp.tril(jnp.ones((N_EXPERTS, N_EXPERTS), F32), k=-1).astype(BF16)

    layer_out_shapes = (
        jax.ShapeDtypeStruct((n_tok, D_MODEL), F32),
        jax.ShapeDtypeStruct((n_tok, D_MODEL), BF16),
        jax.ShapeDtypeStruct((n_tiles, ROUTE_ROWS, T), F32),
        jax.ShapeDtypeStruct((n_tok, LANES), F32),
        jax.ShapeDtypeStruct((n_tiles, N_EXPERTS, LANES), F32),
    )
    shared_consts = (w_q_b, w_in_b)
    tail_consts_p = (sguw_p, sgub_p, lng, lnb, woa_b, wos_b, wout_b, g2, wr_b, br_col, tri, low)
    tail_consts_s = (sguw_s, sgub_s, lng, lnb, woa_b, wos_b, wout_b, g2, wr_b, br_col, tri, low)

    def tile_specs(tile_of):
        return [
            pl.BlockSpec((LT, D_MODEL), lambda *g: (tile_of(*g), 0)),
            pl.BlockSpec((LT, D_MODEL), lambda *g: (tile_of(*g), 0)),
            pl.BlockSpec((sorts_per_step, ROUTE_ROWS, T), lambda *g: (tile_of(*g), 0, 0)),
            pl.BlockSpec((LT, LANES), lambda *g: (tile_of(*g), 0)),
            pl.BlockSpec((sorts_per_step, N_EXPERTS, LANES), lambda *g: (tile_of(*g), 0, 0)),
        ]

    prompt_consts = (g1,) + shared_consts + (bias_p,) + tail_consts_p
    x1, h, route, routet, cnt, kvwin = pl.pallas_call(
        _prompt_layer_kernel,
        grid=(batch, tiles_per_seq),
        in_specs=[pl.BlockSpec((1, LT, D_MODEL), lambda b, j: (b, j, 0))]
                 + [_const_spec(c.shape) for c in prompt_consts],
        out_specs=tile_specs(lambda b, j: b * tiles_per_seq + j)
                  + [pl.BlockSpec((1, WINDOW, 2 * D_KV), lambda b, j: (b, 0, 0))],
        out_shape=layer_out_shapes + (jax.ShapeDtypeStruct((batch, WINDOW, 2 * D_KV), F32),),
        scratch_shapes=[pltpu.VMEM((LT + WINDOW, 2 * D_KV), BF16),
                        pltpu.VMEM((LT, D_ATT), BF16),
                        pltpu.VMEM((LT, D_GMLP), BF16)],
        compiler_params=pltpu.CompilerParams(dimension_semantics=("arbitrary", "arbitrary"),
                                             vmem_limit_bytes=VMEM_LIMIT),
        name="layer_prompt",
    )(x_prompt, *prompt_consts)

    xs_flat = x_sample.reshape(n_sample, D_MODEL)
    ck = cache_win_k[0].reshape(dec_batch, cache_rows, D_KV)
    cv = cache_win_v[0].reshape(dec_batch, cache_rows, D_KV)
    sample_consts = (g1,) + shared_consts + (bias_s,) + tail_consts_s
    any_spec = pl.BlockSpec(memory_space=pl.ANY)
    x1, h, route, routet, cnt, kvnew, vn_s = pl.pallas_call(
        _sample_layer_kernel,
        grid=(n_ssteps,),
        in_specs=[any_spec] * 5
                 + [pl.BlockSpec((LT, D_MODEL), lambda i: (i, 0)),
                    pl.BlockSpec((seqs_per_tile, cache_rows, D_KV), lambda i: (i, 0, 0)),
                    pl.BlockSpec((seqs_per_tile, cache_rows, D_KV), lambda i: (i, 0, 0))]
                 + [_const_spec(c.shape) for c in sample_consts],
        out_specs=tile_specs(lambda i: n_psteps + i)
                  + [pl.BlockSpec((LT, 2 * D_KV), lambda i: (i, 0)),
                     pl.BlockSpec((LT, D_GMLP), lambda i: (i, 0))],
        out_shape=layer_out_shapes + (jax.ShapeDtypeStruct((n_sample, 2 * D_KV), F32),
                                      jax.ShapeDtypeStruct((n_sample, D_GMLP), F32)),
        scratch_shapes=[pltpu.VMEM((LT, D_ATT), BF16), pltpu.VMEM((LT, D_GMLP), BF16)],
        input_output_aliases={0: 0, 1: 1, 2: 2, 3: 3, 4: 4},
        compiler_params=pltpu.CompilerParams(dimension_semantics=("arbitrary",),
                                             vmem_limit_bytes=VMEM_LIMIT),
        name="layer_sample",
    )(x1, h, route, routet, cnt, xs_flat, ck, cv, *sample_consts)

    blk_gran = SUB_BLOCK // GRAN
    counts =cnt[:, :, 0].astype(I32)
    seg_gran = (counts + (GRAN - 1)) // GRAN
    local_off = jnp.cumsum(seg_gran, axis=1) - seg_gran
    tot_gran = jnp.sum(seg_gran, axis=0)
    ptot_gran = (tot_gran + (blk_gran - 1)) // blk_gran * blk_gran
    pend_gran = jnp.cumsum(ptot_gran)
    gstart = pend_gran - ptot_gran
    global_off = gstart[None, :] + jnp.cumsum(seg_gran, axis=0) - seg_gran
    pad_n = ptot_gran - tot_gran
    pad_off = gstart + tot_gran
    n_rows = -(-(TOP_K * n_tok + n_tiles * N_EXPERTS * (GRAN - 1) + N_EXPERTS * (SUB_BLOCK - GRAN))
               // SUB_BLOCK) * SUB_BLOCK + ROW_BLOCK
    region_gran0 = gstart.astype(I32)
    region_subs = (ptot_gran // blk_gran).astype(I32)
    tile_gran = jnp.sum(seg_gran, axis=1).astype(I32)
    tile_big = jnp.any(seg_gran >= 4 * (FLAT_QUADS + 1), axis=1).astype(I32)
    seg_gran_f = seg_gran.reshape(-1).astype(I32)
    local_off_f = local_off.reshape(-1).astype(I32)
    global_off_f = global_off.reshape(-1).astype(I32)

    xs = pl.pallas_call(
        _dispatch_kernel,
        grid_spec=pltpu.PrefetchScalarGridSpec(
            num_scalar_prefetch=7,
            grid=(n_tiles,),
            in_specs=[pl.BlockSpec((1, ROUTE_ROWS, T), lambda i, *_: (i, 0, 0)),
                      pl.BlockSpec((T, D_MODEL), lambda i, *_: (i, 0))],
            out_specs=pl.BlockSpec(memory_space=pl.ANY),
            scratch_shapes=[pltpu.VMEM((2, LOCAL_ROWS // GRAN, GRAN, SLOT_WORDS), U32),
                            pltpu.VMEM((SUB_BLOCK // GRAN, GRAN, SLOT_WORDS), U32),
                            pltpu.SemaphoreType.DMA((3,))]),
        out_shape=jax.ShapeDtypeStruct((n_rows // GRAN, GRAN, SLOT_WORDS), U32),
        compiler_params=pltpu.CompilerParams(dimension_semantics=("arbitrary",),
                                             vmem_limit_bytes=VMEM_LIMIT),
        name="moe_dispatch",
    )(seg_gran_f, local_off_f, global_off_f, tile_gran, tile_big, pad_n.astype(I32), pad_off.astype(I32),
      route, h)

    ys = pl.pallas_call(
        _expert_kernel,
        grid_spec=pltpu.PrefetchScalarGridSpec(
            num_scalar_prefetch=2,
            grid=(N_EXPERTS,),
            in_specs=[pl.BlockSpec(memory_space=pl.ANY),
                      pl.BlockSpec((1, D_MODEL, 2 * D_FF), lambda e, *_: (e, 0, 0)),
                      pl.BlockSpec((1, 1, 2 * D_FF), lambda e, *_: (e, 0, 0)),
                      pl.BlockSpec((1, D_FF, D_MODEL), lambda e, *_: (e, 0, 0)),
                      pl.BlockSpec((1, 1, D_MODEL), lambda e, *_: (e, 0, 0))],
            out_specs=pl.BlockSpec(memory_space=pl.ANY),
            scratch_shapes=[pltpu.VMEM((D_MODEL, 2 * D_FF), BF16), pltpu.VMEM((D_FF, D_MODEL), BF16),
                            pltpu.VMEM((2, ROW_BLOCK // GRAN, GRAN, SLOT_WORDS), U32),
                            pltpu.VMEM((2, ROW_BLOCK // GRAN, GRAN, SLOT_WORDS), U32),
                            pltpu.SemaphoreType.DMA((2,)), pltpu.SemaphoreType.DMA((2,))]),
        out_shape=jax.ShapeDtypeStruct((n_rows // GRAN, GRAN, SLOT_WORDS), U32),
        compiler_params=pltpu.CompilerParams(dimension_semantics=("arbitrary",),
                                             vmem_limit_bytes=VMEM_LIMIT),
        name="moe_experts",
    )(region_gran0, region_subs, xs, w_exp_in[0], b_exp_in[0].reshape(N_EXPERTS, 1, 2 * D_FF),
      w_exp_out[0], b_exp_out[0].reshape(N_EXPERTS, 1, D_MODEL))

    y_p, y_s = pl.pallas_call(
        functools.partial(_combine_kernel, n_ptiles),
        grid_spec=pltpu.PrefetchScalarGridSpec(
            num_scalar_prefetch=5,
            grid=(n_tiles,),
            in_specs=[pl.BlockSpec((T, LANES), lambda i, *_: (i, 0)),
                      pl.BlockSpec((T, D_MODEL), lambda i, *_: (i, 0)),
                      pl.BlockSpec((1, D_MODEL), lambda i, *_: (0, 0)),
                      pl.BlockSpec(memory_space=pl.ANY)],
            out_specs=[pl.BlockSpec((T, D_MODEL), lambda i, *_: (jnp.minimum(i, n_ptiles - 1), 0)),
                       pl.BlockSpec((T, D_MODEL), lambda i, *_: (jnp.maximum(i - n_ptiles, 0), 0))],
            scratch_shapes=[pltpu.VMEM((2, LOCAL_ROWS // GRAN, GRAN, SLOT_WORDS), U32), pltpu.SemaphoreType.DMA((2,))]),
        out_shape=(jax.ShapeDtypeStruct((n_prompt, D_MODEL), F32),
                   jax.ShapeDtypeStruct((n_sample, D_MODEL), F32)),
        compiler_params=pltpu.CompilerParams(dimension_semantics=("arbitrary",),
                                             vmem_limit_bytes=VMEM_LIMIT),
        name="moe_combine",
    )(seg_gran_f, local_off_f, global_off_f, tile_gran, tile_big, routet, x1, gf, ys)

    y_prompt = y_p.reshape(batch, seq, D_MODEL)
    y_sample = y_s.reshape(dec_batch, dec_seq, D_MODEL)
    new_win_k_prompt = kvwin[:, :, :D_KV].reshape(1, batch, WINDOW, N_KV, HEAD_DIM)
    new_win_v_prompt = kvwin[:, :, D_KV:].reshape(1, batch, WINDOW, N_KV, HEAD_DIM)
    new_win_k_sample = kvnew[:, :D_KV].reshape(1, dec_batch, dec_seq, N_KV, HEAD_DIM)
    new_win_v_sample = kvnew[:, D_KV:].reshape(1, dec_batch, dec_seq, N_KV, HEAD_DIM)
    new_sgu_v_sample = vn_s.reshape(1, dec_batch, dec_seq, D_GMLP)
    return (y_prompt, y_sample, new_win_k_prompt, new_win_v_prompt, new_win_k_sample,
            new_win_v_sample, new_sgu_v_sample)
```

```python
import functools
import math

import numpy as np
import jax
import jax.numpy as jnp
from jax import lax
from jax.experimental import pallas as pl
from jax.experimental.pallas import tpu as pltpu

F32 = jnp.float32
BF16 = jnp.bfloat16
I32 = jnp.int32
U32 = jnp.uint32

D_MODEL = 1024
HEAD_DIM = 64
N_HEADS = 16
N_KV = 2
REP = N_HEADS // N_KV
CHUNK = 64
WINDOW = 128
BAND = WINDOW + CHUNK
KEY_PAD = 256
D_ATT = N_HEADS * HEAD_DIM
D_KV = N_KV * HEAD_DIM
NUM_BUCKETS = 32
MAX_DISTANCE = 128
GMLP_CHUNK = 128
D_GMLP = 1024
N_GROUPS = 4
GROUP_W = D_GMLP // N_GROUPS
N_EXPERTS = 32
TOP_K = 4
D_FF = 1024
SWIGLU_LIMIT = 7.0
SWIGLU_ALPHA = 1.702
NORM_EPS = 1e-5
D_IN = D_ATT + 2 * D_KV + 2 * D_GMLP + 2 * D_MODEL
COL_KV = D_ATT
COL_U = COL_KV + 2 * D_KV
COL_VG = COL_U + D_GMLP
COL_GA = COL_VG + D_GMLP
COL_GB = COL_GA + D_MODEL
SQRT_HALF = float(np.sqrt(0.5))

LANES = 128
WORD_SUBLANES = 8
VMEM_LIMIT = 56 * 1024 * 1024

TOK_TILE = 256
LAYER_TILE = 512
GRAN = WORD_SUBLANES
SLOT_WORDS = D_MODEL // 2
FLAT_QUADS = 1
LOCAL_ROWS = TOP_K * TOK_TILE + N_EXPERTS * GRAN
ROW_BLOCK = 512
SUB_BLOCK = 128
PAD_BITS = (SUB_BLOCK // GRAN - 1).bit_length()
TILE_BITS = (LOCAL_ROWS // GRAN).bit_length()
PROJ_PIECES = 16
ROUTE_ROWS = 16


def _const_spec(shape):
    nd = len(shape)
    return pl.BlockSpec(shape, lambda *_: (0,) * nd, pipeline_mode=pl.Buffered(1))


def _rms_norm(x, g):
    ms = jnp.mean(x * x, axis=-1, keepdims=True)
    return x * lax.rsqrt(ms + NORM_EPS) * g


def _gelu(x):
    return 0.5 * x * (1.0 + lax.erf(x * SQRT_HALF))


def _attend(q, k, v, bias, valid):
    logits = lax.dot_general(q, k, (((1,), (1,)), ((), ())), preferred_element_type=F32) + bias
    if valid is not None:
        logits = jnp.where(valid, logits, -jnp.inf)
    m = jnp.max(logits, axis=-1, keepdims=True)
    p = jnp.exp(logits - m)
    den = jnp.sum(p, axis=-1, keepdims=True)
    w = (p * (1.0 / den)).astype(BF16)
    return jnp.dot(w, v, preferred_element_type=F32)


def _attention_rows(zq, row0, n_rows, k_ext, v_ext, bias_ref, valid, oatt_ref):
    lane = lax.broadcasted_iota(I32, (1, LANES), 1)
    stack = []
    for g in range(N_KV):
        in_group = (lane >= g * HEAD_DIM) & (lane < (g + 1) * HEAD_DIM)
        qmask = jnp.where(in_group, HEAD_DIM ** -0.5, 0.0).astype(BF16)
        stack += [zq[row0:row0 + n_rows, t * LANES:(t + 1) * LANES] * qmask for t in range(REP)]
    bias = bias_ref[...].reshape(N_HEADS * n_rows, KEY_PAD)
    o = _attend(jnp.concatenate(stack, axis=0), k_ext, v_ext, bias, valid)
    for g in range(N_KV):
        for t in range(REP):
            r = (g * REP + t) * n_rows
            c0 = t * LANES + g * HEAD_DIM
            oatt_ref[row0:row0 + n_rows, c0:c0 + HEAD_DIM] = (
                o[r:r + n_rows, g * HEAD_DIM:(g + 1) * HEAD_DIM].astype(BF16))


def _proj_piece(xn_b, w_in_ref, k):
    width = (D_IN - COL_U) // PROJ_PIECES
    c0 = COL_U + k * width
    z = jnp.dot(xn_b, w_in_ref[:, c0:c0 + width], preferred_element_type=F32)
    return _gelu(z) if c0 < COL_GA else jax.nn.sigmoid(z)


def _sgu_branch(pieces, osgu_ref, sguw_ref, sgub_ref, lng_ref, lnb_ref, vn_ref):
    u = jnp.concatenate(pieces[:PROJ_PIECES // 4], axis=1)
    vg = jnp.concatenate(pieces[PROJ_PIECES // 4:PROJ_PIECES // 2], axis=1)
    T = u.shape[0]
    mu = jnp.mean(vg, axis=-1, keepdims=True)
    var = jnp.mean(jnp.square(vg - mu), axis=-1, keepdims=True)
    vn = (vg - mu) * lax.rsqrt(var + NORM_EPS) * lng_ref[...] + lnb_ref[...]
    if vn_ref is not None:
        vn_ref[...] = vn
    vn_b = vn.astype(BF16)
    for j in range(T // GMLP_CHUNK):
        rows = slice(j * GMLP_CHUNK, (j + 1) * GMLP_CHUNK)
        for g in range(N_GROUPS):
            cols = slice(g * GROUP_W, (g + 1) * GROUP_W)
            mixed = jnp.dot(sguw_ref[g], vn_b[rows, cols], preferred_element_type=F32) + sgub_ref[g]
            osgu_ref[rows, cols] = (u[rows, cols] * mixed).astype(BF16)


def _layer_tail(x, pieces, oatt_ref, osgu_ref, woa_ref, wos_ref, wout_ref, g2_ref, wr_ref, br_ref,
                tri_ref, low_ref, x1_ref, h_ref, route_ref, routet_ref, cnt_ref):
    T = x.shape[0]
    ga = jnp.concatenate(pieces[PROJ_PIECES // 2:3 * PROJ_PIECES // 4], axis=1)
    gb = jnp.concatenate(pieces[3 * PROJ_PIECES // 4:], axis=1)
    merged = ga * jnp.dot(oatt_ref[...], woa_ref[...], preferred_element_type=F32)
    merged = merged + gb * jnp.dot(osgu_ref[...], wos_ref[...], preferred_element_type=F32)
    x1 = x + jnp.dot(merged.astype(BF16), wout_ref[...], preferred_element_type=F32)
    x1_ref[...] = x1

    h_b = _rms_norm(x1, g2_ref[...]).astype(BF16)
    h_ref[...] = h_b
    lt = lax.dot_general(wr_ref[...], h_b, (((1,), (1,)), ((), ())),
                         preferred_element_type=F32) + br_ref[...]
    for s in range(T // TOK_TILE):
        rec, cnt_b = _route_sort_tile(lt[:, s * TOK_TILE:(s + 1) * TOK_TILE], tri_ref, low_ref)
        route_ref[s] = rec
        rec_pad = jnp.concatenate([rec, jnp.zeros((LANES - ROUTE_ROWS, TOK_TILE), F32)], axis=0)
        routet_ref[s * TOK_TILE:(s + 1) * TOK_TILE, :] = jnp.transpose(rec_pad)
        cnt_ref[s] = cnt_b


def _route_sort_tile(lt, tri_ref, low_ref):
    T = lt.shape[1]
    e_iota = lax.broadcasted_iota(I32, (N_EXPERTS, T), 0).astype(F32)
    cur = lt
    vals, idxs = [], []
    for _ in range(TOP_K):
        m = jnp.max(cur, axis=0, keepdims=True)
        ik = jnp.min(jnp.where(cur == m, e_iota, float(N_EXPERTS)), axis=0, keepdims=True)
        vals.append(m)
        idxs.append(ik)
        cur = jnp.where(e_iota == ik, -jnp.inf, cur)
    exps = [jnp.exp(v - vals[0]) for v in vals]
    den = exps[0] + exps[1] + exps[2] + exps[3]
    gates = [e / den for e in exps]

    onehot = jnp.zeros((N_EXPERTS, T), F32)
    for ik in idxs:
        onehot = onehot + jnp.where(e_iota == ik, 1.0, 0.0)
    rank = jnp.dot(onehot.astype(BF16), tri_ref[...], preferred_element_type=F32)
    cnt = jnp.sum(onehot, axis=1, keepdims=True)
    cnt_b = jnp.broadcast_to(cnt, (N_EXPERTS, LANES))
    gran = jnp.floor((cnt_b + (GRAN - 1)) * (1.0 / GRAN))
    off = jnp.dot(low_ref[...], gran.astype(BF16), preferred_element_type=F32) * GRAN
    base = off[:, 0:1] + rank
    poss = [jnp.sum(jnp.where(e_iota == ik, base, 0.0), axis=0, keepdims=True) for ik in idxs]

    rec = jnp.concatenate(idxs + poss + gates
                          + [jnp.zeros((ROUTE_ROWS - 3 * TOP_K, T), F32)], axis=0)
    return rec, cnt_b


def _prompt_layer_kernel(x_ref, g1_ref, wq_ref, w_in_ref, bias_ref, sguw_ref, sgub_ref, lng_ref,
                         lnb_ref, woa_ref, wos_ref, wout_ref, g2_ref, wr_ref, br_ref, tri_ref, low_ref,
                         x1_ref, h_ref, route_ref, routet_ref, cnt_ref, kvwin_ref,
                         kvx_ref, oatt_ref, osgu_ref):
    T = LAYER_TILE
    j = pl.program_id(1)
    x = x_ref[0]
    xn_b = _rms_norm(x, g1_ref[...]).astype(BF16)
    zq = jnp.dot(xn_b, wq_ref[...], preferred_element_type=F32).astype(BF16)
    zkv = jnp.dot(xn_b, w_in_ref[:, COL_KV:COL_U], preferred_element_type=F32)

    @pl.when(j == pl.num_programs(1) - 1)
    def _():
        kvwin_ref[0] = zkv[T - WINDOW:, :]

    @pl.when(j == 0)
    def _():
        kvx_ref[0:WINDOW, :] = jnp.zeros((WINDOW, 2 * D_KV), BF16)

    @pl.when(j > 0)
    def _():
        kvx_ref[0:WINDOW, :] = kvx_ref[T:T + WINDOW, :]

    kvx_ref[WINDOW:, :] = zkv.astype(BF16)

    col = lax.broadcasted_iota(I32, (1, KEY_PAD), 1)
    key_pad = jnp.zeros((KEY_PAD - BAND, 2 * D_KV), BF16)
    n_chunks = T // CHUNK
    pieces = []
    for c in range(n_chunks):
        r0 = c * CHUNK
        kvb = jnp.concatenate([kvx_ref[r0:r0 + BAND, :], key_pad], axis=0)
        valid = ((col + (j * T + r0 - WINDOW)) >= 0) | (col >= BAND) if r0 < WINDOW else None
        _attention_rows(zq, r0, CHUNK, kvb[:, :D_KV], kvb[:, D_KV:], bias_ref, valid, oatt_ref)
        pieces += [_proj_piece(xn_b, w_in_ref, k) for k in range(len(pieces), (c + 1) * PROJ_PIECES // n_chunks)]

    _sgu_branch(pieces, osgu_ref, sguw_ref, sgub_ref, lng_ref, lnb_ref, None)
    _layer_tail(x, pieces, oatt_ref, osgu_ref, woa_ref, wos_ref, wout_ref, g2_ref, wr_ref, br_ref,
                tri_ref, low_ref, x1_ref, h_ref, route_ref, routet_ref, cnt_ref)


def _sample_layer_kernel(x1_in, h_in, route_in, routet_in, cnt_in,
                         x_ref, ck_ref, cv_ref, g1_ref, wq_ref, w_in_ref, bias_ref, sguw_ref, sgub_ref,
                         lng_ref, lnb_ref, woa_ref, wos_ref, wout_ref, g2_ref, wr_ref, br_ref, tri_ref,
                         low_ref,
                         x1_ref, h_ref, route_ref, routet_ref, cnt_ref, kvnew_ref, vn_ref,
                         oatt_ref, osgu_ref):
    del x1_in, h_in, route_in, routet_in, cnt_in
    T = LAYER_TILE
    nq = x_ref.shape[0] // ck_ref.shape[0]
    x = x_ref[...]
    xn_b = _rms_norm(x, g1_ref[...]).astype(BF16)
    zq = jnp.dot(xn_b, wq_ref[...], preferred_element_type=F32).astype(BF16)
    zkv = jnp.dot(xn_b, w_in_ref[:, COL_KV:COL_U], preferred_element_type=F32)
    kvnew_ref[...] = zkv
    zkv_b = zkv.astype(BF16)
    n_cache = ck_ref.shape[1]
    key_pad = jnp.zeros((KEY_PAD - n_cache - nq, D_KV), BF16)
    n_seqs = T // nq
    pieces = []
    for b in range(n_seqs):
        r0 = b * nq
        kk = jnp.concatenate([ck_ref[b].astype(BF16), zkv_b[r0:r0 + nq, :D_KV], key_pad], axis=0)
        vv = jnp.concatenate([cv_ref[b].astype(BF16), zkv_b[r0:r0 + nq, D_KV:], key_pad], axis=0)
        _attention_rows(zq, r0, nq, kk, vv, bias_ref, None, oatt_ref)
        pieces += [_proj_piece(xn_b, w_in_ref, k) for k in range(len(pieces), (b + 1) * PROJ_PIECES // n_seqs)]

    _sgu_branch(pieces, osgu_ref, sguw_ref, sgub_ref, lng_ref, lnb_ref, vn_ref)
    _layer_tail(x, pieces, oatt_ref, osgu_ref, woa_ref, wos_ref, wout_ref, g2_ref, wr_ref, br_ref,
                tri_ref, low_ref, x1_ref, h_ref, route_ref, routet_ref, cnt_ref)


def _pack_slot_rows(x):
    lo = lax.bitcast_convert_type(x[:, :SLOT_WORDS].astype(BF16).astype(F32), U32)
    hi = lax.bitcast_convert_type(x[:, SLOT_WORDS:].astype(BF16).astype(F32), U32)
    return (lo >> 16) | hi


def _pack_slot_rows_exact(x):
    lo = lax.bitcast_convert_type(x[:, :SLOT_WORDS], U32)
    hi = lax.bitcast_convert_type(x[:, SLOT_WORDS:], U32)
    return (lo >> 16) | hi


def _unpack_slot_rows(w):
    lo = lax.bitcast_convert_type(w << 16, F32)
    hi = lax.bitcast_convert_type(w & jnp.uint32(0xFFFF0000), F32)
    return jnp.concatenate([lo, hi], axis=1).astype(BF16)


def _as_granules(x):
    return x.reshape(x.shape[0] // GRAN, GRAN, SLOT_WORDS)


def _as_rows(x):
    return x.reshape(x.shape[0] * GRAN, SLOT_WORDS)


def _segment_copies(n_gran, src_gran, dst_gran, bits, make_copy, act):
    for b in range(min(bits, 2)):
        @pl.when(((n_gran >> b) & 1) == 1)
        def _(b=b):
            done = n_gran & ((1 << b) - 1)
            act(make_copy(src_gran + done, dst_gran + done, 1 << b))

    if bits > 2:
        def quad(q, carry):
            done = (n_gran & 3) + 4 * q
            act(make_copy(src_gran + done, dst_gran + done, 4))
            return carry
        lax.fori_loop(0, n_gran >> 2, quad, 0)


def _start_tile_copies(tile, live, cnt_ref, loff_ref, goff_ref, big_ref, make_copy):
    for e in range(N_EXPERTS):
        t = tile * N_EXPERTS + e
        n = jnp.where(live, cnt_ref[t], 0)
        src, dst = loff_ref[t], goff_ref[t]
        for b in range(2):
            @pl.when(((n >> b) & 1) == 1)
            def _(b=b, n=n, src=src, dst=dst):
                done = n & ((1 << b) - 1)
                make_copy(src + done, dst + done, 1 << b).start()
        for q in range(FLAT_QUADS):
            @pl.when((n >> 2) > q)
            def _(q=q, n=n, src=src, dst=dst):
                done = (n & 3) + 4 * q
                make_copy(src + done, dst + done, 4).start()

    @pl.when(live & (big_ref[tile] > 0))
    def _():
        def rest(e, carry):
            t = tile * N_EXPERTS + e
            n, src, dst = cnt_ref[t], loff_ref[t], goff_ref[t]

            def quad(q, c):
                done = (n & 3) + 4 * q
                make_copy(src + done, dst + done, 4).start()
                return c
            lax.fori_loop(FLAT_QUADS, n >> 2, quad, 0)
            return carry
        lax.fori_loop(0, N_EXPERTS, rest, 0)


def _wait_granules(n_gran, bits, make_copy):
    for b in range(bits):
        @pl.when(((n_gran >> b) & 1) == 1)
        def _(b=b):
            make_copy(1 << b).wait()


def _dispatch_kernel(cnt_ref, loff_ref, goff_ref, tot_ref, big_ref, padn_ref, padoff_ref,
                     route_ref, h_ref, xs_hbm, stage_ref, zero_ref, sem):
    i = pl.program_id(0)
    last = pl.num_programs(0) - 1
    slot = lax.rem(i, 2)
    T = TOK_TILE

    def seg_copy(tile_slot):
        return lambda s, d, n: pltpu.make_async_copy(
            stage_ref.at[tile_slot, pl.ds(s, n)], xs_hbm.at[pl.ds(d, n)], sem.at[tile_slot])

    def wait_tile(tile, tile_slot):
        _wait_granules(tot_ref[tile], TILE_BITS, lambda n: pltpu.make_async_copy(
            stage_ref.at[tile_slot, pl.ds(0, n)], xs_hbm.at[pl.ds(0, n)], sem.at[tile_slot]))

    @pl.when(i > 1)
    def _():
        wait_tile(i - 2, slot)

    _start_tile_copies(jnp.maximum(i - 1, 0), i > 0, cnt_ref, loff_ref, goff_ref, big_ref, seg_copy(1 - slot))

    pos = route_ref[0, TOP_K:2 * TOP_K, :].astype(I32)
    r_iota = lax.broadcasted_iota(I32, (LOCAL_ROWS, T), 0)
    p = jnp.zeros((LOCAL_ROWS, T), F32)
    for k in range(TOP_K):
        p = jnp.where(r_iota == pos[k:k + 1, :], 1.0, p)
    stage_ref[slot] = _as_granules(
        _pack_slot_rows_exact(jnp.dot(p.astype(BF16), h_ref[...], preferred_element_type=F32)))

    @pl.when(i == last)
    def _():
        _start_tile_copies(i, i >= 0, cnt_ref, loff_ref, goff_ref, big_ref, seg_copy(slot))

        @pl.when(i > 0)
        def _():
            wait_tile(i - 1, 1 - slot)

        wait_tile(i, slot)
        zero_ref[...] = jnp.zeros(zero_ref.shape, U32)

        def pad_copy(s, d, n):
            return pltpu.make_async_copy(zero_ref.at[pl.ds(s, n)], xs_hbm.at[pl.ds(d, n)], sem.at[2])

        def for_pads(act):
            def body(e, carry):
                _segment_copies(padn_ref[e], 0, padoff_ref[e], PAD_BITS, pad_copy, act)
                return carry
            lax.fori_loop(0, N_EXPERTS, body, 0)

        for_pads(lambda cp: cp.start())
        for_pads(lambda cp: cp.wait())


def _expert_kernel(gran0_ref, nsub_ref, xs_hbm, w1_ref, b1_ref, w2_ref, b2_ref, ys_hbm,
                   w1b_ref, w2b_ref, xbuf_ref, ybuf_ref, sem_in, sem_out):
    e = pl.program_id(0)
    subs = ROW_BLOCK // SUB_BLOCK
    sub_gran = SUB_BLOCK // GRAN
    blk_gran = ROW_BLOCK // GRAN
    gran0 = gran0_ref[e]
    nsub = nsub_ref[e]
    nb_full = nsub // subs
    rem = nsub - nb_full * subs
    nb = nb_full + jnp.where(rem > 0, 1, 0)
    last_slot = lax.rem(nb + 1, 2)

    def in_copy(k, slot):
        return pltpu.make_async_copy(xs_hbm.at[pl.ds(gran0 + k * blk_gran, blk_gran)], xbuf_ref.at[slot],
                                     sem_in.at[slot])

    def out_copy(k, slot, m_sub):
        n = m_sub * sub_gran
        return pltpu.make_async_copy(ybuf_ref.at[slot, pl.ds(0, n)], ys_hbm.at[pl.ds(gran0 + k * blk_gran, n)],
                                     sem_out.at[slot])

    @pl.when(nb > 0)
    def _():
        in_copy(0, 0).start()

    w1b_ref[...] = w1_ref[0].astype(BF16)
    w2b_ref[...] = w2_ref[0].astype(BF16)

    def process(k, m_sub):
        n = m_sub * sub_gran
        slot = lax.rem(k, 2)

        @pl.when(k + 1 < nb)
        def _():
            in_copy(k + 1, 1 - slot).start()

        in_copy(k, slot).wait()

        @pl.when(k >= 2)
        def _():
            out_copy(k - 2, slot, subs).wait()

        x = _unpack_slot_rows(_as_rows(xbuf_ref[slot, 0:n]))
        h1 = jnp.dot(x, w1b_ref[...], preferred_element_type=F32) + b1_ref[0]
        gate = jnp.minimum(h1[:, :D_FF], SWIGLU_LIMIT)
        up = jnp.clip(h1[:, D_FF:], -SWIGLU_LIMIT, SWIGLU_LIMIT)
        act = gate * jax.nn.sigmoid(SWIGLU_ALPHA * gate) * (up + 1.0)
        y = jnp.dot(act.astype(BF16), w2b_ref[...], preferred_element_type=F32) + b2_ref[0]
        ybuf_ref[slot, 0:n] = _as_granules(_pack_slot_rows(y))
        out_copy(k, slot, m_sub).start()

    def full_block(k, carry):
        process(k, subs)
        return carry

    lax.fori_loop(0, nb_full, full_block, 0)

    for m_sub in range(1, subs):
        @pl.when(rem == m_sub)
        def _(m_sub=m_sub):
            process(nb_full, m_sub)

    @pl.when(nb >= 2)
    def _():
        out_copy(nb - 2, 1 - last_slot, subs).wait()

    @pl.when((nb >= 1) & (rem == 0))
    def _():
        out_copy(nb - 1, last_slot, subs).wait()

    for m_sub in range(1, subs):
        @pl.when(rem == m_sub)
        def _(m_sub=m_sub):
            out_copy(nb - 1, last_slot, m_sub).wait()


def _combine_kernel(n_prompt_tiles, cnt_ref, loff_ref, goff_ref, tot_ref, big_ref,
                    routet_ref, x1_ref, gf_ref, ys_hbm, yp_ref, ysm_ref, stage_ref, sem):
    i = pl.program_id(0)
    n_steps = pl.num_programs(0)
    slot = lax.rem(i, 2)
    T = TOK_TILE

    def seg_copy(tile_slot):
        return lambda s, d, n: pltpu.make_async_copy(
            ys_hbm.at[pl.ds(d, n)], stage_ref.at[tile_slot, pl.ds(s, n)], sem.at[tile_slot])

    @pl.when(i == 0)
    def _():
        stage_ref[...] = jnp.zeros(stage_ref.shape, U32)
        _start_tile_copies(i, i >= 0, cnt_ref, loff_ref, goff_ref, big_ref, seg_copy(slot))

    _start_tile_copies(jnp.minimum(i + 1, n_steps - 1), i + 1 < n_steps, cnt_ref, loff_ref, goff_ref, big_ref,
                       seg_copy(1 - slot))

    rt = routet_ref[...]
    l_iota = lax.broadcasted_iota(I32, (T, LOCAL_ROWS), 1)
    pw = jnp.zeros((T, LOCAL_ROWS), F32)
    for k in range(TOP_K):
        pos_k = rt[:, TOP_K + k:TOP_K + k + 1].astype(I32)
        pw = jnp.where(l_iota == pos_k, rt[:, 2 * TOP_K + k:2 * TOP_K + k + 1], pw)

    _wait_granules(tot_ref[i], TILE_BITS, lambda n: pltpu.make_async_copy(
        ys_hbm.at[pl.ds(0, n)], stage_ref.at[slot, pl.ds(0, n)], sem.at[slot]))
    moe = jnp.dot(pw.astype(BF16), _unpack_slot_rows(_as_rows(stage_ref[slot])), preferred_element_type=F32)
    y = _rms_norm(x1_ref[...] + moe, gf_ref[...])

    @pl.when(i < n_prompt_tiles)
    def _():
        yp_ref[...] = y

    @pl.when(i >= n_prompt_tiles)
    def _():
        ysm_ref[...] = y


def _t5_bucket(rel):
    half = NUM_BUCKETS // 2
    max_exact = half // 2
    ret = jnp.where(rel > 0, half, 0)
    n = jnp.abs(rel)
    nf = jnp.maximum(n, 1).astype(F32)
    large = max_exact + (jnp.log(nf / max_exact) / math.log(MAX_DISTANCE / max_exact)
                         * (half - max_exact)).astype(jnp.int32)
    large = jnp.minimum(large, half - 1)
    return ret + jnp.where(n < max_exact, n, large)


def _stacked_bias(table, sinks, q_pos, k_pos):
    nq, nk = q_pos.shape[0], k_pos.shape[0]
    bucket = _t5_bucket(k_pos[None, :] - q_pos[:, None])
    onehot = (bucket[:, :, None] == jnp.arange(NUM_BUCKETS)).astype(F32)
    bias = jnp.einsum('qkb,bh->hqk', onehot, table.astype(F32), precision=lax.Precision.HIGHEST)
    bias = bias.reshape(N_KV, REP * nq, nk)
    sink = jnp.repeat(sinks.astype(F32).reshape(N_KV, REP, 1), nq, axis=2).reshape(N_KV, REP * nq, 1)
    tail = jnp.full((N_KV, REP * nq, KEY_PAD - nk - 1), -jnp.inf, F32)
    return jnp.concatenate([bias, sink, tail], axis=-1)


def kernel(x_prompt, x_sample, cache_win_k, cache_win_v, norm1_g, w_in, attn_sinks, rel_bias_table, sgu_ln_g, sgu_ln_b, sgu_w, sgu_b, w_o_att, w_o_sgu, w_out, norm2_g, w_router, b_router, w_exp_in, b_exp_in, w_exp_out, b_exp_out, final_norm_g):
    batch, seq, _ = x_prompt.shape
    dec_batch, dec_seq, _ = x_sample.shape
    cache_rows = cache_win_k.shape[2]
    assert x_prompt.shape[2] == D_MODEL and w_in.shape == (1, D_MODEL, D_IN)
    assert seq % LAYER_TILE == 0 and LAYER_TILE % TOK_TILE == 0 and TOK_TILE % GMLP_CHUNK == 0
    assert LAYER_TILE >= WINDOW and LAYER_TILE % dec_seq == 0 and (dec_batch * dec_seq) % LAYER_TILE == 0
    assert dec_seq <= GMLP_CHUNK and GMLP_CHUNK % dec_seq == 0 and cache_rows == WINDOW
    T = TOK_TILE
    n_prompt = batch * seq
    n_sample = dec_batch * dec_seq
    n_tok = n_prompt + n_sample
    LT = LAYER_TILE
    sorts_per_step = LT // T
    tiles_per_seq = seq // LT
    n_psteps = n_prompt // LT
    n_ssteps = n_sample // LT
    n_ptiles = n_prompt // T
    n_tiles = n_tok // T
    seqs_per_tile = LT // dec_seq

    w_in_b = w_in[0].astype(BF16)
    w_q_b = w_in_b[:, :D_ATT].reshape(D_MODEL, N_KV, REP, HEAD_DIM).transpose(0, 2, 1, 3).reshape(
        D_MODEL, D_ATT)
    woa_b = w_o_att[0].reshape(N_KV, REP, HEAD_DIM, D_MODEL).transpose(1, 0, 2, 3).reshape(
        D_ATT, D_MODEL).astype(BF16)
    wos_b = w_o_sgu[0].astype(BF16)
    wout_b = w_out[0].astype(BF16)
    wr_b = jnp.transpose(w_router[0]).astype(BF16)
    br_col = b_router[0].astype(F32).reshape(N_EXPERTS, 1)
    g1 = norm1_g[0].reshape(1, D_MODEL)
    g2 = norm2_g[0].reshape(1, D_MODEL)
    gf = final_norm_g.reshape(1, D_MODEL)
    lng = sgu_ln_g[0].reshape(1, D_GMLP)
    lnb = sgu_ln_b[0].reshape(1, D_GMLP)
    tril = jnp.tril(jnp.ones((GMLP_CHUNK, GMLP_CHUNK), dtype=bool))
    sguw_p = jnp.where(tril[None], sgu_w[0], 0).astype(BF16)
    sgub_p = jnp.broadcast_to(sgu_b[0][:, :, None], (N_GROUPS, GMLP_CHUNK, GROUP_W)).astype(F32)
    reps = GMLP_CHUNK // dec_seq
    corner = jnp.where(tril[None, :dec_seq, :dec_seq], sgu_w[0][:, :dec_seq, :dec_seq], 0)
    sguw_s = jnp.einsum('ab,gij->gaibj', jnp.eye(reps, dtype=F32), corner).reshape(
        N_GROUPS, GMLP_CHUNK, GMLP_CHUNK).astype(BF16)
    sgub_s = jnp.broadcast_to(jnp.tile(sgu_b[0][:, :dec_seq], (1, reps))[:, :, None],
                              (N_GROUPS, GMLP_CHUNK, GROUP_W)).astype(F32)
    bias_p = _stacked_bias(rel_bias_table, attn_sinks[0], jnp.arange(CHUNK) + WINDOW, jnp.arange(BAND))
    bias_s = _stacked_bias(rel_bias_table, attn_sinks[0], cache_rows + jnp.arange(dec_seq),
                           jnp.arange(cache_rows + dec_seq))
    tri = jnp.triu(jnp.ones((T, T), F32), k=1).astype(BF16)
    low = jnp.tril(jnp.ones((N_EXPERTS, N_EXPERTS), F32), k=-1).astype(BF16)

    layer_out_shapes = (
        jax.ShapeDtypeStruct((n_tok, D_MODEL), F32),
        jax.ShapeDtypeStruct((n_tok, D_MODEL), BF16),
        jax.ShapeDtypeStruct((n_tiles, ROUTE_ROWS, T), F32),
        jax.ShapeDtypeStruct((n_tok, LANES), F32),
        jax.ShapeDtypeStruct((n_tiles, N_EXPERTS, LANES), F32),
    )
    shared_consts = (w_q_b, w_in_b)
    tail_consts_p = (sguw_p, sgub_p, lng, lnb, woa_b, wos_b, wout_b, g2, wr_b, br_col, tri, low)
    tail_consts_s = (sguw_s, sgub_s, lng, lnb, woa_b, wos_b, wout_b, g2, wr_b, br_col, tri, low)

    def tile_specs(tile_of):
        return [
            pl.BlockSpec((LT, D_MODEL), lambda *g: (tile_of(*g), 0)),
            pl.BlockSpec((LT, D_MODEL), lambda *g: (tile_of(*g), 0)),
            pl.BlockSpec((sorts_per_step, ROUTE_ROWS, T), lambda *g: (tile_of(*g), 0, 0)),
            pl.BlockSpec((LT, LANES), lambda *g: (tile_of(*g), 0)),
            pl.BlockSpec((sorts_per_step, N_EXPERTS, LANES), lambda *g: (tile_of(*g), 0, 0)),
        ]

    prompt_consts = (g1,) + shared_consts + (bias_p,) + tail_consts_p
    x1, h, route, routet, cnt, kvwin = pl.pallas_call(
        _prompt_layer_kernel,
        grid=(batch, tiles_per_seq),
        in_specs=[pl.BlockSpec((1, LT, D_MODEL), lambda b, j: (b, j, 0))]
                 + [_const_spec(c.shape) for c in prompt_consts],
        out_specs=tile_specs(lambda b, j: b * tiles_per_seq + j)
                  + [pl.BlockSpec((1, WINDOW, 2 * D_KV), lambda b, j: (b, 0, 0))],
        out_shape=layer_out_shapes + (jax.ShapeDtypeStruct((batch, WINDOW, 2 * D_KV), F32),),
        scratch_shapes=[pltpu.VMEM((LT + WINDOW, 2 * D_KV), BF16),
                        pltpu.VMEM((LT, D_ATT), BF16),
                        pltpu.VMEM((LT, D_GMLP), BF16)],
        compiler_params=pltpu.CompilerParams(dimension_semantics=("arbitrary", "arbitrary"),
                                             vmem_limit_bytes=VMEM_LIMIT),
        name="layer_prompt",
    )(x_prompt, *prompt_consts)

    xs_flat = x_sample.reshape(n_sample, D_MODEL)
    ck = cache_win_k[0].reshape(dec_batch, cache_rows, D_KV)
    cv = cache_win_v[0].reshape(dec_batch, cache_rows, D_KV)
    sample_consts = (g1,) + shared_consts + (bias_s,) + tail_consts_s
    any_spec = pl.BlockSpec(memory_space=pl.ANY)
    x1, h, route, routet, cnt, kvnew, vn_s = pl.pallas_call(
        _sample_layer_kernel,
        grid=(n_ssteps,),
        in_specs=[any_spec] * 5
                 + [pl.BlockSpec((LT, D_MODEL), lambda i: (i, 0)),
                    pl.BlockSpec((seqs_per_tile, cache_rows, D_KV), lambda i: (i, 0, 0)),
                    pl.BlockSpec((seqs_per_tile, cache_rows, D_KV), lambda i: (i, 0, 0))]
                 + [_const_spec(c.shape) for c in sample_consts],
        out_specs=tile_specs(lambda i: n_psteps + i)
                  + [pl.BlockSpec((LT, 2 * D_KV), lambda i: (i, 0)),
                     pl.BlockSpec((LT, D_GMLP), lambda i: (i, 0))],
        out_shape=layer_out_shapes + (jax.ShapeDtypeStruct((n_sample, 2 * D_KV), F32),
                                      jax.ShapeDtypeStruct((n_sample, D_GMLP), F32)),
        scratch_shapes=[pltpu.VMEM((LT, D_ATT), BF16), pltpu.VMEM((LT, D_GMLP), BF16)],
        input_output_aliases={0: 0, 1: 1, 2: 2, 3: 3, 4: 4},
        compiler_params=pltpu.CompilerParams(dimension_semantics=("arbitrary",),
                                             vmem_limit_bytes=VMEM_LIMIT),
        name="layer_sample",
    )(x1, h, route, routet, cnt, xs_flat, ck, cv, *sample_consts)

    blk_gran = SUB_BLOCK // GRAN
    counts =cnt[:, :, 0].astype(I32)
    seg_gran = (counts + (GRAN - 1)) // GRAN
    local_off = jnp.cumsum(seg_gran, axis=1) - seg_gran
    tot_gran = jnp.sum(seg_gran, axis=0)
    ptot_gran = (tot_gran + (blk_gran - 1)) // blk_gran * blk_gran
    pend_gran = jnp.cumsum(ptot_gran)
    gstart = pend_gran - ptot_gran
    global_off = gstart[None, :] + jnp.cumsum(seg_gran, axis=0) - seg_gran
    pad_n = ptot_gran - tot_gran
    pad_off = gstart + tot_gran
    n_rows = -(-(TOP_K * n_tok + n_tiles * N_EXPERTS * (GRAN - 1) + N_EXPERTS * (SUB_BLOCK - GRAN))
               // SUB_BLOCK) * SUB_BLOCK + ROW_BLOCK
    region_gran0 = gstart.astype(I32)
    region_subs = (ptot_gran // blk_gran).astype(I32)
    tile_gran = jnp.sum(seg_gran, axis=1).astype(I32)
    tile_big = jnp.any(seg_gran >= 4 * (FLAT_QUADS + 1), axis=1).astype(I32)
    seg_gran_f = seg_gran.reshape(-1).astype(I32)
    local_off_f = local_off.reshape(-1).astype(I32)
    global_off_f = global_off.reshape(-1).astype(I32)

    xs = pl.pallas_call(
        _dispatch_kernel,
        grid_spec=pltpu.PrefetchScalarGridSpec(
            num_scalar_prefetch=7,
            grid=(n_tiles,),
            in_specs=[pl.BlockSpec((1, ROUTE_ROWS, T), lambda i, *_: (i, 0, 0)),
                      pl.BlockSpec((T, D_MODEL), lambda i, *_: (i, 0))],
            out_specs=pl.BlockSpec(memory_space=pl.ANY),
            scratch_shapes=[pltpu.VMEM((2, LOCAL_ROWS // GRAN, GRAN, SLOT_WORDS), U32),
                            pltpu.VMEM((SUB_BLOCK // GRAN, GRAN, SLOT_WORDS), U32),
                            pltpu.SemaphoreType.DMA((3,))]),
        out_shape=jax.ShapeDtypeStruct((n_rows // GRAN, GRAN, SLOT_WORDS), U32),
        compiler_params=pltpu.CompilerParams(dimension_semantics=("arbitrary",),
                                             vmem_limit_bytes=VMEM_LIMIT),
        name="moe_dispatch",
    )(seg_gran_f, local_off_f, global_off_f, tile_gran, tile_big, pad_n.astype(I32), pad_off.astype(I32),
      route, h)

    ys = pl.pallas_call(
        _expert_kernel,
        grid_spec=pltpu.PrefetchScalarGridSpec(
            num_scalar_prefetch=2,
            grid=(N_EXPERTS,),
            in_specs=[pl.BlockSpec(memory_space=pl.ANY),
                      pl.BlockSpec((1, D_MODEL, 2 * D_FF), lambda e, *_: (e, 0, 0)),
                      pl.BlockSpec((1, 1, 2 * D_FF), lambda e, *_: (e, 0, 0)),
                      pl.BlockSpec((1, D_FF, D_MODEL), lambda e, *_: (e, 0, 0)),
                      pl.BlockSpec((1, 1, D_MODEL), lambda e, *_: (e, 0, 0))],
            out_specs=pl.BlockSpec(memory_space=pl.ANY),
            scratch_shapes=[pltpu.VMEM((D_MODEL, 2 * D_FF), BF16), pltpu.VMEM((D_FF, D_MODEL), BF16),
                            pltpu.VMEM((2, ROW_BLOCK // GRAN, GRAN, SLOT_WORDS), U32),
                            pltpu.VMEM((2, ROW_BLOCK // GRAN, GRAN, SLOT_WORDS), U32),
                            pltpu.SemaphoreType.DMA((2,)), pltpu.SemaphoreType.DMA((2,))]),
        out_shape=jax.ShapeDtypeStruct((n_rows // GRAN, GRAN, SLOT_WORDS), U32),
        compiler_params=pltpu.CompilerParams(dimension_semantics=("arbitrary",),
                                             vmem_limit_bytes=VMEM_LIMIT),
        name="moe_experts",
    )(region_gran0, region_subs, xs, w_exp_in[0], b_exp_in[0].reshape(N_EXPERTS, 1, 2 * D_FF),
      w_exp_out[0], b_exp_out[0].reshape(N_EXPERTS, 1, D_MODEL))

    y_p, y_s = pl.pallas_call(
        functools.partial(_combine_kernel, n_ptiles),
        grid_spec=pltpu.PrefetchScalarGridSpec(
            num_scalar_prefetch=5,
            grid=(n_tiles,),
            in_specs=[pl.BlockSpec((T, LANES), lambda i, *_: (i, 0)),
                      pl.BlockSpec((T, D_MODEL), lambda i, *_: (i, 0)),
                      pl.BlockSpec((1, D_MODEL), lambda i, *_: (0, 0)),
                      pl.BlockSpec(memory_space=pl.ANY)],
            out_specs=[pl.BlockSpec((T, D_MODEL), lambda i, *_: (jnp.minimum(i, n_ptiles - 1), 0)),
                       pl.BlockSpec((T, D_MODEL), lambda i, *_: (jnp.maximum(i - n_ptiles, 0), 0))],
            scratch_shapes=[pltpu.VMEM((2, LOCAL_ROWS // GRAN, GRAN, SLOT_WORDS), U32), pltpu.SemaphoreType.DMA((2,))]),
        out_shape=(jax.ShapeDtypeStruct((n_prompt, D_MODEL), F32),
                   jax.ShapeDtypeStruct((n_sample, D_MODEL), F32)),
        compiler_params=pltpu.CompilerParams(dimension_semantics=("arbitrary",),
                                             vmem_limit_bytes=VMEM_LIMIT),
        name="moe_combine",
    )(seg_gran_f, local_off_f, global_off_f, tile_gran, tile_big, routet, x1, gf, ys)

    y_prompt = y_p.reshape(batch, seq, D_MODEL)
    y_sample = y_s.reshape(dec_batch, dec_seq, D_MODEL)
    new_win_k_prompt = kvwin[:, :, :D_KV].reshape(1, batch, WINDOW, N_KV, HEAD_DIM)
    new_win_v_prompt = kvwin[:, :, D_KV:].reshape(1, batch, WINDOW, N_KV, HEAD_DIM)
    new_win_k_sample = kvnew[:, :D_KV].reshape(1, dec_batch, dec_seq, N_KV, HEAD_DIM)
    new_win_v_sample = kvnew[:, D_KV:].reshape(1, dec_batch, dec_seq, N_KV, HEAD_DIM)
    new_sgu_v_sample = vn_s.reshape(1, dec_batch, dec_seq, D_GMLP)
    return (y_prompt, y_sample, new_win_k_prompt, new_win_v_prompt, new_win_k_sample,
            new_win_v_sample, new_sgu_v_sample)
```

```python
import functools
import math

import numpy as np
import jax
import jax.numpy as jnp
from jax import lax
from jax.experimental import pallas as pl
from jax.experimental.pallas import tpu as pltpu

F32 = jnp.float32
BF16 = jnp.bfloat16
I32 = jnp.int32
U32 = jnp.uint32

D_MODEL = 1024
HEAD_DIM = 64
N_HEADS = 16
N_KV = 2
REP = N_HEADS // N_KV
CHUNK = 64
WINDOW = 128
BAND = WINDOW + CHUNK
KEY_PAD = 256
D_ATT = N_HEADS * HEAD_DIM
D_KV = N_KV * HEAD_DIM
NUM_BUCKETS = 32
MAX_DISTANCE = 128
GMLP_CHUNK = 128
D_GMLP = 1024
N_GROUPS = 4
GROUP_W = D_GMLP // N_GROUPS
N_EXPERTS = 32
TOP_K = 4
D_FF = 1024
SWIGLU_LIMIT = 7.0
SWIGLU_ALPHA = 1.702
NORM_EPS = 1e-5
D_IN = D_ATT + 2 * D_KV + 2 * D_GMLP + 2 * D_MODEL
COL_KV = D_ATT
COL_U = COL_KV + 2 * D_KV
COL_VG = COL_U + D_GMLP
COL_GA = COL_VG + D_GMLP
COL_GB = COL_GA + D_MODEL
SQRT_HALF = float(np.sqrt(0.5))

LANES = 128
WORD_SUBLANES = 8
VMEM_LIMIT = 56 * 1024 * 1024

TOK_TILE = 256
LAYER_TILE = 512
GRAN = WORD_SUBLANES
SLOT_WORDS = D_MODEL // 2
FLAT_QUADS = 1
LOCAL_ROWS = TOP_K * TOK_TILE + N_EXPERTS * GRAN
ROW_BLOCK = 512
SUB_BLOCK = 64
PAD_BITS = (SUB_BLOCK // GRAN - 1).bit_length()
TILE_BITS = (LOCAL_ROWS // GRAN).bit_length()
PROJ_PIECES = 16
ROUTE_ROWS = 16


def _const_spec(shape):
    nd = len(shape)
    return pl.BlockSpec(shape, lambda *_: (0,) * nd, pipeline_mode=pl.Buffered(1))


def _rms_norm(x, g):
    ms = jnp.mean(x * x, axis=-1, keepdims=True)
    return x * lax.rsqrt(ms + NORM_EPS) * g


def _gelu(x):
    return 0.5 * x * (1.0 + lax.erf(x * SQRT_HALF))


def _attend(q, k, v, bias, valid):
    logits = lax.dot_general(q, k, (((1,), (1,)), ((), ())), preferred_element_type=F32) + bias
    if valid is not None:
        logits = jnp.where(valid, logits, -jnp.inf)
    m = jnp.max(logits, axis=-1, keepdims=True)
    p = jnp.exp(logits - m)
    den = jnp.sum(p, axis=-1, keepdims=True)
    w = (p * (1.0 / den)).astype(BF16)
    return jnp.dot(w, v, preferred_element_type=F32)


def _attention_rows(zq, row0, n_rows, k_ext, v_ext, bias_ref, valid, oatt_ref):
    lane = lax.broadcasted_iota(I32, (1, LANES), 1)
    stack = []
    for g in range(N_KV):
        in_group = (lane >= g * HEAD_DIM) & (lane < (g + 1) * HEAD_DIM)
        qmask = jnp.where(in_group, HEAD_DIM ** -0.5, 0.0).astype(BF16)
        stack += [zq[row0:row0 + n_rows, t * LANES:(t + 1) * LANES] * qmask for t in range(REP)]
    bias = bias_ref[...].reshape(N_HEADS * n_rows, KEY_PAD)
    o = _attend(jnp.concatenate(stack, axis=0), k_ext, v_ext, bias, valid)
    for g in range(N_KV):
        for t in range(REP):
            r = (g * REP + t) * n_rows
            c0 = t * LANES + g * HEAD_DIM
            oatt_ref[row0:row0 + n_rows, c0:c0 + HEAD_DIM] = (
                o[r:r + n_rows, g * HEAD_DIM:(g + 1) * HEAD_DIM].astype(BF16))


def _proj_piece(xn_b, w_in_ref, k):
    width = (D_IN - COL_U) // PROJ_PIECES
    c0 = COL_U + k * width
    z = jnp.dot(xn_b, w_in_ref[:, c0:c0 + width], preferred_element_type=F32)
    return _gelu(z) if c0 < COL_GA else jax.nn.sigmoid(z)


def _sgu_branch(pieces, osgu_ref, sguw_ref, sgub_ref, lng_ref, lnb_ref, vn_ref):
    u = jnp.concatenate(pieces[:PROJ_PIECES // 4], axis=1)
    vg = jnp.concatenate(pieces[PROJ_PIECES // 4:PROJ_PIECES // 2], axis=1)
    T = u.shape[0]
    mu = jnp.mean(vg, axis=-1, keepdims=True)
    var = jnp.mean(jnp.square(vg - mu), axis=-1, keepdims=True)
    vn = (vg - mu) * lax.rsqrt(var + NORM_EPS) * lng_ref[...] + lnb_ref[...]
    if vn_ref is not None:
        vn_ref[...] = vn
    vn_b = vn.astype(BF16)
    for j in range(T // GMLP_CHUNK):
        rows = slice(j * GMLP_CHUNK, (j + 1) * GMLP_CHUNK)
        for g in range(N_GROUPS):
            cols = slice(g * GROUP_W, (g + 1) * GROUP_W)
            mixed = jnp.dot(sguw_ref[g], vn_b[rows, cols], preferred_element_type=F32) + sgub_ref[g]
            osgu_ref[rows, cols] = (u[rows, cols] * mixed).astype(BF16)


def _layer_tail(x, pieces, oatt_ref, osgu_ref, woa_ref, wos_ref, wout_ref, g2_ref, wr_ref, br_ref,
                tri_ref, low_ref, x1_ref, h_ref, route_ref, routet_ref, cnt_ref):
    T = x.shape[0]
    ga = jnp.concatenate(pieces[PROJ_PIECES // 2:3 * PROJ_PIECES // 4], axis=1)
    gb = jnp.concatenate(pieces[3 * PROJ_PIECES // 4:], axis=1)
    merged = ga * jnp.dot(oatt_ref[...], woa_ref[...], preferred_element_type=F32)
    merged = merged + gb * jnp.dot(osgu_ref[...], wos_ref[...], preferred_element_type=F32)
    x1 = x + jnp.dot(merged.astype(BF16), wout_ref[...], preferred_element_type=F32)
    x1_ref[...] = x1

    h_b = _rms_norm(x1, g2_ref[...]).astype(BF16)
    h_ref[...] = h_b
    lt = lax.dot_general(wr_ref[...], h_b, (((1,), (1,)), ((), ())),
                         preferred_element_type=F32) + br_ref[...]
    for s in range(T // TOK_TILE):
        rec, cnt_b = _route_sort_tile(lt[:, s * TOK_TILE:(s + 1) * TOK_TILE], tri_ref, low_ref)
        route_ref[s] = rec
        rec_pad = jnp.concatenate([rec, jnp.zeros((LANES - ROUTE_ROWS, TOK_TILE), F32)], axis=0)
        routet_ref[s * TOK_TILE:(s + 1) * TOK_TILE, :] = jnp.transpose(rec_pad)
        cnt_ref[s] = cnt_b


def _route_sort_tile(lt, tri_ref, low_ref):
    T = lt.shape[1]
    e_iota = lax.broadcasted_iota(I32, (N_EXPERTS, T), 0).astype(F32)
    cur = lt
    vals, idxs = [], []
    for _ in range(TOP_K):
        m = jnp.max(cur, axis=0, keepdims=True)
        ik = jnp.min(jnp.where(cur == m, e_iota, float(N_EXPERTS)), axis=0, keepdims=True)
        vals.append(m)
        idxs.append(ik)
        cur = jnp.where(e_iota == ik, -jnp.inf, cur)
    exps = [jnp.exp(v - vals[0]) for v in vals]
    den = exps[0] + exps[1] + exps[2] + exps[3]
    gates = [e / den for e in exps]

    onehot = jnp.zeros((N_EXPERTS, T), F32)
    for ik in idxs:
        onehot = onehot + jnp.where(e_iota == ik, 1.0, 0.0)
    rank = jnp.dot(onehot.astype(BF16), tri_ref[...], preferred_element_type=F32)
    cnt = jnp.sum(onehot, axis=1, keepdims=True)
    cnt_b = jnp.broadcast_to(cnt, (N_EXPERTS, LANES))
    gran = jnp.floor((cnt_b + (GRAN - 1)) * (1.0 / GRAN))
    off = jnp.dot(low_ref[...], gran.astype(BF16), preferred_element_type=F32) * GRAN
    base = off[:, 0:1] + rank
    poss = [jnp.sum(jnp.where(e_iota == ik, base, 0.0), axis=0, keepdims=True) for ik in idxs]

    rec = jnp.concatenate(idxs + poss + gates
                          + [jnp.zeros((ROUTE_ROWS - 3 * TOP_K, T), F32)], axis=0)
    return rec, cnt_b


def _prompt_layer_kernel(x_ref, g1_ref, wq_ref, w_in_ref, bias_ref, sguw_ref, sgub_ref, lng_ref,
                         lnb_ref, woa_ref, wos_ref, wout_ref, g2_ref, wr_ref, br_ref, tri_ref, low_ref,
                         x1_ref, h_ref, route_ref, routet_ref, cnt_ref, kvwin_ref,
                         kvx_ref, oatt_ref, osgu_ref):
    T = LAYER_TILE
    j = pl.program_id(1)
    x = x_ref[0]
    xn_b = _rms_norm(x, g1_ref[...]).astype(BF16)
    zq = jnp.dot(xn_b, wq_ref[...], preferred_element_type=F32).astype(BF16)
    zkv = jnp.dot(xn_b, w_in_ref[:, COL_KV:COL_U], preferred_element_type=F32)

    @pl.when(j == pl.num_programs(1) - 1)
    def _():
        kvwin_ref[0] = zkv[T - WINDOW:, :]

    @pl.when(j == 0)
    def _():
        kvx_ref[0:WINDOW, :] = jnp.zeros((WINDOW, 2 * D_KV), BF16)

    @pl.when(j > 0)
    def _():
        kvx_ref[0:WINDOW, :] = kvx_ref[T:T + WINDOW, :]

    kvx_ref[WINDOW:, :] = zkv.astype(BF16)

    col = lax.broadcasted_iota(I32, (1, KEY_PAD), 1)
    key_pad = jnp.zeros((KEY_PAD - BAND, 2 * D_KV), BF16)
    n_chunks = T // CHUNK
    pieces = []
    for c in range(n_chunks):
        r0 = c * CHUNK
        kvb = jnp.concatenate([kvx_ref[r0:r0 + BAND, :], key_pad], axis=0)
        valid = ((col + (j * T + r0 - WINDOW)) >= 0) | (col >= BAND) if r0 < WINDOW else None
        _attention_rows(zq, r0, CHUNK, kvb[:, :D_KV], kvb[:, D_KV:], bias_ref, valid, oatt_ref)
        pieces += [_proj_piece(xn_b, w_in_ref, k) for k in range(len(pieces), (c + 1) * PROJ_PIECES // n_chunks)]

    _sgu_branch(pieces, osgu_ref, sguw_ref, sgub_ref, lng_ref, lnb_ref, None)
    _layer_tail(x, pieces, oatt_ref, osgu_ref, woa_ref, wos_ref, wout_ref, g2_ref, wr_ref, br_ref,
                tri_ref, low_ref, x1_ref, h_ref, route_ref, routet_ref, cnt_ref)


def _sample_layer_kernel(x1_in, h_in, route_in, routet_in, cnt_in,
                         x_ref, ck_ref, cv_ref, g1_ref, wq_ref, w_in_ref, bias_ref, sguw_ref, sgub_ref,
                         lng_ref, lnb_ref, woa_ref, wos_ref, wout_ref, g2_ref, wr_ref, br_ref, tri_ref,
                         low_ref,
                         x1_ref, h_ref, route_ref, routet_ref, cnt_ref, kvnew_ref, vn_ref,
                         oatt_ref, osgu_ref):
    del x1_in, h_in, route_in, routet_in, cnt_in
    T = LAYER_TILE
    nq = x_ref.shape[0] // ck_ref.shape[0]
    x = x_ref[...]
    xn_b = _rms_norm(x, g1_ref[...]).astype(BF16)
    zq = jnp.dot(xn_b, wq_ref[...], preferred_element_type=F32).astype(BF16)
    zkv = jnp.dot(xn_b, w_in_ref[:, COL_KV:COL_U], preferred_element_type=F32)
    kvnew_ref[...] = zkv
    zkv_b = zkv.astype(BF16)
    n_cache = ck_ref.shape[1]
    key_pad = jnp.zeros((KEY_PAD - n_cache - nq, D_KV), BF16)
    n_seqs = T // nq
    pieces = []
    for b in range(n_seqs):
        r0 = b * nq
        kk = jnp.concatenate([ck_ref[b].astype(BF16), zkv_b[r0:r0 + nq, :D_KV], key_pad], axis=0)
        vv = jnp.concatenate([cv_ref[b].astype(BF16), zkv_b[r0:r0 + nq, D_KV:], key_pad], axis=0)
        _attention_rows(zq, r0, nq, kk, vv, bias_ref, None, oatt_ref)
        pieces += [_proj_piece(xn_b, w_in_ref, k) for k in range(len(pieces), (b + 1) * PROJ_PIECES // n_seqs)]

    _sgu_branch(pieces, osgu_ref, sguw_ref, sgub_ref, lng_ref, lnb_ref, vn_ref)
    _layer_tail(x, pieces, oatt_ref, osgu_ref, woa_ref, wos_ref, wout_ref, g2_ref, wr_ref, br_ref,
                tri_ref, low_ref, x1_ref, h_ref, route_ref, routet_ref, cnt_ref)


def _pack_slot_rows(x):
    lo = lax.bitcast_convert_type(x[:, :SLOT_WORDS].astype(BF16).astype(F32), U32)
    hi = lax.bitcast_convert_type(x[:, SLOT_WORDS:].astype(BF16).astype(F32), U32)
    return (lo >> 16) | hi


def _pack_slot_rows_exact(x):
    lo = lax.bitcast_convert_type(x[:, :SLOT_WORDS], U32)
    hi = lax.bitcast_convert_type(x[:, SLOT_WORDS:], U32)
    return (lo >> 16) | hi


def _unpack_slot_rows(w):
    lo = lax.bitcast_convert_type(w << 16, F32)
    hi = lax.bitcast_convert_type(w & jnp.uint32(0xFFFF0000), F32)
    return jnp.concatenate([lo, hi], axis=1).astype(BF16)


def _as_granules(x):
    return x.reshape(x.shape[0] // GRAN, GRAN, SLOT_WORDS)


def _as_rows(x):
    return x.reshape(x.shape[0] * GRAN, SLOT_WORDS)


def _segment_copies(n_gran, src_gran, dst_gran, bits, make_copy, act):
    for b in range(min(bits, 2)):
        @pl.when(((n_gran >> b) & 1) == 1)
        def _(b=b):
            done = n_gran & ((1 << b) - 1)
            act(make_copy(src_gran + done, dst_gran + done, 1 << b))

    if bits > 2:
        def quad(q, carry):
            done = (n_gran & 3) + 4 * q
            act(make_copy(src_gran + done, dst_gran + done, 4))
            return carry
        lax.fori_loop(0, n_gran >> 2, quad, 0)


def _start_tile_copies(tile, live, cnt_ref, loff_ref, goff_ref, big_ref, make_copy):
    for e in range(N_EXPERTS):
        t = tile * N_EXPERTS + e
        n = jnp.where(live, cnt_ref[t], 0)
        src, dst = loff_ref[t], goff_ref[t]
        for b in range(2):
            @pl.when(((n >> b) & 1) == 1)
            def _(b=b, n=n, src=src, dst=dst):
                done = n & ((1 << b) - 1)
                make_copy(src + done, dst + done, 1 << b).start()
        for q in range(FLAT_QUADS):
            @pl.when((n >> 2) > q)
            def _(q=q, n=n, src=src, dst=dst):
                done = (n & 3) + 4 * q
                make_copy(src + done, dst + done, 4).start()

    @pl.when(live & (big_ref[tile] > 0))
    def _():
        def rest(e, carry):
            t = tile * N_EXPERTS + e
            n, src, dst = cnt_ref[t], loff_ref[t], goff_ref[t]

            def quad(q, c):
                done = (n & 3) + 4 * q
                make_copy(src + done, dst + done, 4).start()
                return c
            lax.fori_loop(FLAT_QUADS, n >> 2, quad, 0)
            return carry
        lax.fori_loop(0, N_EXPERTS, rest, 0)


def _wait_granules(n_gran, bits, make_copy):
    for b in range(bits):
        @pl.when(((n_gran >> b) & 1) == 1)
        def _(b=b):
            make_copy(1 << b).wait()


def _dispatch_kernel(cnt_ref, loff_ref, goff_ref, tot_ref, big_ref, padn_ref, padoff_ref,
                     route_ref, h_ref, xs_hbm, stage_ref, zero_ref, sem):
    i = pl.program_id(0)
    last = pl.num_programs(0) - 1
    slot = lax.rem(i, 2)
    T = TOK_TILE

    def seg_copy(tile_slot):
        return lambda s, d, n: pltpu.make_async_copy(
            stage_ref.at[tile_slot, pl.ds(s, n)], xs_hbm.at[pl.ds(d, n)], sem.at[tile_slot])

    def wait_tile(tile, tile_slot):
        _wait_granules(tot_ref[tile], TILE_BITS, lambda n: pltpu.make_async_copy(
            stage_ref.at[tile_slot, pl.ds(0, n)], xs_hbm.at[pl.ds(0, n)], sem.at[tile_slot]))

    @pl.when(i > 1)
    def _():
        wait_tile(i - 2, slot)

    _start_tile_copies(jnp.maximum(i - 1, 0), i > 0, cnt_ref, loff_ref, goff_ref, big_ref, seg_copy(1 - slot))

    pos = route_ref[0, TOP_K:2 * TOP_K, :].astype(I32)
    r_iota = lax.broadcasted_iota(I32, (LOCAL_ROWS, T), 0)
    p = jnp.zeros((LOCAL_ROWS, T), F32)
    for k in range(TOP_K):
        p = jnp.where(r_iota == pos[k:k + 1, :], 1.0, p)
    stage_ref[slot] = _as_granules(
        _pack_slot_rows_exact(jnp.dot(p.astype(BF16), h_ref[...], preferred_element_type=F32)))

    @pl.when(i == last)
    def _():
        _start_tile_copies(i, i >= 0, cnt_ref, loff_ref, goff_ref, big_ref, seg_copy(slot))

        @pl.when(i > 0)
        def _():
            wait_tile(i - 1, 1 - slot)

        wait_tile(i, slot)
        zero_ref[...] = jnp.zeros(zero_ref.shape, U32)

        def pad_copy(s, d, n):
            return pltpu.make_async_copy(zero_ref.at[pl.ds(s, n)], xs_hbm.at[pl.ds(d, n)], sem.at[2])

        def for_pads(act):
            def body(e, carry):
                _segment_copies(padn_ref[e], 0, padoff_ref[e], PAD_BITS, pad_copy, act)
                return carry
            lax.fori_loop(0, N_EXPERTS, body, 0)

        for_pads(lambda cp: cp.start())
        for_pads(lambda cp: cp.wait())


def _expert_kernel(gran0_ref, nsub_ref, xs_hbm, w1_ref, b1_ref, w2_ref, b2_ref, ys_hbm,
                   w1b_ref, w2b_ref, xbuf_ref, ybuf_ref, sem_in, sem_out):
    e = pl.program_id(0)
    subs = ROW_BLOCK // SUB_BLOCK
    sub_gran = SUB_BLOCK // GRAN
    blk_gran = ROW_BLOCK // GRAN
    gran0 = gran0_ref[e]
    nsub = nsub_ref[e]
    nb_full = nsub // subs
    rem = nsub - nb_full * subs
    nb = nb_full + jnp.where(rem > 0, 1, 0)
    last_slot = lax.rem(nb + 1, 2)

    def in_copy(k, slot):
        return pltpu.make_async_copy(xs_hbm.at[pl.ds(gran0 + k * blk_gran, blk_gran)], xbuf_ref.at[slot],
                                     sem_in.at[slot])

    def out_copy(k, slot, m_sub):
        n = m_sub * sub_gran
        return pltpu.make_async_copy(ybuf_ref.at[slot, pl.ds(0, n)], ys_hbm.at[pl.ds(gran0 + k * blk_gran, n)],
                                     sem_out.at[slot])

    @pl.when(nb > 0)
    def _():
        in_copy(0, 0).start()

    w1b_ref[...] = w1_ref[0].astype(BF16)
    w2b_ref[...] = w2_ref[0].astype(BF16)

    def process(k, m_sub):
        n = m_sub * sub_gran
        slot = lax.rem(k, 2)

        @pl.when(k + 1 < nb)
        def _():
            in_copy(k + 1, 1 - slot).start()

        in_copy(k, slot).wait()

        @pl.when(k >= 2)
        def _():
            out_copy(k - 2, slot, subs).wait()

        x = _unpack_slot_rows(_as_rows(xbuf_ref[slot, 0:n]))
        h1 = jnp.dot(x, w1b_ref[...], preferred_element_type=F32) + b1_ref[0]
        gate = jnp.minimum(h1[:, :D_FF], SWIGLU_LIMIT)
        up = jnp.clip(h1[:, D_FF:], -SWIGLU_LIMIT, SWIGLU_LIMIT)
        act = gate * jax.nn.sigmoid(SWIGLU_ALPHA * gate) * (up + 1.0)
        y = jnp.dot(act.astype(BF16), w2b_ref[...], preferred_element_type=F32) + b2_ref[0]
        ybuf_ref[slot, 0:n] = _as_granules(_pack_slot_rows(y))
        out_copy(k, slot, m_sub).start()

    def full_block(k, carry):
        process(k, subs)
        return carry

    lax.fori_loop(0, nb_full, full_block, 0)

    for m_sub in range(1, subs):
        @pl.when(rem == m_sub)
        def _(m_sub=m_sub):
            process(nb_full, m_sub)

    @pl.when(nb >= 2)
    def _():
        out_copy(nb - 2, 1 - last_slot, subs).wait()

    @pl.when((nb >= 1) & (rem == 0))
    def _():
        out_copy(nb - 1, last_slot, subs).wait()

    for m_sub in range(1, subs):
        @pl.when(rem == m_sub)
        def _(m_sub=m_sub):
            out_copy(nb - 1, last_slot, m_sub).wait()


def _combine_kernel(n_prompt_tiles, cnt_ref, loff_ref, goff_ref, tot_ref, big_ref,
                    routet_ref, x1_ref, gf_ref, ys_hbm, yp_ref, ysm_ref, stage_ref, sem):
    i = pl.program_id(0)
    n_steps = pl.num_programs(0)
    slot = lax.rem(i, 2)
    T = TOK_TILE

    def seg_copy(tile_slot):
        return lambda s, d, n: pltpu.make_async_copy(
            ys_hbm.at[pl.ds(d, n)], stage_ref.at[tile_slot, pl.ds(s, n)], sem.at[tile_slot])

    @pl.when(i == 0)
    def _():
        stage_ref[...] = jnp.zeros(stage_ref.shape, U32)
        _start_tile_copies(i, i >= 0, cnt_ref, loff_ref, goff_ref, big_ref, seg_copy(slot))

    _start_tile_copies(jnp.minimum(i + 1, n_steps - 1), i + 1 < n_steps, cnt_ref, loff_ref, goff_ref, big_ref,
                       seg_copy(1 - slot))

    rt = routet_ref[...]
    l_iota = lax.broadcasted_iota(I32, (T, LOCAL_ROWS), 1)
    pw = jnp.zeros((T, LOCAL_ROWS), F32)
    for k in range(TOP_K):
        pos_k = rt[:, TOP_K + k:TOP_K + k + 1].astype(I32)
        pw = jnp.where(l_iota == pos_k, rt[:, 2 * TOP_K + k:2 * TOP_K + k + 1], pw)

    _wait_granules(tot_ref[i], TILE_BITS, lambda n: pltpu.make_async_copy(
        ys_hbm.at[pl.ds(0, n)], stage_ref.at[slot, pl.ds(0, n)], sem.at[slot]))
    moe = jnp.dot(pw.astype(BF16), _unpack_slot_rows(_as_rows(stage_ref[slot])), preferred_element_type=F32)
    y = _rms_norm(x1_ref[...] + moe, gf_ref[...])

    @pl.when(i < n_prompt_tiles)
    def _():
        yp_ref[...] = y

    @pl.when(i >= n_prompt_tiles)
    def _():
        ysm_ref[...] = y


def _t5_bucket(rel):
    half = NUM_BUCKETS // 2
    max_exact = half // 2
    ret = jnp.where(rel > 0, half, 0)
    n = jnp.abs(rel)
    nf = jnp.maximum(n, 1).astype(F32)
    large = max_exact + (jnp.log(nf / max_exact) / math.log(MAX_DISTANCE / max_exact)
                         * (half - max_exact)).astype(jnp.int32)
    large = jnp.minimum(large, half - 1)
    return ret + jnp.where(n < max_exact, n, large)


def _stacked_bias(table, sinks, q_pos, k_pos):
    nq, nk = q_pos.shape[0], k_pos.shape[0]
    bucket = _t5_bucket(k_pos[None, :] - q_pos[:, None])
    onehot = (bucket[:, :, None] == jnp.arange(NUM_BUCKETS)).astype(F32)
    bias = jnp.einsum('qkb,bh->hqk', onehot, table.astype(F32), precision=lax.Precision.HIGHEST)
    bias = bias.reshape(N_KV, REP * nq, nk)
    sink = jnp.repeat(sinks.astype(F32).reshape(N_KV, REP, 1), nq, axis=2).reshape(N_KV, REP * nq, 1)
    tail = jnp.full((N_KV, REP * nq, KEY_PAD - nk - 1), -jnp.inf, F32)
    return jnp.concatenate([bias, sink, tail], axis=-1)


def kernel(x_prompt, x_sample, cache_win_k, cache_win_v, norm1_g, w_in, attn_sinks, rel_bias_table, sgu_ln_g, sgu_ln_b, sgu_w, sgu_b, w_o_att, w_o_sgu, w_out, norm2_g, w_router, b_router, w_exp_in, b_exp_in, w_exp_out, b_exp_out, final_norm_g):
    batch, seq, _ = x_prompt.shape
    dec_batch, dec_seq, _ = x_sample.shape
    cache_rows = cache_win_k.shape[2]
    assert x_prompt.shape[2] == D_MODEL and w_in.shape == (1, D_MODEL, D_IN)
    assert seq % LAYER_TILE == 0 and LAYER_TILE % TOK_TILE == 0 and TOK_TILE % GMLP_CHUNK == 0
    assert LAYER_TILE >= WINDOW and LAYER_TILE % dec_seq == 0 and (dec_batch * dec_seq) % LAYER_TILE == 0
    assert dec_seq <= GMLP_CHUNK and GMLP_CHUNK % dec_seq == 0 and cache_rows == WINDOW
    T = TOK_TILE
    n_prompt = batch * seq
    n_sample = dec_batch * dec_seq
    n_tok = n_prompt + n_sample
    LT = LAYER_TILE
    sorts_per_step = LT // T
    tiles_per_seq = seq // LT
    n_psteps = n_prompt // LT
    n_ssteps = n_sample // LT
    n_ptiles = n_prompt // T
    n_tiles = n_tok // T
    seqs_per_tile = LT // dec_seq

    w_in_b = w_in[0].astype(BF16)
    w_q_b = w_in_b[:, :D_ATT].reshape(D_MODEL, N_KV, REP, HEAD_DIM).transpose(0, 2, 1, 3).reshape(
        D_MODEL, D_ATT)
    woa_b = w_o_att[0].reshape(N_KV, REP, HEAD_DIM, D_MODEL).transpose(1, 0, 2, 3).reshape(
        D_ATT, D_MODEL).astype(BF16)
    wos_b = w_o_sgu[0].astype(BF16)
    wout_b = w_out[0].astype(BF16)
    wr_b = jnp.transpose(w_router[0]).astype(BF16)
    br_col = b_router[0].astype(F32).reshape(N_EXPERTS, 1)
    g1 = norm1_g[0].reshape(1, D_MODEL)
    g2 = norm2_g[0].reshape(1, D_MODEL)
    gf = final_norm_g.reshape(1, D_MODEL)
    lng = sgu_ln_g[0].reshape(1, D_GMLP)
    lnb = sgu_ln_b[0].reshape(1, D_GMLP)
    tril = jnp.tril(jnp.ones((GMLP_CHUNK, GMLP_CHUNK), dtype=bool))
    sguw_p = jnp.where(tril[None], sgu_w[0], 0).astype(BF16)
    sgub_p = jnp.broadcast_to(sgu_b[0][:, :, None], (N_GROUPS, GMLP_CHUNK, GROUP_W)).astype(F32)
    reps = GMLP_CHUNK // dec_seq
    corner = jnp.where(tril[None, :dec_seq, :dec_seq], sgu_w[0][:, :dec_seq, :dec_seq], 0)
    sguw_s = jnp.einsum('ab,gij->gaibj', jnp.eye(reps, dtype=F32), corner).reshape(
        N_GROUPS, GMLP_CHUNK, GMLP_CHUNK).astype(BF16)
    sgub_s = jnp.broadcast_to(jnp.tile(sgu_b[0][:, :dec_seq], (1, reps))[:, :, None],
                              (N_GROUPS, GMLP_CHUNK, GROUP_W)).astype(F32)
    bias_p = _stacked_bias(rel_bias_table, attn_sinks[0], jnp.arange(CHUNK) + WINDOW, jnp.arange(BAND))
    bias_s = _stacked_bias(rel_bias_table, attn_sinks[0], cache_rows + jnp.arange(dec_seq),
                           jnp.arange(cache_rows + dec_seq))
    tri = jnp.triu(jnp.ones((T, T), F32), k=1).astype(BF16)
    low = jnp.tril(jnp.ones((N_EXPERTS, N_EXPERTS), F32), k=-1).astype(BF16)

    layer_out_shapes = (
        jax.ShapeDtypeStruct((n_tok, D_MODEL), F32),
        jax.ShapeDtypeStruct((n_tok, D_MODEL), BF16),
        jax.ShapeDtypeStruct((n_tiles, ROUTE_ROWS, T), F32),
        jax.ShapeDtypeStruct((n_tok, LANES), F32),
        jax.ShapeDtypeStruct((n_tiles, N_EXPERTS, LANES), F32),
    )
    shared_consts = (w_q_b, w_in_b)
    tail_consts_p = (sguw_p, sgub_p, lng, lnb, woa_b, wos_b, wout_b, g2, wr_b, br_col, tri, low)
    tail_consts_s = (sguw_s, sgub_s, lng, lnb, woa_b, wos_b, wout_b, g2, wr_b, br_col, tri, low)

    def tile_specs(tile_of):
        return [
            pl.BlockSpec((LT, D_MODEL), lambda *g: (tile_of(*g), 0)),
            pl.BlockSpec((LT, D_MODEL), lambda *g: (tile_of(*g), 0)),
            pl.BlockSpec((sorts_per_step, ROUTE_ROWS, T), lambda *g: (tile_of(*g), 0, 0)),
            pl.BlockSpec((LT, LANES), lambda *g: (tile_of(*g), 0)),
            pl.BlockSpec((sorts_per_step, N_EXPERTS, LANES), lambda *g: (tile_of(*g), 0, 0)),
        ]

    prompt_consts = (g1,) + shared_consts + (bias_p,) + tail_consts_p
    x1, h, route, routet, cnt, kvwin = pl.pallas_call(
        _prompt_layer_kernel,
        grid=(batch, tiles_per_seq),
        in_specs=[pl.BlockSpec((1, LT, D_MODEL), lambda b, j: (b, j, 0))]
                 + [_const_spec(c.shape) for c in prompt_consts],
        out_specs=tile_specs(lambda b, j: b * tiles_per_seq + j)
                  + [pl.BlockSpec((1, WINDOW, 2 * D_KV), lambda b, j: (b, 0, 0))],
        out_shape=layer_out_shapes + (jax.ShapeDtypeStruct((batch, WINDOW, 2 * D_KV), F32),),
        scratch_shapes=[pltpu.VMEM((LT + WINDOW, 2 * D_KV), BF16),
                        pltpu.VMEM((LT, D_ATT), BF16),
                        pltpu.VMEM((LT, D_GMLP), BF16)],
        compiler_params=pltpu.CompilerParams(dimension_semantics=("arbitrary", "arbitrary"),
                                             vmem_limit_bytes=VMEM_LIMIT),
        name="layer_prompt",
    )(x_prompt, *prompt_consts)

    xs_flat = x_sample.reshape(n_sample, D_MODEL)
    ck = cache_win_k[0].reshape(dec_batch, cache_rows, D_KV)
    cv = cache_win_v[0].reshape(dec_batch, cache_rows, D_KV)
    sample_consts = (g1,) + shared_consts + (bias_s,) + tail_consts_s
    any_spec = pl.BlockSpec(memory_space=pl.ANY)
    x1, h, route, routet, cnt, kvnew, vn_s = pl.pallas_call(
        _sample_layer_kernel,
        grid=(n_ssteps,),
        in_specs=[any_spec] * 5
                 + [pl.BlockSpec((LT, D_MODEL), lambda i: (i, 0)),
                    pl.BlockSpec((seqs_per_tile, cache_rows, D_KV), lambda i: (i, 0, 0)),
                    pl.BlockSpec((seqs_per_tile, cache_rows, D_KV), lambda i: (i, 0, 0))]
                 + [_const_spec(c.shape) for c in sample_consts],
        out_specs=tile_specs(lambda i: n_psteps + i)
                  + [pl.BlockSpec((LT, 2 * D_KV), lambda i: (i, 0)),
                     pl.BlockSpec((LT, D_GMLP), lambda i: (i, 0))],
        out_shape=layer_out_shapes + (jax.ShapeDtypeStruct((n_sample, 2 * D_KV), F32),
                                      jax.ShapeDtypeStruct((n_sample, D_GMLP), F32)),
        scratch_shapes=[pltpu.VMEM((LT, D_ATT), BF16), pltpu.VMEM((LT, D_GMLP), BF16)],
        input_output_aliases={0: 0, 1: 1, 2: 2, 3: 3, 4: 4},
        compiler_params=pltpu.CompilerParams(dimension_semantics=("arbitrary",),
                                             vmem_limit_bytes=VMEM_LIMIT),
        name="layer_sample",
    )(x1, h, route, routet, cnt, xs_flat, ck, cv, *sample_consts)

    blk_gran = SUB_BLOCK // GRAN
    counts =cnt[:, :, 0].astype(I32)
    seg_gran = (counts + (GRAN - 1)) // GRAN
    local_off = jnp.cumsum(seg_gran, axis=1) - seg_gran
    tot_gran = jnp.sum(seg_gran, axis=0)
    ptot_gran = (tot_gran + (blk_gran - 1)) // blk_gran * blk_gran
    pend_gran = jnp.cumsum(ptot_gran)
    gstart = pend_gran - ptot_gran
    global_off = gstart[None, :] + jnp.cumsum(seg_gran, axis=0) - seg_gran
    pad_n = ptot_gran - tot_gran
    pad_off = gstart + tot_gran
    n_rows = -(-(TOP_K * n_tok + n_tiles * N_EXPERTS * (GRAN - 1) + N_EXPERTS * (SUB_BLOCK - GRAN))
               // SUB_BLOCK) * SUB_BLOCK + ROW_BLOCK
    region_gran0 = gstart.astype(I32)
    region_subs = (ptot_gran // blk_gran).astype(I32)
    tile_gran = jnp.sum(seg_gran, axis=1).astype(I32)
    tile_big = jnp.any(seg_gran >= 4 * (FLAT_QUADS + 1), axis=1).astype(I32)
    seg_gran_f = seg_gran.reshape(-1).astype(I32)
    local_off_f = local_off.reshape(-1).astype(I32)
    global_off_f = global_off.reshape(-1).astype(I32)

    xs = pl.pallas_call(
        _dispatch_kernel,
        grid_spec=pltpu.PrefetchScalarGridSpec(
            num_scalar_prefetch=7,
            grid=(n_tiles,),
            in_specs=[pl.BlockSpec((1, ROUTE_ROWS, T), lambda i, *_: (i, 0, 0)),
                      pl.BlockSpec((T, D_MODEL), lambda i, *_: (i, 0))],
            out_specs=pl.BlockSpec(memory_space=pl.ANY),
            scratch_shapes=[pltpu.VMEM((2, LOCAL_ROWS // GRAN, GRAN, SLOT_WORDS), U32),
                            pltpu.VMEM((SUB_BLOCK // GRAN, GRAN, SLOT_WORDS), U32),
                            pltpu.SemaphoreType.DMA((3,))]),
        out_shape=jax.ShapeDtypeStruct((n_rows // GRAN, GRAN, SLOT_WORDS), U32),
        compiler_params=pltpu.CompilerParams(dimension_semantics=("arbitrary",),
                                             vmem_limit_bytes=VMEM_LIMIT),
        name="moe_dispatch",
    )(seg_gran_f, local_off_f, global_off_f, tile_gran, tile_big, pad_n.astype(I32), pad_off.astype(I32),
      route, h)

    ys = pl.pallas_call(
        _expert_kernel,
        grid_spec=pltpu.PrefetchScalarGridSpec(
            num_scalar_prefetch=2,
            grid=(N_EXPERTS,),
            in_specs=[pl.BlockSpec(memory_space=pl.ANY),
                      pl.BlockSpec((1, D_MODEL, 2 * D_FF), lambda e, *_: (e, 0, 0)),
                      pl.BlockSpec((1, 1, 2 * D_FF), lambda e, *_: (e, 0, 0)),
                      pl.BlockSpec((1, D_FF, D_MODEL), lambda e, *_: (e, 0, 0)),
                      pl.BlockSpec((1, 1, D_MODEL), lambda e, *_: (e, 0, 0))],
            out_specs=pl.BlockSpec(memory_space=pl.ANY),
            scratch_shapes=[pltpu.VMEM((D_MODEL, 2 * D_FF), BF16), pltpu.VMEM((D_FF, D_MODEL), BF16),
                            pltpu.VMEM((2, ROW_BLOCK // GRAN, GRAN, SLOT_WORDS), U32),
                            pltpu.VMEM((2, ROW_BLOCK // GRAN, GRAN, SLOT_WORDS), U32),
                            pltpu.SemaphoreType.DMA((2,)), pltpu.SemaphoreType.DMA((2,))]),
        out_shape=jax.ShapeDtypeStruct((n_rows // GRAN, GRAN, SLOT_WORDS), U32),
        compiler_params=pltpu.CompilerParams(dimension_semantics=("arbitrary",),
                                             vmem_limit_bytes=VMEM_LIMIT),
        name="moe_experts",
    )(region_gran0, region_subs, xs, w_exp_in[0], b_exp_in[0].reshape(N_EXPERTS, 1, 2 * D_FF),
      w_exp_out[0], b_exp_out[0].reshape(N_EXPERTS, 1, D_MODEL))

    y_p, y_s = pl.pallas_call(
        functools.partial(_combine_kernel, n_ptiles),
        grid_spec=pltpu.PrefetchScalarGridSpec(
            num_scalar_prefetch=5,
            grid=(n_tiles,),
            in_specs=[pl.BlockSpec((T, LANES), lambda i, *_: (i, 0)),
                      pl.BlockSpec((T, D_MODEL), lambda i, *_: (i, 0)),
                      pl.BlockSpec((1, D_MODEL), lambda i, *_: (0, 0)),
                      pl.BlockSpec(memory_space=pl.ANY)],
            out_specs=[pl.BlockSpec((T, D_MODEL), lambda i, *_: (jnp.minimum(i, n_ptiles - 1), 0)),
                       pl.BlockSpec((T, D_MODEL), lambda i, *_: (jnp.maximum(i - n_ptiles, 0), 0))],
            scratch_shapes=[pltpu.VMEM((2, LOCAL_ROWS // GRAN, GRAN, SLOT_WORDS), U32), pltpu.SemaphoreType.DMA((2,))]),
        out_shape=(jax.ShapeDtypeStruct((n_prompt, D_MODEL), F32),
                   jax.ShapeDtypeStruct((n_sample, D_MODEL), F32)),
        compiler_params=pltpu.CompilerParams(dimension_semantics=("arbitrary",),
                                             vmem_limit_bytes=VMEM_LIMIT),
        name="moe_combine",
    )(seg_gran_f, local_off_f, global_off_f, tile_gran, tile_big, routet, x1, gf, ys)

    y_prompt = y_p.reshape(batch, seq, D_MODEL)
    y_sample = y_s.reshape(dec_batch, dec_seq, D_MODEL)
    new_win_k_prompt = kvwin[:, :, :D_KV].reshape(1, batch, WINDOW, N_KV, HEAD_DIM)
    new_win_v_prompt = kvwin[:, :, D_KV:].reshape(1, batch, WINDOW, N_KV, HEAD_DIM)
    new_win_k_sample = kvnew[:, :D_KV].reshape(1, dec_batch, dec_seq, N_KV, HEAD_DIM)
    new_win_v_sample = kvnew[:, D_KV:].reshape(1, dec_batch, dec_seq, N_KV, HEAD_DIM)
    new_sgu_v_sample = vn_s.reshape(1, dec_batch, dec_seq, D_GMLP)
    return (y_prompt, y_sample, new_win_k_prompt, new_win_v_prompt, new_win_k_sample,
            new_win_v_sample, new_sgu_v_sample)
```

```python
import functools
import math

import numpy as np
import jax
import jax.numpy as jnp
from jax import lax
from jax.experimental import pallas as pl
from jax.experimental.pallas import tpu as pltpu

F32 = jnp.float32
BF16 = jnp.bfloat16
I32 = jnp.int32
U32 = jnp.uint32

D_MODEL = 1024
HEAD_DIM = 64
N_HEADS = 16
N_KV = 2
REP = N_HEADS // N_KV
CHUNK = 64
WINDOW = 128
BAND = WINDOW + CHUNK
KEY_PAD = 256
D_ATT = N_HEADS * HEAD_DIM
D_KV = N_KV * HEAD_DIM
NUM_BUCKETS = 32
MAX_DISTANCE = 128
GMLP_CHUNK = 128
D_GMLP = 1024
N_GROUPS = 4
GROUP_W = D_GMLP // N_GROUPS
N_EXPERTS = 32
TOP_K = 4
D_FF = 1024
SWIGLU_LIMIT = 7.0
SWIGLU_ALPHA = 1.702
NORM_EPS = 1e-5
D_IN = D_ATT + 2 * D_KV + 2 * D_GMLP + 2 * D_MODEL
COL_KV = D_ATT
COL_U = COL_KV + 2 * D_KV
COL_VG = COL_U + D_GMLP
COL_GA = COL_VG + D_GMLP
COL_GB = COL_GA + D_MODEL
SQRT_HALF = float(np.sqrt(0.5))

LANES = 128
WORD_SUBLANES = 8
VMEM_LIMIT = 56 * 1024 * 1024

TOK_TILE = 256
LAYER_TILE = 512
GRAN = WORD_SUBLANES
SLOT_WORDS = D_MODEL // 2
FLAT_QUADS = 1
LOCAL_ROWS = TOP_K * TOK_TILE + N_EXPERTS * GRAN
ROW_BLOCK = 512
SUB_BLOCK = 64
PAD_BITS = (SUB_BLOCK // GRAN - 1).bit_length()
TILE_BITS = (LOCAL_ROWS // GRAN).bit_length()
PROJ_PIECES = 16
ROUTE_ROWS = 16


def _const_spec(shape):
    nd = len(shape)
    return pl.BlockSpec(shape, lambda *_: (0,) * nd, pipeline_mode=pl.Buffered(1))


def _rms_norm(x, g):
    ms = jnp.mean(x * x, axis=-1, keepdims=True)
    return x * lax.rsqrt(ms + NORM_EPS) * g


def _gelu(x):
    return 0.5 * x * (1.0 + lax.erf(x * SQRT_HALF))


def _attend(q, k, v, bias, valid):
    logits = lax.dot_general(q, k, (((1,), (1,)), ((), ())), preferred_element_type=F32) + bias
    if valid is not None:
        logits = jnp.where(valid, logits, -jnp.inf)
    m = jnp.max(logits, axis=-1, keepdims=True)
    p = jnp.exp(logits - m)
    den = jnp.sum(p, axis=-1, keepdims=True)
    w = (p * (1.0 / den)).astype(BF16)
    return jnp.dot(w, v, preferred_element_type=F32)


def _attention_rows(zq, row0, n_rows, k_ext, v_ext, bias_ref, valid, oatt_ref):
    lane = lax.broadcasted_iota(I32, (1, LANES), 1)
    stack = []
    for g in range(N_KV):
        in_group = (lane >= g * HEAD_DIM) & (lane < (g + 1) * HEAD_DIM)
        qmask = jnp.where(in_group, HEAD_DIM ** -0.5, 0.0).astype(BF16)
        stack += [zq[row0:row0 + n_rows, t * LANES:(t + 1) * LANES] * qmask for t in range(REP)]
    bias = bias_ref[...].reshape(N_HEADS * n_rows, KEY_PAD)
    o = _attend(jnp.concatenate(stack, axis=0), k_ext, v_ext, bias, valid)
    for g in range(N_KV):
        for t in range(REP):
            r = (g * REP + t) * n_rows
            c0 = t * LANES + g * HEAD_DIM
            oatt_ref[row0:row0 + n_rows, c0:c0 + HEAD_DIM] = (
                o[r:r + n_rows, g * HEAD_DIM:(g + 1) * HEAD_DIM].astype(BF16))


def _proj_piece(xn_b, w_in_ref, k):
    width = (D_IN - COL_U) // PROJ_PIECES
    c0 = COL_U + k * width
    z = jnp.dot(xn_b, w_in_ref[:, c0:c0 + width], preferred_element_type=F32)
    return _gelu(z) if c0 < COL_GA else jax.nn.sigmoid(z)


def _sgu_branch(pieces, osgu_ref, sguw_ref, sgub_ref, lng_ref, lnb_ref, vn_ref):
    u = jnp.concatenate(pieces[:PROJ_PIECES // 4], axis=1)
    vg = jnp.concatenate(pieces[PROJ_PIECES // 4:PROJ_PIECES // 2], axis=1)
    T = u.shape[0]
    mu = jnp.mean(vg, axis=-1, keepdims=True)
    var = jnp.mean(jnp.square(vg - mu), axis=-1, keepdims=True)
    vn = (vg - mu) * lax.rsqrt(var + NORM_EPS) * lng_ref[...] + lnb_ref[...]
    if vn_ref is not None:
        vn_ref[...] = vn
    vn_b = vn.astype(BF16)
    for j in range(T // GMLP_CHUNK):
        rows = slice(j * GMLP_CHUNK, (j + 1) * GMLP_CHUNK)
        for g in range(N_GROUPS):
            cols = slice(g * GROUP_W, (g + 1) * GROUP_W)
            mixed = jnp.dot(sguw_ref[g], vn_b[rows, cols], preferred_element_type=F32) + sgub_ref[g]
            osgu_ref[rows, cols] = (u[rows, cols] * mixed).astype(BF16)


def _layer_tail(x, pieces, oatt_ref, osgu_ref, woa_ref, wos_ref, wout_ref, g2_ref, wr_ref, br_ref,
                tri_ref, low_ref, x1_ref, h_ref, route_ref, routet_ref, cnt_ref):
    T = x.shape[0]
    ga = jnp.concatenate(pieces[PROJ_PIECES // 2:3 * PROJ_PIECES // 4], axis=1)
    gb = jnp.concatenate(pieces[3 * PROJ_PIECES // 4:], axis=1)
    merged = ga * jnp.dot(oatt_ref[...], woa_ref[...], preferred_element_type=F32)
    merged = merged + gb * jnp.dot(osgu_ref[...], wos_ref[...], preferred_element_type=F32)
    x1 = x + jnp.dot(merged.astype(BF16), wout_ref[...], preferred_element_type=F32)
    x1_ref[...] = x1

    h_b = _rms_norm(x1, g2_ref[...]).astype(BF16)
    h_ref[...] = h_b
    lt = lax.dot_general(wr_ref[...], h_b, (((1,), (1,)), ((), ())),
                         preferred_element_type=F32) + br_ref[...]
    for s in range(T // TOK_TILE):
        rec, cnt_b = _route_sort_tile(lt[:, s * TOK_TILE:(s + 1) * TOK_TILE], tri_ref, low_ref)
        route_ref[s] = rec
        rec_pad = jnp.concatenate([rec, jnp.zeros((LANES - ROUTE_ROWS, TOK_TILE), F32)], axis=0)
        routet_ref[s * TOK_TILE:(s + 1) * TOK_TILE, :] = jnp.transpose(rec_pad)
        cnt_ref[s] = cnt_b


def _route_sort_tile(lt, tri_ref, low_ref):
    T = lt.shape[1]
    e_iota = lax.broadcasted_iota(I32, (N_EXPERTS, T), 0).astype(F32)
    cur = lt
    vals, idxs = [], []
    for _ in range(TOP_K):
        m = jnp.max(cur, axis=0, keepdims=True)
        ik = jnp.min(jnp.where(cur == m, e_iota, float(N_EXPERTS)), axis=0, keepdims=True)
        vals.append(m)
        idxs.append(ik)
        cur = jnp.where(e_iota == ik, -jnp.inf, cur)
    exps = [jnp.exp(v - vals[0]) for v in vals]
    den = exps[0] + exps[1] + exps[2] + exps[3]
    gates = [e / den for e in exps]

    onehot = jnp.zeros((N_EXPERTS, T), F32)
    for ik in idxs:
        onehot = onehot + jnp.where(e_iota == ik, 1.0, 0.0)
    rank = jnp.dot(onehot.astype(BF16), tri_ref[...], preferred_element_type=F32)
    cnt = jnp.sum(onehot, axis=1, keepdims=True)
    cnt_b = jnp.broadcast_to(cnt, (N_EXPERTS, LANES))
    gran = jnp.floor((cnt_b + (GRAN - 1)) * (1.0 / GRAN))
    off = jnp.dot(low_ref[...], gran.astype(BF16), preferred_element_type=F32) * GRAN
    base = off[:, 0:1] + rank
    poss = [jnp.sum(jnp.where(e_iota == ik, base, 0.0), axis=0, keepdims=True) for ik in idxs]

    rec = jnp.concatenate(idxs + poss + gates
                          + [jnp.zeros((ROUTE_ROWS - 3 * TOP_K, T), F32)], axis=0)
    return rec, cnt_b


def _prompt_layer_kernel(x_ref, g1_ref, wq_ref, w_in_ref, bias_ref, sguw_ref, sgub_ref, lng_ref,
                         lnb_ref, woa_ref, wos_ref, wout_ref, g2_ref, wr_ref, br_ref, tri_ref, low_ref,
                         x1_ref, h_ref, route_ref, routet_ref, cnt_ref, kvwin_ref,
                         kvx_ref, oatt_ref, osgu_ref):
    T = LAYER_TILE
    j = pl.program_id(1)
    @pl.when(j == 0)
    def _():
        kvx_ref[0:WINDOW, :] = jnp.zeros((WINDOW, 2 * D_KV), BF16)

    @pl.when(j > 0)
    def _():
        kvx_ref[0:WINDOW, :] = kvx_ref[T:T + WINDOW, :]

    x = x_ref[0]
    xn_b = _rms_norm(x, g1_ref[...]).astype(BF16)
    zq = jnp.dot(xn_b, wq_ref[...], preferred_element_type=F32).astype(BF16)
    zkv = jnp.dot(xn_b, w_in_ref[:, COL_KV:COL_U], preferred_element_type=F32)
    kvwin_ref[0] = zkv[T - WINDOW:, :]
    kvx_ref[WINDOW:, :] = zkv.astype(BF16)

    col = lax.broadcasted_iota(I32, (1, KEY_PAD), 1)
    key_pad = jnp.zeros((KEY_PAD - BAND, 2 * D_KV), BF16)
    n_chunks = T // CHUNK
    pieces = []
    for c in range(n_chunks):
        r0 = c * CHUNK
        kvb = jnp.concatenate([kvx_ref[r0:r0 + BAND, :], key_pad], axis=0)
        valid = ((col + (j * T + r0 - WINDOW)) >= 0) | (col >= BAND) if r0 < WINDOW else None
        _attention_rows(zq, r0, CHUNK, kvb[:, :D_KV], kvb[:, D_KV:], bias_ref, valid, oatt_ref)
        pieces += [_proj_piece(xn_b, w_in_ref, k) for k in range(len(pieces), (c + 1) * PROJ_PIECES // n_chunks)]

    _sgu_branch(pieces, osgu_ref, sguw_ref, sgub_ref, lng_ref, lnb_ref, None)
    _layer_tail(x, pieces, oatt_ref, osgu_ref, woa_ref, wos_ref, wout_ref, g2_ref, wr_ref, br_ref,
                tri_ref, low_ref, x1_ref, h_ref, route_ref, routet_ref, cnt_ref)


def _sample_layer_kernel(x1_in, h_in, route_in, routet_in, cnt_in,
                         x_ref, ck_ref, cv_ref, g1_ref, wq_ref, w_in_ref, bias_ref, sguw_ref, sgub_ref,
                         lng_ref, lnb_ref, woa_ref, wos_ref, wout_ref, g2_ref, wr_ref, br_ref, tri_ref,
                         low_ref,
                         x1_ref, h_ref, route_ref, routet_ref, cnt_ref, kvnew_ref, vn_ref,
                         oatt_ref, osgu_ref):
    del x1_in, h_in, route_in, routet_in, cnt_in
    T = LAYER_TILE
    nq = x_ref.shape[0] // ck_ref.shape[0]
    x = x_ref[...]
    xn_b = _rms_norm(x, g1_ref[...]).astype(BF16)
    zq = jnp.dot(xn_b, wq_ref[...], preferred_element_type=F32).astype(BF16)
    zkv = jnp.dot(xn_b, w_in_ref[:, COL_KV:COL_U], preferred_element_type=F32)
    kvnew_ref[...] = zkv
    zkv_b = zkv.astype(BF16)
    n_cache = ck_ref.shape[1]
    key_pad = jnp.zeros((KEY_PAD - n_cache - nq, D_KV), BF16)
    n_seqs = T // nq
    pieces = []
    for b in range(n_seqs):
        r0 = b * nq
        kk = jnp.concatenate([ck_ref[b].astype(BF16), zkv_b[r0:r0 + nq, :D_KV], key_pad], axis=0)
        vv = jnp.concatenate([cv_ref[b].astype(BF16), zkv_b[r0:r0 + nq, D_KV:], key_pad], axis=0)
        _attention_rows(zq, r0, nq, kk, vv, bias_ref, None, oatt_ref)
        pieces += [_proj_piece(xn_b, w_in_ref, k) for k in range(len(pieces), (b + 1) * PROJ_PIECES // n_seqs)]

    _sgu_branch(pieces, osgu_ref, sguw_ref, sgub_ref, lng_ref, lnb_ref, vn_ref)
    _layer_tail(x, pieces, oatt_ref, osgu_ref, woa_ref, wos_ref, wout_ref, g2_ref, wr_ref, br_ref,
                tri_ref, low_ref, x1_ref, h_ref, route_ref, routet_ref, cnt_ref)


def _pack_slot_rows(x):
    lo = lax.bitcast_convert_type(x[:, :SLOT_WORDS].astype(BF16).astype(F32), U32)
    hi = lax.bitcast_convert_type(x[:, SLOT_WORDS:].astype(BF16).astype(F32), U32)
    return (lo >> 16) | hi


def _pack_slot_rows_exact(x):
    lo = lax.bitcast_convert_type(x[:, :SLOT_WORDS], U32)
    hi = lax.bitcast_convert_type(x[:, SLOT_WORDS:], U32)
    return (lo >> 16) | hi


def _unpack_slot_rows(w):
    lo = lax.bitcast_convert_type(w << 16, F32)
    hi = lax.bitcast_convert_type(w & jnp.uint32(0xFFFF0000), F32)
    return jnp.concatenate([lo, hi], axis=1).astype(BF16)


def _as_granules(x):
    return x.reshape(x.shape[0] // GRAN, GRAN, SLOT_WORDS)


def _as_rows(x):
    return x.reshape(x.shape[0] * GRAN, SLOT_WORDS)


def _segment_copies(n_gran, src_gran, dst_gran, bits, make_copy, act):
    for b in range(min(bits, 2)):
        @pl.when(((n_gran >> b) & 1) == 1)
        def _(b=b):
            done = n_gran & ((1 << b) - 1)
            act(make_copy(src_gran + done, dst_gran + done, 1 << b))

    if bits > 2:
        def quad(q, carry):
            done = (n_gran & 3) + 4 * q
            act(make_copy(src_gran + done, dst_gran + done, 4))
            return carry
        lax.fori_loop(0, n_gran >> 2, quad, 0)


def _start_tile_copies(tile, live, cnt_ref, loff_ref, goff_ref, big_ref, make_copy):
    for e in range(N_EXPERTS):
        t = tile * N_EXPERTS + e
        n = jnp.where(live, cnt_ref[t], 0)
        src, dst = loff_ref[t], goff_ref[t]
        for b in range(2):
            @pl.when(((n >> b) & 1) == 1)
            def _(b=b, n=n, src=src, dst=dst):
                done = n & ((1 << b) - 1)
                make_copy(src + done, dst + done, 1 << b).start()
        for q in range(FLAT_QUADS):
            @pl.when((n >> 2) > q)
            def _(q=q, n=n, src=src, dst=dst):
                done = (n & 3) + 4 * q
                make_copy(src + done, dst + done, 4).start()

    @pl.when(live & (big_ref[tile] > 0))
    def _():
        def rest(e, carry):
            t = tile * N_EXPERTS + e
            n, src, dst = cnt_ref[t], loff_ref[t], goff_ref[t]

            def quad(q, c):
                done = (n & 3) + 4 * q
                make_copy(src + done, dst + done, 4).start()
                return c
            lax.fori_loop(FLAT_QUADS, n >> 2, quad, 0)
            return carry
        lax.fori_loop(0, N_EXPERTS, rest, 0)


def _wait_granules(n_gran, bits, make_copy):
    for b in range(bits):
        @pl.when(((n_gran >> b) & 1) == 1)
        def _(b=b):
            make_copy(1 << b).wait()


def _dispatch_kernel(cnt_ref, loff_ref, goff_ref, tot_ref, big_ref, padn_ref, padoff_ref,
                     route_ref, h_ref, xs_hbm, stage_ref, zero_ref, sem):
    i = pl.program_id(0)
    last = pl.num_programs(0) - 1
    slot = lax.rem(i, 2)
    T = TOK_TILE

    def seg_copy(tile_slot):
        return lambda s, d, n: pltpu.make_async_copy(
            stage_ref.at[tile_slot, pl.ds(s, n)], xs_hbm.at[pl.ds(d, n)], sem.at[tile_slot])

    def wait_tile(tile, tile_slot):
        _wait_granules(tot_ref[tile], TILE_BITS, lambda n: pltpu.make_async_copy(
            stage_ref.at[tile_slot, pl.ds(0, n)], xs_hbm.at[pl.ds(0, n)], sem.at[tile_slot]))

    @pl.when(i > 1)
    def _():
        wait_tile(i - 2, slot)

    _start_tile_copies(jnp.maximum(i - 1, 0), i > 0, cnt_ref, loff_ref, goff_ref, big_ref, seg_copy(1 - slot))

    pos = route_ref[0, TOP_K:2 * TOP_K, :].astype(I32)
    r_iota = lax.broadcasted_iota(I32, (LOCAL_ROWS, T), 0)
    p = jnp.zeros((LOCAL_ROWS, T), F32)
    for k in range(TOP_K):
        p = jnp.where(r_iota == pos[k:k + 1, :], 1.0, p)
    stage_ref[slot] = _as_granules(
        _pack_slot_rows_exact(jnp.dot(p.astype(BF16), h_ref[...], preferred_element_type=F32)))

    @pl.when(i == last)
    def _():
        _start_tile_copies(i, i >= 0, cnt_ref, loff_ref, goff_ref, big_ref, seg_copy(slot))

        @pl.when(i > 0)
        def _():
            wait_tile(i - 1, 1 - slot)

        wait_tile(i, slot)
        zero_ref[...] = jnp.zeros(zero_ref.shape, U32)

        def pad_copy(s, d, n):
            return pltpu.make_async_copy(zero_ref.at[pl.ds(s, n)], xs_hbm.at[pl.ds(d, n)], sem.at[2])

        def for_pads(act):
            def body(e, carry):
                _segment_copies(padn_ref[e], 0, padoff_ref[e], PAD_BITS, pad_copy, act)
                return carry
            lax.fori_loop(0, N_EXPERTS, body, 0)

        for_pads(lambda cp: cp.start())
        for_pads(lambda cp: cp.wait())


def _expert_kernel(gran0_ref, nsub_ref, xs_hbm, w1_ref, b1_ref, w2_ref, b2_ref, ys_hbm,
                   w1b_ref, w2b_ref, xbuf_ref, ybuf_ref, sem_in, sem_out):
    e = pl.program_id(0)
    subs = ROW_BLOCK // SUB_BLOCK
    sub_gran = SUB_BLOCK // GRAN
    blk_gran = ROW_BLOCK // GRAN
    gran0 = gran0_ref[e]
    nsub = nsub_ref[e]
    nb_full = nsub // subs
    rem = nsub - nb_full * subs
    nb = nb_full + jnp.where(rem > 0, 1, 0)
    last_slot = lax.rem(nb + 1, 2)

    def in_copy(k, slot):
        return pltpu.make_async_copy(xs_hbm.at[pl.ds(gran0 + k * blk_gran, blk_gran)], xbuf_ref.at[slot],
                                     sem_in.at[slot])

    def out_copy(k, slot, m_sub):
        n = m_sub * sub_gran
        return pltpu.make_async_copy(ybuf_ref.at[slot, pl.ds(0, n)], ys_hbm.at[pl.ds(gran0 + k * blk_gran, n)],
                                     sem_out.at[slot])

    @pl.when(nb > 0)
    def _():
        in_copy(0, 0).start()

    w1b_ref[...] = w1_ref[0].astype(BF16)
    w2b_ref[...] = w2_ref[0].astype(BF16)

    def process(k, m_sub):
        n = m_sub * sub_gran
        slot = lax.rem(k, 2)

        @pl.when(k + 1 < nb)
        def _():
            in_copy(k + 1, 1 - slot).start()

        in_copy(k, slot).wait()

        @pl.when(k >= 2)
        def _():
            out_copy(k - 2, slot, subs).wait()

        x = _unpack_slot_rows(_as_rows(xbuf_ref[slot, 0:n]))
        h1 = jnp.dot(x, w1b_ref[...], preferred_element_type=F32) + b1_ref[0]
        gate = jnp.minimum(h1[:, :D_FF], SWIGLU_LIMIT)
        up = jnp.clip(h1[:, D_FF:], -SWIGLU_LIMIT, SWIGLU_LIMIT)
        act = gate * jax.nn.sigmoid(SWIGLU_ALPHA * gate) * (up + 1.0)
        y = jnp.dot(act.astype(BF16), w2b_ref[...], preferred_element_type=F32) + b2_ref[0]
        ybuf_ref[slot, 0:n] = _as_granules(_pack_slot_rows(y))
        out_copy(k, slot, m_sub).start()

    def full_block(k, carry):
        process(k, subs)
        return carry

    lax.fori_loop(0, nb_full, full_block, 0)

    for m_sub in range(1, subs):
        @pl.when(rem == m_sub)
        def _(m_sub=m_sub):
            process(nb_full, m_sub)

    @pl.when(nb >= 2)
    def _():
        out_copy(nb - 2, 1 - last_slot, subs).wait()

    @pl.when((nb >= 1) & (rem == 0))
    def _():
        out_copy(nb - 1, last_slot, subs).wait()

    for m_sub in range(1, subs):
        @pl.when(rem == m_sub)
        def _(m_sub=m_sub):
            out_copy(nb - 1, last_slot, m_sub).wait()


def _combine_kernel(n_prompt_tiles, cnt_ref, loff_ref, goff_ref, tot_ref, big_ref,
                    routet_ref, x1_ref, gf_ref, ys_hbm, yp_ref, ysm_ref, stage_ref, sem):
    i = pl.program_id(0)
    n_steps = pl.num_programs(0)
    slot = lax.rem(i, 2)
    T = TOK_TILE

    def seg_copy(tile_slot):
        return lambda s, d, n: pltpu.make_async_copy(
            ys_hbm.at[pl.ds(d, n)], stage_ref.at[tile_slot, pl.ds(s, n)], sem.at[tile_slot])

    @pl.when(i == 0)
    def _():
        stage_ref[...] = jnp.zeros(stage_ref.shape, U32)
        _start_tile_copies(i, i >= 0, cnt_ref, loff_ref, goff_ref, big_ref, seg_copy(slot))

    _start_tile_copies(jnp.minimum(i + 1, n_steps - 1), i + 1 < n_steps, cnt_ref, loff_ref, goff_ref, big_ref,
                       seg_copy(1 - slot))

    rt = routet_ref[...]
    l_iota = lax.broadcasted_iota(I32, (T, LOCAL_ROWS), 1)
    pw = jnp.zeros((T, LOCAL_ROWS), F32)
    for k in range(TOP_K):
        pos_k = rt[:, TOP_K + k:TOP_K + k + 1].astype(I32)
        pw = jnp.where(l_iota == pos_k, rt[:, 2 * TOP_K + k:2 * TOP_K + k + 1], pw)

    _wait_granules(tot_ref[i], TILE_BITS, lambda n: pltpu.make_async_copy(
        ys_hbm.at[pl.ds(0, n)], stage_ref.at[slot, pl.ds(0, n)], sem.at[slot]))
    moe = jnp.dot(pw.astype(BF16), _unpack_slot_rows(_as_rows(stage_ref[slot])), preferred_element_type=F32)
    y = _rms_norm(x1_ref[...] + moe, gf_ref[...])

    @pl.when(i < n_prompt_tiles)
    def _():
        yp_ref[...] = y

    @pl.when(i >= n_prompt_tiles)
    def _():
        ysm_ref[...] = y


def _t5_bucket(rel):
    half = NUM_BUCKETS // 2
    max_exact = half // 2
    ret = jnp.where(rel > 0, half, 0)
    n = jnp.abs(rel)
    nf = jnp.maximum(n, 1).astype(F32)
    large = max_exact + (jnp.log(nf / max_exact) / math.log(MAX_DISTANCE / max_exact)
                         * (half - max_exact)).astype(jnp.int32)
    large = jnp.minimum(large, half - 1)
    return ret + jnp.where(n < max_exact, n, large)


def _stacked_bias(table, sinks, q_pos, k_pos):
    nq, nk = q_pos.shape[0], k_pos.shape[0]
    bucket = _t5_bucket(k_pos[None, :] - q_pos[:, None])
    onehot = (bucket[:, :, None] == jnp.arange(NUM_BUCKETS)).astype(F32)
    bias = jnp.einsum('qkb,bh->hqk', onehot, table.astype(F32), precision=lax.Precision.HIGHEST)
    bias = bias.reshape(N_KV, REP * nq, nk)
    sink = jnp.repeat(sinks.astype(F32).reshape(N_KV, REP, 1), nq, axis=2).reshape(N_KV, REP * nq, 1)
    tail = jnp.full((N_KV, REP * nq, KEY_PAD - nk - 1), -jnp.inf, F32)
    return jnp.concatenate([bias, sink, tail], axis=-1)


def kernel(x_prompt, x_sample, cache_win_k, cache_win_v, norm1_g, w_in, attn_sinks, rel_bias_table, sgu_ln_g, sgu_ln_b, sgu_w, sgu_b, w_o_att, w_o_sgu, w_out, norm2_g, w_router, b_router, w_exp_in, b_exp_in, w_exp_out, b_exp_out, final_norm_g):
    batch, seq, _ = x_prompt.shape
    dec_batch, dec_seq, _ = x_sample.shape
    cache_rows = cache_win_k.shape[2]
    assert x_prompt.shape[2] == D_MODEL and w_in.shape == (1, D_MODEL, D_IN)
    assert seq % LAYER_TILE == 0 and LAYER_TILE % TOK_TILE == 0 and TOK_TILE % GMLP_CHUNK == 0
    assert LAYER_TILE >= WINDOW and LAYER_TILE % dec_seq == 0 and (dec_batch * dec_seq) % LAYER_TILE == 0
    assert dec_seq <= GMLP_CHUNK and GMLP_CHUNK % dec_seq == 0 and cache_rows == WINDOW
    T = TOK_TILE
    n_prompt = batch * seq
    n_sample = dec_batch * dec_seq
    n_tok = n_prompt + n_sample
    LT = LAYER_TILE
    sorts_per_step = LT // T
    tiles_per_seq = seq // LT
    n_psteps = n_prompt // LT
    n_ssteps = n_sample // LT
    n_ptiles = n_prompt // T
    n_tiles = n_tok // T
    seqs_per_tile = LT // dec_seq

    w_in_b = w_in[0].astype(BF16)
    w_q_b = w_in_b[:, :D_ATT].reshape(D_MODEL, N_KV, REP, HEAD_DIM).transpose(0, 2, 1, 3).reshape(
        D_MODEL, D_ATT)
    woa_b = w_o_att[0].reshape(N_KV, REP, HEAD_DIM, D_MODEL).transpose(1, 0, 2, 3).reshape(
        D_ATT, D_MODEL).astype(BF16)
    wos_b = w_o_sgu[0].astype(BF16)
    wout_b = w_out[0].astype(BF16)
    wr_b = jnp.transpose(w_router[0]).astype(BF16)
    br_col = b_router[0].astype(F32).reshape(N_EXPERTS, 1)
    g1 = norm1_g[0].reshape(1, D_MODEL)
    g2 = norm2_g[0].reshape(1, D_MODEL)
    gf = final_norm_g.reshape(1, D_MODEL)
    lng = sgu_ln_g[0].reshape(1, D_GMLP)
    lnb = sgu_ln_b[0].reshape(1, D_GMLP)
    tril = jnp.tril(jnp.ones((GMLP_CHUNK, GMLP_CHUNK), dtype=bool))
    sguw_p = jnp.where(tril[None], sgu_w[0], 0).astype(BF16)
    sgub_p = jnp.broadcast_to(sgu_b[0][:, :, None], (N_GROUPS, GMLP_CHUNK, GROUP_W)).astype(F32)
    reps = GMLP_CHUNK // dec_seq
    corner = jnp.where(tril[None, :dec_seq, :dec_seq], sgu_w[0][:, :dec_seq, :dec_seq], 0)
    sguw_s = jnp.einsum('ab,gij->gaibj', jnp.eye(reps, dtype=F32), corner).reshape(
        N_GROUPS, GMLP_CHUNK, GMLP_CHUNK).astype(BF16)
    sgub_s = jnp.broadcast_to(jnp.tile(sgu_b[0][:, :dec_seq], (1, reps))[:, :, None],
                              (N_GROUPS, GMLP_CHUNK, GROUP_W)).astype(F32)
    bias_p = _stacked_bias(rel_bias_table, attn_sinks[0], jnp.arange(CHUNK) + WINDOW, jnp.arange(BAND))
    bias_s = _stacked_bias(rel_bias_table, attn_sinks[0], cache_rows + jnp.arange(dec_seq),
                           jnp.arange(cache_rows + dec_seq))
    tri = jnp.triu(jnp.ones((T, T), F32), k=1).astype(BF16)
    low = jnp.tril(jnp.ones((N_EXPERTS, N_EXPERTS), F32), k=-1).astype(BF16)

    layer_out_shapes = (
        jax.ShapeDtypeStruct((n_tok, D_MODEL), F32),
        jax.ShapeDtypeStruct((n_tok, D_MODEL), BF16),
        jax.ShapeDtypeStruct((n_tiles, ROUTE_ROWS, T), F32),
        jax.ShapeDtypeStruct((n_tok, LANES), F32),
        jax.ShapeDtypeStruct((n_tiles, N_EXPERTS, LANES), F32),
    )
    shared_consts = (w_q_b, w_in_b)
    tail_consts_p = (sguw_p, sgub_p, lng, lnb, woa_b, wos_b, wout_b, g2, wr_b, br_col, tri, low)
    tail_consts_s = (sguw_s, sgub_s, lng, lnb, woa_b, wos_b, wout_b, g2, wr_b, br_col, tri, low)

    def tile_specs(tile_of):
        return [
            pl.BlockSpec((LT, D_MODEL), lambda *g: (tile_of(*g), 0)),
            pl.BlockSpec((LT, D_MODEL), lambda *g: (tile_of(*g), 0)),
            pl.BlockSpec((sorts_per_step, ROUTE_ROWS, T), lambda *g: (tile_of(*g), 0, 0)),
            pl.BlockSpec((LT, LANES), lambda *g: (tile_of(*g), 0)),
            pl.BlockSpec((sorts_per_step, N_EXPERTS, LANES), lambda *g: (tile_of(*g), 0, 0)),
        ]

    prompt_consts = (g1,) + shared_consts + (bias_p,) + tail_consts_p
    x1, h, route, routet, cnt, kvwin = pl.pallas_call(
        _prompt_layer_kernel,
        grid=(batch, tiles_per_seq),
        in_specs=[pl.BlockSpec((1, LT, D_MODEL), lambda b, j: (b, j, 0))]
                 + [_const_spec(c.shape) for c in prompt_consts],
        out_specs=tile_specs(lambda b, j: b * tiles_per_seq + j)
                  + [pl.BlockSpec((1, WINDOW, 2 * D_KV), lambda b, j: (b, 0, 0))],
        out_shape=layer_out_shapes + (jax.ShapeDtypeStruct((batch, WINDOW, 2 * D_KV), F32),),
        scratch_shapes=[pltpu.VMEM((LT + WINDOW, 2 * D_KV), BF16),
                        pltpu.VMEM((LT, D_ATT), BF16),
                        pltpu.VMEM((LT, D_GMLP), BF16)],
        compiler_params=pltpu.CompilerParams(dimension_semantics=("arbitrary", "arbitrary"),
                                             vmem_limit_bytes=VMEM_LIMIT),
        name="layer_prompt",
    )(x_prompt, *prompt_consts)

    xs_flat = x_sample.reshape(n_sample, D_MODEL)
    ck = cache_win_k[0].reshape(dec_batch, cache_rows, D_KV)
    cv = cache_win_v[0].reshape(dec_batch, cache_rows, D_KV)
    sample_consts = (g1,) + shared_consts + (bias_s,) + tail_consts_s
    any_spec = pl.BlockSpec(memory_space=pl.ANY)
    x1, h, route, routet, cnt, kvnew, vn_s = pl.pallas_call(
        _sample_layer_kernel,
        grid=(n_ssteps,),
        in_specs=[any_spec] * 5
                 + [pl.BlockSpec((LT, D_MODEL), lambda i: (i, 0)),
                    pl.BlockSpec((seqs_per_tile, cache_rows, D_KV), lambda i: (i, 0, 0)),
                    pl.BlockSpec((seqs_per_tile, cache_rows, D_KV), lambda i: (i, 0, 0))]
                 + [_const_spec(c.shape) for c in sample_consts],
        out_specs=tile_specs(lambda i: n_psteps + i)
                  + [pl.BlockSpec((LT, 2 * D_KV), lambda i: (i, 0)),
                     pl.BlockSpec((LT, D_GMLP), lambda i: (i, 0))],
        out_shape=layer_out_shapes + (jax.ShapeDtypeStruct((n_sample, 2 * D_KV), F32),
                                      jax.ShapeDtypeStruct((n_sample, D_GMLP), F32)),
        scratch_shapes=[pltpu.VMEM((LT, D_ATT), BF16), pltpu.VMEM((LT, D_GMLP), BF16)],
        input_output_aliases={0: 0, 1: 1, 2: 2, 3: 3, 4: 4},
        compiler_params=pltpu.CompilerParams(dimension_semantics=("arbitrary",),
                                             vmem_limit_bytes=VMEM_LIMIT),
        name="layer_sample",
    )(x1, h, route, routet, cnt, xs_flat, ck, cv, *sample_consts)

    blk_gran = SUB_BLOCK // GRAN
    counts =cnt[:, :, 0].astype(I32)
    seg_gran = (counts + (GRAN - 1)) // GRAN
    local_off = jnp.cumsum(seg_gran, axis=1) - seg_gran
    tot_gran = jnp.sum(seg_gran, axis=0)
    ptot_gran = (tot_gran + (blk_gran - 1)) // blk_gran * blk_gran
    pend_gran = jnp.cumsum(ptot_gran)
    gstart = pend_gran - ptot_gran
    global_off = gstart[None, :] + jnp.cumsum(seg_gran, axis=0) - seg_gran
    pad_n = ptot_gran - tot_gran
    pad_off = gstart + tot_gran
    n_rows = -(-(TOP_K * n_tok + n_tiles * N_EXPERTS * (GRAN - 1) + N_EXPERTS * (SUB_BLOCK - GRAN))
               // SUB_BLOCK) * SUB_BLOCK + ROW_BLOCK
    region_gran0 = gstart.astype(I32)
    region_subs = (ptot_gran // blk_gran).astype(I32)
    tile_gran = jnp.sum(seg_gran, axis=1).astype(I32)
    tile_big = jnp.any(seg_gran >= 4 * (FLAT_QUADS + 1), axis=1).astype(I32)
    seg_gran_f = seg_gran.reshape(-1).astype(I32)
    local_off_f = local_off.reshape(-1).astype(I32)
    global_off_f = global_off.reshape(-1).astype(I32)

    xs = pl.pallas_call(
        _dispatch_kernel,
        grid_spec=pltpu.PrefetchScalarGridSpec(
            num_scalar_prefetch=7,
            grid=(n_tiles,),
            in_specs=[pl.BlockSpec((1, ROUTE_ROWS, T), lambda i, *_: (i, 0, 0)),
                      pl.BlockSpec((T, D_MODEL), lambda i, *_: (i, 0))],
            out_specs=pl.BlockSpec(memory_space=pl.ANY),
            scratch_shapes=[pltpu.VMEM((2, LOCAL_ROWS // GRAN, GRAN, SLOT_WORDS), U32),
                            pltpu.VMEM((SUB_BLOCK // GRAN, GRAN, SLOT_WORDS), U32),
                            pltpu.SemaphoreType.DMA((3,))]),
        out_shape=jax.ShapeDtypeStruct((n_rows // GRAN, GRAN, SLOT_WORDS), U32),
        compiler_params=pltpu.CompilerParams(dimension_semantics=("arbitrary",),
                                             vmem_limit_bytes=VMEM_LIMIT),
        name="moe_dispatch",
    )(seg_gran_f, local_off_f, global_off_f, tile_gran, tile_big, pad_n.astype(I32), pad_off.astype(I32),
      route, h)

    ys = pl.pallas_call(
        _expert_kernel,
        grid_spec=pltpu.PrefetchScalarGridSpec(
            num_scalar_prefetch=2,
            grid=(N_EXPERTS,),
            in_specs=[pl.BlockSpec(memory_space=pl.ANY),
                      pl.BlockSpec((1, D_MODEL, 2 * D_FF), lambda e, *_: (e, 0, 0)),
                      pl.BlockSpec((1, 1, 2 * D_FF), lambda e, *_: (e, 0, 0)),
                      pl.BlockSpec((1, D_FF, D_MODEL), lambda e, *_: (e, 0, 0)),
                      pl.BlockSpec((1, 1, D_MODEL), lambda e, *_: (e, 0, 0))],
            out_specs=pl.BlockSpec(memory_space=pl.ANY),
            scratch_shapes=[pltpu.VMEM((D_MODEL, 2 * D_FF), BF16), pltpu.VMEM((D_FF, D_MODEL), BF16),
                            pltpu.VMEM((2, ROW_BLOCK // GRAN, GRAN, SLOT_WORDS), U32),
                            pltpu.VMEM((2, ROW_BLOCK // GRAN, GRAN, SLOT_WORDS), U32),
                            pltpu.SemaphoreType.DMA((2,)), pltpu.SemaphoreType.DMA((2,))]),
        out_shape=jax.ShapeDtypeStruct((n_rows // GRAN, GRAN, SLOT_WORDS), U32),
        compiler_params=pltpu.CompilerParams(dimension_semantics=("arbitrary",),
                                             vmem_limit_bytes=VMEM_LIMIT),
        name="moe_experts",
    )(region_gran0, region_subs, xs, w_exp_in[0], b_exp_in[0].reshape(N_EXPERTS, 1, 2 * D_FF),
      w_exp_out[0], b_exp_out[0].reshape(N_EXPERTS, 1, D_MODEL))

    y_p, y_s = pl.pallas_call(
        functools.partial(_combine_kernel, n_ptiles),
        grid_spec=pltpu.PrefetchScalarGridSpec(
            num_scalar_prefetch=5,
            grid=(n_tiles,),
            in_specs=[pl.BlockSpec((T, LANES), lambda i, *_: (i, 0)),
                      pl.BlockSpec((T, D_MODEL), lambda i, *_: (i, 0)),
                      pl.BlockSpec((1, D_MODEL), lambda i, *_: (0, 0)),
                      pl.BlockSpec(memory_space=pl.ANY)],
            out_specs=[pl.BlockSpec((T, D_MODEL), lambda i, *_: (jnp.minimum(i, n_ptiles - 1), 0)),
                       pl.BlockSpec((T, D_MODEL), lambda i, *_: (jnp.maximum(i - n_ptiles, 0), 0))],
            scratch_shapes=[pltpu.VMEM((2, LOCAL_ROWS // GRAN, GRAN, SLOT_WORDS), U32), pltpu.SemaphoreType.DMA((2,))]),
        out_shape=(jax.ShapeDtypeStruct((n_prompt, D_MODEL), F32),
                   jax.ShapeDtypeStruct((n_sample, D_MODEL), F32)),
        compiler_params=pltpu.CompilerParams(dimension_semantics=("arbitrary",),
                                             vmem_limit_bytes=VMEM_LIMIT),
        name="moe_combine",
    )(seg_gran_f, local_off_f, global_off_f, tile_gran, tile_big, routet, x1, gf, ys)

    y_prompt = y_p.reshape(batch, seq, D_MODEL)
    y_sample = y_s.reshape(dec_batch, dec_seq, D_MODEL)
    new_win_k_prompt = kvwin[:, :, :D_KV].reshape(1, batch, WINDOW, N_KV, HEAD_DIM)
    new_win_v_prompt = kvwin[:, :, D_KV:].reshape(1, batch, WINDOW, N_KV, HEAD_DIM)
    new_win_k_sample = kvnew[:, :D_KV].reshape(1, dec_batch, dec_seq, N_KV, HEAD_DIM)
    new_win_v_sample = kvnew[:, D_KV:].reshape(1, dec_batch, dec_seq, N_KV, HEAD_DIM)
    new_sgu_v_sample = vn_s.reshape(1, dec_batch, dec_seq, D_GMLP)
    return (y_prompt, y_sample, new_win_k_prompt, new_win_v_prompt, new_win_k_sample,
            new_win_v_sample, new_sgu_v_sample)
```

```python
import functools
import math

import numpy as np
import jax
import jax.numpy as jnp
from jax import lax
from jax.experimental import pallas as pl
from jax.experimental.pallas import tpu as pltpu

F32 = jnp.float32
BF16 = jnp.bfloat16
I32 = jnp.int32
U32 = jnp.uint32

D_MODEL = 1024
HEAD_DIM = 64
N_HEADS = 16
N_KV = 2
REP = N_HEADS // N_KV
CHUNK = 64
WINDOW = 128
BAND = WINDOW + CHUNK
KEY_PAD = 256
D_ATT = N_HEADS * HEAD_DIM
D_KV = N_KV * HEAD_DIM
NUM_BUCKETS = 32
MAX_DISTANCE = 128
GMLP_CHUNK = 128
D_GMLP = 1024
N_GROUPS = 4
GROUP_W = D_GMLP // N_GROUPS
N_EXPERTS = 32
TOP_K = 4
D_FF = 1024
SWIGLU_LIMIT = 7.0
SWIGLU_ALPHA = 1.702
NORM_EPS = 1e-5
D_IN = D_ATT + 2 * D_KV + 2 * D_GMLP + 2 * D_MODEL
COL_KV = D_ATT
COL_U = COL_KV + 2 * D_KV
COL_VG = COL_U + D_GMLP
COL_GA = COL_VG + D_GMLP
COL_GB = COL_GA + D_MODEL
SQRT_HALF = float(np.sqrt(0.5))

LANES = 128
WORD_SUBLANES = 8
VMEM_LIMIT = 56 * 1024 * 1024

TOK_TILE = 256
LAYER_TILE = 512
GRAN = WORD_SUBLANES
SLOT_WORDS = D_MODEL // 2
FLAT_QUADS = 1
LOCAL_ROWS = TOP_K * TOK_TILE + N_EXPERTS * GRAN
ROW_BLOCK = 512
SUB_BLOCK = 64
PAD_BITS = (SUB_BLOCK // GRAN - 1).bit_length()
TILE_BITS = (LOCAL_ROWS // GRAN).bit_length()
PROJ_PIECES = 16
ROUTE_ROWS = 16


def _const_spec(shape):
    nd = len(shape)
    return pl.BlockSpec(shape, lambda *_: (0,) * nd, pipeline_mode=pl.Buffered(1))


def _rms_norm(x, g):
    ms = jnp.mean(x * x, axis=-1, keepdims=True)
    return x * lax.rsqrt(ms + NORM_EPS) * g


def _gelu(x):
    return 0.5 * x * (1.0 + lax.erf(x * SQRT_HALF))


def _attend(q, k, v, bias, valid):
    logits = lax.dot_general(q, k, (((1,), (1,)), ((), ())), preferred_element_type=F32) + bias
    if valid is not None:
        logits = jnp.where(valid, logits, -jnp.inf)
    m = jnp.max(logits, axis=-1, keepdims=True)
    p = jnp.exp(logits - m)
    den = jnp.sum(p, axis=-1, keepdims=True)
    w = (p * (1.0 / den)).astype(BF16)
    return jnp.dot(w, v, preferred_element_type=F32)


def _attention_rows(zq, row0, n_rows, k_ext, v_ext, bias_ref, valid, oatt_ref):
    lane = lax.broadcasted_iota(I32, (1, LANES), 1)
    stack = []
    for g in range(N_KV):
        in_group = (lane >= g * HEAD_DIM) & (lane < (g + 1) * HEAD_DIM)
        qmask = jnp.where(in_group, HEAD_DIM ** -0.5, 0.0).astype(BF16)
        stack += [zq[row0:row0 + n_rows, t * LANES:(t + 1) * LANES] * qmask for t in range(REP)]
    bias = bias_ref[...].reshape(N_HEADS * n_rows, KEY_PAD)
    o = _attend(jnp.concatenate(stack, axis=0), k_ext, v_ext, bias, valid)
    for g in range(N_KV):
        for t in range(REP):
            r = (g * REP + t) * n_rows
            c0 = t * LANES + g * HEAD_DIM
            oatt_ref[row0:row0 + n_rows, c0:c0 + HEAD_DIM] = (
                o[r:r + n_rows, g * HEAD_DIM:(g + 1) * HEAD_DIM].astype(BF16))


def _proj_piece(xn_b, w_in_ref, k):
    width = (D_IN - COL_U) // PROJ_PIECES
    c0 = COL_U + k * width
    z = jnp.dot(xn_b, w_in_ref[:, c0:c0 + width], preferred_element_type=F32)
    return _gelu(z) if c0 < COL_GA else jax.nn.sigmoid(z)


def _sgu_branch(pieces, osgu_ref, sguw_ref, sgub_ref, lng_ref, lnb_ref, vn_ref):
    u = jnp.concatenate(pieces[:PROJ_PIECES // 4], axis=1)
    vg = jnp.concatenate(pieces[PROJ_PIECES // 4:PROJ_PIECES // 2], axis=1)
    T = u.shape[0]
    mu = jnp.mean(vg, axis=-1, keepdims=True)
    var = jnp.mean(jnp.square(vg - mu), axis=-1, keepdims=True)
    vn = (vg - mu) * lax.rsqrt(var + NORM_EPS) * lng_ref[...] + lnb_ref[...]
    if vn_ref is not None:
        vn_ref[...] = vn
    vn_b = vn.astype(BF16)
    for j in range(T // GMLP_CHUNK):
        rows = slice(j * GMLP_CHUNK, (j + 1) * GMLP_CHUNK)
        for g in range(N_GROUPS):
            cols = slice(g * GROUP_W, (g + 1) * GROUP_W)
            mixed = jnp.dot(sguw_ref[g], vn_b[rows, cols], preferred_element_type=F32) + sgub_ref[g]
            osgu_ref[rows, cols] = (u[rows, cols] * mixed).astype(BF16)


def _layer_tail(x, pieces, oatt_ref, osgu_ref, woa_ref, wos_ref, wout_ref, g2_ref, wr_ref, br_ref,
                tri_ref, low_ref, x1_ref, h_ref, route_ref, routet_ref, cnt_ref):
    T = x.shape[0]
    ga = jnp.concatenate(pieces[PROJ_PIECES // 2:3 * PROJ_PIECES // 4], axis=1)
    gb = jnp.concatenate(pieces[3 * PROJ_PIECES // 4:], axis=1)
    merged = ga * jnp.dot(oatt_ref[...], woa_ref[...], preferred_element_type=F32)
    merged = merged + gb * jnp.dot(osgu_ref[...], wos_ref[...], preferred_element_type=F32)
    x1 = x + jnp.dot(merged.astype(BF16), wout_ref[...], preferred_element_type=F32)
    x1_ref[...] = x1

    h_b = _rms_norm(x1, g2_ref[...]).astype(BF16)
    h_ref[...] = h_b
    lt = lax.dot_general(wr_ref[...], h_b, (((1,), (1,)), ((), ())),
                         preferred_element_type=F32) + br_ref[...]
    for s in range(T // TOK_TILE):
        rec, cnt_b = _route_sort_tile(lt[:, s * TOK_TILE:(s + 1) * TOK_TILE], tri_ref, low_ref)
        route_ref[s] = rec
        rec_pad = jnp.concatenate([rec, jnp.zeros((LANES - ROUTE_ROWS, TOK_TILE), F32)], axis=0)
        routet_ref[s * TOK_TILE:(s + 1) * TOK_TILE, :] = jnp.transpose(rec_pad)
        cnt_ref[s] = cnt_b


def _route_sort_tile(lt, tri_ref, low_ref):
    T = lt.shape[1]
    e_iota = lax.broadcasted_iota(I32, (N_EXPERTS, T), 0).astype(F32)
    cur = lt
    vals, idxs = [], []
    for _ in range(TOP_K):
        m = jnp.max(cur, axis=0, keepdims=True)
        ik = jnp.min(jnp.where(cur == m, e_iota, float(N_EXPERTS)), axis=0, keepdims=True)
        vals.append(m)
        idxs.append(ik)
        cur = jnp.where(e_iota == ik, -jnp.inf, cur)
    exps = [jnp.exp(v - vals[0]) for v in vals]
    den = exps[0] + exps[1] + exps[2] + exps[3]
    gates = [e / den for e in exps]

    onehot = jnp.zeros((N_EXPERTS, T), F32)
    for ik in idxs:
        onehot = onehot + jnp.where(e_iota == ik, 1.0, 0.0)
    rank = jnp.dot(onehot.astype(BF16), tri_ref[...], preferred_element_type=F32)
    cnt = jnp.sum(onehot, axis=1, keepdims=True)
    cnt_b = jnp.broadcast_to(cnt, (N_EXPERTS, LANES))
    gran = jnp.floor((cnt_b + (GRAN - 1)) * (1.0 / GRAN))
    off = jnp.dot(low_ref[...], gran.astype(BF16), preferred_element_type=F32) * GRAN
    base = off[:, 0:1] + rank
    poss = [jnp.sum(jnp.where(e_iota == ik, base, 0.0), axis=0, keepdims=True) for ik in idxs]

    rec = jnp.concatenate(idxs + poss + gates
                          + [jnp.zeros((ROUTE_ROWS - 3 * TOP_K, T), F32)], axis=0)
    return rec, cnt_b


def _prompt_layer_kernel(x_ref, g1_ref, wq_ref, w_in_ref, bias_ref, sguw_ref, sgub_ref, lng_ref,
                         lnb_ref, woa_ref, wos_ref, wout_ref, g2_ref, wr_ref, br_ref, tri_ref, low_ref,
                         x1_ref, h_ref, route_ref, routet_ref, cnt_ref, kvwin_ref,
                         kvx_ref, oatt_ref, osgu_ref):
    T = LAYER_TILE
    j = pl.program_id(1)
    @pl.when(j == 0)
    def _():
        kvx_ref[0:WINDOW, :] = jnp.zeros((WINDOW, 2 * D_KV), BF16)

    @pl.when(j > 0)
    def _():
        kvx_ref[0:WINDOW, :] = kvx_ref[T:T + WINDOW, :]

    x = x_ref[0]
    xn_b = _rms_norm(x, g1_ref[...]).astype(BF16)
    zq = jnp.dot(xn_b, wq_ref[...], preferred_element_type=F32).astype(BF16)
    zkv = jnp.dot(xn_b, w_in_ref[:, COL_KV:COL_U], preferred_element_type=F32)
    kvwin_ref[0] = zkv[T - WINDOW:, :]
    kvx_ref[WINDOW:, :] = zkv.astype(BF16)

    col = lax.broadcasted_iota(I32, (1, KEY_PAD), 1)
    key_pad = jnp.zeros((KEY_PAD - BAND, 2 * D_KV), BF16)
    n_chunks = T // CHUNK
    pieces = []
    for c in range(n_chunks):
        r0 = c * CHUNK
        kvb = jnp.concatenate([kvx_ref[r0:r0 + BAND, :], key_pad], axis=0)
        valid = ((col + (j * T + r0 - WINDOW)) >= 0) | (col >= BAND) if r0 < WINDOW else None
        _attention_rows(zq, r0, CHUNK, kvb[:, :D_KV], kvb[:, D_KV:], bias_ref, valid, oatt_ref)
        pieces += [_proj_piece(xn_b, w_in_ref, k) for k in range(len(pieces), (c + 1) * PROJ_PIECES // n_chunks)]

    _sgu_branch(pieces, osgu_ref, sguw_ref, sgub_ref, lng_ref, lnb_ref, None)
    _layer_tail(x, pieces, oatt_ref, osgu_ref, woa_ref, wos_ref, wout_ref, g2_ref, wr_ref, br_ref,
                tri_ref, low_ref, x1_ref, h_ref, route_ref, routet_ref, cnt_ref)


def _sample_layer_kernel(x1_in, h_in, route_in, routet_in, cnt_in,
                         x_ref, ck_ref, cv_ref, g1_ref, wq_ref, w_in_ref, bias_ref, sguw_ref, sgub_ref,
                         lng_ref, lnb_ref, woa_ref, wos_ref, wout_ref, g2_ref, wr_ref, br_ref, tri_ref,
                         low_ref,
                         x1_ref, h_ref, route_ref, routet_ref, cnt_ref, kvnew_ref, vn_ref,
                         oatt_ref, osgu_ref):
    del x1_in, h_in, route_in, routet_in, cnt_in
    T = LAYER_TILE
    nq = x_ref.shape[0] // ck_ref.shape[0]
    x = x_ref[...]
    xn_b = _rms_norm(x, g1_ref[...]).astype(BF16)
    zq = jnp.dot(xn_b, wq_ref[...], preferred_element_type=F32).astype(BF16)
    zkv = jnp.dot(xn_b, w_in_ref[:, COL_KV:COL_U], preferred_element_type=F32)
    kvnew_ref[...] = zkv
    zkv_b = zkv.astype(BF16)
    n_cache = ck_ref.shape[1]
    key_pad = jnp.zeros((KEY_PAD - n_cache - nq, D_KV), BF16)
    n_seqs = T // nq
    pieces = []
    for b in range(n_seqs):
        r0 = b * nq
        kk = jnp.concatenate([ck_ref[b].astype(BF16), zkv_b[r0:r0 + nq, :D_KV], key_pad], axis=0)
        vv = jnp.concatenate([cv_ref[b].astype(BF16), zkv_b[r0:r0 + nq, D_KV:], key_pad], axis=0)
        _attention_rows(zq, r0, nq, kk, vv, bias_ref, None, oatt_ref)
        pieces += [_proj_piece(xn_b, w_in_ref, k) for k in range(len(pieces), (b + 1) * PROJ_PIECES // n_seqs)]

    _sgu_branch(pieces, osgu_ref, sguw_ref, sgub_ref, lng_ref, lnb_ref, vn_ref)
    _layer_tail(x, pieces, oatt_ref, osgu_ref, woa_ref, wos_ref, wout_ref, g2_ref, wr_ref, br_ref,
                tri_ref, low_ref, x1_ref, h_ref, route_ref, routet_ref, cnt_ref)


def _pack_slot_rows(x):
    lo = lax.bitcast_convert_type(x[:, :SLOT_WORDS].astype(BF16).astype(F32), U32)
    hi = lax.bitcast_convert_type(x[:, SLOT_WORDS:].astype(BF16).astype(F32), U32)
    return (lo >> 16) | hi


def _pack_slot_rows_exact(x):
    lo = lax.bitcast_convert_type(x[:, :SLOT_WORDS], U32)
    hi = lax.bitcast_convert_type(x[:, SLOT_WORDS:], U32)
    return (lo >> 16) | hi


def _unpack_slot_rows(w):
    lo = lax.bitcast_convert_type(w << 16, F32)
    hi = lax.bitcast_convert_type(w & jnp.uint32(0xFFFF0000), F32)
    return jnp.concatenate([lo, hi], axis=1).astype(BF16)


def _as_granules(x):
    return x.reshape(x.shape[0] // GRAN, GRAN, SLOT_WORDS)


def _as_rows(x):
    return x.reshape(x.shape[0] * GRAN, SLOT_WORDS)


def _segment_copies(n_gran, src_gran, dst_gran, bits, make_copy, act):
    for b in range(min(bits, 2)):
        @pl.when(((n_gran >> b) & 1) == 1)
        def _(b=b):
            done = n_gran & ((1 << b) - 1)
            act(make_copy(src_gran + done, dst_gran + done, 1 << b))

    if bits > 2:
        def quad(q, carry):
            done = (n_gran & 3) + 4 * q
            act(make_copy(src_gran + done, dst_gran + done, 4))
            return carry
        lax.fori_loop(0, n_gran >> 2, quad, 0)


def _start_tile_copies(tile, live, cnt_ref, loff_ref, goff_ref, big_ref, make_copy):
    for e in range(N_EXPERTS):
        t = tile * N_EXPERTS + e
        n = jnp.where(live, cnt_ref[t], 0)
        src, dst = loff_ref[t], goff_ref[t]
        for b in range(2):
            @pl.when(((n >> b) & 1) == 1)
            def _(b=b, n=n, src=src, dst=dst):
                done = n & ((1 << b) - 1)
                make_copy(src + done, dst + done, 1 << b).start()
        for q in range(FLAT_QUADS):
            @pl.when((n >> 2) > q)
            def _(q=q, n=n, src=src, dst=dst):
                done = (n & 3) + 4 * q
                make_copy(src + done, dst + done, 4).start()

    @pl.when(live & (big_ref[tile] > 0))
    def _():
        def rest(e, carry):
            t = tile * N_EXPERTS + e
            n, src, dst = cnt_ref[t], loff_ref[t], goff_ref[t]

            def quad(q, c):
                done = (n & 3) + 4 * q
                make_copy(src + done, dst + done, 4).start()
                return c
            lax.fori_loop(FLAT_QUADS, n >> 2, quad, 0)
            return carry
        lax.fori_loop(0, N_EXPERTS, rest, 0)


def _wait_granules(n_gran, bits, make_copy):
    for b in range(bits):
        @pl.when(((n_gran >> b) & 1) == 1)
        def _(b=b):
            make_copy(1 << b).wait()


def _dispatch_kernel(cnt_ref, loff_ref, goff_ref, tot_ref, big_ref, padn_ref, padoff_ref,
                     route_ref, h_ref, xs_hbm, stage_ref, zero_ref, sem):
    i = pl.program_id(0)
    last = pl.num_programs(0) - 1
    slot = lax.rem(i, 2)
    T = TOK_TILE

    def seg_copy(tile_slot):
        return lambda s, d, n: pltpu.make_async_copy(
            stage_ref.at[tile_slot, pl.ds(s, n)], xs_hbm.at[pl.ds(d, n)], sem.at[tile_slot])

    def wait_tile(tile, tile_slot):
        _wait_granules(tot_ref[tile], TILE_BITS, lambda n: pltpu.make_async_copy(
            stage_ref.at[tile_slot, pl.ds(0, n)], xs_hbm.at[pl.ds(0, n)], sem.at[tile_slot]))

    @pl.when(i > 1)
    def _():
        wait_tile(i - 2, slot)

    _start_tile_copies(jnp.maximum(i - 1, 0), i > 0, cnt_ref, loff_ref, goff_ref, big_ref, seg_copy(1 - slot))

    pos = route_ref[0, TOP_K:2 * TOP_K, :].astype(I32)
    r_iota = lax.broadcasted_iota(I32, (LOCAL_ROWS, T), 0)
    p = jnp.zeros((LOCAL_ROWS, T), F32)
    for k in range(TOP_K):
        p = jnp.where(r_iota == pos[k:k + 1, :], 1.0, p)
    stage_ref[slot] = _as_granules(
        _pack_slot_rows_exact(jnp.dot(p.astype(BF16), h_ref[...], preferred_element_type=F32)))

    @pl.when(i == last)
    def _():
        _start_tile_copies(i, i >= 0, cnt_ref, loff_ref, goff_ref, big_ref, seg_copy(slot))

        @pl.when(i > 0)
        def _():
            wait_tile(i - 1, 1 - slot)

        wait_tile(i, slot)
        zero_ref[...] = jnp.zeros(zero_ref.shape, U32)

        def pad_copy(s, d, n):
            return pltpu.make_async_copy(zero_ref.at[pl.ds(s, n)], xs_hbm.at[pl.ds(d, n)], sem.at[2])

        def for_pads(act):
            def body(e, carry):
                _segment_copies(padn_ref[e], 0, padoff_ref[e], PAD_BITS, pad_copy, act)
                return carry
            lax.fori_loop(0, N_EXPERTS, body, 0)

        for_pads(lambda cp: cp.start())
        for_pads(lambda cp: cp.wait())


def _expert_kernel(gran0_ref, nsub_ref, xs_hbm, w1_ref, b1_ref, w2_ref, b2_ref, ys_hbm,
                   w1b_ref, w2b_ref, xbuf_ref, ybuf_ref, sem_in, sem_out):
    e = pl.program_id(0)
    subs = ROW_BLOCK // SUB_BLOCK
    sub_gran = SUB_BLOCK // GRAN
    blk_gran = ROW_BLOCK // GRAN
    gran0 = gran0_ref[e]
    nsub = nsub_ref[e]
    nb_full = nsub // subs
    rem = nsub - nb_full * subs
    nb = nb_full + jnp.where(rem > 0, 1, 0)
    last_slot = lax.rem(nb + 1, 2)

    def in_copy(k, slot):
        return pltpu.make_async_copy(xs_hbm.at[pl.ds(gran0 + k * blk_gran, blk_gran)], xbuf_ref.at[slot],
                                     sem_in.at[slot])

    def out_copy(k, slot, m_sub):
        n = m_sub * sub_gran
        return pltpu.make_async_copy(ybuf_ref.at[slot, pl.ds(0, n)], ys_hbm.at[pl.ds(gran0 + k * blk_gran, n)],
                                     sem_out.at[slot])

    @pl.when(nb > 0)
    def _():
        in_copy(0, 0).start()

    w1b_ref[...] = w1_ref[0].astype(BF16)
    w2b_ref[...] = w2_ref[0].astype(BF16)

    def process(k, m_sub):
        n = m_sub * sub_gran
        slot = lax.rem(k, 2)

        @pl.when(k + 1 < nb)
        def _():
            in_copy(k + 1, 1 - slot).start()

        in_copy(k, slot).wait()

        @pl.when(k >= 2)
        def _():
            out_copy(k - 2, slot, subs).wait()

        x = _unpack_slot_rows(_as_rows(xbuf_ref[slot, 0:n]))
        h1 = jnp.dot(x, w1b_ref[...], preferred_element_type=F32) + b1_ref[0]
        gate = jnp.minimum(h1[:, :D_FF], SWIGLU_LIMIT)
        up = jnp.clip(h1[:, D_FF:], -SWIGLU_LIMIT, SWIGLU_LIMIT)
        act = gate * jax.nn.sigmoid(SWIGLU_ALPHA * gate) * (up + 1.0)
        y = jnp.dot(act.astype(BF16), w2b_ref[...], preferred_element_type=F32) + b2_ref[0]
        ybuf_ref[slot, 0:n] = _as_granules(_pack_slot_rows(y))
        out_copy(k, slot, m_sub).start()

    def full_block(k, carry):
        process(k, subs)
        return carry

    lax.fori_loop(0, nb_full, full_block, 0)

    for m_sub in range(1, subs):
        @pl.when(rem == m_sub)
        def _(m_sub=m_sub):
            process(nb_full, m_sub)

    @pl.when(nb >= 2)
    def _():
        out_copy(nb - 2, 1 - last_slot, subs).wait()

    @pl.when((nb >= 1) & (rem == 0))
    def _():
        out_copy(nb - 1, last_slot, subs).wait()

    for m_sub in range(1, subs):
        @pl.when(rem == m_sub)
        def _(m_sub=m_sub):
            out_copy(nb - 1, last_slot, m_sub).wait()


def _combine_kernel(n_prompt_tiles, cnt_ref, loff_ref, goff_ref, tot_ref, big_ref,
                    routet_ref, x1_ref, gf_ref, ys_hbm, yp_ref, ysm_ref, stage_ref, sem):
    i = pl.program_id(0)
    n_steps = pl.num_programs(0)
    slot = lax.rem(i, 2)
    T = TOK_TILE

    def seg_copy(tile_slot):
        return lambda s, d, n: pltpu.make_async_copy(
            ys_hbm.at[pl.ds(d, n)], stage_ref.at[tile_slot, pl.ds(s, n)], sem.at[tile_slot])

    @pl.when(i == 0)
    def _():
        stage_ref[...] = jnp.zeros(stage_ref.shape, U32)
        _start_tile_copies(i, i >= 0, cnt_ref, loff_ref, goff_ref, big_ref, seg_copy(slot))

    rt = routet_ref[...]
    l_iota = lax.broadcasted_iota(I32, (T, LOCAL_ROWS), 1)
    pw = jnp.zeros((T, LOCAL_ROWS), F32)
    for k in range(TOP_K):
        pos_k = rt[:, TOP_K + k:TOP_K + k + 1].astype(I32)
        pw = jnp.where(l_iota == pos_k, rt[:, 2 * TOP_K + k:2 * TOP_K + k + 1], pw)
    pw = pw.astype(BF16)

    _start_tile_copies(jnp.minimum(i + 1, n_steps - 1), i + 1 < n_steps, cnt_ref, loff_ref, goff_ref, big_ref,
                       seg_copy(1 - slot))

    _wait_granules(tot_ref[i], TILE_BITS, lambda n: pltpu.make_async_copy(
        ys_hbm.at[pl.ds(0, n)], stage_ref.at[slot, pl.ds(0, n)], sem.at[slot]))
    moe = jnp.dot(pw, _unpack_slot_rows(_as_rows(stage_ref[slot])), preferred_element_type=F32)
    y = _rms_norm(x1_ref[...] + moe, gf_ref[...])

    @pl.when(i < n_prompt_tiles)
    def _():
        yp_ref[...] = y

    @pl.when(i >= n_prompt_tiles)
    def _():
        ysm_ref[...] = y


def _t5_bucket(rel):
    half = NUM_BUCKETS // 2
    max_exact = half // 2
    ret = jnp.where(rel > 0, half, 0)
    n = jnp.abs(rel)
    nf = jnp.maximum(n, 1).astype(F32)
    large = max_exact + (jnp.log(nf / max_exact) / math.log(MAX_DISTANCE / max_exact)
                         * (half - max_exact)).astype(jnp.int32)
    large = jnp.minimum(large, half - 1)
    return ret + jnp.where(n < max_exact, n, large)


def _stacked_bias(table, sinks, q_pos, k_pos):
    nq, nk = q_pos.shape[0], k_pos.shape[0]
    bucket = _t5_bucket(k_pos[None, :] - q_pos[:, None])
    onehot = (bucket[:, :, None] == jnp.arange(NUM_BUCKETS)).astype(F32)
    bias = jnp.einsum('qkb,bh->hqk', onehot, table.astype(F32), precision=lax.Precision.HIGHEST)
    bias = bias.reshape(N_KV, REP * nq, nk)
    sink = jnp.repeat(sinks.astype(F32).reshape(N_KV, REP, 1), nq, axis=2).reshape(N_KV, REP * nq, 1)
    tail = jnp.full((N_KV, REP * nq, KEY_PAD - nk - 1), -jnp.inf, F32)
    return jnp.concatenate([bias, sink, tail], axis=-1)


def kernel(x_prompt, x_sample, cache_win_k, cache_win_v, norm1_g, w_in, attn_sinks, rel_bias_table, sgu_ln_g, sgu_ln_b, sgu_w, sgu_b, w_o_att, w_o_sgu, w_out, norm2_g, w_router, b_router, w_exp_in, b_exp_in, w_exp_out, b_exp_out, final_norm_g):
    batch, seq, _ = x_prompt.shape
    dec_batch, dec_seq, _ = x_sample.shape
    cache_rows = cache_win_k.shape[2]
    assert x_prompt.shape[2] == D_MODEL and w_in.shape == (1, D_MODEL, D_IN)
    assert seq % LAYER_TILE == 0 and LAYER_TILE % TOK_TILE == 0 and TOK_TILE % GMLP_CHUNK == 0
    assert LAYER_TILE >= WINDOW and LAYER_TILE % dec_seq == 0 and (dec_batch * dec_seq) % LAYER_TILE == 0
    assert dec_seq <= GMLP_CHUNK and GMLP_CHUNK % dec_seq == 0 and cache_rows == WINDOW
    T = TOK_TILE
    n_prompt = batch * seq
    n_sample = dec_batch * dec_seq
    n_tok = n_prompt + n_sample
    LT = LAYER_TILE
    sorts_per_step = LT // T
    tiles_per_seq = seq // LT
    n_psteps = n_prompt // LT
    n_ssteps = n_sample // LT
    n_ptiles = n_prompt // T
    n_tiles = n_tok // T
    seqs_per_tile = LT // dec_seq

    w_in_b = w_in[0].astype(BF16)
    w_q_b = w_in_b[:, :D_ATT].reshape(D_MODEL, N_KV, REP, HEAD_DIM).transpose(0, 2, 1, 3).reshape(
        D_MODEL, D_ATT)
    woa_b = w_o_att[0].reshape(N_KV, REP, HEAD_DIM, D_MODEL).transpose(1, 0, 2, 3).reshape(
        D_ATT, D_MODEL).astype(BF16)
    wos_b = w_o_sgu[0].astype(BF16)
    wout_b = w_out[0].astype(BF16)
    wr_b = jnp.transpose(w_router[0]).astype(BF16)
    br_col = b_router[0].astype(F32).reshape(N_EXPERTS, 1)
    g1 = norm1_g[0].reshape(1, D_MODEL)
    g2 = norm2_g[0].reshape(1, D_MODEL)
    gf = final_norm_g.reshape(1, D_MODEL)
    lng = sgu_ln_g[0].reshape(1, D_GMLP)
    lnb = sgu_ln_b[0].reshape(1, D_GMLP)
    tril = jnp.tril(jnp.ones((GMLP_CHUNK, GMLP_CHUNK), dtype=bool))
    sguw_p = jnp.where(tril[None], sgu_w[0], 0).astype(BF16)
    sgub_p = jnp.broadcast_to(sgu_b[0][:, :, None], (N_GROUPS, GMLP_CHUNK, GROUP_W)).astype(F32)
    reps = GMLP_CHUNK // dec_seq
    corner = jnp.where(tril[None, :dec_seq, :dec_seq], sgu_w[0][:, :dec_seq, :dec_seq], 0)
    sguw_s = jnp.einsum('ab,gij->gaibj', jnp.eye(reps, dtype=F32), corner).reshape(
        N_GROUPS, GMLP_CHUNK, GMLP_CHUNK).astype(BF16)
    sgub_s = jnp.broadcast_to(jnp.tile(sgu_b[0][:, :dec_seq], (1, reps))[:, :, None],
                              (N_GROUPS, GMLP_CHUNK, GROUP_W)).astype(F32)
    bias_p = _stacked_bias(rel_bias_table, attn_sinks[0], jnp.arange(CHUNK) + WINDOW, jnp.arange(BAND))
    bias_s = _stacked_bias(rel_bias_table, attn_sinks[0], cache_rows + jnp.arange(dec_seq),
                           jnp.arange(cache_rows + dec_seq))
    tri = jnp.triu(jnp.ones((T, T), F32), k=1).astype(BF16)
    low = jnp.tril(jnp.ones((N_EXPERTS, N_EXPERTS), F32), k=-1).astype(BF16)

    layer_out_shapes = (
        jax.ShapeDtypeStruct((n_tok, D_MODEL), F32),
        jax.ShapeDtypeStruct((n_tok, D_MODEL), BF16),
        jax.ShapeDtypeStruct((n_tiles, ROUTE_ROWS, T), F32),
        jax.ShapeDtypeStruct((n_tok, LANES), F32),
        jax.ShapeDtypeStruct((n_tiles, N_EXPERTS, LANES), F32),
    )
    shared_consts = (w_q_b, w_in_b)
    tail_consts_p = (sguw_p, sgub_p, lng, lnb, woa_b, wos_b, wout_b, g2, wr_b, br_col, tri, low)
    tail_consts_s = (sguw_s, sgub_s, lng, lnb, woa_b, wos_b, wout_b, g2, wr_b, br_col, tri, low)

    def tile_specs(tile_of):
        return [
            pl.BlockSpec((LT, D_MODEL), lambda *g: (tile_of(*g), 0)),
            pl.BlockSpec((LT, D_MODEL), lambda *g: (tile_of(*g), 0)),
            pl.BlockSpec((sorts_per_step, ROUTE_ROWS, T), lambda *g: (tile_of(*g), 0, 0)),
            pl.BlockSpec((LT, LANES), lambda *g: (tile_of(*g), 0)),
            pl.BlockSpec((sorts_per_step, N_EXPERTS, LANES), lambda *g: (tile_of(*g), 0, 0)),
        ]

    prompt_consts = (g1,) + shared_consts + (bias_p,) + tail_consts_p
    x1, h, route, routet, cnt, kvwin = pl.pallas_call(
        _prompt_layer_kernel,
        grid=(batch, tiles_per_seq),
        in_specs=[pl.BlockSpec((1, LT, D_MODEL), lambda b, j: (b, j, 0))]
                 + [_const_spec(c.shape) for c in prompt_consts],
        out_specs=tile_specs(lambda b, j: b * tiles_per_seq + j)
                  + [pl.BlockSpec((1, WINDOW, 2 * D_KV), lambda b, j: (b, 0, 0))],
        out_shape=layer_out_shapes + (jax.ShapeDtypeStruct((batch, WINDOW, 2 * D_KV), F32),),
        scratch_shapes=[pltpu.VMEM((LT + WINDOW, 2 * D_KV), BF16),
                        pltpu.VMEM((LT, D_ATT), BF16),
                        pltpu.VMEM((LT, D_GMLP), BF16)],
        compiler_params=pltpu.CompilerParams(dimension_semantics=("arbitrary", "arbitrary"),
                                             vmem_limit_bytes=VMEM_LIMIT),
        name="layer_prompt",
    )(x_prompt, *prompt_consts)

    xs_flat = x_sample.reshape(n_sample, D_MODEL)
    ck = cache_win_k[0].reshape(dec_batch, cache_rows, D_KV)
    cv = cache_win_v[0].reshape(dec_batch, cache_rows, D_KV)
    sample_consts = (g1,) + shared_consts + (bias_s,) + tail_consts_s
    any_spec = pl.BlockSpec(memory_space=pl.ANY)
    x1, h, route, routet, cnt, kvnew, vn_s = pl.pallas_call(
        _sample_layer_kernel,
        grid=(n_ssteps,),
        in_specs=[any_spec] * 5
                 + [pl.BlockSpec((LT, D_MODEL), lambda i: (i, 0)),
                    pl.BlockSpec((seqs_per_tile, cache_rows, D_KV), lambda i: (i, 0, 0)),
                    pl.BlockSpec((seqs_per_tile, cache_rows, D_KV), lambda i: (i, 0, 0))]
                 + [_const_spec(c.shape) for c in sample_consts],
        out_specs=tile_specs(lambda i: n_psteps + i)
                  + [pl.BlockSpec((LT, 2 * D_KV), lambda i: (i, 0)),
                     pl.BlockSpec((LT, D_GMLP), lambda i: (i, 0))],
        out_shape=layer_out_shapes + (jax.ShapeDtypeStruct((n_sample, 2 * D_KV), F32),
                                      jax.ShapeDtypeStruct((n_sample, D_GMLP), F32)),
        scratch_shapes=[pltpu.VMEM((LT, D_ATT), BF16), pltpu.VMEM((LT, D_GMLP), BF16)],
        input_output_aliases={0: 0, 1: 1, 2: 2, 3: 3, 4: 4},
        compiler_params=pltpu.CompilerParams(dimension_semantics=("arbitrary",),
                                             vmem_limit_bytes=VMEM_LIMIT),
        name="layer_sample",
    )(x1, h, route, routet, cnt, xs_flat, ck, cv, *sample_consts)

    blk_gran = SUB_BLOCK // GRAN
    counts =cnt[:, :, 0].astype(I32)
    seg_gran = (counts + (GRAN - 1)) // GRAN
    local_off = jnp.cumsum(seg_gran, axis=1) - seg_gran
    tot_gran = jnp.sum(seg_gran, axis=0)
    ptot_gran = (tot_gran + (blk_gran - 1)) // blk_gran * blk_gran
    pend_gran = jnp.cumsum(ptot_gran)
    gstart = pend_gran - ptot_gran
    global_off = gstart[None, :] + jnp.cumsum(seg_gran, axis=0) - seg_gran
    pad_n = ptot_gran - tot_gran
    pad_off = gstart + tot_gran
    n_rows = -(-(TOP_K * n_tok + n_tiles * N_EXPERTS * (GRAN - 1) + N_EXPERTS * (SUB_BLOCK - GRAN))
               // SUB_BLOCK) * SUB_BLOCK + ROW_BLOCK
    region_gran0 = gstart.astype(I32)
    region_subs = (ptot_gran // blk_gran).astype(I32)
    tile_gran = jnp.sum(seg_gran, axis=1).astype(I32)
    tile_big = jnp.any(seg_gran >= 4 * (FLAT_QUADS + 1), axis=1).astype(I32)
    seg_gran_f = seg_gran.reshape(-1).astype(I32)
    local_off_f = local_off.reshape(-1).astype(I32)
    global_off_f = global_off.reshape(-1).astype(I32)

    xs = pl.pallas_call(
        _dispatch_kernel,
        grid_spec=pltpu.PrefetchScalarGridSpec(
            num_scalar_prefetch=7,
            grid=(n_tiles,),
            in_specs=[pl.BlockSpec((1, ROUTE_ROWS, T), lambda i, *_: (i, 0, 0)),
                      pl.BlockSpec((T, D_MODEL), lambda i, *_: (i, 0))],
            out_specs=pl.BlockSpec(memory_space=pl.ANY),
            scratch_shapes=[pltpu.VMEM((2, LOCAL_ROWS // GRAN, GRAN, SLOT_WORDS), U32),
                            pltpu.VMEM((SUB_BLOCK // GRAN, GRAN, SLOT_WORDS), U32),
                            pltpu.SemaphoreType.DMA((3,))]),
        out_shape=jax.ShapeDtypeStruct((n_rows // GRAN, GRAN, SLOT_WORDS), U32),
        compiler_params=pltpu.CompilerParams(dimension_semantics=("arbitrary",),
                                             vmem_limit_bytes=VMEM_LIMIT),
        name="moe_dispatch",
    )(seg_gran_f, local_off_f, global_off_f, tile_gran, tile_big, pad_n.astype(I32), pad_off.astype(I32),
      route, h)

    ys = pl.pallas_call(
        _expert_kernel,
        grid_spec=pltpu.PrefetchScalarGridSpec(
            num_scalar_prefetch=2,
            grid=(N_EXPERTS,),
            in_specs=[pl.BlockSpec(memory_space=pl.ANY),
                      pl.BlockSpec((1, D_MODEL, 2 * D_FF), lambda e, *_: (e, 0, 0)),
                      pl.BlockSpec((1, 1, 2 * D_FF), lambda e, *_: (e, 0, 0)),
                      pl.BlockSpec((1, D_FF, D_MODEL), lambda e, *_: (e, 0, 0)),
                      pl.BlockSpec((1, 1, D_MODEL), lambda e, *_: (e, 0, 0))],
            out_specs=pl.BlockSpec(memory_space=pl.ANY),
            scratch_shapes=[pltpu.VMEM((D_MODEL, 2 * D_FF), BF16), pltpu.VMEM((D_FF, D_MODEL), BF16),
                            pltpu.VMEM((2, ROW_BLOCK // GRAN, GRAN, SLOT_WORDS), U32),
                            pltpu.VMEM((2, ROW_BLOCK // GRAN, GRAN, SLOT_WORDS), U32),
                            pltpu.SemaphoreType.DMA((2,)), pltpu.SemaphoreType.DMA((2,))]),
        out_shape=jax.ShapeDtypeStruct((n_rows // GRAN, GRAN, SLOT_WORDS), U32),
        compiler_params=pltpu.CompilerParams(dimension_semantics=("arbitrary",),
                                             vmem_limit_bytes=VMEM_LIMIT),
        name="moe_experts",
    )(region_gran0, region_subs, xs, w_exp_in[0], b_exp_in[0].reshape(N_EXPERTS, 1, 2 * D_FF),
      w_exp_out[0], b_exp_out[0].reshape(N_EXPERTS, 1, D_MODEL))

    y_p, y_s = pl.pallas_call(
        functools.partial(_combine_kernel, n_ptiles),
        grid_spec=pltpu.PrefetchScalarGridSpec(
            num_scalar_prefetch=5,
            grid=(n_tiles,),
            in_specs=[pl.BlockSpec((T, LANES), lambda i, *_: (i, 0)),
                      pl.BlockSpec((T, D_MODEL), lambda i, *_: (i, 0)),
                      pl.BlockSpec((1, D_MODEL), lambda i, *_: (0, 0)),
                      pl.BlockSpec(memory_space=pl.ANY)],
            out_specs=[pl.BlockSpec((T, D_MODEL), lambda i, *_: (jnp.minimum(i, n_ptiles - 1), 0)),
                       pl.BlockSpec((T, D_MODEL), lambda i, *_: (jnp.maximum(i - n_ptiles, 0), 0))],
            scratch_shapes=[pltpu.VMEM((2, LOCAL_ROWS // GRAN, GRAN, SLOT_WORDS), U32), pltpu.SemaphoreType.DMA((2,))]),
        out_shape=(jax.ShapeDtypeStruct((n_prompt, D_MODEL), F32),
                   jax.ShapeDtypeStruct((n_sample, D_MODEL), F32)),
        compiler_params=pltpu.CompilerParams(dimension_semantics=("arbitrary",),
                                             vmem_limit_bytes=VMEM_LIMIT),
        name="moe_combine",
    )(seg_gran_f, local_off_f, global_off_f, tile_gran, tile_big, routet, x1, gf, ys)

    y_prompt = y_p.reshape(batch, seq, D_MODEL)
    y_sample = y_s.reshape(dec_batch, dec_seq, D_MODEL)
    new_win_k_prompt = kvwin[:, :, :D_KV].reshape(1, batch, WINDOW, N_KV, HEAD_DIM)
    new_win_v_prompt = kvwin[:, :, D_KV:].reshape(1, batch, WINDOW, N_KV, HEAD_DIM)
    new_win_k_sample = kvnew[:, :D_KV].reshape(1, dec_batch, dec_seq, N_KV, HEAD_DIM)
    new_win_v_sample = kvnew[:, D_KV:].reshape(1, dec_batch, dec_seq, N_KV, HEAD_DIM)
    new_sgu_v_sample = vn_s.reshape(1, dec_batch, dec_seq, D_GMLP)
    return (y_prompt, y_sample, new_win_k_prompt, new_win_v_prompt, new_win_k_sample,
            new_win_v_sample, new_sgu_v_sample)
```

```python
import functools
import math

import numpy as np
import jax
import jax.numpy as jnp
from jax import lax
from jax.experimental import pallas as pl
from jax.experimental.pallas import tpu as pltpu

F32 = jnp.float32
BF16 = jnp.bfloat16
I32 = jnp.int32
U32 = jnp.uint32

D_MODEL = 1024
HEAD_DIM = 64
N_HEADS = 16
N_KV = 2
REP = N_HEADS // N_KV
CHUNK = 64
WINDOW = 128
BAND = WINDOW + CHUNK
KEY_PAD = 256
D_ATT = N_HEADS * HEAD_DIM
D_KV = N_KV * HEAD_DIM
NUM_BUCKETS = 32
MAX_DISTANCE = 128
GMLP_CHUNK = 128
D_GMLP = 1024
N_GROUPS = 4
GROUP_W = D_GMLP // N_GROUPS
N_EXPERTS = 32
TOP_K = 4
D_FF = 1024
SWIGLU_LIMIT = 7.0
SWIGLU_ALPHA = 1.702
NORM_EPS = 1e-5
D_IN = D_ATT + 2 * D_KV + 2 * D_GMLP + 2 * D_MODEL
COL_KV = D_ATT
COL_U = COL_KV + 2 * D_KV
COL_VG = COL_U + D_GMLP
COL_GA = COL_VG + D_GMLP
COL_GB = COL_GA + D_MODEL
SQRT_HALF = float(np.sqrt(0.5))

LANES = 128
WORD_SUBLANES = 8
VMEM_LIMIT = 56 * 1024 * 1024

TOK_TILE = 256
LAYER_TILE = 512
GRAN = WORD_SUBLANES
SLOT_WORDS = D_MODEL // 2
FLAT_QUADS = 1
LOCAL_ROWS = TOP_K * TOK_TILE + N_EXPERTS * GRAN
ROW_BLOCK = 512
SUB_BLOCK = 64
PAD_BITS = (SUB_BLOCK // GRAN - 1).bit_length()
TILE_BITS = (LOCAL_ROWS // GRAN).bit_length()
PROJ_PIECES = 16
ROUTE_ROWS = 16


def _const_spec(shape):
    nd = len(shape)
    return pl.BlockSpec(shape, lambda *_: (0,) * nd, pipeline_mode=pl.Buffered(1))


def _rms_norm(x, g):
    ms = jnp.mean(x * x, axis=-1, keepdims=True)
    return x * lax.rsqrt(ms + NORM_EPS) * g


def _gelu(x):
    return 0.5 * x * (1.0 + lax.erf(x * SQRT_HALF))


def _attend(q, k, v, bias, valid):
    logits = lax.dot_general(q, k, (((1,), (1,)), ((), ())), preferred_element_type=F32) + bias
    if valid is not None:
        logits = jnp.where(valid, logits, -jnp.inf)
    m = jnp.max(logits, axis=-1, keepdims=True)
    p = jnp.exp(logits - m)
    den = jnp.sum(p, axis=-1, keepdims=True)
    w = (p * (1.0 / den)).astype(BF16)
    return jnp.dot(w, v, preferred_element_type=F32)


def _attention_rows(zq, row0, n_rows, k_ext, v_ext, bias_ref, valid, oatt_ref):
    lane = lax.broadcasted_iota(I32, (1, LANES), 1)
    stack = []
    for g in range(N_KV):
        in_group = (lane >= g * HEAD_DIM) & (lane < (g + 1) * HEAD_DIM)
        qmask = jnp.where(in_group, HEAD_DIM ** -0.5, 0.0).astype(BF16)
        stack += [zq[row0:row0 + n_rows, t * LANES:(t + 1) * LANES] * qmask for t in range(REP)]
    bias = bias_ref[...].reshape(N_HEADS * n_rows, KEY_PAD)
    o = _attend(jnp.concatenate(stack, axis=0), k_ext, v_ext, bias, valid)
    for g in range(N_KV):
        for t in range(REP):
            r = (g * REP + t) * n_rows
            c0 = t * LANES + g * HEAD_DIM
            oatt_ref[row0:row0 + n_rows, c0:c0 + HEAD_DIM] = (
                o[r:r + n_rows, g * HEAD_DIM:(g + 1) * HEAD_DIM].astype(BF16))


def _proj_piece(xn_b, w_in_ref, k):
    width = (D_IN - COL_U) // PROJ_PIECES
    c0 = COL_U + k * width
    z = jnp.dot(xn_b, w_in_ref[:, c0:c0 + width], preferred_element_type=F32)
    return _gelu(z) if c0 < COL_GA else jax.nn.sigmoid(z)


def _sgu_branch(pieces, osgu_ref, sguw_ref, sgub_ref, lng_ref, lnb_ref, vn_ref):
    u = jnp.concatenate(pieces[:PROJ_PIECES // 4], axis=1)
    vg = jnp.concatenate(pieces[PROJ_PIECES // 4:PROJ_PIECES // 2], axis=1)
    T = u.shape[0]
    mu = jnp.mean(vg, axis=-1, keepdims=True)
    var = jnp.mean(jnp.square(vg - mu), axis=-1, keepdims=True)
    vn = (vg - mu) * lax.rsqrt(var + NORM_EPS) * lng_ref[...] + lnb_ref[...]
    if vn_ref is not None:
        vn_ref[...] = vn
    vn_b = vn.astype(BF16)
    for j in range(T // GMLP_CHUNK):
        rows = slice(j * GMLP_CHUNK, (j + 1) * GMLP_CHUNK)
        for g in range(N_GROUPS):
            cols = slice(g * GROUP_W, (g + 1) * GROUP_W)
            mixed = jnp.dot(sguw_ref[g], vn_b[rows, cols], preferred_element_type=F32) + sgub_ref[g]
            osgu_ref[rows, cols] = (u[rows, cols] * mixed).astype(BF16)


def _layer_tail(x, pieces, oatt_ref, osgu_ref, woa_ref, wos_ref, wout_ref, g2_ref, wr_ref, br_ref,
                tri_ref, low_ref, x1_ref, h_ref, route_ref, routet_ref, cnt_ref):
    T = x.shape[0]
    ga = jnp.concatenate(pieces[PROJ_PIECES // 2:3 * PROJ_PIECES // 4], axis=1)
    gb = jnp.concatenate(pieces[3 * PROJ_PIECES // 4:], axis=1)
    merged = ga * jnp.dot(oatt_ref[...], woa_ref[...], preferred_element_type=F32)
    merged = merged + gb * jnp.dot(osgu_ref[...], wos_ref[...], preferred_element_type=F32)
    x1 = x + jnp.dot(merged.astype(BF16), wout_ref[...], preferred_element_type=F32)
    x1_ref[...] = x1

    h_b = _rms_norm(x1, g2_ref[...]).astype(BF16)
    h_ref[...] = h_b
    lt = lax.dot_general(wr_ref[...], h_b, (((1,), (1,)), ((), ())),
                         preferred_element_type=F32) + br_ref[...]
    for s in range(T // TOK_TILE):
        rec, cnt_b = _route_sort_tile(lt[:, s * TOK_TILE:(s + 1) * TOK_TILE], tri_ref, low_ref)
        route_ref[s] = rec
        rec_pad = jnp.concatenate([rec, jnp.zeros((LANES - ROUTE_ROWS, TOK_TILE), F32)], axis=0)
        routet_ref[s * TOK_TILE:(s + 1) * TOK_TILE, :] = jnp.transpose(rec_pad)
        cnt_ref[s] = cnt_b


def _route_sort_tile(lt, tri_ref, low_ref):
    T = lt.shape[1]
    e_iota = lax.broadcasted_iota(I32, (N_EXPERTS, T), 0).astype(F32)
    cur = lt
    vals, idxs = [], []
    for _ in range(TOP_K):
        m = jnp.max(cur, axis=0, keepdims=True)
        ik = jnp.min(jnp.where(cur == m, e_iota, float(N_EXPERTS)), axis=0, keepdims=True)
        vals.append(m)
        idxs.append(ik)
        cur = jnp.where(e_iota == ik, -jnp.inf, cur)
    exps = [jnp.exp(v - vals[0]) for v in vals]
    den = exps[0] + exps[1] + exps[2] + exps[3]
    gates = [e / den for e in exps]

    onehot = jnp.zeros((N_EXPERTS, T), F32)
    for ik in idxs:
        onehot = onehot + jnp.where(e_iota == ik, 1.0, 0.0)
    rank = jnp.dot(onehot.astype(BF16), tri_ref[...], preferred_element_type=F32)
    cnt = jnp.sum(onehot, axis=1, keepdims=True)
    cnt_b = jnp.broadcast_to(cnt, (N_EXPERTS, LANES))
    gran = jnp.floor((cnt_b + (GRAN - 1)) * (1.0 / GRAN))
    off = jnp.dot(low_ref[...], gran.astype(BF16), preferred_element_type=F32) * GRAN
    base = off[:, 0:1] + rank
    poss = [jnp.sum(jnp.where(e_iota == ik, base, 0.0), axis=0, keepdims=True) for ik in idxs]

    rec = jnp.concatenate(idxs + poss + gates
                          + [jnp.zeros((ROUTE_ROWS - 3 * TOP_K, T), F32)], axis=0)
    return rec, cnt_b


def _prompt_layer_kernel(x_ref, g1_ref, wq_ref, w_in_ref, bias_ref, sguw_ref, sgub_ref, lng_ref,
                         lnb_ref, woa_ref, wos_ref, wout_ref, g2_ref, wr_ref, br_ref, tri_ref, low_ref,
                         x1_ref, h_ref, route_ref, routet_ref, cnt_ref, kvwin_ref,
                         kvx_ref, oatt_ref, osgu_ref):
    T = LAYER_TILE
    j = pl.program_id(1)
    @pl.when(j == 0)
    def _():
        kvx_ref[0:WINDOW, :] = jnp.zeros((WINDOW, 2 * D_KV), BF16)

    @pl.when(j > 0)
    def _():
        kvx_ref[0:WINDOW, :] = kvx_ref[T:T + WINDOW, :]

    x = x_ref[0]
    xn_b = _rms_norm(x, g1_ref[...]).astype(BF16)
    zq = jnp.dot(xn_b, wq_ref[...], preferred_element_type=F32).astype(BF16)
    zkv = jnp.dot(xn_b, w_in_ref[:, COL_KV:COL_U], preferred_element_type=F32)
    kvwin_ref[0] = zkv[T - WINDOW:, :]
    kvx_ref[WINDOW:, :] = zkv.astype(BF16)

    col = lax.broadcasted_iota(I32, (1, KEY_PAD), 1)
    key_pad = jnp.zeros((KEY_PAD - BAND, 2 * D_KV), BF16)
    n_chunks = T // CHUNK
    pieces = []
    for c in range(n_chunks):
        r0 = c * CHUNK
        kvb = jnp.concatenate([kvx_ref[r0:r0 + BAND, :], key_pad], axis=0)
        valid = ((col + (j * T + r0 - WINDOW)) >= 0) | (col >= BAND) if r0 < WINDOW else None
        _attention_rows(zq, r0, CHUNK, kvb[:, :D_KV], kvb[:, D_KV:], bias_ref, valid, oatt_ref)
        pieces += [_proj_piece(xn_b, w_in_ref, k) for k in range(len(pieces), (c + 1) * PROJ_PIECES // n_chunks)]

    _sgu_branch(pieces, osgu_ref, sguw_ref, sgub_ref, lng_ref, lnb_ref, None)
    _layer_tail(x, pieces, oatt_ref, osgu_ref, woa_ref, wos_ref, wout_ref, g2_ref, wr_ref, br_ref,
                tri_ref, low_ref, x1_ref, h_ref, route_ref, routet_ref, cnt_ref)


def _sample_layer_kernel(x1_in, h_in, route_in, routet_in, cnt_in,
                         x_ref, ck_ref, cv_ref, g1_ref, wq_ref, w_in_ref, bias_ref, sguw_ref, sgub_ref,
                         lng_ref, lnb_ref, woa_ref, wos_ref, wout_ref, g2_ref, wr_ref, br_ref, tri_ref,
                         low_ref,
                         x1_ref, h_ref, route_ref, routet_ref, cnt_ref, kvnew_ref, vn_ref,
                         oatt_ref, osgu_ref):
    del x1_in, h_in, route_in, routet_in, cnt_in
    T = LAYER_TILE
    nq = x_ref.shape[0] // ck_ref.shape[0]
    x = x_ref[...]
    xn_b = _rms_norm(x, g1_ref[...]).astype(BF16)
    zq = jnp.dot(xn_b, wq_ref[...], preferred_element_type=F32).astype(BF16)
    zkv = jnp.dot(xn_b, w_in_ref[:, COL_KV:COL_U], preferred_element_type=F32)
    kvnew_ref[...] = zkv
    zkv_b = zkv.astype(BF16)
    n_cache = ck_ref.shape[1]
    key_pad = jnp.zeros((KEY_PAD - n_cache - nq, D_KV), BF16)
    n_seqs = T // nq
    pieces = []
    for b in range(n_seqs):
        r0 = b * nq
        kk = jnp.concatenate([ck_ref[b].astype(BF16), zkv_b[r0:r0 + nq, :D_KV], key_pad], axis=0)
        vv = jnp.concatenate([cv_ref[b].astype(BF16), zkv_b[r0:r0 + nq, D_KV:], key_pad], axis=0)
        _attention_rows(zq, r0, nq, kk, vv, bias_ref, None, oatt_ref)
        pieces += [_proj_piece(xn_b, w_in_ref, k) for k in range(len(pieces), (b + 1) * PROJ_PIECES // n_seqs)]

    _sgu_branch(pieces, osgu_ref, sguw_ref, sgub_ref, lng_ref, lnb_ref, vn_ref)
    _layer_tail(x, pieces, oatt_ref, osgu_ref, woa_ref, wos_ref, wout_ref, g2_ref, wr_ref, br_ref,
                tri_ref, low_ref, x1_ref, h_ref, route_ref, routet_ref, cnt_ref)


def _pack_slot_rows(x):
    lo = lax.bitcast_convert_type(x[:, :SLOT_WORDS].astype(BF16).astype(F32), U32)
    hi = lax.bitcast_convert_type(x[:, SLOT_WORDS:].astype(BF16).astype(F32), U32)
    return (lo >> 16) | hi


def _pack_slot_rows_exact(x):
    lo = lax.bitcast_convert_type(x[:, :SLOT_WORDS], U32)
    hi = lax.bitcast_convert_type(x[:, SLOT_WORDS:], U32)
    return (lo >> 16) | hi


def _unpack_slot_rows(w):
    lo = lax.bitcast_convert_type(w << 16, F32)
    hi = lax.bitcast_convert_type(w & jnp.uint32(0xFFFF0000), F32)
    return jnp.concatenate([lo, hi], axis=1).astype(BF16)


def _as_granules(x):
    return x.reshape(x.shape[0] // GRAN, GRAN, SLOT_WORDS)


def _as_rows(x):
    return x.reshape(x.shape[0] * GRAN, SLOT_WORDS)


def _segment_copies(n_gran, src_gran, dst_gran, bits, make_copy, act):
    for b in range(min(bits, 2)):
        @pl.when(((n_gran >> b) & 1) == 1)
        def _(b=b):
            done = n_gran & ((1 << b) - 1)
            act(make_copy(src_gran + done, dst_gran + done, 1 << b))

    if bits > 2:
        def quad(q, carry):
            done = (n_gran & 3) + 4 * q
            act(make_copy(src_gran + done, dst_gran + done, 4))
            return carry
        lax.fori_loop(0, n_gran >> 2, quad, 0)


def _start_tile_copies(tile, live, cnt_ref, loff_ref, goff_ref, big_ref, make_copy):
    for e in range(N_EXPERTS):
        t = tile * N_EXPERTS + e
        n = jnp.where(live, cnt_ref[t], 0)
        src, dst = loff_ref[t], goff_ref[t]
        for b in range(2):
            @pl.when(((n >> b) & 1) == 1)
            def _(b=b, n=n, src=src, dst=dst):
                done = n & ((1 << b) - 1)
                make_copy(src + done, dst + done, 1 << b).start()
        for q in range(FLAT_QUADS):
            @pl.when((n >> 2) > q)
            def _(q=q, n=n, src=src, dst=dst):
                done = (n & 3) + 4 * q
                make_copy(src + done, dst + done, 4).start()

    @pl.when(live & (big_ref[tile] > 0))
    def _():
        def rest(e, carry):
            t = tile * N_EXPERTS + e
            n, src, dst = cnt_ref[t], loff_ref[t], goff_ref[t]

            def quad(q, c):
                done = (n & 3) + 4 * q
                make_copy(src + done, dst + done, 4).start()
                return c
            lax.fori_loop(FLAT_QUADS, n >> 2, quad, 0)
            return carry
        lax.fori_loop(0, N_EXPERTS, rest, 0)


def _wait_granules(n_gran, bits, make_copy):
    for b in range(bits):
        @pl.when(((n_gran >> b) & 1) == 1)
        def _(b=b):
            make_copy(1 << b).wait()


def _dispatch_kernel(cnt_ref, loff_ref, goff_ref, tot_ref, big_ref, padn_ref, padoff_ref,
                     route_ref, h_ref, xs_hbm, stage_ref, zero_ref, sem):
    i = pl.program_id(0)
    last = pl.num_programs(0) - 1
    slot = lax.rem(i, 2)
    T = TOK_TILE

    def seg_copy(tile_slot):
        return lambda s, d, n: pltpu.make_async_copy(
            stage_ref.at[tile_slot, pl.ds(s, n)], xs_hbm.at[pl.ds(d, n)], sem.at[tile_slot])

    def wait_tile(tile, tile_slot):
        _wait_granules(tot_ref[tile], TILE_BITS, lambda n: pltpu.make_async_copy(
            stage_ref.at[tile_slot, pl.ds(0, n)], xs_hbm.at[pl.ds(0, n)], sem.at[tile_slot]))

    @pl.when(i > 1)
    def _():
        wait_tile(i - 2, slot)

    pos = route_ref[0, TOP_K:2 * TOP_K, :].astype(I32)
    r_iota = lax.broadcasted_iota(I32, (LOCAL_ROWS, T), 0)
    p = jnp.zeros((LOCAL_ROWS, T), F32)
    for k in range(TOP_K):
        p = jnp.where(r_iota == pos[k:k + 1, :], 1.0, p)
    stage_ref[slot] = _as_granules(
        _pack_slot_rows_exact(jnp.dot(p.astype(BF16), h_ref[...], preferred_element_type=F32)))

    _start_tile_copies(jnp.maximum(i - 1, 0), i > 0, cnt_ref, loff_ref, goff_ref, big_ref, seg_copy(1 - slot))

    @pl.when(i == last)
    def _():
        _start_tile_copies(i, i >= 0, cnt_ref, loff_ref, goff_ref, big_ref, seg_copy(slot))

        @pl.when(i > 0)
        def _():
            wait_tile(i - 1, 1 - slot)

        wait_tile(i, slot)
        zero_ref[...] = jnp.zeros(zero_ref.shape, U32)

        def pad_copy(s, d, n):
            return pltpu.make_async_copy(zero_ref.at[pl.ds(s, n)], xs_hbm.at[pl.ds(d, n)], sem.at[2])

        def for_pads(act):
            def body(e, carry):
                _segment_copies(padn_ref[e], 0, padoff_ref[e], PAD_BITS, pad_copy, act)
                return carry
            lax.fori_loop(0, N_EXPERTS, body, 0)

        for_pads(lambda cp: cp.start())
        for_pads(lambda cp: cp.wait())


def _expert_kernel(gran0_ref, nsub_ref, xs_hbm, w1_ref, b1_ref, w2_ref, b2_ref, ys_hbm,
                   w1b_ref, w2b_ref, xbuf_ref, ybuf_ref, sem_in, sem_out):
    e = pl.program_id(0)
    subs = ROW_BLOCK // SUB_BLOCK
    sub_gran = SUB_BLOCK // GRAN
    blk_gran = ROW_BLOCK // GRAN
    gran0 = gran0_ref[e]
    nsub = nsub_ref[e]
    nb_full = nsub // subs
    rem = nsub - nb_full * subs
    nb = nb_full + jnp.where(rem > 0, 1, 0)
    last_slot = lax.rem(nb + 1, 2)

    def in_copy(k, slot):
        return pltpu.make_async_copy(xs_hbm.at[pl.ds(gran0 + k * blk_gran, blk_gran)], xbuf_ref.at[slot],
                                     sem_in.at[slot])

    def out_copy(k, slot, m_sub):
        n = m_sub * sub_gran
        return pltpu.make_async_copy(ybuf_ref.at[slot, pl.ds(0, n)], ys_hbm.at[pl.ds(gran0 + k * blk_gran, n)],
                                     sem_out.at[slot])

    @pl.when(nb > 0)
    def _():
        in_copy(0, 0).start()

    w1b_ref[...] = w1_ref[0].astype(BF16)
    w2b_ref[...] = w2_ref[0].astype(BF16)

    def process(k, m_sub):
        n = m_sub * sub_gran
        slot = lax.rem(k, 2)

        @pl.when(k + 1 < nb)
        def _():
            in_copy(k + 1, 1 - slot).start()

        in_copy(k, slot).wait()

        @pl.when(k >= 2)
        def _():
            out_copy(k - 2, slot, subs).wait()

        x = _unpack_slot_rows(_as_rows(xbuf_ref[slot, 0:n]))
        h1 = jnp.dot(x, w1b_ref[...], preferred_element_type=F32) + b1_ref[0]
        gate = jnp.minimum(h1[:, :D_FF], SWIGLU_LIMIT)
        up = jnp.clip(h1[:, D_FF:], -SWIGLU_LIMIT, SWIGLU_LIMIT)
        act = gate * jax.nn.sigmoid(SWIGLU_ALPHA * gate) * (up + 1.0)
        y = jnp.dot(act.astype(BF16), w2b_ref[...], preferred_element_type=F32) + b2_ref[0]
        ybuf_ref[slot, 0:n] = _as_granules(_pack_slot_rows(y))
        out_copy(k, slot, m_sub).start()

    def full_block(k, carry):
        process(k, subs)
        return carry

    lax.fori_loop(0, nb_full, full_block, 0)

    for m_sub in range(1, subs):
        @pl.when(rem == m_sub)
        def _(m_sub=m_sub):
            process(nb_full, m_sub)

    @pl.when(nb >= 2)
    def _():
        out_copy(nb - 2, 1 - last_slot, subs).wait()

    @pl.when((nb >= 1) & (rem == 0))
    def _():
        out_copy(nb - 1, last_slot, subs).wait()

    for m_sub in range(1, subs):
        @pl.when(rem == m_sub)
        def _(m_sub=m_sub):
            out_copy(nb - 1, last_slot, m_sub).wait()


def _combine_kernel(n_prompt_tiles, cnt_ref, loff_ref, goff_ref, tot_ref, big_ref,
                    routet_ref, x1_ref, gf_ref, ys_hbm, yp_ref, ysm_ref, stage_ref, sem):
    i = pl.program_id(0)
    n_steps = pl.num_programs(0)
    slot = lax.rem(i, 2)
    T = TOK_TILE

    def seg_copy(tile_slot):
        return lambda s, d, n: pltpu.make_async_copy(
            ys_hbm.at[pl.ds(d, n)], stage_ref.at[tile_slot, pl.ds(s, n)], sem.at[tile_slot])

    @pl.when(i == 0)
    def _():
        stage_ref[...] = jnp.zeros(stage_ref.shape, U32)
        _start_tile_copies(i, i >= 0, cnt_ref, loff_ref, goff_ref, big_ref, seg_copy(slot))

    rt = routet_ref[...]
    l_iota = lax.broadcasted_iota(I32, (T, LOCAL_ROWS), 1)
    pw = jnp.zeros((T, LOCAL_ROWS), F32)
    for k in range(TOP_K):
        pos_k = rt[:, TOP_K + k:TOP_K + k + 1].astype(I32)
        pw = jnp.where(l_iota == pos_k, rt[:, 2 * TOP_K + k:2 * TOP_K + k + 1], pw)
    pw = pw.astype(BF16)

    _start_tile_copies(jnp.minimum(i + 1, n_steps - 1), i + 1 < n_steps, cnt_ref, loff_ref, goff_ref, big_ref,
                       seg_copy(1 - slot))

    _wait_granules(tot_ref[i], TILE_BITS, lambda n: pltpu.make_async_copy(
        ys_hbm.at[pl.ds(0, n)], stage_ref.at[slot, pl.ds(0, n)], sem.at[slot]))
    moe = jnp.dot(pw, _unpack_slot_rows(_as_rows(stage_ref[slot])), preferred_element_type=F32)
    y = _rms_norm(x1_ref[...] + moe, gf_ref[...])

    @pl.when(i < n_prompt_tiles)
    def _():
        yp_ref[...] = y

    @pl.when(i >= n_prompt_tiles)
    def _():
        ysm_ref[...] = y


def _t5_bucket(rel):
    half = NUM_BUCKETS // 2
    max_exact = half // 2
    ret = jnp.where(rel > 0, half, 0)
    n = jnp.abs(rel)
    nf = jnp.maximum(n, 1).astype(F32)
    large = max_exact + (jnp.log(nf / max_exact) / math.log(MAX_DISTANCE / max_exact)
                         * (half - max_exact)).astype(jnp.int32)
    large = jnp.minimum(large, half - 1)
    return ret + jnp.where(n < max_exact, n, large)


def _stacked_bias(table, sinks, q_pos, k_pos):
    nq, nk = q_pos.shape[0], k_pos.shape[0]
    bucket = _t5_bucket(k_pos[None, :] - q_pos[:, None])
    onehot = (bucket[:, :, None] == jnp.arange(NUM_BUCKETS)).astype(F32)
    bias = jnp.einsum('qkb,bh->hqk', onehot, table.astype(F32), precision=lax.Precision.HIGHEST)
    bias = bias.reshape(N_KV, REP * nq, nk)
    sink = jnp.repeat(sinks.astype(F32).reshape(N_KV, REP, 1), nq, axis=2).reshape(N_KV, REP * nq, 1)
    tail = jnp.full((N_KV, REP * nq, KEY_PAD - nk - 1), -jnp.inf, F32)
    return jnp.concatenate([bias, sink, tail], axis=-1)


def kernel(x_prompt, x_sample, cache_win_k, cache_win_v, norm1_g, w_in, attn_sinks, rel_bias_table, sgu_ln_g, sgu_ln_b, sgu_w, sgu_b, w_o_att, w_o_sgu, w_out, norm2_g, w_router, b_router, w_exp_in, b_exp_in, w_exp_out, b_exp_out, final_norm_g):
    batch, seq, _ = x_prompt.shape
    dec_batch, dec_seq, _ = x_sample.shape
    cache_rows = cache_win_k.shape[2]
    assert x_prompt.shape[2] == D_MODEL and w_in.shape == (1, D_MODEL, D_IN)
    assert seq % LAYER_TILE == 0 and LAYER_TILE % TOK_TILE == 0 and TOK_TILE % GMLP_CHUNK == 0
    assert LAYER_TILE >= WINDOW and LAYER_TILE % dec_seq == 0 and (dec_batch * dec_seq) % LAYER_TILE == 0
    assert dec_seq <= GMLP_CHUNK and GMLP_CHUNK % dec_seq == 0 and cache_rows == WINDOW
    T = TOK_TILE
    n_prompt = batch * seq
    n_sample = dec_batch * dec_seq
    n_tok = n_prompt + n_sample
    LT = LAYER_TILE
    sorts_per_step = LT // T
    tiles_per_seq = seq // LT
    n_psteps = n_prompt // LT
    n_ssteps = n_sample // LT
    n_ptiles = n_prompt // T
    n_tiles = n_tok // T
    seqs_per_tile = LT // dec_seq

    w_in_b = w_in[0].astype(BF16)
    w_q_b = w_in_b[:, :D_ATT].reshape(D_MODEL, N_KV, REP, HEAD_DIM).transpose(0, 2, 1, 3).reshape(
        D_MODEL, D_ATT)
    woa_b = w_o_att[0].reshape(N_KV, REP, HEAD_DIM, D_MODEL).transpose(1, 0, 2, 3).reshape(
        D_ATT, D_MODEL).astype(BF16)
    wos_b = w_o_sgu[0].astype(BF16)
    wout_b = w_out[0].astype(BF16)
    wr_b = jnp.transpose(w_router[0]).astype(BF16)
    br_col = b_router[0].astype(F32).reshape(N_EXPERTS, 1)
    g1 = norm1_g[0].reshape(1, D_MODEL)
    g2 = norm2_g[0].reshape(1, D_MODEL)
    gf = final_norm_g.reshape(1, D_MODEL)
    lng = sgu_ln_g[0].reshape(1, D_GMLP)
    lnb = sgu_ln_b[0].reshape(1, D_GMLP)
    tril = jnp.tril(jnp.ones((GMLP_CHUNK, GMLP_CHUNK), dtype=bool))
    sguw_p = jnp.where(tril[None], sgu_w[0], 0).astype(BF16)
    sgub_p = jnp.broadcast_to(sgu_b[0][:, :, None], (N_GROUPS, GMLP_CHUNK, GROUP_W)).astype(F32)
    reps = GMLP_CHUNK // dec_seq
    corner = jnp.where(tril[None, :dec_seq, :dec_seq], sgu_w[0][:, :dec_seq, :dec_seq], 0)
    sguw_s = jnp.einsum('ab,gij->gaibj', jnp.eye(reps, dtype=F32), corner).reshape(
        N_GROUPS, GMLP_CHUNK, GMLP_CHUNK).astype(BF16)
    sgub_s = jnp.broadcast_to(jnp.tile(sgu_b[0][:, :dec_seq], (1, reps))[:, :, None],
                              (N_GROUPS, GMLP_CHUNK, GROUP_W)).astype(F32)
    bias_p = _stacked_bias(rel_bias_table, attn_sinks[0], jnp.arange(CHUNK) + WINDOW, jnp.arange(BAND))
    bias_s = _stacked_bias(rel_bias_table, attn_sinks[0], cache_rows + jnp.arange(dec_seq),
                           jnp.arange(cache_rows + dec_seq))
    tri = jnp.triu(jnp.ones((T, T), F32), k=1).astype(BF16)
    low = jnp.tril(jnp.ones((N_EXPERTS, N_EXPERTS), F32), k=-1).astype(BF16)

    layer_out_shapes = (
        jax.ShapeDtypeStruct((n_tok, D_MODEL), F32),
        jax.ShapeDtypeStruct((n_tok, D_MODEL), BF16),
        jax.ShapeDtypeStruct((n_tiles, ROUTE_ROWS, T), F32),
        jax.ShapeDtypeStruct((n_tok, LANES), F32),
        jax.ShapeDtypeStruct((n_tiles, N_EXPERTS, LANES), F32),
    )
    shared_consts = (w_q_b, w_in_b)
    tail_consts_p = (sguw_p, sgub_p, lng, lnb, woa_b, wos_b, wout_b, g2, wr_b, br_col, tri, low)
    tail_consts_s = (sguw_s, sgub_s, lng, lnb, woa_b, wos_b, wout_b, g2, wr_b, br_col, tri, low)

    def tile_specs(tile_of):
        return [
            pl.BlockSpec((LT, D_MODEL), lambda *g: (tile_of(*g), 0)),
            pl.BlockSpec((LT, D_MODEL), lambda *g: (tile_of(*g), 0)),
            pl.BlockSpec((sorts_per_step, ROUTE_ROWS, T), lambda *g: (tile_of(*g), 0, 0)),
            pl.BlockSpec((LT, LANES), lambda *g: (tile_of(*g), 0)),
            pl.BlockSpec((sorts_per_step, N_EXPERTS, LANES), lambda *g: (tile_of(*g), 0, 0)),
        ]

    prompt_consts = (g1,) + shared_consts + (bias_p,) + tail_consts_p
    x1, h, route, routet, cnt, kvwin = pl.pallas_call(
        _prompt_layer_kernel,
        grid=(batch, tiles_per_seq),
        in_specs=[pl.BlockSpec((1, LT, D_MODEL), lambda b, j: (b, j, 0))]
                 + [_const_spec(c.shape) for c in prompt_consts],
        out_specs=tile_specs(lambda b, j: b * tiles_per_seq + j)
                  + [pl.BlockSpec((1, WINDOW, 2 * D_KV), lambda b, j: (b, 0, 0))],
        out_shape=layer_out_shapes + (jax.ShapeDtypeStruct((batch, WINDOW, 2 * D_KV), F32),),
        scratch_shapes=[pltpu.VMEM((LT + WINDOW, 2 * D_KV), BF16),
                        pltpu.VMEM((LT, D_ATT), BF16),
                        pltpu.VMEM((LT, D_GMLP), BF16)],
        compiler_params=pltpu.CompilerParams(dimension_semantics=("arbitrary", "arbitrary"),
                                             vmem_limit_bytes=VMEM_LIMIT),
        name="layer_prompt",
    )(x_prompt, *prompt_consts)

    xs_flat = x_sample.reshape(n_sample, D_MODEL)
    ck = cache_win_k[0].reshape(dec_batch, cache_rows, D_KV)
    cv = cache_win_v[0].reshape(dec_batch, cache_rows, D_KV)
    sample_consts = (g1,) + shared_consts + (bias_s,) + tail_consts_s
    any_spec = pl.BlockSpec(memory_space=pl.ANY)
    x1, h, route, routet, cnt, kvnew, vn_s = pl.pallas_call(
        _sample_layer_kernel,
        grid=(n_ssteps,),
        in_specs=[any_spec] * 5
                 + [pl.BlockSpec((LT, D_MODEL), lambda i: (i, 0)),
                    pl.BlockSpec((seqs_per_tile, cache_rows, D_KV), lambda i: (i, 0, 0)),
                    pl.BlockSpec((seqs_per_tile, cache_rows, D_KV), lambda i: (i, 0, 0))]
                 + [_const_spec(c.shape) for c in sample_consts],
        out_specs=tile_specs(lambda i: n_psteps + i)
                  + [pl.BlockSpec((LT, 2 * D_KV), lambda i: (i, 0)),
                     pl.BlockSpec((LT, D_GMLP), lambda i: (i, 0))],
        out_shape=layer_out_shapes + (jax.ShapeDtypeStruct((n_sample, 2 * D_KV), F32),
                                      jax.ShapeDtypeStruct((n_sample, D_GMLP), F32)),
        scratch_shapes=[pltpu.VMEM((LT, D_ATT), BF16), pltpu.VMEM((LT, D_GMLP), BF16)],
        input_output_aliases={0: 0, 1: 1, 2: 2, 3: 3, 4: 4},
        compiler_params=pltpu.CompilerParams(dimension_semantics=("arbitrary",),
                                             vmem_limit_bytes=VMEM_LIMIT),
        name="layer_sample",
    )(x1, h, route, routet, cnt, xs_flat, ck, cv, *sample_consts)

    blk_gran = SUB_BLOCK // GRAN
    counts =cnt[:, :, 0].astype(I32)
    seg_gran = (counts + (GRAN - 1)) // GRAN
    local_off = jnp.cumsum(seg_gran, axis=1) - seg_gran
    tot_gran = jnp.sum(seg_gran, axis=0)
    ptot_gran = (tot_gran + (blk_gran - 1)) // blk_gran * blk_gran
    pend_gran = jnp.cumsum(ptot_gran)
    gstart = pend_gran - ptot_gran
    global_off = gstart[None, :] + jnp.cumsum(seg_gran, axis=0) - seg_gran
    pad_n = ptot_gran - tot_gran
    pad_off = gstart + tot_gran
    n_rows = -(-(TOP_K * n_tok + n_tiles * N_EXPERTS * (GRAN - 1) + N_EXPERTS * (SUB_BLOCK - GRAN))
               // SUB_BLOCK) * SUB_BLOCK + ROW_BLOCK
    region_gran0 = gstart.astype(I32)
    region_subs = (ptot_gran // blk_gran).astype(I32)
    tile_gran = jnp.sum(seg_gran, axis=1).astype(I32)
    tile_big = jnp.any(seg_gran >= 4 * (FLAT_QUADS + 1), axis=1).astype(I32)
    seg_gran_f = seg_gran.reshape(-1).astype(I32)
    local_off_f = local_off.reshape(-1).astype(I32)
    global_off_f = global_off.reshape(-1).astype(I32)

    xs = pl.pallas_call(
        _dispatch_kernel,
        grid_spec=pltpu.PrefetchScalarGridSpec(
            num_scalar_prefetch=7,
            grid=(n_tiles,),
            in_specs=[pl.BlockSpec((1, ROUTE_ROWS, T), lambda i, *_: (i, 0, 0)),
                      pl.BlockSpec((T, D_MODEL), lambda i, *_: (i, 0))],
            out_specs=pl.BlockSpec(memory_space=pl.ANY),
            scratch_shapes=[pltpu.VMEM((2, LOCAL_ROWS // GRAN, GRAN, SLOT_WORDS), U32),
                            pltpu.VMEM((SUB_BLOCK // GRAN, GRAN, SLOT_WORDS), U32),
                            pltpu.SemaphoreType.DMA((3,))]),
        out_shape=jax.ShapeDtypeStruct((n_rows // GRAN, GRAN, SLOT_WORDS), U32),
        compiler_params=pltpu.CompilerParams(dimension_semantics=("arbitrary",),
                                             vmem_limit_bytes=VMEM_LIMIT),
        name="moe_dispatch",
    )(seg_gran_f, local_off_f, global_off_f, tile_gran, tile_big, pad_n.astype(I32), pad_off.astype(I32),
      route, h)

    ys = pl.pallas_call(
        _expert_kernel,
        grid_spec=pltpu.PrefetchScalarGridSpec(
            num_scalar_prefetch=2,
            grid=(N_EXPERTS,),
            in_specs=[pl.BlockSpec(memory_space=pl.ANY),
                      pl.BlockSpec((1, D_MODEL, 2 * D_FF), lambda e, *_: (e, 0, 0)),
                      pl.BlockSpec((1, 1, 2 * D_FF), lambda e, *_: (e, 0, 0)),
                      pl.BlockSpec((1, D_FF, D_MODEL), lambda e, *_: (e, 0, 0)),
                      pl.BlockSpec((1, 1, D_MODEL), lambda e, *_: (e, 0, 0))],
            out_specs=pl.BlockSpec(memory_space=pl.ANY),
            scratch_shapes=[pltpu.VMEM((D_MODEL, 2 * D_FF), BF16), pltpu.VMEM((D_FF, D_MODEL), BF16),
                            pltpu.VMEM((2, ROW_BLOCK // GRAN, GRAN, SLOT_WORDS), U32),
                            pltpu.VMEM((2, ROW_BLOCK // GRAN, GRAN, SLOT_WORDS), U32),
                            pltpu.SemaphoreType.DMA((2,)), pltpu.SemaphoreType.DMA((2,))]),
        out_shape=jax.ShapeDtypeStruct((n_rows // GRAN, GRAN, SLOT_WORDS), U32),
        compiler_params=pltpu.CompilerParams(dimension_semantics=("arbitrary",),
                                             vmem_limit_bytes=VMEM_LIMIT),
        name="moe_experts",
    )(region_gran0, region_subs, xs, w_exp_in[0], b_exp_in[0].reshape(N_EXPERTS, 1, 2 * D_FF),
      w_exp_out[0], b_exp_out[0].reshape(N_EXPERTS, 1, D_MODEL))

    y_p, y_s = pl.pallas_call(
        functools.partial(_combine_kernel, n_ptiles),
        grid_spec=pltpu.PrefetchScalarGridSpec(
            num_scalar_prefetch=5,
            grid=(n_tiles,),
            in_specs=[pl.BlockSpec((T, LANES), lambda i, *_: (i, 0)),
                      pl.BlockSpec((T, D_MODEL), lambda i, *_: (i, 0)),
                      pl.BlockSpec((1, D_MODEL), lambda i, *_: (0, 0)),
                      pl.BlockSpec(memory_space=pl.ANY)],
            out_specs=[pl.BlockSpec((T, D_MODEL), lambda i, *_: (jnp.minimum(i, n_ptiles - 1), 0)),
                       pl.BlockSpec((T, D_MODEL), lambda i, *_: (jnp.maximum(i - n_ptiles, 0), 0))],
            scratch_shapes=[pltpu.VMEM((2, LOCAL_ROWS // GRAN, GRAN, SLOT_WORDS), U32), pltpu.SemaphoreType.DMA((2,))]),
        out_shape=(jax.ShapeDtypeStruct((n_prompt, D_MODEL), F32),
                   jax.ShapeDtypeStruct((n_sample, D_MODEL), F32)),
        compiler_params=pltpu.CompilerParams(dimension_semantics=("arbitrary",),
                                             vmem_limit_bytes=VMEM_LIMIT),
        name="moe_combine",
    )(seg_gran_f, local_off_f, global_off_f, tile_gran, tile_big, routet, x1, gf, ys)

    y_prompt = y_p.reshape(batch, seq, D_MODEL)
    y_sample = y_s.reshape(dec_batch, dec_seq, D_MODEL)
    new_win_k_prompt = kvwin[:, :, :D_KV].reshape(1, batch, WINDOW, N_KV, HEAD_DIM)
    new_win_v_prompt = kvwin[:, :, D_KV:].reshape(1, batch, WINDOW, N_KV, HEAD_DIM)
    new_win_k_sample = kvnew[:, :D_KV].reshape(1, dec_batch, dec_seq, N_KV, HEAD_DIM)
    new_win_v_sample = kvnew[:, D_KV:].reshape(1, dec_batch, dec_seq, N_KV, HEAD_DIM)
    new_sgu_v_sample = vn_s.reshape(1, dec_batch, dec_seq, D_GMLP)
    return (y_prompt, y_sample, new_win_k_prompt, new_win_v_prompt, new_win_k_sample,
            new_win_v_sample, new_sgu_v_sample)
```
